```python
import jax, jax.numpy as jnp
from jax import lax
import numpy as np

D_MODEL = 1024
BATCH = 2
SEQ = 8192
DEPTH = 1

GLA_HEADS = 4
GLA_VAL_WIDTH = D_MODEL // 2
GLA_DV = GLA_VAL_WIDTH // GLA_HEADS
GLA_DK = GLA_DV // 2
GLA_KEY_WIDTH = GLA_HEADS * GLA_DK
GLA_GATE_RANK = 16
GLA_TAU = 16.0
GLA_CHUNK = 64
ATT_WIDTH = D_MODEL - GLA_VAL_WIDTH
ATT_HEAD_DIM = 64
ATT_HEADS = ATT_WIDTH // ATT_HEAD_DIM
ROT_DIM = ATT_HEAD_DIM // 4
ROPE_THETA = 500000.0
DILATED_PATTERNS = ((128, 1), (512, 4), (2048, 16))
MIX_WIDTH = GLA_VAL_WIDTH + ATT_WIDTH
IN_WIDTH = 2 * GLA_KEY_WIDTH + 2 * GLA_VAL_WIDTH + 2 * GLA_GATE_RANK + 3 * ATT_WIDTH
MOE_GROUPS = 4
MOE_EXPERTS_PER_GROUP = 8
MOE_N_EXPERTS = MOE_GROUPS * MOE_EXPERTS_PER_GROUP
MOE_TOP_K = 2
MOE_D_FF = D_MODEL // 2
MOE_BLOCK = 128
EPS = 1e-6
NEG_INF = -1e30

kernel_name = 'hybrid_gla_dilated_attn_hier_moe'


def _rmsnorm(x, w):
    xf = x.astype(jnp.float32)
    y = xf * lax.rsqrt(jnp.mean(xf * xf, axis=-1, keepdims=True) + EPS)
    return (y * w.astype(jnp.float32)).astype(x.dtype)


def _gla_one_direction(q, k, v, log_a):
    B, H, S, dk = q.shape
    dv = v.shape[-1]
    C = GLA_CHUNK
    n = S // C
    q = q.reshape(B, H, n, C, dk)
    k = k.reshape(B, H, n, C, dk)
    v = v.reshape(B, H, n, C, dv)
    b = jnp.cumsum(log_a.reshape(B, H, n, C, dk), axis=-2)
    b_last = b[..., -1:, :]
    q_dec = q * jnp.exp(b)
    k_inv = k * jnp.exp(-b)
    k_end = k * jnp.exp(b_last - b)
    lower = jnp.tril(jnp.ones((C, C), dtype=bool))
    attn = jnp.where(lower, jnp.einsum('bhncd,bhnsd->bhncs', q_dec, k_inv), 0.0)
    o_intra = jnp.einsum('bhncs,bhnsv->bhncv', attn, v)
    chunk_kv = jnp.einsum('bhncd,bhncv->bhndv', k_end, v)
    chunk_decay = jnp.exp(b_last[..., 0, :])

    def step(state, inp):
        kv_n, dec_n = inp
        return state * dec_n[..., None] + kv_n, state

    init = jnp.zeros((B, H, dk, dv), jnp.float32)
    _, states = lax.scan(step, init, (jnp.moveaxis(chunk_kv, 2, 0), jnp.moveaxis(chunk_decay, 2, 0)))
    states = jnp.moveaxis(states, 0, 2)
    o_inter = jnp.einsum('bhncd,bhndv->bhncv', q_dec, states)
    return (o_intra + o_inter).reshape(B, H, S, dv)


def _gla_mixer(q, k, v, g_out, lr_f, lr_b, wf, bf, wb, bb, norm_w):
    B, S, _ = q.shape
    f32 = jnp.float32

    def heads(t, d):
        return t.reshape(B, S, GLA_HEADS, d).transpose(0, 2, 1, 3).astype(f32)

    qh = heads(q, GLA_DK) * (GLA_DK ** -0.5)
    kh = heads(k, GLA_DK)
    vh = heads(v, GLA_DV)
    log_a_f = heads(jax.nn.log_sigmoid(lr_f.astype(f32) @ wf.astype(f32) + bf.astype(f32)), GLA_DK) / GLA_TAU
    log_a_b = heads(jax.nn.log_sigmoid(lr_b.astype(f32) @ wb.astype(f32) + bb.astype(f32)), GLA_DK) / GLA_TAU

    def flip(t):
        return jnp.flip(t, axis=2)

    o = _gla_one_direction(qh, kh, vh, log_a_f) + flip(
        _gla_one_direction(flip(qh), flip(kh), flip(vh), flip(log_a_b)))
    o = o.transpose(0, 2, 1, 3)
    gate = g_out.reshape(B, S, GLA_HEADS, GLA_DV).astype(f32)
    o = _rmsnorm(o, norm_w) * jax.nn.silu(gate)
    return o.reshape(B, S, GLA_VAL_WIDTH).astype(q.dtype)


def _rope_tables(S):
    inv = ROPE_THETA ** (-(jnp.arange(0, ROT_DIM, 2, dtype=jnp.float32) / ROT_DIM))
    ang = jnp.arange(S, dtype=jnp.float32)[:, None] * inv[None, :]
    return jnp.cos(ang), jnp.sin(ang)


def _apply_partial_rope(t, cos, sin):
    half = ROT_DIM // 2
    t1 = t[..., :half]
    t2 = t[..., half:ROT_DIM]
    c = cos[None, :, None, :]
    s = sin[None, :, None, :]
    return jnp.concatenate([t1 * c - t2 * s, t2 * c + t1 * s, t[..., ROT_DIM:]], axis=-1)


def _banded_attention(q, k, v, radius):
    N, L, hd = q.shape
    R = radius
    nb = -(-L // R)
    Lp = nb * R
    q_p = jnp.pad(q, ((0, 0), (0, Lp - L), (0, 0))).reshape(N, nb, R, hd)
    k_p = jnp.pad(k, ((0, 0), (R, Lp - L + R), (0, 0))).reshape(N, nb + 2, R, hd)
    v_p = jnp.pad(v, ((0, 0), (R, Lp - L + R), (0, 0))).reshape(N, nb + 2, R, hd)
    k_win = jnp.concatenate([k_p[:, :-2], k_p[:, 1:-1], k_p[:, 2:]], axis=2)
    v_win = jnp.concatenate([v_p[:, :-2], v_p[:, 1:-1], v_p[:, 2:]], axis=2)
    s = jnp.einsum('nbqd,nbkd->nbqk', q_p, k_win) * (hd ** -0.5)
    qpos = (jnp.arange(nb)[:, None] * R + jnp.arange(R)[None, :])[:, :, None]
    kpos = (jnp.arange(nb)[:, None] * R - R + jnp.arange(3 * R)[None, :])[:, None, :]
    valid = (jnp.abs(qpos - kpos) <= R) & (kpos >= 0) & (kpos < L)
    s = jnp.where(valid, s, NEG_INF)
    m = jnp.max(s, axis=-1, keepdims=True)
    p = jnp.exp(s - m)
    z = jnp.sum(p, axis=-1, keepdims=True)
    o = jnp.einsum('nbqk,nbkd->nbqd', p / z, v_win)
    lse = (m + jnp.log(z))[..., 0]
    return o.reshape(N, Lp, hd)[:, :L], lse.reshape(N, Lp)[:, :L]


def _dilated_attention(q, k, v, window, dilation):
    B, S, H, hd = q.shape
    L = S // dilation

    def to_classes(t):
        return t.reshape(B, L, dilation, H, hd).transpose(0, 2, 3, 1, 4).reshape(B * dilation * H, L, hd)

    o, lse = _banded_attention(to_classes(q), to_classes(k), to_classes(v), window // (2 * dilation))
    o = o.reshape(B, dilation, H, L, hd).transpose(0, 3, 1, 2, 4).reshape(B, S, H, hd)
    lse = lse.reshape(B, dilation, H, L).transpose(0, 3, 1, 2).reshape(B, S, H)
    return o, lse


def _dilated_mixer(q, k, v):
    B, S, _ = q.shape
    cos, sin = _rope_tables(S)

    def heads(t):
        return t.reshape(B, S, ATT_HEADS, ATT_HEAD_DIM).astype(jnp.float32)

    qh = _apply_partial_rope(heads(q), cos, sin)
    kh = _apply_partial_rope(heads(k), cos, sin)
    vh = heads(v)
    outs, lses = [], []
    for window, dilation in DILATED_PATTERNS:
        o, lse = _dilated_attention(qh, kh, vh, window, dilation)
        outs.append(o)
        lses.append(lse)
    w = jax.nn.softmax(jnp.stack(lses, axis=0), axis=0)
    o = jnp.einsum('pbsh,pbshd->bshd', w, jnp.stack(outs, axis=0))
    return o.reshape(B, S, ATT_WIDTH).astype(q.dtype)


def _hier_moe(h, wg, bg, we, be, w_gate, w_up, w_down):
    T, D = h.shape
    f32 = jnp.float32
    hf = h.astype(f32)
    g_logits = hf @ wg.astype(f32) + bg.astype(f32)
    g_prob = jax.nn.softmax(g_logits, axis=-1)
    g_sel = jnp.argmax(g_logits, axis=-1)
    rows = jnp.arange(T)
    g_w = g_prob[rows, g_sel]
    e_logits = jnp.einsum('td,gde->tge', hf, we.astype(f32)) + be.astype(f32)
    e_sel_logits = e_logits[rows, g_sel]
    top_v, top_i = lax.top_k(e_sel_logits, MOE_TOP_K)
    weights = g_w[:, None] * jax.nn.softmax(top_v, axis=-1)
    expert = g_sel[:, None] * MOE_EXPERTS_PER_GROUP + top_i

    A = T * MOE_TOP_K
    e_flat = expert.reshape(-1)
    tok_flat = jnp.repeat(rows, MOE_TOP_K)
    w_flat = weights.reshape(-1)
    order = jnp.argsort(e_flat)
    e_s, tok_s, w_s = e_flat[order], tok_flat[order], w_flat[order]
    counts = jnp.zeros((MOE_N_EXPERTS,), jnp.int32).at[e_flat].add(1)
    padded = ((counts + MOE_BLOCK - 1) // MOE_BLOCK) * MOE_BLOCK
    start = jnp.cumsum(counts) - counts
    pend = jnp.cumsum(padded)
    pstart = pend - padded
    dest = pstart[e_s] + (jnp.arange(A) - start[e_s])
    n_blocks = -(-A // MOE_BLOCK) + MOE_N_EXPERTS
    cap = n_blocks * MOE_BLOCK
    buf_tok = jnp.full((cap,), T, jnp.int32).at[dest].set(tok_s.astype(jnp.int32))
    h_pad = jnp.concatenate([h, jnp.zeros((1, D), h.dtype)], axis=0)
    xb = h_pad[buf_tok].reshape(n_blocks, MOE_BLOCK, D)
    block_expert = jnp.clip(jnp.searchsorted(pend, jnp.arange(n_blocks) * MOE_BLOCK, side='right'),
                            0, MOE_N_EXPERTS - 1)

    def expert_block(args):
        xblk, eid = args
        return (jax.nn.silu(xblk @ w_gate[eid]) * (xblk @ w_up[eid])) @ w_down[eid]

    yb = lax.map(expert_block, (xb, block_expert)).reshape(cap, D)
    y_s = yb[dest] * w_s[:, None].astype(h.dtype)
    return jnp.zeros((T, D), h.dtype).at[tok_s].add(y_s)


def setup_inputs(seed: int = 0) -> dict:
    key = jax.random.key(seed)
    ks = jax.random.split(key, 18)

    def nrm(k, shape, scale):
        return jax.random.normal(k, shape, jnp.float32) * scale

    return {
        'x': nrm(ks[0], (BATCH, SEQ, D_MODEL), 1.0),
        'norm1_w': 1.0 + nrm(ks[1], (DEPTH, D_MODEL), 0.02),
        'w_in': nrm(ks[2], (DEPTH, D_MODEL, IN_WIDTH), D_MODEL ** -0.5),
        'gla_fwd_gate_w': nrm(ks[3], (DEPTH, GLA_GATE_RANK, GLA_KEY_WIDTH), GLA_GATE_RANK ** -0.5),
        'gla_fwd_gate_b': nrm(ks[4], (DEPTH, GLA_KEY_WIDTH), 0.1),
        'gla_bwd_gate_w': nrm(ks[5], (DEPTH, GLA_GATE_RANK, GLA_KEY_WIDTH), GLA_GATE_RANK ** -0.5),
        'gla_bwd_gate_b': nrm(ks[6], (DEPTH, GLA_KEY_WIDTH), 0.1),
        'gla_norm_w': 1.0 + nrm(ks[7], (DEPTH, GLA_DV), 0.02),
        'w_out': nrm(ks[8], (DEPTH, MIX_WIDTH, D_MODEL), MIX_WIDTH ** -0.5),
        'norm2_w': 1.0 + nrm(ks[9], (DEPTH, D_MODEL), 0.02),
        'router_group_w': nrm(ks[10], (DEPTH, D_MODEL, MOE_GROUPS), D_MODEL ** -0.5),
        'router_group_b': nrm(ks[11], (DEPTH, MOE_GROUPS), 0.01),
        'router_expert_w': nrm(ks[12], (DEPTH, MOE_GROUPS, D_MODEL, MOE_EXPERTS_PER_GROUP), D_MODEL ** -0.5),
        'router_expert_b': nrm(ks[13], (DEPTH, MOE_GROUPS, MOE_EXPERTS_PER_GROUP), 0.01),
        'expert_w_gate': nrm(ks[14], (DEPTH, MOE_N_EXPERTS, D_MODEL, MOE_D_FF), D_MODEL ** -0.5),
        'expert_w_up': nrm(ks[15], (DEPTH, MOE_N_EXPERTS, D_MODEL, MOE_D_FF), D_MODEL ** -0.5),
        'expert_w_down': nrm(ks[16], (DEPTH, MOE_N_EXPERTS, MOE_D_FF, D_MODEL), MOE_D_FF ** -0.5),
        'final_norm_w': 1.0 + nrm(ks[17], (D_MODEL,), 0.02),
    }


def reference(x, norm1_w, w_in, gla_fwd_gate_w, gla_fwd_gate_b, gla_bwd_gate_w, gla_bwd_gate_b,
              gla_norm_w, w_out, norm2_w, router_group_w, router_group_b, router_expert_w,
              router_expert_b, expert_w_gate, expert_w_up, expert_w_down, final_norm_w):
    B, S, D = x.shape
    sizes = [GLA_KEY_WIDTH, GLA_KEY_WIDTH, GLA_VAL_WIDTH, GLA_VAL_WIDTH,
             GLA_GATE_RANK, GLA_GATE_RANK, ATT_WIDTH, ATT_WIDTH]
    split_at = [int(c) for c in np.cumsum(sizes)]
    h = x
    for l in range(DEPTH):
        u = _rmsnorm(h, norm1_w[l])
        proj = u @ w_in[l]
        gq, gk, gv, gg, glf, glb, aq, ak, av = jnp.split(proj, split_at, axis=-1)
        gla_out = _gla_mixer(gq, gk, gv, gg, glf, glb, gla_fwd_gate_w[l], gla_fwd_gate_b[l],
                             gla_bwd_gate_w[l], gla_bwd_gate_b[l], gla_norm_w[l])
        att_out = _dilated_mixer(aq, ak, av)
        mixed = jnp.concatenate([gla_out, att_out], axis=-1)
        h = h + mixed @ w_out[l]
        u = _rmsnorm(h, norm2_w[l])
        moe = _hier_moe(u.reshape(B * S, D), router_group_w[l], router_group_b[l],
                        router_expert_w[l], router_expert_b[l], expert_w_gate[l],
                        expert_w_up[l], expert_w_down[l])
        h = h + moe.reshape(B, S, D)
    return _rmsnorm(h, final_norm_w)
```

```python
import functools

import jax
import jax.numpy as jnp
from jax import lax
from jax.experimental import pallas as pl
from jax.experimental.pallas import tpu as pltpu

F32 = jnp.float32
BF16 = jnp.bfloat16

D_MODEL = 1024
GLA_HEADS = 4
GLA_DV = 128
GLA_DK = 64
GLA_KEY_WIDTH = GLA_HEADS * GLA_DK
GLA_VAL_WIDTH = GLA_HEADS * GLA_DV
GLA_GATE_RANK = 16
GLA_TAU = 16.0
GLA_CHUNK = 64
ATT_WIDTH = 512
ATT_HEAD_DIM = 64
ATT_HEADS = 8
ROT_DIM = 16
ROPE_THETA = 500000.0
DILATED_PATTERNS = ((128, 1), (512, 4), (2048, 16))
ATT_RADIUS = 64
MOE_GROUPS = 4
MOE_EXPERTS_PER_GROUP = 8
MOE_N_EXPERTS = 32
MOE_TOP_K = 2
MOE_D_FF = 512
EPS = 1e-6
NEG_INF = -1e30

LANES = 128
MOE_ROWS = 256
VMEM_LIMIT = 56 * 1024 * 1024


def _cparams(sem):
    return pltpu.CompilerParams(dimension_semantics=sem, vmem_limit_bytes=VMEM_LIMIT)


def _dot(a, b):
    return jnp.dot(a, b, preferred_element_type=F32)


def _dot_nt(a, b):
    return lax.dot_general(a, b, (((1,), (1,)), ((), ())), preferred_element_type=F32)


def _dot_tn(a, b):
    return lax.dot_general(a, b, (((0,), (0,)), ((), ())), preferred_element_type=F32)


def _rms(x, w):
    return x * lax.rsqrt(jnp.mean(x * x, axis=-1, keepdims=True) + EPS) * w


def _inproj_kernel(x_ref, n1_ref, wg_ref, wlr_ref, wa_ref, gw_ref, gb_ref,
                   rc_ref, rs1_ref, rs2_ref, gla_ref, loga_ref, att_ref):
    x = x_ref[...]
    ub = _rms(x, n1_ref[...]).astype(BF16)
    g = _dot(ub, wg_ref[...])
    gla_ref[:, :GLA_KEY_WIDTH] = g[:, :GLA_KEY_WIDTH] * (GLA_DK ** -0.5)
    gla_ref[:, GLA_KEY_WIDTH:] = g[:, GLA_KEY_WIDTH:]
    lr = _dot(ub, wlr_ref[...])
    gate = _dot(lr.astype(BF16), gw_ref[...]) + gb_ref[...]
    loga_ref[...] = (jnp.minimum(gate, 0.0) - jnp.log(1.0 + jnp.exp(-jnp.abs(gate)))) * (1.0 / GLA_TAU)
    a = _dot(ub, wa_ref[...])
    qk = a[:, :2 * ATT_WIDTH]
    reps = 2 * ATT_WIDTH // LANES
    c = jnp.concatenate([rc_ref[...]] * reps, axis=1)
    s1 = jnp.concatenate([rs1_ref[...]] * reps, axis=1)
    s2 = jnp.concatenate([rs2_ref[...]] * reps, axis=1)
    half = ROT_DIM // 2
    n = 2 * ATT_WIDTH
    roped = qk * c + pltpu.roll(qk, n - half, 1) * s1 + pltpu.roll(qk, half, 1) * s2
    att_ref[:, :ATT_WIDTH] = roped[:, :ATT_WIDTH] * (ATT_HEAD_DIM ** -0.5)
    att_ref[:, ATT_WIDTH:2 * ATT_WIDTH] = roped[:, ATT_WIDTH:]
    att_ref[:, 2 * ATT_WIDTH:] = a[:, 2 * ATT_WIDTH:]


def _rope_lane_tables(S):
    half = ROT_DIM // 2
    inv = ROPE_THETA ** (-(jnp.arange(0, ROT_DIM, 2, dtype=F32) / ROT_DIM))
    ang = jnp.arange(S, dtype=F32)[:, None] * inv[None, :]
    cos, sin = jnp.cos(ang), jnp.sin(ang)
    ones = jnp.ones((S, ATT_HEAD_DIM - ROT_DIM), F32)
    zeros8 = jnp.zeros((S, half), F32)
    zrest = jnp.zeros((S, ATT_HEAD_DIM - ROT_DIM), F32)
    c = jnp.concatenate([cos, cos, ones], axis=1)
    s1 = jnp.concatenate([-sin, zeros8, zrest], axis=1)
    s2 = jnp.concatenate([zeros8, sin, zrest], axis=1)
    rep = LANES // ATT_HEAD_DIM
    return (jnp.tile(c, (1, rep)), jnp.tile(s1, (1, rep)), jnp.tile(s2, (1, rep)))


def _inproj(x2, S, norm1_w, w_in, wf, bfw, wb, bbw, tm=512):
    T = x2.shape[0]
    o_lr = 2 * GLA_KEY_WIDTH + 2 * GLA_VAL_WIDTH
    o_att = o_lr + 2 * GLA_GATE_RANK
    wg = w_in[:, :o_lr].astype(BF16)
    wlr = jnp.zeros((D_MODEL, LANES), F32).at[:, :2 * GLA_GATE_RANK].set(w_in[:, o_lr:o_att]).astype(BF16)
    wa = w_in[:, o_att:].astype(BF16)
    gw = jnp.zeros((LANES, 2 * GLA_KEY_WIDTH), F32)
    gw = gw.at[:GLA_GATE_RANK, :GLA_KEY_WIDTH].set(wf)
    gw = gw.at[GLA_GATE_RANK:2 * GLA_GATE_RANK, GLA_KEY_WIDTH:].set(wb).astype(BF16)
    gb = jnp.concatenate([bfw, bbw])[None, :]
    rc, rs1, rs2 = _rope_lane_tables(S)
    nS = S // tm
    row = lambda i: (i, 0)
    const = lambda i: (0, 0)
    pos = lambda i: (i % nS, 0)
    return pl.pallas_call(
        _inproj_kernel,
        grid=(T // tm,),
        in_specs=[
            pl.BlockSpec((tm, D_MODEL), row),
            pl.BlockSpec((1, D_MODEL), const),
            pl.BlockSpec((D_MODEL, o_lr), const),
            pl.BlockSpec((D_MODEL, LANES), const),
            pl.BlockSpec((D_MODEL, 3 * ATT_WIDTH), const),
            pl.BlockSpec((LANES, 2 * GLA_KEY_WIDTH), const),
            pl.BlockSpec((1, 2 * GLA_KEY_WIDTH), const),
            pl.BlockSpec((tm, LANES), pos),
            pl.BlockSpec((tm, LANES), pos),
            pl.BlockSpec((tm, LANES), pos),
        ],
        out_specs=[
            pl.BlockSpec((tm, o_lr), row),
            pl.BlockSpec((tm, 2 * GLA_KEY_WIDTH), row),
            pl.BlockSpec((tm, 3 * ATT_WIDTH), row),
        ],
        out_shape=[
            jax.ShapeDtypeStruct((T, o_lr), F32),
            jax.ShapeDtypeStruct((T, 2 * GLA_KEY_WIDTH), F32),
            jax.ShapeDtypeStruct((T, 3 * ATT_WIDTH), F32),
        ],
        compiler_params=_cparams(("arbitrary",)),
        name="inproj",
    )(x2, norm1_w[None, :], wg, wlr, wa, gw, gb, rc, rs1, rs2)


def _gla_direction(q, k, v, la, s_ref, o_ref, forward, G):
    C = GLA_CHUNK
    R = G * C
    r = lax.broadcasted_iota(jnp.int32, (R, R), 0)
    c = lax.broadcasted_iota(jnp.int32, (R, R), 1)
    same = (r >> 6) == (c >> 6)
    tri = (c <= r) if forward else (c >= r)
    t_mat = jnp.where(same, jnp.where(tri, 1.0, 0.0), 0.0)
    e_mat = jnp.where(same, 1.0, 0.0)
    te = jnp.concatenate([t_mat, e_mat], axis=0).astype(BF16)
    hi = la.astype(BF16)
    lo = (la - hi.astype(F32)).astype(BF16)
    bt = _dot(te, hi) + _dot(te, lo)
    b = bt[:R]
    tot = bt[R:]
    q_dec = (q * jnp.exp(b)).astype(BF16)
    k_inv = k * jnp.exp(-b)
    k_end = (k * jnp.exp(tot - b)).astype(BF16)
    dec = jnp.exp(tot)

    lane_k = lax.broadcasted_iota(jnp.int32, (C, GLA_KEY_WIDTH), 1)
    lane_v = lax.broadcasted_iota(jnp.int32, (C, GLA_VAL_WIDTH), 1)
    row_c = lax.broadcasted_iota(jnp.int32, (C, GLA_KEY_WIDTH), 0)
    col_in = lane_k & (C - 1)
    a_mask = (col_in <= row_c) if forward else (col_in >= row_c)
    srow = lax.broadcasted_iota(jnp.int32, (GLA_VAL_WIDTH, GLA_KEY_WIDTH), 0)
    scol = lax.broadcasted_iota(jnp.int32, (GLA_VAL_WIDTH, GLA_KEY_WIDTH), 1)
    s_mask = (srow >> 7) == (scol >> 6)

    order = range(G) if forward else range(G - 1, -1, -1)
    for g in order:
        rows = slice(g * C, (g + 1) * C)
        qd, ki, ke, vv = q_dec[rows], k_inv[rows], k_end[rows], v[rows]
        km = jnp.concatenate([jnp.where((lane_k >> 6) == h, ki, 0.0) for h in range(GLA_HEADS)], axis=0)
        a = _dot_nt(qd, km.astype(BF16))
        a = jnp.where(a_mask, a, 0.0).astype(BF16)
        vbd = jnp.concatenate([jnp.where((lane_v >> 7) == h, vv, 0.0) for h in range(GLA_HEADS)], axis=0)
        st = s_ref[...]
        o = _dot(a, vbd.astype(BF16)) + _dot_nt(qd, st.astype(BF16))
        o_ref[rows, :] = o
        kv = _dot_tn(vv.astype(BF16), ke)
        s_ref[...] = st * dec[g * C:g * C + 1, :] + jnp.where(s_mask, kv, 0.0)


def _gla_kernel(qf_ref, kf_ref, vf_ref, laf_ref, qb_ref, kb_ref, vb_ref, lab_ref,
                of_ref, ob_ref, sf_ref, sb_ref, *, G):
    @pl.when(pl.program_id(1) == 0)
    def _():
        sf_ref[...] = jnp.zeros_like(sf_ref)
        sb_ref[...] = jnp.zeros_like(sb_ref)

    _gla_direction(qf_ref[...], kf_ref[...], vf_ref[...], laf_ref[...], sf_ref, of_ref, True, G)
    _gla_direction(qb_ref[...], kb_ref[...], vb_ref[...], lab_ref[...], sb_ref, ob_ref, False, G)


def _gla(gla_slab, loga, B, S, G=4):
    T = B * S
    R = G * GLA_CHUNK
    ns = S // R
    fwd = lambda col: (lambda b, i: (b * ns + i, col))
    bwd = lambda col: (lambda b, i: (b * ns + ns - 1 - i, col))
    kw, vw = GLA_KEY_WIDTH, GLA_VAL_WIDTH
    return pl.pallas_call(
        functools.partial(_gla_kernel, G=G),
        grid=(B, ns),
        in_specs=[
            pl.BlockSpec((R, kw), fwd(0)), pl.BlockSpec((R, kw), fwd(1)),
            pl.BlockSpec((R, vw), fwd(1)), pl.BlockSpec((R, kw), fwd(0)),
            pl.BlockSpec((R, kw), bwd(0)), pl.BlockSpec((R, kw), bwd(1)),
            pl.BlockSpec((R, vw), bwd(1)), pl.BlockSpec((R, kw), bwd(1)),
        ],
        out_specs=[pl.BlockSpec((R, vw), fwd(0)), pl.BlockSpec((R, vw), bwd(0))],
        out_shape=[jax.ShapeDtypeStruct((T, vw), F32), jax.ShapeDtypeStruct((T, vw), F32)],
        scratch_shapes=[pltpu.VMEM((vw, kw), F32), pltpu.VMEM((vw, kw), F32)],
        compiler_params=_cparams(("arbitrary", "arbitrary")),
        name="gla",
    )(gla_slab, gla_slab, gla_slab, loga, gla_slab, gla_slab, gla_slab, loga)


ATT_QB = 128
ATT_KB = ATT_QB + 2 * ATT_RADIUS


def _att_kernel(q_ref, k_ref, v_ref, o_ref, m_ref, l_ref, *, S):
    QB, KB = ATT_QB, ATT_KB
    lane = lax.broadcasted_iota(jnp.int32, (QB, LANES), 1)
    head0 = lane < ATT_HEAD_DIM
    rowi = lax.broadcasted_iota(jnp.int32, (2 * QB, KB), 0) & (QB - 1)
    coli = lax.broadcasted_iota(jnp.int32, (2 * QB, KB), 1)
    rel = rowi - coli

    for pi, (_, d) in enumerate(DILATED_PATTERNS):
        L = S // d
        nb = L // QB
        shift = nb.bit_length() - 1
        first = pi == 0
        last = pi == len(DILATED_PATTERNS) - 1

        def body(n, carry, d=d, L=L, nb=nb, shift=shift, first=first, last=last):
            cls = n >> shift
            q0 = (n & (nb - 1)) * QB
            ws = jnp.clip(q0 - ATT_RADIUS, 0, L - KB)
            if d == 1:
                qsl = pl.ds(pl.multiple_of(q0, QB), QB)
                ksl = pl.ds(pl.multiple_of(ws, ATT_RADIUS), KB)
            else:
                qsl = pl.ds(cls + d * q0, QB, stride=d)
                ksl = pl.ds(cls + d * ws, KB, stride=d)
            q = q_ref[qsl, :]
            q2 = jnp.concatenate([jnp.where(head0, q, 0.0), jnp.where(head0, 0.0, q)], axis=0).astype(BF16)
            s = _dot_nt(q2, k_ref[ksl, :].astype(BF16))
            valid = jnp.abs(rel + (q0 - ws)) <= ATT_RADIUS
            s = jnp.where(valid, s, NEG_INF)
            m_blk = jnp.max(s, axis=-1, keepdims=True)
            p = jnp.exp(s - m_blk)
            l_blk = jnp.sum(p, axis=-1, keepdims=True)
            pv = _dot(p.astype(BF16), v_ref[ksl, :].astype(BF16))
            acc_b = jnp.where(head0, pv[:QB], pv[QB:])
            m_b = jnp.where(head0, m_blk[:QB], m_blk[QB:])
            l_b = jnp.where(head0, l_blk[:QB], l_blk[QB:])
            if first:
                acc, m_new, l_new = acc_b, m_b, l_b
            else:
                m_old = m_ref[qsl, :]
                m_new = jnp.maximum(m_old, m_b)
                w_old = jnp.exp(m_old - m_new)
                w_blk = jnp.exp(m_b - m_new)
                acc = o_ref[qsl, :] * w_old + acc_b * w_blk
                l_new = l_ref[qsl, :] * w_old + l_b * w_blk
            if last:
                o_ref[qsl, :] = acc / l_new
            else:
                o_ref[qsl, :] = acc
                m_ref[qsl, :] = m_new
                l_ref[qsl, :] = l_new
            return carry

        lax.fori_loop(0, S // QB, body, 0)


def _attention(att_slab, B, S):
    T = B * S
    ncol = ATT_WIDTH // LANES
    return pl.pallas_call(
        functools.partial(_att_kernel, S=S),
        grid=(B, ncol),
        in_specs=[
            pl.BlockSpec((S, LANES), lambda b, h: (b, h)),
            pl.BlockSpec((S, LANES), lambda b, h: (b, ncol + h)),
            pl.BlockSpec((S, LANES), lambda b, h: (b, 2 * ncol + h)),
        ],
        out_specs=pl.BlockSpec((S, LANES), lambda b, h: (b, h)),
        out_shape=jax.ShapeDtypeStruct((T, ATT_WIDTH), F32),
        scratch_shapes=[pltpu.VMEM((S, LANES), F32), pltpu.VMEM((S, LANES), F32)],
        compiler_params=_cparams(("arbitrary", "arbitrary")),
        name="dilated_attention",
    )(att_slab, att_slab, att_slab)


def _outproj_kernel(of_ref, ob_ref, gg_ref, att_ref, x_ref, gnw_ref, wo1_ref, wo2_ref,
                    n2_ref, wr_ref, br_ref, h_ref, u_ref, lg_ref):
    o = of_ref[...] + ob_ref[...]
    gate = gg_ref[...]
    gnw = gnw_ref[...]
    parts = []
    for h in range(GLA_HEADS):
        sl = slice(h * GLA_DV, (h + 1) * GLA_DV)
        parts.append(_rms(o[:, sl], gnw))
    y = jnp.concatenate(parts, axis=1) * (gate / (1.0 + jnp.exp(-gate)))
    mix = _dot(y.astype(BF16), wo1_ref[...]) + _dot(att_ref[...].astype(BF16), wo2_ref[...])
    h = x_ref[...] + mix
    h_ref[...] = h
    u = _rms(h, n2_ref[...])
    u_ref[...] = u
    lg_ref[...] = jnp.dot(u, wr_ref[...], preferred_element_type=F32,
                          precision=lax.Precision.HIGHEST) + br_ref[...]


def _outproj(o_f, o_b, gla_slab, att_out, x2, gla_norm_w, w_out, norm2_w, wr, br, tm=512):
    T = x2.shape[0]
    row = lambda i: (i, 0)
    const = lambda i: (0, 0)
    wo = w_out.astype(BF16)
    return pl.pallas_call(
        _outproj_kernel,
        grid=(T // tm,),
        in_specs=[
            pl.BlockSpec((tm, GLA_VAL_WIDTH), row),
            pl.BlockSpec((tm, GLA_VAL_WIDTH), row),
            pl.BlockSpec((tm, GLA_VAL_WIDTH), lambda i: (i, 2)),
            pl.BlockSpec((tm, ATT_WIDTH), row),
            pl.BlockSpec((tm, D_MODEL), row),
            pl.BlockSpec((1, GLA_DV), const),
            pl.BlockSpec((GLA_VAL_WIDTH, D_MODEL), const),
            pl.BlockSpec((ATT_WIDTH, D_MODEL), const),
            pl.BlockSpec((1, D_MODEL), const),
            pl.BlockSpec((D_MODEL, LANES), const),
            pl.BlockSpec((1, LANES), const),
        ],
        out_specs=[
            pl.BlockSpec((tm, D_MODEL), row),
            pl.BlockSpec((tm, D_MODEL), row),
            pl.BlockSpec((tm, LANES), row),
        ],
        out_shape=[
            jax.ShapeDtypeStruct((T, D_MODEL), F32),
            jax.ShapeDtypeStruct((T, D_MODEL), F32),
            jax.ShapeDtypeStruct((T, LANES), F32),
        ],
        compiler_params=_cparams(("arbitrary",)),
        name="outproj",
    )(o_f, o_b, gla_slab, att_out, x2, gla_norm_w[None, :], wo[:GLA_VAL_WIDTH], wo[GLA_VAL_WIDTH:],
      norm2_w[None, :], wr, br)


INFO_E1, INFO_E2, INFO_R1, INFO_R2, INFO_W1, INFO_W2 = range(6)


def _route_kernel(lg_ref, info_ref, cnt_ref, carry_ref):
    @pl.when(pl.program_id(0) == 0)
    def _():
        carry_ref[...] = jnp.zeros_like(carry_ref)

    lg = lg_ref[...]
    tr = lg.shape[0]
    lane = lax.broadcasted_iota(jnp.int32, (tr, LANES), 1)
    big = jnp.int32(1 << 20)
    is_g = (lane >= MOE_N_EXPERTS) & (lane < MOE_N_EXPERTS + MOE_GROUPS)
    gl = jnp.where(is_g, lg, -jnp.inf)
    gmax = jnp.max(gl, axis=-1, keepdims=True)
    gsel = jnp.min(jnp.where(gl == gmax, lane - MOE_N_EXPERTS, big), axis=-1, keepdims=True)
    g_w = 1.0 / jnp.sum(jnp.where(is_g, jnp.exp(lg - gmax), 0.0), axis=-1, keepdims=True)
    in_grp = (lane < MOE_N_EXPERTS) & ((lane >> 3) == gsel)
    el = jnp.where(in_grp, lg, -jnp.inf)
    v1 = jnp.max(el, axis=-1, keepdims=True)
    i1 = jnp.min(jnp.where(el == v1, lane, big), axis=-1, keepdims=True)
    el2 = jnp.where(lane == i1, -jnp.inf, el)
    v2 = jnp.max(el2, axis=-1, keepdims=True)
    i2 = jnp.min(jnp.where(el2 == v2, lane, big), axis=-1, keepdims=True)
    t = jnp.exp(v2 - v1)
    w1 = g_w * (1.0 / (1.0 + t))
    w2 = g_w * (t / (1.0 + t))

    hit1 = lane == i1
    hit2 = lane == i2
    member = jnp.where(hit1 | hit2, 1.0, 0.0)
    r = lax.broadcasted_iota(jnp.int32, (tr, tr), 0)
    c = lax.broadcasted_iota(jnp.int32, (tr, tr), 1)
    strict = jnp.where(c < r, 1.0, 0.0).astype(BF16)
    prefix = _dot(strict, member.astype(BF16)) + carry_ref[...]
    rank1 = jnp.sum(jnp.where(hit1, prefix, 0.0), axis=-1, keepdims=True)
    rank2 = jnp.sum(jnp.where(hit2, prefix, 0.0), axis=-1, keepdims=True)
    carry = carry_ref[...] + jnp.sum(member, axis=0, keepdims=True)
    carry_ref[...] = carry
    cnt_ref[...] = carry

    info = jnp.where(lane == INFO_E1, i1.astype(F32), 0.0)
    info = jnp.where(lane == INFO_E2, i2.astype(F32), info)
    info = jnp.where(lane == INFO_R1, rank1, info)
    info = jnp.where(lane == INFO_R2, rank2, info)
    info = jnp.where(lane == INFO_W1, w1, info)
    info = jnp.where(lane == INFO_W2, w2, info)
    info_ref[...] = info


def _route(logits, tr=512):
    T = logits.shape[0]
    return pl.pallas_call(
        _route_kernel,
        grid=(T // tr,),
        in_specs=[pl.BlockSpec((tr, LANES), lambda i: (i, 0))],
        out_specs=[pl.BlockSpec((tr, LANES), lambda i: (i, 0)), pl.BlockSpec((1, LANES), lambda i: (0, 0))],
        out_shape=[jax.ShapeDtypeStruct((T, LANES), F32), jax.ShapeDtypeStruct((1, LANES), F32)],
        scratch_shapes=[pltpu.VMEM((1, LANES), F32)],
        compiler_params=_cparams(("arbitrary",)),
        name="route",
    )(logits)


def _row_copy(src_ref, src_row, dst_ref, dst_row, sem):
    return pltpu.make_async_copy(src_ref.at[pl.ds(src_row, 1)], dst_ref.at[pl.ds(dst_row, 1)], sem)


def _dispatch_kernel(dest_ref, u_ref, xs_in_ref, xs_ref, sem, *, td):
    del xs_in_ref
    base = pl.program_id(0) * (td * MOE_TOP_K)

    def issue(r, carry):
        for k in range(MOE_TOP_K):
            _row_copy(u_ref, r, xs_ref, dest_ref[base + MOE_TOP_K * r + k], sem).start()
        return carry

    lax.fori_loop(0, td, issue, 0)

    def drain(r, carry):
        for k in range(MOE_TOP_K):
            _row_copy(u_ref, 0, xs_ref, 0, sem).wait()
        return carry

    lax.fori_loop(0, td, drain, 0)


def _dispatch(dest, u2, cap, td=256):
    T = u2.shape[0]
    xs0 = jnp.zeros((cap, D_MODEL), F32)
    return pl.pallas_call(
        functools.partial(_dispatch_kernel, td=td),
        grid_spec=pltpu.PrefetchScalarGridSpec(
            num_scalar_prefetch=1,
            grid=(T // td,),
            in_specs=[pl.BlockSpec((td, D_MODEL), lambda i, d: (i, 0)),
                      pl.BlockSpec(memory_space=pl.ANY)],
            out_specs=pl.BlockSpec(memory_space=pl.ANY),
            scratch_shapes=[pltpu.SemaphoreType.DMA(())],
        ),
        out_shape=jax.ShapeDtypeStruct((cap, D_MODEL), F32),
        input_output_aliases={2: 0},
        compiler_params=_cparams(("arbitrary",)),
        name="dispatch",
    )(dest, u2, xs0)


def _expert_kernel(be_ref, nu_ref, x_ref, wg_ref, wu_ref, wd_ref, y_ref, wgb, wub, wdb):
    b = pl.program_id(0)
    prev = be_ref[jnp.maximum(b - 1, 0)]
    fresh = (b == 0) | (be_ref[b] != prev)

    @pl.when(fresh)
    def _():
        wgb[...] = wg_ref[...].astype(BF16)
        wub[...] = wu_ref[...].astype(BF16)
        wdb[...] = wd_ref[...].astype(BF16)

    @pl.when(b < nu_ref[0])
    def _():
        xb = x_ref[...].astype(BF16)
        g = _dot(xb, wgb[...])
        u = _dot(xb, wub[...])
        hid = (g / (1.0 + jnp.exp(-g))) * u
        y_ref[...] = _dot(hid.astype(BF16), wdb[...])

    @pl.when(b >= nu_ref[0])
    def _():
        y_ref[...] = jnp.zeros_like(y_ref)


def _experts(block_expert, n_used, xs, w_gate, w_up, w_down):
    cap = xs.shape[0]
    nblk = cap // MOE_ROWS
    rows = lambda b, be, nu: (jnp.minimum(b, nu[0] - 1), 0)
    wsel = lambda b, be, nu: (be[b], 0, 0)
    return pl.pallas_call(
        _expert_kernel,
        grid_spec=pltpu.PrefetchScalarGridSpec(
            num_scalar_prefetch=2,
            grid=(nblk,),
            in_specs=[
                pl.BlockSpec((MOE_ROWS, D_MODEL), rows),
                pl.BlockSpec((None, D_MODEL, MOE_D_FF), wsel),
                pl.BlockSpec((None, D_MODEL, MOE_D_FF), wsel),
                pl.BlockSpec((None, MOE_D_FF, D_MODEL), wsel),
            ],
            out_specs=pl.BlockSpec((MOE_ROWS, D_MODEL), lambda b, be, nu: (b, 0)),
            scratch_shapes=[pltpu.VMEM((D_MODEL, MOE_D_FF), BF16),
                            pltpu.VMEM((D_MODEL, MOE_D_FF), BF16),
                            pltpu.VMEM((MOE_D_FF, D_MODEL), BF16)],
        ),
        out_shape=jax.ShapeDtypeStruct((cap, D_MODEL), F32),
        compiler_params=_cparams(("arbitrary",)),
        name="experts",
    )(block_expert, n_used, xs, w_gate, w_up, w_down)


def _combine_kernel(dest_ref, ys_ref, info_ref, h_ref, fw_ref, o_ref, buf, sem, *, tc):
    i = pl.program_id(0)
    n = pl.num_programs(0)

    def issue(step, slot):
        base = step * (tc * MOE_TOP_K)

        def body(r, carry):
            for k in range(MOE_TOP_K):
                _row_copy(ys_ref, dest_ref[base + MOE_TOP_K * r + k], buf.at[slot, k], r, sem.at[slot]).start()
            return carry

        lax.fori_loop(0, tc, body, 0)

    @pl.when(i == 0)
    def _():
        issue(0, 0)

    slot = i % 2

    @pl.when(i + 1 < n)
    def _():
        issue(i + 1, 1 - slot)

    def drain(r, carry):
        for k in range(MOE_TOP_K):
            _row_copy(ys_ref, 0, buf.at[slot, k], 0, sem.at[slot]).wait()
        return carry

    lax.fori_loop(0, tc, drain, 0)

    info = info_ref[...]
    lane = lax.broadcasted_iota(jnp.int32, info.shape, 1)
    w1 = jnp.sum(jnp.where(lane == INFO_W1, info, 0.0), axis=-1, keepdims=True)
    w2 = jnp.sum(jnp.where(lane == INFO_W2, info, 0.0), axis=-1, keepdims=True)
    h = h_ref[...] + (buf[slot, 0] * w1 + buf[slot, 1] * w2)
    o_ref[...] = _rms(h, fw_ref[...])


def _combine(dest, ys, info, h, final_w, tc=256):
    T = h.shape[0]
    return pl.pallas_call(
        functools.partial(_combine_kernel, tc=tc),
        grid_spec=pltpu.PrefetchScalarGridSpec(
            num_scalar_prefetch=1,
            grid=(T // tc,),
            in_specs=[pl.BlockSpec(memory_space=pl.ANY),
                      pl.BlockSpec((tc, LANES), lambda i, d: (i, 0)),
                      pl.BlockSpec((tc, D_MODEL), lambda i, d: (i, 0)),
                      pl.BlockSpec((1, D_MODEL), lambda i, d: (0, 0))],
            out_specs=pl.BlockSpec((tc, D_MODEL), lambda i, d: (i, 0)),
            scratch_shapes=[pltpu.VMEM((2, MOE_TOP_K, tc, D_MODEL), F32),
                            pltpu.SemaphoreType.DMA((2,))],
        ),
        out_shape=jax.ShapeDtypeStruct((T, D_MODEL), F32),
        compiler_params=_cparams(("arbitrary",)),
        name="combine",
    )(dest, ys, info, h, final_w[None, :])


def _moe_plan(info, counts, T):
    nblk = -(-(T * MOE_TOP_K) // MOE_ROWS) + MOE_N_EXPERTS
    cnt = counts[0, :MOE_N_EXPERTS].astype(jnp.int32)
    padded = ((cnt + MOE_ROWS - 1) // MOE_ROWS) * MOE_ROWS
    pend = jnp.cumsum(padded)
    pstart = pend - padded
    e = info[:, INFO_E1:INFO_E2 + 1].astype(jnp.int32)
    rank = info[:, INFO_R1:INFO_R2 + 1].astype(jnp.int32)
    dest = (pstart[e] + rank).reshape(-1)
    n_used = (pend[-1] // MOE_ROWS).reshape(1)
    block_expert = jnp.clip(jnp.searchsorted(pend, jnp.arange(nblk, dtype=jnp.int32) * MOE_ROWS, side='right'),
                            0, MOE_N_EXPERTS - 1).astype(jnp.int32)
    return dest, n_used, block_expert, nblk * MOE_ROWS


def _router_weights(router_group_w, router_group_b, router_expert_w, router_expert_b):
    we = jnp.transpose(router_expert_w, (1, 0, 2)).reshape(D_MODEL, MOE_N_EXPERTS)
    wr = jnp.zeros((D_MODEL, LANES), F32)
    wr = wr.at[:, :MOE_N_EXPERTS].set(we).at[:, MOE_N_EXPERTS:MOE_N_EXPERTS + MOE_GROUPS].set(router_group_w)
    br = jnp.zeros((1, LANES), F32)
    br = br.at[0, :MOE_N_EXPERTS].set(router_expert_b.reshape(-1))
    br = br.at[0, MOE_N_EXPERTS:MOE_N_EXPERTS + MOE_GROUPS].set(router_group_b)
    return wr, br


def kernel(x, norm1_w, w_in, gla_fwd_gate_w, gla_fwd_gate_b, gla_bwd_gate_w, gla_bwd_gate_b,
           gla_norm_w, w_out, norm2_w, router_group_w, router_group_b, router_expert_w,
           router_expert_b, expert_w_gate, expert_w_up, expert_w_down, final_norm_w):
    B, S, D = x.shape
    T = B * S
    assert norm1_w.shape[0] == 1, "single-layer trunk: the final norm is fused into the combine step"
    h = x.reshape(T, D)
    gla_slab, loga, att_slab = _inproj(h, S, norm1_w[0], w_in[0], gla_fwd_gate_w[0], gla_fwd_gate_b[0],
                                       gla_bwd_gate_w[0], gla_bwd_gate_b[0])
    o_f, o_b = _gla(gla_slab, loga, B, S)
    att_out = _attention(att_slab, B, S)
    wr, br = _router_weights(router_group_w[0], router_group_b[0], router_expert_w[0], router_expert_b[0])
    h, u2, logits = _outproj(o_f, o_b, gla_slab, att_out, h, gla_norm_w[0], w_out[0], norm2_w[0], wr, br)
    info, counts = _route(logits)
    dest, n_used, block_expert, cap = _moe_plan(info, counts, T)
    xs = _dispatch(dest, u2, cap)
    ys = _experts(block_expert, n_used, xs, expert_w_gate[0], expert_w_up[0], expert_w_down[0])
    out = _combine(dest, ys, info, h, final_norm_w)
    return out.reshape(B, S, D)
```

```python
import functools

import jax
import jax.numpy as jnp
from jax import lax
from jax.experimental import pallas as pl
from jax.experimental.pallas import tpu as pltpu

F32 = jnp.float32
BF16 = jnp.bfloat16

D_MODEL = 1024
GLA_HEADS = 4
GLA_DV = 128
GLA_DK = 64
GLA_KEY_WIDTH = GLA_HEADS * GLA_DK
GLA_VAL_WIDTH = GLA_HEADS * GLA_DV
GLA_GATE_RANK = 16
GLA_TAU = 16.0
GLA_CHUNK = 64
ATT_WIDTH = 512
ATT_HEAD_DIM = 64
ATT_HEADS = 8
ROT_DIM = 16
ROPE_THETA = 500000.0
DILATED_PATTERNS = ((128, 1), (512, 4), (2048, 16))
ATT_RADIUS = 64
MOE_GROUPS = 4
MOE_EXPERTS_PER_GROUP = 8
MOE_N_EXPERTS = 32
MOE_TOP_K = 2
MOE_D_FF = 512
EPS = 1e-6
NEG_INF = -1e30
LOG2E = 1.4426950408889634

LANES = 128
MOE_ROWS = 256
VMEM_LIMIT = 56 * 1024 * 1024


def _cparams(sem):
    return pltpu.CompilerParams(dimension_semantics=sem, vmem_limit_bytes=VMEM_LIMIT)


def _dot(a, b):
    return jnp.dot(a, b, preferred_element_type=F32)


def _dot_nt(a, b):
    return lax.dot_general(a, b, (((1,), (1,)), ((), ())), preferred_element_type=F32)


def _dot_tn(a, b):
    return lax.dot_general(a, b, (((0,), (0,)), ((), ())), preferred_element_type=F32)


def _rms(x, w):
    return x * lax.rsqrt(jnp.mean(x * x, axis=-1, keepdims=True) + EPS) * w


def _inproj_kernel(x_ref, n1_ref, wg_ref, wlr_ref, wa_ref, gw_ref, gb_ref,
                   rc_ref, rs1_ref, rs2_ref, gla_ref, loga_ref, att_ref):
    x = x_ref[...]
    ub = _rms(x, n1_ref[...]).astype(BF16)
    g = _dot(ub, wg_ref[...])
    gla_ref[:, :GLA_KEY_WIDTH] = g[:, :GLA_KEY_WIDTH] * (GLA_DK ** -0.5)
    gla_ref[:, GLA_KEY_WIDTH:] = g[:, GLA_KEY_WIDTH:]
    lr = _dot(ub, wlr_ref[...])
    gate = _dot(lr.astype(BF16), gw_ref[...]) + gb_ref[...]
    loga_ref[...] = (jnp.minimum(gate, 0.0) - jnp.log(1.0 + jnp.exp(-jnp.abs(gate)))) * (1.0 / GLA_TAU)
    a = _dot(ub, wa_ref[...])
    qk = a[:, :2 * ATT_WIDTH]
    reps = 2 * ATT_WIDTH // LANES
    c = jnp.concatenate([rc_ref[...]] * reps, axis=1)
    s1 = jnp.concatenate([rs1_ref[...]] * reps, axis=1)
    s2 = jnp.concatenate([rs2_ref[...]] * reps, axis=1)
    half = ROT_DIM // 2
    n = 2 * ATT_WIDTH
    roped = qk * c + pltpu.roll(qk, n - half, 1) * s1 + pltpu.roll(qk, half, 1) * s2
    att_ref[:, :ATT_WIDTH] = roped[:, :ATT_WIDTH] * (ATT_HEAD_DIM ** -0.5 * LOG2E)
    att_ref[:, ATT_WIDTH:2 * ATT_WIDTH] = roped[:, ATT_WIDTH:]
    att_ref[:, 2 * ATT_WIDTH:] = a[:, 2 * ATT_WIDTH:]


def _rope_lane_tables(S):
    half = ROT_DIM // 2
    inv = ROPE_THETA ** (-(jnp.arange(0, ROT_DIM, 2, dtype=F32) / ROT_DIM))
    ang = jnp.arange(S, dtype=F32)[:, None] * inv[None, :]
    cos, sin = jnp.cos(ang), jnp.sin(ang)
    ones = jnp.ones((S, ATT_HEAD_DIM - ROT_DIM), F32)
    zeros8 = jnp.zeros((S, half), F32)
    zrest = jnp.zeros((S, ATT_HEAD_DIM - ROT_DIM), F32)
    c = jnp.concatenate([cos, cos, ones], axis=1)
    s1 = jnp.concatenate([-sin, zeros8, zrest], axis=1)
    s2 = jnp.concatenate([zeros8, sin, zrest], axis=1)
    rep = LANES // ATT_HEAD_DIM
    return (jnp.tile(c, (1, rep)), jnp.tile(s1, (1, rep)), jnp.tile(s2, (1, rep)))


def _inproj(x2, S, norm1_w, w_in, wf, bfw, wb, bbw, tm=512):
    T = x2.shape[0]
    o_lr = 2 * GLA_KEY_WIDTH + 2 * GLA_VAL_WIDTH
    o_att = o_lr + 2 * GLA_GATE_RANK
    wg = w_in[:, :o_lr].astype(BF16)
    wlr = jnp.zeros((D_MODEL, LANES), F32).at[:, :2 * GLA_GATE_RANK].set(w_in[:, o_lr:o_att]).astype(BF16)
    wa = w_in[:, o_att:].astype(BF16)
    gw = jnp.zeros((LANES, 2 * GLA_KEY_WIDTH), F32)
    gw = gw.at[:GLA_GATE_RANK, :GLA_KEY_WIDTH].set(wf)
    gw = gw.at[GLA_GATE_RANK:2 * GLA_GATE_RANK, GLA_KEY_WIDTH:].set(wb).astype(BF16)
    gb = jnp.concatenate([bfw, bbw])[None, :]
    rc, rs1, rs2 = _rope_lane_tables(S)
    nS = S // tm
    row = lambda i: (i, 0)
    const = lambda i: (0, 0)
    pos = lambda i: (i % nS, 0)
    return pl.pallas_call(
        _inproj_kernel,
        grid=(T // tm,),
        in_specs=[
            pl.BlockSpec((tm, D_MODEL), row),
            pl.BlockSpec((1, D_MODEL), const),
            pl.BlockSpec((D_MODEL, o_lr), const),
            pl.BlockSpec((D_MODEL, LANES), const),
            pl.BlockSpec((D_MODEL, 3 * ATT_WIDTH), const),
            pl.BlockSpec((LANES, 2 * GLA_KEY_WIDTH), const),
            pl.BlockSpec((1, 2 * GLA_KEY_WIDTH), const),
            pl.BlockSpec((tm, LANES), pos),
            pl.BlockSpec((tm, LANES), pos),
            pl.BlockSpec((tm, LANES), pos),
        ],
        out_specs=[
            pl.BlockSpec((tm, o_lr), row),
            pl.BlockSpec((tm, 2 * GLA_KEY_WIDTH), row),
            pl.BlockSpec((tm, 3 * ATT_WIDTH), row),
        ],
        out_shape=[
            jax.ShapeDtypeStruct((T, o_lr), F32),
            jax.ShapeDtypeStruct((T, 2 * GLA_KEY_WIDTH), F32),
            jax.ShapeDtypeStruct((T, 3 * ATT_WIDTH), F32),
        ],
        compiler_params=_cparams(("arbitrary",)),
        name="inproj",
    )(x2, norm1_w[None, :], wg, wlr, wa, gw, gb, rc, rs1, rs2)


def _gla_direction(q, k, v, la, s_ref, o_ref, forward, G):
    C = GLA_CHUNK
    R = G * C
    r = lax.broadcasted_iota(jnp.int32, (R, R), 0)
    c = lax.broadcasted_iota(jnp.int32, (R, R), 1)
    same = (r >> 6) == (c >> 6)
    tri = (c <= r) if forward else (c >= r)
    t_mat = jnp.where(same, jnp.where(tri, 1.0, 0.0), 0.0)
    e_mat = jnp.where(same, 1.0, 0.0)
    te = jnp.concatenate([t_mat, e_mat], axis=0).astype(BF16)
    hi = la.astype(BF16)
    lo = (la - hi.astype(F32)).astype(BF16)
    bt = _dot(te, hi) + _dot(te, lo)
    b = bt[:R]
    tot = bt[R:]
    q_dec = (q * jnp.exp(b)).astype(BF16)
    k_inv = k * jnp.exp(-b)
    k_end = (k * jnp.exp(tot - b)).astype(BF16)
    dec = jnp.exp(tot)

    lane_k = lax.broadcasted_iota(jnp.int32, (C, GLA_KEY_WIDTH), 1)
    lane_v = lax.broadcasted_iota(jnp.int32, (C, GLA_VAL_WIDTH), 1)
    row_c = lax.broadcasted_iota(jnp.int32, (C, GLA_KEY_WIDTH), 0)
    col_in = lane_k & (C - 1)
    a_mask = (col_in <= row_c) if forward else (col_in >= row_c)
    srow = lax.broadcasted_iota(jnp.int32, (GLA_VAL_WIDTH, GLA_KEY_WIDTH), 0)
    scol = lax.broadcasted_iota(jnp.int32, (GLA_VAL_WIDTH, GLA_KEY_WIDTH), 1)
    s_mask = (srow >> 7) == (scol >> 6)

    order = range(G) if forward else range(G - 1, -1, -1)
    for g in order:
        rows = slice(g * C, (g + 1) * C)
        qd, ki, ke, vv = q_dec[rows], k_inv[rows], k_end[rows], v[rows]
        km = jnp.concatenate([jnp.where((lane_k >> 6) == h, ki, 0.0) for h in range(GLA_HEADS)], axis=0)
        a = _dot_nt(qd, km.astype(BF16))
        a = jnp.where(a_mask, a, 0.0).astype(BF16)
        vbd = jnp.concatenate([jnp.where((lane_v >> 7) == h, vv, 0.0) for h in range(GLA_HEADS)], axis=0)
        st = s_ref[...]
        o = _dot(a, vbd.astype(BF16)) + _dot_nt(qd, st.astype(BF16))
        o_ref[rows, :] = o
        kv = _dot_tn(vv.astype(BF16), ke)
        s_ref[...] = st * dec[g * C:g * C + 1, :] + jnp.where(s_mask, kv, 0.0)


def _gla_kernel(qf_ref, kf_ref, vf_ref, laf_ref, qb_ref, kb_ref, vb_ref, lab_ref,
                of_ref, ob_ref, sf_ref, sb_ref, *, G):
    @pl.when(pl.program_id(1) == 0)
    def _():
        sf_ref[...] = jnp.zeros_like(sf_ref)
        sb_ref[...] = jnp.zeros_like(sb_ref)

    _gla_direction(qf_ref[...], kf_ref[...], vf_ref[...], laf_ref[...], sf_ref, of_ref, True, G)
    _gla_direction(qb_ref[...], kb_ref[...], vb_ref[...], lab_ref[...], sb_ref, ob_ref, False, G)


def _gla(gla_slab, loga, B, S, G=4):
    T = B * S
    R = G * GLA_CHUNK
    ns = S // R
    fwd = lambda col: (lambda b, i: (b * ns + i, col))
    bwd = lambda col: (lambda b, i: (b * ns + ns - 1 - i, col))
    kw, vw = GLA_KEY_WIDTH, GLA_VAL_WIDTH
    return pl.pallas_call(
        functools.partial(_gla_kernel, G=G),
        grid=(B, ns),
        in_specs=[
            pl.BlockSpec((R, kw), fwd(0)), pl.BlockSpec((R, kw), fwd(1)),
            pl.BlockSpec((R, vw), fwd(1)), pl.BlockSpec((R, kw), fwd(0)),
            pl.BlockSpec((R, kw), bwd(0)), pl.BlockSpec((R, kw), bwd(1)),
            pl.BlockSpec((R, vw), bwd(1)), pl.BlockSpec((R, kw), bwd(1)),
        ],
        out_specs=[pl.BlockSpec((R, vw), fwd(0)), pl.BlockSpec((R, vw), bwd(0))],
        out_shape=[jax.ShapeDtypeStruct((T, vw), F32), jax.ShapeDtypeStruct((T, vw), F32)],
        scratch_shapes=[pltpu.VMEM((vw, kw), F32), pltpu.VMEM((vw, kw), F32)],
        compiler_params=_cparams(("arbitrary", "arbitrary")),
        name="gla",
    )(gla_slab, gla_slab, gla_slab, loga, gla_slab, gla_slab, gla_slab, loga)


ATT_QB = 128
ATT_KB = ATT_QB + 2 * ATT_RADIUS


ATT_UNROLL = 4


def _att_kernel(q_ref, k_ref, v_ref, o_ref, m_ref, l_ref, bias_ref, *, S):
    QB, KB = ATT_QB, ATT_KB
    lane = lax.broadcasted_iota(jnp.int32, (QB, LANES), 1)
    head0 = lane < ATT_HEAD_DIM

    @pl.when((pl.program_id(0) == 0) & (pl.program_id(1) == 0))
    def _():
        rowi = lax.broadcasted_iota(jnp.int32, (2 * QB, KB), 0) & (QB - 1)
        coli = lax.broadcasted_iota(jnp.int32, (2 * QB, KB), 1)
        for case in range(3):
            valid = jnp.abs(rowi - coli + case * ATT_RADIUS) <= ATT_RADIUS
            bias_ref[case] = jnp.where(valid, 0.0, NEG_INF)

    for pi, (_, d) in enumerate(DILATED_PATTERNS):
        L = S // d
        nb = L // QB
        shift = nb.bit_length() - 1
        first = pi == 0
        last = pi == len(DILATED_PATTERNS) - 1

        def scores(n, d=d, L=L, nb=nb, shift=shift):
            cls = n >> shift
            q0 = (n & (nb - 1)) * QB
            ws = jnp.clip(q0 - ATT_RADIUS, 0, L - KB)
            if d == 1:
                qsl = pl.ds(pl.multiple_of(q0, QB), QB)
                ksl = pl.ds(pl.multiple_of(ws, ATT_RADIUS), KB)
            else:
                qsl = pl.ds(cls + d * q0, QB, stride=d)
                ksl = pl.ds(cls + d * ws, KB, stride=d)
            q = q_ref[qsl, :]
            q2 = jnp.concatenate([jnp.where(head0, q, 0.0), jnp.where(head0, 0.0, q)], axis=0).astype(BF16)
            s = _dot_nt(q2, k_ref[ksl, :].astype(BF16))
            return qsl, ksl, s + bias_ref[(q0 - ws) >> 6]

        def softmax_pv(qsl, ksl, s):
            m_blk = jnp.max(s, axis=-1, keepdims=True)
            p = jnp.exp2(s - m_blk)
            l_blk = jnp.sum(p, axis=-1, keepdims=True)
            pv = _dot(p.astype(BF16), v_ref[ksl, :].astype(BF16))
            acc_b = jnp.where(head0, pv[:QB], pv[QB:])
            m_b = jnp.where(head0, m_blk[:QB], m_blk[QB:])
            l_b = jnp.where(head0, l_blk[:QB], l_blk[QB:])
            return qsl, acc_b, m_b, l_b

        def body(n, carry, first=first, last=last):
            staged = [scores(n * ATT_UNROLL + u) for u in range(ATT_UNROLL)]
            blocks = [softmax_pv(*st) for st in staged]
            for qsl, acc_b, m_b, l_b in blocks:
                if first:
                    acc, m_new, l_new = acc_b, m_b, l_b
                else:
                    m_old = m_ref[qsl, :]
                    m_new = jnp.maximum(m_old, m_b)
                    w_old = jnp.exp2(m_old - m_new)
                    w_blk = jnp.exp2(m_b - m_new)
                    acc = o_ref[qsl, :] * w_old + acc_b * w_blk
                    l_new = l_ref[qsl, :] * w_old + l_b * w_blk
                if last:
                    o_ref[qsl, :] = acc / l_new
                else:
                    o_ref[qsl, :] = acc
                    m_ref[qsl, :] = m_new
                    l_ref[qsl, :] = l_new
            return carry

        lax.fori_loop(0, S // (QB * ATT_UNROLL), body, 0)


def _attention(att_slab, B, S):
    T = B * S
    ncol = ATT_WIDTH // LANES
    return pl.pallas_call(
        functools.partial(_att_kernel, S=S),
        grid=(B, ncol),
        in_specs=[
            pl.BlockSpec((S, LANES), lambda b, h: (b, h)),
            pl.BlockSpec((S, LANES), lambda b, h: (b, ncol + h)),
            pl.BlockSpec((S, LANES), lambda b, h: (b, 2 * ncol + h)),
        ],
        out_specs=pl.BlockSpec((S, LANES), lambda b, h: (b, h)),
        out_shape=jax.ShapeDtypeStruct((T, ATT_WIDTH), F32),
        scratch_shapes=[pltpu.VMEM((S, LANES), F32), pltpu.VMEM((S, LANES), F32),
                        pltpu.VMEM((3, 2 * ATT_QB, ATT_KB), F32)],
        compiler_params=_cparams(("arbitrary", "arbitrary")),
        name="dilated_attention",
    )(att_slab, att_slab, att_slab)


def _outproj_kernel(of_ref, ob_ref, gg_ref, att_ref, x_ref, gnw_ref, wo1_ref, wo2_ref,
                    n2_ref, wr_ref, br_ref, h_ref, u_ref, lg_ref):
    o = of_ref[...] + ob_ref[...]
    gate = gg_ref[...]
    gnw = gnw_ref[...]
    parts = []
    for h in range(GLA_HEADS):
        sl = slice(h * GLA_DV, (h + 1) * GLA_DV)
        parts.append(_rms(o[:, sl], gnw))
    y = jnp.concatenate(parts, axis=1) * (gate / (1.0 + jnp.exp(-gate)))
    mix = _dot(y.astype(BF16), wo1_ref[...]) + _dot(att_ref[...].astype(BF16), wo2_ref[...])
    h = x_ref[...] + mix
    h_ref[...] = h
    u = _rms(h, n2_ref[...])
    u_ref[...] = u
    lg_ref[...] = jnp.dot(u, wr_ref[...], preferred_element_type=F32,
                          precision=lax.Precision.HIGHEST) + br_ref[...]


def _outproj(o_f, o_b, gla_slab, att_out, x2, gla_norm_w, w_out, norm2_w, wr, br, tm=512):
    T = x2.shape[0]
    row = lambda i: (i, 0)
    const = lambda i: (0, 0)
    wo = w_out.astype(BF16)
    return pl.pallas_call(
        _outproj_kernel,
        grid=(T // tm,),
        in_specs=[
            pl.BlockSpec((tm, GLA_VAL_WIDTH), row),
            pl.BlockSpec((tm, GLA_VAL_WIDTH), row),
            pl.BlockSpec((tm, GLA_VAL_WIDTH), lambda i: (i, 2)),
            pl.BlockSpec((tm, ATT_WIDTH), row),
            pl.BlockSpec((tm, D_MODEL), row),
            pl.BlockSpec((1, GLA_DV), const),
            pl.BlockSpec((GLA_VAL_WIDTH, D_MODEL), const),
            pl.BlockSpec((ATT_WIDTH, D_MODEL), const),
            pl.BlockSpec((1, D_MODEL), const),
            pl.BlockSpec((D_MODEL, LANES), const),
            pl.BlockSpec((1, LANES), const),
        ],
        out_specs=[
            pl.BlockSpec((tm, D_MODEL), row),
            pl.BlockSpec((tm, D_MODEL), row),
            pl.BlockSpec((tm, LANES), row),
        ],
        out_shape=[
            jax.ShapeDtypeStruct((T, D_MODEL), F32),
            jax.ShapeDtypeStruct((T, D_MODEL), F32),
            jax.ShapeDtypeStruct((T, LANES), F32),
        ],
        compiler_params=_cparams(("arbitrary",)),
        name="outproj",
    )(o_f, o_b, gla_slab, att_out, x2, gla_norm_w[None, :], wo[:GLA_VAL_WIDTH], wo[GLA_VAL_WIDTH:],
      norm2_w[None, :], wr, br)


INFO_E1, INFO_E2, INFO_R1, INFO_R2, INFO_W1, INFO_W2 = range(6)


def _route_kernel(lg_ref, info_ref, cnt_ref, carry_ref):
    @pl.when(pl.program_id(0) == 0)
    def _():
        carry_ref[...] = jnp.zeros_like(carry_ref)

    lg = lg_ref[...]
    tr = lg.shape[0]
    lane = lax.broadcasted_iota(jnp.int32, (tr, LANES), 1)
    big = jnp.int32(1 << 20)
    is_g = (lane >= MOE_N_EXPERTS) & (lane < MOE_N_EXPERTS + MOE_GROUPS)
    gl = jnp.where(is_g, lg, -jnp.inf)
    gmax = jnp.max(gl, axis=-1, keepdims=True)
    gsel = jnp.min(jnp.where(gl == gmax, lane - MOE_N_EXPERTS, big), axis=-1, keepdims=True)
    g_w = 1.0 / jnp.sum(jnp.where(is_g, jnp.exp(lg - gmax), 0.0), axis=-1, keepdims=True)
    in_grp = (lane < MOE_N_EXPERTS) & ((lane >> 3) == gsel)
    el = jnp.where(in_grp, lg, -jnp.inf)
    v1 = jnp.max(el, axis=-1, keepdims=True)
    i1 = jnp.min(jnp.where(el == v1, lane, big), axis=-1, keepdims=True)
    el2 = jnp.where(lane == i1, -jnp.inf, el)
    v2 = jnp.max(el2, axis=-1, keepdims=True)
    i2 = jnp.min(jnp.where(el2 == v2, lane, big), axis=-1, keepdims=True)
    t = jnp.exp(v2 - v1)
    w1 = g_w * (1.0 / (1.0 + t))
    w2 = g_w * (t / (1.0 + t))

    hit1 = lane == i1
    hit2 = lane == i2
    member = jnp.where(hit1 | hit2, 1.0, 0.0)
    r = lax.broadcasted_iota(jnp.int32, (tr, tr), 0)
    c = lax.broadcasted_iota(jnp.int32, (tr, tr), 1)
    strict = jnp.where(c < r, 1.0, 0.0).astype(BF16)
    prefix = _dot(strict, member.astype(BF16)) + carry_ref[...]
    rank1 = jnp.sum(jnp.where(hit1, prefix, 0.0), axis=-1, keepdims=True)
    rank2 = jnp.sum(jnp.where(hit2, prefix, 0.0), axis=-1, keepdims=True)
    carry = carry_ref[...] + jnp.sum(member, axis=0, keepdims=True)
    carry_ref[...] = carry
    cnt_ref[...] = carry

    info = jnp.where(lane == INFO_E1, i1.astype(F32), 0.0)
    info = jnp.where(lane == INFO_E2, i2.astype(F32), info)
    info = jnp.where(lane == INFO_R1, rank1, info)
    info = jnp.where(lane == INFO_R2, rank2, info)
    info = jnp.where(lane == INFO_W1, w1, info)
    info = jnp.where(lane == INFO_W2, w2, info)
    info_ref[...] = info


def _route(logits, tr=512):
    T = logits.shape[0]
    return pl.pallas_call(
        _route_kernel,
        grid=(T // tr,),
        in_specs=[pl.BlockSpec((tr, LANES), lambda i: (i, 0))],
        out_specs=[pl.BlockSpec((tr, LANES), lambda i: (i, 0)), pl.BlockSpec((1, LANES), lambda i: (0, 0))],
        out_shape=[jax.ShapeDtypeStruct((T, LANES), F32), jax.ShapeDtypeStruct((1, LANES), F32)],
        scratch_shapes=[pltpu.VMEM((1, LANES), F32)],
        compiler_params=_cparams(("arbitrary",)),
        name="route",
    )(logits)


def _row_copy(src_ref, src_row, dst_ref, dst_row, sem):
    return pltpu.make_async_copy(src_ref.at[pl.ds(src_row, 1)], dst_ref.at[pl.ds(dst_row, 1)], sem)


ROW_UNROLL = 8


def _dispatch_kernel(dest_ref, zblk_ref, u_ref, xs_ref, zbuf, sem, zsem, *, td):
    @pl.when(pl.program_id(0) == 0)
    def _():
        zbuf[...] = jnp.zeros_like(zbuf)

        def zero_copy(j):
            start = pl.multiple_of(zblk_ref[j] * MOE_ROWS, MOE_ROWS)
            return pltpu.make_async_copy(zbuf, xs_ref.at[pl.ds(start, MOE_ROWS)], zsem)

        def start(j, carry):
            @pl.when(zblk_ref[j] >= 0)
            def _():
                zero_copy(j).start()
            return carry

        def wait(j, carry):
            @pl.when(zblk_ref[j] >= 0)
            def _():
                zero_copy(j).wait()
            return carry

        lax.fori_loop(0, 2 * MOE_N_EXPERTS, start, 0)
        lax.fori_loop(0, 2 * MOE_N_EXPERTS, wait, 0)

    base = pl.program_id(0) * (td * MOE_TOP_K)

    def issue(g, carry):
        for j in range(ROW_UNROLL):
            r = g * ROW_UNROLL + j
            for k in range(MOE_TOP_K):
                _row_copy(u_ref, r, xs_ref, dest_ref[base + MOE_TOP_K * r + k], sem).start(priority=k)
        return carry

    lax.fori_loop(0, td // ROW_UNROLL, issue, 0)
    for k in range(MOE_TOP_K):
        pltpu.make_async_copy(u_ref, xs_ref.at[pl.ds(0, td)], sem).wait()


def _dispatch(dest, zero_blocks, u2, cap, td=256):
    T = u2.shape[0]
    return pl.pallas_call(
        functools.partial(_dispatch_kernel, td=td),
        grid_spec=pltpu.PrefetchScalarGridSpec(
            num_scalar_prefetch=2,
            grid=(T // td,),
            in_specs=[pl.BlockSpec((td, D_MODEL), lambda i, d, z: (i, 0))],
            out_specs=pl.BlockSpec(memory_space=pl.ANY),
            scratch_shapes=[pltpu.VMEM((MOE_ROWS, D_MODEL), F32),
                            pltpu.SemaphoreType.DMA(()), pltpu.SemaphoreType.DMA(())],
        ),
        out_shape=jax.ShapeDtypeStruct((cap, D_MODEL), F32),
        compiler_params=_cparams(("arbitrary",)),
        name="dispatch",
    )(dest, zero_blocks, u2)


def _expert_kernel(be_ref, nu_ref, x_ref, wg_ref, wu_ref, wd_ref, y_ref, wgb, wub, wdb):
    b = pl.program_id(0)
    prev = be_ref[jnp.maximum(b - 1, 0)]
    fresh = (b == 0) | (be_ref[b] != prev)

    @pl.when(fresh)
    def _():
        wgb[...] = wg_ref[...].astype(BF16)
        wub[...] = wu_ref[...].astype(BF16)
        wdb[...] = wd_ref[...].astype(BF16)

    @pl.when(b < nu_ref[0])
    def _():
        xb = x_ref[...].astype(BF16)
        g = _dot(xb, wgb[...])
        u = _dot(xb, wub[...])
        hid = (g / (1.0 + jnp.exp(-g))) * u
        y_ref[...] = _dot(hid.astype(BF16), wdb[...])

    @pl.when(b >= nu_ref[0])
    def _():
        y_ref[...] = jnp.zeros_like(y_ref)


def _experts(block_expert, n_used, xs, w_gate, w_up, w_down):
    cap = xs.shape[0]
    nblk = cap // MOE_ROWS
    rows = lambda b, be, nu: (jnp.minimum(b, nu[0] - 1), 0)
    wsel = lambda b, be, nu: (be[b], 0, 0)
    return pl.pallas_call(
        _expert_kernel,
        grid_spec=pltpu.PrefetchScalarGridSpec(
            num_scalar_prefetch=2,
            grid=(nblk,),
            in_specs=[
                pl.BlockSpec((MOE_ROWS, D_MODEL), rows),
                pl.BlockSpec((None, D_MODEL, MOE_D_FF), wsel),
                pl.BlockSpec((None, D_MODEL, MOE_D_FF), wsel),
                pl.BlockSpec((None, MOE_D_FF, D_MODEL), wsel),
            ],
            out_specs=pl.BlockSpec((MOE_ROWS, D_MODEL), lambda b, be, nu: (b, 0)),
            scratch_shapes=[pltpu.VMEM((D_MODEL, MOE_D_FF), BF16),
                            pltpu.VMEM((D_MODEL, MOE_D_FF), BF16),
                            pltpu.VMEM((MOE_D_FF, D_MODEL), BF16)],
        ),
        out_shape=jax.ShapeDtypeStruct((cap, D_MODEL), F32),
        compiler_params=_cparams(("arbitrary",)),
        name="experts",
    )(block_expert, n_used, xs, w_gate, w_up, w_down)


def _combine_kernel(dest_ref, ys_ref, info_ref, h_ref, fw_ref, o_ref, buf, sem, *, tc):
    i = pl.program_id(0)
    n = pl.num_programs(0)

    def issue(step, slot):
        base = step * (tc * MOE_TOP_K)

        def body(g, carry):
            for j in range(ROW_UNROLL):
                r = g * ROW_UNROLL + j
                for k in range(MOE_TOP_K):
                    _row_copy(ys_ref, dest_ref[base + MOE_TOP_K * r + k], buf.at[slot, k], r,
                              sem.at[slot]).start(priority=k)
            return carry

        lax.fori_loop(0, tc // ROW_UNROLL, body, 0)

    @pl.when(i == 0)
    def _():
        issue(0, 0)

    slot = i % 2

    @pl.when(i + 1 < n)
    def _():
        issue(i + 1, 1 - slot)

    for k in range(MOE_TOP_K):
        pltpu.make_async_copy(ys_ref.at[pl.ds(0, tc)], buf.at[slot, k], sem.at[slot]).wait()

    info = info_ref[...]
    lane = lax.broadcasted_iota(jnp.int32, info.shape, 1)
    w1 = jnp.sum(jnp.where(lane == INFO_W1, info, 0.0), axis=-1, keepdims=True)
    w2 = jnp.sum(jnp.where(lane == INFO_W2, info, 0.0), axis=-1, keepdims=True)
    h = h_ref[...] + (buf[slot, 0] * w1 + buf[slot, 1] * w2)
    o_ref[...] = _rms(h, fw_ref[...])


def _combine(dest, ys, info, h, final_w, tc=256):
    T = h.shape[0]
    return pl.pallas_call(
        functools.partial(_combine_kernel, tc=tc),
        grid_spec=pltpu.PrefetchScalarGridSpec(
            num_scalar_prefetch=1,
            grid=(T // tc,),
            in_specs=[pl.BlockSpec(memory_space=pl.ANY),
                      pl.BlockSpec((tc, LANES), lambda i, d: (i, 0)),
                      pl.BlockSpec((tc, D_MODEL), lambda i, d: (i, 0)),
                      pl.BlockSpec((1, D_MODEL), lambda i, d: (0, 0))],
            out_specs=pl.BlockSpec((tc, D_MODEL), lambda i, d: (i, 0)),
            scratch_shapes=[pltpu.VMEM((2, MOE_TOP_K, tc, D_MODEL), F32),
                            pltpu.SemaphoreType.DMA((2,))],
        ),
        out_shape=jax.ShapeDtypeStruct((T, D_MODEL), F32),
        compiler_params=_cparams(("arbitrary",)),
        name="combine",
    )(dest, ys, info, h, final_w[None, :])


def _moe_plan(info, counts, T):
    nblk = -(-(T * MOE_TOP_K) // MOE_ROWS) + MOE_N_EXPERTS
    cnt = counts[0, :MOE_N_EXPERTS].astype(jnp.int32)
    padded = ((cnt + MOE_ROWS - 1) // MOE_ROWS) * MOE_ROWS
    pend = jnp.cumsum(padded)
    pstart = pend - padded
    e = info[:, INFO_E1:INFO_E2 + 1].astype(jnp.int32)
    rank = info[:, INFO_R1:INFO_R2 + 1].astype(jnp.int32)
    ids = jnp.arange(MOE_N_EXPERTS, dtype=jnp.int32)
    dest = (jnp.sum(jnp.where(e[:, :, None] == ids, pstart, 0), axis=-1) + rank).reshape(-1)
    n_used = pend[-1] // MOE_ROWS
    blocks = jnp.arange(nblk, dtype=jnp.int32)
    block_expert = jnp.minimum(jnp.sum(pend[None, :] <= (blocks * MOE_ROWS)[:, None], axis=1),
                               MOE_N_EXPERTS - 1).astype(jnp.int32)
    last_block = jnp.where(padded > 0, pend // MOE_ROWS - 1, -1)
    tail = n_used + ids
    zero_blocks = jnp.concatenate([last_block, jnp.where(tail < nblk, tail, -1)]).astype(jnp.int32)
    return dest, n_used.reshape(1), block_expert, zero_blocks, nblk * MOE_ROWS


def _router_weights(router_group_w, router_group_b, router_expert_w, router_expert_b):
    we = jnp.transpose(router_expert_w, (1, 0, 2)).reshape(D_MODEL, MOE_N_EXPERTS)
    wr = jnp.zeros((D_MODEL, LANES), F32)
    wr = wr.at[:, :MOE_N_EXPERTS].set(we).at[:, MOE_N_EXPERTS:MOE_N_EXPERTS + MOE_GROUPS].set(router_group_w)
    br = jnp.zeros((1, LANES), F32)
    br = br.at[0, :MOE_N_EXPERTS].set(router_expert_b.reshape(-1))
    br = br.at[0, MOE_N_EXPERTS:MOE_N_EXPERTS + MOE_GROUPS].set(router_group_b)
    return wr, br


def kernel(x, norm1_w, w_in, gla_fwd_gate_w, gla_fwd_gate_b, gla_bwd_gate_w, gla_bwd_gate_b,
           gla_norm_w, w_out, norm2_w, router_group_w, router_group_b, router_expert_w,
           router_expert_b, expert_w_gate, expert_w_up, expert_w_down, final_norm_w):
    B, S, D = x.shape
    T = B * S
    assert norm1_w.shape[0] == 1, "single-layer trunk: the final norm is fused into the combine step"
    h = x.reshape(T, D)
    gla_slab, loga, att_slab = _inproj(h, S, norm1_w[0], w_in[0], gla_fwd_gate_w[0], gla_fwd_gate_b[0],
                                       gla_bwd_gate_w[0], gla_bwd_gate_b[0])
    o_f, o_b = _gla(gla_slab, loga, B, S)
    att_out = _attention(att_slab, B, S)
    wr, br = _router_weights(router_group_w[0], router_group_b[0], router_expert_w[0], router_expert_b[0])
    h, u2, logits = _outproj(o_f, o_b, gla_slab, att_out, h, gla_norm_w[0], w_out[0], norm2_w[0], wr, br)
    info, counts = _route(logits)
    dest, n_used, block_expert, zero_blocks, cap = _moe_plan(info, counts, T)
    xs = _dispatch(dest, zero_blocks, u2, cap)
    ys = _experts(block_expert, n_used, xs, expert_w_gate[0], expert_w_up[0], expert_w_down[0])
    out = _combine(dest, ys, info, h, final_norm_w)
    return out.reshape(B, S, D)
```

```python
import functools

import jax
import jax.numpy as jnp
from jax import lax
from jax.experimental import pallas as pl
from jax.experimental.pallas import tpu as pltpu

F32 = jnp.float32
BF16 = jnp.bfloat16

D_MODEL = 1024
GLA_HEADS = 4
GLA_DV = 128
GLA_DK = 64
GLA_KEY_WIDTH = GLA_HEADS * GLA_DK
GLA_VAL_WIDTH = GLA_HEADS * GLA_DV
GLA_GATE_RANK = 16
GLA_TAU = 16.0
GLA_CHUNK = 64
ATT_WIDTH = 512
ATT_HEAD_DIM = 64
ATT_HEADS = 8
ROT_DIM = 16
ROPE_THETA = 500000.0
DILATED_PATTERNS = ((128, 1), (512, 4), (2048, 16))
ATT_RADIUS = 64
MOE_GROUPS = 4
MOE_EXPERTS_PER_GROUP = 8
MOE_N_EXPERTS = 32
MOE_TOP_K = 2
MOE_D_FF = 512
EPS = 1e-6
NEG_INF = -1e30
LOG2E = 1.4426950408889634

LANES = 128
MOE_ROWS = 256
VMEM_LIMIT = 56 * 1024 * 1024


def _cparams(sem):
    return pltpu.CompilerParams(dimension_semantics=sem, vmem_limit_bytes=VMEM_LIMIT)


def _dot(a, b):
    return jnp.dot(a, b, preferred_element_type=F32)


def _dot_nt(a, b):
    return lax.dot_general(a, b, (((1,), (1,)), ((), ())), preferred_element_type=F32)


def _dot_tn(a, b):
    return lax.dot_general(a, b, (((0,), (0,)), ((), ())), preferred_element_type=F32)


def _rms(x, w):
    return x * lax.rsqrt(jnp.mean(x * x, axis=-1, keepdims=True) + EPS) * w


def _inproj_kernel(x_ref, n1_ref, wg_ref, wlr_ref, wa_ref, gw_ref, gb_ref,
                   rc_ref, rs1_ref, rs2_ref, gla_ref, loga_ref, att_ref):
    x = x_ref[...]
    ub = _rms(x, n1_ref[...]).astype(BF16)
    g = _dot(ub, wg_ref[...])
    gla_ref[:, :GLA_KEY_WIDTH] = g[:, :GLA_KEY_WIDTH] * (GLA_DK ** -0.5)
    gla_ref[:, GLA_KEY_WIDTH:] = g[:, GLA_KEY_WIDTH:]
    lr = _dot(ub, wlr_ref[...])
    gate = _dot(lr.astype(BF16), gw_ref[...]) + gb_ref[...]
    loga_ref[...] = (jnp.minimum(gate, 0.0) - jnp.log(1.0 + jnp.exp(-jnp.abs(gate)))) * (1.0 / GLA_TAU)
    a = _dot(ub, wa_ref[...])
    qk = a[:, :2 * ATT_WIDTH]
    reps = 2 * ATT_WIDTH // LANES
    c = jnp.concatenate([rc_ref[...]] * reps, axis=1)
    s1 = jnp.concatenate([rs1_ref[...]] * reps, axis=1)
    s2 = jnp.concatenate([rs2_ref[...]] * reps, axis=1)
    half = ROT_DIM // 2
    n = 2 * ATT_WIDTH
    roped = qk * c + pltpu.roll(qk, n - half, 1) * s1 + pltpu.roll(qk, half, 1) * s2
    att_ref[:, :ATT_WIDTH] = roped[:, :ATT_WIDTH] * (ATT_HEAD_DIM ** -0.5 * LOG2E)
    att_ref[:, ATT_WIDTH:2 * ATT_WIDTH] = roped[:, ATT_WIDTH:]
    att_ref[:, 2 * ATT_WIDTH:] = a[:, 2 * ATT_WIDTH:]


def _rope_lane_tables(S):
    half = ROT_DIM // 2
    inv = ROPE_THETA ** (-(jnp.arange(0, ROT_DIM, 2, dtype=F32) / ROT_DIM))
    ang = jnp.arange(S, dtype=F32)[:, None] * inv[None, :]
    cos, sin = jnp.cos(ang), jnp.sin(ang)
    ones = jnp.ones((S, ATT_HEAD_DIM - ROT_DIM), F32)
    zeros8 = jnp.zeros((S, half), F32)
    zrest = jnp.zeros((S, ATT_HEAD_DIM - ROT_DIM), F32)
    c = jnp.concatenate([cos, cos, ones], axis=1)
    s1 = jnp.concatenate([-sin, zeros8, zrest], axis=1)
    s2 = jnp.concatenate([zeros8, sin, zrest], axis=1)
    rep = LANES // ATT_HEAD_DIM
    return (jnp.tile(c, (1, rep)), jnp.tile(s1, (1, rep)), jnp.tile(s2, (1, rep)))


def _inproj(x2, S, norm1_w, w_in, wf, bfw, wb, bbw, tm=512):
    T = x2.shape[0]
    o_lr = 2 * GLA_KEY_WIDTH + 2 * GLA_VAL_WIDTH
    o_att = o_lr + 2 * GLA_GATE_RANK
    wg = w_in[:, :o_lr].astype(BF16)
    wlr = jnp.zeros((D_MODEL, LANES), F32).at[:, :2 * GLA_GATE_RANK].set(w_in[:, o_lr:o_att]).astype(BF16)
    wa = w_in[:, o_att:].astype(BF16)
    gw = jnp.zeros((LANES, 2 * GLA_KEY_WIDTH), F32)
    gw = gw.at[:GLA_GATE_RANK, :GLA_KEY_WIDTH].set(wf)
    gw = gw.at[GLA_GATE_RANK:2 * GLA_GATE_RANK, GLA_KEY_WIDTH:].set(wb).astype(BF16)
    gb = jnp.concatenate([bfw, bbw])[None, :]
    rc, rs1, rs2 = _rope_lane_tables(S)
    nS = S // tm
    row = lambda i: (i, 0)
    const = lambda i: (0, 0)
    pos = lambda i: (i % nS, 0)
    return pl.pallas_call(
        _inproj_kernel,
        grid=(T // tm,),
        in_specs=[
            pl.BlockSpec((tm, D_MODEL), row),
            pl.BlockSpec((1, D_MODEL), const),
            pl.BlockSpec((D_MODEL, o_lr), const),
            pl.BlockSpec((D_MODEL, LANES), const),
            pl.BlockSpec((D_MODEL, 3 * ATT_WIDTH), const),
            pl.BlockSpec((LANES, 2 * GLA_KEY_WIDTH), const),
            pl.BlockSpec((1, 2 * GLA_KEY_WIDTH), const),
            pl.BlockSpec((tm, LANES), pos),
            pl.BlockSpec((tm, LANES), pos),
            pl.BlockSpec((tm, LANES), pos),
        ],
        out_specs=[
            pl.BlockSpec((tm, o_lr), row),
            pl.BlockSpec((tm, 2 * GLA_KEY_WIDTH), row),
            pl.BlockSpec((tm, 3 * ATT_WIDTH), row),
        ],
        out_shape=[
            jax.ShapeDtypeStruct((T, o_lr), F32),
            jax.ShapeDtypeStruct((T, 2 * GLA_KEY_WIDTH), F32),
            jax.ShapeDtypeStruct((T, 3 * ATT_WIDTH), F32),
        ],
        compiler_params=_cparams(("arbitrary",)),
        name="inproj",
    )(x2, norm1_w[None, :], wg, wlr, wa, gw, gb, rc, rs1, rs2)


def _gla_direction(q, k, v, la, s_ref, o_ref, forward, G):
    C = GLA_CHUNK
    R = G * C
    r = lax.broadcasted_iota(jnp.int32, (R, R), 0)
    c = lax.broadcasted_iota(jnp.int32, (R, R), 1)
    same = (r >> 6) == (c >> 6)
    tri = (c <= r) if forward else (c >= r)
    t_mat = jnp.where(same, jnp.where(tri, 1.0, 0.0), 0.0)
    e_mat = jnp.where(same, 1.0, 0.0)
    te = jnp.concatenate([t_mat, e_mat], axis=0).astype(BF16)
    hi = la.astype(BF16)
    lo = (la - hi.astype(F32)).astype(BF16)
    bt = _dot(te, hi) + _dot(te, lo)
    b = bt[:R]
    tot = bt[R:]
    q_dec = (q * jnp.exp(b)).astype(BF16)
    k_inv = k * jnp.exp(-b)
    k_end = (k * jnp.exp(tot - b)).astype(BF16)
    dec = jnp.exp(tot)

    lane_k = lax.broadcasted_iota(jnp.int32, (C, GLA_KEY_WIDTH), 1)
    lane_v = lax.broadcasted_iota(jnp.int32, (C, GLA_VAL_WIDTH), 1)
    row_c = lax.broadcasted_iota(jnp.int32, (C, GLA_KEY_WIDTH), 0)
    col_in = lane_k & (C - 1)
    a_mask = (col_in <= row_c) if forward else (col_in >= row_c)
    srow = lax.broadcasted_iota(jnp.int32, (GLA_VAL_WIDTH, GLA_KEY_WIDTH), 0)
    scol = lax.broadcasted_iota(jnp.int32, (GLA_VAL_WIDTH, GLA_KEY_WIDTH), 1)
    s_mask = (srow >> 7) == (scol >> 6)

    order = range(G) if forward else range(G - 1, -1, -1)
    for g in order:
        rows = slice(g * C, (g + 1) * C)
        qd, ki, ke, vv = q_dec[rows], k_inv[rows], k_end[rows], v[rows]
        km = jnp.concatenate([jnp.where((lane_k >> 6) == h, ki, 0.0) for h in range(GLA_HEADS)], axis=0)
        a = _dot_nt(qd, km.astype(BF16))
        a = jnp.where(a_mask, a, 0.0).astype(BF16)
        vbd = jnp.concatenate([jnp.where((lane_v >> 7) == h, vv, 0.0) for h in range(GLA_HEADS)], axis=0)
        st = s_ref[...]
        o = _dot(a, vbd.astype(BF16)) + _dot_nt(qd, st.astype(BF16))
        o_ref[rows, :] = o
        kv = _dot_tn(vv.astype(BF16), ke)
        s_ref[...] = st * dec[g * C:g * C + 1, :] + jnp.where(s_mask, kv, 0.0)


def _gla_kernel(qf_ref, kf_ref, vf_ref, laf_ref, qb_ref, kb_ref, vb_ref, lab_ref,
                of_ref, ob_ref, sf_ref, sb_ref, *, G):
    @pl.when(pl.program_id(1) == 0)
    def _():
        sf_ref[...] = jnp.zeros_like(sf_ref)
        sb_ref[...] = jnp.zeros_like(sb_ref)

    _gla_direction(qf_ref[...], kf_ref[...], vf_ref[...], laf_ref[...], sf_ref, of_ref, True, G)
    _gla_direction(qb_ref[...], kb_ref[...], vb_ref[...], lab_ref[...], sb_ref, ob_ref, False, G)


def _gla(gla_slab, loga, B, S, G=4):
    T = B * S
    R = G * GLA_CHUNK
    ns = S // R
    fwd = lambda col: (lambda b, i: (b * ns + i, col))
    bwd = lambda col: (lambda b, i: (b * ns + ns - 1 - i, col))
    kw, vw = GLA_KEY_WIDTH, GLA_VAL_WIDTH
    return pl.pallas_call(
        functools.partial(_gla_kernel, G=G),
        grid=(B, ns),
        in_specs=[
            pl.BlockSpec((R, kw), fwd(0)), pl.BlockSpec((R, kw), fwd(1)),
            pl.BlockSpec((R, vw), fwd(1)), pl.BlockSpec((R, kw), fwd(0)),
            pl.BlockSpec((R, kw), bwd(0)), pl.BlockSpec((R, kw), bwd(1)),
            pl.BlockSpec((R, vw), bwd(1)), pl.BlockSpec((R, kw), bwd(1)),
        ],
        out_specs=[pl.BlockSpec((R, vw), fwd(0)), pl.BlockSpec((R, vw), bwd(0))],
        out_shape=[jax.ShapeDtypeStruct((T, vw), F32), jax.ShapeDtypeStruct((T, vw), F32)],
        scratch_shapes=[pltpu.VMEM((vw, kw), F32), pltpu.VMEM((vw, kw), F32)],
        compiler_params=_cparams(("arbitrary", "arbitrary")),
        name="gla",
    )(gla_slab, gla_slab, gla_slab, loga, gla_slab, gla_slab, gla_slab, loga)


ATT_QB = 128
ATT_KB = ATT_QB + 2 * ATT_RADIUS


ATT_UNROLL = 4


def _att_kernel(q_ref, k_ref, v_ref, o_ref, m_ref, l_ref, bias_ref, *, S):
    QB, KB = ATT_QB, ATT_KB
    lane = lax.broadcasted_iota(jnp.int32, (QB, LANES), 1)
    head0 = lane < ATT_HEAD_DIM

    @pl.when((pl.program_id(0) == 0) & (pl.program_id(1) == 0))
    def _():
        rowi = lax.broadcasted_iota(jnp.int32, (2 * QB, KB), 0) & (QB - 1)
        coli = lax.broadcasted_iota(jnp.int32, (2 * QB, KB), 1)
        for case in range(3):
            valid = jnp.abs(rowi - coli + case * ATT_RADIUS) <= ATT_RADIUS
            bias_ref[case] = jnp.where(valid, 0.0, NEG_INF)

    for pi, (_, d) in enumerate(DILATED_PATTERNS):
        L = S // d
        nb = L // QB
        shift = nb.bit_length() - 1
        first = pi == 0
        last = pi == len(DILATED_PATTERNS) - 1

        def scores(n, d=d, L=L, nb=nb, shift=shift):
            cls = n >> shift
            q0 = (n & (nb - 1)) * QB
            ws = jnp.clip(q0 - ATT_RADIUS, 0, L - KB)
            if d == 1:
                qsl = pl.ds(pl.multiple_of(q0, QB), QB)
                ksl = pl.ds(pl.multiple_of(ws, ATT_RADIUS), KB)
            else:
                qsl = pl.ds(cls + d * q0, QB, stride=d)
                ksl = pl.ds(cls + d * ws, KB, stride=d)
            q = q_ref[qsl, :]
            q2 = jnp.concatenate([jnp.where(head0, q, 0.0), jnp.where(head0, 0.0, q)], axis=0).astype(BF16)
            s = _dot_nt(q2, k_ref[ksl, :].astype(BF16))
            return qsl, ksl, s + bias_ref[(q0 - ws) >> 6]

        def softmax_pv(qsl, ksl, s):
            m_blk = jnp.max(s, axis=-1, keepdims=True)
            p = jnp.exp2(s - m_blk)
            l_blk = jnp.sum(p, axis=-1, keepdims=True)
            pv = _dot(p.astype(BF16), v_ref[ksl, :].astype(BF16))
            acc_b = jnp.where(head0, pv[:QB], pv[QB:])
            m_b = jnp.where(head0, m_blk[:QB], m_blk[QB:])
            l_b = jnp.where(head0, l_blk[:QB], l_blk[QB:])
            return qsl, acc_b, m_b, l_b

        def body(n, carry, first=first, last=last):
            staged = [scores(n * ATT_UNROLL + u) for u in range(ATT_UNROLL)]
            blocks = [softmax_pv(*st) for st in staged]
            for qsl, acc_b, m_b, l_b in blocks:
                if first:
                    acc, m_new, l_new = acc_b, m_b, l_b
                else:
                    m_old = m_ref[qsl, :]
                    m_new = jnp.maximum(m_old, m_b)
                    w_old = jnp.exp2(m_old - m_new)
                    w_blk = jnp.exp2(m_b - m_new)
                    acc = o_ref[qsl, :] * w_old + acc_b * w_blk
                    l_new = l_ref[qsl, :] * w_old + l_b * w_blk
                if last:
                    o_ref[qsl, :] = acc / l_new
                else:
                    o_ref[qsl, :] = acc
                    m_ref[qsl, :] = m_new
                    l_ref[qsl, :] = l_new
            return carry

        lax.fori_loop(0, S // (QB * ATT_UNROLL), body, 0)


def _attention(att_slab, B, S):
    T = B * S
    ncol = ATT_WIDTH // LANES
    return pl.pallas_call(
        functools.partial(_att_kernel, S=S),
        grid=(B, ncol),
        in_specs=[
            pl.BlockSpec((S, LANES), lambda b, h: (b, h)),
            pl.BlockSpec((S, LANES), lambda b, h: (b, ncol + h)),
            pl.BlockSpec((S, LANES), lambda b, h: (b, 2 * ncol + h)),
        ],
        out_specs=pl.BlockSpec((S, LANES), lambda b, h: (b, h)),
        out_shape=jax.ShapeDtypeStruct((T, ATT_WIDTH), F32),
        scratch_shapes=[pltpu.VMEM((S, LANES), F32), pltpu.VMEM((S, LANES), F32),
                        pltpu.VMEM((3, 2 * ATT_QB, ATT_KB), F32)],
        compiler_params=_cparams(("arbitrary", "arbitrary")),
        name="dilated_attention",
    )(att_slab, att_slab, att_slab)


ROW_TILE = D_MODEL // LANES


def _to_row_tiles(ref, x):
    n = x.shape[0]
    for j in range(ROW_TILE):
        ref[pl.ds(j, n, stride=ROW_TILE), :] = x[:, j * LANES:(j + 1) * LANES]


def _from_row_tiles(ref, n):
    return jnp.concatenate([ref[pl.ds(j, n, stride=ROW_TILE), :] for j in range(ROW_TILE)], axis=1)


def _tile_copy(src_ref, src_row, dst_ref, dst_row, sem):
    src = pl.ds(pl.multiple_of(src_row * ROW_TILE, ROW_TILE), ROW_TILE)
    dst = pl.ds(pl.multiple_of(dst_row * ROW_TILE, ROW_TILE), ROW_TILE)
    return pltpu.make_async_copy(src_ref.at[src], dst_ref.at[dst], sem)


def _outproj_kernel(of_ref, ob_ref, gg_ref, att_ref, x_ref, gnw_ref, wo1_ref, wo2_ref,
                    n2_ref, wr_ref, br_ref, h_ref, u_ref, lg_ref):
    o = of_ref[...] + ob_ref[...]
    gate = gg_ref[...]
    gnw = gnw_ref[...]
    parts = []
    for h in range(GLA_HEADS):
        sl = slice(h * GLA_DV, (h + 1) * GLA_DV)
        parts.append(_rms(o[:, sl], gnw))
    y = jnp.concatenate(parts, axis=1) * (gate / (1.0 + jnp.exp(-gate)))
    mix = _dot(y.astype(BF16), wo1_ref[...]) + _dot(att_ref[...].astype(BF16), wo2_ref[...])
    h = x_ref[...] + mix
    h_ref[...] = h
    u = _rms(h, n2_ref[...])
    _to_row_tiles(u_ref, u)
    lg_ref[...] = jnp.dot(u, wr_ref[...], preferred_element_type=F32,
                          precision=lax.Precision.HIGHEST) + br_ref[...]


def _outproj(o_f, o_b, gla_slab, att_out, x2, gla_norm_w, w_out, norm2_w, wr, br, tm=512):
    T = x2.shape[0]
    row = lambda i: (i, 0)
    const = lambda i: (0, 0)
    wo = w_out.astype(BF16)
    return pl.pallas_call(
        _outproj_kernel,
        grid=(T // tm,),
        in_specs=[
            pl.BlockSpec((tm, GLA_VAL_WIDTH), row),
            pl.BlockSpec((tm, GLA_VAL_WIDTH), row),
            pl.BlockSpec((tm, GLA_VAL_WIDTH), lambda i: (i, 2)),
            pl.BlockSpec((tm, ATT_WIDTH), row),
            pl.BlockSpec((tm, D_MODEL), row),
            pl.BlockSpec((1, GLA_DV), const),
            pl.BlockSpec((GLA_VAL_WIDTH, D_MODEL), const),
            pl.BlockSpec((ATT_WIDTH, D_MODEL), const),
            pl.BlockSpec((1, D_MODEL), const),
            pl.BlockSpec((D_MODEL, LANES), const),
            pl.BlockSpec((1, LANES), const),
        ],
        out_specs=[
            pl.BlockSpec((tm, D_MODEL), row),
            pl.BlockSpec((tm * ROW_TILE, LANES), row),
            pl.BlockSpec((tm, LANES), row),
        ],
        out_shape=[
            jax.ShapeDtypeStruct((T, D_MODEL), F32),
            jax.ShapeDtypeStruct((T * ROW_TILE, LANES), F32),
            jax.ShapeDtypeStruct((T, LANES), F32),
        ],
        compiler_params=_cparams(("arbitrary",)),
        name="outproj",
    )(o_f, o_b, gla_slab, att_out, x2, gla_norm_w[None, :], wo[:GLA_VAL_WIDTH], wo[GLA_VAL_WIDTH:],
      norm2_w[None, :], wr, br)


INFO_E1, INFO_E2, INFO_R1, INFO_R2, INFO_W1, INFO_W2 = range(6)


def _route_kernel(lg_ref, info_ref, cnt_ref, carry_ref):
    @pl.when(pl.program_id(0) == 0)
    def _():
        carry_ref[...] = jnp.zeros_like(carry_ref)

    lg = lg_ref[...]
    tr = lg.shape[0]
    lane = lax.broadcasted_iota(jnp.int32, (tr, LANES), 1)
    big = jnp.int32(1 << 20)
    is_g = (lane >= MOE_N_EXPERTS) & (lane < MOE_N_EXPERTS + MOE_GROUPS)
    gl = jnp.where(is_g, lg, -jnp.inf)
    gmax = jnp.max(gl, axis=-1, keepdims=True)
    gsel = jnp.min(jnp.where(gl == gmax, lane - MOE_N_EXPERTS, big), axis=-1, keepdims=True)
    g_w = 1.0 / jnp.sum(jnp.where(is_g, jnp.exp(lg - gmax), 0.0), axis=-1, keepdims=True)
    in_grp = (lane < MOE_N_EXPERTS) & ((lane >> 3) == gsel)
    el = jnp.where(in_grp, lg, -jnp.inf)
    v1 = jnp.max(el, axis=-1, keepdims=True)
    i1 = jnp.min(jnp.where(el == v1, lane, big), axis=-1, keepdims=True)
    el2 = jnp.where(lane == i1, -jnp.inf, el)
    v2 = jnp.max(el2, axis=-1, keepdims=True)
    i2 = jnp.min(jnp.where(el2 == v2, lane, big), axis=-1, keepdims=True)
    t = jnp.exp(v2 - v1)
    w1 = g_w * (1.0 / (1.0 + t))
    w2 = g_w * (t / (1.0 + t))

    hit1 = lane == i1
    hit2 = lane == i2
    member = jnp.where(hit1 | hit2, 1.0, 0.0)
    r = lax.broadcasted_iota(jnp.int32, (tr, tr), 0)
    c = lax.broadcasted_iota(jnp.int32, (tr, tr), 1)
    strict = jnp.where(c < r, 1.0, 0.0).astype(BF16)
    prefix = _dot(strict, member.astype(BF16)) + carry_ref[...]
    rank1 = jnp.sum(jnp.where(hit1, prefix, 0.0), axis=-1, keepdims=True)
    rank2 = jnp.sum(jnp.where(hit2, prefix, 0.0), axis=-1, keepdims=True)
    carry = carry_ref[...] + jnp.sum(member, axis=0, keepdims=True)
    carry_ref[...] = carry
    cnt_ref[...] = carry

    info = jnp.where(lane == INFO_E1, i1.astype(F32), 0.0)
    info = jnp.where(lane == INFO_E2, i2.astype(F32), info)
    info = jnp.where(lane == INFO_R1, rank1, info)
    info = jnp.where(lane == INFO_R2, rank2, info)
    info = jnp.where(lane == INFO_W1, w1, info)
    info = jnp.where(lane == INFO_W2, w2, info)
    info_ref[...] = info


def _route(logits, tr=512):
    T = logits.shape[0]
    return pl.pallas_call(
        _route_kernel,
        grid=(T // tr,),
        in_specs=[pl.BlockSpec((tr, LANES), lambda i: (i, 0))],
        out_specs=[pl.BlockSpec((tr, LANES), lambda i: (i, 0)), pl.BlockSpec((1, LANES), lambda i: (0, 0))],
        out_shape=[jax.ShapeDtypeStruct((T, LANES), F32), jax.ShapeDtypeStruct((1, LANES), F32)],
        scratch_shapes=[pltpu.VMEM((1, LANES), F32)],
        compiler_params=_cparams(("arbitrary",)),
        name="route",
    )(logits)


ROW_UNROLL = 8


def _dispatch_kernel(dest_ref, zblk_ref, u_ref, xs_ref, zbuf, sem, zsem, *, td):
    @pl.when(pl.program_id(0) == 0)
    def _():
        zbuf[...] = jnp.zeros_like(zbuf)

        def zero_copy(j):
            start = pl.multiple_of(zblk_ref[j] * (MOE_ROWS * ROW_TILE), MOE_ROWS * ROW_TILE)
            return pltpu.make_async_copy(zbuf, xs_ref.at[pl.ds(start, MOE_ROWS * ROW_TILE)], zsem)

        def start(j, carry):
            @pl.when(zblk_ref[j] >= 0)
            def _():
                zero_copy(j).start()
            return carry

        def wait(j, carry):
            @pl.when(zblk_ref[j] >= 0)
            def _():
                zero_copy(j).wait()
            return carry

        lax.fori_loop(0, 2 * MOE_N_EXPERTS, start, 0)
        lax.fori_loop(0, 2 * MOE_N_EXPERTS, wait, 0)

    base = pl.program_id(0) * (td * MOE_TOP_K)

    def issue(g, carry):
        for j in range(ROW_UNROLL):
            r = g * ROW_UNROLL + j
            for k in range(MOE_TOP_K):
                _tile_copy(u_ref, r, xs_ref, dest_ref[base + MOE_TOP_K * r + k], sem).start(priority=k)
        return carry

    lax.fori_loop(0, td // ROW_UNROLL, issue, 0)
    for k in range(MOE_TOP_K):
        pltpu.make_async_copy(u_ref, xs_ref.at[pl.ds(0, td * ROW_TILE)], sem).wait()


def _dispatch(dest, zero_blocks, u2, cap, td=256):
    T = u2.shape[0] // ROW_TILE
    return pl.pallas_call(
        functools.partial(_dispatch_kernel, td=td),
        grid_spec=pltpu.PrefetchScalarGridSpec(
            num_scalar_prefetch=2,
            grid=(T // td,),
            in_specs=[pl.BlockSpec((td * ROW_TILE, LANES), lambda i, d, z: (i, 0))],
            out_specs=pl.BlockSpec(memory_space=pl.ANY),
            scratch_shapes=[pltpu.VMEM((MOE_ROWS * ROW_TILE, LANES), F32),
                            pltpu.SemaphoreType.DMA(()), pltpu.SemaphoreType.DMA(())],
        ),
        out_shape=jax.ShapeDtypeStruct((cap * ROW_TILE, LANES), F32),
        compiler_params=_cparams(("arbitrary",)),
        name="dispatch",
    )(dest, zero_blocks, u2)


def _expert_kernel(be_ref, nu_ref, x_ref, wg_ref, wu_ref, wd_ref, y_ref, wgb, wub, wdb):
    b = pl.program_id(0)
    prev = be_ref[jnp.maximum(b - 1, 0)]
    fresh = (b == 0) | (be_ref[b] != prev)

    @pl.when(fresh)
    def _():
        wgb[...] = wg_ref[...].astype(BF16)
        wub[...] = wu_ref[...].astype(BF16)
        wdb[...] = wd_ref[...].astype(BF16)

    @pl.when(b < nu_ref[0])
    def _():
        xb = _from_row_tiles(x_ref, MOE_ROWS).astype(BF16)
        g = _dot(xb, wgb[...])
        u = _dot(xb, wub[...])
        hid = (g / (1.0 + jnp.exp(-g))) * u
        _to_row_tiles(y_ref, _dot(hid.astype(BF16), wdb[...]))

    @pl.when(b >= nu_ref[0])
    def _():
        y_ref[...] = jnp.zeros_like(y_ref)


def _experts(block_expert, n_used, xs, w_gate, w_up, w_down):
    cap = xs.shape[0] // ROW_TILE
    nblk = cap // MOE_ROWS
    rows = lambda b, be, nu: (jnp.minimum(b, nu[0] - 1), 0)
    wsel = lambda b, be, nu: (be[b], 0, 0)
    return pl.pallas_call(
        _expert_kernel,
        grid_spec=pltpu.PrefetchScalarGridSpec(
            num_scalar_prefetch=2,
            grid=(nblk,),
            in_specs=[
                pl.BlockSpec((MOE_ROWS * ROW_TILE, LANES), rows),
                pl.BlockSpec((None, D_MODEL, MOE_D_FF), wsel),
                pl.BlockSpec((None, D_MODEL, MOE_D_FF), wsel),
                pl.BlockSpec((None, MOE_D_FF, D_MODEL), wsel),
            ],
            out_specs=pl.BlockSpec((MOE_ROWS * ROW_TILE, LANES), lambda b, be, nu: (b, 0)),
            scratch_shapes=[pltpu.VMEM((D_MODEL, MOE_D_FF), BF16),
                            pltpu.VMEM((D_MODEL, MOE_D_FF), BF16),
                            pltpu.VMEM((MOE_D_FF, D_MODEL), BF16)],
        ),
        out_shape=jax.ShapeDtypeStruct((cap * ROW_TILE, LANES), F32),
        compiler_params=_cparams(("arbitrary",)),
        name="experts",
    )(block_expert, n_used, xs, w_gate, w_up, w_down)


def _combine_kernel(dest_ref, ys_ref, info_ref, h_ref, fw_ref, o_ref, buf, sem, *, tc):
    i = pl.program_id(0)
    n = pl.num_programs(0)

    def issue(step, slot):
        base = step * (tc * MOE_TOP_K)

        def body(g, carry):
            for j in range(ROW_UNROLL):
                r = g * ROW_UNROLL + j
                for k in range(MOE_TOP_K):
                    _tile_copy(ys_ref, dest_ref[base + MOE_TOP_K * r + k], buf.at[slot, k], r,
                               sem.at[slot]).start(priority=k)
            return carry

        lax.fori_loop(0, tc // ROW_UNROLL, body, 0)

    @pl.when(i == 0)
    def _():
        issue(0, 0)

    slot = i % 2

    @pl.when(i + 1 < n)
    def _():
        issue(i + 1, 1 - slot)

    for k in range(MOE_TOP_K):
        pltpu.make_async_copy(ys_ref.at[pl.ds(0, tc * ROW_TILE)], buf.at[slot, k], sem.at[slot]).wait()

    info = info_ref[...]
    lane = lax.broadcasted_iota(jnp.int32, info.shape, 1)
    w1 = jnp.sum(jnp.where(lane == INFO_W1, info, 0.0), axis=-1, keepdims=True)
    w2 = jnp.sum(jnp.where(lane == INFO_W2, info, 0.0), axis=-1, keepdims=True)
    y1 = _from_row_tiles(buf.at[slot, 0], tc)
    y2 = _from_row_tiles(buf.at[slot, 1], tc)
    h = h_ref[...] + (y1 * w1 + y2 * w2)
    o_ref[...] = _rms(h, fw_ref[...])


def _combine(dest, ys, info, h, final_w, tc=256):
    T = h.shape[0]
    return pl.pallas_call(
        functools.partial(_combine_kernel, tc=tc),
        grid_spec=pltpu.PrefetchScalarGridSpec(
            num_scalar_prefetch=1,
            grid=(T // tc,),
            in_specs=[pl.BlockSpec(memory_space=pl.ANY),
                      pl.BlockSpec((tc, LANES), lambda i, d: (i, 0)),
                      pl.BlockSpec((tc, D_MODEL), lambda i, d: (i, 0)),
                      pl.BlockSpec((1, D_MODEL), lambda i, d: (0, 0))],
            out_specs=pl.BlockSpec((tc, D_MODEL), lambda i, d: (i, 0)),
            scratch_shapes=[pltpu.VMEM((2, MOE_TOP_K, tc * ROW_TILE, LANES), F32),
                            pltpu.SemaphoreType.DMA((2,))],
        ),
        out_shape=jax.ShapeDtypeStruct((T, D_MODEL), F32),
        compiler_params=_cparams(("arbitrary",)),
        name="combine",
    )(dest, ys, info, h, final_w[None, :])


def _moe_plan(info, counts, T):
    nblk = -(-(T * MOE_TOP_K) // MOE_ROWS) + MOE_N_EXPERTS
    cnt = counts[0, :MOE_N_EXPERTS].astype(jnp.int32)
    padded = ((cnt + MOE_ROWS - 1) // MOE_ROWS) * MOE_ROWS
    pend = jnp.cumsum(padded)
    pstart = pend - padded
    e = info[:, INFO_E1:INFO_E2 + 1].astype(jnp.int32)
    rank = info[:, INFO_R1:INFO_R2 + 1].astype(jnp.int32)
    ids = jnp.arange(MOE_N_EXPERTS, dtype=jnp.int32)
    dest = (jnp.sum(jnp.where(e[:, :, None] == ids, pstart, 0), axis=-1) + rank).reshape(-1)
    n_used = pend[-1] // MOE_ROWS
    blocks = jnp.arange(nblk, dtype=jnp.int32)
    block_expert = jnp.minimum(jnp.sum(pend[None, :] <= (blocks * MOE_ROWS)[:, None], axis=1),
                               MOE_N_EXPERTS - 1).astype(jnp.int32)
    last_block = jnp.where(padded > 0, pend // MOE_ROWS - 1, -1)
    tail = n_used + ids
    zero_blocks = jnp.concatenate([last_block, jnp.where(tail < nblk, tail, -1)]).astype(jnp.int32)
    return dest, n_used.reshape(1), block_expert, zero_blocks, nblk * MOE_ROWS


def _router_weights(router_group_w, router_group_b, router_expert_w, router_expert_b):
    we = jnp.transpose(router_expert_w, (1, 0, 2)).reshape(D_MODEL, MOE_N_EXPERTS)
    wr = jnp.zeros((D_MODEL, LANES), F32)
    wr = wr.at[:, :MOE_N_EXPERTS].set(we).at[:, MOE_N_EXPERTS:MOE_N_EXPERTS + MOE_GROUPS].set(router_group_w)
    br = jnp.zeros((1, LANES), F32)
    br = br.at[0, :MOE_N_EXPERTS].set(router_expert_b.reshape(-1))
    br = br.at[0, MOE_N_EXPERTS:MOE_N_EXPERTS + MOE_GROUPS].set(router_group_b)
    return wr, br


def kernel(x, norm1_w, w_in, gla_fwd_gate_w, gla_fwd_gate_b, gla_bwd_gate_w, gla_bwd_gate_b,
           gla_norm_w, w_out, norm2_w, router_group_w, router_group_b, router_expert_w,
           router_expert_b, expert_w_gate, expert_w_up, expert_w_down, final_norm_w):
    B, S, D = x.shape
    T = B * S
    assert norm1_w.shape[0] == 1, "single-layer trunk: the final norm is fused into the combine step"
    h = x.reshape(T, D)
    gla_slab, loga, att_slab = _inproj(h, S, norm1_w[0], w_in[0], gla_fwd_gate_w[0], gla_fwd_gate_b[0],
                                       gla_bwd_gate_w[0], gla_bwd_gate_b[0])
    o_f, o_b = _gla(gla_slab, loga, B, S)
    att_out = _attention(att_slab, B, S)
    wr, br = _router_weights(router_group_w[0], router_group_b[0], router_expert_w[0], router_expert_b[0])
    h, u2, logits = _outproj(o_f, o_b, gla_slab, att_out, h, gla_norm_w[0], w_out[0], norm2_w[0], wr, br)
    info, counts = _route(logits)
    dest, n_used, block_expert, zero_blocks, cap = _moe_plan(info, counts, T)
    xs = _dispatch(dest, zero_blocks, u2, cap)
    ys = _experts(block_expert, n_used, xs, expert_w_gate[0], expert_w_up[0], expert_w_down[0])
    out = _combine(dest, ys, info, h, final_norm_w)
    return out.reshape(B, S, D)
```

```python
import functools

import jax
import jax.numpy as jnp
from jax import lax
from jax.experimental import pallas as pl
from jax.experimental.pallas import tpu as pltpu

F32 = jnp.float32
BF16 = jnp.bfloat16

D_MODEL = 1024
GLA_HEADS = 4
GLA_DV = 128
GLA_DK = 64
GLA_KEY_WIDTH = GLA_HEADS * GLA_DK
GLA_VAL_WIDTH = GLA_HEADS * GLA_DV
GLA_GATE_RANK = 16
GLA_TAU = 16.0
GLA_CHUNK = 64
ATT_WIDTH = 512
ATT_HEAD_DIM = 64
ATT_HEADS = 8
ROT_DIM = 16
ROPE_THETA = 500000.0
DILATED_PATTERNS = ((128, 1), (512, 4), (2048, 16))
ATT_RADIUS = 64
MOE_GROUPS = 4
MOE_EXPERTS_PER_GROUP = 8
MOE_N_EXPERTS = 32
MOE_TOP_K = 2
MOE_D_FF = 512
EPS = 1e-6
NEG_INF = -1e30
LOG2E = 1.4426950408889634

LANES = 128
MOE_ROWS = 256
VMEM_LIMIT = 56 * 1024 * 1024


def _cparams(sem):
    return pltpu.CompilerParams(dimension_semantics=sem, vmem_limit_bytes=VMEM_LIMIT)


def _dot(a, b):
    return jnp.dot(a, b, preferred_element_type=F32)


def _dot_nt(a, b):
    return lax.dot_general(a, b, (((1,), (1,)), ((), ())), preferred_element_type=F32)


def _dot_tn(a, b):
    return lax.dot_general(a, b, (((0,), (0,)), ((), ())), preferred_element_type=F32)


def _rms(x, w):
    return x * lax.rsqrt(jnp.mean(x * x, axis=-1, keepdims=True) + EPS) * w


def _inproj_kernel(x_ref, n1_ref, wg_ref, wlr_ref, wa_ref, gw_ref, gb_ref,
                   rc_ref, rs1_ref, rs2_ref, gla_ref, loga_ref, att_ref, stage_ref):
    x = x_ref[...]
    ub = _rms(x, n1_ref[...]).astype(BF16)
    g = _dot(ub, wg_ref[...])
    gla_ref[:, :GLA_KEY_WIDTH] = g[:, :GLA_KEY_WIDTH] * (GLA_DK ** -0.5)
    gla_ref[:, GLA_KEY_WIDTH:] = g[:, GLA_KEY_WIDTH:]
    lr = _dot(ub, wlr_ref[...])
    gate = _dot(lr.astype(BF16), gw_ref[...]) + gb_ref[...]
    loga_ref[...] = (jnp.minimum(gate, 0.0) - jnp.log(1.0 + jnp.exp(-jnp.abs(gate)))) * (1.0 / GLA_TAU)
    a = _dot(ub, wa_ref[...])
    qk = a[:, :2 * ATT_WIDTH]
    reps = 2 * ATT_WIDTH // LANES
    c = jnp.concatenate([rc_ref[...]] * reps, axis=1)
    s1 = jnp.concatenate([rs1_ref[...]] * reps, axis=1)
    s2 = jnp.concatenate([rs2_ref[...]] * reps, axis=1)
    half = ROT_DIM // 2
    n = 2 * ATT_WIDTH
    roped = qk * c + pltpu.roll(qk, n - half, 1) * s1 + pltpu.roll(qk, half, 1) * s2
    qkv = jnp.concatenate([roped[:, :ATT_WIDTH] * (ATT_HEAD_DIM ** -0.5 * LOG2E), roped[:, ATT_WIDTH:],
                           a[:, 2 * ATT_WIDTH:]], axis=1)
    rows = x.shape[0] // ATT_CLASSES
    for j in range(3 * ATT_WIDTH // LANES):
        cols = slice(j * LANES, (j + 1) * LANES)
        stage_ref[j] = qkv[:, cols]
        for c in range(ATT_CLASSES):
            att_ref[c, :, cols] = stage_ref[j, pl.ds(c, rows, stride=ATT_CLASSES), :]


def _rope_lane_tables(S):
    half = ROT_DIM // 2
    inv = ROPE_THETA ** (-(jnp.arange(0, ROT_DIM, 2, dtype=F32) / ROT_DIM))
    ang = jnp.arange(S, dtype=F32)[:, None] * inv[None, :]
    cos, sin = jnp.cos(ang), jnp.sin(ang)
    ones = jnp.ones((S, ATT_HEAD_DIM - ROT_DIM), F32)
    zeros8 = jnp.zeros((S, half), F32)
    zrest = jnp.zeros((S, ATT_HEAD_DIM - ROT_DIM), F32)
    c = jnp.concatenate([cos, cos, ones], axis=1)
    s1 = jnp.concatenate([-sin, zeros8, zrest], axis=1)
    s2 = jnp.concatenate([zeros8, sin, zrest], axis=1)
    rep = LANES // ATT_HEAD_DIM
    return (jnp.tile(c, (1, rep)), jnp.tile(s1, (1, rep)), jnp.tile(s2, (1, rep)))


def _inproj(x2, S, norm1_w, w_in, wf, bfw, wb, bbw, tm=512):
    T = x2.shape[0]
    o_lr = 2 * GLA_KEY_WIDTH + 2 * GLA_VAL_WIDTH
    o_att = o_lr + 2 * GLA_GATE_RANK
    wg = w_in[:, :o_lr].astype(BF16)
    wlr = jnp.zeros((D_MODEL, LANES), F32).at[:, :2 * GLA_GATE_RANK].set(w_in[:, o_lr:o_att]).astype(BF16)
    wa = w_in[:, o_att:].astype(BF16)
    gw = jnp.zeros((LANES, 2 * GLA_KEY_WIDTH), F32)
    gw = gw.at[:GLA_GATE_RANK, :GLA_KEY_WIDTH].set(wf)
    gw = gw.at[GLA_GATE_RANK:2 * GLA_GATE_RANK, GLA_KEY_WIDTH:].set(wb).astype(BF16)
    gb = jnp.concatenate([bfw, bbw])[None, :]
    rc, rs1, rs2 = _rope_lane_tables(S)
    nS = S // tm
    row = lambda i: (i, 0)
    const = lambda i: (0, 0)
    pos = lambda i: (i % nS, 0)
    return pl.pallas_call(
        _inproj_kernel,
        grid=(T // tm,),
        in_specs=[
            pl.BlockSpec((tm, D_MODEL), row),
            pl.BlockSpec((1, D_MODEL), const),
            pl.BlockSpec((D_MODEL, o_lr), const),
            pl.BlockSpec((D_MODEL, LANES), const),
            pl.BlockSpec((D_MODEL, 3 * ATT_WIDTH), const),
            pl.BlockSpec((LANES, 2 * GLA_KEY_WIDTH), const),
            pl.BlockSpec((1, 2 * GLA_KEY_WIDTH), const),
            pl.BlockSpec((tm, LANES), pos),
            pl.BlockSpec((tm, LANES), pos),
            pl.BlockSpec((tm, LANES), pos),
        ],
        out_specs=[
            pl.BlockSpec((tm, o_lr), row),
            pl.BlockSpec((tm, 2 * GLA_KEY_WIDTH), row),
            pl.BlockSpec((None, ATT_CLASSES, tm // ATT_CLASSES, 3 * ATT_WIDTH),
                         lambda i: (i // nS, 0, i % nS, 0)),
        ],
        out_shape=[
            jax.ShapeDtypeStruct((T, o_lr), F32),
            jax.ShapeDtypeStruct((T, 2 * GLA_KEY_WIDTH), F32),
            jax.ShapeDtypeStruct((T // S, ATT_CLASSES, S // ATT_CLASSES, 3 * ATT_WIDTH), F32),
        ],
        scratch_shapes=[pltpu.VMEM((3 * ATT_WIDTH // LANES, tm, LANES), F32)],
        compiler_params=_cparams(("arbitrary",)),
        name="inproj",
    )(x2, norm1_w[None, :], wg, wlr, wa, gw, gb, rc, rs1, rs2)


def _gla_direction(q, k, v, la, s_ref, o_ref, forward, G):
    C = GLA_CHUNK
    R = G * C
    r = lax.broadcasted_iota(jnp.int32, (R, R), 0)
    c = lax.broadcasted_iota(jnp.int32, (R, R), 1)
    same = (r >> 6) == (c >> 6)
    tri = (c <= r) if forward else (c >= r)
    t_mat = jnp.where(same, jnp.where(tri, 1.0, 0.0), 0.0)
    e_mat = jnp.where(same, 1.0, 0.0)
    te = jnp.concatenate([t_mat, e_mat], axis=0).astype(BF16)
    hi = la.astype(BF16)
    lo = (la - hi.astype(F32)).astype(BF16)
    bt = _dot(te, hi) + _dot(te, lo)
    b = bt[:R]
    tot = bt[R:]
    q_dec = (q * jnp.exp(b)).astype(BF16)
    k_inv = k * jnp.exp(-b)
    k_end = (k * jnp.exp(tot - b)).astype(BF16)
    dec = jnp.exp(tot)

    lane_k = lax.broadcasted_iota(jnp.int32, (C, GLA_KEY_WIDTH), 1)
    lane_v = lax.broadcasted_iota(jnp.int32, (C, GLA_VAL_WIDTH), 1)
    row_c = lax.broadcasted_iota(jnp.int32, (C, GLA_KEY_WIDTH), 0)
    col_in = lane_k & (C - 1)
    a_mask = (col_in <= row_c) if forward else (col_in >= row_c)
    srow = lax.broadcasted_iota(jnp.int32, (GLA_VAL_WIDTH, GLA_KEY_WIDTH), 0)
    scol = lax.broadcasted_iota(jnp.int32, (GLA_VAL_WIDTH, GLA_KEY_WIDTH), 1)
    s_mask = (srow >> 7) == (scol >> 6)

    order = range(G) if forward else range(G - 1, -1, -1)
    for g in order:
        rows = slice(g * C, (g + 1) * C)
        qd, ki, ke, vv = q_dec[rows], k_inv[rows], k_end[rows], v[rows]
        km = jnp.concatenate([jnp.where((lane_k >> 6) == h, ki, 0.0) for h in range(GLA_HEADS)], axis=0)
        a = _dot_nt(qd, km.astype(BF16))
        a = jnp.where(a_mask, a, 0.0).astype(BF16)
        vbd = jnp.concatenate([jnp.where((lane_v >> 7) == h, vv, 0.0) for h in range(GLA_HEADS)], axis=0)
        st = s_ref[...]
        o = _dot(a, vbd.astype(BF16)) + _dot_nt(qd, st.astype(BF16))
        o_ref[rows, :] = o
        kv = _dot_tn(vv.astype(BF16), ke)
        s_ref[...] = st * dec[g * C:g * C + 1, :] + jnp.where(s_mask, kv, 0.0)


def _gla_kernel(qf_ref, kf_ref, vf_ref, laf_ref, qb_ref, kb_ref, vb_ref, lab_ref,
                of_ref, ob_ref, sf_ref, sb_ref, *, G):
    @pl.when(pl.program_id(1) == 0)
    def _():
        sf_ref[...] = jnp.zeros_like(sf_ref)
        sb_ref[...] = jnp.zeros_like(sb_ref)

    _gla_direction(qf_ref[...], kf_ref[...], vf_ref[...], laf_ref[...], sf_ref, of_ref, True, G)
    _gla_direction(qb_ref[...], kb_ref[...], vb_ref[...], lab_ref[...], sb_ref, ob_ref, False, G)


def _gla(gla_slab, loga, B, S, G=4):
    T = B * S
    R = G * GLA_CHUNK
    ns = S // R
    fwd = lambda col: (lambda b, i: (b * ns + i, col))
    bwd = lambda col: (lambda b, i: (b * ns + ns - 1 - i, col))
    kw, vw = GLA_KEY_WIDTH, GLA_VAL_WIDTH
    return pl.pallas_call(
        functools.partial(_gla_kernel, G=G),
        grid=(B, ns),
        in_specs=[
            pl.BlockSpec((R, kw), fwd(0)), pl.BlockSpec((R, kw), fwd(1)),
            pl.BlockSpec((R, vw), fwd(1)), pl.BlockSpec((R, kw), fwd(0)),
            pl.BlockSpec((R, kw), bwd(0)), pl.BlockSpec((R, kw), bwd(1)),
            pl.BlockSpec((R, vw), bwd(1)), pl.BlockSpec((R, kw), bwd(1)),
        ],
        out_specs=[pl.BlockSpec((R, vw), fwd(0)), pl.BlockSpec((R, vw), bwd(0))],
        out_shape=[jax.ShapeDtypeStruct((T, vw), F32), jax.ShapeDtypeStruct((T, vw), F32)],
        scratch_shapes=[pltpu.VMEM((vw, kw), F32), pltpu.VMEM((vw, kw), F32)],
        compiler_params=_cparams(("arbitrary", "arbitrary")),
        name="gla",
    )(gla_slab, gla_slab, gla_slab, loga, gla_slab, gla_slab, gla_slab, loga)


ATT_CLASSES = 4
ATT_QB = 128
ATT_KB = ATT_QB + 2 * ATT_RADIUS


ATT_UNROLL = 4


def _att_kernel(q_ref, k_ref, v_ref, o_ref, m_ref, l_ref, bias_ref, *, S):
    QB, KB, NC = ATT_QB, ATT_KB, ATT_CLASSES
    L4 = S // NC
    lane = lax.broadcasted_iota(jnp.int32, (QB, LANES), 1)
    head0 = lane < ATT_HEAD_DIM

    @pl.when((pl.program_id(0) == 0) & (pl.program_id(1) == 0))
    def _():
        rowi = lax.broadcasted_iota(jnp.int32, (2 * QB, KB), 0) & (QB - 1)
        coli = lax.broadcasted_iota(jnp.int32, (2 * QB, KB), 1)
        qpos = (rowi & (QB // NC - 1)) * NC + (rowi >> 5)
        kpos = (coli & (KB // NC - 1)) * NC + (coli >> 6)
        for case in range(3):
            bias_ref[0, case] = jnp.where(jnp.abs(rowi - coli + case * ATT_RADIUS) <= ATT_RADIUS, 0.0, NEG_INF)
            bias_ref[1, case] = jnp.where(jnp.abs(qpos - kpos + case * ATT_RADIUS) <= ATT_RADIUS, 0.0, NEG_INF)

    for pi, (_, d) in enumerate(DILATED_PATTERNS):
        L = S // d
        nb = L // QB
        shift = nb.bit_length() - 1
        first = pi == 0
        last = pi == len(DILATED_PATTERNS) - 1

        def scores(n, d=d, L=L, nb=nb, shift=shift):
            cls = n >> shift
            q0 = (n & (nb - 1)) * QB
            ws = jnp.clip(q0 - ATT_RADIUS, 0, L - KB)
            if d == 1:
                qsls = [pl.ds(pl.multiple_of(c * L4 + q0 // NC, QB // NC), QB // NC) for c in range(NC)]
                ksls = [pl.ds(pl.multiple_of(c * L4 + ws // NC, ATT_RADIUS // NC), KB // NC) for c in range(NC)]
            elif d == NC:
                qsls = [pl.ds(pl.multiple_of(cls * L4 + q0, QB), QB)]
                ksls = [pl.ds(pl.multiple_of(cls * L4 + ws, ATT_RADIUS), KB)]
            else:
                base = (cls & (NC - 1)) * L4 + (cls >> 2)
                qsls = [pl.ds(base + NC * q0, QB, stride=NC)]
                ksls = [pl.ds(base + NC * ws, KB, stride=NC)]
            q = jnp.concatenate([q_ref[sl, :] for sl in qsls], axis=0)
            kw = jnp.concatenate([k_ref[sl, :] for sl in ksls], axis=0)
            q2 = jnp.concatenate([jnp.where(head0, q, 0.0), jnp.where(head0, 0.0, q)], axis=0).astype(BF16)
            s = _dot_nt(q2, kw.astype(BF16))
            return qsls, ksls, s + bias_ref[1 if d == 1 else 0, (q0 - ws) >> 6]

        def softmax_pv(qsls, ksls, s):
            m_blk = jnp.max(s, axis=-1, keepdims=True)
            p = jnp.exp2(s - m_blk)
            l_blk = jnp.sum(p, axis=-1, keepdims=True)
            vw = jnp.concatenate([v_ref[sl, :] for sl in ksls], axis=0)
            pv = _dot(p.astype(BF16), vw.astype(BF16))
            acc_b = jnp.where(head0, pv[:QB], pv[QB:])
            m_b = jnp.where(head0, m_blk[:QB], m_blk[QB:])
            l_b = jnp.where(head0, l_blk[:QB], l_blk[QB:])
            return qsls, acc_b, m_b, l_b

        def load(ref, sls):
            return jnp.concatenate([ref[sl, :] for sl in sls], axis=0)

        def store(ref, sls, val):
            n = val.shape[0] // len(sls)
            for i, sl in enumerate(sls):
                ref[sl, :] = val[i * n:(i + 1) * n]

        def body(n, carry, first=first, last=last):
            staged = [scores(n * ATT_UNROLL + u) for u in range(ATT_UNROLL)]
            blocks = [softmax_pv(*st) for st in staged]
            for qsls, acc_b, m_b, l_b in blocks:
                if first:
                    acc, m_new, l_new = acc_b, m_b, l_b
                else:
                    m_old = load(m_ref, qsls)
                    m_new = jnp.maximum(m_old, m_b)
                    w_old = jnp.exp2(m_old - m_new)
                    w_blk = jnp.exp2(m_b - m_new)
                    acc = load(o_ref, qsls) * w_old + acc_b * w_blk
                    l_new = load(l_ref, qsls) * w_old + l_b * w_blk
                if last:
                    store(o_ref, qsls, acc / l_new)
                else:
                    store(o_ref, qsls, acc)
                    store(m_ref, qsls, m_new)
                    store(l_ref, qsls, l_new)
            return carry

        lax.fori_loop(0, S // (QB * ATT_UNROLL), body, 0)


def _attention(att_slab, B, S):
    T = B * S
    ncol = ATT_WIDTH // LANES
    return pl.pallas_call(
        functools.partial(_att_kernel, S=S),
        grid=(B, ncol),
        in_specs=[
            pl.BlockSpec((S, LANES), lambda b, h: (b, h)),
            pl.BlockSpec((S, LANES), lambda b, h: (b, ncol + h)),
            pl.BlockSpec((S, LANES), lambda b, h: (b, 2 * ncol + h)),
        ],
        out_specs=pl.BlockSpec((S, LANES), lambda b, h: (b, h)),
        out_shape=jax.ShapeDtypeStruct((T, ATT_WIDTH), F32),
        scratch_shapes=[pltpu.VMEM((S, LANES), F32), pltpu.VMEM((S, LANES), F32),
                        pltpu.VMEM((2, 3, 2 * ATT_QB, ATT_KB), F32)],
        compiler_params=_cparams(("arbitrary", "arbitrary")),
        name="dilated_attention",
    )(att_slab, att_slab, att_slab)


ROW_TILE = D_MODEL // LANES


def _to_row_tiles(ref, x):
    n = x.shape[0]
    for j in range(ROW_TILE):
        ref[pl.ds(j, n, stride=ROW_TILE), :] = x[:, j * LANES:(j + 1) * LANES]


def _from_row_tiles(ref, n):
    return jnp.concatenate([ref[pl.ds(j, n, stride=ROW_TILE), :] for j in range(ROW_TILE)], axis=1)


def _tile_copy(src_ref, src_row, dst_ref, dst_row, sem):
    src = pl.ds(pl.multiple_of(src_row * ROW_TILE, ROW_TILE), ROW_TILE)
    dst = pl.ds(pl.multiple_of(dst_row * ROW_TILE, ROW_TILE), ROW_TILE)
    return pltpu.make_async_copy(src_ref.at[src], dst_ref.at[dst], sem)


def _outproj_kernel(of_ref, ob_ref, gg_ref, att_ref, x_ref, gnw_ref, wo1_ref, wo2_ref,
                    n2_ref, wr_ref, br_ref, h_ref, u_ref, lg_ref, stage_ref):
    rows = stage_ref.shape[1] // ATT_CLASSES
    for j in range(ATT_WIDTH // LANES):
        for c in range(ATT_CLASSES):
            stage_ref[j, pl.ds(c, rows, stride=ATT_CLASSES), :] = att_ref[c, :, j * LANES:(j + 1) * LANES]
    att = jnp.concatenate([stage_ref[j] for j in range(ATT_WIDTH // LANES)], axis=1)
    o = of_ref[...] + ob_ref[...]
    gate = gg_ref[...]
    gnw = gnw_ref[...]
    parts = []
    for h in range(GLA_HEADS):
        sl = slice(h * GLA_DV, (h + 1) * GLA_DV)
        parts.append(_rms(o[:, sl], gnw))
    y = jnp.concatenate(parts, axis=1) * (gate / (1.0 + jnp.exp(-gate)))
    mix = _dot(y.astype(BF16), wo1_ref[...]) + _dot(att.astype(BF16), wo2_ref[...])
    h = x_ref[...] + mix
    h_ref[...] = h
    u = _rms(h, n2_ref[...])
    _to_row_tiles(u_ref, u)
    lg_ref[...] = jnp.dot(u, wr_ref[...], preferred_element_type=F32,
                          precision=lax.Precision.HIGHEST) + br_ref[...]


def _outproj(o_f, o_b, gla_slab, att_out, x2, gla_norm_w, w_out, norm2_w, wr, br, tm=512):
    T = x2.shape[0]
    nS = att_out.shape[2] * ATT_CLASSES // tm
    row = lambda i: (i, 0)
    const = lambda i: (0, 0)
    wo = w_out.astype(BF16)
    return pl.pallas_call(
        _outproj_kernel,
        grid=(T // tm,),
        in_specs=[
            pl.BlockSpec((tm, GLA_VAL_WIDTH), row),
            pl.BlockSpec((tm, GLA_VAL_WIDTH), row),
            pl.BlockSpec((tm, GLA_VAL_WIDTH), lambda i: (i, 2)),
            pl.BlockSpec((None, ATT_CLASSES, tm // ATT_CLASSES, ATT_WIDTH), lambda i: (i // nS, 0, i % nS, 0)),
            pl.BlockSpec((tm, D_MODEL), row),
            pl.BlockSpec((1, GLA_DV), const),
            pl.BlockSpec((GLA_VAL_WIDTH, D_MODEL), const),
            pl.BlockSpec((ATT_WIDTH, D_MODEL), const),
            pl.BlockSpec((1, D_MODEL), const),
            pl.BlockSpec((D_MODEL, LANES), const),
            pl.BlockSpec((1, LANES), const),
        ],
        out_specs=[
            pl.BlockSpec((tm, D_MODEL), row),
            pl.BlockSpec((tm * ROW_TILE, LANES), row),
            pl.BlockSpec((tm, LANES), row),
        ],
        out_shape=[
            jax.ShapeDtypeStruct((T, D_MODEL), F32),
            jax.ShapeDtypeStruct((T * ROW_TILE, LANES), F32),
            jax.ShapeDtypeStruct((T, LANES), F32),
        ],
        scratch_shapes=[pltpu.VMEM((ATT_WIDTH // LANES, tm, LANES), F32)],
        compiler_params=_cparams(("arbitrary",)),
        name="outproj",
    )(o_f, o_b, gla_slab, att_out, x2, gla_norm_w[None, :], wo[:GLA_VAL_WIDTH], wo[GLA_VAL_WIDTH:],
      norm2_w[None, :], wr, br)


INFO_E1, INFO_E2, INFO_R1, INFO_R2, INFO_W1, INFO_W2 = range(6)


def _route_kernel(lg_ref, info_ref, cnt_ref, carry_ref):
    @pl.when(pl.program_id(0) == 0)
    def _():
        carry_ref[...] = jnp.zeros_like(carry_ref)

    lg = lg_ref[...]
    tr = lg.shape[0]
    lane = lax.broadcasted_iota(jnp.int32, (tr, LANES), 1)
    big = jnp.int32(1 << 20)
    is_g = (lane >= MOE_N_EXPERTS) & (lane < MOE_N_EXPERTS + MOE_GROUPS)
    gl = jnp.where(is_g, lg, -jnp.inf)
    gmax = jnp.max(gl, axis=-1, keepdims=True)
    gsel = jnp.min(jnp.where(gl == gmax, lane - MOE_N_EXPERTS, big), axis=-1, keepdims=True)
    g_w = 1.0 / jnp.sum(jnp.where(is_g, jnp.exp(lg - gmax), 0.0), axis=-1, keepdims=True)
    in_grp = (lane < MOE_N_EXPERTS) & ((lane >> 3) == gsel)
    el = jnp.where(in_grp, lg, -jnp.inf)
    v1 = jnp.max(el, axis=-1, keepdims=True)
    i1 = jnp.min(jnp.where(el == v1, lane, big), axis=-1, keepdims=True)
    el2 = jnp.where(lane == i1, -jnp.inf, el)
    v2 = jnp.max(el2, axis=-1, keepdims=True)
    i2 = jnp.min(jnp.where(el2 == v2, lane, big), axis=-1, keepdims=True)
    t = jnp.exp(v2 - v1)
    w1 = g_w * (1.0 / (1.0 + t))
    w2 = g_w * (t / (1.0 + t))

    hit1 = lane == i1
    hit2 = lane == i2
    member = jnp.where(hit1 | hit2, 1.0, 0.0)
    r = lax.broadcasted_iota(jnp.int32, (tr, tr), 0)
    c = lax.broadcasted_iota(jnp.int32, (tr, tr), 1)
    strict = jnp.where(c < r, 1.0, 0.0).astype(BF16)
    prefix = _dot(strict, member.astype(BF16)) + carry_ref[...]
    rank1 = jnp.sum(jnp.where(hit1, prefix, 0.0), axis=-1, keepdims=True)
    rank2 = jnp.sum(jnp.where(hit2, prefix, 0.0), axis=-1, keepdims=True)
    carry = carry_ref[...] + jnp.sum(member, axis=0, keepdims=True)
    carry_ref[...] = carry
    cnt_ref[...] = carry

    info = jnp.where(lane == INFO_E1, i1.astype(F32), 0.0)
    info = jnp.where(lane == INFO_E2, i2.astype(F32), info)
    info = jnp.where(lane == INFO_R1, rank1, info)
    info = jnp.where(lane == INFO_R2, rank2, info)
    info = jnp.where(lane == INFO_W1, w1, info)
    info = jnp.where(lane == INFO_W2, w2, info)
    info_ref[...] = info


def _route(logits, tr=512):
    T = logits.shape[0]
    return pl.pallas_call(
        _route_kernel,
        grid=(T // tr,),
        in_specs=[pl.BlockSpec((tr, LANES), lambda i: (i, 0))],
        out_specs=[pl.BlockSpec((tr, LANES), lambda i: (i, 0)), pl.BlockSpec((1, LANES), lambda i: (0, 0))],
        out_shape=[jax.ShapeDtypeStruct((T, LANES), F32), jax.ShapeDtypeStruct((1, LANES), F32)],
        scratch_shapes=[pltpu.VMEM((1, LANES), F32)],
        compiler_params=_cparams(("arbitrary",)),
        name="route",
    )(logits)


ROW_UNROLL = 8


def _dispatch_kernel(dest_ref, zblk_ref, u_ref, xs_ref, zbuf, sem, zsem, *, td):
    @pl.when(pl.program_id(0) == 0)
    def _():
        zbuf[...] = jnp.zeros_like(zbuf)

        def zero_copy(j):
            start = pl.multiple_of(zblk_ref[j] * (MOE_ROWS * ROW_TILE), MOE_ROWS * ROW_TILE)
            return pltpu.make_async_copy(zbuf, xs_ref.at[pl.ds(start, MOE_ROWS * ROW_TILE)], zsem)

        def start(j, carry):
            @pl.when(zblk_ref[j] >= 0)
            def _():
                zero_copy(j).start()
            return carry

        def wait(j, carry):
            @pl.when(zblk_ref[j] >= 0)
            def _():
                zero_copy(j).wait()
            return carry

        lax.fori_loop(0, 2 * MOE_N_EXPERTS, start, 0)
        lax.fori_loop(0, 2 * MOE_N_EXPERTS, wait, 0)

    base = pl.program_id(0) * (td * MOE_TOP_K)

    def issue(g, carry):
        for j in range(ROW_UNROLL):
            r = g * ROW_UNROLL + j
            for k in range(MOE_TOP_K):
                _tile_copy(u_ref, r, xs_ref, dest_ref[base + MOE_TOP_K * r + k], sem).start(priority=k)
        return carry

    lax.fori_loop(0, td // ROW_UNROLL, issue, 0)
    for k in range(MOE_TOP_K):
        pltpu.make_async_copy(u_ref, xs_ref.at[pl.ds(0, td * ROW_TILE)], sem).wait()


def _dispatch(dest, zero_blocks, u2, cap, td=256):
    T = u2.shape[0] // ROW_TILE
    return pl.pallas_call(
        functools.partial(_dispatch_kernel, td=td),
        grid_spec=pltpu.PrefetchScalarGridSpec(
            num_scalar_prefetch=2,
            grid=(T // td,),
            in_specs=[pl.BlockSpec((td * ROW_TILE, LANES), lambda i, d, z: (i, 0))],
            out_specs=pl.BlockSpec(memory_space=pl.ANY),
            scratch_shapes=[pltpu.VMEM((MOE_ROWS * ROW_TILE, LANES), F32),
                            pltpu.SemaphoreType.DMA(()), pltpu.SemaphoreType.DMA(())],
        ),
        out_shape=jax.ShapeDtypeStruct((cap * ROW_TILE, LANES), F32),
        compiler_params=_cparams(("arbitrary",)),
        name="dispatch",
    )(dest, zero_blocks, u2)


def _expert_kernel(be_ref, nu_ref, x_ref, wg_ref, wu_ref, wd_ref, y_ref, wgb, wub, wdb):
    b = pl.program_id(0)
    prev = be_ref[jnp.maximum(b - 1, 0)]
    fresh = (b == 0) | (be_ref[b] != prev)

    @pl.when(fresh)
    def _():
        wgb[...] = wg_ref[...].astype(BF16)
        wub[...] = wu_ref[...].astype(BF16)
        wdb[...] = wd_ref[...].astype(BF16)

    @pl.when(b < nu_ref[0])
    def _():
        xb = _from_row_tiles(x_ref, MOE_ROWS).astype(BF16)
        g = _dot(xb, wgb[...])
        u = _dot(xb, wub[...])
        hid = (g / (1.0 + jnp.exp(-g))) * u
        _to_row_tiles(y_ref, _dot(hid.astype(BF16), wdb[...]))

    @pl.when(b >= nu_ref[0])
    def _():
        y_ref[...] = jnp.zeros_like(y_ref)


def _experts(block_expert, n_used, xs, w_gate, w_up, w_down):
    cap = xs.shape[0] // ROW_TILE
    nblk = cap // MOE_ROWS
    rows = lambda b, be, nu: (jnp.minimum(b, nu[0] - 1), 0)
    wsel = lambda b, be, nu: (be[b], 0, 0)
    return pl.pallas_call(
        _expert_kernel,
        grid_spec=pltpu.PrefetchScalarGridSpec(
            num_scalar_prefetch=2,
            grid=(nblk,),
            in_specs=[
                pl.BlockSpec((MOE_ROWS * ROW_TILE, LANES), rows),
                pl.BlockSpec((None, D_MODEL, MOE_D_FF), wsel),
                pl.BlockSpec((None, D_MODEL, MOE_D_FF), wsel),
                pl.BlockSpec((None, MOE_D_FF, D_MODEL), wsel),
            ],
            out_specs=pl.BlockSpec((MOE_ROWS * ROW_TILE, LANES), lambda b, be, nu: (b, 0)),
            scratch_shapes=[pltpu.VMEM((D_MODEL, MOE_D_FF), BF16),
                            pltpu.VMEM((D_MODEL, MOE_D_FF), BF16),
                            pltpu.VMEM((MOE_D_FF, D_MODEL), BF16)],
        ),
        out_shape=jax.ShapeDtypeStruct((cap * ROW_TILE, LANES), F32),
        compiler_params=_cparams(("arbitrary",)),
        name="experts",
    )(block_expert, n_used, xs, w_gate, w_up, w_down)


def _combine_kernel(dest_ref, ys_ref, info_ref, h_ref, fw_ref, o_ref, buf, sem, *, tc):
    i = pl.program_id(0)
    n = pl.num_programs(0)

    def issue(step, slot):
        base = step * (tc * MOE_TOP_K)

        def body(g, carry):
            for j in range(ROW_UNROLL):
                r = g * ROW_UNROLL + j
                for k in range(MOE_TOP_K):
                    _tile_copy(ys_ref, dest_ref[base + MOE_TOP_K * r + k], buf.at[slot, k], r,
                               sem.at[slot]).start(priority=k)
            return carry

        lax.fori_loop(0, tc // ROW_UNROLL, body, 0)

    @pl.when(i == 0)
    def _():
        issue(0, 0)

    slot = i % 2

    @pl.when(i + 1 < n)
    def _():
        issue(i + 1, 1 - slot)

    for k in range(MOE_TOP_K):
        pltpu.make_async_copy(ys_ref.at[pl.ds(0, tc * ROW_TILE)], buf.at[slot, k], sem.at[slot]).wait()

    info = info_ref[...]
    lane = lax.broadcasted_iota(jnp.int32, info.shape, 1)
    w1 = jnp.sum(jnp.where(lane == INFO_W1, info, 0.0), axis=-1, keepdims=True)
    w2 = jnp.sum(jnp.where(lane == INFO_W2, info, 0.0), axis=-1, keepdims=True)
    y1 = _from_row_tiles(buf.at[slot, 0], tc)
    y2 = _from_row_tiles(buf.at[slot, 1], tc)
    h = h_ref[...] + (y1 * w1 + y2 * w2)
    o_ref[...] = _rms(h, fw_ref[...])


def _combine(dest, ys, info, h, final_w, tc=256):
    T = h.shape[0]
    return pl.pallas_call(
        functools.partial(_combine_kernel, tc=tc),
        grid_spec=pltpu.PrefetchScalarGridSpec(
            num_scalar_prefetch=1,
            grid=(T // tc,),
            in_specs=[pl.BlockSpec(memory_space=pl.ANY),
                      pl.BlockSpec((tc, LANES), lambda i, d: (i, 0)),
                      pl.BlockSpec((tc, D_MODEL), lambda i, d: (i, 0)),
                      pl.BlockSpec((1, D_MODEL), lambda i, d: (0, 0))],
            out_specs=pl.BlockSpec((tc, D_MODEL), lambda i, d: (i, 0)),
            scratch_shapes=[pltpu.VMEM((2, MOE_TOP_K, tc * ROW_TILE, LANES), F32),
                            pltpu.SemaphoreType.DMA((2,))],
        ),
        out_shape=jax.ShapeDtypeStruct((T, D_MODEL), F32),
        compiler_params=_cparams(("arbitrary",)),
        name="combine",
    )(dest, ys, info, h, final_w[None, :])


def _moe_plan(info, counts, T):
    nblk = -(-(T * MOE_TOP_K) // MOE_ROWS) + MOE_N_EXPERTS
    cnt = counts[0, :MOE_N_EXPERTS].astype(jnp.int32)
    padded = ((cnt + MOE_ROWS - 1) // MOE_ROWS) * MOE_ROWS
    pend = jnp.cumsum(padded)
    pstart = pend - padded
    e = info[:, INFO_E1:INFO_E2 + 1].astype(jnp.int32)
    rank = info[:, INFO_R1:INFO_R2 + 1].astype(jnp.int32)
    ids = jnp.arange(MOE_N_EXPERTS, dtype=jnp.int32)
    dest = (jnp.sum(jnp.where(e[:, :, None] == ids, pstart, 0), axis=-1) + rank).reshape(-1)
    n_used = pend[-1] // MOE_ROWS
    blocks = jnp.arange(nblk, dtype=jnp.int32)
    block_expert = jnp.minimum(jnp.sum(pend[None, :] <= (blocks * MOE_ROWS)[:, None], axis=1),
                               MOE_N_EXPERTS - 1).astype(jnp.int32)
    last_block = jnp.where(padded > 0, pend // MOE_ROWS - 1, -1)
    tail = n_used + ids
    zero_blocks = jnp.concatenate([last_block, jnp.where(tail < nblk, tail, -1)]).astype(jnp.int32)
    return dest, n_used.reshape(1), block_expert, zero_blocks, nblk * MOE_ROWS


def _router_weights(router_group_w, router_group_b, router_expert_w, router_expert_b):
    we = jnp.transpose(router_expert_w, (1, 0, 2)).reshape(D_MODEL, MOE_N_EXPERTS)
    wr = jnp.zeros((D_MODEL, LANES), F32)
    wr = wr.at[:, :MOE_N_EXPERTS].set(we).at[:, MOE_N_EXPERTS:MOE_N_EXPERTS + MOE_GROUPS].set(router_group_w)
    br = jnp.zeros((1, LANES), F32)
    br = br.at[0, :MOE_N_EXPERTS].set(router_expert_b.reshape(-1))
    br = br.at[0, MOE_N_EXPERTS:MOE_N_EXPERTS + MOE_GROUPS].set(router_group_b)
    return wr, br


def kernel(x, norm1_w, w_in, gla_fwd_gate_w, gla_fwd_gate_b, gla_bwd_gate_w, gla_bwd_gate_b,
           gla_norm_w, w_out, norm2_w, router_group_w, router_group_b, router_expert_w,
           router_expert_b, expert_w_gate, expert_w_up, expert_w_down, final_norm_w):
    B, S, D = x.shape
    T = B * S
    assert norm1_w.shape[0] == 1, "single-layer trunk: the final norm is fused into the combine step"
    h = x.reshape(T, D)
    gla_slab, loga, att_slab = _inproj(h, S, norm1_w[0], w_in[0], gla_fwd_gate_w[0], gla_fwd_gate_b[0],
                                       gla_bwd_gate_w[0], gla_bwd_gate_b[0])
    o_f, o_b = _gla(gla_slab, loga, B, S)
    att_out = _attention(att_slab.reshape(T, 3 * ATT_WIDTH), B, S)
    att_out = att_out.reshape(B, ATT_CLASSES, S // ATT_CLASSES, ATT_WIDTH)
    wr, br = _router_weights(router_group_w[0], router_group_b[0], router_expert_w[0], router_expert_b[0])
    h, u2, logits = _outproj(o_f, o_b, gla_slab, att_out, h, gla_norm_w[0], w_out[0], norm2_w[0], wr, br)
    info, counts = _route(logits)
    dest, n_used, block_expert, zero_blocks, cap = _moe_plan(info, counts, T)
    xs = _dispatch(dest, zero_blocks, u2, cap)
    ys = _experts(block_expert, n_used, xs, expert_w_gate[0], expert_w_up[0], expert_w_down[0])
    out = _combine(dest, ys, info, h, final_norm_w)
    return out.reshape(B, S, D)
```

```python
import functools

import jax
import jax.numpy as jnp
from jax import lax
from jax.experimental import pallas as pl
from jax.experimental.pallas import tpu as pltpu

F32 = jnp.float32
BF16 = jnp.bfloat16

D_MODEL = 1024
GLA_HEADS = 4
GLA_DV = 128
GLA_DK = 64
GLA_KEY_WIDTH = GLA_HEADS * GLA_DK
GLA_VAL_WIDTH = GLA_HEADS * GLA_DV
GLA_GATE_RANK = 16
GLA_TAU = 16.0
GLA_CHUNK = 64
ATT_WIDTH = 512
ATT_HEAD_DIM = 64
ATT_HEADS = 8
ROT_DIM = 16
ROPE_THETA = 500000.0
DILATED_PATTERNS = ((128, 1), (512, 4), (2048, 16))
ATT_RADIUS = 64
MOE_GROUPS = 4
MOE_EXPERTS_PER_GROUP = 8
MOE_N_EXPERTS = 32
MOE_TOP_K = 2
MOE_D_FF = 512
EPS = 1e-6
NEG_INF = -1e30
LOG2E = 1.4426950408889634

LANES = 128
MOE_ROWS = 256
VMEM_LIMIT = 56 * 1024 * 1024


def _cparams(sem):
    return pltpu.CompilerParams(dimension_semantics=sem, vmem_limit_bytes=VMEM_LIMIT)


def _dot(a, b):
    return jnp.dot(a, b, preferred_element_type=F32)


def _dot_nt(a, b):
    return lax.dot_general(a, b, (((1,), (1,)), ((), ())), preferred_element_type=F32)


def _dot_tn(a, b):
    return lax.dot_general(a, b, (((0,), (0,)), ((), ())), preferred_element_type=F32)


def _rms(x, w):
    return x * lax.rsqrt(jnp.mean(x * x, axis=-1, keepdims=True) + EPS) * w


def _inproj_kernel(x_ref, n1_ref, wg_ref, wlr_ref, wa_ref, gw_ref, gb_ref,
                   rc_ref, rs1_ref, rs2_ref, gla_ref, loga_ref, att_ref, stage_ref):
    x = x_ref[...]
    ub = _rms(x, n1_ref[...]).astype(BF16)
    g = _dot(ub, wg_ref[...])
    gla_ref[:, :GLA_KEY_WIDTH] = g[:, :GLA_KEY_WIDTH] * (GLA_DK ** -0.5)
    gla_ref[:, GLA_KEY_WIDTH:] = g[:, GLA_KEY_WIDTH:]
    lr = _dot(ub, wlr_ref[...])
    gate = _dot(lr.astype(BF16), gw_ref[...]) + gb_ref[...]
    loga_ref[...] = (jnp.minimum(gate, 0.0) - jnp.log(1.0 + jnp.exp(-jnp.abs(gate)))) * (1.0 / GLA_TAU)
    a = _dot(ub, wa_ref[...])
    qk = a[:, :2 * ATT_WIDTH]
    reps = 2 * ATT_WIDTH // LANES
    c = jnp.concatenate([rc_ref[...]] * reps, axis=1)
    s1 = jnp.concatenate([rs1_ref[...]] * reps, axis=1)
    s2 = jnp.concatenate([rs2_ref[...]] * reps, axis=1)
    half = ROT_DIM // 2
    n = 2 * ATT_WIDTH
    roped = qk * c + pltpu.roll(qk, n - half, 1) * s1 + pltpu.roll(qk, half, 1) * s2
    qkv = jnp.concatenate([roped[:, :ATT_WIDTH] * (ATT_HEAD_DIM ** -0.5 * LOG2E), roped[:, ATT_WIDTH:],
                           a[:, 2 * ATT_WIDTH:]], axis=1)
    rows = x.shape[0] // ATT_CLASSES
    for j in range(3 * ATT_WIDTH // LANES):
        cols = slice(j * LANES, (j + 1) * LANES)
        stage_ref[j] = qkv[:, cols]
        for c in range(ATT_CLASSES):
            att_ref[c, :, cols] = stage_ref[j, pl.ds(c, rows, stride=ATT_CLASSES), :]


def _rope_lane_tables(S):
    half = ROT_DIM // 2
    inv = ROPE_THETA ** (-(jnp.arange(0, ROT_DIM, 2, dtype=F32) / ROT_DIM))
    ang = jnp.arange(S, dtype=F32)[:, None] * inv[None, :]
    cos, sin = jnp.cos(ang), jnp.sin(ang)
    ones = jnp.ones((S, ATT_HEAD_DIM - ROT_DIM), F32)
    zeros8 = jnp.zeros((S, half), F32)
    zrest = jnp.zeros((S, ATT_HEAD_DIM - ROT_DIM), F32)
    c = jnp.concatenate([cos, cos, ones], axis=1)
    s1 = jnp.concatenate([-sin, zeros8, zrest], axis=1)
    s2 = jnp.concatenate([zeros8, sin, zrest], axis=1)
    rep = LANES // ATT_HEAD_DIM
    return (jnp.tile(c, (1, rep)), jnp.tile(s1, (1, rep)), jnp.tile(s2, (1, rep)))


def _inproj(x2, S, norm1_w, w_in, wf, bfw, wb, bbw, tm=512):
    T = x2.shape[0]
    o_lr = 2 * GLA_KEY_WIDTH + 2 * GLA_VAL_WIDTH
    o_att = o_lr + 2 * GLA_GATE_RANK
    wg = w_in[:, :o_lr].astype(BF16)
    wlr = jnp.zeros((D_MODEL, LANES), F32).at[:, :2 * GLA_GATE_RANK].set(w_in[:, o_lr:o_att]).astype(BF16)
    wa = w_in[:, o_att:].astype(BF16)
    gw = jnp.zeros((LANES, 2 * GLA_KEY_WIDTH), F32)
    gw = gw.at[:GLA_GATE_RANK, :GLA_KEY_WIDTH].set(wf)
    gw = gw.at[GLA_GATE_RANK:2 * GLA_GATE_RANK, GLA_KEY_WIDTH:].set(wb).astype(BF16)
    gb = jnp.concatenate([bfw, bbw])[None, :]
    rc, rs1, rs2 = _rope_lane_tables(S)
    nS = S // tm
    row = lambda i: (i, 0)
    const = lambda i: (0, 0)
    pos = lambda i: (i % nS, 0)
    return pl.pallas_call(
        _inproj_kernel,
        grid=(T // tm,),
        in_specs=[
            pl.BlockSpec((tm, D_MODEL), row),
            pl.BlockSpec((1, D_MODEL), const),
            pl.BlockSpec((D_MODEL, o_lr), const),
            pl.BlockSpec((D_MODEL, LANES), const),
            pl.BlockSpec((D_MODEL, 3 * ATT_WIDTH), const),
            pl.BlockSpec((LANES, 2 * GLA_KEY_WIDTH), const),
            pl.BlockSpec((1, 2 * GLA_KEY_WIDTH), const),
            pl.BlockSpec((tm, LANES), pos),
            pl.BlockSpec((tm, LANES), pos),
            pl.BlockSpec((tm, LANES), pos),
        ],
        out_specs=[
            pl.BlockSpec((tm, o_lr), row),
            pl.BlockSpec((tm, 2 * GLA_KEY_WIDTH), row),
            pl.BlockSpec((None, ATT_CLASSES, tm // ATT_CLASSES, 3 * ATT_WIDTH),
                         lambda i: (i // nS, 0, i % nS, 0)),
        ],
        out_shape=[
            jax.ShapeDtypeStruct((T, o_lr), F32),
            jax.ShapeDtypeStruct((T, 2 * GLA_KEY_WIDTH), F32),
            jax.ShapeDtypeStruct((T // S, ATT_CLASSES, S // ATT_CLASSES, 3 * ATT_WIDTH), F32),
        ],
        scratch_shapes=[pltpu.VMEM((3 * ATT_WIDTH // LANES, tm, LANES), F32)],
        compiler_params=_cparams(("arbitrary",)),
        name="inproj",
    )(x2, norm1_w[None, :], wg, wlr, wa, gw, gb, rc, rs1, rs2)


def _gla_direction(q, k, v, la, s_ref, o_ref, forward, G):
    C = GLA_CHUNK
    R = G * C
    r = lax.broadcasted_iota(jnp.int32, (R, R), 0)
    c = lax.broadcasted_iota(jnp.int32, (R, R), 1)
    same = (r >> 6) == (c >> 6)
    tri = (c <= r) if forward else (c >= r)
    t_mat = jnp.where(same, jnp.where(tri, 1.0, 0.0), 0.0).astype(BF16)
    hi = la.astype(BF16)
    lo = (la - hi.astype(F32)).astype(BF16)
    b = _dot(t_mat, hi) + _dot(t_mat, lo)
    edge = C - 1 if forward else 0
    tot = jnp.concatenate([jnp.broadcast_to(b[g * C + edge:g * C + edge + 1], (C, GLA_KEY_WIDTH))
                           for g in range(G)], axis=0)
    q_dec = (q * jnp.exp(b)).astype(BF16)
    k_inv = k * jnp.exp(-b)
    k_end = (k * jnp.exp(tot - b)).astype(BF16)
    dec = jnp.exp(tot)

    lane_k = lax.broadcasted_iota(jnp.int32, (C, GLA_KEY_WIDTH), 1)
    lane_v = lax.broadcasted_iota(jnp.int32, (C, GLA_VAL_WIDTH), 1)
    row_c = lax.broadcasted_iota(jnp.int32, (C, GLA_KEY_WIDTH), 0)
    col_in = lane_k & (C - 1)
    a_mask = (col_in <= row_c) if forward else (col_in >= row_c)
    srow = lax.broadcasted_iota(jnp.int32, (GLA_VAL_WIDTH, GLA_KEY_WIDTH), 0)
    scol = lax.broadcasted_iota(jnp.int32, (GLA_VAL_WIDTH, GLA_KEY_WIDTH), 1)
    s_mask = (srow >> 7) == (scol >> 6)

    order = range(G) if forward else range(G - 1, -1, -1)
    for g in order:
        rows = slice(g * C, (g + 1) * C)
        qd, ki, ke, vv = q_dec[rows], k_inv[rows], k_end[rows], v[rows]
        km = jnp.concatenate([jnp.where((lane_k >> 6) == h, ki, 0.0) for h in range(GLA_HEADS)], axis=0)
        a = _dot_nt(qd, km.astype(BF16))
        a = jnp.where(a_mask, a, 0.0).astype(BF16)
        vbd = jnp.concatenate([jnp.where((lane_v >> 7) == h, vv, 0.0) for h in range(GLA_HEADS)], axis=0)
        st = s_ref[...]
        o = _dot(a, vbd.astype(BF16)) + _dot_nt(qd, st.astype(BF16))
        o_ref[rows, :] = o
        kv = _dot_tn(vv.astype(BF16), ke)
        s_ref[...] = st * dec[g * C:g * C + 1, :] + jnp.where(s_mask, kv, 0.0)


def _gla_kernel(qf_ref, kf_ref, vf_ref, laf_ref, qb_ref, kb_ref, vb_ref, lab_ref,
                of_ref, ob_ref, sf_ref, sb_ref, *, G):
    @pl.when(pl.program_id(1) == 0)
    def _():
        sf_ref[...] = jnp.zeros_like(sf_ref)
        sb_ref[...] = jnp.zeros_like(sb_ref)

    _gla_direction(qf_ref[...], kf_ref[...], vf_ref[...], laf_ref[...], sf_ref, of_ref, True, G)
    _gla_direction(qb_ref[...], kb_ref[...], vb_ref[...], lab_ref[...], sb_ref, ob_ref, False, G)


def _gla(gla_slab, loga, B, S, G=4):
    T = B * S
    R = G * GLA_CHUNK
    ns = S // R
    fwd = lambda col: (lambda b, i: (b * ns + i, col))
    bwd = lambda col: (lambda b, i: (b * ns + ns - 1 - i, col))
    kw, vw = GLA_KEY_WIDTH, GLA_VAL_WIDTH
    return pl.pallas_call(
        functools.partial(_gla_kernel, G=G),
        grid=(B, ns),
        in_specs=[
            pl.BlockSpec((R, kw), fwd(0)), pl.BlockSpec((R, kw), fwd(1)),
            pl.BlockSpec((R, vw), fwd(1)), pl.BlockSpec((R, kw), fwd(0)),
            pl.BlockSpec((R, kw), bwd(0)), pl.BlockSpec((R, kw), bwd(1)),
            pl.BlockSpec((R, vw), bwd(1)), pl.BlockSpec((R, kw), bwd(1)),
        ],
        out_specs=[pl.BlockSpec((R, vw), fwd(0)), pl.BlockSpec((R, vw), bwd(0))],
        out_shape=[jax.ShapeDtypeStruct((T, vw), F32), jax.ShapeDtypeStruct((T, vw), F32)],
        scratch_shapes=[pltpu.VMEM((vw, kw), F32), pltpu.VMEM((vw, kw), F32)],
        compiler_params=_cparams(("arbitrary", "arbitrary")),
        name="gla",
    )(gla_slab, gla_slab, gla_slab, loga, gla_slab, gla_slab, gla_slab, loga)


ATT_CLASSES = 4
ATT_QB = 128
ATT_KB = ATT_QB + 2 * ATT_RADIUS


ATT_UNROLL = 4


def _att_kernel(q_ref, k_ref, v_ref, o_ref, m_ref, l_ref, bias_ref, *, S):
    QB, KB, NC = ATT_QB, ATT_KB, ATT_CLASSES
    L4 = S // NC
    lane = lax.broadcasted_iota(jnp.int32, (QB, LANES), 1)
    head0 = lane < ATT_HEAD_DIM

    @pl.when((pl.program_id(0) == 0) & (pl.program_id(1) == 0))
    def _():
        rowi = lax.broadcasted_iota(jnp.int32, (2 * QB, KB), 0) & (QB - 1)
        coli = lax.broadcasted_iota(jnp.int32, (2 * QB, KB), 1)
        qpos = (rowi & (QB // NC - 1)) * NC + (rowi >> 5)
        kpos = (coli & (KB // NC - 1)) * NC + (coli >> 6)
        for case in range(3):
            bias_ref[0, case] = jnp.where(jnp.abs(rowi - coli + case * ATT_RADIUS) <= ATT_RADIUS, 0.0, NEG_INF)
            bias_ref[1, case] = jnp.where(jnp.abs(qpos - kpos + case * ATT_RADIUS) <= ATT_RADIUS, 0.0, NEG_INF)

    for pi, (_, d) in enumerate(DILATED_PATTERNS):
        L = S // d
        nb = L // QB
        shift = nb.bit_length() - 1
        first = pi == 0
        last = pi == len(DILATED_PATTERNS) - 1

        def scores(n, d=d, L=L, nb=nb, shift=shift):
            cls = n >> shift
            q0 = (n & (nb - 1)) * QB
            ws = jnp.clip(q0 - ATT_RADIUS, 0, L - KB)
            if d == 1:
                qsls = [pl.ds(pl.multiple_of(c * L4 + q0 // NC, QB // NC), QB // NC) for c in range(NC)]
                ksls = [pl.ds(pl.multiple_of(c * L4 + ws // NC, ATT_RADIUS // NC), KB // NC) for c in range(NC)]
            elif d == NC:
                qsls = [pl.ds(pl.multiple_of(cls * L4 + q0, QB), QB)]
                ksls = [pl.ds(pl.multiple_of(cls * L4 + ws, ATT_RADIUS), KB)]
            else:
                base = (cls & (NC - 1)) * L4 + (cls >> 2)
                qsls = [pl.ds(base + NC * q0, QB, stride=NC)]
                ksls = [pl.ds(base + NC * ws, KB, stride=NC)]
            q = jnp.concatenate([q_ref[sl, :] for sl in qsls], axis=0)
            kw = jnp.concatenate([k_ref[sl, :] for sl in ksls], axis=0)
            q2 = jnp.concatenate([jnp.where(head0, q, 0.0), jnp.where(head0, 0.0, q)], axis=0).astype(BF16)
            s = _dot_nt(q2, kw.astype(BF16))
            return qsls, ksls, s + bias_ref[1 if d == 1 else 0, (q0 - ws) >> 6]

        def softmax_pv(qsls, ksls, s):
            m_blk = jnp.max(s, axis=-1, keepdims=True)
            p = jnp.exp2(s - m_blk)
            l_blk = jnp.sum(p, axis=-1, keepdims=True)
            vw = jnp.concatenate([v_ref[sl, :] for sl in ksls], axis=0)
            pv = _dot(p.astype(BF16), vw.astype(BF16))
            acc_b = jnp.where(head0, pv[:QB], pv[QB:])
            m_b = jnp.where(head0, m_blk[:QB], m_blk[QB:])
            l_b = jnp.where(head0, l_blk[:QB], l_blk[QB:])
            return qsls, acc_b, m_b, l_b

        def load(ref, sls):
            return jnp.concatenate([ref[sl, :] for sl in sls], axis=0)

        def store(ref, sls, val):
            n = val.shape[0] // len(sls)
            for i, sl in enumerate(sls):
                ref[sl, :] = val[i * n:(i + 1) * n]

        def body(n, carry, first=first, last=last):
            staged = [scores(n * ATT_UNROLL + u) for u in range(ATT_UNROLL)]
            blocks = [softmax_pv(*st) for st in staged]
            for qsls, acc_b, m_b, l_b in blocks:
                if first:
                    acc, m_new, l_new = acc_b, m_b, l_b
                else:
                    m_old = load(m_ref, qsls)
                    m_new = jnp.maximum(m_old, m_b)
                    w_old = jnp.exp2(m_old - m_new)
                    w_blk = jnp.exp2(m_b - m_new)
                    acc = load(o_ref, qsls) * w_old + acc_b * w_blk
                    l_new = load(l_ref, qsls) * w_old + l_b * w_blk
                if last:
                    store(o_ref, qsls, acc / l_new)
                else:
                    store(o_ref, qsls, acc)
                    store(m_ref, qsls, m_new)
                    store(l_ref, qsls, l_new)
            return carry

        lax.fori_loop(0, S // (QB * ATT_UNROLL), body, 0)


def _attention(att_slab, B, S):
    T = B * S
    ncol = ATT_WIDTH // LANES
    return pl.pallas_call(
        functools.partial(_att_kernel, S=S),
        grid=(B, ncol),
        in_specs=[
            pl.BlockSpec((S, LANES), lambda b, h: (b, h)),
            pl.BlockSpec((S, LANES), lambda b, h: (b, ncol + h)),
            pl.BlockSpec((S, LANES), lambda b, h: (b, 2 * ncol + h)),
        ],
        out_specs=pl.BlockSpec((S, LANES), lambda b, h: (b, h)),
        out_shape=jax.ShapeDtypeStruct((T, ATT_WIDTH), F32),
        scratch_shapes=[pltpu.VMEM((S, LANES), F32), pltpu.VMEM((S, LANES), F32),
                        pltpu.VMEM((2, 3, 2 * ATT_QB, ATT_KB), F32)],
        compiler_params=_cparams(("arbitrary", "arbitrary")),
        name="dilated_attention",
    )(att_slab, att_slab, att_slab)


ROW_TILE = D_MODEL // LANES


def _to_row_tiles(ref, x):
    n = x.shape[0]
    for j in range(ROW_TILE):
        ref[pl.ds(j, n, stride=ROW_TILE), :] = x[:, j * LANES:(j + 1) * LANES]


def _from_row_tiles(ref, n):
    return jnp.concatenate([ref[pl.ds(j, n, stride=ROW_TILE), :] for j in range(ROW_TILE)], axis=1)


def _tile_copy(src_ref, src_row, dst_ref, dst_row, sem):
    src = pl.ds(pl.multiple_of(src_row * ROW_TILE, ROW_TILE), ROW_TILE)
    dst = pl.ds(pl.multiple_of(dst_row * ROW_TILE, ROW_TILE), ROW_TILE)
    return pltpu.make_async_copy(src_ref.at[src], dst_ref.at[dst], sem)


def _outproj_kernel(of_ref, ob_ref, gg_ref, att_ref, x_ref, gnw_ref, wo1_ref, wo2_ref,
                    n2_ref, wr_ref, br_ref, h_ref, u_ref, lg_ref, stage_ref):
    rows = stage_ref.shape[1] // ATT_CLASSES
    for j in range(ATT_WIDTH // LANES):
        for c in range(ATT_CLASSES):
            stage_ref[j, pl.ds(c, rows, stride=ATT_CLASSES), :] = att_ref[c, :, j * LANES:(j + 1) * LANES]
    att = jnp.concatenate([stage_ref[j] for j in range(ATT_WIDTH // LANES)], axis=1)
    o = of_ref[...] + ob_ref[...]
    gate = gg_ref[...]
    gnw = gnw_ref[...]
    parts = []
    for h in range(GLA_HEADS):
        sl = slice(h * GLA_DV, (h + 1) * GLA_DV)
        parts.append(_rms(o[:, sl], gnw))
    y = jnp.concatenate(parts, axis=1) * (gate / (1.0 + jnp.exp(-gate)))
    mix = _dot(y.astype(BF16), wo1_ref[...]) + _dot(att.astype(BF16), wo2_ref[...])
    h = x_ref[...] + mix
    h_ref[...] = h
    u = _rms(h, n2_ref[...])
    _to_row_tiles(u_ref, u)
    u_hi = u.astype(BF16)
    u_lo = (u - u_hi.astype(F32)).astype(BF16)
    hi_both = _dot(u_hi, wr_ref[...])
    lg_ref[...] = (hi_both[:, :LANES] + hi_both[:, LANES:] + _dot(u_lo, wr_ref[:, :LANES])) + br_ref[...]


def _outproj(o_f, o_b, gla_slab, att_out, x2, gla_norm_w, w_out, norm2_w, wr, br, tm=512):
    T = x2.shape[0]
    nS = att_out.shape[2] * ATT_CLASSES // tm
    row = lambda i: (i, 0)
    const = lambda i: (0, 0)
    wo = w_out.astype(BF16)
    wr_hi = wr.astype(BF16)
    wr_lo = (wr - wr_hi.astype(F32)).astype(BF16)
    wr = jnp.concatenate([wr_hi, wr_lo], axis=1)
    return pl.pallas_call(
        _outproj_kernel,
        grid=(T // tm,),
        in_specs=[
            pl.BlockSpec((tm, GLA_VAL_WIDTH), row),
            pl.BlockSpec((tm, GLA_VAL_WIDTH), row),
            pl.BlockSpec((tm, GLA_VAL_WIDTH), lambda i: (i, 2)),
            pl.BlockSpec((None, ATT_CLASSES, tm // ATT_CLASSES, ATT_WIDTH), lambda i: (i // nS, 0, i % nS, 0)),
            pl.BlockSpec((tm, D_MODEL), row),
            pl.BlockSpec((1, GLA_DV), const),
            pl.BlockSpec((GLA_VAL_WIDTH, D_MODEL), const),
            pl.BlockSpec((ATT_WIDTH, D_MODEL), const),
            pl.BlockSpec((1, D_MODEL), const),
            pl.BlockSpec((D_MODEL, 2 * LANES), const),
            pl.BlockSpec((1, LANES), const),
        ],
        out_specs=[
            pl.BlockSpec((tm, D_MODEL), row),
            pl.BlockSpec((tm * ROW_TILE, LANES), row),
            pl.BlockSpec((tm, LANES), row),
        ],
        out_shape=[
            jax.ShapeDtypeStruct((T, D_MODEL), F32),
            jax.ShapeDtypeStruct((T * ROW_TILE, LANES), F32),
            jax.ShapeDtypeStruct((T, LANES), F32),
        ],
        scratch_shapes=[pltpu.VMEM((ATT_WIDTH // LANES, tm, LANES), F32)],
        compiler_params=_cparams(("arbitrary",)),
        name="outproj",
    )(o_f, o_b, gla_slab, att_out, x2, gla_norm_w[None, :], wo[:GLA_VAL_WIDTH], wo[GLA_VAL_WIDTH:],
      norm2_w[None, :], wr, br)


INFO_E1, INFO_E2, INFO_R1, INFO_R2, INFO_W1, INFO_W2 = range(6)


def _route_kernel(lg_ref, info_ref, cnt_ref, carry_ref):
    @pl.when(pl.program_id(0) == 0)
    def _():
        carry_ref[...] = jnp.zeros_like(carry_ref)

    lg = lg_ref[...]
    tr = lg.shape[0]
    lane = lax.broadcasted_iota(jnp.int32, (tr, LANES), 1)
    big = jnp.int32(1 << 20)
    is_g = (lane >= MOE_N_EXPERTS) & (lane < MOE_N_EXPERTS + MOE_GROUPS)
    gl = jnp.where(is_g, lg, -jnp.inf)
    gmax = jnp.max(gl, axis=-1, keepdims=True)
    gsel = jnp.min(jnp.where(gl == gmax, lane - MOE_N_EXPERTS, big), axis=-1, keepdims=True)
    g_w = 1.0 / jnp.sum(jnp.where(is_g, jnp.exp(lg - gmax), 0.0), axis=-1, keepdims=True)
    in_grp = (lane < MOE_N_EXPERTS) & ((lane >> 3) == gsel)
    el = jnp.where(in_grp, lg, -jnp.inf)
    v1 = jnp.max(el, axis=-1, keepdims=True)
    i1 = jnp.min(jnp.where(el == v1, lane, big), axis=-1, keepdims=True)
    el2 = jnp.where(lane == i1, -jnp.inf, el)
    v2 = jnp.max(el2, axis=-1, keepdims=True)
    i2 = jnp.min(jnp.where(el2 == v2, lane, big), axis=-1, keepdims=True)
    t = jnp.exp(v2 - v1)
    w1 = g_w * (1.0 / (1.0 + t))
    w2 = g_w * (t / (1.0 + t))

    hit1 = lane == i1
    hit2 = lane == i2
    member = jnp.where(hit1 | hit2, 1.0, 0.0)
    r = lax.broadcasted_iota(jnp.int32, (tr, tr), 0)
    c = lax.broadcasted_iota(jnp.int32, (tr, tr), 1)
    strict = jnp.where(c < r, 1.0, 0.0).astype(BF16)
    prefix = _dot(strict, member.astype(BF16)) + carry_ref[...]
    rank1 = jnp.sum(jnp.where(hit1, prefix, 0.0), axis=-1, keepdims=True)
    rank2 = jnp.sum(jnp.where(hit2, prefix, 0.0), axis=-1, keepdims=True)
    carry = carry_ref[...] + jnp.sum(member, axis=0, keepdims=True)
    carry_ref[...] = carry
    cnt_ref[...] = carry

    info = jnp.where(lane == INFO_E1, i1.astype(F32), 0.0)
    info = jnp.where(lane == INFO_E2, i2.astype(F32), info)
    info = jnp.where(lane == INFO_R1, rank1, info)
    info = jnp.where(lane == INFO_R2, rank2, info)
    info = jnp.where(lane == INFO_W1, w1, info)
    info = jnp.where(lane == INFO_W2, w2, info)
    info_ref[...] = info


def _route(logits, tr=512):
    T = logits.shape[0]
    return pl.pallas_call(
        _route_kernel,
        grid=(T // tr,),
        in_specs=[pl.BlockSpec((tr, LANES), lambda i: (i, 0))],
        out_specs=[pl.BlockSpec((tr, LANES), lambda i: (i, 0)), pl.BlockSpec((1, LANES), lambda i: (0, 0))],
        out_shape=[jax.ShapeDtypeStruct((T, LANES), F32), jax.ShapeDtypeStruct((1, LANES), F32)],
        scratch_shapes=[pltpu.VMEM((1, LANES), F32)],
        compiler_params=_cparams(("arbitrary",)),
        name="route",
    )(logits)


ROW_UNROLL = 8


def _dispatch_kernel(dest_ref, zblk_ref, u_ref, xs_ref, zbuf, sem, zsem, *, td):
    @pl.when(pl.program_id(0) == 0)
    def _():
        zbuf[...] = jnp.zeros_like(zbuf)

        def zero_copy(j):
            start = pl.multiple_of(zblk_ref[j] * (MOE_ROWS * ROW_TILE), MOE_ROWS * ROW_TILE)
            return pltpu.make_async_copy(zbuf, xs_ref.at[pl.ds(start, MOE_ROWS * ROW_TILE)], zsem)

        def start(j, carry):
            @pl.when(zblk_ref[j] >= 0)
            def _():
                zero_copy(j).start()
            return carry

        def wait(j, carry):
            @pl.when(zblk_ref[j] >= 0)
            def _():
                zero_copy(j).wait()
            return carry

        lax.fori_loop(0, 2 * MOE_N_EXPERTS, start, 0)
        lax.fori_loop(0, 2 * MOE_N_EXPERTS, wait, 0)

    base = pl.program_id(0) * (td * MOE_TOP_K)

    def issue(g, carry):
        for j in range(ROW_UNROLL):
            r = g * ROW_UNROLL + j
            for k in range(MOE_TOP_K):
                _tile_copy(u_ref, r, xs_ref, dest_ref[base + MOE_TOP_K * r + k], sem).start(priority=k)
        return carry

    lax.fori_loop(0, td // ROW_UNROLL, issue, 0)
    for k in range(MOE_TOP_K):
        pltpu.make_async_copy(u_ref, xs_ref.at[pl.ds(0, td * ROW_TILE)], sem).wait()


def _dispatch(dest, zero_blocks, u2, cap, td=256):
    T = u2.shape[0] // ROW_TILE
    return pl.pallas_call(
        functools.partial(_dispatch_kernel, td=td),
        grid_spec=pltpu.PrefetchScalarGridSpec(
            num_scalar_prefetch=2,
            grid=(T // td,),
            in_specs=[pl.BlockSpec((td * ROW_TILE, LANES), lambda i, d, z: (i, 0))],
            out_specs=pl.BlockSpec(memory_space=pl.ANY),
            scratch_shapes=[pltpu.VMEM((MOE_ROWS * ROW_TILE, LANES), F32),
                            pltpu.SemaphoreType.DMA(()), pltpu.SemaphoreType.DMA(())],
        ),
        out_shape=jax.ShapeDtypeStruct((cap * ROW_TILE, LANES), F32),
        compiler_params=_cparams(("arbitrary",)),
        name="dispatch",
    )(dest, zero_blocks, u2)


def _expert_kernel(be_ref, nxt_ref, nu_ref, x_ref, wg_hbm, wu_hbm, wd_hbm, y_ref,
                   stage_g, stage_u, stage_d, wgb, wub, wdb, sem):
    b = pl.program_id(0)
    live = b < nu_ref[0]
    fresh = live & ((b == 0) | (be_ref[b] != be_ref[jnp.maximum(b - 1, 0)]))

    def weight_copies(e):
        return (pltpu.make_async_copy(wg_hbm.at[e], stage_g, sem.at[0]),
                pltpu.make_async_copy(wu_hbm.at[e], stage_u, sem.at[1]),
                pltpu.make_async_copy(wd_hbm.at[e], stage_d, sem.at[2]))

    @pl.when(b == 0)
    def _():
        for c in weight_copies(be_ref[0]):
            c.start()

    @pl.when(fresh)
    def _():
        for c in weight_copies(be_ref[b]):
            c.wait()
        wgb[...] = stage_g[...].astype(BF16)
        wub[...] = stage_u[...].astype(BF16)
        wdb[...] = stage_d[...].astype(BF16)

        @pl.when(nxt_ref[b] >= 0)
        def _():
            for c in weight_copies(nxt_ref[b]):
                c.start()

    @pl.when(live)
    def _():
        xb = _from_row_tiles(x_ref, MOE_ROWS).astype(BF16)
        g = _dot(xb, wgb[...])
        u = _dot(xb, wub[...])
        hid = (g / (1.0 + jnp.exp(-g))) * u
        _to_row_tiles(y_ref, _dot(hid.astype(BF16), wdb[...]))

    @pl.when(jnp.logical_not(live))
    def _():
        y_ref[...] = jnp.zeros_like(y_ref)


def _experts(block_expert, next_expert, n_used, xs, w_gate, w_up, w_down):
    cap = xs.shape[0] // ROW_TILE
    nblk = cap // MOE_ROWS
    rows = lambda b, be, nx, nu: (jnp.minimum(b, nu[0] - 1), 0)
    return pl.pallas_call(
        _expert_kernel,
        grid_spec=pltpu.PrefetchScalarGridSpec(
            num_scalar_prefetch=3,
            grid=(nblk,),
            in_specs=[
                pl.BlockSpec((MOE_ROWS * ROW_TILE, LANES), rows),
                pl.BlockSpec(memory_space=pl.ANY),
                pl.BlockSpec(memory_space=pl.ANY),
                pl.BlockSpec(memory_space=pl.ANY),
            ],
            out_specs=pl.BlockSpec((MOE_ROWS * ROW_TILE, LANES), lambda b, be, nx, nu: (b, 0)),
            scratch_shapes=[pltpu.VMEM((D_MODEL, MOE_D_FF), F32),
                            pltpu.VMEM((D_MODEL, MOE_D_FF), F32),
                            pltpu.VMEM((MOE_D_FF, D_MODEL), F32),
                            pltpu.VMEM((D_MODEL, MOE_D_FF), BF16),
                            pltpu.VMEM((D_MODEL, MOE_D_FF), BF16),
                            pltpu.VMEM((MOE_D_FF, D_MODEL), BF16),
                            pltpu.SemaphoreType.DMA((3,))],
        ),
        out_shape=jax.ShapeDtypeStruct((cap * ROW_TILE, LANES), F32),
        compiler_params=_cparams(("arbitrary",)),
        name="experts",
    )(block_expert, next_expert, n_used, xs, w_gate, w_up, w_down)


def _combine_kernel(dest_ref, ys_ref, info_ref, h_ref, fw_ref, o_ref, buf, sem, *, tc):
    i = pl.program_id(0)
    n = pl.num_programs(0)

    def issue(step, slot):
        base = step * (tc * MOE_TOP_K)

        def body(g, carry):
            for j in range(ROW_UNROLL):
                r = g * ROW_UNROLL + j
                for k in range(MOE_TOP_K):
                    _tile_copy(ys_ref, dest_ref[base + MOE_TOP_K * r + k], buf.at[slot, k], r,
                               sem.at[slot]).start(priority=k)
            return carry

        lax.fori_loop(0, tc // ROW_UNROLL, body, 0)

    @pl.when(i == 0)
    def _():
        issue(0, 0)

    slot = i % 2

    @pl.when(i + 1 < n)
    def _():
        issue(i + 1, 1 - slot)

    for k in range(MOE_TOP_K):
        pltpu.make_async_copy(ys_ref.at[pl.ds(0, tc * ROW_TILE)], buf.at[slot, k], sem.at[slot]).wait()

    info = info_ref[...]
    lane = lax.broadcasted_iota(jnp.int32, info.shape, 1)
    w1 = jnp.sum(jnp.where(lane == INFO_W1, info, 0.0), axis=-1, keepdims=True)
    w2 = jnp.sum(jnp.where(lane == INFO_W2, info, 0.0), axis=-1, keepdims=True)
    y1 = _from_row_tiles(buf.at[slot, 0], tc)
    y2 = _from_row_tiles(buf.at[slot, 1], tc)
    h = h_ref[...] + (y1 * w1 + y2 * w2)
    o_ref[...] = _rms(h, fw_ref[...])


def _combine(dest, ys, info, h, final_w, tc=256):
    T = h.shape[0]
    return pl.pallas_call(
        functools.partial(_combine_kernel, tc=tc),
        grid_spec=pltpu.PrefetchScalarGridSpec(
            num_scalar_prefetch=1,
            grid=(T // tc,),
            in_specs=[pl.BlockSpec(memory_space=pl.ANY),
                      pl.BlockSpec((tc, LANES), lambda i, d: (i, 0)),
                      pl.BlockSpec((tc, D_MODEL), lambda i, d: (i, 0)),
                      pl.BlockSpec((1, D_MODEL), lambda i, d: (0, 0))],
            out_specs=pl.BlockSpec((tc, D_MODEL), lambda i, d: (i, 0)),
            scratch_shapes=[pltpu.VMEM((2, MOE_TOP_K, tc * ROW_TILE, LANES), F32),
                            pltpu.SemaphoreType.DMA((2,))],
        ),
        out_shape=jax.ShapeDtypeStruct((T, D_MODEL), F32),
        compiler_params=_cparams(("arbitrary",)),
        name="combine",
    )(dest, ys, info, h, final_w[None, :])


def _moe_plan(info, counts, T):
    nblk = -(-(T * MOE_TOP_K) // MOE_ROWS) + MOE_N_EXPERTS
    cnt = counts[0, :MOE_N_EXPERTS].astype(jnp.int32)
    padded = ((cnt + MOE_ROWS - 1) // MOE_ROWS) * MOE_ROWS
    pend = jnp.cumsum(padded)
    pstart = pend - padded
    e = info[:, INFO_E1:INFO_E2 + 1].astype(jnp.int32)
    rank = info[:, INFO_R1:INFO_R2 + 1].astype(jnp.int32)
    ids = jnp.arange(MOE_N_EXPERTS, dtype=jnp.int32)
    dest = (jnp.sum(jnp.where(e[:, :, None] == ids, pstart, 0), axis=-1) + rank).reshape(-1)
    n_used = pend[-1] // MOE_ROWS
    blocks = jnp.arange(nblk, dtype=jnp.int32)
    block_expert = jnp.minimum(jnp.sum(pend[None, :] <= (blocks * MOE_ROWS)[:, None], axis=1),
                               MOE_N_EXPERTS - 1).astype(jnp.int32)
    later = jnp.where((ids[None, :] > ids[:, None]) & (padded[None, :] > 0), ids[None, :], MOE_N_EXPERTS)
    next_active = jnp.min(later, axis=1)
    next_active = jnp.where(next_active < MOE_N_EXPERTS, next_active, -1)
    next_expert = jnp.sum(jnp.where(block_expert[:, None] == ids, next_active, 0), axis=1).astype(jnp.int32)
    last_block = jnp.where(padded > 0, pend // MOE_ROWS - 1, -1)
    tail = n_used + ids
    zero_blocks = jnp.concatenate([last_block, jnp.where(tail < nblk, tail, -1)]).astype(jnp.int32)
    return dest, n_used.reshape(1), block_expert, next_expert, zero_blocks, nblk * MOE_ROWS


def _router_weights(router_group_w, router_group_b, router_expert_w, router_expert_b):
    we = jnp.transpose(router_expert_w, (1, 0, 2)).reshape(D_MODEL, MOE_N_EXPERTS)
    wr = jnp.zeros((D_MODEL, LANES), F32)
    wr = wr.at[:, :MOE_N_EXPERTS].set(we).at[:, MOE_N_EXPERTS:MOE_N_EXPERTS + MOE_GROUPS].set(router_group_w)
    br = jnp.zeros((1, LANES), F32)
    br = br.at[0, :MOE_N_EXPERTS].set(router_expert_b.reshape(-1))
    br = br.at[0, MOE_N_EXPERTS:MOE_N_EXPERTS + MOE_GROUPS].set(router_group_b)
    return wr, br


def kernel(x, norm1_w, w_in, gla_fwd_gate_w, gla_fwd_gate_b, gla_bwd_gate_w, gla_bwd_gate_b,
           gla_norm_w, w_out, norm2_w, router_group_w, router_group_b, router_expert_w,
           router_expert_b, expert_w_gate, expert_w_up, expert_w_down, final_norm_w):
    B, S, D = x.shape
    T = B * S
    assert norm1_w.shape[0] == 1, "single-layer trunk: the final norm is fused into the combine step"
    h = x.reshape(T, D)
    gla_slab, loga, att_slab = _inproj(h, S, norm1_w[0], w_in[0], gla_fwd_gate_w[0], gla_fwd_gate_b[0],
                                       gla_bwd_gate_w[0], gla_bwd_gate_b[0])
    o_f, o_b = _gla(gla_slab, loga, B, S)
    att_out = _attention(att_slab.reshape(T, 3 * ATT_WIDTH), B, S)
    att_out = att_out.reshape(B, ATT_CLASSES, S // ATT_CLASSES, ATT_WIDTH)
    wr, br = _router_weights(router_group_w[0], router_group_b[0], router_expert_w[0], router_expert_b[0])
    h, u2, logits = _outproj(o_f, o_b, gla_slab, att_out, h, gla_norm_w[0], w_out[0], norm2_w[0], wr, br)
    info, counts = _route(logits)
    dest, n_used, block_expert, next_expert, zero_blocks, cap = _moe_plan(info, counts, T)
    xs = _dispatch(dest, zero_blocks, u2, cap)
    ys = _experts(block_expert, next_expert, n_used, xs, expert_w_gate[0], expert_w_up[0], expert_w_down[0])
    out = _combine(dest, ys, info, h, final_norm_w)
    return out.reshape(B, S, D)
```

```python
import functools

import jax
import jax.numpy as jnp
from jax import lax
from jax.experimental import pallas as pl
from jax.experimental.pallas import tpu as pltpu

F32 = jnp.float32
BF16 = jnp.bfloat16

D_MODEL = 1024
GLA_HEADS = 4
GLA_DV = 128
GLA_DK = 64
GLA_KEY_WIDTH = GLA_HEADS * GLA_DK
GLA_VAL_WIDTH = GLA_HEADS * GLA_DV
GLA_GATE_RANK = 16
GLA_TAU = 16.0
GLA_CHUNK = 64
ATT_WIDTH = 512
ATT_HEAD_DIM = 64
ATT_HEADS = 8
ROT_DIM = 16
ROPE_THETA = 500000.0
DILATED_PATTERNS = ((128, 1), (512, 4), (2048, 16))
ATT_RADIUS = 64
MOE_GROUPS = 4
MOE_EXPERTS_PER_GROUP = 8
MOE_N_EXPERTS = 32
MOE_TOP_K = 2
MOE_D_FF = 512
EPS = 1e-6
NEG_INF = -1e30
LOG2E = 1.4426950408889634

LANES = 128
MOE_ROWS = 256
VMEM_LIMIT = 56 * 1024 * 1024


def _cparams(sem):
    return pltpu.CompilerParams(dimension_semantics=sem, vmem_limit_bytes=VMEM_LIMIT)


def _dot(a, b):
    return jnp.dot(a, b, preferred_element_type=F32)


def _dot_nt(a, b):
    return lax.dot_general(a, b, (((1,), (1,)), ((), ())), preferred_element_type=F32)


def _dot_tn(a, b):
    return lax.dot_general(a, b, (((0,), (0,)), ((), ())), preferred_element_type=F32)


def _rms(x, w):
    return x * lax.rsqrt(jnp.mean(x * x, axis=-1, keepdims=True) + EPS) * w


def _inproj_kernel(x_ref, n1_ref, wg_ref, wlr_ref, wa_ref, gw_ref, gb_ref,
                   rc_ref, rs1_ref, rs2_ref, gla_ref, loga_ref, att_ref, stage_ref):
    x = x_ref[...]
    ub = _rms(x, n1_ref[...]).astype(BF16)
    g = _dot(ub, wg_ref[...])
    gla_ref[:, :GLA_KEY_WIDTH] = g[:, :GLA_KEY_WIDTH] * (GLA_DK ** -0.5)
    gla_ref[:, GLA_KEY_WIDTH:] = g[:, GLA_KEY_WIDTH:]
    lr = _dot(ub, wlr_ref[...])
    gate = _dot(lr.astype(BF16), gw_ref[...]) + gb_ref[...]
    loga_ref[...] = (jnp.minimum(gate, 0.0) - jnp.log(1.0 + jnp.exp(-jnp.abs(gate)))) * (1.0 / GLA_TAU)
    a = _dot(ub, wa_ref[...])
    qk = a[:, :2 * ATT_WIDTH]
    reps = 2 * ATT_WIDTH // LANES
    c = jnp.concatenate([rc_ref[...]] * reps, axis=1)
    s1 = jnp.concatenate([rs1_ref[...]] * reps, axis=1)
    s2 = jnp.concatenate([rs2_ref[...]] * reps, axis=1)
    half = ROT_DIM // 2
    n = 2 * ATT_WIDTH
    roped = qk * c + pltpu.roll(qk, n - half, 1) * s1 + pltpu.roll(qk, half, 1) * s2
    qkv = jnp.concatenate([roped[:, :ATT_WIDTH] * (ATT_HEAD_DIM ** -0.5 * LOG2E), roped[:, ATT_WIDTH:],
                           a[:, 2 * ATT_WIDTH:]], axis=1)
    rows = x.shape[0] // ATT_CLASSES
    for j in range(3 * ATT_WIDTH // LANES):
        cols = slice(j * LANES, (j + 1) * LANES)
        stage_ref[j] = qkv[:, cols]
        for c in range(ATT_CLASSES):
            att_ref[c, :, cols] = stage_ref[j, pl.ds(c, rows, stride=ATT_CLASSES), :]


def _rope_lane_tables(S):
    half = ROT_DIM // 2
    inv = ROPE_THETA ** (-(jnp.arange(0, ROT_DIM, 2, dtype=F32) / ROT_DIM))
    ang = inv[:, None] * jnp.arange(S, dtype=F32)[None, :]
    cos, sin = jnp.cos(ang), jnp.sin(ang)
    lane = jnp.arange(LANES) % ATT_HEAD_DIM
    freq = jnp.arange(half)[:, None]
    first = ((lane[None, :] == freq)).astype(F32)
    second = ((lane[None, :] == freq + half)).astype(F32)
    expand = lambda t, sel: lax.dot_general(t, sel, (((0,), (0,)), ((), ())), precision=lax.Precision.HIGHEST)
    rest = (lane >= ROT_DIM).astype(F32)[None, :]
    return expand(cos, first + second) + rest, expand(-sin, first), expand(sin, second)


def _inproj(x2, S, norm1_w, w_in, wf, bfw, wb, bbw, tm=512):
    T = x2.shape[0]
    o_lr = 2 * GLA_KEY_WIDTH + 2 * GLA_VAL_WIDTH
    o_att = o_lr + 2 * GLA_GATE_RANK
    w_main = w_in[:, :o_lr + LANES].astype(BF16)
    wa = w_in[:, o_att:].astype(BF16)
    zeros = jnp.zeros((GLA_GATE_RANK, GLA_KEY_WIDTH), F32)
    gw = jnp.concatenate([jnp.concatenate([wf, zeros], axis=1), jnp.concatenate([zeros, wb], axis=1),
                          jnp.zeros((LANES - 2 * GLA_GATE_RANK, 2 * GLA_KEY_WIDTH), F32)], axis=0).astype(BF16)
    gb = jnp.concatenate([bfw, bbw])[None, :]
    rc, rs1, rs2 = _rope_lane_tables(S)
    nS = S // tm
    row = lambda i: (i, 0)
    const = lambda i: (0, 0)
    pos = lambda i: (i % nS, 0)
    return pl.pallas_call(
        _inproj_kernel,
        grid=(T // tm,),
        in_specs=[
            pl.BlockSpec((tm, D_MODEL), row),
            pl.BlockSpec((1, D_MODEL), const),
            pl.BlockSpec((D_MODEL, o_lr), const),
            pl.BlockSpec((D_MODEL, LANES), lambda i: (0, o_lr // LANES)),
            pl.BlockSpec((D_MODEL, 3 * ATT_WIDTH), const),
            pl.BlockSpec((LANES, 2 * GLA_KEY_WIDTH), const),
            pl.BlockSpec((1, 2 * GLA_KEY_WIDTH), const),
            pl.BlockSpec((tm, LANES), pos),
            pl.BlockSpec((tm, LANES), pos),
            pl.BlockSpec((tm, LANES), pos),
        ],
        out_specs=[
            pl.BlockSpec((tm, o_lr), row),
            pl.BlockSpec((tm, 2 * GLA_KEY_WIDTH), row),
            pl.BlockSpec((None, ATT_CLASSES, tm // ATT_CLASSES, 3 * ATT_WIDTH),
                         lambda i: (i // nS, 0, i % nS, 0)),
        ],
        out_shape=[
            jax.ShapeDtypeStruct((T, o_lr), F32),
            jax.ShapeDtypeStruct((T, 2 * GLA_KEY_WIDTH), F32),
            jax.ShapeDtypeStruct((T // S, ATT_CLASSES, S // ATT_CLASSES, 3 * ATT_WIDTH), F32),
        ],
        scratch_shapes=[pltpu.VMEM((3 * ATT_WIDTH // LANES, tm, LANES), F32)],
        compiler_params=_cparams(("arbitrary",)),
        name="inproj",
    )(x2, norm1_w[None, :], w_main, w_main, wa, gw, gb, rc, rs1, rs2)


def _gla_direction(q, k, v, la, s_ref, o_ref, forward, G):
    C = GLA_CHUNK
    R = G * C
    r = lax.broadcasted_iota(jnp.int32, (R, R), 0)
    c = lax.broadcasted_iota(jnp.int32, (R, R), 1)
    same = (r >> 6) == (c >> 6)
    tri = (c <= r) if forward else (c >= r)
    t_mat = jnp.where(same, jnp.where(tri, 1.0, 0.0), 0.0).astype(BF16)
    hi = la.astype(BF16)
    lo = (la - hi.astype(F32)).astype(BF16)
    b = _dot(t_mat, hi) + _dot(t_mat, lo)
    edge = C - 1 if forward else 0
    tot = jnp.concatenate([jnp.broadcast_to(b[g * C + edge:g * C + edge + 1], (C, GLA_KEY_WIDTH))
                           for g in range(G)], axis=0)
    q_dec = (q * jnp.exp(b)).astype(BF16)
    k_inv = k * jnp.exp(-b)
    k_end = (k * jnp.exp(tot - b)).astype(BF16)
    dec = jnp.exp(tot)

    lane_k = lax.broadcasted_iota(jnp.int32, (C, GLA_KEY_WIDTH), 1)
    lane_v = lax.broadcasted_iota(jnp.int32, (C, GLA_VAL_WIDTH), 1)
    row_c = lax.broadcasted_iota(jnp.int32, (C, GLA_KEY_WIDTH), 0)
    col_in = lane_k & (C - 1)
    a_mask = (col_in <= row_c) if forward else (col_in >= row_c)
    srow = lax.broadcasted_iota(jnp.int32, (GLA_VAL_WIDTH, GLA_KEY_WIDTH), 0)
    scol = lax.broadcasted_iota(jnp.int32, (GLA_VAL_WIDTH, GLA_KEY_WIDTH), 1)
    s_mask = (srow >> 7) == (scol >> 6)

    order = range(G) if forward else range(G - 1, -1, -1)
    for g in order:
        rows = slice(g * C, (g + 1) * C)
        qd, ki, ke, vv = q_dec[rows], k_inv[rows], k_end[rows], v[rows]
        km = jnp.concatenate([jnp.where((lane_k >> 6) == h, ki, 0.0) for h in range(GLA_HEADS)], axis=0)
        a = _dot_nt(qd, km.astype(BF16))
        a = jnp.where(a_mask, a, 0.0).astype(BF16)
        vbd = jnp.concatenate([jnp.where((lane_v >> 7) == h, vv, 0.0) for h in range(GLA_HEADS)], axis=0)
        st = s_ref[...]
        o = _dot(a, vbd.astype(BF16)) + _dot_nt(qd, st.astype(BF16))
        o_ref[rows, :] = o
        kv = _dot_tn(vv.astype(BF16), ke)
        s_ref[...] = st * dec[g * C:g * C + 1, :] + jnp.where(s_mask, kv, 0.0)


def _gla_kernel(qf_ref, kf_ref, vf_ref, laf_ref, qb_ref, kb_ref, vb_ref, lab_ref,
                of_ref, ob_ref, sf_ref, sb_ref, *, G):
    @pl.when(pl.program_id(1) == 0)
    def _():
        sf_ref[...] = jnp.zeros_like(sf_ref)
        sb_ref[...] = jnp.zeros_like(sb_ref)

    _gla_direction(qf_ref[...], kf_ref[...], vf_ref[...], laf_ref[...], sf_ref, of_ref, True, G)
    _gla_direction(qb_ref[...], kb_ref[...], vb_ref[...], lab_ref[...], sb_ref, ob_ref, False, G)


def _gla(gla_slab, loga, B, S, G=4):
    T = B * S
    R = G * GLA_CHUNK
    ns = S // R
    fwd = lambda col: (lambda b, i: (b * ns + i, col))
    bwd = lambda col: (lambda b, i: (b * ns + ns - 1 - i, col))
    kw, vw = GLA_KEY_WIDTH, GLA_VAL_WIDTH
    return pl.pallas_call(
        functools.partial(_gla_kernel, G=G),
        grid=(B, ns),
        in_specs=[
            pl.BlockSpec((R, kw), fwd(0)), pl.BlockSpec((R, kw), fwd(1)),
            pl.BlockSpec((R, vw), fwd(1)), pl.BlockSpec((R, kw), fwd(0)),
            pl.BlockSpec((R, kw), bwd(0)), pl.BlockSpec((R, kw), bwd(1)),
            pl.BlockSpec((R, vw), bwd(1)), pl.BlockSpec((R, kw), bwd(1)),
        ],
        out_specs=[pl.BlockSpec((R, vw), fwd(0)), pl.BlockSpec((R, vw), bwd(0))],
        out_shape=[jax.ShapeDtypeStruct((T, vw), F32), jax.ShapeDtypeStruct((T, vw), F32)],
        scratch_shapes=[pltpu.VMEM((vw, kw), F32), pltpu.VMEM((vw, kw), F32)],
        compiler_params=_cparams(("arbitrary", "arbitrary")),
        name="gla",
    )(gla_slab, gla_slab, gla_slab, loga, gla_slab, gla_slab, gla_slab, loga)


ATT_CLASSES = 4
ATT_QB = 128
ATT_KB = ATT_QB + 2 * ATT_RADIUS


ATT_UNROLL = 4


def _att_kernel(q_ref, k_ref, v_ref, o_ref, m_ref, l_ref, bias_ref, *, S):
    QB, KB, NC = ATT_QB, ATT_KB, ATT_CLASSES
    L4 = S // NC
    lane = lax.broadcasted_iota(jnp.int32, (QB, LANES), 1)
    head0 = lane < ATT_HEAD_DIM

    @pl.when((pl.program_id(0) == 0) & (pl.program_id(1) == 0))
    def _():
        rowi = lax.broadcasted_iota(jnp.int32, (2 * QB, KB), 0) & (QB - 1)
        coli = lax.broadcasted_iota(jnp.int32, (2 * QB, KB), 1)
        qpos = (rowi & (QB // NC - 1)) * NC + (rowi >> 5)
        kpos = (coli & (KB // NC - 1)) * NC + (coli >> 6)
        for case in range(3):
            bias_ref[0, case] = jnp.where(jnp.abs(rowi - coli + case * ATT_RADIUS) <= ATT_RADIUS, 0.0, NEG_INF)
            bias_ref[1, case] = jnp.where(jnp.abs(qpos - kpos + case * ATT_RADIUS) <= ATT_RADIUS, 0.0, NEG_INF)

    for pi, (_, d) in enumerate(DILATED_PATTERNS):
        L = S // d
        nb = L // QB
        shift = nb.bit_length() - 1
        first = pi == 0
        last = pi == len(DILATED_PATTERNS) - 1

        def scores(n, d=d, L=L, nb=nb, shift=shift):
            cls = n >> shift
            q0 = (n & (nb - 1)) * QB
            ws = jnp.clip(q0 - ATT_RADIUS, 0, L - KB)
            if d == 1:
                qsls = [pl.ds(pl.multiple_of(c * L4 + q0 // NC, QB // NC), QB // NC) for c in range(NC)]
                ksls = [pl.ds(pl.multiple_of(c * L4 + ws // NC, ATT_RADIUS // NC), KB // NC) for c in range(NC)]
            elif d == NC:
                qsls = [pl.ds(pl.multiple_of(cls * L4 + q0, QB), QB)]
                ksls = [pl.ds(pl.multiple_of(cls * L4 + ws, ATT_RADIUS), KB)]
            else:
                base = (cls & (NC - 1)) * L4 + (cls >> 2)
                qsls = [pl.ds(base + NC * q0, QB, stride=NC)]
                ksls = [pl.ds(base + NC * ws, KB, stride=NC)]
            q = jnp.concatenate([q_ref[sl, :] for sl in qsls], axis=0)
            kw = jnp.concatenate([k_ref[sl, :] for sl in ksls], axis=0)
            q2 = jnp.concatenate([jnp.where(head0, q, 0.0), jnp.where(head0, 0.0, q)], axis=0).astype(BF16)
            s = _dot_nt(q2, kw.astype(BF16))
            return qsls, ksls, s + bias_ref[1 if d == 1 else 0, (q0 - ws) >> 6]

        def softmax_pv(qsls, ksls, s):
            m_blk = jnp.max(s, axis=-1, keepdims=True)
            p = jnp.exp2(s - m_blk)
            l_blk = jnp.sum(p, axis=-1, keepdims=True)
            vw = jnp.concatenate([v_ref[sl, :] for sl in ksls], axis=0)
            pv = _dot(p.astype(BF16), vw.astype(BF16))
            acc_b = jnp.where(head0, pv[:QB], pv[QB:])
            m_b = jnp.where(head0, m_blk[:QB], m_blk[QB:])
            l_b = jnp.where(head0, l_blk[:QB], l_blk[QB:])
            return qsls, acc_b, m_b, l_b

        def load(ref, sls):
            return jnp.concatenate([ref[sl, :] for sl in sls], axis=0)

        def store(ref, sls, val):
            n = val.shape[0] // len(sls)
            for i, sl in enumerate(sls):
                ref[sl, :] = val[i * n:(i + 1) * n]

        def body(n, carry, first=first, last=last):
            staged = [scores(n * ATT_UNROLL + u) for u in range(ATT_UNROLL)]
            blocks = [softmax_pv(*st) for st in staged]
            for qsls, acc_b, m_b, l_b in blocks:
                if first:
                    acc, m_new, l_new = acc_b, m_b, l_b
                else:
                    m_old = load(m_ref, qsls)
                    m_new = jnp.maximum(m_old, m_b)
                    w_old = jnp.exp2(m_old - m_new)
                    w_blk = jnp.exp2(m_b - m_new)
                    acc = load(o_ref, qsls) * w_old + acc_b * w_blk
                    l_new = load(l_ref, qsls) * w_old + l_b * w_blk
                if last:
                    store(o_ref, qsls, acc / l_new)
                else:
                    store(o_ref, qsls, acc)
                    store(m_ref, qsls, m_new)
                    store(l_ref, qsls, l_new)
            return carry

        lax.fori_loop(0, S // (QB * ATT_UNROLL), body, 0)


def _attention(att_slab, B, S):
    T = B * S
    ncol = ATT_WIDTH // LANES
    return pl.pallas_call(
        functools.partial(_att_kernel, S=S),
        grid=(B, ncol),
        in_specs=[
            pl.BlockSpec((S, LANES), lambda b, h: (b, h)),
            pl.BlockSpec((S, LANES), lambda b, h: (b, ncol + h)),
            pl.BlockSpec((S, LANES), lambda b, h: (b, 2 * ncol + h)),
        ],
        out_specs=pl.BlockSpec((S, LANES), lambda b, h: (b, h)),
        out_shape=jax.ShapeDtypeStruct((T, ATT_WIDTH), F32),
        scratch_shapes=[pltpu.VMEM((S, LANES), F32), pltpu.VMEM((S, LANES), F32),
                        pltpu.VMEM((2, 3, 2 * ATT_QB, ATT_KB), F32)],
        compiler_params=_cparams(("arbitrary", "arbitrary")),
        name="dilated_attention",
    )(att_slab, att_slab, att_slab)


ROW_TILE = D_MODEL // LANES


def _to_row_tiles(ref, x):
    n = x.shape[0]
    for j in range(ROW_TILE):
        ref[pl.ds(j, n, stride=ROW_TILE), :] = x[:, j * LANES:(j + 1) * LANES]


def _from_row_tiles(ref, n):
    return jnp.concatenate([ref[pl.ds(j, n, stride=ROW_TILE), :] for j in range(ROW_TILE)], axis=1)


def _tile_copy(src_ref, src_row, dst_ref, dst_row, sem):
    src = pl.ds(pl.multiple_of(src_row * ROW_TILE, ROW_TILE), ROW_TILE)
    dst = pl.ds(pl.multiple_of(dst_row * ROW_TILE, ROW_TILE), ROW_TILE)
    return pltpu.make_async_copy(src_ref.at[src], dst_ref.at[dst], sem)


def _outproj_kernel(of_ref, ob_ref, gg_ref, att_ref, x_ref, gnw_ref, wo1_ref, wo2_ref,
                    n2_ref, wr_ref, br_ref, h_ref, u_ref, lg_ref, stage_ref):
    rows = stage_ref.shape[1] // ATT_CLASSES
    for j in range(ATT_WIDTH // LANES):
        for c in range(ATT_CLASSES):
            stage_ref[j, pl.ds(c, rows, stride=ATT_CLASSES), :] = att_ref[c, :, j * LANES:(j + 1) * LANES]
    att = jnp.concatenate([stage_ref[j] for j in range(ATT_WIDTH // LANES)], axis=1)
    o = of_ref[...] + ob_ref[...]
    gate = gg_ref[...]
    gnw = gnw_ref[...]
    parts = []
    for h in range(GLA_HEADS):
        sl = slice(h * GLA_DV, (h + 1) * GLA_DV)
        parts.append(_rms(o[:, sl], gnw))
    y = jnp.concatenate(parts, axis=1) * (gate / (1.0 + jnp.exp(-gate)))
    mix = _dot(y.astype(BF16), wo1_ref[...]) + _dot(att.astype(BF16), wo2_ref[...])
    h = x_ref[...] + mix
    h_ref[...] = h
    u = _rms(h, n2_ref[...])
    _to_row_tiles(u_ref, u)
    u_hi = u.astype(BF16)
    u_lo = (u - u_hi.astype(F32)).astype(BF16)
    hi_both = _dot(u_hi, wr_ref[...])
    lg_ref[...] = (hi_both[:, :LANES] + hi_both[:, LANES:] + _dot(u_lo, wr_ref[:, :LANES])) + br_ref[...]


def _outproj(o_f, o_b, gla_slab, att_out, x2, gla_norm_w, w_out, norm2_w, wr, br, tm=512):
    T = x2.shape[0]
    nS = att_out.shape[2] * ATT_CLASSES // tm
    row = lambda i: (i, 0)
    const = lambda i: (0, 0)
    wo = w_out.astype(BF16)
    wr_hi = wr.astype(BF16)
    wr_lo = (wr - wr_hi.astype(F32)).astype(BF16)
    wr = jnp.concatenate([wr_hi, wr_lo], axis=1)
    return pl.pallas_call(
        _outproj_kernel,
        grid=(T // tm,),
        in_specs=[
            pl.BlockSpec((tm, GLA_VAL_WIDTH), row),
            pl.BlockSpec((tm, GLA_VAL_WIDTH), row),
            pl.BlockSpec((tm, GLA_VAL_WIDTH), lambda i: (i, 2)),
            pl.BlockSpec((None, ATT_CLASSES, tm // ATT_CLASSES, ATT_WIDTH), lambda i: (i // nS, 0, i % nS, 0)),
            pl.BlockSpec((tm, D_MODEL), row),
            pl.BlockSpec((1, GLA_DV), const),
            pl.BlockSpec((GLA_VAL_WIDTH, D_MODEL), const),
            pl.BlockSpec((ATT_WIDTH, D_MODEL), const),
            pl.BlockSpec((1, D_MODEL), const),
            pl.BlockSpec((D_MODEL, 2 * LANES), const),
            pl.BlockSpec((1, LANES), const),
        ],
        out_specs=[
            pl.BlockSpec((tm, D_MODEL), row),
            pl.BlockSpec((tm * ROW_TILE, LANES), row),
            pl.BlockSpec((tm, LANES), row),
        ],
        out_shape=[
            jax.ShapeDtypeStruct((T, D_MODEL), F32),
            jax.ShapeDtypeStruct((T * ROW_TILE, LANES), F32),
            jax.ShapeDtypeStruct((T, LANES), F32),
        ],
        scratch_shapes=[pltpu.VMEM((ATT_WIDTH // LANES, tm, LANES), F32)],
        compiler_params=_cparams(("arbitrary",)),
        name="outproj",
    )(o_f, o_b, gla_slab, att_out, x2, gla_norm_w[None, :], wo[:GLA_VAL_WIDTH], wo[GLA_VAL_WIDTH:],
      norm2_w[None, :], wr, br)


INFO_E1, INFO_E2, INFO_R1, INFO_R2, INFO_W1, INFO_W2 = range(6)


def _route_kernel(lg_ref, info_ref, cnt_ref, carry_ref):
    @pl.when(pl.program_id(0) == 0)
    def _():
        carry_ref[...] = jnp.zeros_like(carry_ref)

    lg = lg_ref[...]
    tr = lg.shape[0]
    lane = lax.broadcasted_iota(jnp.int32, (tr, LANES), 1)
    big = jnp.int32(1 << 20)
    is_g = (lane >= MOE_N_EXPERTS) & (lane < MOE_N_EXPERTS + MOE_GROUPS)
    gl = jnp.where(is_g, lg, -jnp.inf)
    gmax = jnp.max(gl, axis=-1, keepdims=True)
    gsel = jnp.min(jnp.where(gl == gmax, lane - MOE_N_EXPERTS, big), axis=-1, keepdims=True)
    g_w = 1.0 / jnp.sum(jnp.where(is_g, jnp.exp(lg - gmax), 0.0), axis=-1, keepdims=True)
    in_grp = (lane < MOE_N_EXPERTS) & ((lane >> 3) == gsel)
    el = jnp.where(in_grp, lg, -jnp.inf)
    v1 = jnp.max(el, axis=-1, keepdims=True)
    i1 = jnp.min(jnp.where(el == v1, lane, big), axis=-1, keepdims=True)
    el2 = jnp.where(lane == i1, -jnp.inf, el)
    v2 = jnp.max(el2, axis=-1, keepdims=True)
    i2 = jnp.min(jnp.where(el2 == v2, lane, big), axis=-1, keepdims=True)
    t = jnp.exp(v2 - v1)
    w1 = g_w * (1.0 / (1.0 + t))
    w2 = g_w * (t / (1.0 + t))

    hit1 = lane == i1
    hit2 = lane == i2
    member = jnp.where(hit1 | hit2, 1.0, 0.0)
    r = lax.broadcasted_iota(jnp.int32, (tr, tr), 0)
    c = lax.broadcasted_iota(jnp.int32, (tr, tr), 1)
    strict = jnp.where(c < r, 1.0, 0.0).astype(BF16)
    prefix = _dot(strict, member.astype(BF16)) + carry_ref[...]
    rank1 = jnp.sum(jnp.where(hit1, prefix, 0.0), axis=-1, keepdims=True)
    rank2 = jnp.sum(jnp.where(hit2, prefix, 0.0), axis=-1, keepdims=True)
    carry = carry_ref[...] + jnp.sum(member, axis=0, keepdims=True)
    carry_ref[...] = carry
    cnt_ref[...] = carry

    info = jnp.where(lane == INFO_E1, i1.astype(F32), 0.0)
    info = jnp.where(lane == INFO_E2, i2.astype(F32), info)
    info = jnp.where(lane == INFO_R1, rank1, info)
    info = jnp.where(lane == INFO_R2, rank2, info)
    info = jnp.where(lane == INFO_W1, w1, info)
    info = jnp.where(lane == INFO_W2, w2, info)
    info_ref[...] = info


def _route(logits, tr=512):
    T = logits.shape[0]
    return pl.pallas_call(
        _route_kernel,
        grid=(T // tr,),
        in_specs=[pl.BlockSpec((tr, LANES), lambda i: (i, 0))],
        out_specs=[pl.BlockSpec((tr, LANES), lambda i: (i, 0)), pl.BlockSpec((1, LANES), lambda i: (0, 0))],
        out_shape=[jax.ShapeDtypeStruct((T, LANES), F32), jax.ShapeDtypeStruct((1, LANES), F32)],
        scratch_shapes=[pltpu.VMEM((1, LANES), F32)],
        compiler_params=_cparams(("arbitrary",)),
        name="route",
    )(logits)


ROW_UNROLL = 8


def _dispatch_kernel(dest_ref, pend_ref, u_ref, xs_ref, zbuf, sem, zsem, *, td, T, nblk):
    @pl.when(pl.program_id(0) == 0)
    def _():
        zbuf[...] = jnp.zeros_like(zbuf)
        n_used = pend_ref[MOE_N_EXPERTS - 1] >> 8

        def zero_copy(blk):
            start = pl.multiple_of(blk * (MOE_ROWS * ROW_TILE), MOE_ROWS * ROW_TILE)
            return pltpu.make_async_copy(zbuf, xs_ref.at[pl.ds(start, MOE_ROWS * ROW_TILE)], zsem)

        def each_pad_block(fn):
            def per_expert(e, carry):
                prev = jnp.where(e > 0, pend_ref[jnp.maximum(e - 1, 0)], 0)

                @pl.when(pend_ref[e] > prev)
                def _():
                    fn((pend_ref[e] >> 8) - 1)
                return carry

            def per_tail(j, carry):
                @pl.when(n_used + j < nblk)
                def _():
                    fn(n_used + j)
                return carry

            lax.fori_loop(0, MOE_N_EXPERTS, per_expert, 0)
            lax.fori_loop(0, MOE_N_EXPERTS, per_tail, 0)

        each_pad_block(lambda blk: zero_copy(blk).start())
        each_pad_block(lambda blk: zero_copy(blk).wait())

    base = pl.program_id(0) * td

    def issue(g, carry):
        for j in range(ROW_UNROLL):
            r = g * ROW_UNROLL + j
            for k in range(MOE_TOP_K):
                _tile_copy(u_ref, r, xs_ref, dest_ref[k * T + base + r], sem).start(priority=k)
        return carry

    lax.fori_loop(0, td // ROW_UNROLL, issue, 0)
    for k in range(MOE_TOP_K):
        pltpu.make_async_copy(u_ref, xs_ref.at[pl.ds(0, td * ROW_TILE)], sem).wait()


def _dispatch(dest, pend, u2, cap, td=256):
    T = u2.shape[0] // ROW_TILE
    return pl.pallas_call(
        functools.partial(_dispatch_kernel, td=td, T=T, nblk=cap // MOE_ROWS),
        grid_spec=pltpu.PrefetchScalarGridSpec(
            num_scalar_prefetch=2,
            grid=(T // td,),
            in_specs=[pl.BlockSpec((td * ROW_TILE, LANES), lambda i, d, z: (i, 0))],
            out_specs=pl.BlockSpec(memory_space=pl.ANY),
            scratch_shapes=[pltpu.VMEM((MOE_ROWS * ROW_TILE, LANES), F32),
                            pltpu.SemaphoreType.DMA(()), pltpu.SemaphoreType.DMA(())],
        ),
        out_shape=jax.ShapeDtypeStruct((cap * ROW_TILE, LANES), F32),
        compiler_params=_cparams(("arbitrary",)),
        name="dispatch",
    )(dest, pend, u2)


def _expert_kernel(pend_ref, x_ref, wg_hbm, wu_hbm, wd_hbm, y_ref,
                   stage_g, stage_u, stage_d, wgb, wub, wdb, cur_ref, sem):
    b = pl.program_id(0)
    last = MOE_N_EXPERTS - 1
    live = b < (pend_ref[last] >> 8)

    def weight_copies(e):
        return (pltpu.make_async_copy(wg_hbm.at[e], stage_g, sem.at[0]),
                pltpu.make_async_copy(wu_hbm.at[e], stage_u, sem.at[1]),
                pltpu.make_async_copy(wd_hbm.at[e], stage_d, sem.at[2]))

    def owner(start, row):
        return lax.while_loop(lambda e: (e < last) & (pend_ref[e] <= row), lambda e: e + 1, start)

    @pl.when(b == 0)
    def _():
        first = owner(0, 0)
        cur_ref[0] = -1
        for c in weight_copies(first):
            c.start()

    @pl.when(live)
    def _():
        prev = cur_ref[0]
        e = owner(jnp.maximum(prev, 0), b * MOE_ROWS)
        cur_ref[0] = e

        @pl.when(e != prev)
        def _():
            for c in weight_copies(e):
                c.wait()
            wgb[...] = stage_g[...].astype(BF16)
            wub[...] = stage_u[...].astype(BF16)
            wdb[...] = stage_d[...].astype(BF16)

            @pl.when(pend_ref[e] < pend_ref[last])
            def _():
                for c in weight_copies(owner(e + 1, pend_ref[e])):
                    c.start()

        xb = _from_row_tiles(x_ref, MOE_ROWS).astype(BF16)
        g = _dot(xb, wgb[...])
        u = _dot(xb, wub[...])
        hid = (g / (1.0 + jnp.exp(-g))) * u
        _to_row_tiles(y_ref, _dot(hid.astype(BF16), wdb[...]))

    @pl.when(jnp.logical_not(live))
    def _():
        y_ref[...] = jnp.zeros_like(y_ref)


def _experts(pend, xs, w_gate, w_up, w_down):
    cap = xs.shape[0] // ROW_TILE
    nblk = cap // MOE_ROWS
    rows = lambda b, pend: (jnp.minimum(b, (pend[MOE_N_EXPERTS - 1] >> 8) - 1), 0)
    return pl.pallas_call(
        _expert_kernel,
        grid_spec=pltpu.PrefetchScalarGridSpec(
            num_scalar_prefetch=1,
            grid=(nblk,),
            in_specs=[
                pl.BlockSpec((MOE_ROWS * ROW_TILE, LANES), rows),
                pl.BlockSpec(memory_space=pl.ANY),
                pl.BlockSpec(memory_space=pl.ANY),
                pl.BlockSpec(memory_space=pl.ANY),
            ],
            out_specs=pl.BlockSpec((MOE_ROWS * ROW_TILE, LANES), lambda b, pend: (b, 0)),
            scratch_shapes=[pltpu.VMEM((D_MODEL, MOE_D_FF), F32),
                            pltpu.VMEM((D_MODEL, MOE_D_FF), F32),
                            pltpu.VMEM((MOE_D_FF, D_MODEL), F32),
                            pltpu.VMEM((D_MODEL, MOE_D_FF), BF16),
                            pltpu.VMEM((D_MODEL, MOE_D_FF), BF16),
                            pltpu.VMEM((MOE_D_FF, D_MODEL), BF16),
                            pltpu.SMEM((1,), jnp.int32),
                            pltpu.SemaphoreType.DMA((3,))],
        ),
        out_shape=jax.ShapeDtypeStruct((cap * ROW_TILE, LANES), F32),
        compiler_params=_cparams(("arbitrary",)),
        name="experts",
    )(pend, xs, w_gate, w_up, w_down)


def _combine_kernel(dest_ref, ys_ref, info_ref, h_ref, fw_ref, o_ref, buf, sem, *, tc, T):
    i = pl.program_id(0)
    n = pl.num_programs(0)

    def issue(step, slot):
        base = step * tc

        def body(g, carry):
            for j in range(ROW_UNROLL):
                r = g * ROW_UNROLL + j
                for k in range(MOE_TOP_K):
                    _tile_copy(ys_ref, dest_ref[k * T + base + r], buf.at[slot, k], r,
                               sem.at[slot]).start(priority=k)
            return carry

        lax.fori_loop(0, tc // ROW_UNROLL, body, 0)

    @pl.when(i == 0)
    def _():
        issue(0, 0)

    slot = i % 2

    @pl.when(i + 1 < n)
    def _():
        issue(i + 1, 1 - slot)

    for k in range(MOE_TOP_K):
        pltpu.make_async_copy(ys_ref.at[pl.ds(0, tc * ROW_TILE)], buf.at[slot, k], sem.at[slot]).wait()

    info = info_ref[...]
    lane = lax.broadcasted_iota(jnp.int32, info.shape, 1)
    w1 = jnp.sum(jnp.where(lane == INFO_W1, info, 0.0), axis=-1, keepdims=True)
    w2 = jnp.sum(jnp.where(lane == INFO_W2, info, 0.0), axis=-1, keepdims=True)
    y1 = _from_row_tiles(buf.at[slot, 0], tc)
    y2 = _from_row_tiles(buf.at[slot, 1], tc)
    h = h_ref[...] + (y1 * w1 + y2 * w2)
    o_ref[...] = _rms(h, fw_ref[...])


def _combine(dest, ys, info, h, final_w, tc=256):
    T = h.shape[0]
    return pl.pallas_call(
        functools.partial(_combine_kernel, tc=tc, T=T),
        grid_spec=pltpu.PrefetchScalarGridSpec(
            num_scalar_prefetch=1,
            grid=(T // tc,),
            in_specs=[pl.BlockSpec(memory_space=pl.ANY),
                      pl.BlockSpec((tc, LANES), lambda i, d: (i, 0)),
                      pl.BlockSpec((tc, D_MODEL), lambda i, d: (i, 0)),
                      pl.BlockSpec((1, D_MODEL), lambda i, d: (0, 0))],
            out_specs=pl.BlockSpec((tc, D_MODEL), lambda i, d: (i, 0)),
            scratch_shapes=[pltpu.VMEM((2, MOE_TOP_K, tc * ROW_TILE, LANES), F32),
                            pltpu.SemaphoreType.DMA((2,))],
        ),
        out_shape=jax.ShapeDtypeStruct((T, D_MODEL), F32),
        compiler_params=_cparams(("arbitrary",)),
        name="combine",
    )(dest, ys, info, h, final_w[None, :])


def _plan_kernel(info_ref, cnt_ref, dest_ref, pend_ref):
    cnt = cnt_ref[...].astype(jnp.int32)
    nblk_e = ((cnt + (MOE_ROWS - 1)) >> 8).astype(F32)
    r = lax.broadcasted_iota(jnp.int32, (LANES, LANES), 0)
    c = lax.broadcasted_iota(jnp.int32, (LANES, LANES), 1)
    before = jnp.where(r < c, 1.0, 0.0).astype(BF16)
    first_blk = _dot(jnp.broadcast_to(nblk_e, (8, LANES)).astype(BF16), before)[0:1]
    pstart = first_blk * float(MOE_ROWS)
    pend_ref[...] = ((first_blk + nblk_e) * float(MOE_ROWS)).astype(jnp.int32)

    info = info_ref[...]
    lane = lax.broadcasted_iota(jnp.int32, info.shape, 1)
    col = lambda idx: jnp.sum(jnp.where(lane == idx, info, 0.0), axis=-1, keepdims=True)
    start_of = lambda e: jnp.sum(jnp.where(lane == e.astype(jnp.int32), pstart, 0.0), axis=-1, keepdims=True)
    d1 = col(INFO_R1) + start_of(col(INFO_E1))
    d2 = col(INFO_R2) + start_of(col(INFO_E2))
    both = jnp.where(lane == 0, d1, jnp.where(lane == 1, d2, 0.0))
    dest_ref[...] = both.T[:8].astype(jnp.int32)


def _plan(info, counts, tr=512):
    T = info.shape[0]
    dest8, pend = pl.pallas_call(
        _plan_kernel,
        grid=(T // tr,),
        in_specs=[pl.BlockSpec((tr, LANES), lambda i: (i, 0)), pl.BlockSpec((1, LANES), lambda i: (0, 0))],
        out_specs=[pl.BlockSpec((8, tr), lambda i: (0, i)), pl.BlockSpec((1, LANES), lambda i: (0, 0))],
        out_shape=[jax.ShapeDtypeStruct((8, T), jnp.int32), jax.ShapeDtypeStruct((1, LANES), jnp.int32)],
        compiler_params=_cparams(("arbitrary",)),
        name="plan",
    )(info, counts)
    return dest8[:MOE_TOP_K].reshape(-1), pend[0, :MOE_N_EXPERTS]


def _moe_capacity(T):
    return (-(-(T * MOE_TOP_K) // MOE_ROWS) + MOE_N_EXPERTS) * MOE_ROWS


def _router_weights(router_group_w, router_group_b, router_expert_w, router_expert_b):
    we = jnp.transpose(router_expert_w, (1, 0, 2)).reshape(D_MODEL, MOE_N_EXPERTS)
    pad = LANES - MOE_N_EXPERTS - MOE_GROUPS
    wr = jnp.concatenate([we, router_group_w, jnp.zeros((D_MODEL, pad), F32)], axis=1)
    br = jnp.concatenate([router_expert_b.reshape(-1), router_group_b, jnp.zeros((pad,), F32)])[None, :]
    return wr, br


def kernel(x, norm1_w, w_in, gla_fwd_gate_w, gla_fwd_gate_b, gla_bwd_gate_w, gla_bwd_gate_b,
           gla_norm_w, w_out, norm2_w, router_group_w, router_group_b, router_expert_w,
           router_expert_b, expert_w_gate, expert_w_up, expert_w_down, final_norm_w):
    B, S, D = x.shape
    T = B * S
    assert norm1_w.shape[0] == 1, "single-layer trunk: the final norm is fused into the combine step"
    h = x.reshape(T, D)
    gla_slab, loga, att_slab = _inproj(h, S, norm1_w[0], w_in[0], gla_fwd_gate_w[0], gla_fwd_gate_b[0],
                                       gla_bwd_gate_w[0], gla_bwd_gate_b[0])
    o_f, o_b = _gla(gla_slab, loga, B, S)
    att_out = _attention(att_slab.reshape(T, 3 * ATT_WIDTH), B, S)
    att_out = att_out.reshape(B, ATT_CLASSES, S // ATT_CLASSES, ATT_WIDTH)
    wr, br = _router_weights(router_group_w[0], router_group_b[0], router_expert_w[0], router_expert_b[0])
    h, u2, logits = _outproj(o_f, o_b, gla_slab, att_out, h, gla_norm_w[0], w_out[0], norm2_w[0], wr, br)
    info, counts = _route(logits)
    dest, pend = _plan(info, counts)
    xs = _dispatch(dest, pend, u2, _moe_capacity(T))
    ys = _experts(pend, xs, expert_w_gate[0], expert_w_up[0], expert_w_down[0])
    out = _combine(dest, ys, info, h, final_norm_w)
    return out.reshape(B, S, D)
```

```python
import functools

import jax
import jax.numpy as jnp
from jax import lax
from jax.experimental import pallas as pl
from jax.experimental.pallas import tpu as pltpu

F32 = jnp.float32
BF16 = jnp.bfloat16

D_MODEL = 1024
GLA_HEADS = 4
GLA_DV = 128
GLA_DK = 64
GLA_KEY_WIDTH = GLA_HEADS * GLA_DK
GLA_VAL_WIDTH = GLA_HEADS * GLA_DV
GLA_GATE_RANK = 16
GLA_TAU = 16.0
GLA_CHUNK = 64
ATT_WIDTH = 512
ATT_HEAD_DIM = 64
ATT_HEADS = 8
ROT_DIM = 16
ROPE_THETA = 500000.0
DILATED_PATTERNS = ((128, 1), (512, 4), (2048, 16))
ATT_RADIUS = 64
MOE_GROUPS = 4
MOE_EXPERTS_PER_GROUP = 8
MOE_N_EXPERTS = 32
MOE_TOP_K = 2
MOE_D_FF = 512
EPS = 1e-6
NEG_INF = -1e30
LOG2E = 1.4426950408889634

LANES = 128
MOE_ROWS = 256
VMEM_LIMIT = 56 * 1024 * 1024


def _cparams(sem):
    return pltpu.CompilerParams(dimension_semantics=sem, vmem_limit_bytes=VMEM_LIMIT)


def _dot(a, b):
    return jnp.dot(a, b, preferred_element_type=F32)


def _dot_nt(a, b):
    return lax.dot_general(a, b, (((1,), (1,)), ((), ())), preferred_element_type=F32)


def _dot_tn(a, b):
    return lax.dot_general(a, b, (((0,), (0,)), ((), ())), preferred_element_type=F32)


def _rms(x, w):
    return x * lax.rsqrt(jnp.mean(x * x, axis=-1, keepdims=True) + EPS) * w


def _inproj_kernel(x_ref, n1_ref, wg_ref, wlr_ref, wa_ref, gw_ref, gb_ref,
                   rc_ref, rs1_ref, rs2_ref, gla_ref, loga_ref, att_ref, stage_ref):
    x = x_ref[...]
    ub = _rms(x, n1_ref[...]).astype(BF16)
    g = _dot(ub, wg_ref[...])
    gla_ref[:, :GLA_KEY_WIDTH] = g[:, :GLA_KEY_WIDTH] * (GLA_DK ** -0.5)
    gla_ref[:, GLA_KEY_WIDTH:] = g[:, GLA_KEY_WIDTH:]
    lr = _dot(ub, wlr_ref[...])
    gate = _dot(lr.astype(BF16), gw_ref[...]) + gb_ref[...]
    loga_ref[...] = (jnp.minimum(gate, 0.0) - jnp.log(1.0 + jnp.exp(-jnp.abs(gate)))) * (1.0 / GLA_TAU)
    a = _dot(ub, wa_ref[...])
    qk = a[:, :2 * ATT_WIDTH]
    reps = 2 * ATT_WIDTH // LANES
    c = jnp.concatenate([rc_ref[...]] * reps, axis=1)
    s1 = jnp.concatenate([rs1_ref[...]] * reps, axis=1)
    s2 = jnp.concatenate([rs2_ref[...]] * reps, axis=1)
    half = ROT_DIM // 2
    n = 2 * ATT_WIDTH
    roped = qk * c + pltpu.roll(qk, n - half, 1) * s1 + pltpu.roll(qk, half, 1) * s2
    qkv = jnp.concatenate([roped[:, :ATT_WIDTH] * (ATT_HEAD_DIM ** -0.5 * LOG2E), roped[:, ATT_WIDTH:],
                           a[:, 2 * ATT_WIDTH:]], axis=1)
    rows = x.shape[0] // ATT_CLASSES
    for j in range(3 * ATT_WIDTH // LANES):
        cols = slice(j * LANES, (j + 1) * LANES)
        stage_ref[j] = qkv[:, cols]
        for c in range(ATT_CLASSES):
            att_ref[c, :, cols] = stage_ref[j, pl.ds(c, rows, stride=ATT_CLASSES), :]


def _rope_lane_tables(S):
    half = ROT_DIM // 2
    inv = ROPE_THETA ** (-(jnp.arange(0, ROT_DIM, 2, dtype=F32) / ROT_DIM))
    ang = inv[:, None] * jnp.arange(S, dtype=F32)[None, :]
    cos, sin = jnp.cos(ang), jnp.sin(ang)
    lane = jnp.arange(LANES) % ATT_HEAD_DIM
    freq = jnp.arange(half)[:, None]
    first = ((lane[None, :] == freq)).astype(F32)
    second = ((lane[None, :] == freq + half)).astype(F32)
    expand = lambda t, sel: lax.dot_general(t, sel, (((0,), (0,)), ((), ())), precision=lax.Precision.HIGHEST)
    rest = (lane >= ROT_DIM).astype(F32)[None, :]
    return expand(cos, first + second) + rest, expand(-sin, first), expand(sin, second)


def _inproj(x2, S, norm1_w, w_in, wf, bfw, wb, bbw, tm=512):
    T = x2.shape[0]
    o_lr = 2 * GLA_KEY_WIDTH + 2 * GLA_VAL_WIDTH
    o_att = o_lr + 2 * GLA_GATE_RANK
    w_main = w_in[:, :o_lr + LANES].astype(BF16)
    wa = w_in[:, o_att:].astype(BF16)
    zeros = jnp.zeros((GLA_GATE_RANK, GLA_KEY_WIDTH), F32)
    gw = jnp.concatenate([jnp.concatenate([wf, zeros], axis=1), jnp.concatenate([zeros, wb], axis=1),
                          jnp.zeros((LANES - 2 * GLA_GATE_RANK, 2 * GLA_KEY_WIDTH), F32)], axis=0).astype(BF16)
    gb = jnp.concatenate([bfw, bbw])[None, :]
    rc, rs1, rs2 = _rope_lane_tables(S)
    nS = S // tm
    row = lambda i: (i, 0)
    const = lambda i: (0, 0)
    pos = lambda i: (i % nS, 0)
    return pl.pallas_call(
        _inproj_kernel,
        grid=(T // tm,),
        in_specs=[
            pl.BlockSpec((tm, D_MODEL), row),
            pl.BlockSpec((1, D_MODEL), const),
            pl.BlockSpec((D_MODEL, o_lr), const),
            pl.BlockSpec((D_MODEL, LANES), lambda i: (0, o_lr // LANES)),
            pl.BlockSpec((D_MODEL, 3 * ATT_WIDTH), const),
            pl.BlockSpec((LANES, 2 * GLA_KEY_WIDTH), const),
            pl.BlockSpec((1, 2 * GLA_KEY_WIDTH), const),
            pl.BlockSpec((tm, LANES), pos),
            pl.BlockSpec((tm, LANES), pos),
            pl.BlockSpec((tm, LANES), pos),
        ],
        out_specs=[
            pl.BlockSpec((tm, o_lr), row),
            pl.BlockSpec((tm, 2 * GLA_KEY_WIDTH), row),
            pl.BlockSpec((None, ATT_CLASSES, tm // ATT_CLASSES, 3 * ATT_WIDTH),
                         lambda i: (i // nS, 0, i % nS, 0)),
        ],
        out_shape=[
            jax.ShapeDtypeStruct((T, o_lr), F32),
            jax.ShapeDtypeStruct((T, 2 * GLA_KEY_WIDTH), F32),
            jax.ShapeDtypeStruct((T // S, ATT_CLASSES, S // ATT_CLASSES, 3 * ATT_WIDTH), F32),
        ],
        scratch_shapes=[pltpu.VMEM((3 * ATT_WIDTH // LANES, tm, LANES), F32)],
        compiler_params=_cparams(("arbitrary",)),
        name="inproj",
    )(x2, norm1_w[None, :], w_main, w_main, wa, gw, gb, rc, rs1, rs2)


def _gla_decays(q, k, v, la, forward, G):
    C = GLA_CHUNK
    R = G * C
    r = lax.broadcasted_iota(jnp.int32, (R, R), 0)
    c = lax.broadcasted_iota(jnp.int32, (R, R), 1)
    same = (r >> 6) == (c >> 6)
    tri = (c <= r) if forward else (c >= r)
    t_mat = jnp.where(same, jnp.where(tri, 1.0, 0.0), 0.0).astype(BF16)
    hi = la.astype(BF16)
    lo = (la - hi.astype(F32)).astype(BF16)
    b = _dot(t_mat, hi) + _dot(t_mat, lo)
    edge = C - 1 if forward else 0
    tot = jnp.concatenate([jnp.broadcast_to(b[g * C + edge:g * C + edge + 1], (C, GLA_KEY_WIDTH))
                           for g in range(G)], axis=0)
    order = list(range(G)) if forward else list(range(G - 1, -1, -1))
    return dict(q_dec=q * jnp.exp(b), k_inv=(k * jnp.exp(-b)).astype(BF16), k_end=k * jnp.exp(tot - b),
                tot=tot, vb=v.astype(BF16), order=order, forward=forward, G=G)


def _gla_scores(prep):
    C, H = GLA_CHUNK, GLA_HEADS
    lane_k = lax.broadcasted_iota(jnp.int32, (C, GLA_KEY_WIDTH), 1)
    qd_heads, scores = {}, {}
    for g in prep["order"]:
        rows = slice(g * C, (g + 1) * C)
        qd = prep["q_dec"][rows]
        qd_heads[g] = jnp.concatenate([jnp.where((lane_k >> 6) == h, qd, 0.0) for h in range(H)],
                                      axis=0).astype(BF16)
        scores[g] = _dot_nt(qd_heads[g], prep["k_inv"][rows])
    return qd_heads, scores


def _gla_chunk_updates(prep):
    C, H, G = GLA_CHUNK, GLA_HEADS, prep["G"]
    k_end, tot, vb = prep["k_end"], prep["tot"], prep["vb"]
    kv, dec_t = {}, {}
    lane = lax.broadcasted_iota(jnp.int32, (GLA_KEY_WIDTH, 2 * C), 1)
    zeros = jnp.zeros((C, GLA_DV), BF16)
    for p in range(G // 2):
        pair = slice(2 * p * C, (2 * p + 2) * C)
        ke_t = k_end[pair].T.astype(BF16)
        tot_t = tot[pair].T
        swapped = pltpu.roll(tot_t, C, 1)
        for half in range(2):
            g = 2 * p + half
            rows = slice(g * C, (g + 1) * C)
            own = (lane < C) if half == 0 else (lane >= C)
            dec_t[g] = jnp.exp(jnp.where(own, tot_t, swapped))
            parts = []
            for h in range(H):
                v_h = vb[rows, h * GLA_DV:(h + 1) * GLA_DV]
                v_pad = jnp.concatenate([v_h, zeros] if half == 0 else [zeros, v_h], axis=0)
                parts.append(_dot(ke_t[h * C:(h + 1) * C], v_pad))
            kv[g] = jnp.concatenate(parts, axis=0)
    return kv, dec_t


def _gla_states(prep, kv, dec_t, s_ref):
    st = s_ref[...]
    states = {}
    for g in prep["order"]:
        states[g] = st.astype(BF16)
        st = st * dec_t[g] + kv[g]
    s_ref[...] = st
    return states


def _gla_outputs(prep, qd_heads, scores, inter, o_ref):
    C, H = GLA_CHUNK, GLA_HEADS
    row_q = lax.broadcasted_iota(jnp.int32, (H * C, C), 0) & (C - 1)
    col_k = lax.broadcasted_iota(jnp.int32, (H * C, C), 1)
    a_mask = (col_k <= row_q) if prep["forward"] else (col_k >= row_q)
    for g in prep["order"]:
        rows = slice(g * C, (g + 1) * C)
        a = jnp.where(a_mask, scores[g], 0.0).astype(BF16)
        vv = prep["vb"][rows]
        o_ref[rows, :] = jnp.concatenate(
            [_dot(a[h * C:(h + 1) * C], vv[:, h * GLA_DV:(h + 1) * GLA_DV]) + inter[g][h * C:(h + 1) * C]
             for h in range(H)], axis=1)


def _gla_kernel(qf_ref, kf_ref, vf_ref, laf_ref, qb_ref, kb_ref, vb_ref, lab_ref,
                of_ref, ob_ref, sf_ref, sb_ref, *, G):
    @pl.when(pl.program_id(1) == 0)
    def _():
        sf_ref[...] = jnp.zeros_like(sf_ref)
        sb_ref[...] = jnp.zeros_like(sb_ref)

    dirs = [(_gla_decays(qf_ref[...], kf_ref[...], vf_ref[...], laf_ref[...], True, G), sf_ref, of_ref),
            (_gla_decays(qb_ref[...], kb_ref[...], vb_ref[...], lab_ref[...], False, G), sb_ref, ob_ref)]
    scored = [_gla_scores(prep) for prep, _, _ in dirs]
    updates = [_gla_chunk_updates(prep) for prep, _, _ in dirs]
    states = [_gla_states(prep, kv, dec_t, s_ref) for (prep, s_ref, _), (kv, dec_t) in zip(dirs, updates)]
    inters = [{g: _dot(qd_heads[g], st[g]) for g in prep["order"]}
              for (prep, _, _), (qd_heads, _), st in zip(dirs, scored, states)]
    for (prep, _, o_ref), (qd_heads, scores), inter in zip(dirs, scored, inters):
        _gla_outputs(prep, qd_heads, scores, inter, o_ref)


def _gla(gla_slab, loga, B, S, G=4):
    T = B * S
    R = G * GLA_CHUNK
    ns = S // R
    fwd = lambda col: (lambda b, i: (b * ns + i, col))
    bwd = lambda col: (lambda b, i: (b * ns + ns - 1 - i, col))
    kw, vw = GLA_KEY_WIDTH, GLA_VAL_WIDTH
    return pl.pallas_call(
        functools.partial(_gla_kernel, G=G),
        grid=(B, ns),
        in_specs=[
            pl.BlockSpec((R, kw), fwd(0)), pl.BlockSpec((R, kw), fwd(1)),
            pl.BlockSpec((R, vw), fwd(1)), pl.BlockSpec((R, kw), fwd(0)),
            pl.BlockSpec((R, kw), bwd(0)), pl.BlockSpec((R, kw), bwd(1)),
            pl.BlockSpec((R, vw), bwd(1)), pl.BlockSpec((R, kw), bwd(1)),
        ],
        out_specs=[pl.BlockSpec((R, vw), fwd(0)), pl.BlockSpec((R, vw), bwd(0))],
        out_shape=[jax.ShapeDtypeStruct((T, vw), F32), jax.ShapeDtypeStruct((T, vw), F32)],
        scratch_shapes=[pltpu.VMEM((kw, GLA_DV), F32), pltpu.VMEM((kw, GLA_DV), F32)],
        compiler_params=_cparams(("arbitrary", "arbitrary")),
        name="gla",
    )(gla_slab, gla_slab, gla_slab, loga, gla_slab, gla_slab, gla_slab, loga)


ATT_CLASSES = 4
ATT_QB = 128
ATT_KB = ATT_QB + 2 * ATT_RADIUS


ATT_UNROLL = 4


def _att_kernel(q_ref, k_ref, v_ref, o_ref, m_ref, l_ref, bias_ref, *, S):
    QB, KB, NC = ATT_QB, ATT_KB, ATT_CLASSES
    L4 = S // NC
    lane = lax.broadcasted_iota(jnp.int32, (QB, LANES), 1)
    head0 = lane < ATT_HEAD_DIM

    @pl.when((pl.program_id(0) == 0) & (pl.program_id(1) == 0))
    def _():
        rowi = lax.broadcasted_iota(jnp.int32, (2 * QB, KB), 0) & (QB - 1)
        coli = lax.broadcasted_iota(jnp.int32, (2 * QB, KB), 1)
        qpos = (rowi & (QB // NC - 1)) * NC + (rowi >> 5)
        kpos = (coli & (KB // NC - 1)) * NC + (coli >> 6)
        for case in range(3):
            bias_ref[0, case] = jnp.where(jnp.abs(rowi - coli + case * ATT_RADIUS) <= ATT_RADIUS, 0.0, NEG_INF)
            bias_ref[1, case] = jnp.where(jnp.abs(qpos - kpos + case * ATT_RADIUS) <= ATT_RADIUS, 0.0, NEG_INF)

    for pi, (_, d) in enumerate(DILATED_PATTERNS):
        L = S // d
        nb = L // QB
        shift = nb.bit_length() - 1
        first = pi == 0
        last = pi == len(DILATED_PATTERNS) - 1

        def scores(n, d=d, L=L, nb=nb, shift=shift):
            cls = n >> shift
            q0 = (n & (nb - 1)) * QB
            ws = jnp.clip(q0 - ATT_RADIUS, 0, L - KB)
            if d == 1:
                qsls = [pl.ds(pl.multiple_of(c * L4 + q0 // NC, QB // NC), QB // NC) for c in range(NC)]
                ksls = [pl.ds(pl.multiple_of(c * L4 + ws // NC, ATT_RADIUS // NC), KB // NC) for c in range(NC)]
            elif d == NC:
                qsls = [pl.ds(pl.multiple_of(cls * L4 + q0, QB), QB)]
                ksls = [pl.ds(pl.multiple_of(cls * L4 + ws, ATT_RADIUS), KB)]
            else:
                base = (cls & (NC - 1)) * L4 + (cls >> 2)
                qsls = [pl.ds(base + NC * q0, QB, stride=NC)]
                ksls = [pl.ds(base + NC * ws, KB, stride=NC)]
            q = jnp.concatenate([q_ref[sl, :] for sl in qsls], axis=0)
            kw = jnp.concatenate([k_ref[sl, :] for sl in ksls], axis=0)
            q2 = jnp.concatenate([jnp.where(head0, q, 0.0), jnp.where(head0, 0.0, q)], axis=0).astype(BF16)
            s = _dot_nt(q2, kw.astype(BF16))
            return qsls, ksls, s + bias_ref[1 if d == 1 else 0, (q0 - ws) >> 6]

        def softmax_pv(qsls, ksls, s):
            m_blk = jnp.max(s, axis=-1, keepdims=True)
            p = jnp.exp2(s - m_blk)
            l_blk = jnp.sum(p, axis=-1, keepdims=True)
            vw = jnp.concatenate([v_ref[sl, :] for sl in ksls], axis=0)
            pv = _dot(p.astype(BF16), vw.astype(BF16))
            acc_b = jnp.where(head0, pv[:QB], pv[QB:])
            m_b = jnp.where(head0, m_blk[:QB], m_blk[QB:])
            l_b = jnp.where(head0, l_blk[:QB], l_blk[QB:])
            return qsls, acc_b, m_b, l_b

        def load(ref, sls):
            return jnp.concatenate([ref[sl, :] for sl in sls], axis=0)

        def store(ref, sls, val):
            n = val.shape[0] // len(sls)
            for i, sl in enumerate(sls):
                ref[sl, :] = val[i * n:(i + 1) * n]

        def body(n, carry, first=first, last=last):
            staged = [scores(n * ATT_UNROLL + u) for u in range(ATT_UNROLL)]
            blocks = [softmax_pv(*st) for st in staged]
            for qsls, acc_b, m_b, l_b in blocks:
                if first:
                    acc, m_new, l_new = acc_b, m_b, l_b
                else:
                    m_old = load(m_ref, qsls)
                    m_new = jnp.maximum(m_old, m_b)
                    w_old = jnp.exp2(m_old - m_new)
                    w_blk = jnp.exp2(m_b - m_new)
                    acc = load(o_ref, qsls) * w_old + acc_b * w_blk
                    l_new = load(l_ref, qsls) * w_old + l_b * w_blk
                if last:
                    store(o_ref, qsls, acc / l_new)
                else:
                    store(o_ref, qsls, acc)
                    store(m_ref, qsls, m_new)
                    store(l_ref, qsls, l_new)
            return carry

        lax.fori_loop(0, S // (QB * ATT_UNROLL), body, 0)


def _attention(att_slab, B, S):
    T = B * S
    ncol = ATT_WIDTH // LANES
    return pl.pallas_call(
        functools.partial(_att_kernel, S=S),
        grid=(B, ncol),
        in_specs=[
            pl.BlockSpec((S, LANES), lambda b, h: (b, h)),
            pl.BlockSpec((S, LANES), lambda b, h: (b, ncol + h)),
            pl.BlockSpec((S, LANES), lambda b, h: (b, 2 * ncol + h)),
        ],
        out_specs=pl.BlockSpec((S, LANES), lambda b, h: (b, h)),
        out_shape=jax.ShapeDtypeStruct((T, ATT_WIDTH), F32),
        scratch_shapes=[pltpu.VMEM((S, LANES), F32), pltpu.VMEM((S, LANES), F32),
                        pltpu.VMEM((2, 3, 2 * ATT_QB, ATT_KB), F32)],
        compiler_params=_cparams(("arbitrary", "arbitrary")),
        name="dilated_attention",
    )(att_slab, att_slab, att_slab)


ROW_TILE = D_MODEL // LANES


def _to_row_tiles(ref, x):
    n = x.shape[0]
    for j in range(ROW_TILE):
        ref[pl.ds(j, n, stride=ROW_TILE), :] = x[:, j * LANES:(j + 1) * LANES]


def _from_row_tiles(ref, n):
    return jnp.concatenate([ref[pl.ds(j, n, stride=ROW_TILE), :] for j in range(ROW_TILE)], axis=1)


def _tile_copy(src_ref, src_row, dst_ref, dst_row, sem):
    src = pl.ds(pl.multiple_of(src_row * ROW_TILE, ROW_TILE), ROW_TILE)
    dst = pl.ds(pl.multiple_of(dst_row * ROW_TILE, ROW_TILE), ROW_TILE)
    return pltpu.make_async_copy(src_ref.at[src], dst_ref.at[dst], sem)


def _outproj_kernel(of_ref, ob_ref, gg_ref, att_ref, x_ref, gnw_ref, wo1_ref, wo2_ref,
                    n2_ref, wr_ref, br_ref, h_ref, u_ref, lg_ref, stage_ref):
    rows = stage_ref.shape[1] // ATT_CLASSES
    for j in range(ATT_WIDTH // LANES):
        for c in range(ATT_CLASSES):
            stage_ref[j, pl.ds(c, rows, stride=ATT_CLASSES), :] = att_ref[c, :, j * LANES:(j + 1) * LANES]
    att = jnp.concatenate([stage_ref[j] for j in range(ATT_WIDTH // LANES)], axis=1)
    o = of_ref[...] + ob_ref[...]
    gate = gg_ref[...]
    gnw = gnw_ref[...]
    parts = []
    for h in range(GLA_HEADS):
        sl = slice(h * GLA_DV, (h + 1) * GLA_DV)
        parts.append(_rms(o[:, sl], gnw))
    y = jnp.concatenate(parts, axis=1) * (gate / (1.0 + jnp.exp(-gate)))
    mix = _dot(y.astype(BF16), wo1_ref[...]) + _dot(att.astype(BF16), wo2_ref[...])
    h = x_ref[...] + mix
    h_ref[...] = h
    u = _rms(h, n2_ref[...])
    _to_row_tiles(u_ref, u)
    u_hi = u.astype(BF16)
    u_lo = (u - u_hi.astype(F32)).astype(BF16)
    hi_both = _dot(u_hi, wr_ref[...])
    lg_ref[...] = (hi_both[:, :LANES] + hi_both[:, LANES:] + _dot(u_lo, wr_ref[:, :LANES])) + br_ref[...]


def _outproj(o_f, o_b, gla_slab, att_out, x2, gla_norm_w, w_out, norm2_w, wr, br, tm=512):
    T = x2.shape[0]
    nS = att_out.shape[2] * ATT_CLASSES // tm
    row = lambda i: (i, 0)
    const = lambda i: (0, 0)
    wo = w_out.astype(BF16)
    wr_hi = wr.astype(BF16)
    wr_lo = (wr - wr_hi.astype(F32)).astype(BF16)
    wr = jnp.concatenate([wr_hi, wr_lo], axis=1)
    return pl.pallas_call(
        _outproj_kernel,
        grid=(T // tm,),
        in_specs=[
            pl.BlockSpec((tm, GLA_VAL_WIDTH), row),
            pl.BlockSpec((tm, GLA_VAL_WIDTH), row),
            pl.BlockSpec((tm, GLA_VAL_WIDTH), lambda i: (i, 2)),
            pl.BlockSpec((None, ATT_CLASSES, tm // ATT_CLASSES, ATT_WIDTH), lambda i: (i // nS, 0, i % nS, 0)),
            pl.BlockSpec((tm, D_MODEL), row),
            pl.BlockSpec((1, GLA_DV), const),
            pl.BlockSpec((GLA_VAL_WIDTH, D_MODEL), const),
            pl.BlockSpec((ATT_WIDTH, D_MODEL), const),
            pl.BlockSpec((1, D_MODEL), const),
            pl.BlockSpec((D_MODEL, 2 * LANES), const),
            pl.BlockSpec((1, LANES), const),
        ],
        out_specs=[
            pl.BlockSpec((tm, D_MODEL), row),
            pl.BlockSpec((tm * ROW_TILE, LANES), row),
            pl.BlockSpec((tm, LANES), row),
        ],
        out_shape=[
            jax.ShapeDtypeStruct((T, D_MODEL), F32),
            jax.ShapeDtypeStruct((T * ROW_TILE, LANES), F32),
            jax.ShapeDtypeStruct((T, LANES), F32),
        ],
        scratch_shapes=[pltpu.VMEM((ATT_WIDTH // LANES, tm, LANES), F32)],
        compiler_params=_cparams(("arbitrary",)),
        name="outproj",
    )(o_f, o_b, gla_slab, att_out, x2, gla_norm_w[None, :], wo[:GLA_VAL_WIDTH], wo[GLA_VAL_WIDTH:],
      norm2_w[None, :], wr, br)


INFO_E1, INFO_E2, INFO_R1, INFO_R2, INFO_W1, INFO_W2 = range(6)


def _route_kernel(lg_ref, info_ref, cnt_ref, carry_ref):
    @pl.when(pl.program_id(0) == 0)
    def _():
        carry_ref[...] = jnp.zeros_like(carry_ref)

    lg = lg_ref[...]
    tr = lg.shape[0]
    lane = lax.broadcasted_iota(jnp.int32, (tr, LANES), 1)
    big = jnp.int32(1 << 20)
    is_g = (lane >= MOE_N_EXPERTS) & (lane < MOE_N_EXPERTS + MOE_GROUPS)
    gl = jnp.where(is_g, lg, -jnp.inf)
    gmax = jnp.max(gl, axis=-1, keepdims=True)
    gsel = jnp.min(jnp.where(gl == gmax, lane - MOE_N_EXPERTS, big), axis=-1, keepdims=True)
    g_w = 1.0 / jnp.sum(jnp.where(is_g, jnp.exp(lg - gmax), 0.0), axis=-1, keepdims=True)
    in_grp = (lane < MOE_N_EXPERTS) & ((lane >> 3) == gsel)
    el = jnp.where(in_grp, lg, -jnp.inf)
    v1 = jnp.max(el, axis=-1, keepdims=True)
    i1 = jnp.min(jnp.where(el == v1, lane, big), axis=-1, keepdims=True)
    el2 = jnp.where(lane == i1, -jnp.inf, el)
    v2 = jnp.max(el2, axis=-1, keepdims=True)
    i2 = jnp.min(jnp.where(el2 == v2, lane, big), axis=-1, keepdims=True)
    t = jnp.exp(v2 - v1)
    w1 = g_w * (1.0 / (1.0 + t))
    w2 = g_w * (t / (1.0 + t))

    hit1 = lane == i1
    hit2 = lane == i2
    member = jnp.where(hit1 | hit2, 1.0, 0.0)
    r = lax.broadcasted_iota(jnp.int32, (tr, tr), 0)
    c = lax.broadcasted_iota(jnp.int32, (tr, tr), 1)
    strict = jnp.where(c < r, 1.0, 0.0).astype(BF16)
    prefix = _dot(strict, member.astype(BF16)) + carry_ref[...]
    rank1 = jnp.sum(jnp.where(hit1, prefix, 0.0), axis=-1, keepdims=True)
    rank2 = jnp.sum(jnp.where(hit2, prefix, 0.0), axis=-1, keepdims=True)
    carry = carry_ref[...] + jnp.sum(member, axis=0, keepdims=True)
    carry_ref[...] = carry
    cnt_ref[...] = carry

    info = jnp.where(lane == INFO_E1, i1.astype(F32), 0.0)
    info = jnp.where(lane == INFO_E2, i2.astype(F32), info)
    info = jnp.where(lane == INFO_R1, rank1, info)
    info = jnp.where(lane == INFO_R2, rank2, info)
    info = jnp.where(lane == INFO_W1, w1, info)
    info = jnp.where(lane == INFO_W2, w2, info)
    info_ref[...] = info


def _route(logits, tr=512):
    T = logits.shape[0]
    return pl.pallas_call(
        _route_kernel,
        grid=(T // tr,),
        in_specs=[pl.BlockSpec((tr, LANES), lambda i: (i, 0))],
        out_specs=[pl.BlockSpec((tr, LANES), lambda i: (i, 0)), pl.BlockSpec((1, LANES), lambda i: (0, 0))],
        out_shape=[jax.ShapeDtypeStruct((T, LANES), F32), jax.ShapeDtypeStruct((1, LANES), F32)],
        scratch_shapes=[pltpu.VMEM((1, LANES), F32)],
        compiler_params=_cparams(("arbitrary",)),
        name="route",
    )(logits)


ROW_UNROLL = 8


def _dispatch_kernel(dest_ref, pend_ref, u_ref, xs_ref, zbuf, sem, zsem, *, td, T, nblk):
    @pl.when(pl.program_id(0) == 0)
    def _():
        zbuf[...] = jnp.zeros_like(zbuf)
        n_used = pend_ref[MOE_N_EXPERTS - 1] >> 8

        def zero_copy(blk):
            start = pl.multiple_of(blk * (MOE_ROWS * ROW_TILE), MOE_ROWS * ROW_TILE)
            return pltpu.make_async_copy(zbuf, xs_ref.at[pl.ds(start, MOE_ROWS * ROW_TILE)], zsem)

        def each_pad_block(fn):
            def per_expert(e, carry):
                prev = jnp.where(e > 0, pend_ref[jnp.maximum(e - 1, 0)], 0)

                @pl.when(pend_ref[e] > prev)
                def _():
                    fn((pend_ref[e] >> 8) - 1)
                return carry

            def per_tail(j, carry):
                @pl.when(n_used + j < nblk)
                def _():
                    fn(n_used + j)
                return carry

            lax.fori_loop(0, MOE_N_EXPERTS, per_expert, 0)
            lax.fori_loop(0, MOE_N_EXPERTS, per_tail, 0)

        each_pad_block(lambda blk: zero_copy(blk).start())
        each_pad_block(lambda blk: zero_copy(blk).wait())

    base = pl.program_id(0) * td

    def issue(g, carry):
        for j in range(ROW_UNROLL):
            r = g * ROW_UNROLL + j
            for k in range(MOE_TOP_K):
                _tile_copy(u_ref, r, xs_ref, dest_ref[k * T + base + r], sem).start(priority=k)
        return carry

    lax.fori_loop(0, td // ROW_UNROLL, issue, 0)
    for k in range(MOE_TOP_K):
        pltpu.make_async_copy(u_ref, xs_ref.at[pl.ds(0, td * ROW_TILE)], sem).wait()


def _dispatch(dest, pend, u2, cap, td=256):
    T = u2.shape[0] // ROW_TILE
    return pl.pallas_call(
        functools.partial(_dispatch_kernel, td=td, T=T, nblk=cap // MOE_ROWS),
        grid_spec=pltpu.PrefetchScalarGridSpec(
            num_scalar_prefetch=2,
            grid=(T // td,),
            in_specs=[pl.BlockSpec((td * ROW_TILE, LANES), lambda i, d, z: (i, 0))],
            out_specs=pl.BlockSpec(memory_space=pl.ANY),
            scratch_shapes=[pltpu.VMEM((MOE_ROWS * ROW_TILE, LANES), F32),
                            pltpu.SemaphoreType.DMA(()), pltpu.SemaphoreType.DMA(())],
        ),
        out_shape=jax.ShapeDtypeStruct((cap * ROW_TILE, LANES), F32),
        compiler_params=_cparams(("arbitrary",)),
        name="dispatch",
    )(dest, pend, u2)


def _expert_kernel(pend_ref, x_ref, wg_hbm, wu_hbm, wd_hbm, y_ref,
                   stage_g, stage_u, stage_d, wgb, wub, wdb, cur_ref, sem):
    b = pl.program_id(0)
    last = MOE_N_EXPERTS - 1
    live = b < (pend_ref[last] >> 8)

    def weight_copies(e):
        return (pltpu.make_async_copy(wg_hbm.at[e], stage_g, sem.at[0]),
                pltpu.make_async_copy(wu_hbm.at[e], stage_u, sem.at[1]),
                pltpu.make_async_copy(wd_hbm.at[e], stage_d, sem.at[2]))

    def owner(start, row):
        return lax.while_loop(lambda e: (e < last) & (pend_ref[e] <= row), lambda e: e + 1, start)

    @pl.when(b == 0)
    def _():
        first = owner(0, 0)
        cur_ref[0] = -1
        for c in weight_copies(first):
            c.start()

    @pl.when(live)
    def _():
        prev = cur_ref[0]
        e = owner(jnp.maximum(prev, 0), b * MOE_ROWS)
        cur_ref[0] = e

        @pl.when(e != prev)
        def _():
            for c in weight_copies(e):
                c.wait()
            wgb[...] = stage_g[...].astype(BF16)
            wub[...] = stage_u[...].astype(BF16)
            wdb[...] = stage_d[...].astype(BF16)

            @pl.when(pend_ref[e] < pend_ref[last])
            def _():
                for c in weight_copies(owner(e + 1, pend_ref[e])):
                    c.start()

        xb = _from_row_tiles(x_ref, MOE_ROWS).astype(BF16)
        g = _dot(xb, wgb[...])
        u = _dot(xb, wub[...])
        hid = (g / (1.0 + jnp.exp(-g))) * u
        _to_row_tiles(y_ref, _dot(hid.astype(BF16), wdb[...]))

    @pl.when(jnp.logical_not(live))
    def _():
        y_ref[...] = jnp.zeros_like(y_ref)


def _experts(pend, xs, w_gate, w_up, w_down):
    cap = xs.shape[0] // ROW_TILE
    nblk = cap // MOE_ROWS
    rows = lambda b, pend: (jnp.minimum(b, (pend[MOE_N_EXPERTS - 1] >> 8) - 1), 0)
    return pl.pallas_call(
        _expert_kernel,
        grid_spec=pltpu.PrefetchScalarGridSpec(
            num_scalar_prefetch=1,
            grid=(nblk,),
            in_specs=[
                pl.BlockSpec((MOE_ROWS * ROW_TILE, LANES), rows),
                pl.BlockSpec(memory_space=pl.ANY),
                pl.BlockSpec(memory_space=pl.ANY),
                pl.BlockSpec(memory_space=pl.ANY),
            ],
            out_specs=pl.BlockSpec((MOE_ROWS * ROW_TILE, LANES), lambda b, pend: (b, 0)),
            scratch_shapes=[pltpu.VMEM((D_MODEL, MOE_D_FF), F32),
                            pltpu.VMEM((D_MODEL, MOE_D_FF), F32),
                            pltpu.VMEM((MOE_D_FF, D_MODEL), F32),
                            pltpu.VMEM((D_MODEL, MOE_D_FF), BF16),
                            pltpu.VMEM((D_MODEL, MOE_D_FF), BF16),
                            pltpu.VMEM((MOE_D_FF, D_MODEL), BF16),
                            pltpu.SMEM((1,), jnp.int32),
                            pltpu.SemaphoreType.DMA((3,))],
        ),
        out_shape=jax.ShapeDtypeStruct((cap * ROW_TILE, LANES), F32),
        compiler_params=_cparams(("arbitrary",)),
        name="experts",
    )(pend, xs, w_gate, w_up, w_down)


def _combine_kernel(dest_ref, ys_ref, info_ref, h_ref, fw_ref, o_ref, buf, sem, *, tc, T):
    i = pl.program_id(0)
    n = pl.num_programs(0)

    def issue(step, slot):
        base = step * tc

        def body(g, carry):
            for j in range(ROW_UNROLL):
                r = g * ROW_UNROLL + j
                for k in range(MOE_TOP_K):
                    _tile_copy(ys_ref, dest_ref[k * T + base + r], buf.at[slot, k], r,
                               sem.at[slot]).start(priority=k)
            return carry

        lax.fori_loop(0, tc // ROW_UNROLL, body, 0)

    @pl.when(i == 0)
    def _():
        issue(0, 0)

    slot = i % 2

    @pl.when(i + 1 < n)
    def _():
        issue(i + 1, 1 - slot)

    for k in range(MOE_TOP_K):
        pltpu.make_async_copy(ys_ref.at[pl.ds(0, tc * ROW_TILE)], buf.at[slot, k], sem.at[slot]).wait()

    info = info_ref[...]
    lane = lax.broadcasted_iota(jnp.int32, info.shape, 1)
    w1 = jnp.sum(jnp.where(lane == INFO_W1, info, 0.0), axis=-1, keepdims=True)
    w2 = jnp.sum(jnp.where(lane == INFO_W2, info, 0.0), axis=-1, keepdims=True)
    y1 = _from_row_tiles(buf.at[slot, 0], tc)
    y2 = _from_row_tiles(buf.at[slot, 1], tc)
    h = h_ref[...] + (y1 * w1 + y2 * w2)
    o_ref[...] = _rms(h, fw_ref[...])


def _combine(dest, ys, info, h, final_w, tc=256):
    T = h.shape[0]
    return pl.pallas_call(
        functools.partial(_combine_kernel, tc=tc, T=T),
        grid_spec=pltpu.PrefetchScalarGridSpec(
            num_scalar_prefetch=1,
            grid=(T // tc,),
            in_specs=[pl.BlockSpec(memory_space=pl.ANY),
                      pl.BlockSpec((tc, LANES), lambda i, d: (i, 0)),
                      pl.BlockSpec((tc, D_MODEL), lambda i, d: (i, 0)),
                      pl.BlockSpec((1, D_MODEL), lambda i, d: (0, 0))],
            out_specs=pl.BlockSpec((tc, D_MODEL), lambda i, d: (i, 0)),
            scratch_shapes=[pltpu.VMEM((2, MOE_TOP_K, tc * ROW_TILE, LANES), F32),
                            pltpu.SemaphoreType.DMA((2,))],
        ),
        out_shape=jax.ShapeDtypeStruct((T, D_MODEL), F32),
        compiler_params=_cparams(("arbitrary",)),
        name="combine",
    )(dest, ys, info, h, final_w[None, :])


def _plan_kernel(info_ref, cnt_ref, dest_ref, pend_ref):
    cnt = cnt_ref[...].astype(jnp.int32)
    nblk_e = ((cnt + (MOE_ROWS - 1)) >> 8).astype(F32)
    r = lax.broadcasted_iota(jnp.int32, (LANES, LANES), 0)
    c = lax.broadcasted_iota(jnp.int32, (LANES, LANES), 1)
    before = jnp.where(r < c, 1.0, 0.0).astype(BF16)
    first_blk = _dot(jnp.broadcast_to(nblk_e, (8, LANES)).astype(BF16), before)[0:1]
    pstart = first_blk * float(MOE_ROWS)
    pend_ref[...] = ((first_blk + nblk_e) * float(MOE_ROWS)).astype(jnp.int32)

    info = info_ref[...]
    lane = lax.broadcasted_iota(jnp.int32, info.shape, 1)
    col = lambda idx: jnp.sum(jnp.where(lane == idx, info, 0.0), axis=-1, keepdims=True)
    start_of = lambda e: jnp.sum(jnp.where(lane == e.astype(jnp.int32), pstart, 0.0), axis=-1, keepdims=True)
    d1 = col(INFO_R1) + start_of(col(INFO_E1))
    d2 = col(INFO_R2) + start_of(col(INFO_E2))
    both = jnp.where(lane == 0, d1, jnp.where(lane == 1, d2, 0.0))
    dest_ref[...] = both.T[:8].astype(jnp.int32)


def _plan(info, counts, tr=512):
    T = info.shape[0]
    dest8, pend = pl.pallas_call(
        _plan_kernel,
        grid=(T // tr,),
        in_specs=[pl.BlockSpec((tr, LANES), lambda i: (i, 0)), pl.BlockSpec((1, LANES), lambda i: (0, 0))],
        out_specs=[pl.BlockSpec((8, tr), lambda i: (0, i)), pl.BlockSpec((1, LANES), lambda i: (0, 0))],
        out_shape=[jax.ShapeDtypeStruct((8, T), jnp.int32), jax.ShapeDtypeStruct((1, LANES), jnp.int32)],
        compiler_params=_cparams(("arbitrary",)),
        name="plan",
    )(info, counts)
    return dest8[:MOE_TOP_K].reshape(-1), pend[0, :MOE_N_EXPERTS]


def _moe_capacity(T):
    return (-(-(T * MOE_TOP_K) // MOE_ROWS) + MOE_N_EXPERTS) * MOE_ROWS


def _router_weights(router_group_w, router_group_b, router_expert_w, router_expert_b):
    we = jnp.transpose(router_expert_w, (1, 0, 2)).reshape(D_MODEL, MOE_N_EXPERTS)
    pad = LANES - MOE_N_EXPERTS - MOE_GROUPS
    wr = jnp.concatenate([we, router_group_w, jnp.zeros((D_MODEL, pad), F32)], axis=1)
    br = jnp.concatenate([router_expert_b.reshape(-1), router_group_b, jnp.zeros((pad,), F32)])[None, :]
    return wr, br


def kernel(x, norm1_w, w_in, gla_fwd_gate_w, gla_fwd_gate_b, gla_bwd_gate_w, gla_bwd_gate_b,
           gla_norm_w, w_out, norm2_w, router_group_w, router_group_b, router_expert_w,
           router_expert_b, expert_w_gate, expert_w_up, expert_w_down, final_norm_w):
    B, S, D = x.shape
    T = B * S
    assert norm1_w.shape[0] == 1, "single-layer trunk: the final norm is fused into the combine step"
    h = x.reshape(T, D)
    gla_slab, loga, att_slab = _inproj(h, S, norm1_w[0], w_in[0], gla_fwd_gate_w[0], gla_fwd_gate_b[0],
                                       gla_bwd_gate_w[0], gla_bwd_gate_b[0])
    o_f, o_b = _gla(gla_slab, loga, B, S)
    att_out = _attention(att_slab.reshape(T, 3 * ATT_WIDTH), B, S)
    att_out = att_out.reshape(B, ATT_CLASSES, S // ATT_CLASSES, ATT_WIDTH)
    wr, br = _router_weights(router_group_w[0], router_group_b[0], router_expert_w[0], router_expert_b[0])
    h, u2, logits = _outproj(o_f, o_b, gla_slab, att_out, h, gla_norm_w[0], w_out[0], norm2_w[0], wr, br)
    info, counts = _route(logits)
    dest, pend = _plan(info, counts)
    xs = _dispatch(dest, pend, u2, _moe_capacity(T))
    ys = _experts(pend, xs, expert_w_gate[0], expert_w_up[0], expert_w_down[0])
    out = _combine(dest, ys, info, h, final_norm_w)
    return out.reshape(B, S, D)
```

```python
import functools

import jax
import jax.numpy as jnp
from jax import lax
from jax.experimental import pallas as pl
from jax.experimental.pallas import tpu as pltpu

F32 = jnp.float32
BF16 = jnp.bfloat16

D_MODEL = 1024
GLA_HEADS = 4
GLA_DV = 128
GLA_DK = 64
GLA_KEY_WIDTH = GLA_HEADS * GLA_DK
GLA_VAL_WIDTH = GLA_HEADS * GLA_DV
GLA_GATE_RANK = 16
GLA_TAU = 16.0
GLA_CHUNK = 64
ATT_WIDTH = 512
ATT_HEAD_DIM = 64
ATT_HEADS = 8
ROT_DIM = 16
ROPE_THETA = 500000.0
DILATED_PATTERNS = ((128, 1), (512, 4), (2048, 16))
ATT_RADIUS = 64
MOE_GROUPS = 4
MOE_EXPERTS_PER_GROUP = 8
MOE_N_EXPERTS = 32
MOE_TOP_K = 2
MOE_D_FF = 512
EPS = 1e-6
NEG_INF = -1e30
LOG2E = 1.4426950408889634

LANES = 128
MOE_ROWS = 256
VMEM_LIMIT = 56 * 1024 * 1024


def _cparams(sem):
    return pltpu.CompilerParams(dimension_semantics=sem, vmem_limit_bytes=VMEM_LIMIT)


def _dot(a, b):
    return jnp.dot(a, b, preferred_element_type=F32)


def _dot_nt(a, b):
    return lax.dot_general(a, b, (((1,), (1,)), ((), ())), preferred_element_type=F32)


def _dot_tn(a, b):
    return lax.dot_general(a, b, (((0,), (0,)), ((), ())), preferred_element_type=F32)


def _rms(x, w):
    return x * lax.rsqrt(jnp.mean(x * x, axis=-1, keepdims=True) + EPS) * w


def _inproj_kernel(x_ref, n1_ref, wg_ref, wlr_ref, wa_ref, gw_ref, gb_ref,
                   rc_ref, rs1_ref, rs2_ref, gla_ref, loga_ref, att_ref, stage_ref):
    x = x_ref[...]
    ub = _rms(x, n1_ref[...]).astype(BF16)
    g = _dot(ub, wg_ref[...])
    gla_ref[:, :GLA_KEY_WIDTH] = g[:, :GLA_KEY_WIDTH] * (GLA_DK ** -0.5)
    gla_ref[:, GLA_KEY_WIDTH:] = g[:, GLA_KEY_WIDTH:]
    lr = _dot(ub, wlr_ref[...])
    gate = _dot(lr.astype(BF16), gw_ref[...]) + gb_ref[...]
    loga_ref[...] = (jnp.minimum(gate, 0.0) - jnp.log(1.0 + jnp.exp(-jnp.abs(gate)))) * (1.0 / GLA_TAU)
    a = _dot(ub, wa_ref[...])
    qk = a[:, :2 * ATT_WIDTH]
    reps = 2 * ATT_WIDTH // LANES
    c = jnp.concatenate([rc_ref[...]] * reps, axis=1)
    s1 = jnp.concatenate([rs1_ref[...]] * reps, axis=1)
    s2 = jnp.concatenate([rs2_ref[...]] * reps, axis=1)
    half = ROT_DIM // 2
    n = 2 * ATT_WIDTH
    roped = qk * c + pltpu.roll(qk, n - half, 1) * s1 + pltpu.roll(qk, half, 1) * s2
    qkv = jnp.concatenate([roped[:, :ATT_WIDTH] * (ATT_HEAD_DIM ** -0.5 * LOG2E), roped[:, ATT_WIDTH:],
                           a[:, 2 * ATT_WIDTH:]], axis=1)
    rows = x.shape[0] // ATT_CLASSES
    for j in range(3 * ATT_WIDTH // LANES):
        cols = slice(j * LANES, (j + 1) * LANES)
        stage_ref[j] = qkv[:, cols]
        for c in range(ATT_CLASSES):
            att_ref[c, :, cols] = stage_ref[j, pl.ds(c, rows, stride=ATT_CLASSES), :]


def _rope_lane_tables(S):
    half = ROT_DIM // 2
    inv = ROPE_THETA ** (-(jnp.arange(0, ROT_DIM, 2, dtype=F32) / ROT_DIM))
    ang = inv[:, None] * jnp.arange(S, dtype=F32)[None, :]
    cos, sin = jnp.cos(ang), jnp.sin(ang)
    lane = jnp.arange(LANES) % ATT_HEAD_DIM
    freq = jnp.arange(half)[:, None]
    first = ((lane[None, :] == freq)).astype(F32)
    second = ((lane[None, :] == freq + half)).astype(F32)
    expand = lambda t, sel: lax.dot_general(t, sel, (((0,), (0,)), ((), ())), precision=lax.Precision.HIGHEST)
    rest = (lane >= ROT_DIM).astype(F32)[None, :]
    return expand(cos, first + second) + rest, expand(-sin, first), expand(sin, second)


def _inproj(x2, S, norm1_w, w_in, wf, bfw, wb, bbw, tm=512):
    T = x2.shape[0]
    o_lr = 2 * GLA_KEY_WIDTH + 2 * GLA_VAL_WIDTH
    o_att = o_lr + 2 * GLA_GATE_RANK
    w_main = w_in[:, :o_lr + LANES].astype(BF16)
    wa = w_in[:, o_att:].astype(BF16)
    zeros = jnp.zeros((GLA_GATE_RANK, GLA_KEY_WIDTH), F32)
    gw = jnp.concatenate([jnp.concatenate([wf, zeros], axis=1), jnp.concatenate([zeros, wb], axis=1),
                          jnp.zeros((LANES - 2 * GLA_GATE_RANK, 2 * GLA_KEY_WIDTH), F32)], axis=0).astype(BF16)
    gb = jnp.concatenate([bfw, bbw])[None, :]
    rc, rs1, rs2 = _rope_lane_tables(S)
    nS = S // tm
    row = lambda i: (i, 0)
    const = lambda i: (0, 0)
    pos = lambda i: (i % nS, 0)
    return pl.pallas_call(
        _inproj_kernel,
        grid=(T // tm,),
        in_specs=[
            pl.BlockSpec((tm, D_MODEL), row),
            pl.BlockSpec((1, D_MODEL), const),
            pl.BlockSpec((D_MODEL, o_lr), const),
            pl.BlockSpec((D_MODEL, LANES), lambda i: (0, o_lr // LANES)),
            pl.BlockSpec((D_MODEL, 3 * ATT_WIDTH), const),
            pl.BlockSpec((LANES, 2 * GLA_KEY_WIDTH), const),
            pl.BlockSpec((1, 2 * GLA_KEY_WIDTH), const),
            pl.BlockSpec((tm, LANES), pos),
            pl.BlockSpec((tm, LANES), pos),
            pl.BlockSpec((tm, LANES), pos),
        ],
        out_specs=[
            pl.BlockSpec((tm, o_lr), row),
            pl.BlockSpec((tm, 2 * GLA_KEY_WIDTH), row),
            pl.BlockSpec((None, ATT_CLASSES, tm // ATT_CLASSES, 3 * ATT_WIDTH),
                         lambda i: (i // nS, 0, i % nS, 0)),
        ],
        out_shape=[
            jax.ShapeDtypeStruct((T, o_lr), F32),
            jax.ShapeDtypeStruct((T, 2 * GLA_KEY_WIDTH), F32),
            jax.ShapeDtypeStruct((T // S, ATT_CLASSES, S // ATT_CLASSES, 3 * ATT_WIDTH), F32),
        ],
        scratch_shapes=[pltpu.VMEM((3 * ATT_WIDTH // LANES, tm, LANES), F32)],
        compiler_params=_cparams(("arbitrary",)),
        name="inproj",
    )(x2, norm1_w[None, :], w_main, w_main, wa, gw, gb, rc, rs1, rs2)


def _gla_decays(q, k, v, la, forward, G):
    C = GLA_CHUNK
    R = G * C
    r = lax.broadcasted_iota(jnp.int32, (R, R), 0)
    c = lax.broadcasted_iota(jnp.int32, (R, R), 1)
    same = (r >> 6) == (c >> 6)
    tri = (c <= r) if forward else (c >= r)
    t_mat = jnp.where(same, jnp.where(tri, 1.0, 0.0), 0.0).astype(BF16)
    hi = la.astype(BF16)
    lo = (la - hi.astype(F32)).astype(BF16)
    b = _dot(t_mat, hi) + _dot(t_mat, lo)
    edge = C - 1 if forward else 0
    tot = jnp.concatenate([jnp.broadcast_to(b[g * C + edge:g * C + edge + 1], (C, GLA_KEY_WIDTH))
                           for g in range(G)], axis=0)
    order = list(range(G)) if forward else list(range(G - 1, -1, -1))
    return dict(q_dec=q * jnp.exp(b), k_inv=(k * jnp.exp(-b)).astype(BF16), k_end=k * jnp.exp(tot - b),
                tot=tot, vb=v.astype(BF16), order=order, forward=forward, G=G)


def _gla_scores(prep):
    C, H = GLA_CHUNK, GLA_HEADS
    lane_k = lax.broadcasted_iota(jnp.int32, (C, GLA_KEY_WIDTH), 1)
    qd_heads, scores = {}, {}
    for g in prep["order"]:
        rows = slice(g * C, (g + 1) * C)
        qd = prep["q_dec"][rows]
        qd_heads[g] = jnp.concatenate([jnp.where((lane_k >> 6) == h, qd, 0.0) for h in range(H)],
                                      axis=0).astype(BF16)
        scores[g] = _dot_nt(qd_heads[g], prep["k_inv"][rows])
    return qd_heads, scores


def _gla_chunk_updates(prep):
    C, H, G = GLA_CHUNK, GLA_HEADS, prep["G"]
    k_end, tot, vb = prep["k_end"], prep["tot"], prep["vb"]
    kv, dec_t = {}, {}
    lane = lax.broadcasted_iota(jnp.int32, (GLA_KEY_WIDTH, 2 * C), 1)
    zeros = jnp.zeros((C, GLA_DV), BF16)
    for p in range(G // 2):
        pair = slice(2 * p * C, (2 * p + 2) * C)
        ke_t = k_end[pair].T.astype(BF16)
        tot_t = tot[pair].T
        swapped = pltpu.roll(tot_t, C, 1)
        for half in range(2):
            g = 2 * p + half
            rows = slice(g * C, (g + 1) * C)
            own = (lane < C) if half == 0 else (lane >= C)
            dec_t[g] = jnp.exp(jnp.where(own, tot_t, swapped))
            parts = []
            for h in range(H):
                v_h = vb[rows, h * GLA_DV:(h + 1) * GLA_DV]
                v_pad = jnp.concatenate([v_h, zeros] if half == 0 else [zeros, v_h], axis=0)
                parts.append(_dot(ke_t[h * C:(h + 1) * C], v_pad))
            kv[g] = jnp.concatenate(parts, axis=0)
    return kv, dec_t


def _gla_states(prep, kv, dec_t, s_ref):
    st = s_ref[...]
    states = {}
    for g in prep["order"]:
        states[g] = st.astype(BF16)
        st = st * dec_t[g] + kv[g]
    s_ref[...] = st
    return states


def _gla_outputs(prep, qd_heads, scores, inter, o_ref):
    C, H = GLA_CHUNK, GLA_HEADS
    row_q = lax.broadcasted_iota(jnp.int32, (H * C, C), 0) & (C - 1)
    col_k = lax.broadcasted_iota(jnp.int32, (H * C, C), 1)
    a_mask = (col_k <= row_q) if prep["forward"] else (col_k >= row_q)
    for g in prep["order"]:
        rows = slice(g * C, (g + 1) * C)
        a = jnp.where(a_mask, scores[g], 0.0).astype(BF16)
        vv = prep["vb"][rows]
        o_ref[rows, :] = jnp.concatenate(
            [_dot(a[h * C:(h + 1) * C], vv[:, h * GLA_DV:(h + 1) * GLA_DV]) + inter[g][h * C:(h + 1) * C]
             for h in range(H)], axis=1)


def _gla_kernel(qf_ref, kf_ref, vf_ref, laf_ref, qb_ref, kb_ref, vb_ref, lab_ref,
                of_ref, ob_ref, sf_ref, sb_ref, *, G):
    @pl.when(pl.program_id(1) == 0)
    def _():
        sf_ref[...] = jnp.zeros_like(sf_ref)
        sb_ref[...] = jnp.zeros_like(sb_ref)

    dirs = [(_gla_decays(qf_ref[...], kf_ref[...], vf_ref[...], laf_ref[...], True, G), sf_ref, of_ref),
            (_gla_decays(qb_ref[...], kb_ref[...], vb_ref[...], lab_ref[...], False, G), sb_ref, ob_ref)]
    scored = [_gla_scores(prep) for prep, _, _ in dirs]
    updates = [_gla_chunk_updates(prep) for prep, _, _ in dirs]
    states = [_gla_states(prep, kv, dec_t, s_ref) for (prep, s_ref, _), (kv, dec_t) in zip(dirs, updates)]
    inters = [{g: _dot(qd_heads[g], st[g]) for g in prep["order"]}
              for (prep, _, _), (qd_heads, _), st in zip(dirs, scored, states)]
    for (prep, _, o_ref), (qd_heads, scores), inter in zip(dirs, scored, inters):
        _gla_outputs(prep, qd_heads, scores, inter, o_ref)


def _gla(gla_slab, loga, B, S, G=4):
    T = B * S
    R = G * GLA_CHUNK
    ns = S // R
    fwd = lambda col: (lambda b, i: (b * ns + i, col))
    bwd = lambda col: (lambda b, i: (b * ns + ns - 1 - i, col))
    kw, vw = GLA_KEY_WIDTH, GLA_VAL_WIDTH
    return pl.pallas_call(
        functools.partial(_gla_kernel, G=G),
        grid=(B, ns),
        in_specs=[
            pl.BlockSpec((R, kw), fwd(0)), pl.BlockSpec((R, kw), fwd(1)),
            pl.BlockSpec((R, vw), fwd(1)), pl.BlockSpec((R, kw), fwd(0)),
            pl.BlockSpec((R, kw), bwd(0)), pl.BlockSpec((R, kw), bwd(1)),
            pl.BlockSpec((R, vw), bwd(1)), pl.BlockSpec((R, kw), bwd(1)),
        ],
        out_specs=[pl.BlockSpec((R, vw), fwd(0)), pl.BlockSpec((R, vw), bwd(0))],
        out_shape=[jax.ShapeDtypeStruct((T, vw), F32), jax.ShapeDtypeStruct((T, vw), F32)],
        scratch_shapes=[pltpu.VMEM((kw, GLA_DV), F32), pltpu.VMEM((kw, GLA_DV), F32)],
        compiler_params=_cparams(("arbitrary", "arbitrary")),
        name="gla",
    )(gla_slab, gla_slab, gla_slab, loga, gla_slab, gla_slab, gla_slab, loga)


ATT_CLASSES = 4
ATT_QB = 128
ATT_KB = ATT_QB + 2 * ATT_RADIUS


ATT_UNROLL = 4


def _att_kernel(q_ref, k_ref, v_ref, o_ref, m_ref, l_ref, bias_ref, *, S):
    QB, KB, NC = ATT_QB, ATT_KB, ATT_CLASSES
    L4 = S // NC
    lane = lax.broadcasted_iota(jnp.int32, (QB, LANES), 1)
    head0 = lane < ATT_HEAD_DIM

    @pl.when((pl.program_id(0) == 0) & (pl.program_id(1) == 0))
    def _():
        rowi = lax.broadcasted_iota(jnp.int32, (2 * QB, KB), 0) & (QB - 1)
        coli = lax.broadcasted_iota(jnp.int32, (2 * QB, KB), 1)
        qpos = (rowi & (QB // NC - 1)) * NC + (rowi >> 5)
        kpos = (coli & (KB // NC - 1)) * NC + (coli >> 6)
        for case in range(3):
            bias_ref[0, case] = jnp.where(jnp.abs(rowi - coli + case * ATT_RADIUS) <= ATT_RADIUS, 0.0, NEG_INF)
            bias_ref[1, case] = jnp.where(jnp.abs(qpos - kpos + case * ATT_RADIUS) <= ATT_RADIUS, 0.0, NEG_INF)

    for pi, (_, d) in enumerate(DILATED_PATTERNS):
        L = S // d
        nb = L // QB
        shift = nb.bit_length() - 1
        first = pi == 0
        last = pi == len(DILATED_PATTERNS) - 1

        def scores(n, d=d, L=L, nb=nb, shift=shift):
            cls = n >> shift
            q0 = (n & (nb - 1)) * QB
            ws = jnp.clip(q0 - ATT_RADIUS, 0, L - KB)
            if d == 1:
                qsls = [pl.ds(pl.multiple_of(c * L4 + q0 // NC, QB // NC), QB // NC) for c in range(NC)]
                ksls = [pl.ds(pl.multiple_of(c * L4 + ws // NC, ATT_RADIUS // NC), KB // NC) for c in range(NC)]
            elif d == NC:
                qsls = [pl.ds(pl.multiple_of(cls * L4 + q0, QB), QB)]
                ksls = [pl.ds(pl.multiple_of(cls * L4 + ws, ATT_RADIUS), KB)]
            else:
                base = (cls & (NC - 1)) * L4 + (cls >> 2)
                qsls = [pl.ds(base + NC * q0, QB, stride=NC)]
                ksls = [pl.ds(base + NC * ws, KB, stride=NC)]
            q = jnp.concatenate([q_ref[sl, :] for sl in qsls], axis=0)
            kw = jnp.concatenate([k_ref[sl, :] for sl in ksls], axis=0)
            q2 = jnp.concatenate([jnp.where(head0, q, 0.0), jnp.where(head0, 0.0, q)], axis=0).astype(BF16)
            s = _dot_nt(q2, kw.astype(BF16))
            return qsls, ksls, s + bias_ref[1 if d == 1 else 0, (q0 - ws) >> 6]

        def softmax_pv(qsls, ksls, s):
            m_blk = jnp.max(s, axis=-1, keepdims=True)
            p = jnp.exp2(s - m_blk)
            l_blk = jnp.sum(p, axis=-1, keepdims=True)
            vw = jnp.concatenate([v_ref[sl, :] for sl in ksls], axis=0)
            pv = _dot(p.astype(BF16), vw.astype(BF16))
            acc_b = jnp.where(head0, pv[:QB], pv[QB:])
            m_b = jnp.where(head0, m_blk[:QB], m_blk[QB:])
            l_b = jnp.where(head0, l_blk[:QB], l_blk[QB:])
            return qsls, acc_b, m_b, l_b

        def load(ref, sls):
            return jnp.concatenate([ref[sl, :] for sl in sls], axis=0)

        def store(ref, sls, val):
            n = val.shape[0] // len(sls)
            for i, sl in enumerate(sls):
                ref[sl, :] = val[i * n:(i + 1) * n]

        def body(n, carry, first=first, last=last):
            staged = [scores(n * ATT_UNROLL + u) for u in range(ATT_UNROLL)]
            blocks = [softmax_pv(*st) for st in staged]
            for qsls, acc_b, m_b, l_b in blocks:
                if first:
                    acc, m_new, l_new = acc_b, m_b, l_b
                else:
                    m_old = load(m_ref, qsls)
                    m_new = jnp.maximum(m_old, m_b)
                    w_old = jnp.exp2(m_old - m_new)
                    w_blk = jnp.exp2(m_b - m_new)
                    acc = load(o_ref, qsls) * w_old + acc_b * w_blk
                    l_new = load(l_ref, qsls) * w_old + l_b * w_blk
                if last:
                    store(o_ref, qsls, acc / l_new)
                else:
                    store(o_ref, qsls, acc)
                    store(m_ref, qsls, m_new)
                    store(l_ref, qsls, l_new)
            return carry

        lax.fori_loop(0, S // (QB * ATT_UNROLL), body, 0)


def _attention(att_slab, B, S):
    T = B * S
    ncol = ATT_WIDTH // LANES
    return pl.pallas_call(
        functools.partial(_att_kernel, S=S),
        grid=(B, ncol),
        in_specs=[
            pl.BlockSpec((S, LANES), lambda b, h: (b, h)),
            pl.BlockSpec((S, LANES), lambda b, h: (b, ncol + h)),
            pl.BlockSpec((S, LANES), lambda b, h: (b, 2 * ncol + h)),
        ],
        out_specs=pl.BlockSpec((S, LANES), lambda b, h: (b, h)),
        out_shape=jax.ShapeDtypeStruct((T, ATT_WIDTH), F32),
        scratch_shapes=[pltpu.VMEM((S, LANES), F32), pltpu.VMEM((S, LANES), F32),
                        pltpu.VMEM((2, 3, 2 * ATT_QB, ATT_KB), F32)],
        compiler_params=_cparams(("arbitrary", "arbitrary")),
        name="dilated_attention",
    )(att_slab, att_slab, att_slab)


ROW_TILE = D_MODEL // LANES


def _to_row_tiles(ref, x):
    n = x.shape[0]
    for j in range(ROW_TILE):
        ref[pl.ds(j, n, stride=ROW_TILE), :] = x[:, j * LANES:(j + 1) * LANES]


def _from_row_tiles(ref, n):
    return jnp.concatenate([ref[pl.ds(j, n, stride=ROW_TILE), :] for j in range(ROW_TILE)], axis=1)


def _tile_copy(src_ref, src_row, dst_ref, dst_row, sem):
    src = pl.ds(pl.multiple_of(src_row * ROW_TILE, ROW_TILE), ROW_TILE)
    dst = pl.ds(pl.multiple_of(dst_row * ROW_TILE, ROW_TILE), ROW_TILE)
    return pltpu.make_async_copy(src_ref.at[src], dst_ref.at[dst], sem)


def _outproj_kernel(of_ref, ob_ref, gg_ref, att_ref, x_ref, gnw_ref, wo1_ref, wo2_ref,
                    n2_ref, wr_ref, br_ref, h_ref, u_ref, lg_ref, stage_ref):
    rows = stage_ref.shape[1] // ATT_CLASSES
    for j in range(ATT_WIDTH // LANES):
        for c in range(ATT_CLASSES):
            stage_ref[j, pl.ds(c, rows, stride=ATT_CLASSES), :] = att_ref[c, :, j * LANES:(j + 1) * LANES]
    att = jnp.concatenate([stage_ref[j] for j in range(ATT_WIDTH // LANES)], axis=1)
    o = of_ref[...] + ob_ref[...]
    gate = gg_ref[...]
    gnw = gnw_ref[...]
    parts = []
    for h in range(GLA_HEADS):
        sl = slice(h * GLA_DV, (h + 1) * GLA_DV)
        parts.append(_rms(o[:, sl], gnw))
    y = jnp.concatenate(parts, axis=1) * (gate / (1.0 + jnp.exp(-gate)))
    mix = _dot(y.astype(BF16), wo1_ref[...]) + _dot(att.astype(BF16), wo2_ref[...])
    h = x_ref[...] + mix
    h_ref[...] = h
    u = _rms(h, n2_ref[...])
    _to_row_tiles(u_ref, u)
    u_hi = u.astype(BF16)
    u_lo = (u - u_hi.astype(F32)).astype(BF16)
    hi_both = _dot_nt(wr_ref[...], u_hi)
    lg_ref[...] = (hi_both[:LANES] + hi_both[LANES:] + _dot_nt(wr_ref[:LANES], u_lo)) + br_ref[...]


def _outproj(o_f, o_b, gla_slab, att_out, x2, gla_norm_w, w_out, norm2_w, wr, br, tm=512):
    T = x2.shape[0]
    nS = att_out.shape[2] * ATT_CLASSES // tm
    row = lambda i: (i, 0)
    const = lambda i: (0, 0)
    wo = w_out.astype(BF16)
    wr_hi = wr.astype(BF16)
    wr_lo = (wr - wr_hi.astype(F32)).astype(BF16)
    wr = jnp.concatenate([wr_hi, wr_lo], axis=0)
    return pl.pallas_call(
        _outproj_kernel,
        grid=(T // tm,),
        in_specs=[
            pl.BlockSpec((tm, GLA_VAL_WIDTH), row),
            pl.BlockSpec((tm, GLA_VAL_WIDTH), row),
            pl.BlockSpec((tm, GLA_VAL_WIDTH), lambda i: (i, 2)),
            pl.BlockSpec((None, ATT_CLASSES, tm // ATT_CLASSES, ATT_WIDTH), lambda i: (i // nS, 0, i % nS, 0)),
            pl.BlockSpec((tm, D_MODEL), row),
            pl.BlockSpec((1, GLA_DV), const),
            pl.BlockSpec((GLA_VAL_WIDTH, D_MODEL), const),
            pl.BlockSpec((ATT_WIDTH, D_MODEL), const),
            pl.BlockSpec((1, D_MODEL), const),
            pl.BlockSpec((2 * LANES, D_MODEL), const),
            pl.BlockSpec((LANES, 1), const),
        ],
        out_specs=[
            pl.BlockSpec((tm, D_MODEL), row),
            pl.BlockSpec((tm * ROW_TILE, LANES), row),
            pl.BlockSpec((LANES, tm), lambda i: (0, i)),
        ],
        out_shape=[
            jax.ShapeDtypeStruct((T, D_MODEL), F32),
            jax.ShapeDtypeStruct((T * ROW_TILE, LANES), F32),
            jax.ShapeDtypeStruct((LANES, T), F32),
        ],
        scratch_shapes=[pltpu.VMEM((ATT_WIDTH // LANES, tm, LANES), F32)],
        compiler_params=_cparams(("arbitrary",)),
        name="outproj",
    )(o_f, o_b, gla_slab, att_out, x2, gla_norm_w[None, :], wo[:GLA_VAL_WIDTH], wo[GLA_VAL_WIDTH:],
      norm2_w[None, :], wr, br)


INFO_E1, INFO_E2, INFO_R1, INFO_R2, INFO_W1, INFO_W2 = range(6)
ROUTE_ROWS = 40


def _route_kernel(lg_ref, info_ref, cnt_ref, carry_ref):
    @pl.when(pl.program_id(0) == 0)
    def _():
        carry_ref[...] = jnp.zeros_like(carry_ref)

    lg = lg_ref[:ROUTE_ROWS, :]
    tr = lg.shape[1]
    row = lax.broadcasted_iota(jnp.int32, (ROUTE_ROWS, tr), 0)
    big = jnp.int32(1 << 20)
    is_g = (row >= MOE_N_EXPERTS) & (row < MOE_N_EXPERTS + MOE_GROUPS)
    gl = jnp.where(is_g, lg, -jnp.inf)
    gmax = jnp.max(gl, axis=0, keepdims=True)
    gsel = jnp.min(jnp.where(gl == gmax, row - MOE_N_EXPERTS, big), axis=0, keepdims=True)
    g_w = 1.0 / jnp.sum(jnp.where(is_g, jnp.exp(lg - gmax), 0.0), axis=0, keepdims=True)
    in_grp = (row < MOE_N_EXPERTS) & ((row >> 3) == gsel)
    el = jnp.where(in_grp, lg, -jnp.inf)
    v1 = jnp.max(el, axis=0, keepdims=True)
    i1 = jnp.min(jnp.where(el == v1, row, big), axis=0, keepdims=True)
    el2 = jnp.where(row == i1, -jnp.inf, el)
    v2 = jnp.max(el2, axis=0, keepdims=True)
    i2 = jnp.min(jnp.where(el2 == v2, row, big), axis=0, keepdims=True)
    t = jnp.exp(v2 - v1)
    w1 = g_w * (1.0 / (1.0 + t))
    w2 = g_w * (t / (1.0 + t))

    erow = lax.broadcasted_iota(jnp.int32, (MOE_N_EXPERTS, tr), 0)
    hit1 = erow == i1
    hit2 = erow == i2
    member = jnp.where(hit1 | hit2, 1.0, 0.0)
    r = lax.broadcasted_iota(jnp.int32, (tr, tr), 0)
    c = lax.broadcasted_iota(jnp.int32, (tr, tr), 1)
    earlier = jnp.where(r < c, 1.0, 0.0).astype(BF16)
    carry = carry_ref[...]
    prefix = _dot(member.astype(BF16), earlier) + carry[:, 0:1]
    rank1 = jnp.sum(jnp.where(hit1, prefix, 0.0), axis=0, keepdims=True)
    rank2 = jnp.sum(jnp.where(hit2, prefix, 0.0), axis=0, keepdims=True)
    carry = carry + jnp.sum(member, axis=1, keepdims=True)
    carry_ref[...] = carry
    cnt_ref[...] = carry

    zero = jnp.zeros_like(w1)
    info_ref[...] = jnp.concatenate([i1.astype(F32), i2.astype(F32), rank1, rank2, w1, w2, zero, zero], axis=0)


def _route(logits_t, tr=512):
    T = logits_t.shape[1]
    return pl.pallas_call(
        _route_kernel,
        grid=(T // tr,),
        in_specs=[pl.BlockSpec((LANES, tr), lambda i: (0, i))],
        out_specs=[pl.BlockSpec((8, tr), lambda i: (0, i)),
                   pl.BlockSpec((MOE_N_EXPERTS, LANES), lambda i: (0, 0))],
        out_shape=[jax.ShapeDtypeStruct((8, T), F32), jax.ShapeDtypeStruct((MOE_N_EXPERTS, LANES), F32)],
        scratch_shapes=[pltpu.VMEM((MOE_N_EXPERTS, LANES), F32)],
        compiler_params=_cparams(("arbitrary",)),
        name="route",
    )(logits_t)


ROW_UNROLL = 8


def _dispatch_kernel(dest_ref, pend_ref, u_ref, xs_ref, zbuf, sem, zsem, *, td, T, nblk):
    @pl.when(pl.program_id(0) == 0)
    def _():
        zbuf[...] = jnp.zeros_like(zbuf)
        n_used = pend_ref[MOE_N_EXPERTS - 1] >> 8

        def zero_copy(blk):
            start = pl.multiple_of(blk * (MOE_ROWS * ROW_TILE), MOE_ROWS * ROW_TILE)
            return pltpu.make_async_copy(zbuf, xs_ref.at[pl.ds(start, MOE_ROWS * ROW_TILE)], zsem)

        def each_pad_block(fn):
            def per_expert(e, carry):
                prev = jnp.where(e > 0, pend_ref[jnp.maximum(e - 1, 0)], 0)

                @pl.when(pend_ref[e] > prev)
                def _():
                    fn((pend_ref[e] >> 8) - 1)
                return carry

            def per_tail(j, carry):
                @pl.when(n_used + j < nblk)
                def _():
                    fn(n_used + j)
                return carry

            lax.fori_loop(0, MOE_N_EXPERTS, per_expert, 0)
            lax.fori_loop(0, MOE_N_EXPERTS, per_tail, 0)

        each_pad_block(lambda blk: zero_copy(blk).start())
        each_pad_block(lambda blk: zero_copy(blk).wait())

    base = pl.program_id(0) * td

    def issue(g, carry):
        for j in range(ROW_UNROLL):
            r = g * ROW_UNROLL + j
            for k in range(MOE_TOP_K):
                _tile_copy(u_ref, r, xs_ref, dest_ref[k * T + base + r], sem).start(priority=k)
        return carry

    lax.fori_loop(0, td // ROW_UNROLL, issue, 0)
    for k in range(MOE_TOP_K):
        pltpu.make_async_copy(u_ref, xs_ref.at[pl.ds(0, td * ROW_TILE)], sem).wait()


def _dispatch(dest, pend, u2, cap, td=256):
    T = u2.shape[0] // ROW_TILE
    return pl.pallas_call(
        functools.partial(_dispatch_kernel, td=td, T=T, nblk=cap // MOE_ROWS),
        grid_spec=pltpu.PrefetchScalarGridSpec(
            num_scalar_prefetch=2,
            grid=(T // td,),
            in_specs=[pl.BlockSpec((td * ROW_TILE, LANES), lambda i, d, z: (i, 0))],
            out_specs=pl.BlockSpec(memory_space=pl.ANY),
            scratch_shapes=[pltpu.VMEM((MOE_ROWS * ROW_TILE, LANES), F32),
                            pltpu.SemaphoreType.DMA(()), pltpu.SemaphoreType.DMA(())],
        ),
        out_shape=jax.ShapeDtypeStruct((cap * ROW_TILE, LANES), F32),
        compiler_params=_cparams(("arbitrary",)),
        name="dispatch",
    )(dest, pend, u2)


def _expert_kernel(pend_ref, x_ref, wg_hbm, wu_hbm, wd_hbm, y_ref,
                   stage_g, stage_u, stage_d, wgb, wub, wdb, cur_ref, sem):
    b = pl.program_id(0)
    last = MOE_N_EXPERTS - 1
    live = b < (pend_ref[last] >> 8)

    def weight_copies(e):
        return (pltpu.make_async_copy(wg_hbm.at[e], stage_g, sem.at[0]),
                pltpu.make_async_copy(wu_hbm.at[e], stage_u, sem.at[1]),
                pltpu.make_async_copy(wd_hbm.at[e], stage_d, sem.at[2]))

    def owner(start, row):
        return lax.while_loop(lambda e: (e < last) & (pend_ref[e] <= row), lambda e: e + 1, start)

    @pl.when(b == 0)
    def _():
        first = owner(0, 0)
        cur_ref[0] = -1
        for c in weight_copies(first):
            c.start()

    @pl.when(live)
    def _():
        prev = cur_ref[0]
        e = owner(jnp.maximum(prev, 0), b * MOE_ROWS)
        cur_ref[0] = e

        @pl.when(e != prev)
        def _():
            for c in weight_copies(e):
                c.wait()
            wgb[...] = stage_g[...].astype(BF16)
            wub[...] = stage_u[...].astype(BF16)
            wdb[...] = stage_d[...].astype(BF16)

            @pl.when(pend_ref[e] < pend_ref[last])
            def _():
                for c in weight_copies(owner(e + 1, pend_ref[e])):
                    c.start()

        xb = _from_row_tiles(x_ref, MOE_ROWS).astype(BF16)
        g = _dot(xb, wgb[...])
        u = _dot(xb, wub[...])
        hid = (g / (1.0 + jnp.exp(-g))) * u
        _to_row_tiles(y_ref, _dot(hid.astype(BF16), wdb[...]))

    @pl.when(jnp.logical_not(live))
    def _():
        y_ref[...] = jnp.zeros_like(y_ref)


def _experts(pend, xs, w_gate, w_up, w_down):
    cap = xs.shape[0] // ROW_TILE
    nblk = cap // MOE_ROWS
    rows = lambda b, pend: (jnp.minimum(b, (pend[MOE_N_EXPERTS - 1] >> 8) - 1), 0)
    return pl.pallas_call(
        _expert_kernel,
        grid_spec=pltpu.PrefetchScalarGridSpec(
            num_scalar_prefetch=1,
            grid=(nblk,),
            in_specs=[
                pl.BlockSpec((MOE_ROWS * ROW_TILE, LANES), rows),
                pl.BlockSpec(memory_space=pl.ANY),
                pl.BlockSpec(memory_space=pl.ANY),
                pl.BlockSpec(memory_space=pl.ANY),
            ],
            out_specs=pl.BlockSpec((MOE_ROWS * ROW_TILE, LANES), lambda b, pend: (b, 0)),
            scratch_shapes=[pltpu.VMEM((D_MODEL, MOE_D_FF), F32),
                            pltpu.VMEM((D_MODEL, MOE_D_FF), F32),
                            pltpu.VMEM((MOE_D_FF, D_MODEL), F32),
                            pltpu.VMEM((D_MODEL, MOE_D_FF), BF16),
                            pltpu.VMEM((D_MODEL, MOE_D_FF), BF16),
                            pltpu.VMEM((MOE_D_FF, D_MODEL), BF16),
                            pltpu.SMEM((1,), jnp.int32),
                            pltpu.SemaphoreType.DMA((3,))],
        ),
        out_shape=jax.ShapeDtypeStruct((cap * ROW_TILE, LANES), F32),
        compiler_params=_cparams(("arbitrary",)),
        name="experts",
    )(pend, xs, w_gate, w_up, w_down)


def _combine_kernel(dest_ref, ys_ref, info_ref, h_ref, fw_ref, o_ref, buf, sem, *, tc, T):
    i = pl.program_id(0)
    n = pl.num_programs(0)

    def issue(step, slot):
        base = step * tc

        def body(g, carry):
            for j in range(ROW_UNROLL):
                r = g * ROW_UNROLL + j
                for k in range(MOE_TOP_K):
                    _tile_copy(ys_ref, dest_ref[k * T + base + r], buf.at[slot, k], r,
                               sem.at[slot]).start(priority=k)
            return carry

        lax.fori_loop(0, tc // ROW_UNROLL, body, 0)

    @pl.when(i == 0)
    def _():
        issue(0, 0)

    slot = i % 2

    @pl.when(i + 1 < n)
    def _():
        issue(i + 1, 1 - slot)

    for k in range(MOE_TOP_K):
        pltpu.make_async_copy(ys_ref.at[pl.ds(0, tc * ROW_TILE)], buf.at[slot, k], sem.at[slot]).wait()

    info_t = jnp.concatenate([info_ref[...]] * (LANES // 8), axis=0).T
    w1 = info_t[:, INFO_W1:INFO_W1 + 1]
    w2 = info_t[:, INFO_W2:INFO_W2 + 1]
    y1 = _from_row_tiles(buf.at[slot, 0], tc)
    y2 = _from_row_tiles(buf.at[slot, 1], tc)
    h = h_ref[...] + (y1 * w1 + y2 * w2)
    o_ref[...] = _rms(h, fw_ref[...])


def _combine(dest, ys, info, h, final_w, tc=256):
    T = h.shape[0]
    return pl.pallas_call(
        functools.partial(_combine_kernel, tc=tc, T=T),
        grid_spec=pltpu.PrefetchScalarGridSpec(
            num_scalar_prefetch=1,
            grid=(T // tc,),
            in_specs=[pl.BlockSpec(memory_space=pl.ANY),
                      pl.BlockSpec((8, tc), lambda i, d: (0, i)),
                      pl.BlockSpec((tc, D_MODEL), lambda i, d: (i, 0)),
                      pl.BlockSpec((1, D_MODEL), lambda i, d: (0, 0))],
            out_specs=pl.BlockSpec((tc, D_MODEL), lambda i, d: (i, 0)),
            scratch_shapes=[pltpu.VMEM((2, MOE_TOP_K, tc * ROW_TILE, LANES), F32),
                            pltpu.SemaphoreType.DMA((2,))],
        ),
        out_shape=jax.ShapeDtypeStruct((T, D_MODEL), F32),
        compiler_params=_cparams(("arbitrary",)),
        name="combine",
    )(dest, ys, info, h, final_w[None, :])


def _plan_kernel(info_ref, cnt_ref, dest_ref, pend_ref):
    cnt = cnt_ref[...].astype(jnp.int32)
    nblk_e = ((cnt + (MOE_ROWS - 1)) >> 8).astype(F32)
    r = lax.broadcasted_iota(jnp.int32, (MOE_N_EXPERTS, MOE_N_EXPERTS), 0)
    c = lax.broadcasted_iota(jnp.int32, (MOE_N_EXPERTS, MOE_N_EXPERTS), 1)
    before = jnp.where(c < r, 1.0, 0.0).astype(BF16)
    first_blk = _dot(before, nblk_e.astype(BF16))
    pstart = first_blk[:, 0:1] * float(MOE_ROWS)
    pend_ref[...] = ((first_blk + nblk_e) * float(MOE_ROWS)).astype(jnp.int32)

    info = info_ref[...]
    erow = lax.broadcasted_iota(jnp.int32, (MOE_N_EXPERTS, info.shape[1]), 0)
    start_of = lambda e: jnp.sum(jnp.where(erow == e.astype(jnp.int32), pstart, 0.0), axis=0, keepdims=True)
    d1 = info[INFO_R1:INFO_R1 + 1] + start_of(info[INFO_E1:INFO_E1 + 1])
    d2 = info[INFO_R2:INFO_R2 + 1] + start_of(info[INFO_E2:INFO_E2 + 1])
    zero = jnp.zeros_like(d1)
    dest_ref[...] = jnp.concatenate([d1, d2] + [zero] * 6, axis=0).astype(jnp.int32)


def _plan(info, counts, tr=2048):
    T = info.shape[1]
    dest8, pend = pl.pallas_call(
        _plan_kernel,
        grid=(T // tr,),
        in_specs=[pl.BlockSpec((8, tr), lambda i: (0, i)),
                  pl.BlockSpec((MOE_N_EXPERTS, LANES), lambda i: (0, 0))],
        out_specs=[pl.BlockSpec((8, tr), lambda i: (0, i)),
                   pl.BlockSpec((MOE_N_EXPERTS, LANES), lambda i: (0, 0))],
        out_shape=[jax.ShapeDtypeStruct((8, T), jnp.int32),
                   jax.ShapeDtypeStruct((MOE_N_EXPERTS, LANES), jnp.int32)],
        compiler_params=_cparams(("arbitrary",)),
        name="plan",
    )(info, counts)
    return dest8[:MOE_TOP_K].reshape(-1), pend[:, 0]


def _moe_capacity(T):
    return (-(-(T * MOE_TOP_K) // MOE_ROWS) + MOE_N_EXPERTS) * MOE_ROWS


def _router_weights(router_group_w, router_group_b, router_expert_w, router_expert_b):
    we = jnp.transpose(router_expert_w, (0, 2, 1)).reshape(MOE_N_EXPERTS, D_MODEL)
    pad = LANES - MOE_N_EXPERTS - MOE_GROUPS
    wr = jnp.concatenate([we, router_group_w.T, jnp.zeros((pad, D_MODEL), F32)], axis=0)
    br = jnp.concatenate([router_expert_b.reshape(-1), router_group_b, jnp.zeros((pad,), F32)])[:, None]
    return wr, br


def kernel(x, norm1_w, w_in, gla_fwd_gate_w, gla_fwd_gate_b, gla_bwd_gate_w, gla_bwd_gate_b,
           gla_norm_w, w_out, norm2_w, router_group_w, router_group_b, router_expert_w,
           router_expert_b, expert_w_gate, expert_w_up, expert_w_down, final_norm_w):
    B, S, D = x.shape
    T = B * S
    assert norm1_w.shape[0] == 1, "single-layer trunk: the final norm is fused into the combine step"
    h = x.reshape(T, D)
    gla_slab, loga, att_slab = _inproj(h, S, norm1_w[0], w_in[0], gla_fwd_gate_w[0], gla_fwd_gate_b[0],
                                       gla_bwd_gate_w[0], gla_bwd_gate_b[0])
    o_f, o_b = _gla(gla_slab, loga, B, S)
    att_out = _attention(att_slab.reshape(T, 3 * ATT_WIDTH), B, S)
    att_out = att_out.reshape(B, ATT_CLASSES, S // ATT_CLASSES, ATT_WIDTH)
    wr, br = _router_weights(router_group_w[0], router_group_b[0], router_expert_w[0], router_expert_b[0])
    h, u2, logits = _outproj(o_f, o_b, gla_slab, att_out, h, gla_norm_w[0], w_out[0], norm2_w[0], wr, br)
    info, counts = _route(logits)
    dest, pend = _plan(info, counts)
    xs = _dispatch(dest, pend, u2, _moe_capacity(T))
    ys = _experts(pend, xs, expert_w_gate[0], expert_w_up[0], expert_w_down[0])
    out = _combine(dest, ys, info, h, final_norm_w)
    return out.reshape(B, S, D)
```

```python
import functools

import jax
import jax.numpy as jnp
from jax import lax
from jax.experimental import pallas as pl
from jax.experimental.pallas import tpu as pltpu

F32 = jnp.float32
BF16 = jnp.bfloat16

D_MODEL = 1024
GLA_HEADS = 4
GLA_DV = 128
GLA_DK = 64
GLA_KEY_WIDTH = GLA_HEADS * GLA_DK
GLA_VAL_WIDTH = GLA_HEADS * GLA_DV
GLA_GATE_RANK = 16
GLA_TAU = 16.0
GLA_CHUNK = 64
ATT_WIDTH = 512
ATT_HEAD_DIM = 64
ATT_HEADS = 8
ROT_DIM = 16
ROPE_THETA = 500000.0
DILATED_PATTERNS = ((128, 1), (512, 4), (2048, 16))
ATT_RADIUS = 64
MOE_GROUPS = 4
MOE_EXPERTS_PER_GROUP = 8
MOE_N_EXPERTS = 32
MOE_TOP_K = 2
MOE_D_FF = 512
EPS = 1e-6
NEG_INF = -1e30
LOG2E = 1.4426950408889634

LANES = 128
MOE_ROWS = 256
VMEM_LIMIT = 56 * 1024 * 1024


def _cparams(sem):
    return pltpu.CompilerParams(dimension_semantics=sem, vmem_limit_bytes=VMEM_LIMIT)


def _dot(a, b):
    return jnp.dot(a, b, preferred_element_type=F32)


def _dot_nt(a, b):
    return lax.dot_general(a, b, (((1,), (1,)), ((), ())), preferred_element_type=F32)


def _dot_tn(a, b):
    return lax.dot_general(a, b, (((0,), (0,)), ((), ())), preferred_element_type=F32)


def _rms(x, w):
    return x * lax.rsqrt(jnp.mean(x * x, axis=-1, keepdims=True) + EPS) * w


def _inproj_kernel(x_ref, n1_ref, wg_ref, wlr_ref, wa_ref, gw_ref, gb_ref,
                   rc_ref, rs1_ref, rs2_ref, gla_ref, loga_ref, att_ref, stage_ref):
    x = x_ref[...]
    ub = _rms(x, n1_ref[...]).astype(BF16)
    g = _dot(ub, wg_ref[...])
    gla_ref[:, :GLA_KEY_WIDTH] = g[:, :GLA_KEY_WIDTH] * (GLA_DK ** -0.5)
    gla_ref[:, GLA_KEY_WIDTH:] = g[:, GLA_KEY_WIDTH:]
    lr = _dot(ub, wlr_ref[...])
    gate = _dot(lr.astype(BF16), gw_ref[...]) + gb_ref[...]
    loga_ref[...] = (jnp.minimum(gate, 0.0) - jnp.log(1.0 + jnp.exp(-jnp.abs(gate)))) * (1.0 / GLA_TAU)
    a = _dot(ub, wa_ref[...])
    qk = a[:, :2 * ATT_WIDTH]
    reps = 2 * ATT_WIDTH // LANES
    c = jnp.concatenate([rc_ref[...]] * reps, axis=1)
    s1 = jnp.concatenate([rs1_ref[...]] * reps, axis=1)
    s2 = jnp.concatenate([rs2_ref[...]] * reps, axis=1)
    half = ROT_DIM // 2
    n = 2 * ATT_WIDTH
    roped = qk * c + pltpu.roll(qk, n - half, 1) * s1 + pltpu.roll(qk, half, 1) * s2
    qkv = jnp.concatenate([roped[:, :ATT_WIDTH] * (ATT_HEAD_DIM ** -0.5 * LOG2E), roped[:, ATT_WIDTH:],
                           a[:, 2 * ATT_WIDTH:]], axis=1)
    rows = x.shape[0] // ATT_CLASSES
    for j in range(3 * ATT_WIDTH // LANES):
        cols = slice(j * LANES, (j + 1) * LANES)
        stage_ref[j] = qkv[:, cols]
        for c in range(ATT_CLASSES):
            att_ref[c, :, cols] = stage_ref[j, pl.ds(c, rows, stride=ATT_CLASSES), :]


def _rope_lane_tables(S):
    half = ROT_DIM // 2
    inv = ROPE_THETA ** (-(jnp.arange(0, ROT_DIM, 2, dtype=F32) / ROT_DIM))
    ang = inv[:, None] * jnp.arange(S, dtype=F32)[None, :]
    cos, sin = jnp.cos(ang), jnp.sin(ang)
    lane = jnp.arange(LANES) % ATT_HEAD_DIM
    freq = jnp.arange(half)[:, None]
    first = ((lane[None, :] == freq)).astype(F32)
    second = ((lane[None, :] == freq + half)).astype(F32)
    expand = lambda t, sel: lax.dot_general(t, sel, (((0,), (0,)), ((), ())), precision=lax.Precision.HIGHEST)
    rest = (lane >= ROT_DIM).astype(F32)[None, :]
    return expand(cos, first + second) + rest, expand(-sin, first), expand(sin, second)


def _inproj(x2, S, norm1_w, w_in, wf, bfw, wb, bbw, tm=512):
    T = x2.shape[0]
    o_lr = 2 * GLA_KEY_WIDTH + 2 * GLA_VAL_WIDTH
    o_att = o_lr + 2 * GLA_GATE_RANK
    w_main = w_in[:, :o_lr + LANES].astype(BF16)
    wa = w_in[:, o_att:].astype(BF16)
    zeros = jnp.zeros((GLA_GATE_RANK, GLA_KEY_WIDTH), F32)
    gw = jnp.concatenate([jnp.concatenate([wf, zeros], axis=1), jnp.concatenate([zeros, wb], axis=1),
                          jnp.zeros((LANES - 2 * GLA_GATE_RANK, 2 * GLA_KEY_WIDTH), F32)], axis=0).astype(BF16)
    gb = jnp.concatenate([bfw, bbw])[None, :]
    rc, rs1, rs2 = _rope_lane_tables(S)
    nS = S // tm
    row = lambda i: (i, 0)
    const = lambda i: (0, 0)
    pos = lambda i: (i % nS, 0)
    return pl.pallas_call(
        _inproj_kernel,
        grid=(T // tm,),
        in_specs=[
            pl.BlockSpec((tm, D_MODEL), row),
            pl.BlockSpec((1, D_MODEL), const),
            pl.BlockSpec((D_MODEL, o_lr), const),
            pl.BlockSpec((D_MODEL, LANES), lambda i: (0, o_lr // LANES)),
            pl.BlockSpec((D_MODEL, 3 * ATT_WIDTH), const),
            pl.BlockSpec((LANES, 2 * GLA_KEY_WIDTH), const),
            pl.BlockSpec((1, 2 * GLA_KEY_WIDTH), const),
            pl.BlockSpec((tm, LANES), pos),
            pl.BlockSpec((tm, LANES), pos),
            pl.BlockSpec((tm, LANES), pos),
        ],
        out_specs=[
            pl.BlockSpec((tm, o_lr), row),
            pl.BlockSpec((tm, 2 * GLA_KEY_WIDTH), row),
            pl.BlockSpec((None, ATT_CLASSES, tm // ATT_CLASSES, 3 * ATT_WIDTH),
                         lambda i: (i // nS, 0, i % nS, 0)),
        ],
        out_shape=[
            jax.ShapeDtypeStruct((T, o_lr), F32),
            jax.ShapeDtypeStruct((T, 2 * GLA_KEY_WIDTH), F32),
            jax.ShapeDtypeStruct((T // S, ATT_CLASSES, S // ATT_CLASSES, 3 * ATT_WIDTH), F32),
        ],
        scratch_shapes=[pltpu.VMEM((3 * ATT_WIDTH // LANES, tm, LANES), F32)],
        compiler_params=_cparams(("arbitrary",)),
        name="inproj",
    )(x2, norm1_w[None, :], w_main, w_main, wa, gw, gb, rc, rs1, rs2)


def _gla_decays(q, k, v, la, forward, G):
    C = GLA_CHUNK
    R = G * C
    r = lax.broadcasted_iota(jnp.int32, (R, R), 0)
    c = lax.broadcasted_iota(jnp.int32, (R, R), 1)
    same = (r >> 6) == (c >> 6)
    tri = (c <= r) if forward else (c >= r)
    t_mat = jnp.where(same, jnp.where(tri, 1.0, 0.0), 0.0).astype(BF16)
    hi = la.astype(BF16)
    lo = (la - hi.astype(F32)).astype(BF16)
    b = _dot(t_mat, hi) + _dot(t_mat, lo)
    edge = C - 1 if forward else 0
    tot = jnp.concatenate([jnp.broadcast_to(b[g * C + edge:g * C + edge + 1], (C, GLA_KEY_WIDTH))
                           for g in range(G)], axis=0)
    order = list(range(G)) if forward else list(range(G - 1, -1, -1))
    return dict(q_dec=q * jnp.exp(b), k_inv=(k * jnp.exp(-b)).astype(BF16), k_end=k * jnp.exp(tot - b),
                tot=tot, vb=v.astype(BF16), order=order, forward=forward, G=G)


def _gla_scores(prep):
    C, H = GLA_CHUNK, GLA_HEADS
    lane_k = lax.broadcasted_iota(jnp.int32, (C, GLA_KEY_WIDTH), 1)
    qd_heads, scores = {}, {}
    for g in prep["order"]:
        rows = slice(g * C, (g + 1) * C)
        qd = prep["q_dec"][rows]
        qd_heads[g] = jnp.concatenate([jnp.where((lane_k >> 6) == h, qd, 0.0) for h in range(H)],
                                      axis=0).astype(BF16)
        scores[g] = _dot_nt(qd_heads[g], prep["k_inv"][rows])
    return qd_heads, scores


def _gla_chunk_updates(prep):
    C, H, G = GLA_CHUNK, GLA_HEADS, prep["G"]
    k_end, tot, vb = prep["k_end"], prep["tot"], prep["vb"]
    kv, dec_t = {}, {}
    lane = lax.broadcasted_iota(jnp.int32, (GLA_KEY_WIDTH, 2 * C), 1)
    zeros = jnp.zeros((C, GLA_DV), BF16)
    for p in range(G // 2):
        pair = slice(2 * p * C, (2 * p + 2) * C)
        ke_t = k_end[pair].T.astype(BF16)
        tot_t = tot[pair].T
        swapped = pltpu.roll(tot_t, C, 1)
        for half in range(2):
            g = 2 * p + half
            rows = slice(g * C, (g + 1) * C)
            own = (lane < C) if half == 0 else (lane >= C)
            dec_t[g] = jnp.exp(jnp.where(own, tot_t, swapped))
            parts = []
            for h in range(H):
                v_h = vb[rows, h * GLA_DV:(h + 1) * GLA_DV]
                v_pad = jnp.concatenate([v_h, zeros] if half == 0 else [zeros, v_h], axis=0)
                parts.append(_dot(ke_t[h * C:(h + 1) * C], v_pad))
            kv[g] = jnp.concatenate(parts, axis=0)
    return kv, dec_t


def _gla_states(prep, kv, dec_t, s_ref):
    st = s_ref[...]
    states = {}
    for g in prep["order"]:
        states[g] = st.astype(BF16)
        st = st * dec_t[g] + kv[g]
    s_ref[...] = st
    return states


def _gla_outputs(prep, qd_heads, scores, inter, o_ref):
    C, H = GLA_CHUNK, GLA_HEADS
    row_q = lax.broadcasted_iota(jnp.int32, (H * C, C), 0) & (C - 1)
    col_k = lax.broadcasted_iota(jnp.int32, (H * C, C), 1)
    a_mask = (col_k <= row_q) if prep["forward"] else (col_k >= row_q)
    for g in prep["order"]:
        rows = slice(g * C, (g + 1) * C)
        a = jnp.where(a_mask, scores[g], 0.0).astype(BF16)
        vv = prep["vb"][rows]
        o_ref[rows, :] = jnp.concatenate(
            [_dot(a[h * C:(h + 1) * C], vv[:, h * GLA_DV:(h + 1) * GLA_DV]) + inter[g][h * C:(h + 1) * C]
             for h in range(H)], axis=1)


def _gla_kernel(qf_ref, kf_ref, vf_ref, laf_ref, qb_ref, kb_ref, vb_ref, lab_ref,
                of_ref, ob_ref, sf_ref, sb_ref, *, G):
    @pl.when(pl.program_id(1) == 0)
    def _():
        sf_ref[...] = jnp.zeros_like(sf_ref)
        sb_ref[...] = jnp.zeros_like(sb_ref)

    dirs = [(_gla_decays(qf_ref[...], kf_ref[...], vf_ref[...], laf_ref[...], True, G), sf_ref, of_ref),
            (_gla_decays(qb_ref[...], kb_ref[...], vb_ref[...], lab_ref[...], False, G), sb_ref, ob_ref)]
    scored = [_gla_scores(prep) for prep, _, _ in dirs]
    updates = [_gla_chunk_updates(prep) for prep, _, _ in dirs]
    states = [_gla_states(prep, kv, dec_t, s_ref) for (prep, s_ref, _), (kv, dec_t) in zip(dirs, updates)]
    inters = [{g: _dot(qd_heads[g], st[g]) for g in prep["order"]}
              for (prep, _, _), (qd_heads, _), st in zip(dirs, scored, states)]
    for (prep, _, o_ref), (qd_heads, scores), inter in zip(dirs, scored, inters):
        _gla_outputs(prep, qd_heads, scores, inter, o_ref)


def _gla(gla_slab, loga, B, S, G=4):
    T = B * S
    R = G * GLA_CHUNK
    ns = S // R
    fwd = lambda col: (lambda b, i: (b * ns + i, col))
    bwd = lambda col: (lambda b, i: (b * ns + ns - 1 - i, col))
    kw, vw = GLA_KEY_WIDTH, GLA_VAL_WIDTH
    return pl.pallas_call(
        functools.partial(_gla_kernel, G=G),
        grid=(B, ns),
        in_specs=[
            pl.BlockSpec((R, kw), fwd(0)), pl.BlockSpec((R, kw), fwd(1)),
            pl.BlockSpec((R, vw), fwd(1)), pl.BlockSpec((R, kw), fwd(0)),
            pl.BlockSpec((R, kw), bwd(0)), pl.BlockSpec((R, kw), bwd(1)),
            pl.BlockSpec((R, vw), bwd(1)), pl.BlockSpec((R, kw), bwd(1)),
        ],
        out_specs=[pl.BlockSpec((R, vw), fwd(0)), pl.BlockSpec((R, vw), bwd(0))],
        out_shape=[jax.ShapeDtypeStruct((T, vw), F32), jax.ShapeDtypeStruct((T, vw), F32)],
        scratch_shapes=[pltpu.VMEM((kw, GLA_DV), F32), pltpu.VMEM((kw, GLA_DV), F32)],
        compiler_params=_cparams(("arbitrary", "arbitrary")),
        name="gla",
    )(gla_slab, gla_slab, gla_slab, loga, gla_slab, gla_slab, gla_slab, loga)


ATT_CLASSES = 4
ATT_QB = 128
ATT_KB = ATT_QB + 2 * ATT_RADIUS


ATT_UNROLL = 4


def _att_kernel(q_ref, k_ref, v_ref, o_ref, m_ref, l_ref, bias_ref, *, S):
    QB, KB, NC = ATT_QB, ATT_KB, ATT_CLASSES
    L4 = S // NC
    lane = lax.broadcasted_iota(jnp.int32, (QB, LANES), 1)
    head0 = lane < ATT_HEAD_DIM

    @pl.when((pl.program_id(0) == 0) & (pl.program_id(1) == 0))
    def _():
        rowi = lax.broadcasted_iota(jnp.int32, (2 * QB, KB), 0) & (QB - 1)
        coli = lax.broadcasted_iota(jnp.int32, (2 * QB, KB), 1)
        qpos = (rowi & (QB // NC - 1)) * NC + (rowi >> 5)
        kpos = (coli & (KB // NC - 1)) * NC + (coli >> 6)
        for case in range(3):
            bias_ref[0, case] = jnp.where(jnp.abs(rowi - coli + case * ATT_RADIUS) <= ATT_RADIUS, 0.0, NEG_INF)
            bias_ref[1, case] = jnp.where(jnp.abs(qpos - kpos + case * ATT_RADIUS) <= ATT_RADIUS, 0.0, NEG_INF)

    for pi, (_, d) in enumerate(DILATED_PATTERNS):
        L = S // d
        nb = L // QB
        shift = nb.bit_length() - 1
        first = pi == 0
        last = pi == len(DILATED_PATTERNS) - 1

        def scores(n, d=d, L=L, nb=nb, shift=shift):
            cls = n >> shift
            q0 = (n & (nb - 1)) * QB
            ws = jnp.clip(q0 - ATT_RADIUS, 0, L - KB)
            if d == 1:
                qsls = [pl.ds(pl.multiple_of(c * L4 + q0 // NC, QB // NC), QB // NC) for c in range(NC)]
                ksls = [pl.ds(pl.multiple_of(c * L4 + ws // NC, ATT_RADIUS // NC), KB // NC) for c in range(NC)]
            elif d == NC:
                qsls = [pl.ds(pl.multiple_of(cls * L4 + q0, QB), QB)]
                ksls = [pl.ds(pl.multiple_of(cls * L4 + ws, ATT_RADIUS), KB)]
            else:
                base = (cls & (NC - 1)) * L4 + (cls >> 2)
                qsls = [pl.ds(base + NC * q0, QB, stride=NC)]
                ksls = [pl.ds(base + NC * ws, KB, stride=NC)]
            q = jnp.concatenate([q_ref[sl, :] for sl in qsls], axis=0)
            kw = jnp.concatenate([k_ref[sl, :] for sl in ksls], axis=0)
            q2 = jnp.concatenate([jnp.where(head0, q, 0.0), jnp.where(head0, 0.0, q)], axis=0).astype(BF16)
            s = _dot_nt(q2, kw.astype(BF16))
            return qsls, ksls, s + bias_ref[1 if d == 1 else 0, (q0 - ws) >> 6]

        def softmax_pv(qsls, ksls, s):
            m_blk = jnp.max(s, axis=-1, keepdims=True)
            p = jnp.exp2(s - m_blk)
            vw = jnp.concatenate([v_ref[sl, :] for sl in ksls], axis=0)
            v_ones = jnp.concatenate([vw.astype(BF16), jnp.ones((KB, LANES), BF16)], axis=1)
            pv = _dot(p.astype(BF16), v_ones)
            acc_b = jnp.where(head0, pv[:QB, :LANES], pv[QB:, :LANES])
            m_b = jnp.where(head0, m_blk[:QB], m_blk[QB:])
            l_b = jnp.where(head0, pv[:QB, LANES:], pv[QB:, LANES:])
            return qsls, acc_b, m_b, l_b

        def load(ref, sls):
            return jnp.concatenate([ref[sl, :] for sl in sls], axis=0)

        def store(ref, sls, val):
            n = val.shape[0] // len(sls)
            for i, sl in enumerate(sls):
                ref[sl, :] = val[i * n:(i + 1) * n]

        def body(n, carry, first=first, last=last):
            staged = [scores(n * ATT_UNROLL + u) for u in range(ATT_UNROLL)]
            blocks = [softmax_pv(*st) for st in staged]
            for qsls, acc_b, m_b, l_b in blocks:
                if first:
                    acc, m_new, l_new = acc_b, m_b, l_b
                else:
                    m_old = load(m_ref, qsls)
                    m_new = jnp.maximum(m_old, m_b)
                    w_old = jnp.exp2(m_old - m_new)
                    w_blk = jnp.exp2(m_b - m_new)
                    acc = load(o_ref, qsls) * w_old + acc_b * w_blk
                    l_new = load(l_ref, qsls) * w_old + l_b * w_blk
                if last:
                    store(o_ref, qsls, acc / l_new)
                else:
                    store(o_ref, qsls, acc)
                    store(m_ref, qsls, m_new)
                    store(l_ref, qsls, l_new)
            return carry

        lax.fori_loop(0, S // (QB * ATT_UNROLL), body, 0)


def _attention(att_slab, B, S):
    T = B * S
    ncol = ATT_WIDTH // LANES
    return pl.pallas_call(
        functools.partial(_att_kernel, S=S),
        grid=(B, ncol),
        in_specs=[
            pl.BlockSpec((S, LANES), lambda b, h: (b, h)),
            pl.BlockSpec((S, LANES), lambda b, h: (b, ncol + h)),
            pl.BlockSpec((S, LANES), lambda b, h: (b, 2 * ncol + h)),
        ],
        out_specs=pl.BlockSpec((S, LANES), lambda b, h: (b, h)),
        out_shape=jax.ShapeDtypeStruct((T, ATT_WIDTH), F32),
        scratch_shapes=[pltpu.VMEM((S, LANES), F32), pltpu.VMEM((S, LANES), F32),
                        pltpu.VMEM((2, 3, 2 * ATT_QB, ATT_KB), F32)],
        compiler_params=_cparams(("arbitrary", "arbitrary")),
        name="dilated_attention",
    )(att_slab, att_slab, att_slab)


PACK_WORDS = D_MODEL // 2
ROW_TILE = PACK_WORDS // LANES
HIGH_HALF = -65536


def _pack_rows(x):
    bits = lambda v: lax.bitcast_convert_type(v.astype(BF16).astype(F32), jnp.int32)
    low = (bits(x[:, :PACK_WORDS]) >> 16) & 0xFFFF
    return (bits(x[:, PACK_WORDS:]) & HIGH_HALF) | low


def _unpack_rows(w):
    low = lax.bitcast_convert_type(w << 16, F32)
    high = lax.bitcast_convert_type(w & HIGH_HALF, F32)
    return jnp.concatenate([low, high], axis=1).astype(BF16)


def _to_row_tiles(ref, w):
    n = w.shape[0]
    for j in range(ROW_TILE):
        ref[pl.ds(j, n, stride=ROW_TILE), :] = w[:, j * LANES:(j + 1) * LANES]


def _from_row_tiles(ref, n):
    return jnp.concatenate([ref[pl.ds(j, n, stride=ROW_TILE), :] for j in range(ROW_TILE)], axis=1)


def _tile_copy(src_ref, src_row, dst_ref, dst_row, sem):
    src = pl.ds(pl.multiple_of(src_row * ROW_TILE, ROW_TILE), ROW_TILE)
    dst = pl.ds(pl.multiple_of(dst_row * ROW_TILE, ROW_TILE), ROW_TILE)
    return pltpu.make_async_copy(src_ref.at[src], dst_ref.at[dst], sem)


def _outproj_kernel(of_ref, ob_ref, gg_ref, att_ref, x_ref, gnw_ref, wo1_ref, wo2_ref,
                    n2_ref, wr_ref, br_ref, h_ref, u_ref, lg_ref, stage_ref):
    rows = stage_ref.shape[1] // ATT_CLASSES
    for j in range(ATT_WIDTH // LANES):
        for c in range(ATT_CLASSES):
            stage_ref[j, pl.ds(c, rows, stride=ATT_CLASSES), :] = att_ref[c, :, j * LANES:(j + 1) * LANES]
    att = jnp.concatenate([stage_ref[j] for j in range(ATT_WIDTH // LANES)], axis=1)
    o = of_ref[...] + ob_ref[...]
    gate = gg_ref[...]
    gnw = gnw_ref[...]
    parts = []
    for h in range(GLA_HEADS):
        sl = slice(h * GLA_DV, (h + 1) * GLA_DV)
        parts.append(_rms(o[:, sl], gnw))
    y = jnp.concatenate(parts, axis=1) * (gate / (1.0 + jnp.exp(-gate)))
    mix = _dot(y.astype(BF16), wo1_ref[...]) + _dot(att.astype(BF16), wo2_ref[...])
    h = x_ref[...] + mix
    h_ref[...] = h
    u = _rms(h, n2_ref[...])
    _to_row_tiles(u_ref, _pack_rows(u))
    u_hi = u.astype(BF16)
    u_lo = (u - u_hi.astype(F32)).astype(BF16)
    hi_both = _dot_nt(wr_ref[...], u_hi)
    lg_ref[...] = (hi_both[:LANES] + hi_both[LANES:] + _dot_nt(wr_ref[:LANES], u_lo)) + br_ref[...]


def _outproj(o_f, o_b, gla_slab, att_out, x2, gla_norm_w, w_out, norm2_w, wr, br, tm=512):
    T = x2.shape[0]
    nS = att_out.shape[2] * ATT_CLASSES // tm
    row = lambda i: (i, 0)
    const = lambda i: (0, 0)
    wo = w_out.astype(BF16)
    wr_hi = wr.astype(BF16)
    wr_lo = (wr - wr_hi.astype(F32)).astype(BF16)
    wr = jnp.concatenate([wr_hi, wr_lo], axis=0)
    return pl.pallas_call(
        _outproj_kernel,
        grid=(T // tm,),
        in_specs=[
            pl.BlockSpec((tm, GLA_VAL_WIDTH), row),
            pl.BlockSpec((tm, GLA_VAL_WIDTH), row),
            pl.BlockSpec((tm, GLA_VAL_WIDTH), lambda i: (i, 2)),
            pl.BlockSpec((None, ATT_CLASSES, tm // ATT_CLASSES, ATT_WIDTH), lambda i: (i // nS, 0, i % nS, 0)),
            pl.BlockSpec((tm, D_MODEL), row),
            pl.BlockSpec((1, GLA_DV), const),
            pl.BlockSpec((GLA_VAL_WIDTH, D_MODEL), const),
            pl.BlockSpec((ATT_WIDTH, D_MODEL), const),
            pl.BlockSpec((1, D_MODEL), const),
            pl.BlockSpec((2 * LANES, D_MODEL), const),
            pl.BlockSpec((LANES, 1), const),
        ],
        out_specs=[
            pl.BlockSpec((tm, D_MODEL), row),
            pl.BlockSpec((tm * ROW_TILE, LANES), row),
            pl.BlockSpec((LANES, tm), lambda i: (0, i)),
        ],
        out_shape=[
            jax.ShapeDtypeStruct((T, D_MODEL), F32),
            jax.ShapeDtypeStruct((T * ROW_TILE, LANES), jnp.int32),
            jax.ShapeDtypeStruct((LANES, T), F32),
        ],
        scratch_shapes=[pltpu.VMEM((ATT_WIDTH // LANES, tm, LANES), F32)],
        compiler_params=_cparams(("arbitrary",)),
        name="outproj",
    )(o_f, o_b, gla_slab, att_out, x2, gla_norm_w[None, :], wo[:GLA_VAL_WIDTH], wo[GLA_VAL_WIDTH:],
      norm2_w[None, :], wr, br)


INFO_E1, INFO_E2, INFO_R1, INFO_R2, INFO_W1, INFO_W2 = range(6)
ROUTE_ROWS = 40


def _route_kernel(lg_ref, info_ref, cnt_ref, carry_ref):
    @pl.when(pl.program_id(0) == 0)
    def _():
        carry_ref[...] = jnp.zeros_like(carry_ref)

    lg = lg_ref[:ROUTE_ROWS, :]
    tr = lg.shape[1]
    row = lax.broadcasted_iota(jnp.int32, (ROUTE_ROWS, tr), 0)
    big = jnp.int32(1 << 20)
    is_g = (row >= MOE_N_EXPERTS) & (row < MOE_N_EXPERTS + MOE_GROUPS)
    gl = jnp.where(is_g, lg, -jnp.inf)
    gmax = jnp.max(gl, axis=0, keepdims=True)
    gsel = jnp.min(jnp.where(gl == gmax, row - MOE_N_EXPERTS, big), axis=0, keepdims=True)
    g_w = 1.0 / jnp.sum(jnp.where(is_g, jnp.exp(lg - gmax), 0.0), axis=0, keepdims=True)
    in_grp = (row < MOE_N_EXPERTS) & ((row >> 3) == gsel)
    el = jnp.where(in_grp, lg, -jnp.inf)
    v1 = jnp.max(el, axis=0, keepdims=True)
    i1 = jnp.min(jnp.where(el == v1, row, big), axis=0, keepdims=True)
    el2 = jnp.where(row == i1, -jnp.inf, el)
    v2 = jnp.max(el2, axis=0, keepdims=True)
    i2 = jnp.min(jnp.where(el2 == v2, row, big), axis=0, keepdims=True)
    t = jnp.exp(v2 - v1)
    w1 = g_w * (1.0 / (1.0 + t))
    w2 = g_w * (t / (1.0 + t))

    erow = lax.broadcasted_iota(jnp.int32, (MOE_N_EXPERTS, tr), 0)
    hit1 = erow == i1
    hit2 = erow == i2
    member = jnp.where(hit1 | hit2, 1.0, 0.0)
    r = lax.broadcasted_iota(jnp.int32, (tr, tr), 0)
    c = lax.broadcasted_iota(jnp.int32, (tr, tr), 1)
    earlier = jnp.where(r < c, 1.0, 0.0).astype(BF16)
    carry = carry_ref[...]
    prefix = _dot(member.astype(BF16), earlier) + carry[:, 0:1]
    rank1 = jnp.sum(jnp.where(hit1, prefix, 0.0), axis=0, keepdims=True)
    rank2 = jnp.sum(jnp.where(hit2, prefix, 0.0), axis=0, keepdims=True)
    carry = carry + jnp.sum(member, axis=1, keepdims=True)
    carry_ref[...] = carry
    cnt_ref[...] = carry

    zero = jnp.zeros_like(w1)
    info_ref[...] = jnp.concatenate([i1.astype(F32), i2.astype(F32), rank1, rank2, w1, w2, zero, zero], axis=0)


def _route(logits_t, tr=512):
    T = logits_t.shape[1]
    return pl.pallas_call(
        _route_kernel,
        grid=(T // tr,),
        in_specs=[pl.BlockSpec((LANES, tr), lambda i: (0, i))],
        out_specs=[pl.BlockSpec((8, tr), lambda i: (0, i)),
                   pl.BlockSpec((MOE_N_EXPERTS, LANES), lambda i: (0, 0))],
        out_shape=[jax.ShapeDtypeStruct((8, T), F32), jax.ShapeDtypeStruct((MOE_N_EXPERTS, LANES), F32)],
        scratch_shapes=[pltpu.VMEM((MOE_N_EXPERTS, LANES), F32)],
        compiler_params=_cparams(("arbitrary",)),
        name="route",
    )(logits_t)


ROW_UNROLL = 8


def _dispatch_kernel(dest_ref, pend_ref, u_ref, xs_ref, zbuf, sem, zsem, *, td, T, nblk):
    @pl.when(pl.program_id(0) == 0)
    def _():
        zbuf[...] = jnp.zeros_like(zbuf)
        n_used = pend_ref[MOE_N_EXPERTS - 1] >> 8

        def zero_copy(blk):
            start = pl.multiple_of(blk * (MOE_ROWS * ROW_TILE), MOE_ROWS * ROW_TILE)
            return pltpu.make_async_copy(zbuf, xs_ref.at[pl.ds(start, MOE_ROWS * ROW_TILE)], zsem)

        def each_pad_block(fn):
            def per_expert(e, carry):
                prev = jnp.where(e > 0, pend_ref[jnp.maximum(e - 1, 0)], 0)

                @pl.when(pend_ref[e] > prev)
                def _():
                    fn((pend_ref[e] >> 8) - 1)
                return carry

            def per_tail(j, carry):
                @pl.when(n_used + j < nblk)
                def _():
                    fn(n_used + j)
                return carry

            lax.fori_loop(0, MOE_N_EXPERTS, per_expert, 0)
            lax.fori_loop(0, MOE_N_EXPERTS, per_tail, 0)

        each_pad_block(lambda blk: zero_copy(blk).start())
        each_pad_block(lambda blk: zero_copy(blk).wait())

    base = pl.program_id(0) * td

    def issue(g, carry):
        for j in range(ROW_UNROLL):
            r = g * ROW_UNROLL + j
            for k in range(MOE_TOP_K):
                _tile_copy(u_ref, r, xs_ref, dest_ref[k * T + base + r], sem).start(priority=k)
        return carry

    lax.fori_loop(0, td // ROW_UNROLL, issue, 0)
    for k in range(MOE_TOP_K):
        pltpu.make_async_copy(u_ref, xs_ref.at[pl.ds(0, td * ROW_TILE)], sem).wait()


def _dispatch(dest, pend, u2, cap, td=256):
    T = u2.shape[0] // ROW_TILE
    return pl.pallas_call(
        functools.partial(_dispatch_kernel, td=td, T=T, nblk=cap // MOE_ROWS),
        grid_spec=pltpu.PrefetchScalarGridSpec(
            num_scalar_prefetch=2,
            grid=(T // td,),
            in_specs=[pl.BlockSpec((td * ROW_TILE, LANES), lambda i, d, z: (i, 0))],
            out_specs=pl.BlockSpec(memory_space=pl.ANY),
            scratch_shapes=[pltpu.VMEM((MOE_ROWS * ROW_TILE, LANES), jnp.int32),
                            pltpu.SemaphoreType.DMA(()), pltpu.SemaphoreType.DMA(())],
        ),
        out_shape=jax.ShapeDtypeStruct((cap * ROW_TILE, LANES), jnp.int32),
        compiler_params=_cparams(("arbitrary",)),
        name="dispatch",
    )(dest, pend, u2)


def _expert_kernel(pend_ref, x_ref, wg_hbm, wu_hbm, wd_hbm, y_ref,
                   stage_g, stage_u, stage_d, wgb, wub, wdb, cur_ref, sem):
    b = pl.program_id(0)
    last = MOE_N_EXPERTS - 1
    live = b < (pend_ref[last] >> 8)

    def weight_copies(e):
        return (pltpu.make_async_copy(wg_hbm.at[e], stage_g, sem.at[0]),
                pltpu.make_async_copy(wu_hbm.at[e], stage_u, sem.at[1]),
                pltpu.make_async_copy(wd_hbm.at[e], stage_d, sem.at[2]))

    def owner(start, row):
        return lax.while_loop(lambda e: (e < last) & (pend_ref[e] <= row), lambda e: e + 1, start)

    @pl.when(b == 0)
    def _():
        first = owner(0, 0)
        cur_ref[0] = -1
        for c in weight_copies(first):
            c.start()

    @pl.when(live)
    def _():
        prev = cur_ref[0]
        e = owner(jnp.maximum(prev, 0), b * MOE_ROWS)
        cur_ref[0] = e

        @pl.when(e != prev)
        def _():
            for c in weight_copies(e):
                c.wait()
            wgb[...] = stage_g[...].astype(BF16)
            wub[...] = stage_u[...].astype(BF16)
            wdb[...] = stage_d[...].astype(BF16)

            @pl.when(pend_ref[e] < pend_ref[last])
            def _():
                for c in weight_copies(owner(e + 1, pend_ref[e])):
                    c.start()

        xb = _unpack_rows(_from_row_tiles(x_ref, MOE_ROWS))
        g = _dot(xb, wgb[...])
        u = _dot(xb, wub[...])
        hid = (g / (1.0 + jnp.exp(-g))) * u
        _to_row_tiles(y_ref, _pack_rows(_dot(hid.astype(BF16), wdb[...])))

    @pl.when(jnp.logical_not(live))
    def _():
        y_ref[...] = jnp.zeros_like(y_ref)


def _experts(pend, xs, w_gate, w_up, w_down):
    cap = xs.shape[0] // ROW_TILE
    nblk = cap // MOE_ROWS
    rows = lambda b, pend: (jnp.minimum(b, (pend[MOE_N_EXPERTS - 1] >> 8) - 1), 0)
    return pl.pallas_call(
        _expert_kernel,
        grid_spec=pltpu.PrefetchScalarGridSpec(
            num_scalar_prefetch=1,
            grid=(nblk,),
            in_specs=[
                pl.BlockSpec((MOE_ROWS * ROW_TILE, LANES), rows),
                pl.BlockSpec(memory_space=pl.ANY),
                pl.BlockSpec(memory_space=pl.ANY),
                pl.BlockSpec(memory_space=pl.ANY),
            ],
            out_specs=pl.BlockSpec((MOE_ROWS * ROW_TILE, LANES), lambda b, pend: (b, 0)),
            scratch_shapes=[pltpu.VMEM((D_MODEL, MOE_D_FF), F32),
                            pltpu.VMEM((D_MODEL, MOE_D_FF), F32),
                            pltpu.VMEM((MOE_D_FF, D_MODEL), F32),
                            pltpu.VMEM((D_MODEL, MOE_D_FF), BF16),
                            pltpu.VMEM((D_MODEL, MOE_D_FF), BF16),
                            pltpu.VMEM((MOE_D_FF, D_MODEL), BF16),
                            pltpu.SMEM((1,), jnp.int32),
                            pltpu.SemaphoreType.DMA((3,))],
        ),
        out_shape=jax.ShapeDtypeStruct((cap * ROW_TILE, LANES), jnp.int32),
        compiler_params=_cparams(("arbitrary",)),
        name="experts",
    )(pend, xs, w_gate, w_up, w_down)


def _combine_kernel(dest_ref, ys_ref, info_ref, h_ref, fw_ref, o_ref, buf, sem, *, tc, T):
    i = pl.program_id(0)
    n = pl.num_programs(0)

    def issue(step, slot):
        base = step * tc

        def body(g, carry):
            for j in range(ROW_UNROLL):
                r = g * ROW_UNROLL + j
                for k in range(MOE_TOP_K):
                    _tile_copy(ys_ref, dest_ref[k * T + base + r], buf.at[slot, k], r,
                               sem.at[slot]).start(priority=k)
            return carry

        lax.fori_loop(0, tc // ROW_UNROLL, body, 0)

    @pl.when(i == 0)
    def _():
        issue(0, 0)

    slot = i % 2

    @pl.when(i + 1 < n)
    def _():
        issue(i + 1, 1 - slot)

    for k in range(MOE_TOP_K):
        pltpu.make_async_copy(ys_ref.at[pl.ds(0, tc * ROW_TILE)], buf.at[slot, k], sem.at[slot]).wait()

    info_t = jnp.concatenate([info_ref[...]] * (LANES // 8), axis=0).T
    w1 = info_t[:, INFO_W1:INFO_W1 + 1]
    w2 = info_t[:, INFO_W2:INFO_W2 + 1]
    y1 = _unpack_rows(_from_row_tiles(buf.at[slot, 0], tc)).astype(F32)
    y2 = _unpack_rows(_from_row_tiles(buf.at[slot, 1], tc)).astype(F32)
    h = h_ref[...] + (y1 * w1 + y2 * w2)
    o_ref[...] = _rms(h, fw_ref[...])


def _combine(dest, ys, info, h, final_w, tc=256):
    T = h.shape[0]
    return pl.pallas_call(
        functools.partial(_combine_kernel, tc=tc, T=T),
        grid_spec=pltpu.PrefetchScalarGridSpec(
            num_scalar_prefetch=1,
            grid=(T // tc,),
            in_specs=[pl.BlockSpec(memory_space=pl.ANY),
                      pl.BlockSpec((8, tc), lambda i, d: (0, i)),
                      pl.BlockSpec((tc, D_MODEL), lambda i, d: (i, 0)),
                      pl.BlockSpec((1, D_MODEL), lambda i, d: (0, 0))],
            out_specs=pl.BlockSpec((tc, D_MODEL), lambda i, d: (i, 0)),
            scratch_shapes=[pltpu.VMEM((2, MOE_TOP_K, tc * ROW_TILE, LANES), jnp.int32),
                            pltpu.SemaphoreType.DMA((2,))],
        ),
        out_shape=jax.ShapeDtypeStruct((T, D_MODEL), F32),
        compiler_params=_cparams(("arbitrary",)),
        name="combine",
    )(dest, ys, info, h, final_w[None, :])


def _plan_kernel(info_ref, cnt_ref, dest_ref, pend_ref):
    cnt = cnt_ref[...].astype(jnp.int32)
    nblk_e = ((cnt + (MOE_ROWS - 1)) >> 8).astype(F32)
    r = lax.broadcasted_iota(jnp.int32, (MOE_N_EXPERTS, MOE_N_EXPERTS), 0)
    c = lax.broadcasted_iota(jnp.int32, (MOE_N_EXPERTS, MOE_N_EXPERTS), 1)
    before = jnp.where(c < r, 1.0, 0.0).astype(BF16)
    first_blk = _dot(before, nblk_e.astype(BF16))
    pstart = first_blk[:, 0:1] * float(MOE_ROWS)
    pend_ref[...] = ((first_blk + nblk_e) * float(MOE_ROWS)).astype(jnp.int32)

    info = info_ref[...]
    erow = lax.broadcasted_iota(jnp.int32, (MOE_N_EXPERTS, info.shape[1]), 0)
    start_of = lambda e: jnp.sum(jnp.where(erow == e.astype(jnp.int32), pstart, 0.0), axis=0, keepdims=True)
    d1 = info[INFO_R1:INFO_R1 + 1] + start_of(info[INFO_E1:INFO_E1 + 1])
    d2 = info[INFO_R2:INFO_R2 + 1] + start_of(info[INFO_E2:INFO_E2 + 1])
    zero = jnp.zeros_like(d1)
    dest_ref[...] = jnp.concatenate([d1, d2] + [zero] * 6, axis=0).astype(jnp.int32)


def _plan(info, counts, tr=2048):
    T = info.shape[1]
    dest8, pend = pl.pallas_call(
        _plan_kernel,
        grid=(T // tr,),
        in_specs=[pl.BlockSpec((8, tr), lambda i: (0, i)),
                  pl.BlockSpec((MOE_N_EXPERTS, LANES), lambda i: (0, 0))],
        out_specs=[pl.BlockSpec((8, tr), lambda i: (0, i)),
                   pl.BlockSpec((MOE_N_EXPERTS, LANES), lambda i: (0, 0))],
        out_shape=[jax.ShapeDtypeStruct((8, T), jnp.int32),
                   jax.ShapeDtypeStruct((MOE_N_EXPERTS, LANES), jnp.int32)],
        compiler_params=_cparams(("arbitrary",)),
        name="plan",
    )(info, counts)
    return dest8[:MOE_TOP_K].reshape(-1), pend[:, 0]


def _moe_capacity(T):
    return (-(-(T * MOE_TOP_K) // MOE_ROWS) + MOE_N_EXPERTS) * MOE_ROWS


def _router_weights(router_group_w, router_group_b, router_expert_w, router_expert_b):
    we = jnp.transpose(router_expert_w, (0, 2, 1)).reshape(MOE_N_EXPERTS, D_MODEL)
    pad = LANES - MOE_N_EXPERTS - MOE_GROUPS
    wr = jnp.concatenate([we, router_group_w.T, jnp.zeros((pad, D_MODEL), F32)], axis=0)
    br = jnp.concatenate([router_expert_b.reshape(-1), router_group_b, jnp.zeros((pad,), F32)])[:, None]
    return wr, br


def kernel(x, norm1_w, w_in, gla_fwd_gate_w, gla_fwd_gate_b, gla_bwd_gate_w, gla_bwd_gate_b,
           gla_norm_w, w_out, norm2_w, router_group_w, router_group_b, router_expert_w,
           router_expert_b, expert_w_gate, expert_w_up, expert_w_down, final_norm_w):
    B, S, D = x.shape
    T = B * S
    assert norm1_w.shape[0] == 1, "single-layer trunk: the final norm is fused into the combine step"
    h = x.reshape(T, D)
    gla_slab, loga, att_slab = _inproj(h, S, norm1_w[0], w_in[0], gla_fwd_gate_w[0], gla_fwd_gate_b[0],
                                       gla_bwd_gate_w[0], gla_bwd_gate_b[0])
    o_f, o_b = _gla(gla_slab, loga, B, S)
    att_out = _attention(att_slab.reshape(T, 3 * ATT_WIDTH), B, S)
    att_out = att_out.reshape(B, ATT_CLASSES, S // ATT_CLASSES, ATT_WIDTH)
    wr, br = _router_weights(router_group_w[0], router_group_b[0], router_expert_w[0], router_expert_b[0])
    h, u2, logits = _outproj(o_f, o_b, gla_slab, att_out, h, gla_norm_w[0], w_out[0], norm2_w[0], wr, br)
    info, counts = _route(logits)
    dest, pend = _plan(info, counts)
    xs = _dispatch(dest, pend, u2, _moe_capacity(T))
    ys = _experts(pend, xs, expert_w_gate[0], expert_w_up[0], expert_w_down[0])
    out = _combine(dest, ys, info, h, final_norm_w)
    return out.reshape(B, S, D)
```

```python
import functools

import jax
import jax.numpy as jnp
from jax import lax
from jax.experimental import pallas as pl
from jax.experimental.pallas import tpu as pltpu

F32 = jnp.float32
BF16 = jnp.bfloat16

D_MODEL = 1024
GLA_HEADS = 4
GLA_DV = 128
GLA_DK = 64
GLA_KEY_WIDTH = GLA_HEADS * GLA_DK
GLA_VAL_WIDTH = GLA_HEADS * GLA_DV
GLA_GATE_RANK = 16
GLA_TAU = 16.0
GLA_CHUNK = 64
ATT_WIDTH = 512
ATT_HEAD_DIM = 64
ATT_HEADS = 8
ROT_DIM = 16
ROPE_THETA = 500000.0
DILATED_PATTERNS = ((128, 1), (512, 4), (2048, 16))
ATT_RADIUS = 64
MOE_GROUPS = 4
MOE_EXPERTS_PER_GROUP = 8
MOE_N_EXPERTS = 32
MOE_TOP_K = 2
MOE_D_FF = 512
EPS = 1e-6
NEG_INF = -1e30
LOG2E = 1.4426950408889634

LANES = 128
MOE_ROWS = 256
VMEM_LIMIT = 56 * 1024 * 1024


def _cparams(sem):
    return pltpu.CompilerParams(dimension_semantics=sem, vmem_limit_bytes=VMEM_LIMIT)


def _dot(a, b):
    return jnp.dot(a, b, preferred_element_type=F32)


def _dot_nt(a, b):
    return lax.dot_general(a, b, (((1,), (1,)), ((), ())), preferred_element_type=F32)


def _dot_tn(a, b):
    return lax.dot_general(a, b, (((0,), (0,)), ((), ())), preferred_element_type=F32)


def _rms(x, w):
    return x * lax.rsqrt(jnp.mean(x * x, axis=-1, keepdims=True) + EPS) * w


def _inproj_kernel(x_ref, n1_ref, wg_ref, wlr_ref, wa_ref, gw_ref, gb_ref,
                   rc_ref, rs1_ref, rs2_ref, gla_ref, loga_ref, att_ref, stage_ref):
    x = x_ref[...]
    ub = _rms(x, n1_ref[...]).astype(BF16)
    g = _dot(ub, wg_ref[...])
    gla_ref[:, :GLA_KEY_WIDTH] = g[:, :GLA_KEY_WIDTH] * (GLA_DK ** -0.5)
    gla_ref[:, GLA_KEY_WIDTH:] = g[:, GLA_KEY_WIDTH:]
    lr = _dot(ub, wlr_ref[...])
    gate = _dot(lr.astype(BF16), gw_ref[...]) + gb_ref[...]
    loga_ref[...] = (jnp.minimum(gate, 0.0) - jnp.log(1.0 + jnp.exp(-jnp.abs(gate)))) * (1.0 / GLA_TAU)
    a = _dot(ub, wa_ref[...])
    qk = a[:, :2 * ATT_WIDTH]
    reps = 2 * ATT_WIDTH // LANES
    c = jnp.concatenate([rc_ref[...]] * reps, axis=1)
    s1 = jnp.concatenate([rs1_ref[...]] * reps, axis=1)
    s2 = jnp.concatenate([rs2_ref[...]] * reps, axis=1)
    half = ROT_DIM // 2
    n = 2 * ATT_WIDTH
    roped = qk * c + pltpu.roll(qk, n - half, 1) * s1 + pltpu.roll(qk, half, 1) * s2
    qkv = jnp.concatenate([roped[:, :ATT_WIDTH] * (ATT_HEAD_DIM ** -0.5 * LOG2E), roped[:, ATT_WIDTH:],
                           a[:, 2 * ATT_WIDTH:]], axis=1)
    rows = x.shape[0] // ATT_CLASSES
    for j in range(3 * ATT_WIDTH // LANES):
        cols = slice(j * LANES, (j + 1) * LANES)
        stage_ref[j] = qkv[:, cols]
        for c in range(ATT_CLASSES):
            att_ref[c, :, cols] = stage_ref[j, pl.ds(c, rows, stride=ATT_CLASSES), :]


def _rope_lane_tables(S):
    half = ROT_DIM // 2
    inv = ROPE_THETA ** (-(jnp.arange(0, ROT_DIM, 2, dtype=F32) / ROT_DIM))
    ang = inv[:, None] * jnp.arange(S, dtype=F32)[None, :]
    cos, sin = jnp.cos(ang), jnp.sin(ang)
    lane = jnp.arange(LANES) % ATT_HEAD_DIM
    freq = jnp.arange(half)[:, None]
    first = ((lane[None, :] == freq)).astype(F32)
    second = ((lane[None, :] == freq + half)).astype(F32)
    expand = lambda t, sel: lax.dot_general(t, sel, (((0,), (0,)), ((), ())), precision=lax.Precision.HIGHEST)
    rest = (lane >= ROT_DIM).astype(F32)[None, :]
    return expand(cos, first + second) + rest, expand(-sin, first), expand(sin, second)


def _inproj(x2, S, norm1_w, w_in, wf, bfw, wb, bbw, tm=512):
    T = x2.shape[0]
    o_lr = 2 * GLA_KEY_WIDTH + 2 * GLA_VAL_WIDTH
    o_att = o_lr + 2 * GLA_GATE_RANK
    w_main = w_in[:, :o_lr + LANES].astype(BF16)
    wa = w_in[:, o_att:].astype(BF16)
    zeros = jnp.zeros((GLA_GATE_RANK, GLA_KEY_WIDTH), F32)
    gw = jnp.concatenate([jnp.concatenate([wf, zeros], axis=1), jnp.concatenate([zeros, wb], axis=1),
                          jnp.zeros((LANES - 2 * GLA_GATE_RANK, 2 * GLA_KEY_WIDTH), F32)], axis=0).astype(BF16)
    gb = jnp.concatenate([bfw, bbw])[None, :]
    rc, rs1, rs2 = _rope_lane_tables(S)
    nS = S // tm
    row = lambda i: (i, 0)
    const = lambda i: (0, 0)
    pos = lambda i: (i % nS, 0)
    return pl.pallas_call(
        _inproj_kernel,
        grid=(T // tm,),
        in_specs=[
            pl.BlockSpec((tm, D_MODEL), row),
            pl.BlockSpec((1, D_MODEL), const),
            pl.BlockSpec((D_MODEL, o_lr), const),
            pl.BlockSpec((D_MODEL, LANES), lambda i: (0, o_lr // LANES)),
            pl.BlockSpec((D_MODEL, 3 * ATT_WIDTH), const),
            pl.BlockSpec((LANES, 2 * GLA_KEY_WIDTH), const),
            pl.BlockSpec((1, 2 * GLA_KEY_WIDTH), const),
            pl.BlockSpec((tm, LANES), pos),
            pl.BlockSpec((tm, LANES), pos),
            pl.BlockSpec((tm, LANES), pos),
        ],
        out_specs=[
            pl.BlockSpec((tm, o_lr), row),
            pl.BlockSpec((tm, 2 * GLA_KEY_WIDTH), row),
            pl.BlockSpec((None, ATT_CLASSES, tm // ATT_CLASSES, 3 * ATT_WIDTH),
                         lambda i: (i // nS, 0, i % nS, 0)),
        ],
        out_shape=[
            jax.ShapeDtypeStruct((T, o_lr), F32),
            jax.ShapeDtypeStruct((T, 2 * GLA_KEY_WIDTH), F32),
            jax.ShapeDtypeStruct((T // S, ATT_CLASSES, S // ATT_CLASSES, 3 * ATT_WIDTH), F32),
        ],
        scratch_shapes=[pltpu.VMEM((3 * ATT_WIDTH // LANES, tm, LANES), F32)],
        compiler_params=_cparams(("arbitrary",)),
        name="inproj",
    )(x2, norm1_w[None, :], w_main, w_main, wa, gw, gb, rc, rs1, rs2)


def _gla_decays(q, k, v, la, forward, G):
    C = GLA_CHUNK
    R = G * C
    r = lax.broadcasted_iota(jnp.int32, (R, R), 0)
    c = lax.broadcasted_iota(jnp.int32, (R, R), 1)
    same = (r >> 6) == (c >> 6)
    tri = (c <= r) if forward else (c >= r)
    t_mat = jnp.where(same, jnp.where(tri, 1.0, 0.0), 0.0).astype(BF16)
    hi = la.astype(BF16)
    lo = (la - hi.astype(F32)).astype(BF16)
    b = _dot(t_mat, hi) + _dot(t_mat, lo)
    edge = C - 1 if forward else 0
    tot = jnp.concatenate([jnp.broadcast_to(b[g * C + edge:g * C + edge + 1], (C, GLA_KEY_WIDTH))
                           for g in range(G)], axis=0)
    order = list(range(G)) if forward else list(range(G - 1, -1, -1))
    return dict(q_dec=q * jnp.exp(b), k_inv=(k * jnp.exp(-b)).astype(BF16), k_end=k * jnp.exp(tot - b),
                tot=tot, vb=v.astype(BF16), order=order, forward=forward, G=G)


def _gla_scores(prep):
    C, H = GLA_CHUNK, GLA_HEADS
    lane_k = lax.broadcasted_iota(jnp.int32, (C, GLA_KEY_WIDTH), 1)
    qd_heads, scores = {}, {}
    for g in prep["order"]:
        rows = slice(g * C, (g + 1) * C)
        qd = prep["q_dec"][rows]
        qd_heads[g] = jnp.concatenate([jnp.where((lane_k >> 6) == h, qd, 0.0) for h in range(H)],
                                      axis=0).astype(BF16)
        scores[g] = _dot_nt(qd_heads[g], prep["k_inv"][rows])
    return qd_heads, scores


def _gla_chunk_updates(prep):
    C, H, G = GLA_CHUNK, GLA_HEADS, prep["G"]
    k_end, tot, vb = prep["k_end"], prep["tot"], prep["vb"]
    kv, dec_t = {}, {}
    lane = lax.broadcasted_iota(jnp.int32, (GLA_KEY_WIDTH, 2 * C), 1)
    zeros = jnp.zeros((C, GLA_DV), BF16)
    for p in range(G // 2):
        pair = slice(2 * p * C, (2 * p + 2) * C)
        ke_t = k_end[pair].T.astype(BF16)
        tot_t = tot[pair].T
        swapped = pltpu.roll(tot_t, C, 1)
        for half in range(2):
            g = 2 * p + half
            rows = slice(g * C, (g + 1) * C)
            own = (lane < C) if half == 0 else (lane >= C)
            dec_t[g] = jnp.exp(jnp.where(own, tot_t, swapped))
            parts = []
            for h in range(H):
                v_h = vb[rows, h * GLA_DV:(h + 1) * GLA_DV]
                v_pad = jnp.concatenate([v_h, zeros] if half == 0 else [zeros, v_h], axis=0)
                parts.append(_dot(ke_t[h * C:(h + 1) * C], v_pad))
            kv[g] = jnp.concatenate(parts, axis=0)
    return kv, dec_t


def _gla_states(prep, kv, dec_t, s_ref):
    st = s_ref[...]
    states = {}
    for g in prep["order"]:
        states[g] = st.astype(BF16)
        st = st * dec_t[g] + kv[g]
    s_ref[...] = st
    return states


def _gla_outputs(prep, qd_heads, scores, inter, o_ref):
    C, H = GLA_CHUNK, GLA_HEADS
    row_q = lax.broadcasted_iota(jnp.int32, (H * C, C), 0) & (C - 1)
    col_k = lax.broadcasted_iota(jnp.int32, (H * C, C), 1)
    a_mask = (col_k <= row_q) if prep["forward"] else (col_k >= row_q)
    for g in prep["order"]:
        rows = slice(g * C, (g + 1) * C)
        a = jnp.where(a_mask, scores[g], 0.0).astype(BF16)
        vv = prep["vb"][rows]
        o_ref[rows, :] = jnp.concatenate(
            [_dot(a[h * C:(h + 1) * C], vv[:, h * GLA_DV:(h + 1) * GLA_DV]) + inter[g][h * C:(h + 1) * C]
             for h in range(H)], axis=1)


def _gla_kernel(qf_ref, kf_ref, vf_ref, laf_ref, qb_ref, kb_ref, vb_ref, lab_ref,
                of_ref, ob_ref, sf_ref, sb_ref, *, G):
    @pl.when(pl.program_id(1) == 0)
    def _():
        sf_ref[...] = jnp.zeros_like(sf_ref)
        sb_ref[...] = jnp.zeros_like(sb_ref)

    dirs = [(_gla_decays(qf_ref[...], kf_ref[...], vf_ref[...], laf_ref[...], True, G), sf_ref, of_ref),
            (_gla_decays(qb_ref[...], kb_ref[...], vb_ref[...], lab_ref[...], False, G), sb_ref, ob_ref)]
    scored = [_gla_scores(prep) for prep, _, _ in dirs]
    updates = [_gla_chunk_updates(prep) for prep, _, _ in dirs]
    states = [_gla_states(prep, kv, dec_t, s_ref) for (prep, s_ref, _), (kv, dec_t) in zip(dirs, updates)]
    inters = [{g: _dot(qd_heads[g], st[g]) for g in prep["order"]}
              for (prep, _, _), (qd_heads, _), st in zip(dirs, scored, states)]
    for (prep, _, o_ref), (qd_heads, scores), inter in zip(dirs, scored, inters):
        _gla_outputs(prep, qd_heads, scores, inter, o_ref)


def _gla(gla_slab, loga, B, S, G=8):
    T = B * S
    R = G * GLA_CHUNK
    ns = S // R
    fwd = lambda col: (lambda b, i: (b * ns + i, col))
    bwd = lambda col: (lambda b, i: (b * ns + ns - 1 - i, col))
    kw, vw = GLA_KEY_WIDTH, GLA_VAL_WIDTH
    return pl.pallas_call(
        functools.partial(_gla_kernel, G=G),
        grid=(B, ns),
        in_specs=[
            pl.BlockSpec((R, kw), fwd(0)), pl.BlockSpec((R, kw), fwd(1)),
            pl.BlockSpec((R, vw), fwd(1)), pl.BlockSpec((R, kw), fwd(0)),
            pl.BlockSpec((R, kw), bwd(0)), pl.BlockSpec((R, kw), bwd(1)),
            pl.BlockSpec((R, vw), bwd(1)), pl.BlockSpec((R, kw), bwd(1)),
        ],
        out_specs=[pl.BlockSpec((R, vw), fwd(0)), pl.BlockSpec((R, vw), bwd(0))],
        out_shape=[jax.ShapeDtypeStruct((T, vw), F32), jax.ShapeDtypeStruct((T, vw), F32)],
        scratch_shapes=[pltpu.VMEM((kw, GLA_DV), F32), pltpu.VMEM((kw, GLA_DV), F32)],
        compiler_params=_cparams(("arbitrary", "arbitrary")),
        name="gla",
    )(gla_slab, gla_slab, gla_slab, loga, gla_slab, gla_slab, gla_slab, loga)


ATT_CLASSES = 4
ATT_QB = 128
ATT_KB = ATT_QB + 2 * ATT_RADIUS


ATT_UNROLL = 4


def _att_kernel(q_ref, k_ref, v_ref, o_ref, m_ref, l_ref, bias_ref, *, S):
    QB, KB, NC = ATT_QB, ATT_KB, ATT_CLASSES
    L4 = S // NC
    lane = lax.broadcasted_iota(jnp.int32, (QB, LANES), 1)
    head0 = lane < ATT_HEAD_DIM

    @pl.when((pl.program_id(0) == 0) & (pl.program_id(1) == 0))
    def _():
        rowi = lax.broadcasted_iota(jnp.int32, (2 * QB, KB), 0) & (QB - 1)
        coli = lax.broadcasted_iota(jnp.int32, (2 * QB, KB), 1)
        qpos = (rowi & (QB // NC - 1)) * NC + (rowi >> 5)
        kpos = (coli & (KB // NC - 1)) * NC + (coli >> 6)
        for case in range(3):
            bias_ref[0, case] = jnp.where(jnp.abs(rowi - coli + case * ATT_RADIUS) <= ATT_RADIUS, 0.0, NEG_INF)
            bias_ref[1, case] = jnp.where(jnp.abs(qpos - kpos + case * ATT_RADIUS) <= ATT_RADIUS, 0.0, NEG_INF)

    for pi, (_, d) in enumerate(DILATED_PATTERNS):
        L = S // d
        nb = L // QB
        shift = nb.bit_length() - 1
        first = pi == 0
        last = pi == len(DILATED_PATTERNS) - 1

        def scores(n, d=d, L=L, nb=nb, shift=shift):
            cls = n >> shift
            q0 = (n & (nb - 1)) * QB
            ws = jnp.clip(q0 - ATT_RADIUS, 0, L - KB)
            if d == 1:
                qsls = [pl.ds(pl.multiple_of(c * L4 + q0 // NC, QB // NC), QB // NC) for c in range(NC)]
                ksls = [pl.ds(pl.multiple_of(c * L4 + ws // NC, ATT_RADIUS // NC), KB // NC) for c in range(NC)]
            elif d == NC:
                qsls = [pl.ds(pl.multiple_of(cls * L4 + q0, QB), QB)]
                ksls = [pl.ds(pl.multiple_of(cls * L4 + ws, ATT_RADIUS), KB)]
            else:
                base = (cls & (NC - 1)) * L4 + (cls >> 2)
                qsls = [pl.ds(base + NC * q0, QB, stride=NC)]
                ksls = [pl.ds(base + NC * ws, KB, stride=NC)]
            q = jnp.concatenate([q_ref[sl, :] for sl in qsls], axis=0)
            kw = jnp.concatenate([k_ref[sl, :] for sl in ksls], axis=0)
            q2 = jnp.concatenate([jnp.where(head0, q, 0.0), jnp.where(head0, 0.0, q)], axis=0).astype(BF16)
            s = _dot_nt(q2, kw.astype(BF16))
            return qsls, ksls, s + bias_ref[1 if d == 1 else 0, (q0 - ws) >> 6]

        def softmax_pv(qsls, ksls, s):
            m_blk = jnp.max(s, axis=-1, keepdims=True)
            p = jnp.exp2(s - m_blk)
            vw = jnp.concatenate([v_ref[sl, :] for sl in ksls], axis=0)
            v_ones = jnp.concatenate([vw.astype(BF16), jnp.ones((KB, LANES), BF16)], axis=1)
            pv = _dot(p.astype(BF16), v_ones)
            acc_b = jnp.where(head0, pv[:QB, :LANES], pv[QB:, :LANES])
            m_b = jnp.where(head0, m_blk[:QB], m_blk[QB:])
            l_b = jnp.where(head0, pv[:QB, LANES:], pv[QB:, LANES:])
            return qsls, acc_b, m_b, l_b

        def load(ref, sls):
            return jnp.concatenate([ref[sl, :] for sl in sls], axis=0)

        def store(ref, sls, val):
            n = val.shape[0] // len(sls)
            for i, sl in enumerate(sls):
                ref[sl, :] = val[i * n:(i + 1) * n]

        def body(n, carry, first=first, last=last):
            staged = [scores(n * ATT_UNROLL + u) for u in range(ATT_UNROLL)]
            blocks = [softmax_pv(*st) for st in staged]
            for qsls, acc_b, m_b, l_b in blocks:
                if first:
                    acc, m_new, l_new = acc_b, m_b, l_b
                else:
                    m_old = load(m_ref, qsls)
                    m_new = jnp.maximum(m_old, m_b)
                    w_old = jnp.exp2(m_old - m_new)
                    w_blk = jnp.exp2(m_b - m_new)
                    acc = load(o_ref, qsls) * w_old + acc_b * w_blk
                    l_new = load(l_ref, qsls) * w_old + l_b * w_blk
                if last:
                    store(o_ref, qsls, acc / l_new)
                else:
                    store(o_ref, qsls, acc)
                    store(m_ref, qsls, m_new)
                    store(l_ref, qsls, l_new)
            return carry

        lax.fori_loop(0, S // (QB * ATT_UNROLL), body, 0)


def _attention(att_slab, B, S):
    T = B * S
    ncol = ATT_WIDTH // LANES
    return pl.pallas_call(
        functools.partial(_att_kernel, S=S),
        grid=(B, ncol),
        in_specs=[
            pl.BlockSpec((S, LANES), lambda b, h: (b, h)),
            pl.BlockSpec((S, LANES), lambda b, h: (b, ncol + h)),
            pl.BlockSpec((S, LANES), lambda b, h: (b, 2 * ncol + h)),
        ],
        out_specs=pl.BlockSpec((S, LANES), lambda b, h: (b, h)),
        out_shape=jax.ShapeDtypeStruct((T, ATT_WIDTH), F32),
        scratch_shapes=[pltpu.VMEM((S, LANES), F32), pltpu.VMEM((S, LANES), F32),
                        pltpu.VMEM((2, 3, 2 * ATT_QB, ATT_KB), F32)],
        compiler_params=_cparams(("arbitrary", "arbitrary")),
        name="dilated_attention",
    )(att_slab, att_slab, att_slab)


PACK_WORDS = D_MODEL // 2
ROW_TILE = PACK_WORDS // LANES
HIGH_HALF = -65536


def _pack_rows(x):
    bits = lambda v: lax.bitcast_convert_type(v.astype(BF16).astype(F32), jnp.int32)
    low = (bits(x[:, :PACK_WORDS]) >> 16) & 0xFFFF
    return (bits(x[:, PACK_WORDS:]) & HIGH_HALF) | low


def _unpack_rows(w):
    low = lax.bitcast_convert_type(w << 16, F32)
    high = lax.bitcast_convert_type(w & HIGH_HALF, F32)
    return jnp.concatenate([low, high], axis=1).astype(BF16)


def _to_row_tiles(ref, w):
    n = w.shape[0]
    for j in range(ROW_TILE):
        ref[pl.ds(j, n, stride=ROW_TILE), :] = w[:, j * LANES:(j + 1) * LANES]


def _from_row_tiles(ref, n):
    return jnp.concatenate([ref[pl.ds(j, n, stride=ROW_TILE), :] for j in range(ROW_TILE)], axis=1)


def _tile_copy(src_ref, src_row, dst_ref, dst_row, sem):
    src = pl.ds(pl.multiple_of(src_row * ROW_TILE, ROW_TILE), ROW_TILE)
    dst = pl.ds(pl.multiple_of(dst_row * ROW_TILE, ROW_TILE), ROW_TILE)
    return pltpu.make_async_copy(src_ref.at[src], dst_ref.at[dst], sem)


def _outproj_kernel(of_ref, ob_ref, gg_ref, att_ref, x_ref, gnw_ref, wo1_ref, wo2_ref,
                    n2_ref, wr_ref, br_ref, h_ref, u_ref, lg_ref, stage_ref):
    rows = stage_ref.shape[1] // ATT_CLASSES
    for j in range(ATT_WIDTH // LANES):
        for c in range(ATT_CLASSES):
            stage_ref[j, pl.ds(c, rows, stride=ATT_CLASSES), :] = att_ref[c, :, j * LANES:(j + 1) * LANES]
    att = jnp.concatenate([stage_ref[j] for j in range(ATT_WIDTH // LANES)], axis=1)
    o = of_ref[...] + ob_ref[...]
    gate = gg_ref[...]
    gnw = gnw_ref[...]
    parts = []
    for h in range(GLA_HEADS):
        sl = slice(h * GLA_DV, (h + 1) * GLA_DV)
        parts.append(_rms(o[:, sl], gnw))
    y = jnp.concatenate(parts, axis=1) * (gate / (1.0 + jnp.exp(-gate)))
    mix = _dot(y.astype(BF16), wo1_ref[...]) + _dot(att.astype(BF16), wo2_ref[...])
    h = x_ref[...] + mix
    h_ref[...] = h
    u = _rms(h, n2_ref[...])
    _to_row_tiles(u_ref, _pack_rows(u))
    u_hi = u.astype(BF16)
    u_lo = (u - u_hi.astype(F32)).astype(BF16)
    hi_both = _dot_nt(wr_ref[...], u_hi)
    lg_ref[...] = (hi_both[:LANES] + hi_both[LANES:] + _dot_nt(wr_ref[:LANES], u_lo)) + br_ref[...]


def _outproj(o_f, o_b, gla_slab, att_out, x2, gla_norm_w, w_out, norm2_w, wr, br, tm=512):
    T = x2.shape[0]
    nS = att_out.shape[2] * ATT_CLASSES // tm
    row = lambda i: (i, 0)
    const = lambda i: (0, 0)
    wo = w_out.astype(BF16)
    wr_hi = wr.astype(BF16)
    wr_lo = (wr - wr_hi.astype(F32)).astype(BF16)
    wr = jnp.concatenate([wr_hi, wr_lo], axis=0)
    return pl.pallas_call(
        _outproj_kernel,
        grid=(T // tm,),
        in_specs=[
            pl.BlockSpec((tm, GLA_VAL_WIDTH), row),
            pl.BlockSpec((tm, GLA_VAL_WIDTH), row),
            pl.BlockSpec((tm, GLA_VAL_WIDTH), lambda i: (i, 2)),
            pl.BlockSpec((None, ATT_CLASSES, tm // ATT_CLASSES, ATT_WIDTH), lambda i: (i // nS, 0, i % nS, 0)),
            pl.BlockSpec((tm, D_MODEL), row),
            pl.BlockSpec((1, GLA_DV), const),
            pl.BlockSpec((GLA_VAL_WIDTH, D_MODEL), const),
            pl.BlockSpec((ATT_WIDTH, D_MODEL), const),
            pl.BlockSpec((1, D_MODEL), const),
            pl.BlockSpec((2 * LANES, D_MODEL), const),
            pl.BlockSpec((LANES, 1), const),
        ],
        out_specs=[
            pl.BlockSpec((tm, D_MODEL), row),
            pl.BlockSpec((tm * ROW_TILE, LANES), row),
            pl.BlockSpec((LANES, tm), lambda i: (0, i)),
        ],
        out_shape=[
            jax.ShapeDtypeStruct((T, D_MODEL), F32),
            jax.ShapeDtypeStruct((T * ROW_TILE, LANES), jnp.int32),
            jax.ShapeDtypeStruct((LANES, T), F32),
        ],
        scratch_shapes=[pltpu.VMEM((ATT_WIDTH // LANES, tm, LANES), F32)],
        compiler_params=_cparams(("arbitrary",)),
        name="outproj",
    )(o_f, o_b, gla_slab, att_out, x2, gla_norm_w[None, :], wo[:GLA_VAL_WIDTH], wo[GLA_VAL_WIDTH:],
      norm2_w[None, :], wr, br)


INFO_E1, INFO_E2, INFO_R1, INFO_R2, INFO_W1, INFO_W2 = range(6)
ROUTE_ROWS = 40


def _route_kernel(lg_ref, info_ref, cnt_ref, carry_ref):
    @pl.when(pl.program_id(0) == 0)
    def _():
        carry_ref[...] = jnp.zeros_like(carry_ref)

    lg = lg_ref[:ROUTE_ROWS, :]
    tr = lg.shape[1]
    row = lax.broadcasted_iota(jnp.int32, (ROUTE_ROWS, tr), 0)
    big = jnp.int32(1 << 20)
    is_g = (row >= MOE_N_EXPERTS) & (row < MOE_N_EXPERTS + MOE_GROUPS)
    gl = jnp.where(is_g, lg, -jnp.inf)
    gmax = jnp.max(gl, axis=0, keepdims=True)
    gsel = jnp.min(jnp.where(gl == gmax, row - MOE_N_EXPERTS, big), axis=0, keepdims=True)
    g_w = 1.0 / jnp.sum(jnp.where(is_g, jnp.exp(lg - gmax), 0.0), axis=0, keepdims=True)
    in_grp = (row < MOE_N_EXPERTS) & ((row >> 3) == gsel)
    el = jnp.where(in_grp, lg, -jnp.inf)
    v1 = jnp.max(el, axis=0, keepdims=True)
    i1 = jnp.min(jnp.where(el == v1, row, big), axis=0, keepdims=True)
    el2 = jnp.where(row == i1, -jnp.inf, el)
    v2 = jnp.max(el2, axis=0, keepdims=True)
    i2 = jnp.min(jnp.where(el2 == v2, row, big), axis=0, keepdims=True)
    t = jnp.exp(v2 - v1)
    w1 = g_w * (1.0 / (1.0 + t))
    w2 = g_w * (t / (1.0 + t))

    erow = lax.broadcasted_iota(jnp.int32, (MOE_N_EXPERTS, tr), 0)
    hit1 = erow == i1
    hit2 = erow == i2
    member = jnp.where(hit1 | hit2, 1.0, 0.0)
    r = lax.broadcasted_iota(jnp.int32, (tr, tr), 0)
    c = lax.broadcasted_iota(jnp.int32, (tr, tr), 1)
    earlier = jnp.where(r < c, 1.0, 0.0).astype(BF16)
    carry = carry_ref[...]
    prefix = _dot(member.astype(BF16), earlier) + carry[:, 0:1]
    rank1 = jnp.sum(jnp.where(hit1, prefix, 0.0), axis=0, keepdims=True)
    rank2 = jnp.sum(jnp.where(hit2, prefix, 0.0), axis=0, keepdims=True)
    carry = carry + jnp.sum(member, axis=1, keepdims=True)
    carry_ref[...] = carry
    cnt_ref[...] = carry

    zero = jnp.zeros_like(w1)
    info_ref[...] = jnp.concatenate([i1.astype(F32), i2.astype(F32), rank1, rank2, w1, w2, zero, zero], axis=0)


def _route(logits_t, tr=512):
    T = logits_t.shape[1]
    return pl.pallas_call(
        _route_kernel,
        grid=(T // tr,),
        in_specs=[pl.BlockSpec((LANES, tr), lambda i: (0, i))],
        out_specs=[pl.BlockSpec((8, tr), lambda i: (0, i)),
                   pl.BlockSpec((MOE_N_EXPERTS, LANES), lambda i: (0, 0))],
        out_shape=[jax.ShapeDtypeStruct((8, T), F32), jax.ShapeDtypeStruct((MOE_N_EXPERTS, LANES), F32)],
        scratch_shapes=[pltpu.VMEM((MOE_N_EXPERTS, LANES), F32)],
        compiler_params=_cparams(("arbitrary",)),
        name="route",
    )(logits_t)


ROW_UNROLL = 8


def _dispatch_kernel(dest_ref, pend_ref, u_ref, xs_ref, zbuf, sem, zsem, *, td, T, nblk):
    @pl.when(pl.program_id(0) == 0)
    def _():
        zbuf[...] = jnp.zeros_like(zbuf)
        n_used = pend_ref[MOE_N_EXPERTS - 1] >> 8

        def zero_copy(blk):
            start = pl.multiple_of(blk * (MOE_ROWS * ROW_TILE), MOE_ROWS * ROW_TILE)
            return pltpu.make_async_copy(zbuf, xs_ref.at[pl.ds(start, MOE_ROWS * ROW_TILE)], zsem)

        def each_pad_block(fn):
            def per_expert(e, carry):
                prev = jnp.where(e > 0, pend_ref[jnp.maximum(e - 1, 0)], 0)

                @pl.when(pend_ref[e] > prev)
                def _():
                    fn((pend_ref[e] >> 8) - 1)
                return carry

            def per_tail(j, carry):
                @pl.when(n_used + j < nblk)
                def _():
                    fn(n_used + j)
                return carry

            lax.fori_loop(0, MOE_N_EXPERTS, per_expert, 0)
            lax.fori_loop(0, MOE_N_EXPERTS, per_tail, 0)

        each_pad_block(lambda blk: zero_copy(blk).start())
        each_pad_block(lambda blk: zero_copy(blk).wait())

    base = pl.program_id(0) * td

    def issue(g, carry):
        for j in range(ROW_UNROLL):
            r = g * ROW_UNROLL + j
            for k in range(MOE_TOP_K):
                _tile_copy(u_ref, r, xs_ref, dest_ref[k * T + base + r], sem).start(priority=k)
        return carry

    lax.fori_loop(0, td // ROW_UNROLL, issue, 0)
    for k in range(MOE_TOP_K):
        pltpu.make_async_copy(u_ref, xs_ref.at[pl.ds(0, td * ROW_TILE)], sem).wait()


def _dispatch(dest, pend, u2, cap, td=512):
    T = u2.shape[0] // ROW_TILE
    return pl.pallas_call(
        functools.partial(_dispatch_kernel, td=td, T=T, nblk=cap // MOE_ROWS),
        grid_spec=pltpu.PrefetchScalarGridSpec(
            num_scalar_prefetch=2,
            grid=(T // td,),
            in_specs=[pl.BlockSpec((td * ROW_TILE, LANES), lambda i, d, z: (i, 0))],
            out_specs=pl.BlockSpec(memory_space=pl.ANY),
            scratch_shapes=[pltpu.VMEM((MOE_ROWS * ROW_TILE, LANES), jnp.int32),
                            pltpu.SemaphoreType.DMA(()), pltpu.SemaphoreType.DMA(())],
        ),
        out_shape=jax.ShapeDtypeStruct((cap * ROW_TILE, LANES), jnp.int32),
        compiler_params=_cparams(("arbitrary",)),
        name="dispatch",
    )(dest, pend, u2)


def _expert_kernel(pend_ref, x_ref, wg_hbm, wu_hbm, wd_hbm, y_ref,
                   stage_g, stage_u, stage_d, wgb, wub, wdb, cur_ref, sem):
    b = pl.program_id(0)
    last = MOE_N_EXPERTS - 1
    live = b < (pend_ref[last] >> 8)

    def weight_copies(e):
        return (pltpu.make_async_copy(wg_hbm.at[e], stage_g, sem.at[0]),
                pltpu.make_async_copy(wu_hbm.at[e], stage_u, sem.at[1]),
                pltpu.make_async_copy(wd_hbm.at[e], stage_d, sem.at[2]))

    def owner(start, row):
        return lax.while_loop(lambda e: (e < last) & (pend_ref[e] <= row), lambda e: e + 1, start)

    @pl.when(b == 0)
    def _():
        first = owner(0, 0)
        cur_ref[0] = -1
        for c in weight_copies(first):
            c.start()

    @pl.when(live)
    def _():
        prev = cur_ref[0]
        e = owner(jnp.maximum(prev, 0), b * MOE_ROWS)
        cur_ref[0] = e

        @pl.when(e != prev)
        def _():
            for c in weight_copies(e):
                c.wait()
            wgb[...] = stage_g[...].astype(BF16)
            wub[...] = stage_u[...].astype(BF16)
            wdb[...] = stage_d[...].astype(BF16)

            @pl.when(pend_ref[e] < pend_ref[last])
            def _():
                for c in weight_copies(owner(e + 1, pend_ref[e])):
                    c.start(priority=1)

        xb = _unpack_rows(_from_row_tiles(x_ref, MOE_ROWS))
        g = _dot(xb, wgb[...])
        u = _dot(xb, wub[...])
        hid = (g / (1.0 + jnp.exp(-g))) * u
        _to_row_tiles(y_ref, _pack_rows(_dot(hid.astype(BF16), wdb[...])))

    @pl.when(jnp.logical_not(live))
    def _():
        y_ref[...] = jnp.zeros_like(y_ref)


def _experts(pend, xs, w_gate, w_up, w_down):
    cap = xs.shape[0] // ROW_TILE
    nblk = cap // MOE_ROWS
    rows = lambda b, pend: (jnp.minimum(b, (pend[MOE_N_EXPERTS - 1] >> 8) - 1), 0)
    return pl.pallas_call(
        _expert_kernel,
        grid_spec=pltpu.PrefetchScalarGridSpec(
            num_scalar_prefetch=1,
            grid=(nblk,),
            in_specs=[
                pl.BlockSpec((MOE_ROWS * ROW_TILE, LANES), rows),
                pl.BlockSpec(memory_space=pl.ANY),
                pl.BlockSpec(memory_space=pl.ANY),
                pl.BlockSpec(memory_space=pl.ANY),
            ],
            out_specs=pl.BlockSpec((MOE_ROWS * ROW_TILE, LANES), lambda b, pend: (b, 0)),
            scratch_shapes=[pltpu.VMEM((D_MODEL, MOE_D_FF), F32),
                            pltpu.VMEM((D_MODEL, MOE_D_FF), F32),
                            pltpu.VMEM((MOE_D_FF, D_MODEL), F32),
                            pltpu.VMEM((D_MODEL, MOE_D_FF), BF16),
                            pltpu.VMEM((D_MODEL, MOE_D_FF), BF16),
                            pltpu.VMEM((MOE_D_FF, D_MODEL), BF16),
                            pltpu.SMEM((1,), jnp.int32),
                            pltpu.SemaphoreType.DMA((3,))],
        ),
        out_shape=jax.ShapeDtypeStruct((cap * ROW_TILE, LANES), jnp.int32),
        compiler_params=_cparams(("arbitrary",)),
        name="experts",
    )(pend, xs, w_gate, w_up, w_down)


def _combine_kernel(dest_ref, ys_ref, info_ref, h_ref, fw_ref, o_ref, buf, sem, *, tc, T):
    i = pl.program_id(0)
    n = pl.num_programs(0)

    def issue(step, slot):
        base = step * tc

        def body(g, carry):
            for j in range(ROW_UNROLL):
                r = g * ROW_UNROLL + j
                for k in range(MOE_TOP_K):
                    _tile_copy(ys_ref, dest_ref[k * T + base + r], buf.at[slot, k], r,
                               sem.at[slot]).start(priority=k)
            return carry

        lax.fori_loop(0, tc // ROW_UNROLL, body, 0)

    @pl.when(i == 0)
    def _():
        issue(0, 0)

    slot = i % 2

    @pl.when(i + 1 < n)
    def _():
        issue(i + 1, 1 - slot)

    for k in range(MOE_TOP_K):
        pltpu.make_async_copy(ys_ref.at[pl.ds(0, tc * ROW_TILE)], buf.at[slot, k], sem.at[slot]).wait()

    info_t = jnp.concatenate([info_ref[...]] * (LANES // 8), axis=0).T
    w1 = info_t[:, INFO_W1:INFO_W1 + 1]
    w2 = info_t[:, INFO_W2:INFO_W2 + 1]
    y1 = _unpack_rows(_from_row_tiles(buf.at[slot, 0], tc)).astype(F32)
    y2 = _unpack_rows(_from_row_tiles(buf.at[slot, 1], tc)).astype(F32)
    h = h_ref[...] + (y1 * w1 + y2 * w2)
    o_ref[...] = _rms(h, fw_ref[...])


def _combine(dest, ys, info, h, final_w, tc=512):
    T = h.shape[0]
    return pl.pallas_call(
        functools.partial(_combine_kernel, tc=tc, T=T),
        grid_spec=pltpu.PrefetchScalarGridSpec(
            num_scalar_prefetch=1,
            grid=(T // tc,),
            in_specs=[pl.BlockSpec(memory_space=pl.ANY),
                      pl.BlockSpec((8, tc), lambda i, d: (0, i)),
                      pl.BlockSpec((tc, D_MODEL), lambda i, d: (i, 0)),
                      pl.BlockSpec((1, D_MODEL), lambda i, d: (0, 0))],
            out_specs=pl.BlockSpec((tc, D_MODEL), lambda i, d: (i, 0)),
            scratch_shapes=[pltpu.VMEM((2, MOE_TOP_K, tc * ROW_TILE, LANES), jnp.int32),
                            pltpu.SemaphoreType.DMA((2,))],
        ),
        out_shape=jax.ShapeDtypeStruct((T, D_MODEL), F32),
        compiler_params=_cparams(("arbitrary",)),
        name="combine",
    )(dest, ys, info, h, final_w[None, :])


def _plan_kernel(info_ref, cnt_ref, dest_ref, pend_ref):
    cnt = cnt_ref[...].astype(jnp.int32)
    nblk_e = ((cnt + (MOE_ROWS - 1)) >> 8).astype(F32)
    r = lax.broadcasted_iota(jnp.int32, (MOE_N_EXPERTS, MOE_N_EXPERTS), 0)
    c = lax.broadcasted_iota(jnp.int32, (MOE_N_EXPERTS, MOE_N_EXPERTS), 1)
    before = jnp.where(c < r, 1.0, 0.0).astype(BF16)
    first_blk = _dot(before, nblk_e.astype(BF16))
    pstart = first_blk[:, 0:1] * float(MOE_ROWS)
    pend_ref[...] = ((first_blk + nblk_e) * float(MOE_ROWS)).astype(jnp.int32)

    info = info_ref[...]
    erow = lax.broadcasted_iota(jnp.int32, (MOE_N_EXPERTS, info.shape[1]), 0)
    start_of = lambda e: jnp.sum(jnp.where(erow == e.astype(jnp.int32), pstart, 0.0), axis=0, keepdims=True)
    d1 = info[INFO_R1:INFO_R1 + 1] + start_of(info[INFO_E1:INFO_E1 + 1])
    d2 = info[INFO_R2:INFO_R2 + 1] + start_of(info[INFO_E2:INFO_E2 + 1])
    zero = jnp.zeros_like(d1)
    dest_ref[...] = jnp.concatenate([d1, d2] + [zero] * 6, axis=0).astype(jnp.int32)


def _plan(info, counts, tr=2048):
    T = info.shape[1]
    dest8, pend = pl.pallas_call(
        _plan_kernel,
        grid=(T // tr,),
        in_specs=[pl.BlockSpec((8, tr), lambda i: (0, i)),
                  pl.BlockSpec((MOE_N_EXPERTS, LANES), lambda i: (0, 0))],
        out_specs=[pl.BlockSpec((8, tr), lambda i: (0, i)),
                   pl.BlockSpec((MOE_N_EXPERTS, LANES), lambda i: (0, 0))],
        out_shape=[jax.ShapeDtypeStruct((8, T), jnp.int32),
                   jax.ShapeDtypeStruct((MOE_N_EXPERTS, LANES), jnp.int32)],
        compiler_params=_cparams(("arbitrary",)),
        name="plan",
    )(info, counts)
    return dest8[:MOE_TOP_K].reshape(-1), pend[:, 0]


def _moe_capacity(T):
    return (-(-(T * MOE_TOP_K) // MOE_ROWS) + MOE_N_EXPERTS) * MOE_ROWS


def _router_weights(router_group_w, router_group_b, router_expert_w, router_expert_b):
    we = jnp.transpose(router_expert_w, (0, 2, 1)).reshape(MOE_N_EXPERTS, D_MODEL)
    pad = LANES - MOE_N_EXPERTS - MOE_GROUPS
    wr = jnp.concatenate([we, router_group_w.T, jnp.zeros((pad, D_MODEL), F32)], axis=0)
    br = jnp.concatenate([router_expert_b.reshape(-1), router_group_b, jnp.zeros((pad,), F32)])[:, None]
    return wr, br


def kernel(x, norm1_w, w_in, gla_fwd_gate_w, gla_fwd_gate_b, gla_bwd_gate_w, gla_bwd_gate_b,
           gla_norm_w, w_out, norm2_w, router_group_w, router_group_b, router_expert_w,
           router_expert_b, expert_w_gate, expert_w_up, expert_w_down, final_norm_w):
    B, S, D = x.shape
    T = B * S
    assert norm1_w.shape[0] == 1, "single-layer trunk: the final norm is fused into the combine step"
    h = x.reshape(T, D)
    gla_slab, loga, att_slab = _inproj(h, S, norm1_w[0], w_in[0], gla_fwd_gate_w[0], gla_fwd_gate_b[0],
                                       gla_bwd_gate_w[0], gla_bwd_gate_b[0])
    o_f, o_b = _gla(gla_slab, loga, B, S)
    att_out = _attention(att_slab.reshape(T, 3 * ATT_WIDTH), B, S)
    att_out = att_out.reshape(B, ATT_CLASSES, S // ATT_CLASSES, ATT_WIDTH)
    wr, br = _router_weights(router_group_w[0], router_group_b[0], router_expert_w[0], router_expert_b[0])
    h, u2, logits = _outproj(o_f, o_b, gla_slab, att_out, h, gla_norm_w[0], w_out[0], norm2_w[0], wr, br)
    info, counts = _route(logits)
    dest, pend = _plan(info, counts)
    xs = _dispatch(dest, pend, u2, _moe_capacity(T))
    ys = _experts(pend, xs, expert_w_gate[0], expert_w_up[0], expert_w_down[0])
    out = _combine(dest, ys, info, h, final_norm_w)
    return out.reshape(B, S, D)
```

```python
import functools

import jax
import jax.numpy as jnp
from jax import lax
from jax.experimental import pallas as pl
from jax.experimental.pallas import tpu as pltpu

F32 = jnp.float32
BF16 = jnp.bfloat16

D_MODEL = 1024
GLA_HEADS = 4
GLA_DV = 128
GLA_DK = 64
GLA_KEY_WIDTH = GLA_HEADS * GLA_DK
GLA_VAL_WIDTH = GLA_HEADS * GLA_DV
GLA_GATE_RANK = 16
GLA_TAU = 16.0
GLA_CHUNK = 64
ATT_WIDTH = 512
ATT_HEAD_DIM = 64
ATT_HEADS = 8
ROT_DIM = 16
ROPE_THETA = 500000.0
DILATED_PATTERNS = ((128, 1), (512, 4), (2048, 16))
ATT_RADIUS = 64
MOE_GROUPS = 4
MOE_EXPERTS_PER_GROUP = 8
MOE_N_EXPERTS = 32
MOE_TOP_K = 2
MOE_D_FF = 512
EPS = 1e-6
NEG_INF = -1e30
LOG2E = 1.4426950408889634

LANES = 128
MOE_ROWS = 256
VMEM_LIMIT = 56 * 1024 * 1024


def _cparams(sem):
    return pltpu.CompilerParams(dimension_semantics=sem, vmem_limit_bytes=VMEM_LIMIT)


def _dot(a, b):
    return jnp.dot(a, b, preferred_element_type=F32)


def _dot_nt(a, b):
    return lax.dot_general(a, b, (((1,), (1,)), ((), ())), preferred_element_type=F32)


def _dot_tn(a, b):
    return lax.dot_general(a, b, (((0,), (0,)), ((), ())), preferred_element_type=F32)


def _rms(x, w):
    return x * lax.rsqrt(jnp.mean(x * x, axis=-1, keepdims=True) + EPS) * w


def _inproj_kernel(x_ref, n1_ref, wg_ref, wlr_ref, wa_ref, gw_ref, gb_ref,
                   rc_ref, rs1_ref, rs2_ref, gla_ref, loga_ref, att_ref, stage_ref):
    x = x_ref[...]
    ub = _rms(x, n1_ref[...]).astype(BF16)
    g = _dot(ub, wg_ref[...])
    gla_ref[:, :GLA_KEY_WIDTH] = g[:, :GLA_KEY_WIDTH] * (GLA_DK ** -0.5)
    gla_ref[:, GLA_KEY_WIDTH:] = g[:, GLA_KEY_WIDTH:]
    lr = _dot(ub, wlr_ref[...])
    gate = _dot(lr.astype(BF16), gw_ref[...]) + gb_ref[...]
    loga_ref[...] = (jnp.minimum(gate, 0.0) - jnp.log(1.0 + jnp.exp(-jnp.abs(gate)))) * (1.0 / GLA_TAU)
    a = _dot(ub, wa_ref[...])
    qk = a[:, :2 * ATT_WIDTH]
    reps = 2 * ATT_WIDTH // LANES
    c = jnp.concatenate([rc_ref[...]] * reps, axis=1)
    s1 = jnp.concatenate([rs1_ref[...]] * reps, axis=1)
    s2 = jnp.concatenate([rs2_ref[...]] * reps, axis=1)
    half = ROT_DIM // 2
    n = 2 * ATT_WIDTH
    roped = qk * c + pltpu.roll(qk, n - half, 1) * s1 + pltpu.roll(qk, half, 1) * s2
    qkv = jnp.concatenate([roped[:, :ATT_WIDTH] * (ATT_HEAD_DIM ** -0.5 * LOG2E), roped[:, ATT_WIDTH:],
                           a[:, 2 * ATT_WIDTH:]], axis=1)
    rows = x.shape[0] // ATT_CLASSES
    for j in range(3 * ATT_WIDTH // LANES):
        cols = slice(j * LANES, (j + 1) * LANES)
        stage_ref[j] = qkv[:, cols]
        for c in range(ATT_CLASSES):
            att_ref[c, :, cols] = stage_ref[j, pl.ds(c, rows, stride=ATT_CLASSES), :]


def _rope_lane_tables(S):
    half = ROT_DIM // 2
    inv = ROPE_THETA ** (-(jnp.arange(0, ROT_DIM, 2, dtype=F32) / ROT_DIM))
    ang = inv[:, None] * jnp.arange(S, dtype=F32)[None, :]
    cos, sin = jnp.cos(ang), jnp.sin(ang)
    lane = jnp.arange(LANES) % ATT_HEAD_DIM
    freq = jnp.arange(half)[:, None]
    first = ((lane[None, :] == freq)).astype(F32)
    second = ((lane[None, :] == freq + half)).astype(F32)
    expand = lambda t, sel: lax.dot_general(t, sel, (((0,), (0,)), ((), ())), precision=lax.Precision.HIGHEST)
    rest = (lane >= ROT_DIM).astype(F32)[None, :]
    return expand(cos, first + second) + rest, expand(-sin, first), expand(sin, second)


def _inproj(x2, S, norm1_w, w_in, wf, bfw, wb, bbw, tm=512):
    T = x2.shape[0]
    o_lr = 2 * GLA_KEY_WIDTH + 2 * GLA_VAL_WIDTH
    o_att = o_lr + 2 * GLA_GATE_RANK
    w_main = w_in[:, :o_lr + LANES].astype(BF16)
    wa = w_in[:, o_att:].astype(BF16)
    zeros = jnp.zeros((GLA_GATE_RANK, GLA_KEY_WIDTH), F32)
    gw = jnp.concatenate([jnp.concatenate([wf, zeros], axis=1), jnp.concatenate([zeros, wb], axis=1),
                          jnp.zeros((LANES - 2 * GLA_GATE_RANK, 2 * GLA_KEY_WIDTH), F32)], axis=0).astype(BF16)
    gb = jnp.concatenate([bfw, bbw])[None, :]
    rc, rs1, rs2 = _rope_lane_tables(S)
    nS = S // tm
    row = lambda i: (i, 0)
    const = lambda i: (0, 0)
    pos = lambda i: (i % nS, 0)
    return pl.pallas_call(
        _inproj_kernel,
        grid=(T // tm,),
        in_specs=[
            pl.BlockSpec((tm, D_MODEL), row),
            pl.BlockSpec((1, D_MODEL), const),
            pl.BlockSpec((D_MODEL, o_lr), const),
            pl.BlockSpec((D_MODEL, LANES), lambda i: (0, o_lr // LANES)),
            pl.BlockSpec((D_MODEL, 3 * ATT_WIDTH), const),
            pl.BlockSpec((LANES, 2 * GLA_KEY_WIDTH), const),
            pl.BlockSpec((1, 2 * GLA_KEY_WIDTH), const),
            pl.BlockSpec((tm, LANES), pos),
            pl.BlockSpec((tm, LANES), pos),
            pl.BlockSpec((tm, LANES), pos),
        ],
        out_specs=[
            pl.BlockSpec((tm, o_lr), row),
            pl.BlockSpec((tm, 2 * GLA_KEY_WIDTH), row),
            pl.BlockSpec((None, ATT_CLASSES, tm // ATT_CLASSES, 3 * ATT_WIDTH),
                         lambda i: (i // nS, 0, i % nS, 0)),
        ],
        out_shape=[
            jax.ShapeDtypeStruct((T, o_lr), F32),
            jax.ShapeDtypeStruct((T, 2 * GLA_KEY_WIDTH), F32),
            jax.ShapeDtypeStruct((T // S, ATT_CLASSES, S // ATT_CLASSES, 3 * ATT_WIDTH), F32),
        ],
        scratch_shapes=[pltpu.VMEM((3 * ATT_WIDTH // LANES, tm, LANES), F32)],
        compiler_params=_cparams(("arbitrary",)),
        name="inproj",
    )(x2, norm1_w[None, :], w_main, w_main, wa, gw, gb, rc, rs1, rs2)


def _gla_decays(q, k, v, la, forward, G):
    C = GLA_CHUNK
    R = G * C
    r = lax.broadcasted_iota(jnp.int32, (R, R), 0)
    c = lax.broadcasted_iota(jnp.int32, (R, R), 1)
    same = (r >> 6) == (c >> 6)
    tri = (c <= r) if forward else (c >= r)
    t_mat = jnp.where(same, jnp.where(tri, 1.0, 0.0), 0.0).astype(BF16)
    hi = la.astype(BF16)
    lo = (la - hi.astype(F32)).astype(BF16)
    b = _dot(t_mat, hi) + _dot(t_mat, lo)
    edge = C - 1 if forward else 0
    tot = jnp.concatenate([jnp.broadcast_to(b[g * C + edge:g * C + edge + 1], (C, GLA_KEY_WIDTH))
                           for g in range(G)], axis=0)
    order = list(range(G)) if forward else list(range(G - 1, -1, -1))
    return dict(q_dec=q * jnp.exp(b), k_inv=(k * jnp.exp(-b)).astype(BF16), k_end=k * jnp.exp(tot - b),
                tot=tot, vb=v.astype(BF16), order=order, forward=forward, G=G)


def _gla_scores(prep):
    C, H = GLA_CHUNK, GLA_HEADS
    lane_k = lax.broadcasted_iota(jnp.int32, (C, GLA_KEY_WIDTH), 1)
    qd_heads, scores = {}, {}
    for g in prep["order"]:
        rows = slice(g * C, (g + 1) * C)
        qd = prep["q_dec"][rows]
        qd_heads[g] = jnp.concatenate([jnp.where((lane_k >> 6) == h, qd, 0.0) for h in range(H)],
                                      axis=0).astype(BF16)
        scores[g] = _dot_nt(qd_heads[g], prep["k_inv"][rows])
    return qd_heads, scores


def _gla_chunk_updates(prep):
    C, H, G = GLA_CHUNK, GLA_HEADS, prep["G"]
    k_end, tot, vb = prep["k_end"], prep["tot"], prep["vb"]
    kv, dec_t = {}, {}
    lane = lax.broadcasted_iota(jnp.int32, (GLA_KEY_WIDTH, 2 * C), 1)
    zeros = jnp.zeros((C, GLA_DV), BF16)
    for p in range(G // 2):
        pair = slice(2 * p * C, (2 * p + 2) * C)
        ke_t = k_end[pair].T.astype(BF16)
        tot_t = tot[pair].T
        swapped = pltpu.roll(tot_t, C, 1)
        for half in range(2):
            g = 2 * p + half
            rows = slice(g * C, (g + 1) * C)
            own = (lane < C) if half == 0 else (lane >= C)
            dec_t[g] = jnp.exp(jnp.where(own, tot_t, swapped))
            parts = []
            for h in range(H):
                v_h = vb[rows, h * GLA_DV:(h + 1) * GLA_DV]
                v_pad = jnp.concatenate([v_h, zeros] if half == 0 else [zeros, v_h], axis=0)
                parts.append(_dot(ke_t[h * C:(h + 1) * C], v_pad))
            kv[g] = jnp.concatenate(parts, axis=0)
    return kv, dec_t


def _gla_states(prep, kv, dec_t, s_ref):
    st = s_ref[...]
    states = {}
    for g in prep["order"]:
        states[g] = st.astype(BF16)
        st = st * dec_t[g] + kv[g]
    s_ref[...] = st
    return states


def _gla_outputs(prep, qd_heads, scores, inter, o_ref):
    C, H = GLA_CHUNK, GLA_HEADS
    row_q = lax.broadcasted_iota(jnp.int32, (H * C, C), 0) & (C - 1)
    col_k = lax.broadcasted_iota(jnp.int32, (H * C, C), 1)
    a_mask = (col_k <= row_q) if prep["forward"] else (col_k >= row_q)
    for g in prep["order"]:
        rows = slice(g * C, (g + 1) * C)
        a = jnp.where(a_mask, scores[g], 0.0).astype(BF16)
        vv = prep["vb"][rows]
        o_ref[rows, :] = jnp.concatenate(
            [_dot(a[h * C:(h + 1) * C], vv[:, h * GLA_DV:(h + 1) * GLA_DV]) + inter[g][h * C:(h + 1) * C]
             for h in range(H)], axis=1)


def _gla_kernel(qf_ref, kf_ref, vf_ref, laf_ref, qb_ref, kb_ref, vb_ref, lab_ref,
                of_ref, ob_ref, sf_ref, sb_ref, *, G):
    @pl.when(pl.program_id(1) == 0)
    def _():
        sf_ref[...] = jnp.zeros_like(sf_ref)
        sb_ref[...] = jnp.zeros_like(sb_ref)

    dirs = [(_gla_decays(qf_ref[...], kf_ref[...], vf_ref[...], laf_ref[...], True, G), sf_ref, of_ref),
            (_gla_decays(qb_ref[...], kb_ref[...], vb_ref[...], lab_ref[...], False, G), sb_ref, ob_ref)]
    scored = [_gla_scores(prep) for prep, _, _ in dirs]
    updates = [_gla_chunk_updates(prep) for prep, _, _ in dirs]
    states = [_gla_states(prep, kv, dec_t, s_ref) for (prep, s_ref, _), (kv, dec_t) in zip(dirs, updates)]
    inters = [{g: _dot(qd_heads[g], st[g]) for g in prep["order"]}
              for (prep, _, _), (qd_heads, _), st in zip(dirs, scored, states)]
    for (prep, _, o_ref), (qd_heads, scores), inter in zip(dirs, scored, inters):
        _gla_outputs(prep, qd_heads, scores, inter, o_ref)


def _gla(gla_slab, loga, B, S, G=8):
    T = B * S
    R = G * GLA_CHUNK
    ns = S // R
    fwd = lambda col: (lambda b, i: (b * ns + i, col))
    bwd = lambda col: (lambda b, i: (b * ns + ns - 1 - i, col))
    kw, vw = GLA_KEY_WIDTH, GLA_VAL_WIDTH
    return pl.pallas_call(
        functools.partial(_gla_kernel, G=G),
        grid=(B, ns),
        in_specs=[
            pl.BlockSpec((R, kw), fwd(0)), pl.BlockSpec((R, kw), fwd(1)),
            pl.BlockSpec((R, vw), fwd(1)), pl.BlockSpec((R, kw), fwd(0)),
            pl.BlockSpec((R, kw), bwd(0)), pl.BlockSpec((R, kw), bwd(1)),
            pl.BlockSpec((R, vw), bwd(1)), pl.BlockSpec((R, kw), bwd(1)),
        ],
        out_specs=[pl.BlockSpec((R, vw), fwd(0)), pl.BlockSpec((R, vw), bwd(0))],
        out_shape=[jax.ShapeDtypeStruct((T, vw), F32), jax.ShapeDtypeStruct((T, vw), F32)],
        scratch_shapes=[pltpu.VMEM((kw, GLA_DV), F32), pltpu.VMEM((kw, GLA_DV), F32)],
        compiler_params=_cparams(("arbitrary", "arbitrary")),
        name="gla",
    )(gla_slab, gla_slab, gla_slab, loga, gla_slab, gla_slab, gla_slab, loga)


ATT_CLASSES = 4
ATT_QB = 128
ATT_KB = ATT_QB + 2 * ATT_RADIUS


ATT_UNROLL = 4


def _att_kernel(q_ref, k_ref, v_ref, o_ref, m_ref, l_ref, bias_ref, *, S):
    QB, KB, NC = ATT_QB, ATT_KB, ATT_CLASSES
    L4 = S // NC
    lane = lax.broadcasted_iota(jnp.int32, (QB, LANES), 1)
    head0 = lane < ATT_HEAD_DIM

    @pl.when((pl.program_id(0) == 0) & (pl.program_id(1) == 0))
    def _():
        rowi = lax.broadcasted_iota(jnp.int32, (2 * QB, KB), 0) & (QB - 1)
        coli = lax.broadcasted_iota(jnp.int32, (2 * QB, KB), 1)
        qpos = (rowi & (QB // NC - 1)) * NC + (rowi >> 5)
        kpos = (coli & (KB // NC - 1)) * NC + (coli >> 6)
        for case in range(3):
            bias_ref[0, case] = jnp.where(jnp.abs(rowi - coli + case * ATT_RADIUS) <= ATT_RADIUS, 0.0, NEG_INF)
            bias_ref[1, case] = jnp.where(jnp.abs(qpos - kpos + case * ATT_RADIUS) <= ATT_RADIUS, 0.0, NEG_INF)

    for pi, (_, d) in enumerate(DILATED_PATTERNS):
        L = S // d
        nb = L // QB
        shift = nb.bit_length() - 1
        first = pi == 0
        last = pi == len(DILATED_PATTERNS) - 1

        def scores(n, d=d, L=L, nb=nb, shift=shift):
            cls = n >> shift
            q0 = (n & (nb - 1)) * QB
            ws = jnp.clip(q0 - ATT_RADIUS, 0, L - KB)
            if d == 1:
                qsls = [pl.ds(pl.multiple_of(c * L4 + q0 // NC, QB // NC), QB // NC) for c in range(NC)]
                ksls = [pl.ds(pl.multiple_of(c * L4 + ws // NC, ATT_RADIUS // NC), KB // NC) for c in range(NC)]
            elif d == NC:
                qsls = [pl.ds(pl.multiple_of(cls * L4 + q0, QB), QB)]
                ksls = [pl.ds(pl.multiple_of(cls * L4 + ws, ATT_RADIUS), KB)]
            else:
                base = (cls & (NC - 1)) * L4 + (cls >> 2)
                qsls = [pl.ds(base + NC * q0, QB, stride=NC)]
                ksls = [pl.ds(base + NC * ws, KB, stride=NC)]
            q = jnp.concatenate([q_ref[sl, :] for sl in qsls], axis=0)
            kw = jnp.concatenate([k_ref[sl, :] for sl in ksls], axis=0)
            q2 = jnp.concatenate([jnp.where(head0, q, 0.0), jnp.where(head0, 0.0, q)], axis=0).astype(BF16)
            s = _dot_nt(q2, kw.astype(BF16))
            return qsls, ksls, s + bias_ref[1 if d == 1 else 0, (q0 - ws) >> 6]

        def softmax_pv(qsls, ksls, s):
            m_blk = jnp.max(s, axis=-1, keepdims=True)
            p = jnp.exp2(s - m_blk)
            vw = jnp.concatenate([v_ref[sl, :] for sl in ksls], axis=0)
            v_ones = jnp.concatenate([vw.astype(BF16), jnp.ones((KB, LANES), BF16)], axis=1)
            pv = _dot(p.astype(BF16), v_ones)
            acc_b = jnp.where(head0, pv[:QB, :LANES], pv[QB:, :LANES])
            m_b = jnp.where(head0, m_blk[:QB], m_blk[QB:])
            l_b = jnp.where(head0, pv[:QB, LANES:], pv[QB:, LANES:])
            return qsls, acc_b, m_b, l_b

        def load(ref, sls):
            return jnp.concatenate([ref[sl, :] for sl in sls], axis=0)

        def store(ref, sls, val):
            n = val.shape[0] // len(sls)
            for i, sl in enumerate(sls):
                ref[sl, :] = val[i * n:(i + 1) * n]

        def body(n, carry, first=first, last=last):
            staged = [scores(n * ATT_UNROLL + u) for u in range(ATT_UNROLL)]
            blocks = [softmax_pv(*st) for st in staged]
            for qsls, acc_b, m_b, l_b in blocks:
                if first:
                    acc, m_new, l_new = acc_b, m_b, l_b
                else:
                    m_old = load(m_ref, qsls)
                    m_new = jnp.maximum(m_old, m_b)
                    w_old = jnp.exp2(m_old - m_new)
                    w_blk = jnp.exp2(m_b - m_new)
                    acc = load(o_ref, qsls) * w_old + acc_b * w_blk
                    l_new = load(l_ref, qsls) * w_old + l_b * w_blk
                if last:
                    store(o_ref, qsls, acc / l_new)
                else:
                    store(o_ref, qsls, acc)
                    store(m_ref, qsls, m_new)
                    store(l_ref, qsls, l_new)
            return carry

        lax.fori_loop(0, S // (QB * ATT_UNROLL), body, 0)


def _attention(att_slab, B, S):
    T = B * S
    ncol = ATT_WIDTH // LANES
    return pl.pallas_call(
        functools.partial(_att_kernel, S=S),
        grid=(B, ncol),
        in_specs=[
            pl.BlockSpec((S, LANES), lambda b, h: (b, h)),
            pl.BlockSpec((S, LANES), lambda b, h: (b, ncol + h)),
            pl.BlockSpec((S, LANES), lambda b, h: (b, 2 * ncol + h)),
        ],
        out_specs=pl.BlockSpec((S, LANES), lambda b, h: (b, h)),
        out_shape=jax.ShapeDtypeStruct((T, ATT_WIDTH), F32),
        scratch_shapes=[pltpu.VMEM((S, LANES), F32), pltpu.VMEM((S, LANES), F32),
                        pltpu.VMEM((2, 3, 2 * ATT_QB, ATT_KB), F32)],
        compiler_params=_cparams(("arbitrary", "arbitrary")),
        name="dilated_attention",
    )(att_slab, att_slab, att_slab)


PACK_WORDS = D_MODEL // 2
ROW_TILE = PACK_WORDS // LANES
HIGH_HALF = -65536


def _pack_rows(x):
    bits = lambda v: lax.bitcast_convert_type(v.astype(BF16).astype(F32), jnp.int32)
    low = (bits(x[:, :PACK_WORDS]) >> 16) & 0xFFFF
    return (bits(x[:, PACK_WORDS:]) & HIGH_HALF) | low


def _unpack_rows(w):
    low = lax.bitcast_convert_type(w << 16, F32)
    high = lax.bitcast_convert_type(w & HIGH_HALF, F32)
    return jnp.concatenate([low, high], axis=1).astype(BF16)


def _to_row_tiles(ref, w):
    n = w.shape[0]
    for j in range(ROW_TILE):
        ref[pl.ds(j, n, stride=ROW_TILE), :] = w[:, j * LANES:(j + 1) * LANES]


def _from_row_tiles(ref, n):
    return jnp.concatenate([ref[pl.ds(j, n, stride=ROW_TILE), :] for j in range(ROW_TILE)], axis=1)


def _tile_copy(src_ref, src_row, dst_ref, dst_row, sem):
    src = pl.ds(pl.multiple_of(src_row * ROW_TILE, ROW_TILE), ROW_TILE)
    dst = pl.ds(pl.multiple_of(dst_row * ROW_TILE, ROW_TILE), ROW_TILE)
    return pltpu.make_async_copy(src_ref.at[src], dst_ref.at[dst], sem)


def _outproj_kernel(of_ref, ob_ref, gg_ref, att_ref, x_ref, gnw_ref, wo1_ref, wo2_ref,
                    n2_ref, wr_ref, br_ref, h_ref, u_ref, lg_ref, stage_ref):
    rows = stage_ref.shape[1] // ATT_CLASSES
    for j in range(ATT_WIDTH // LANES):
        for c in range(ATT_CLASSES):
            stage_ref[j, pl.ds(c, rows, stride=ATT_CLASSES), :] = att_ref[c, :, j * LANES:(j + 1) * LANES]
    att = jnp.concatenate([stage_ref[j] for j in range(ATT_WIDTH // LANES)], axis=1)
    o = of_ref[...] + ob_ref[...]
    gate = gg_ref[...]
    gnw = gnw_ref[...]
    parts = []
    for h in range(GLA_HEADS):
        sl = slice(h * GLA_DV, (h + 1) * GLA_DV)
        parts.append(_rms(o[:, sl], gnw))
    y = jnp.concatenate(parts, axis=1) * (gate / (1.0 + jnp.exp(-gate)))
    mix = _dot(y.astype(BF16), wo1_ref[...]) + _dot(att.astype(BF16), wo2_ref[...])
    h = x_ref[...] + mix
    h_ref[...] = h
    u = _rms(h, n2_ref[...])
    _to_row_tiles(u_ref, _pack_rows(u))
    u_hi = u.astype(BF16)
    u_lo = (u - u_hi.astype(F32)).astype(BF16)
    hi_both = _dot_nt(wr_ref[...], u_hi)
    lg_ref[...] = (hi_both[:LANES] + hi_both[LANES:] + _dot_nt(wr_ref[:LANES], u_lo)) + br_ref[...]


def _outproj(o_f, o_b, gla_slab, att_out, x2, gla_norm_w, w_out, norm2_w, wr, br, tm=512):
    T = x2.shape[0]
    nS = att_out.shape[2] * ATT_CLASSES // tm
    row = lambda i: (i, 0)
    const = lambda i: (0, 0)
    wo = w_out.astype(BF16)
    wr_hi = wr.astype(BF16)
    wr_lo = (wr - wr_hi.astype(F32)).astype(BF16)
    wr = jnp.concatenate([wr_hi, wr_lo], axis=0)
    return pl.pallas_call(
        _outproj_kernel,
        grid=(T // tm,),
        in_specs=[
            pl.BlockSpec((tm, GLA_VAL_WIDTH), row),
            pl.BlockSpec((tm, GLA_VAL_WIDTH), row),
            pl.BlockSpec((tm, GLA_VAL_WIDTH), lambda i: (i, 2)),
            pl.BlockSpec((None, ATT_CLASSES, tm // ATT_CLASSES, ATT_WIDTH), lambda i: (i // nS, 0, i % nS, 0)),
            pl.BlockSpec((tm, D_MODEL), row),
            pl.BlockSpec((1, GLA_DV), const),
            pl.BlockSpec((GLA_VAL_WIDTH, D_MODEL), const),
            pl.BlockSpec((ATT_WIDTH, D_MODEL), const),
            pl.BlockSpec((1, D_MODEL), const),
            pl.BlockSpec((2 * LANES, D_MODEL), const),
            pl.BlockSpec((LANES, 1), const),
        ],
        out_specs=[
            pl.BlockSpec((tm, D_MODEL), row),
            pl.BlockSpec((tm * ROW_TILE, LANES), row),
            pl.BlockSpec((LANES, tm), lambda i: (0, i)),
        ],
        out_shape=[
            jax.ShapeDtypeStruct((T, D_MODEL), F32),
            jax.ShapeDtypeStruct((T * ROW_TILE, LANES), jnp.int32),
            jax.ShapeDtypeStruct((LANES, T), F32),
        ],
        scratch_shapes=[pltpu.VMEM((ATT_WIDTH // LANES, tm, LANES), F32)],
        compiler_params=_cparams(("arbitrary",)),
        name="outproj",
    )(o_f, o_b, gla_slab, att_out, x2, gla_norm_w[None, :], wo[:GLA_VAL_WIDTH], wo[GLA_VAL_WIDTH:],
      norm2_w[None, :], wr, br)


INFO_E1, INFO_E2, INFO_R1, INFO_R2, INFO_W1, INFO_W2 = range(6)
ROUTE_ROWS = 40


def _route_kernel(lg_ref, info_ref, cnt_ref, carry_ref):
    @pl.when(pl.program_id(0) == 0)
    def _():
        carry_ref[...] = jnp.zeros_like(carry_ref)

    lg = lg_ref[:ROUTE_ROWS, :]
    tr = lg.shape[1]
    row = lax.broadcasted_iota(jnp.int32, (ROUTE_ROWS, tr), 0)
    big = jnp.int32(1 << 20)
    is_g = (row >= MOE_N_EXPERTS) & (row < MOE_N_EXPERTS + MOE_GROUPS)
    gl = jnp.where(is_g, lg, -jnp.inf)
    gmax = jnp.max(gl, axis=0, keepdims=True)
    gsel = jnp.min(jnp.where(gl == gmax, row - MOE_N_EXPERTS, big), axis=0, keepdims=True)
    g_w = 1.0 / jnp.sum(jnp.where(is_g, jnp.exp(lg - gmax), 0.0), axis=0, keepdims=True)
    in_grp = (row < MOE_N_EXPERTS) & ((row >> 3) == gsel)
    el = jnp.where(in_grp, lg, -jnp.inf)
    v1 = jnp.max(el, axis=0, keepdims=True)
    i1 = jnp.min(jnp.where(el == v1, row, big), axis=0, keepdims=True)
    el2 = jnp.where(row == i1, -jnp.inf, el)
    v2 = jnp.max(el2, axis=0, keepdims=True)
    i2 = jnp.min(jnp.where(el2 == v2, row, big), axis=0, keepdims=True)
    t = jnp.exp(v2 - v1)
    w1 = g_w * (1.0 / (1.0 + t))
    w2 = g_w * (t / (1.0 + t))

    erow = lax.broadcasted_iota(jnp.int32, (MOE_N_EXPERTS, tr), 0)
    hit1 = erow == i1
    hit2 = erow == i2
    member = jnp.where(hit1 | hit2, 1.0, 0.0)
    r = lax.broadcasted_iota(jnp.int32, (tr, tr), 0)
    c = lax.broadcasted_iota(jnp.int32, (tr, tr), 1)
    earlier = jnp.where(r < c, 1.0, 0.0).astype(BF16)
    carry = carry_ref[...]
    prefix = _dot(member.astype(BF16), earlier) + carry[:, 0:1]
    rank1 = jnp.sum(jnp.where(hit1, prefix, 0.0), axis=0, keepdims=True)
    rank2 = jnp.sum(jnp.where(hit2, prefix, 0.0), axis=0, keepdims=True)
    carry = carry + jnp.sum(member, axis=1, keepdims=True)
    carry_ref[...] = carry
    cnt_ref[...] = carry

    zero = jnp.zeros_like(w1)
    info_ref[...] = jnp.concatenate([i1.astype(F32), i2.astype(F32), rank1, rank2, w1, w2, zero, zero], axis=0)


def _route(logits_t, tr=512):
    T = logits_t.shape[1]
    return pl.pallas_call(
        _route_kernel,
        grid=(T // tr,),
        in_specs=[pl.BlockSpec((LANES, tr), lambda i: (0, i))],
        out_specs=[pl.BlockSpec((8, tr), lambda i: (0, i)),
                   pl.BlockSpec((MOE_N_EXPERTS, LANES), lambda i: (0, 0))],
        out_shape=[jax.ShapeDtypeStruct((8, T), F32), jax.ShapeDtypeStruct((MOE_N_EXPERTS, LANES), F32)],
        scratch_shapes=[pltpu.VMEM((MOE_N_EXPERTS, LANES), F32)],
        compiler_params=_cparams(("arbitrary",)),
        name="route",
    )(logits_t)


ROW_UNROLL = 8


def _dispatch_kernel(dest_ref, pend_ref, u_ref, xs_ref, zbuf, sem, zsem, *, td, T, nblk):
    @pl.when(pl.program_id(0) == 0)
    def _():
        zbuf[...] = jnp.zeros_like(zbuf)
        n_used = pend_ref[MOE_N_EXPERTS - 1] >> 8

        def zero_copy(blk):
            start = pl.multiple_of(blk * (MOE_ROWS * ROW_TILE), MOE_ROWS * ROW_TILE)
            return pltpu.make_async_copy(zbuf, xs_ref.at[pl.ds(start, MOE_ROWS * ROW_TILE)], zsem)

        def each_pad_block(fn):
            def per_expert(e, carry):
                prev = jnp.where(e > 0, pend_ref[jnp.maximum(e - 1, 0)], 0)

                @pl.when(pend_ref[e] > prev)
                def _():
                    fn((pend_ref[e] >> 8) - 1)
                return carry

            def per_tail(j, carry):
                @pl.when(n_used + j < nblk)
                def _():
                    fn(n_used + j)
                return carry

            lax.fori_loop(0, MOE_N_EXPERTS, per_expert, 0)
            lax.fori_loop(0, MOE_N_EXPERTS, per_tail, 0)

        each_pad_block(lambda blk: zero_copy(blk).start())
        each_pad_block(lambda blk: zero_copy(blk).wait())

    base = pl.program_id(0) * td

    def issue(g, carry):
        for j in range(ROW_UNROLL):
            r = g * ROW_UNROLL + j
            for k in range(MOE_TOP_K):
                _tile_copy(u_ref, r, xs_ref, dest_ref[k * T + base + r], sem).start(priority=k)
        return carry

    lax.fori_loop(0, td // ROW_UNROLL, issue, 0)
    for k in range(MOE_TOP_K):
        pltpu.make_async_copy(u_ref, xs_ref.at[pl.ds(0, td * ROW_TILE)], sem).wait()


def _dispatch(dest, pend, u2, cap, td=512):
    T = u2.shape[0] // ROW_TILE
    return pl.pallas_call(
        functools.partial(_dispatch_kernel, td=td, T=T, nblk=cap // MOE_ROWS),
        grid_spec=pltpu.PrefetchScalarGridSpec(
            num_scalar_prefetch=2,
            grid=(T // td,),
            in_specs=[pl.BlockSpec((td * ROW_TILE, LANES), lambda i, d, z: (i, 0))],
            out_specs=pl.BlockSpec(memory_space=pl.ANY),
            scratch_shapes=[pltpu.VMEM((MOE_ROWS * ROW_TILE, LANES), jnp.int32),
                            pltpu.SemaphoreType.DMA(()), pltpu.SemaphoreType.DMA(())],
        ),
        out_shape=jax.ShapeDtypeStruct((cap * ROW_TILE, LANES), jnp.int32),
        compiler_params=_cparams(("arbitrary",)),
        name="dispatch",
    )(dest, pend, u2)


def _expert_kernel(pend_ref, xs_hbm, wg_hbm, wu_hbm, wd_hbm, ys_hbm,
                   xbuf, ybuf, zbuf, stage_g, stage_u, stage_d, wgb, wub, wdb, xsem, ysem, wsem, zsem, *, nblk):
    last = MOE_N_EXPERTS - 1
    n_used = pend_ref[last] >> 8
    block_rows = MOE_ROWS * ROW_TILE

    def x_copy(b, slot):
        start = pl.multiple_of(b * block_rows, block_rows)
        return pltpu.make_async_copy(xs_hbm.at[pl.ds(start, block_rows)], xbuf.at[slot], xsem.at[slot])

    def y_copy(b, slot):
        start = pl.multiple_of(b * block_rows, block_rows)
        return pltpu.make_async_copy(ybuf.at[slot], ys_hbm.at[pl.ds(start, block_rows)], ysem.at[slot])

    def zero_copy(b):
        start = pl.multiple_of(b * block_rows, block_rows)
        return pltpu.make_async_copy(zbuf, ys_hbm.at[pl.ds(start, block_rows)], zsem)

    def weight_copies(e):
        return (pltpu.make_async_copy(wg_hbm.at[e], stage_g, wsem.at[0]),
                pltpu.make_async_copy(wu_hbm.at[e], stage_u, wsem.at[1]),
                pltpu.make_async_copy(wd_hbm.at[e], stage_d, wsem.at[2]))

    def owner(start, row):
        return lax.while_loop(lambda e: (e < last) & (pend_ref[e] <= row), lambda e: e + 1, start)

    for c in weight_copies(owner(0, 0)):
        c.start()
    x_copy(0, 0).start()

    zbuf[...] = jnp.zeros_like(zbuf)

    def tail(fn):
        def step(b, carry):
            fn(b)
            return carry
        lax.fori_loop(n_used, nblk, step, 0)

    tail(lambda b: zero_copy(b).start())

    def body(b, cur):
        slot = b & 1
        e = owner(jnp.maximum(cur, 0), b * MOE_ROWS)

        @pl.when(e != cur)
        def _():
            for c in weight_copies(e):
                c.wait()
            wgb[...] = stage_g[...].astype(BF16)
            wub[...] = stage_u[...].astype(BF16)
            wdb[...] = stage_d[...].astype(BF16)

            @pl.when(pend_ref[e] < pend_ref[last])
            def _():
                for c in weight_copies(owner(e + 1, pend_ref[e])):
                    c.start()

        x_copy(b, slot).wait()

        @pl.when(b + 1 < n_used)
        def _():
            x_copy(b + 1, 1 - slot).start()

        @pl.when(b >= 2)
        def _():
            y_copy(b - 2, slot).wait()

        xb = _unpack_rows(_from_row_tiles(xbuf.at[slot], MOE_ROWS))
        g = _dot(xb, wgb[...])
        u = _dot(xb, wub[...])
        hid = (g / (1.0 + jnp.exp(-g))) * u
        _to_row_tiles(ybuf.at[slot], _pack_rows(_dot(hid.astype(BF16), wdb[...])))
        y_copy(b, slot).start()
        return e

    lax.fori_loop(0, n_used, body, jnp.int32(-1))

    @pl.when(n_used >= 2)
    def _():
        y_copy(n_used - 2, n_used & 1).wait()
    y_copy(n_used - 1, (n_used - 1) & 1).wait()
    tail(lambda b: zero_copy(b).wait())


def _experts(pend, xs, w_gate, w_up, w_down):
    cap = xs.shape[0] // ROW_TILE
    nblk = cap // MOE_ROWS
    block = (MOE_ROWS * ROW_TILE, LANES)
    anywhere = pl.BlockSpec(memory_space=pl.ANY)
    return pl.pallas_call(
        functools.partial(_expert_kernel, nblk=nblk),
        grid_spec=pltpu.PrefetchScalarGridSpec(
            num_scalar_prefetch=1,
            grid=(1,),
            in_specs=[anywhere, anywhere, anywhere, anywhere],
            out_specs=anywhere,
            scratch_shapes=[pltpu.VMEM((2,) + block, jnp.int32),
                            pltpu.VMEM((2,) + block, jnp.int32),
                            pltpu.VMEM(block, jnp.int32),
                            pltpu.VMEM((D_MODEL, MOE_D_FF), F32),
                            pltpu.VMEM((D_MODEL, MOE_D_FF), F32),
                            pltpu.VMEM((MOE_D_FF, D_MODEL), F32),
                            pltpu.VMEM((D_MODEL, MOE_D_FF), BF16),
                            pltpu.VMEM((D_MODEL, MOE_D_FF), BF16),
                            pltpu.VMEM((MOE_D_FF, D_MODEL), BF16),
                            pltpu.SemaphoreType.DMA((2,)),
                            pltpu.SemaphoreType.DMA((2,)),
                            pltpu.SemaphoreType.DMA((3,)),
                            pltpu.SemaphoreType.DMA(())],
        ),
        out_shape=jax.ShapeDtypeStruct((cap * ROW_TILE, LANES), jnp.int32),
        compiler_params=_cparams(("arbitrary",)),
        name="experts",
    )(pend, xs, w_gate, w_up, w_down)


def _combine_kernel(dest_ref, ys_ref, info_ref, h_ref, fw_ref, o_ref, buf, sem, *, tc, T):
    i = pl.program_id(0)
    n = pl.num_programs(0)

    def issue(step, slot):
        base = step * tc

        def body(g, carry):
            for j in range(ROW_UNROLL):
                r = g * ROW_UNROLL + j
                for k in range(MOE_TOP_K):
                    _tile_copy(ys_ref, dest_ref[k * T + base + r], buf.at[slot, k], r,
                               sem.at[slot]).start(priority=k)
            return carry

        lax.fori_loop(0, tc // ROW_UNROLL, body, 0)

    @pl.when(i == 0)
    def _():
        issue(0, 0)

    slot = i % 2

    @pl.when(i + 1 < n)
    def _():
        issue(i + 1, 1 - slot)

    for k in range(MOE_TOP_K):
        pltpu.make_async_copy(ys_ref.at[pl.ds(0, tc * ROW_TILE)], buf.at[slot, k], sem.at[slot]).wait()

    info_t = jnp.concatenate([info_ref[...]] * (LANES // 8), axis=0).T
    w1 = info_t[:, INFO_W1:INFO_W1 + 1]
    w2 = info_t[:, INFO_W2:INFO_W2 + 1]
    y1 = _unpack_rows(_from_row_tiles(buf.at[slot, 0], tc)).astype(F32)
    y2 = _unpack_rows(_from_row_tiles(buf.at[slot, 1], tc)).astype(F32)
    h = h_ref[...] + (y1 * w1 + y2 * w2)
    o_ref[...] = _rms(h, fw_ref[...])


def _combine(dest, ys, info, h, final_w, tc=512):
    T = h.shape[0]
    return pl.pallas_call(
        functools.partial(_combine_kernel, tc=tc, T=T),
        grid_spec=pltpu.PrefetchScalarGridSpec(
            num_scalar_prefetch=1,
            grid=(T // tc,),
            in_specs=[pl.BlockSpec(memory_space=pl.ANY),
                      pl.BlockSpec((8, tc), lambda i, d: (0, i)),
                      pl.BlockSpec((tc, D_MODEL), lambda i, d: (i, 0)),
                      pl.BlockSpec((1, D_MODEL), lambda i, d: (0, 0))],
            out_specs=pl.BlockSpec((tc, D_MODEL), lambda i, d: (i, 0)),
            scratch_shapes=[pltpu.VMEM((2, MOE_TOP_K, tc * ROW_TILE, LANES), jnp.int32),
                            pltpu.SemaphoreType.DMA((2,))],
        ),
        out_shape=jax.ShapeDtypeStruct((T, D_MODEL), F32),
        compiler_params=_cparams(("arbitrary",)),
        name="combine",
    )(dest, ys, info, h, final_w[None, :])


def _plan_kernel(info_ref, cnt_ref, dest_ref, pend_ref):
    cnt = cnt_ref[...].astype(jnp.int32)
    nblk_e = ((cnt + (MOE_ROWS - 1)) >> 8).astype(F32)
    r = lax.broadcasted_iota(jnp.int32, (MOE_N_EXPERTS, MOE_N_EXPERTS), 0)
    c = lax.broadcasted_iota(jnp.int32, (MOE_N_EXPERTS, MOE_N_EXPERTS), 1)
    before = jnp.where(c < r, 1.0, 0.0).astype(BF16)
    first_blk = _dot(before, nblk_e.astype(BF16))
    pstart = first_blk[:, 0:1] * float(MOE_ROWS)
    pend_ref[...] = ((first_blk + nblk_e) * float(MOE_ROWS)).astype(jnp.int32)

    info = info_ref[...]
    erow = lax.broadcasted_iota(jnp.int32, (MOE_N_EXPERTS, info.shape[1]), 0)
    start_of = lambda e: jnp.sum(jnp.where(erow == e.astype(jnp.int32), pstart, 0.0), axis=0, keepdims=True)
    d1 = info[INFO_R1:INFO_R1 + 1] + start_of(info[INFO_E1:INFO_E1 + 1])
    d2 = info[INFO_R2:INFO_R2 + 1] + start_of(info[INFO_E2:INFO_E2 + 1])
    zero = jnp.zeros_like(d1)
    dest_ref[...] = jnp.concatenate([d1, d2] + [zero] * 6, axis=0).astype(jnp.int32)


def _plan(info, counts, tr=2048):
    T = info.shape[1]
    dest8, pend = pl.pallas_call(
        _plan_kernel,
        grid=(T // tr,),
        in_specs=[pl.BlockSpec((8, tr), lambda i: (0, i)),
                  pl.BlockSpec((MOE_N_EXPERTS, LANES), lambda i: (0, 0))],
        out_specs=[pl.BlockSpec((8, tr), lambda i: (0, i)),
                   pl.BlockSpec((MOE_N_EXPERTS, LANES), lambda i: (0, 0))],
        out_shape=[jax.ShapeDtypeStruct((8, T), jnp.int32),
                   jax.ShapeDtypeStruct((MOE_N_EXPERTS, LANES), jnp.int32)],
        compiler_params=_cparams(("arbitrary",)),
        name="plan",
    )(info, counts)
    return dest8[:MOE_TOP_K].reshape(-1), pend[:, 0]


def _moe_capacity(T):
    return (-(-(T * MOE_TOP_K) // MOE_ROWS) + MOE_N_EXPERTS) * MOE_ROWS


def _router_weights(router_group_w, router_group_b, router_expert_w, router_expert_b):
    we = jnp.transpose(router_expert_w, (0, 2, 1)).reshape(MOE_N_EXPERTS, D_MODEL)
    pad = LANES - MOE_N_EXPERTS - MOE_GROUPS
    wr = jnp.concatenate([we, router_group_w.T, jnp.zeros((pad, D_MODEL), F32)], axis=0)
    br = jnp.concatenate([router_expert_b.reshape(-1), router_group_b, jnp.zeros((pad,), F32)])[:, None]
    return wr, br


def kernel(x, norm1_w, w_in, gla_fwd_gate_w, gla_fwd_gate_b, gla_bwd_gate_w, gla_bwd_gate_b,
           gla_norm_w, w_out, norm2_w, router_group_w, router_group_b, router_expert_w,
           router_expert_b, expert_w_gate, expert_w_up, expert_w_down, final_norm_w):
    B, S, D = x.shape
    T = B * S
    assert norm1_w.shape[0] == 1, "single-layer trunk: the final norm is fused into the combine step"
    h = x.reshape(T, D)
    gla_slab, loga, att_slab = _inproj(h, S, norm1_w[0], w_in[0], gla_fwd_gate_w[0], gla_fwd_gate_b[0],
                                       gla_bwd_gate_w[0], gla_bwd_gate_b[0])
    o_f, o_b = _gla(gla_slab, loga, B, S)
    att_out = _attention(att_slab.reshape(T, 3 * ATT_WIDTH), B, S)
    att_out = att_out.reshape(B, ATT_CLASSES, S // ATT_CLASSES, ATT_WIDTH)
    wr, br = _router_weights(router_group_w[0], router_group_b[0], router_expert_w[0], router_expert_b[0])
    h, u2, logits = _outproj(o_f, o_b, gla_slab, att_out, h, gla_norm_w[0], w_out[0], norm2_w[0], wr, br)
    info, counts = _route(logits)
    dest, pend = _plan(info, counts)
    xs = _dispatch(dest, pend, u2, _moe_capacity(T))
    ys = _experts(pend, xs, expert_w_gate[0], expert_w_up[0], expert_w_down[0])
    out = _combine(dest, ys, info, h, final_norm_w)
    return out.reshape(B, S, D)
```

```python
import functools

import jax
import jax.numpy as jnp
from jax import lax
from jax.experimental import pallas as pl
from jax.experimental.pallas import tpu as pltpu

F32 = jnp.float32
BF16 = jnp.bfloat16

D_MODEL = 1024
GLA_HEADS = 4
GLA_DV = 128
GLA_DK = 64
GLA_KEY_WIDTH = GLA_HEADS * GLA_DK
GLA_VAL_WIDTH = GLA_HEADS * GLA_DV
GLA_GATE_RANK = 16
GLA_TAU = 16.0
GLA_CHUNK = 64
ATT_WIDTH = 512
ATT_HEAD_DIM = 64
ATT_HEADS = 8
ROT_DIM = 16
ROPE_THETA = 500000.0
DILATED_PATTERNS = ((128, 1), (512, 4), (2048, 16))
ATT_RADIUS = 64
MOE_GROUPS = 4
MOE_EXPERTS_PER_GROUP = 8
MOE_N_EXPERTS = 32
MOE_TOP_K = 2
MOE_D_FF = 512
EPS = 1e-6
NEG_INF = -1e30
LOG2E = 1.4426950408889634

LANES = 128
MOE_ROWS = 256
VMEM_LIMIT = 56 * 1024 * 1024


def _cparams(sem):
    return pltpu.CompilerParams(dimension_semantics=sem, vmem_limit_bytes=VMEM_LIMIT)


def _dot(a, b):
    return jnp.dot(a, b, preferred_element_type=F32)


def _dot_nt(a, b):
    return lax.dot_general(a, b, (((1,), (1,)), ((), ())), preferred_element_type=F32)


def _dot_tn(a, b):
    return lax.dot_general(a, b, (((0,), (0,)), ((), ())), preferred_element_type=F32)


def _rms(x, w):
    return x * lax.rsqrt(jnp.mean(x * x, axis=-1, keepdims=True) + EPS) * w


def _inproj_kernel(x_ref, n1_ref, wg_ref, wlr_ref, wa_ref, gw_ref, gb_ref,
                   rc_ref, rs1_ref, rs2_ref, gla_ref, loga_ref, att_ref, stage_ref):
    x = x_ref[...]
    ub = _rms(x, n1_ref[...]).astype(BF16)
    g = _dot(ub, wg_ref[...])
    gla_ref[:, :GLA_KEY_WIDTH] = g[:, :GLA_KEY_WIDTH] * (GLA_DK ** -0.5)
    gla_ref[:, GLA_KEY_WIDTH:] = g[:, GLA_KEY_WIDTH:]
    lr = _dot(ub, wlr_ref[...])
    gate = _dot(lr.astype(BF16), gw_ref[...]) + gb_ref[...]
    loga_ref[...] = (jnp.minimum(gate, 0.0) - jnp.log(1.0 + jnp.exp(-jnp.abs(gate)))) * (1.0 / GLA_TAU)
    a = _dot(ub, wa_ref[...])
    qk = a[:, :2 * ATT_WIDTH]
    reps = 2 * ATT_WIDTH // LANES
    c = jnp.concatenate([rc_ref[...]] * reps, axis=1)
    s1 = jnp.concatenate([rs1_ref[...]] * reps, axis=1)
    s2 = jnp.concatenate([rs2_ref[...]] * reps, axis=1)
    half = ROT_DIM // 2
    n = 2 * ATT_WIDTH
    roped = qk * c + pltpu.roll(qk, n - half, 1) * s1 + pltpu.roll(qk, half, 1) * s2
    qkv = jnp.concatenate([roped[:, :ATT_WIDTH] * (ATT_HEAD_DIM ** -0.5 * LOG2E), roped[:, ATT_WIDTH:],
                           a[:, 2 * ATT_WIDTH:]], axis=1)
    rows = x.shape[0] // ATT_CLASSES
    for j in range(3 * ATT_WIDTH // LANES):
        cols = slice(j * LANES, (j + 1) * LANES)
        stage_ref[j] = qkv[:, cols]
        for c in range(ATT_CLASSES):
            att_ref[c, :, cols] = stage_ref[j, pl.ds(c, rows, stride=ATT_CLASSES), :]


def _rope_lane_tables(S):
    half = ROT_DIM // 2
    inv = ROPE_THETA ** (-(jnp.arange(0, ROT_DIM, 2, dtype=F32) / ROT_DIM))
    ang = inv[:, None] * jnp.arange(S, dtype=F32)[None, :]
    cos, sin = jnp.cos(ang), jnp.sin(ang)
    lane = jnp.arange(LANES) % ATT_HEAD_DIM
    freq = jnp.arange(half)[:, None]
    first = ((lane[None, :] == freq)).astype(F32)
    second = ((lane[None, :] == freq + half)).astype(F32)
    expand = lambda t, sel: lax.dot_general(t, sel, (((0,), (0,)), ((), ())), precision=lax.Precision.HIGHEST)
    rest = (lane >= ROT_DIM).astype(F32)[None, :]
    return expand(cos, first + second) + rest, expand(-sin, first), expand(sin, second)


def _inproj(x2, S, norm1_w, w_in, wf, bfw, wb, bbw, tm=512):
    T = x2.shape[0]
    o_lr = 2 * GLA_KEY_WIDTH + 2 * GLA_VAL_WIDTH
    o_att = o_lr + 2 * GLA_GATE_RANK
    w_main = w_in[:, :o_lr + LANES].astype(BF16)
    wa = w_in[:, o_att:].astype(BF16)
    zeros = jnp.zeros((GLA_GATE_RANK, GLA_KEY_WIDTH), F32)
    gw = jnp.concatenate([jnp.concatenate([wf, zeros], axis=1), jnp.concatenate([zeros, wb], axis=1),
                          jnp.zeros((LANES - 2 * GLA_GATE_RANK, 2 * GLA_KEY_WIDTH), F32)], axis=0).astype(BF16)
    gb = jnp.concatenate([bfw, bbw])[None, :]
    rc, rs1, rs2 = _rope_lane_tables(S)
    nS = S // tm
    row = lambda i: (i, 0)
    const = lambda i: (0, 0)
    pos = lambda i: (i % nS, 0)
    return pl.pallas_call(
        _inproj_kernel,
        grid=(T // tm,),
        in_specs=[
            pl.BlockSpec((tm, D_MODEL), row),
            pl.BlockSpec((1, D_MODEL), const),
            pl.BlockSpec((D_MODEL, o_lr), const),
            pl.BlockSpec((D_MODEL, LANES), lambda i: (0, o_lr // LANES)),
            pl.BlockSpec((D_MODEL, 3 * ATT_WIDTH), const),
            pl.BlockSpec((LANES, 2 * GLA_KEY_WIDTH), const),
            pl.BlockSpec((1, 2 * GLA_KEY_WIDTH), const),
            pl.BlockSpec((tm, LANES), pos),
            pl.BlockSpec((tm, LANES), pos),
            pl.BlockSpec((tm, LANES), pos),
        ],
        out_specs=[
            pl.BlockSpec((tm, o_lr), row),
            pl.BlockSpec((tm, 2 * GLA_KEY_WIDTH), row),
            pl.BlockSpec((None, ATT_CLASSES, tm // ATT_CLASSES, 3 * ATT_WIDTH),
                         lambda i: (i // nS, 0, i % nS, 0)),
        ],
        out_shape=[
            jax.ShapeDtypeStruct((T, o_lr), F32),
            jax.ShapeDtypeStruct((T, 2 * GLA_KEY_WIDTH), F32),
            jax.ShapeDtypeStruct((T // S, ATT_CLASSES, S // ATT_CLASSES, 3 * ATT_WIDTH), F32),
        ],
        scratch_shapes=[pltpu.VMEM((3 * ATT_WIDTH // LANES, tm, LANES), F32)],
        compiler_params=_cparams(("arbitrary",)),
        name="inproj",
    )(x2, norm1_w[None, :], w_main, w_main, wa, gw, gb, rc, rs1, rs2)


def _gla_decays(q, k, v, la, forward, G):
    C = GLA_CHUNK
    R = G * C
    r = lax.broadcasted_iota(jnp.int32, (R, R), 0)
    c = lax.broadcasted_iota(jnp.int32, (R, R), 1)
    same = (r >> 6) == (c >> 6)
    tri = (c <= r) if forward else (c >= r)
    t_mat = jnp.where(same, jnp.where(tri, 1.0, 0.0), 0.0).astype(BF16)
    hi = la.astype(BF16)
    lo = (la - hi.astype(F32)).astype(BF16)
    b = _dot(t_mat, hi) + _dot(t_mat, lo)
    edge = C - 1 if forward else 0
    tot = jnp.concatenate([jnp.broadcast_to(b[g * C + edge:g * C + edge + 1], (C, GLA_KEY_WIDTH))
                           for g in range(G)], axis=0)
    order = list(range(G)) if forward else list(range(G - 1, -1, -1))
    return dict(q_dec=q * jnp.exp(b), k_inv=(k * jnp.exp(-b)).astype(BF16), k_end=k * jnp.exp(tot - b),
                tot=tot, vb=v.astype(BF16), order=order, forward=forward, G=G)


def _gla_scores(prep):
    C, H = GLA_CHUNK, GLA_HEADS
    lane_k = lax.broadcasted_iota(jnp.int32, (C, GLA_KEY_WIDTH), 1)
    qd_heads, scores = {}, {}
    for g in prep["order"]:
        rows = slice(g * C, (g + 1) * C)
        qd = prep["q_dec"][rows]
        qd_heads[g] = jnp.concatenate([jnp.where((lane_k >> 6) == h, qd, 0.0) for h in range(H)],
                                      axis=0).astype(BF16)
        scores[g] = _dot_nt(qd_heads[g], prep["k_inv"][rows])
    return qd_heads, scores


def _gla_chunk_updates(prep):
    C, H, G = GLA_CHUNK, GLA_HEADS, prep["G"]
    k_end, tot, vb = prep["k_end"], prep["tot"], prep["vb"]
    kv, dec_t = {}, {}
    lane = lax.broadcasted_iota(jnp.int32, (GLA_KEY_WIDTH, 2 * C), 1)
    zeros = jnp.zeros((C, GLA_DV), BF16)
    for p in range(G // 2):
        pair = slice(2 * p * C, (2 * p + 2) * C)
        ke_t = k_end[pair].T.astype(BF16)
        tot_t = tot[pair].T
        swapped = pltpu.roll(tot_t, C, 1)
        for half in range(2):
            g = 2 * p + half
            rows = slice(g * C, (g + 1) * C)
            own = (lane < C) if half == 0 else (lane >= C)
            dec_t[g] = jnp.exp(jnp.where(own, tot_t, swapped))
            parts = []
            for h in range(H):
                v_h = vb[rows, h * GLA_DV:(h + 1) * GLA_DV]
                v_pad = jnp.concatenate([v_h, zeros] if half == 0 else [zeros, v_h], axis=0)
                parts.append(_dot(ke_t[h * C:(h + 1) * C], v_pad))
            kv[g] = jnp.concatenate(parts, axis=0)
    return kv, dec_t


def _gla_states(prep, kv, dec_t, s_ref):
    st = s_ref[...]
    states = {}
    for g in prep["order"]:
        states[g] = st.astype(BF16)
        st = st * dec_t[g] + kv[g]
    s_ref[...] = st
    return states


def _gla_outputs(prep, qd_heads, scores, inter, o_ref):
    C, H = GLA_CHUNK, GLA_HEADS
    row_q = lax.broadcasted_iota(jnp.int32, (H * C, C), 0) & (C - 1)
    col_k = lax.broadcasted_iota(jnp.int32, (H * C, C), 1)
    a_mask = (col_k <= row_q) if prep["forward"] else (col_k >= row_q)
    for g in prep["order"]:
        rows = slice(g * C, (g + 1) * C)
        a = jnp.where(a_mask, scores[g], 0.0).astype(BF16)
        vv = prep["vb"][rows]
        o_ref[rows, :] = jnp.concatenate(
            [_dot(a[h * C:(h + 1) * C], vv[:, h * GLA_DV:(h + 1) * GLA_DV]) + inter[g][h * C:(h + 1) * C]
             for h in range(H)], axis=1)


def _gla_kernel(qf_ref, kf_ref, vf_ref, laf_ref, qb_ref, kb_ref, vb_ref, lab_ref,
                of_ref, ob_ref, sf_ref, sb_ref, *, G):
    @pl.when(pl.program_id(1) == 0)
    def _():
        sf_ref[...] = jnp.zeros_like(sf_ref)
        sb_ref[...] = jnp.zeros_like(sb_ref)

    dirs = [(_gla_decays(qf_ref[...], kf_ref[...], vf_ref[...], laf_ref[...], True, G), sf_ref, of_ref),
            (_gla_decays(qb_ref[...], kb_ref[...], vb_ref[...], lab_ref[...], False, G), sb_ref, ob_ref)]
    scored = [_gla_scores(prep) for prep, _, _ in dirs]
    updates = [_gla_chunk_updates(prep) for prep, _, _ in dirs]
    states = [_gla_states(prep, kv, dec_t, s_ref) for (prep, s_ref, _), (kv, dec_t) in zip(dirs, updates)]
    inters = [{g: _dot(qd_heads[g], st[g]) for g in prep["order"]}
              for (prep, _, _), (qd_heads, _), st in zip(dirs, scored, states)]
    for (prep, _, o_ref), (qd_heads, scores), inter in zip(dirs, scored, inters):
        _gla_outputs(prep, qd_heads, scores, inter, o_ref)


def _gla(gla_slab, loga, B, S, G=8):
    T = B * S
    R = G * GLA_CHUNK
    ns = S // R
    fwd = lambda col: (lambda b, i: (b * ns + i, col))
    bwd = lambda col: (lambda b, i: (b * ns + ns - 1 - i, col))
    kw, vw = GLA_KEY_WIDTH, GLA_VAL_WIDTH
    return pl.pallas_call(
        functools.partial(_gla_kernel, G=G),
        grid=(B, ns),
        in_specs=[
            pl.BlockSpec((R, kw), fwd(0)), pl.BlockSpec((R, kw), fwd(1)),
            pl.BlockSpec((R, vw), fwd(1)), pl.BlockSpec((R, kw), fwd(0)),
            pl.BlockSpec((R, kw), bwd(0)), pl.BlockSpec((R, kw), bwd(1)),
            pl.BlockSpec((R, vw), bwd(1)), pl.BlockSpec((R, kw), bwd(1)),
        ],
        out_specs=[pl.BlockSpec((R, vw), fwd(0)), pl.BlockSpec((R, vw), bwd(0))],
        out_shape=[jax.ShapeDtypeStruct((T, vw), F32), jax.ShapeDtypeStruct((T, vw), F32)],
        scratch_shapes=[pltpu.VMEM((kw, GLA_DV), F32), pltpu.VMEM((kw, GLA_DV), F32)],
        compiler_params=_cparams(("arbitrary", "arbitrary")),
        name="gla",
    )(gla_slab, gla_slab, gla_slab, loga, gla_slab, gla_slab, gla_slab, loga)


ATT_CLASSES = 4
ATT_QB = 128
ATT_KB = ATT_QB + 2 * ATT_RADIUS


ATT_UNROLL = 4


def _att_kernel(q_ref, k_ref, v_ref, o_ref, m_ref, l_ref, bias_ref, *, S):
    QB, KB, NC = ATT_QB, ATT_KB, ATT_CLASSES
    L4 = S // NC
    lane = lax.broadcasted_iota(jnp.int32, (QB, LANES), 1)
    head0 = lane < ATT_HEAD_DIM

    @pl.when((pl.program_id(0) == 0) & (pl.program_id(1) == 0))
    def _():
        rowi = lax.broadcasted_iota(jnp.int32, (2 * QB, KB), 0) & (QB - 1)
        coli = lax.broadcasted_iota(jnp.int32, (2 * QB, KB), 1)
        qpos = (rowi & (QB // NC - 1)) * NC + (rowi >> 5)
        kpos = (coli & (KB // NC - 1)) * NC + (coli >> 6)
        for case in range(3):
            bias_ref[0, case] = jnp.where(jnp.abs(rowi - coli + case * ATT_RADIUS) <= ATT_RADIUS, 0.0, NEG_INF)
            bias_ref[1, case] = jnp.where(jnp.abs(qpos - kpos + case * ATT_RADIUS) <= ATT_RADIUS, 0.0, NEG_INF)

    for pi, (_, d) in enumerate(DILATED_PATTERNS):
        L = S // d
        nb = L // QB
        shift = nb.bit_length() - 1
        first = pi == 0
        last = pi == len(DILATED_PATTERNS) - 1

        def scores(n, d=d, L=L, nb=nb, shift=shift):
            cls = n >> shift
            q0 = (n & (nb - 1)) * QB
            ws = jnp.clip(q0 - ATT_RADIUS, 0, L - KB)
            if d == 1:
                qsls = [pl.ds(pl.multiple_of(c * L4 + q0 // NC, QB // NC), QB // NC) for c in range(NC)]
                ksls = [pl.ds(pl.multiple_of(c * L4 + ws // NC, ATT_RADIUS // NC), KB // NC) for c in range(NC)]
            elif d == NC:
                qsls = [pl.ds(pl.multiple_of(cls * L4 + q0, QB), QB)]
                ksls = [pl.ds(pl.multiple_of(cls * L4 + ws, ATT_RADIUS), KB)]
            else:
                base = (cls & (NC - 1)) * L4 + (cls >> 2)
                qsls = [pl.ds(base + NC * q0, QB, stride=NC)]
                ksls = [pl.ds(base + NC * ws, KB, stride=NC)]
            q = jnp.concatenate([q_ref[sl, :] for sl in qsls], axis=0)
            kw = jnp.concatenate([k_ref[sl, :] for sl in ksls], axis=0)
            q2 = jnp.concatenate([jnp.where(head0, q, 0.0), jnp.where(head0, 0.0, q)], axis=0).astype(BF16)
            s = _dot_nt(q2, kw.astype(BF16))
            return qsls, ksls, s + bias_ref[1 if d == 1 else 0, (q0 - ws) >> 6]

        def softmax_pv(qsls, ksls, s):
            m_blk = jnp.max(s, axis=-1, keepdims=True)
            p = jnp.exp2(s - m_blk)
            vw = jnp.concatenate([v_ref[sl, :] for sl in ksls], axis=0)
            v_ones = jnp.concatenate([vw.astype(BF16), jnp.ones((KB, LANES), BF16)], axis=1)
            pv = _dot(p.astype(BF16), v_ones)
            acc_b = jnp.where(head0, pv[:QB, :LANES], pv[QB:, :LANES])
            m_b = jnp.where(head0, m_blk[:QB], m_blk[QB:])
            l_b = jnp.where(head0, pv[:QB, LANES:], pv[QB:, LANES:])
            return qsls, acc_b, m_b, l_b

        def load(ref, sls):
            return jnp.concatenate([ref[sl, :] for sl in sls], axis=0)

        def store(ref, sls, val):
            n = val.shape[0] // len(sls)
            for i, sl in enumerate(sls):
                ref[sl, :] = val[i * n:(i + 1) * n]

        def body(n, carry, first=first, last=last):
            staged = [scores(n * ATT_UNROLL + u) for u in range(ATT_UNROLL)]
            blocks = [softmax_pv(*st) for st in staged]
            for qsls, acc_b, m_b, l_b in blocks:
                if first:
                    acc, m_new, l_new = acc_b, m_b, l_b
                else:
                    m_old = load(m_ref, qsls)
                    m_new = jnp.maximum(m_old, m_b)
                    w_old = jnp.exp2(m_old - m_new)
                    w_blk = jnp.exp2(m_b - m_new)
                    acc = load(o_ref, qsls) * w_old + acc_b * w_blk
                    l_new = load(l_ref, qsls) * w_old + l_b * w_blk
                if last:
                    store(o_ref, qsls, acc / l_new)
                else:
                    store(o_ref, qsls, acc)
                    store(m_ref, qsls, m_new)
                    store(l_ref, qsls, l_new)
            return carry

        lax.fori_loop(0, S // (QB * ATT_UNROLL), body, 0)


def _attention(att_slab, B, S):
    T = B * S
    ncol = ATT_WIDTH // LANES
    return pl.pallas_call(
        functools.partial(_att_kernel, S=S),
        grid=(B, ncol),
        in_specs=[
            pl.BlockSpec((S, LANES), lambda b, h: (b, h)),
            pl.BlockSpec((S, LANES), lambda b, h: (b, ncol + h)),
            pl.BlockSpec((S, LANES), lambda b, h: (b, 2 * ncol + h)),
        ],
        out_specs=pl.BlockSpec((S, LANES), lambda b, h: (b, h)),
        out_shape=jax.ShapeDtypeStruct((T, ATT_WIDTH), F32),
        scratch_shapes=[pltpu.VMEM((S, LANES), F32), pltpu.VMEM((S, LANES), F32),
                        pltpu.VMEM((2, 3, 2 * ATT_QB, ATT_KB), F32)],
        compiler_params=_cparams(("arbitrary", "arbitrary")),
        name="dilated_attention",
    )(att_slab, att_slab, att_slab)


PACK_WORDS = D_MODEL // 2
ROW_TILE = PACK_WORDS // LANES
HIGH_HALF = -65536


def _pack_rows(x):
    bits = lambda v: lax.bitcast_convert_type(v.astype(BF16).astype(F32), jnp.int32)
    low = (bits(x[:, :PACK_WORDS]) >> 16) & 0xFFFF
    return (bits(x[:, PACK_WORDS:]) & HIGH_HALF) | low


def _unpack_rows(w):
    low = lax.bitcast_convert_type(w << 16, F32)
    high = lax.bitcast_convert_type(w & HIGH_HALF, F32)
    return jnp.concatenate([low, high], axis=1).astype(BF16)


def _to_row_tiles(ref, w):
    n = w.shape[0]
    for j in range(ROW_TILE):
        ref[pl.ds(j, n, stride=ROW_TILE), :] = w[:, j * LANES:(j + 1) * LANES]


def _from_row_tiles(ref, n):
    return jnp.concatenate([ref[pl.ds(j, n, stride=ROW_TILE), :] for j in range(ROW_TILE)], axis=1)


def _tile_copy(src_ref, src_row, dst_ref, dst_row, sem):
    src = pl.ds(pl.multiple_of(src_row * ROW_TILE, ROW_TILE), ROW_TILE)
    dst = pl.ds(pl.multiple_of(dst_row * ROW_TILE, ROW_TILE), ROW_TILE)
    return pltpu.make_async_copy(src_ref.at[src], dst_ref.at[dst], sem)


def _outproj_kernel(of_ref, ob_ref, gg_ref, att_ref, x_ref, gnw_ref, wo1_ref, wo2_ref,
                    n2_ref, wr_ref, br_ref, h_ref, u_ref, lg_ref, stage_ref):
    rows = stage_ref.shape[1] // ATT_CLASSES
    for j in range(ATT_WIDTH // LANES):
        for c in range(ATT_CLASSES):
            stage_ref[j, pl.ds(c, rows, stride=ATT_CLASSES), :] = att_ref[c, :, j * LANES:(j + 1) * LANES]
    att = jnp.concatenate([stage_ref[j] for j in range(ATT_WIDTH // LANES)], axis=1)
    o = of_ref[...] + ob_ref[...]
    gate = gg_ref[...]
    gnw = gnw_ref[...]
    parts = []
    for h in range(GLA_HEADS):
        sl = slice(h * GLA_DV, (h + 1) * GLA_DV)
        parts.append(_rms(o[:, sl], gnw))
    y = jnp.concatenate(parts, axis=1) * (gate / (1.0 + jnp.exp(-gate)))
    mix = _dot(y.astype(BF16), wo1_ref[...]) + _dot(att.astype(BF16), wo2_ref[...])
    h = x_ref[...] + mix
    h_ref[...] = h
    u = _rms(h, n2_ref[...])
    _to_row_tiles(u_ref, _pack_rows(u))
    u_hi = u.astype(BF16)
    u_lo = (u - u_hi.astype(F32)).astype(BF16)
    hi_both = _dot_nt(wr_ref[...], u_hi)
    lg_ref[...] = (hi_both[:LANES] + hi_both[LANES:] + _dot_nt(wr_ref[:LANES], u_lo)) + br_ref[...]


def _outproj(o_f, o_b, gla_slab, att_out, x2, gla_norm_w, w_out, norm2_w, wr, br, tm=512):
    T = x2.shape[0]
    nS = att_out.shape[2] * ATT_CLASSES // tm
    row = lambda i: (i, 0)
    const = lambda i: (0, 0)
    wo = w_out.astype(BF16)
    wr_hi = wr.astype(BF16)
    wr_lo = (wr - wr_hi.astype(F32)).astype(BF16)
    wr = jnp.concatenate([wr_hi, wr_lo], axis=0)
    return pl.pallas_call(
        _outproj_kernel,
        grid=(T // tm,),
        in_specs=[
            pl.BlockSpec((tm, GLA_VAL_WIDTH), row),
            pl.BlockSpec((tm, GLA_VAL_WIDTH), row),
            pl.BlockSpec((tm, GLA_VAL_WIDTH), lambda i: (i, 2)),
            pl.BlockSpec((None, ATT_CLASSES, tm // ATT_CLASSES, ATT_WIDTH), lambda i: (i // nS, 0, i % nS, 0)),
            pl.BlockSpec((tm, D_MODEL), row),
            pl.BlockSpec((1, GLA_DV), const),
            pl.BlockSpec((GLA_VAL_WIDTH, D_MODEL), const),
            pl.BlockSpec((ATT_WIDTH, D_MODEL), const),
            pl.BlockSpec((1, D_MODEL), const),
            pl.BlockSpec((2 * LANES, D_MODEL), const),
            pl.BlockSpec((LANES, 1), const),
        ],
        out_specs=[
            pl.BlockSpec((tm, D_MODEL), row),
            pl.BlockSpec((tm * ROW_TILE, LANES), row),
            pl.BlockSpec((LANES, tm), lambda i: (0, i)),
        ],
        out_shape=[
            jax.ShapeDtypeStruct((T, D_MODEL), F32),
            jax.ShapeDtypeStruct((T * ROW_TILE, LANES), jnp.int32),
            jax.ShapeDtypeStruct((LANES, T), F32),
        ],
        scratch_shapes=[pltpu.VMEM((ATT_WIDTH // LANES, tm, LANES), F32)],
        compiler_params=_cparams(("arbitrary",)),
        name="outproj",
    )(o_f, o_b, gla_slab, att_out, x2, gla_norm_w[None, :], wo[:GLA_VAL_WIDTH], wo[GLA_VAL_WIDTH:],
      norm2_w[None, :], wr, br)


INFO_E1, INFO_E2, INFO_R1, INFO_R2, INFO_W1, INFO_W2 = range(6)
ROUTE_ROWS = 40


def _route_kernel(lg_ref, info_ref, cnt_ref, carry_ref):
    @pl.when(pl.program_id(0) == 0)
    def _():
        carry_ref[...] = jnp.zeros_like(carry_ref)

    lg = lg_ref[:ROUTE_ROWS, :]
    tr = lg.shape[1]
    row = lax.broadcasted_iota(jnp.int32, (ROUTE_ROWS, tr), 0)
    big = jnp.int32(1 << 20)
    is_g = (row >= MOE_N_EXPERTS) & (row < MOE_N_EXPERTS + MOE_GROUPS)
    gl = jnp.where(is_g, lg, -jnp.inf)
    gmax = jnp.max(gl, axis=0, keepdims=True)
    gsel = jnp.min(jnp.where(gl == gmax, row - MOE_N_EXPERTS, big), axis=0, keepdims=True)
    g_w = 1.0 / jnp.sum(jnp.where(is_g, jnp.exp(lg - gmax), 0.0), axis=0, keepdims=True)
    in_grp = (row < MOE_N_EXPERTS) & ((row >> 3) == gsel)
    el = jnp.where(in_grp, lg, -jnp.inf)
    v1 = jnp.max(el, axis=0, keepdims=True)
    i1 = jnp.min(jnp.where(el == v1, row, big), axis=0, keepdims=True)
    el2 = jnp.where(row == i1, -jnp.inf, el)
    v2 = jnp.max(el2, axis=0, keepdims=True)
    i2 = jnp.min(jnp.where(el2 == v2, row, big), axis=0, keepdims=True)
    t = jnp.exp(v2 - v1)
    w1 = g_w * (1.0 / (1.0 + t))
    w2 = g_w * (t / (1.0 + t))

    erow = lax.broadcasted_iota(jnp.int32, (MOE_N_EXPERTS, tr), 0)
    hit1 = erow == i1
    hit2 = erow == i2
    member = jnp.where(hit1 | hit2, 1.0, 0.0)
    r = lax.broadcasted_iota(jnp.int32, (tr, tr), 0)
    c = lax.broadcasted_iota(jnp.int32, (tr, tr), 1)
    earlier = jnp.where(r < c, 1.0, 0.0).astype(BF16)
    carry = carry_ref[...]
    prefix = _dot(member.astype(BF16), earlier) + carry[:, 0:1]
    rank1 = jnp.sum(jnp.where(hit1, prefix, 0.0), axis=0, keepdims=True)
    rank2 = jnp.sum(jnp.where(hit2, prefix, 0.0), axis=0, keepdims=True)
    carry = carry + jnp.sum(member, axis=1, keepdims=True)
    carry_ref[...] = carry
    cnt_ref[...] = carry

    zero = jnp.zeros_like(w1)
    info_ref[...] = jnp.concatenate([i1.astype(F32), i2.astype(F32), rank1, rank2, w1, w2, zero, zero], axis=0)


def _route(logits_t, tr=512):
    T = logits_t.shape[1]
    return pl.pallas_call(
        _route_kernel,
        grid=(T // tr,),
        in_specs=[pl.BlockSpec((LANES, tr), lambda i: (0, i))],
        out_specs=[pl.BlockSpec((8, tr), lambda i: (0, i)),
                   pl.BlockSpec((MOE_N_EXPERTS, LANES), lambda i: (0, 0))],
        out_shape=[jax.ShapeDtypeStruct((8, T), F32), jax.ShapeDtypeStruct((MOE_N_EXPERTS, LANES), F32)],
        scratch_shapes=[pltpu.VMEM((MOE_N_EXPERTS, LANES), F32)],
        compiler_params=_cparams(("arbitrary",)),
        name="route",
    )(logits_t)


ROW_UNROLL = 8


def _dispatch_kernel(dest_ref, pend_ref, u_ref, xs_ref, zbuf, sem, zsem, *, td, T, nblk):
    @pl.when(pl.program_id(0) == 0)
    def _():
        zbuf[...] = jnp.zeros_like(zbuf)
        n_used = pend_ref[MOE_N_EXPERTS - 1] >> 8

        def zero_copy(blk):
            start = pl.multiple_of(blk * (MOE_ROWS * ROW_TILE), MOE_ROWS * ROW_TILE)
            return pltpu.make_async_copy(zbuf, xs_ref.at[pl.ds(start, MOE_ROWS * ROW_TILE)], zsem)

        def each_pad_block(fn):
            def per_expert(e, carry):
                prev = jnp.where(e > 0, pend_ref[jnp.maximum(e - 1, 0)], 0)

                @pl.when(pend_ref[e] > prev)
                def _():
                    fn((pend_ref[e] >> 8) - 1)
                return carry

            def per_tail(j, carry):
                @pl.when(n_used + j < nblk)
                def _():
                    fn(n_used + j)
                return carry

            lax.fori_loop(0, MOE_N_EXPERTS, per_expert, 0)
            lax.fori_loop(0, MOE_N_EXPERTS, per_tail, 0)

        each_pad_block(lambda blk: zero_copy(blk).start())
        each_pad_block(lambda blk: zero_copy(blk).wait())

    base = pl.program_id(0) * td

    def issue(g, carry):
        for j in range(ROW_UNROLL):
            r = g * ROW_UNROLL + j
            for k in range(MOE_TOP_K):
                _tile_copy(u_ref, r, xs_ref, dest_ref[k * T + base + r], sem).start(priority=k)
        return carry

    lax.fori_loop(0, td // ROW_UNROLL, issue, 0)
    for k in range(MOE_TOP_K):
        pltpu.make_async_copy(u_ref, xs_ref.at[pl.ds(0, td * ROW_TILE)], sem).wait()


def _dispatch(dest, pend, u2, cap, td=512):
    T = u2.shape[0] // ROW_TILE
    return pl.pallas_call(
        functools.partial(_dispatch_kernel, td=td, T=T, nblk=cap // MOE_ROWS),
        grid_spec=pltpu.PrefetchScalarGridSpec(
            num_scalar_prefetch=2,
            grid=(T // td,),
            in_specs=[pl.BlockSpec((td * ROW_TILE, LANES), lambda i, d, z: (i, 0))],
            out_specs=pl.BlockSpec(memory_space=pl.ANY),
            scratch_shapes=[pltpu.VMEM((MOE_ROWS * ROW_TILE, LANES), jnp.int32),
                            pltpu.SemaphoreType.DMA(()), pltpu.SemaphoreType.DMA(())],
        ),
        out_shape=jax.ShapeDtypeStruct((cap * ROW_TILE, LANES), jnp.int32),
        compiler_params=_cparams(("arbitrary",)),
        name="dispatch",
    )(dest, pend, u2)


def _expert_kernel(pend_ref, xs_hbm, wg_hbm, wu_hbm, wd_hbm, ys_hbm,
                   xbuf, ybuf, zbuf, stage_g, stage_u, stage_d, wgb, wub, wdb, xsem, ysem, wsem, zsem, *, nblk):
    last = MOE_N_EXPERTS - 1
    n_used = pend_ref[last] >> 8
    block_rows = MOE_ROWS * ROW_TILE

    def x_copy(b, slot):
        start = pl.multiple_of(b * block_rows, block_rows)
        return pltpu.make_async_copy(xs_hbm.at[pl.ds(start, block_rows)], xbuf.at[slot], xsem.at[slot])

    def y_copy(b, slot):
        start = pl.multiple_of(b * block_rows, block_rows)
        return pltpu.make_async_copy(ybuf.at[slot], ys_hbm.at[pl.ds(start, block_rows)], ysem.at[slot])

    def zero_copy(b):
        start = pl.multiple_of(b * block_rows, block_rows)
        return pltpu.make_async_copy(zbuf, ys_hbm.at[pl.ds(start, block_rows)], zsem)

    def weight_copies(e):
        return (pltpu.make_async_copy(wg_hbm.at[e], stage_g, wsem.at[0]),
                pltpu.make_async_copy(wu_hbm.at[e], stage_u, wsem.at[1]),
                pltpu.make_async_copy(wd_hbm.at[e], stage_d, wsem.at[2]))

    def owner(start, row):
        return lax.while_loop(lambda e: (e < last) & (pend_ref[e] <= row), lambda e: e + 1, start)

    for c in weight_copies(owner(0, 0)):
        c.start()
    x_copy(0, 0).start()

    zbuf[...] = jnp.zeros_like(zbuf)

    def tail(fn):
        def step(b, carry):
            fn(b)
            return carry
        lax.fori_loop(n_used, nblk, step, 0)

    tail(lambda b: zero_copy(b).start())

    def body(b, cur):
        slot = b & 1
        e = owner(jnp.maximum(cur, 0), b * MOE_ROWS)
        x_copy(b, slot).wait()

        @pl.when(b + 1 < n_used)
        def _():
            x_copy(b + 1, 1 - slot).start()

        @pl.when(e != cur)
        def _():
            for c in weight_copies(e):
                c.wait()
            wgb[...] = stage_g[...].astype(BF16)
            wub[...] = stage_u[...].astype(BF16)
            wdb[...] = stage_d[...].astype(BF16)

            @pl.when(pend_ref[e] < pend_ref[last])
            def _():
                for c in weight_copies(owner(e + 1, pend_ref[e])):
                    c.start(priority=1)

        @pl.when(b >= 2)
        def _():
            y_copy(b - 2, slot).wait()

        xb = _unpack_rows(_from_row_tiles(xbuf.at[slot], MOE_ROWS))
        g = _dot(xb, wgb[...])
        u = _dot(xb, wub[...])
        hid = (g / (1.0 + jnp.exp(-g))) * u
        _to_row_tiles(ybuf.at[slot], _pack_rows(_dot(hid.astype(BF16), wdb[...])))
        y_copy(b, slot).start()
        return e

    lax.fori_loop(0, n_used, body, jnp.int32(-1))

    @pl.when(n_used >= 2)
    def _():
        y_copy(n_used - 2, n_used & 1).wait()
    y_copy(n_used - 1, (n_used - 1) & 1).wait()
    tail(lambda b: zero_copy(b).wait())


def _experts(pend, xs, w_gate, w_up, w_down):
    cap = xs.shape[0] // ROW_TILE
    nblk = cap // MOE_ROWS
    block = (MOE_ROWS * ROW_TILE, LANES)
    anywhere = pl.BlockSpec(memory_space=pl.ANY)
    return pl.pallas_call(
        functools.partial(_expert_kernel, nblk=nblk),
        grid_spec=pltpu.PrefetchScalarGridSpec(
            num_scalar_prefetch=1,
            grid=(1,),
            in_specs=[anywhere, anywhere, anywhere, anywhere],
            out_specs=anywhere,
            scratch_shapes=[pltpu.VMEM((2,) + block, jnp.int32),
                            pltpu.VMEM((2,) + block, jnp.int32),
                            pltpu.VMEM(block, jnp.int32),
                            pltpu.VMEM((D_MODEL, MOE_D_FF), F32),
                            pltpu.VMEM((D_MODEL, MOE_D_FF), F32),
                            pltpu.VMEM((MOE_D_FF, D_MODEL), F32),
                            pltpu.VMEM((D_MODEL, MOE_D_FF), BF16),
                            pltpu.VMEM((D_MODEL, MOE_D_FF), BF16),
                            pltpu.VMEM((MOE_D_FF, D_MODEL), BF16),
                            pltpu.SemaphoreType.DMA((2,)),
                            pltpu.SemaphoreType.DMA((2,)),
                            pltpu.SemaphoreType.DMA((3,)),
                            pltpu.SemaphoreType.DMA(())],
        ),
        out_shape=jax.ShapeDtypeStruct((cap * ROW_TILE, LANES), jnp.int32),
        compiler_params=_cparams(("arbitrary",)),
        name="experts",
    )(pend, xs, w_gate, w_up, w_down)


def _combine_kernel(dest_ref, ys_ref, info_ref, h_ref, fw_ref, o_ref, buf, sem, *, tc, T):
    i = pl.program_id(0)
    n = pl.num_programs(0)

    def issue(step, slot):
        base = step * tc

        def body(g, carry):
            for j in range(ROW_UNROLL):
                r = g * ROW_UNROLL + j
                for k in range(MOE_TOP_K):
                    _tile_copy(ys_ref, dest_ref[k * T + base + r], buf.at[slot, k], r,
                               sem.at[slot]).start(priority=k)
            return carry

        lax.fori_loop(0, tc // ROW_UNROLL, body, 0)

    @pl.when(i == 0)
    def _():
        issue(0, 0)

    slot = i % 2

    @pl.when(i + 1 < n)
    def _():
        issue(i + 1, 1 - slot)

    for k in range(MOE_TOP_K):
        pltpu.make_async_copy(ys_ref.at[pl.ds(0, tc * ROW_TILE)], buf.at[slot, k], sem.at[slot]).wait()

    info_t = jnp.concatenate([info_ref[...]] * (LANES // 8), axis=0).T
    w1 = info_t[:, INFO_W1:INFO_W1 + 1]
    w2 = info_t[:, INFO_W2:INFO_W2 + 1]
    y1 = _unpack_rows(_from_row_tiles(buf.at[slot, 0], tc)).astype(F32)
    y2 = _unpack_rows(_from_row_tiles(buf.at[slot, 1], tc)).astype(F32)
    h = h_ref[...] + (y1 * w1 + y2 * w2)
    o_ref[...] = _rms(h, fw_ref[...])


def _combine(dest, ys, info, h, final_w, tc=512):
    T = h.shape[0]
    return pl.pallas_call(
        functools.partial(_combine_kernel, tc=tc, T=T),
        grid_spec=pltpu.PrefetchScalarGridSpec(
            num_scalar_prefetch=1,
            grid=(T // tc,),
            in_specs=[pl.BlockSpec(memory_space=pl.ANY),
                      pl.BlockSpec((8, tc), lambda i, d: (0, i)),
                      pl.BlockSpec((tc, D_MODEL), lambda i, d: (i, 0)),
                      pl.BlockSpec((1, D_MODEL), lambda i, d: (0, 0))],
            out_specs=pl.BlockSpec((tc, D_MODEL), lambda i, d: (i, 0)),
            scratch_shapes=[pltpu.VMEM((2, MOE_TOP_K, tc * ROW_TILE, LANES), jnp.int32),
                            pltpu.SemaphoreType.DMA((2,))],
        ),
        out_shape=jax.ShapeDtypeStruct((T, D_MODEL), F32),
        compiler_params=_cparams(("arbitrary",)),
        name="combine",
    )(dest, ys, info, h, final_w[None, :])


def _plan_kernel(info_ref, cnt_ref, dest_ref, pend_ref):
    cnt = cnt_ref[...].astype(jnp.int32)
    nblk_e = ((cnt + (MOE_ROWS - 1)) >> 8).astype(F32)
    r = lax.broadcasted_iota(jnp.int32, (MOE_N_EXPERTS, MOE_N_EXPERTS), 0)
    c = lax.broadcasted_iota(jnp.int32, (MOE_N_EXPERTS, MOE_N_EXPERTS), 1)
    before = jnp.where(c < r, 1.0, 0.0).astype(BF16)
    first_blk = _dot(before, nblk_e.astype(BF16))
    pstart = first_blk[:, 0:1] * float(MOE_ROWS)
    pend_ref[...] = ((first_blk + nblk_e) * float(MOE_ROWS)).astype(jnp.int32)

    info = info_ref[...]
    erow = lax.broadcasted_iota(jnp.int32, (MOE_N_EXPERTS, info.shape[1]), 0)
    start_of = lambda e: jnp.sum(jnp.where(erow == e.astype(jnp.int32), pstart, 0.0), axis=0, keepdims=True)
    d1 = info[INFO_R1:INFO_R1 + 1] + start_of(info[INFO_E1:INFO_E1 + 1])
    d2 = info[INFO_R2:INFO_R2 + 1] + start_of(info[INFO_E2:INFO_E2 + 1])
    zero = jnp.zeros_like(d1)
    dest_ref[...] = jnp.concatenate([d1, d2] + [zero] * 6, axis=0).astype(jnp.int32)


def _plan(info, counts, tr=2048):
    T = info.shape[1]
    dest8, pend = pl.pallas_call(
        _plan_kernel,
        grid=(T // tr,),
        in_specs=[pl.BlockSpec((8, tr), lambda i: (0, i)),
                  pl.BlockSpec((MOE_N_EXPERTS, LANES), lambda i: (0, 0))],
        out_specs=[pl.BlockSpec((8, tr), lambda i: (0, i)),
                   pl.BlockSpec((MOE_N_EXPERTS, LANES), lambda i: (0, 0))],
        out_shape=[jax.ShapeDtypeStruct((8, T), jnp.int32),
                   jax.ShapeDtypeStruct((MOE_N_EXPERTS, LANES), jnp.int32)],
        compiler_params=_cparams(("arbitrary",)),
        name="plan",
    )(info, counts)
    return dest8[:MOE_TOP_K].reshape(-1), pend[:, 0]


def _moe_capacity(T):
    return (-(-(T * MOE_TOP_K) // MOE_ROWS) + MOE_N_EXPERTS) * MOE_ROWS


def _router_weights(router_group_w, router_group_b, router_expert_w, router_expert_b):
    we = jnp.transpose(router_expert_w, (0, 2, 1)).reshape(MOE_N_EXPERTS, D_MODEL)
    pad = LANES - MOE_N_EXPERTS - MOE_GROUPS
    wr = jnp.concatenate([we, router_group_w.T, jnp.zeros((pad, D_MODEL), F32)], axis=0)
    br = jnp.concatenate([router_expert_b.reshape(-1), router_group_b, jnp.zeros((pad,), F32)])[:, None]
    return wr, br


def kernel(x, norm1_w, w_in, gla_fwd_gate_w, gla_fwd_gate_b, gla_bwd_gate_w, gla_bwd_gate_b,
           gla_norm_w, w_out, norm2_w, router_group_w, router_group_b, router_expert_w,
           router_expert_b, expert_w_gate, expert_w_up, expert_w_down, final_norm_w):
    B, S, D = x.shape
    T = B * S
    assert norm1_w.shape[0] == 1, "single-layer trunk: the final norm is fused into the combine step"
    h = x.reshape(T, D)
    gla_slab, loga, att_slab = _inproj(h, S, norm1_w[0], w_in[0], gla_fwd_gate_w[0], gla_fwd_gate_b[0],
                                       gla_bwd_gate_w[0], gla_bwd_gate_b[0])
    o_f, o_b = _gla(gla_slab, loga, B, S)
    att_out = _attention(att_slab.reshape(T, 3 * ATT_WIDTH), B, S)
    att_out = att_out.reshape(B, ATT_CLASSES, S // ATT_CLASSES, ATT_WIDTH)
    wr, br = _router_weights(router_group_w[0], router_group_b[0], router_expert_w[0], router_expert_b[0])
    h, u2, logits = _outproj(o_f, o_b, gla_slab, att_out, h, gla_norm_w[0], w_out[0], norm2_w[0], wr, br)
    info, counts = _route(logits)
    dest, pend = _plan(info, counts)
    xs = _dispatch(dest, pend, u2, _moe_capacity(T))
    ys = _experts(pend, xs, expert_w_gate[0], expert_w_up[0], expert_w_down[0])
    out = _combine(dest, ys, info, h, final_norm_w)
    return out.reshape(B, S, D)
```

```python
import functools

import jax
import jax.numpy as jnp
from jax import lax
from jax.experimental import pallas as pl
from jax.experimental.pallas import tpu as pltpu

F32 = jnp.float32
BF16 = jnp.bfloat16

D_MODEL = 1024
GLA_HEADS = 4
GLA_DV = 128
GLA_DK = 64
GLA_KEY_WIDTH = GLA_HEADS * GLA_DK
GLA_VAL_WIDTH = GLA_HEADS * GLA_DV
GLA_GATE_RANK = 16
GLA_TAU = 16.0
GLA_CHUNK = 64
ATT_WIDTH = 512
ATT_HEAD_DIM = 64
ATT_HEADS = 8
ROT_DIM = 16
ROPE_THETA = 500000.0
DILATED_PATTERNS = ((128, 1), (512, 4), (2048, 16))
ATT_RADIUS = 64
MOE_GROUPS = 4
MOE_EXPERTS_PER_GROUP = 8
MOE_N_EXPERTS = 32
MOE_TOP_K = 2
MOE_D_FF = 512
EPS = 1e-6
NEG_INF = -1e30
LOG2E = 1.4426950408889634

LANES = 128
MOE_ROWS = 256
VMEM_LIMIT = 56 * 1024 * 1024


def _cparams(sem):
    return pltpu.CompilerParams(dimension_semantics=sem, vmem_limit_bytes=VMEM_LIMIT)


def _dot(a, b):
    return jnp.dot(a, b, preferred_element_type=F32)


def _dot_nt(a, b):
    return lax.dot_general(a, b, (((1,), (1,)), ((), ())), preferred_element_type=F32)


def _dot_tn(a, b):
    return lax.dot_general(a, b, (((0,), (0,)), ((), ())), preferred_element_type=F32)


def _rms(x, w):
    return x * lax.rsqrt(jnp.mean(x * x, axis=-1, keepdims=True) + EPS) * w


def _inproj_kernel(x_ref, n1_ref, wg_ref, wlr_ref, wa_ref, gw_ref, gb_ref,
                   rc_ref, rs1_ref, rs2_ref, gla_ref, loga_ref, att_ref, stage_ref):
    x = x_ref[...]
    ub = _rms(x, n1_ref[...]).astype(BF16)
    g = _dot(ub, wg_ref[...])
    gla_ref[:, :GLA_KEY_WIDTH] = g[:, :GLA_KEY_WIDTH] * (GLA_DK ** -0.5)
    gla_ref[:, GLA_KEY_WIDTH:] = g[:, GLA_KEY_WIDTH:]
    lr = _dot(ub, wlr_ref[...])
    gate = _dot(lr.astype(BF16), gw_ref[...]) + gb_ref[...]
    loga_ref[...] = (jnp.minimum(gate, 0.0) - jnp.log(1.0 + jnp.exp(-jnp.abs(gate)))) * (1.0 / GLA_TAU)
    a = _dot(ub, wa_ref[...])
    qk = a[:, :2 * ATT_WIDTH]
    reps = 2 * ATT_WIDTH // LANES
    c = jnp.concatenate([rc_ref[...]] * reps, axis=1)
    s1 = jnp.concatenate([rs1_ref[...]] * reps, axis=1)
    s2 = jnp.concatenate([rs2_ref[...]] * reps, axis=1)
    half = ROT_DIM // 2
    n = 2 * ATT_WIDTH
    roped = qk * c + pltpu.roll(qk, n - half, 1) * s1 + pltpu.roll(qk, half, 1) * s2
    qkv = jnp.concatenate([roped[:, :ATT_WIDTH] * (ATT_HEAD_DIM ** -0.5 * LOG2E), roped[:, ATT_WIDTH:],
                           a[:, 2 * ATT_WIDTH:]], axis=1)
    rows = x.shape[0] // ATT_CLASSES
    for j in range(3 * ATT_WIDTH // LANES):
        cols = slice(j * LANES, (j + 1) * LANES)
        stage_ref[j] = qkv[:, cols]
        for c in range(ATT_CLASSES):
            att_ref[c, :, cols] = stage_ref[j, pl.ds(c, rows, stride=ATT_CLASSES), :]


def _rope_lane_tables(S):
    half = ROT_DIM // 2
    inv = ROPE_THETA ** (-(jnp.arange(0, ROT_DIM, 2, dtype=F32) / ROT_DIM))
    ang = inv[:, None] * jnp.arange(S, dtype=F32)[None, :]
    cos, sin = jnp.cos(ang), jnp.sin(ang)
    lane = jnp.arange(LANES) % ATT_HEAD_DIM
    freq = jnp.arange(half)[:, None]
    first = ((lane[None, :] == freq)).astype(F32)
    second = ((lane[None, :] == freq + half)).astype(F32)
    expand = lambda t, sel: lax.dot_general(t, sel, (((0,), (0,)), ((), ())), precision=lax.Precision.HIGHEST)
    rest = (lane >= ROT_DIM).astype(F32)[None, :]
    return expand(cos, first + second) + rest, expand(-sin, first), expand(sin, second)


def _inproj(x2, S, norm1_w, w_in, wf, bfw, wb, bbw, tm=512):
    T = x2.shape[0]
    o_lr = 2 * GLA_KEY_WIDTH + 2 * GLA_VAL_WIDTH
    o_att = o_lr + 2 * GLA_GATE_RANK
    w_main = w_in[:, :o_lr + LANES].astype(BF16)
    wa = w_in[:, o_att:].astype(BF16)
    zeros = jnp.zeros((GLA_GATE_RANK, GLA_KEY_WIDTH), F32)
    gw = jnp.concatenate([jnp.concatenate([wf, zeros], axis=1), jnp.concatenate([zeros, wb], axis=1),
                          jnp.zeros((LANES - 2 * GLA_GATE_RANK, 2 * GLA_KEY_WIDTH), F32)], axis=0).astype(BF16)
    gb = jnp.concatenate([bfw, bbw])[None, :]
    rc, rs1, rs2 = _rope_lane_tables(S)
    nS = S // tm
    row = lambda i: (i, 0)
    const = lambda i: (0, 0)
    pos = lambda i: (i % nS, 0)
    return pl.pallas_call(
        _inproj_kernel,
        grid=(T // tm,),
        in_specs=[
            pl.BlockSpec((tm, D_MODEL), row),
            pl.BlockSpec((1, D_MODEL), const),
            pl.BlockSpec((D_MODEL, o_lr), const),
            pl.BlockSpec((D_MODEL, LANES), lambda i: (0, o_lr // LANES)),
            pl.BlockSpec((D_MODEL, 3 * ATT_WIDTH), const),
            pl.BlockSpec((LANES, 2 * GLA_KEY_WIDTH), const),
            pl.BlockSpec((1, 2 * GLA_KEY_WIDTH), const),
            pl.BlockSpec((tm, LANES), pos),
            pl.BlockSpec((tm, LANES), pos),
            pl.BlockSpec((tm, LANES), pos),
        ],
        out_specs=[
            pl.BlockSpec((tm, o_lr), row),
            pl.BlockSpec((tm, 2 * GLA_KEY_WIDTH), row),
            pl.BlockSpec((None, ATT_CLASSES, tm // ATT_CLASSES, 3 * ATT_WIDTH),
                         lambda i: (i // nS, 0, i % nS, 0)),
        ],
        out_shape=[
            jax.ShapeDtypeStruct((T, o_lr), F32),
            jax.ShapeDtypeStruct((T, 2 * GLA_KEY_WIDTH), F32),
            jax.ShapeDtypeStruct((T // S, ATT_CLASSES, S // ATT_CLASSES, 3 * ATT_WIDTH), F32),
        ],
        scratch_shapes=[pltpu.VMEM((3 * ATT_WIDTH // LANES, tm, LANES), F32)],
        compiler_params=_cparams(("arbitrary",)),
        name="inproj",
    )(x2, norm1_w[None, :], w_main, w_main, wa, gw, gb, rc, rs1, rs2)


def _gla_decays(q, k, v, la, forward, G):
    C = GLA_CHUNK
    R = G * C
    r = lax.broadcasted_iota(jnp.int32, (R, R), 0)
    c = lax.broadcasted_iota(jnp.int32, (R, R), 1)
    same = (r >> 6) == (c >> 6)
    tri = (c <= r) if forward else (c >= r)
    t_mat = jnp.where(same, jnp.where(tri, 1.0, 0.0), 0.0).astype(BF16)
    hi = la.astype(BF16)
    lo = (la - hi.astype(F32)).astype(BF16)
    b = _dot(t_mat, hi) + _dot(t_mat, lo)
    edge = C - 1 if forward else 0
    tot = jnp.concatenate([jnp.broadcast_to(b[g * C + edge:g * C + edge + 1], (C, GLA_KEY_WIDTH))
                           for g in range(G)], axis=0)
    order = list(range(G)) if forward else list(range(G - 1, -1, -1))
    return dict(q_dec=q * jnp.exp(b), k_inv=(k * jnp.exp(-b)).astype(BF16), k_end=k * jnp.exp(tot - b),
                tot=tot, vb=v.astype(BF16), order=order, forward=forward, G=G)


def _gla_scores(prep):
    C, H = GLA_CHUNK, GLA_HEADS
    lane_k = lax.broadcasted_iota(jnp.int32, (C, GLA_KEY_WIDTH), 1)
    qd_heads, scores = {}, {}
    for g in prep["order"]:
        rows = slice(g * C, (g + 1) * C)
        qd = prep["q_dec"][rows]
        qd_heads[g] = jnp.concatenate([jnp.where((lane_k >> 6) == h, qd, 0.0) for h in range(H)],
                                      axis=0).astype(BF16)
        scores[g] = _dot_nt(qd_heads[g], prep["k_inv"][rows])
    return qd_heads, scores


def _gla_chunk_updates(prep):
    C, H, G = GLA_CHUNK, GLA_HEADS, prep["G"]
    k_end, tot, vb = prep["k_end"], prep["tot"], prep["vb"]
    kv, dec_t = {}, {}
    lane = lax.broadcasted_iota(jnp.int32, (GLA_KEY_WIDTH, 2 * C), 1)
    zeros = jnp.zeros((C, GLA_DV), BF16)
    for p in range(G // 2):
        pair = slice(2 * p * C, (2 * p + 2) * C)
        ke_t = k_end[pair].T.astype(BF16)
        tot_t = tot[pair].T
        swapped = pltpu.roll(tot_t, C, 1)
        for half in range(2):
            g = 2 * p + half
            rows = slice(g * C, (g + 1) * C)
            own = (lane < C) if half == 0 else (lane >= C)
            dec_t[g] = jnp.exp(jnp.where(own, tot_t, swapped))
            parts = []
            for h in range(H):
                v_h = vb[rows, h * GLA_DV:(h + 1) * GLA_DV]
                v_pad = jnp.concatenate([v_h, zeros] if half == 0 else [zeros, v_h], axis=0)
                parts.append(_dot(ke_t[h * C:(h + 1) * C], v_pad))
            kv[g] = jnp.concatenate(parts, axis=0)
    return kv, dec_t


def _gla_states(prep, kv, dec_t, s_ref):
    st = s_ref[...]
    states = {}
    for g in prep["order"]:
        states[g] = st.astype(BF16)
        st = st * dec_t[g] + kv[g]
    s_ref[...] = st
    return states


def _gla_outputs(prep, qd_heads, scores, inter, o_ref):
    C, H = GLA_CHUNK, GLA_HEADS
    row_q = lax.broadcasted_iota(jnp.int32, (H * C, C), 0) & (C - 1)
    col_k = lax.broadcasted_iota(jnp.int32, (H * C, C), 1)
    a_mask = (col_k <= row_q) if prep["forward"] else (col_k >= row_q)
    for g in prep["order"]:
        rows = slice(g * C, (g + 1) * C)
        a = jnp.where(a_mask, scores[g], 0.0).astype(BF16)
        vv = prep["vb"][rows]
        o_ref[rows, :] = jnp.concatenate(
            [_dot(a[h * C:(h + 1) * C], vv[:, h * GLA_DV:(h + 1) * GLA_DV]) + inter[g][h * C:(h + 1) * C]
             for h in range(H)], axis=1)


def _gla_kernel(qf_ref, kf_ref, vf_ref, laf_ref, qb_ref, kb_ref, vb_ref, lab_ref,
                of_ref, ob_ref, sf_ref, sb_ref, *, G):
    @pl.when(pl.program_id(1) == 0)
    def _():
        sf_ref[...] = jnp.zeros_like(sf_ref)
        sb_ref[...] = jnp.zeros_like(sb_ref)

    dirs = [(_gla_decays(qf_ref[...], kf_ref[...], vf_ref[...], laf_ref[...], True, G), sf_ref, of_ref),
            (_gla_decays(qb_ref[...], kb_ref[...], vb_ref[...], lab_ref[...], False, G), sb_ref, ob_ref)]
    scored = [_gla_scores(prep) for prep, _, _ in dirs]
    updates = [_gla_chunk_updates(prep) for prep, _, _ in dirs]
    states = [_gla_states(prep, kv, dec_t, s_ref) for (prep, s_ref, _), (kv, dec_t) in zip(dirs, updates)]
    inters = [{g: _dot(qd_heads[g], st[g]) for g in prep["order"]}
              for (prep, _, _), (qd_heads, _), st in zip(dirs, scored, states)]
    for (prep, _, o_ref), (qd_heads, scores), inter in zip(dirs, scored, inters):
        _gla_outputs(prep, qd_heads, scores, inter, o_ref)


def _gla(gla_slab, loga, B, S, G=8):
    T = B * S
    R = G * GLA_CHUNK
    ns = S // R
    fwd = lambda col: (lambda b, i: (b * ns + i, col))
    bwd = lambda col: (lambda b, i: (b * ns + ns - 1 - i, col))
    kw, vw = GLA_KEY_WIDTH, GLA_VAL_WIDTH
    return pl.pallas_call(
        functools.partial(_gla_kernel, G=G),
        grid=(B, ns),
        in_specs=[
            pl.BlockSpec((R, kw), fwd(0)), pl.BlockSpec((R, kw), fwd(1)),
            pl.BlockSpec((R, vw), fwd(1)), pl.BlockSpec((R, kw), fwd(0)),
            pl.BlockSpec((R, kw), bwd(0)), pl.BlockSpec((R, kw), bwd(1)),
            pl.BlockSpec((R, vw), bwd(1)), pl.BlockSpec((R, kw), bwd(1)),
        ],
        out_specs=[pl.BlockSpec((R, vw), fwd(0)), pl.BlockSpec((R, vw), bwd(0))],
        out_shape=[jax.ShapeDtypeStruct((T, vw), F32), jax.ShapeDtypeStruct((T, vw), F32)],
        scratch_shapes=[pltpu.VMEM((kw, GLA_DV), F32), pltpu.VMEM((kw, GLA_DV), F32)],
        compiler_params=_cparams(("arbitrary", "arbitrary")),
        name="gla",
    )(gla_slab, gla_slab, gla_slab, loga, gla_slab, gla_slab, gla_slab, loga)


ATT_CLASSES = 4
ATT_QB = 128
ATT_KB = ATT_QB + 2 * ATT_RADIUS


ATT_UNROLL = 4


def _att_kernel(q_ref, k_ref, v_ref, o_ref, m_ref, l_ref, bias_ref, *, S):
    QB, KB, NC = ATT_QB, ATT_KB, ATT_CLASSES
    L4 = S // NC
    lane = lax.broadcasted_iota(jnp.int32, (QB, LANES), 1)
    head0 = lane < ATT_HEAD_DIM

    @pl.when((pl.program_id(0) == 0) & (pl.program_id(1) == 0))
    def _():
        rowi = lax.broadcasted_iota(jnp.int32, (2 * QB, KB), 0) & (QB - 1)
        coli = lax.broadcasted_iota(jnp.int32, (2 * QB, KB), 1)
        qpos = (rowi & (QB // NC - 1)) * NC + (rowi >> 5)
        kpos = (coli & (KB // NC - 1)) * NC + (coli >> 6)
        for case in range(3):
            bias_ref[0, case] = jnp.where(jnp.abs(rowi - coli + case * ATT_RADIUS) <= ATT_RADIUS, 0.0, NEG_INF)
            bias_ref[1, case] = jnp.where(jnp.abs(qpos - kpos + case * ATT_RADIUS) <= ATT_RADIUS, 0.0, NEG_INF)

    for pi, (_, d) in enumerate(DILATED_PATTERNS):
        L = S // d
        nb = L // QB
        shift = nb.bit_length() - 1
        first = pi == 0
        last = pi == len(DILATED_PATTERNS) - 1

        def scores(n, d=d, L=L, nb=nb, shift=shift):
            cls = n >> shift
            q0 = (n & (nb - 1)) * QB
            ws = jnp.clip(q0 - ATT_RADIUS, 0, L - KB)
            if d == 1:
                qsls = [pl.ds(pl.multiple_of(c * L4 + q0 // NC, QB // NC), QB // NC) for c in range(NC)]
                ksls = [pl.ds(pl.multiple_of(c * L4 + ws // NC, ATT_RADIUS // NC), KB // NC) for c in range(NC)]
            elif d == NC:
                qsls = [pl.ds(pl.multiple_of(cls * L4 + q0, QB), QB)]
                ksls = [pl.ds(pl.multiple_of(cls * L4 + ws, ATT_RADIUS), KB)]
            else:
                base = (cls & (NC - 1)) * L4 + (cls >> 2)
                qsls = [pl.ds(base + NC * q0, QB, stride=NC)]
                ksls = [pl.ds(base + NC * ws, KB, stride=NC)]
            q = jnp.concatenate([q_ref[sl, :] for sl in qsls], axis=0)
            kw = jnp.concatenate([k_ref[sl, :] for sl in ksls], axis=0)
            q2 = jnp.concatenate([jnp.where(head0, q, 0.0), jnp.where(head0, 0.0, q)], axis=0).astype(BF16)
            s = _dot_nt(q2, kw.astype(BF16))
            return qsls, ksls, s + bias_ref[1 if d == 1 else 0, (q0 - ws) >> 6]

        def softmax_pv(qsls, ksls, s):
            m_blk = jnp.max(s, axis=-1, keepdims=True)
            p = jnp.exp2(s - m_blk)
            vw = jnp.concatenate([v_ref[sl, :] for sl in ksls], axis=0)
            v_ones = jnp.concatenate([vw.astype(BF16), jnp.ones((KB, LANES), BF16)], axis=1)
            pv = _dot(p.astype(BF16), v_ones)
            acc_b = jnp.where(head0, pv[:QB, :LANES], pv[QB:, :LANES])
            m_b = jnp.where(head0, m_blk[:QB], m_blk[QB:])
            l_b = jnp.where(head0, pv[:QB, LANES:], pv[QB:, LANES:])
            return qsls, acc_b, m_b, l_b

        def load(ref, sls):
            return jnp.concatenate([ref[sl, :] for sl in sls], axis=0)

        def store(ref, sls, val):
            n = val.shape[0] // len(sls)
            for i, sl in enumerate(sls):
                ref[sl, :] = val[i * n:(i + 1) * n]

        def body(n, carry, first=first, last=last):
            staged = [scores(n * ATT_UNROLL + u) for u in range(ATT_UNROLL)]
            blocks = [softmax_pv(*st) for st in staged]
            for qsls, acc_b, m_b, l_b in blocks:
                if first:
                    acc, m_new, l_new = acc_b, m_b, l_b
                else:
                    m_old = load(m_ref, qsls)
                    m_new = jnp.maximum(m_old, m_b)
                    w_old = jnp.exp2(m_old - m_new)
                    w_blk = jnp.exp2(m_b - m_new)
                    acc = load(o_ref, qsls) * w_old + acc_b * w_blk
                    l_new = load(l_ref, qsls) * w_old + l_b * w_blk
                if last:
                    store(o_ref, qsls, acc / l_new)
                else:
                    store(o_ref, qsls, acc)
                    store(m_ref, qsls, m_new)
                    store(l_ref, qsls, l_new)
            return carry

        lax.fori_loop(0, S // (QB * ATT_UNROLL), body, 0)


def _attention(att_slab, B, S):
    T = B * S
    ncol = ATT_WIDTH // LANES
    return pl.pallas_call(
        functools.partial(_att_kernel, S=S),
        grid=(B, ncol),
        in_specs=[
            pl.BlockSpec((S, LANES), lambda b, h: (b, h)),
            pl.BlockSpec((S, LANES), lambda b, h: (b, ncol + h)),
            pl.BlockSpec((S, LANES), lambda b, h: (b, 2 * ncol + h)),
        ],
        out_specs=pl.BlockSpec((S, LANES), lambda b, h: (b, h)),
        out_shape=jax.ShapeDtypeStruct((T, ATT_WIDTH), F32),
        scratch_shapes=[pltpu.VMEM((S, LANES), F32), pltpu.VMEM((S, LANES), F32),
                        pltpu.VMEM((2, 3, 2 * ATT_QB, ATT_KB), F32)],
        compiler_params=_cparams(("arbitrary", "arbitrary")),
        name="dilated_attention",
    )(att_slab, att_slab, att_slab)


PACK_WORDS = D_MODEL // 2
ROW_TILE = PACK_WORDS // LANES
HIGH_HALF = -65536


def _pack_rows(x):
    bits = lambda v: lax.bitcast_convert_type(v.astype(BF16).astype(F32), jnp.int32)
    low = (bits(x[:, :PACK_WORDS]) >> 16) & 0xFFFF
    return (bits(x[:, PACK_WORDS:]) & HIGH_HALF) | low


def _unpack_rows(w):
    low = lax.bitcast_convert_type(w << 16, F32)
    high = lax.bitcast_convert_type(w & HIGH_HALF, F32)
    return jnp.concatenate([low, high], axis=1).astype(BF16)


def _to_row_tiles(ref, w):
    n = w.shape[0]
    for j in range(ROW_TILE):
        ref[pl.ds(j, n, stride=ROW_TILE), :] = w[:, j * LANES:(j + 1) * LANES]


def _from_row_tiles(ref, n):
    return jnp.concatenate([ref[pl.ds(j, n, stride=ROW_TILE), :] for j in range(ROW_TILE)], axis=1)


def _tile_copy(src_ref, src_row, dst_ref, dst_row, sem):
    src = pl.ds(pl.multiple_of(src_row * ROW_TILE, ROW_TILE), ROW_TILE)
    dst = pl.ds(pl.multiple_of(dst_row * ROW_TILE, ROW_TILE), ROW_TILE)
    return pltpu.make_async_copy(src_ref.at[src], dst_ref.at[dst], sem)


def _outproj_kernel(of_ref, ob_ref, gg_ref, att_ref, x_ref, gnw_ref, wo1_ref, wo2_ref,
                    n2_ref, wr_ref, br_ref, h_ref, u_ref, lg_ref, stage_ref):
    rows = stage_ref.shape[1] // ATT_CLASSES
    for j in range(ATT_WIDTH // LANES):
        for c in range(ATT_CLASSES):
            stage_ref[j, pl.ds(c, rows, stride=ATT_CLASSES), :] = att_ref[c, :, j * LANES:(j + 1) * LANES]
    att = jnp.concatenate([stage_ref[j] for j in range(ATT_WIDTH // LANES)], axis=1)
    o = of_ref[...] + ob_ref[...]
    gate = gg_ref[...]
    gnw = gnw_ref[...]
    parts = []
    for h in range(GLA_HEADS):
        sl = slice(h * GLA_DV, (h + 1) * GLA_DV)
        parts.append(_rms(o[:, sl], gnw))
    y = jnp.concatenate(parts, axis=1) * (gate / (1.0 + jnp.exp(-gate)))
    mix = _dot(y.astype(BF16), wo1_ref[...]) + _dot(att.astype(BF16), wo2_ref[...])
    h = x_ref[...] + mix
    h_ref[...] = h
    u = _rms(h, n2_ref[...])
    _to_row_tiles(u_ref, _pack_rows(u))
    u_hi = u.astype(BF16)
    u_lo = (u - u_hi.astype(F32)).astype(BF16)
    hi_both = _dot_nt(wr_ref[...], u_hi)
    lg_ref[...] = (hi_both[:LANES] + hi_both[LANES:] + _dot_nt(wr_ref[:LANES], u_lo)) + br_ref[...]


def _outproj(o_f, o_b, gla_slab, att_out, x2, gla_norm_w, w_out, norm2_w, wr, br, tm=1024):
    T = x2.shape[0]
    nS = att_out.shape[2] * ATT_CLASSES // tm
    row = lambda i: (i, 0)
    const = lambda i: (0, 0)
    wo = w_out.astype(BF16)
    wr_hi = wr.astype(BF16)
    wr_lo = (wr - wr_hi.astype(F32)).astype(BF16)
    wr = jnp.concatenate([wr_hi, wr_lo], axis=0)
    return pl.pallas_call(
        _outproj_kernel,
        grid=(T // tm,),
        in_specs=[
            pl.BlockSpec((tm, GLA_VAL_WIDTH), row),
            pl.BlockSpec((tm, GLA_VAL_WIDTH), row),
            pl.BlockSpec((tm, GLA_VAL_WIDTH), lambda i: (i, 2)),
            pl.BlockSpec((None, ATT_CLASSES, tm // ATT_CLASSES, ATT_WIDTH), lambda i: (i // nS, 0, i % nS, 0)),
            pl.BlockSpec((tm, D_MODEL), row),
            pl.BlockSpec((1, GLA_DV), const),
            pl.BlockSpec((GLA_VAL_WIDTH, D_MODEL), const),
            pl.BlockSpec((ATT_WIDTH, D_MODEL), const),
            pl.BlockSpec((1, D_MODEL), const),
            pl.BlockSpec((2 * LANES, D_MODEL), const),
            pl.BlockSpec((LANES, 1), const),
        ],
        out_specs=[
            pl.BlockSpec((tm, D_MODEL), row),
            pl.BlockSpec((tm * ROW_TILE, LANES), row),
            pl.BlockSpec((LANES, tm), lambda i: (0, i)),
        ],
        out_shape=[
            jax.ShapeDtypeStruct((T, D_MODEL), F32),
            jax.ShapeDtypeStruct((T * ROW_TILE, LANES), jnp.int32),
            jax.ShapeDtypeStruct((LANES, T), F32),
        ],
        scratch_shapes=[pltpu.VMEM((ATT_WIDTH // LANES, tm, LANES), F32)],
        compiler_params=_cparams(("arbitrary",)),
        name="outproj",
    )(o_f, o_b, gla_slab, att_out, x2, gla_norm_w[None, :], wo[:GLA_VAL_WIDTH], wo[GLA_VAL_WIDTH:],
      norm2_w[None, :], wr, br)


INFO_E1, INFO_E2, INFO_R1, INFO_R2, INFO_W1, INFO_W2 = range(6)
ROUTE_ROWS = 40


def _route_kernel(lg_ref, info_ref, cnt_ref, carry_ref):
    @pl.when(pl.program_id(0) == 0)
    def _():
        carry_ref[...] = jnp.zeros_like(carry_ref)

    lg = lg_ref[:ROUTE_ROWS, :]
    tr = lg.shape[1]
    row = lax.broadcasted_iota(jnp.int32, (ROUTE_ROWS, tr), 0)
    big = jnp.int32(1 << 20)
    is_g = (row >= MOE_N_EXPERTS) & (row < MOE_N_EXPERTS + MOE_GROUPS)
    gl = jnp.where(is_g, lg, -jnp.inf)
    gmax = jnp.max(gl, axis=0, keepdims=True)
    gsel = jnp.min(jnp.where(gl == gmax, row - MOE_N_EXPERTS, big), axis=0, keepdims=True)
    g_w = 1.0 / jnp.sum(jnp.where(is_g, jnp.exp(lg - gmax), 0.0), axis=0, keepdims=True)
    in_grp = (row < MOE_N_EXPERTS) & ((row >> 3) == gsel)
    el = jnp.where(in_grp, lg, -jnp.inf)
    v1 = jnp.max(el, axis=0, keepdims=True)
    i1 = jnp.min(jnp.where(el == v1, row, big), axis=0, keepdims=True)
    el2 = jnp.where(row == i1, -jnp.inf, el)
    v2 = jnp.max(el2, axis=0, keepdims=True)
    i2 = jnp.min(jnp.where(el2 == v2, row, big), axis=0, keepdims=True)
    t = jnp.exp(v2 - v1)
    w1 = g_w * (1.0 / (1.0 + t))
    w2 = g_w * (t / (1.0 + t))

    erow = lax.broadcasted_iota(jnp.int32, (MOE_N_EXPERTS, tr), 0)
    hit1 = erow == i1
    hit2 = erow == i2
    member = jnp.where(hit1 | hit2, 1.0, 0.0)
    r = lax.broadcasted_iota(jnp.int32, (tr, tr), 0)
    c = lax.broadcasted_iota(jnp.int32, (tr, tr), 1)
    earlier = jnp.where(r < c, 1.0, 0.0).astype(BF16)
    carry = carry_ref[...]
    prefix = _dot(member.astype(BF16), earlier) + carry[:, 0:1]
    rank1 = jnp.sum(jnp.where(hit1, prefix, 0.0), axis=0, keepdims=True)
    rank2 = jnp.sum(jnp.where(hit2, prefix, 0.0), axis=0, keepdims=True)
    carry = carry + jnp.sum(member, axis=1, keepdims=True)
    carry_ref[...] = carry
    cnt_ref[...] = carry

    zero = jnp.zeros_like(w1)
    info_ref[...] = jnp.concatenate([i1.astype(F32), i2.astype(F32), rank1, rank2, w1, w2, zero, zero], axis=0)


def _route(logits_t, tr=512):
    T = logits_t.shape[1]
    return pl.pallas_call(
        _route_kernel,
        grid=(T // tr,),
        in_specs=[pl.BlockSpec((LANES, tr), lambda i: (0, i))],
        out_specs=[pl.BlockSpec((8, tr), lambda i: (0, i)),
                   pl.BlockSpec((MOE_N_EXPERTS, LANES), lambda i: (0, 0))],
        out_shape=[jax.ShapeDtypeStruct((8, T), F32), jax.ShapeDtypeStruct((MOE_N_EXPERTS, LANES), F32)],
        scratch_shapes=[pltpu.VMEM((MOE_N_EXPERTS, LANES), F32)],
        compiler_params=_cparams(("arbitrary",)),
        name="route",
    )(logits_t)


ROW_UNROLL = 8


def _dispatch_kernel(dest_ref, pend_ref, u_ref, xs_ref, zbuf, sem, zsem, *, td, T, nblk):
    @pl.when(pl.program_id(0) == 0)
    def _():
        zbuf[...] = jnp.zeros_like(zbuf)
        n_used = pend_ref[MOE_N_EXPERTS - 1] >> 8

        def zero_copy(blk):
            start = pl.multiple_of(blk * (MOE_ROWS * ROW_TILE), MOE_ROWS * ROW_TILE)
            return pltpu.make_async_copy(zbuf, xs_ref.at[pl.ds(start, MOE_ROWS * ROW_TILE)], zsem)

        def each_pad_block(fn):
            def per_expert(e, carry):
                prev = jnp.where(e > 0, pend_ref[jnp.maximum(e - 1, 0)], 0)

                @pl.when(pend_ref[e] > prev)
                def _():
                    fn((pend_ref[e] >> 8) - 1)
                return carry

            def per_tail(j, carry):
                @pl.when(n_used + j < nblk)
                def _():
                    fn(n_used + j)
                return carry

            lax.fori_loop(0, MOE_N_EXPERTS, per_expert, 0)
            lax.fori_loop(0, MOE_N_EXPERTS, per_tail, 0)

        each_pad_block(lambda blk: zero_copy(blk).start())
        each_pad_block(lambda blk: zero_copy(blk).wait())

    base = pl.program_id(0) * td

    def issue(g, carry):
        for j in range(ROW_UNROLL):
            r = g * ROW_UNROLL + j
            for k in range(MOE_TOP_K):
                _tile_copy(u_ref, r, xs_ref, dest_ref[k * T + base + r], sem).start(priority=k)
        return carry

    lax.fori_loop(0, td // ROW_UNROLL, issue, 0)
    for k in range(MOE_TOP_K):
        pltpu.make_async_copy(u_ref, xs_ref.at[pl.ds(0, td * ROW_TILE)], sem).wait()


def _dispatch(dest, pend, u2, cap, td=1024):
    T = u2.shape[0] // ROW_TILE
    return pl.pallas_call(
        functools.partial(_dispatch_kernel, td=td, T=T, nblk=cap // MOE_ROWS),
        grid_spec=pltpu.PrefetchScalarGridSpec(
            num_scalar_prefetch=2,
            grid=(T // td,),
            in_specs=[pl.BlockSpec((td * ROW_TILE, LANES), lambda i, d, z: (i, 0))],
            out_specs=pl.BlockSpec(memory_space=pl.ANY),
            scratch_shapes=[pltpu.VMEM((MOE_ROWS * ROW_TILE, LANES), jnp.int32),
                            pltpu.SemaphoreType.DMA(()), pltpu.SemaphoreType.DMA(())],
        ),
        out_shape=jax.ShapeDtypeStruct((cap * ROW_TILE, LANES), jnp.int32),
        compiler_params=_cparams(("arbitrary",)),
        name="dispatch",
    )(dest, pend, u2)


def _expert_kernel(pend_ref, xs_hbm, wg_hbm, wu_hbm, wd_hbm, ys_hbm,
                   xbuf, ybuf, zbuf, stage_g, stage_u, stage_d, wgb, wub, wdb, xsem, ysem, wsem, zsem, *, nblk):
    last = MOE_N_EXPERTS - 1
    n_used = pend_ref[last] >> 8
    block_rows = MOE_ROWS * ROW_TILE

    def x_copy(b, slot):
        start = pl.multiple_of(b * block_rows, block_rows)
        return pltpu.make_async_copy(xs_hbm.at[pl.ds(start, block_rows)], xbuf.at[slot], xsem.at[slot])

    def y_copy(b, slot):
        start = pl.multiple_of(b * block_rows, block_rows)
        return pltpu.make_async_copy(ybuf.at[slot], ys_hbm.at[pl.ds(start, block_rows)], ysem.at[slot])

    def zero_copy(b):
        start = pl.multiple_of(b * block_rows, block_rows)
        return pltpu.make_async_copy(zbuf, ys_hbm.at[pl.ds(start, block_rows)], zsem)

    def weight_copies(e):
        return (pltpu.make_async_copy(wg_hbm.at[e], stage_g, wsem.at[0]),
                pltpu.make_async_copy(wu_hbm.at[e], stage_u, wsem.at[1]),
                pltpu.make_async_copy(wd_hbm.at[e], stage_d, wsem.at[2]))

    def owner(start, row):
        return lax.while_loop(lambda e: (e < last) & (pend_ref[e] <= row), lambda e: e + 1, start)

    for c in weight_copies(owner(0, 0)):
        c.start()
    x_copy(0, 0).start()

    zbuf[...] = jnp.zeros_like(zbuf)

    def tail(fn):
        def step(b, carry):
            fn(b)
            return carry
        lax.fori_loop(n_used, nblk, step, 0)

    tail(lambda b: zero_copy(b).start())

    def body(b, cur):
        slot = b & 1
        e = owner(jnp.maximum(cur, 0), b * MOE_ROWS)
        x_copy(b, slot).wait()

        @pl.when(b + 1 < n_used)
        def _():
            x_copy(b + 1, 1 - slot).start()

        @pl.when(e != cur)
        def _():
            for c in weight_copies(e):
                c.wait()
            wgb[...] = stage_g[...].astype(BF16)
            wub[...] = stage_u[...].astype(BF16)
            wdb[...] = stage_d[...].astype(BF16)

            @pl.when(pend_ref[e] < pend_ref[last])
            def _():
                for c in weight_copies(owner(e + 1, pend_ref[e])):
                    c.start(priority=1)

        @pl.when(b >= 2)
        def _():
            y_copy(b - 2, slot).wait()

        xb = _unpack_rows(_from_row_tiles(xbuf.at[slot], MOE_ROWS))
        g = _dot(xb, wgb[...])
        u = _dot(xb, wub[...])
        hid = (g / (1.0 + jnp.exp(-g))) * u
        _to_row_tiles(ybuf.at[slot], _pack_rows(_dot(hid.astype(BF16), wdb[...])))
        y_copy(b, slot).start()
        return e

    lax.fori_loop(0, n_used, body, jnp.int32(-1))

    @pl.when(n_used >= 2)
    def _():
        y_copy(n_used - 2, n_used & 1).wait()
    y_copy(n_used - 1, (n_used - 1) & 1).wait()
    tail(lambda b: zero_copy(b).wait())


def _experts(pend, xs, w_gate, w_up, w_down):
    cap = xs.shape[0] // ROW_TILE
    nblk = cap // MOE_ROWS
    block = (MOE_ROWS * ROW_TILE, LANES)
    anywhere = pl.BlockSpec(memory_space=pl.ANY)
    return pl.pallas_call(
        functools.partial(_expert_kernel, nblk=nblk),
        grid_spec=pltpu.PrefetchScalarGridSpec(
            num_scalar_prefetch=1,
            grid=(1,),
            in_specs=[anywhere, anywhere, anywhere, anywhere],
            out_specs=anywhere,
            scratch_shapes=[pltpu.VMEM((2,) + block, jnp.int32),
                            pltpu.VMEM((2,) + block, jnp.int32),
                            pltpu.VMEM(block, jnp.int32),
                            pltpu.VMEM((D_MODEL, MOE_D_FF), F32),
                            pltpu.VMEM((D_MODEL, MOE_D_FF), F32),
                            pltpu.VMEM((MOE_D_FF, D_MODEL), F32),
                            pltpu.VMEM((D_MODEL, MOE_D_FF), BF16),
                            pltpu.VMEM((D_MODEL, MOE_D_FF), BF16),
                            pltpu.VMEM((MOE_D_FF, D_MODEL), BF16),
                            pltpu.SemaphoreType.DMA((2,)),
                            pltpu.SemaphoreType.DMA((2,)),
                            pltpu.SemaphoreType.DMA((3,)),
                            pltpu.SemaphoreType.DMA(())],
        ),
        out_shape=jax.ShapeDtypeStruct((cap * ROW_TILE, LANES), jnp.int32),
        compiler_params=_cparams(("arbitrary",)),
        name="experts",
    )(pend, xs, w_gate, w_up, w_down)


def _combine_kernel(dest_ref, ys_ref, info_ref, h_ref, fw_ref, o_ref, buf, sem, *, tc, T):
    i = pl.program_id(0)
    n = pl.num_programs(0)

    def issue(step, slot):
        base = step * tc

        def body(g, carry):
            for j in range(ROW_UNROLL):
                r = g * ROW_UNROLL + j
                for k in range(MOE_TOP_K):
                    _tile_copy(ys_ref, dest_ref[k * T + base + r], buf.at[slot, k], r,
                               sem.at[slot]).start(priority=k)
            return carry

        lax.fori_loop(0, tc // ROW_UNROLL, body, 0)

    @pl.when(i == 0)
    def _():
        issue(0, 0)

    slot = i % 2

    @pl.when(i + 1 < n)
    def _():
        issue(i + 1, 1 - slot)

    for k in range(MOE_TOP_K):
        pltpu.make_async_copy(ys_ref.at[pl.ds(0, tc * ROW_TILE)], buf.at[slot, k], sem.at[slot]).wait()

    info_t = jnp.concatenate([info_ref[...]] * (LANES // 8), axis=0).T
    w1 = info_t[:, INFO_W1:INFO_W1 + 1]
    w2 = info_t[:, INFO_W2:INFO_W2 + 1]
    y1 = _unpack_rows(_from_row_tiles(buf.at[slot, 0], tc)).astype(F32)
    y2 = _unpack_rows(_from_row_tiles(buf.at[slot, 1], tc)).astype(F32)
    h = h_ref[...] + (y1 * w1 + y2 * w2)
    o_ref[...] = _rms(h, fw_ref[...])


def _combine(dest, ys, info, h, final_w, tc=1024):
    T = h.shape[0]
    return pl.pallas_call(
        functools.partial(_combine_kernel, tc=tc, T=T),
        grid_spec=pltpu.PrefetchScalarGridSpec(
            num_scalar_prefetch=1,
            grid=(T // tc,),
            in_specs=[pl.BlockSpec(memory_space=pl.ANY),
                      pl.BlockSpec((8, tc), lambda i, d: (0, i)),
                      pl.BlockSpec((tc, D_MODEL), lambda i, d: (i, 0)),
                      pl.BlockSpec((1, D_MODEL), lambda i, d: (0, 0))],
            out_specs=pl.BlockSpec((tc, D_MODEL), lambda i, d: (i, 0)),
            scratch_shapes=[pltpu.VMEM((2, MOE_TOP_K, tc * ROW_TILE, LANES), jnp.int32),
                            pltpu.SemaphoreType.DMA((2,))],
        ),
        out_shape=jax.ShapeDtypeStruct((T, D_MODEL), F32),
        compiler_params=_cparams(("arbitrary",)),
        name="combine",
    )(dest, ys, info, h, final_w[None, :])


def _plan_kernel(info_ref, cnt_ref, dest_ref, pend_ref):
    cnt = cnt_ref[...].astype(jnp.int32)
    nblk_e = ((cnt + (MOE_ROWS - 1)) >> 8).astype(F32)
    r = lax.broadcasted_iota(jnp.int32, (MOE_N_EXPERTS, MOE_N_EXPERTS), 0)
    c = lax.broadcasted_iota(jnp.int32, (MOE_N_EXPERTS, MOE_N_EXPERTS), 1)
    before = jnp.where(c < r, 1.0, 0.0).astype(BF16)
    first_blk = _dot(before, nblk_e.astype(BF16))
    pstart = first_blk[:, 0:1] * float(MOE_ROWS)
    pend_ref[...] = ((first_blk + nblk_e) * float(MOE_ROWS)).astype(jnp.int32)

    info = info_ref[...]
    erow = lax.broadcasted_iota(jnp.int32, (MOE_N_EXPERTS, info.shape[1]), 0)
    start_of = lambda e: jnp.sum(jnp.where(erow == e.astype(jnp.int32), pstart, 0.0), axis=0, keepdims=True)
    d1 = info[INFO_R1:INFO_R1 + 1] + start_of(info[INFO_E1:INFO_E1 + 1])
    d2 = info[INFO_R2:INFO_R2 + 1] + start_of(info[INFO_E2:INFO_E2 + 1])
    zero = jnp.zeros_like(d1)
    dest_ref[...] = jnp.concatenate([d1, d2] + [zero] * 6, axis=0).astype(jnp.int32)


def _plan(info, counts, tr=2048):
    T = info.shape[1]
    dest8, pend = pl.pallas_call(
        _plan_kernel,
        grid=(T // tr,),
        in_specs=[pl.BlockSpec((8, tr), lambda i: (0, i)),
                  pl.BlockSpec((MOE_N_EXPERTS, LANES), lambda i: (0, 0))],
        out_specs=[pl.BlockSpec((8, tr), lambda i: (0, i)),
                   pl.BlockSpec((MOE_N_EXPERTS, LANES), lambda i: (0, 0))],
        out_shape=[jax.ShapeDtypeStruct((8, T), jnp.int32),
                   jax.ShapeDtypeStruct((MOE_N_EXPERTS, LANES), jnp.int32)],
        compiler_params=_cparams(("arbitrary",)),
        name="plan",
    )(info, counts)
    return dest8[:MOE_TOP_K].reshape(-1), pend[:, 0]


def _moe_capacity(T):
    return (-(-(T * MOE_TOP_K) // MOE_ROWS) + MOE_N_EXPERTS) * MOE_ROWS


def _router_weights(router_group_w, router_group_b, router_expert_w, router_expert_b):
    we = jnp.transpose(router_expert_w, (0, 2, 1)).reshape(MOE_N_EXPERTS, D_MODEL)
    pad = LANES - MOE_N_EXPERTS - MOE_GROUPS
    wr = jnp.concatenate([we, router_group_w.T, jnp.zeros((pad, D_MODEL), F32)], axis=0)
    br = jnp.concatenate([router_expert_b.reshape(-1), router_group_b, jnp.zeros((pad,), F32)])[:, None]
    return wr, br


def kernel(x, norm1_w, w_in, gla_fwd_gate_w, gla_fwd_gate_b, gla_bwd_gate_w, gla_bwd_gate_b,
           gla_norm_w, w_out, norm2_w, router_group_w, router_group_b, router_expert_w,
           router_expert_b, expert_w_gate, expert_w_up, expert_w_down, final_norm_w):
    B, S, D = x.shape
    T = B * S
    assert norm1_w.shape[0] == 1, "single-layer trunk: the final norm is fused into the combine step"
    h = x.reshape(T, D)
    gla_slab, loga, att_slab = _inproj(h, S, norm1_w[0], w_in[0], gla_fwd_gate_w[0], gla_fwd_gate_b[0],
                                       gla_bwd_gate_w[0], gla_bwd_gate_b[0])
    o_f, o_b = _gla(gla_slab, loga, B, S)
    att_out = _attention(att_slab.reshape(T, 3 * ATT_WIDTH), B, S)
    att_out = att_out.reshape(B, ATT_CLASSES, S // ATT_CLASSES, ATT_WIDTH)
    wr, br = _router_weights(router_group_w[0], router_group_b[0], router_expert_w[0], router_expert_b[0])
    h, u2, logits = _outproj(o_f, o_b, gla_slab, att_out, h, gla_norm_w[0], w_out[0], norm2_w[0], wr, br)
    info, counts = _route(logits)
    dest, pend = _plan(info, counts)
    xs = _dispatch(dest, pend, u2, _moe_capacity(T))
    ys = _experts(pend, xs, expert_w_gate[0], expert_w_up[0], expert_w_down[0])
    out = _combine(dest, ys, info, h, final_norm_w)
    return out.reshape(B, S, D)
```

```python
import functools

import jax
import jax.numpy as jnp
from jax import lax
from jax.experimental import pallas as pl
from jax.experimental.pallas import tpu as pltpu

F32 = jnp.float32
BF16 = jnp.bfloat16

D_MODEL = 1024
GLA_HEADS = 4
GLA_DV = 128
GLA_DK = 64
GLA_KEY_WIDTH = GLA_HEADS * GLA_DK
GLA_VAL_WIDTH = GLA_HEADS * GLA_DV
GLA_GATE_RANK = 16
GLA_TAU = 16.0
GLA_CHUNK = 64
ATT_WIDTH = 512
ATT_HEAD_DIM = 64
ATT_HEADS = 8
ROT_DIM = 16
ROPE_THETA = 500000.0
DILATED_PATTERNS = ((128, 1), (512, 4), (2048, 16))
ATT_RADIUS = 64
MOE_GROUPS = 4
MOE_EXPERTS_PER_GROUP = 8
MOE_N_EXPERTS = 32
MOE_TOP_K = 2
MOE_D_FF = 512
EPS = 1e-6
NEG_INF = -1e30
LOG2E = 1.4426950408889634

LANES = 128
MOE_ROWS = 256
VMEM_LIMIT = 56 * 1024 * 1024


def _cparams(sem):
    return pltpu.CompilerParams(dimension_semantics=sem, vmem_limit_bytes=VMEM_LIMIT)


def _dot(a, b):
    return jnp.dot(a, b, preferred_element_type=F32)


def _dot_nt(a, b):
    return lax.dot_general(a, b, (((1,), (1,)), ((), ())), preferred_element_type=F32)


def _dot_tn(a, b):
    return lax.dot_general(a, b, (((0,), (0,)), ((), ())), preferred_element_type=F32)


def _rms(x, w):
    return x * lax.rsqrt(jnp.mean(x * x, axis=-1, keepdims=True) + EPS) * w


def _inproj_kernel(x_ref, n1_ref, wg_ref, wlr_ref, wa_ref, gw_ref, gb_ref,
                   rc_ref, rs1_ref, rs2_ref, gla_ref, loga_ref, att_ref, stage_ref):
    x = x_ref[...]
    ub = _rms(x, n1_ref[...]).astype(BF16)
    g = _dot(ub, wg_ref[...])
    gla_ref[:, :GLA_KEY_WIDTH] = g[:, :GLA_KEY_WIDTH] * (GLA_DK ** -0.5)
    gla_ref[:, GLA_KEY_WIDTH:] = g[:, GLA_KEY_WIDTH:]
    lr = _dot(ub, wlr_ref[...])
    gate = _dot(lr.astype(BF16), gw_ref[...]) + gb_ref[...]
    loga_ref[...] = (jnp.minimum(gate, 0.0) - jnp.log(1.0 + jnp.exp(-jnp.abs(gate)))) * (1.0 / GLA_TAU)
    a = _dot(ub, wa_ref[...])
    qk = a[:, :2 * ATT_WIDTH]
    reps = 2 * ATT_WIDTH // LANES
    c = jnp.concatenate([rc_ref[...]] * reps, axis=1)
    s1 = jnp.concatenate([rs1_ref[...]] * reps, axis=1)
    s2 = jnp.concatenate([rs2_ref[...]] * reps, axis=1)
    half = ROT_DIM // 2
    n = 2 * ATT_WIDTH
    roped = qk * c + pltpu.roll(qk, n - half, 1) * s1 + pltpu.roll(qk, half, 1) * s2
    qkv = jnp.concatenate([roped[:, :ATT_WIDTH] * (ATT_HEAD_DIM ** -0.5 * LOG2E), roped[:, ATT_WIDTH:],
                           a[:, 2 * ATT_WIDTH:]], axis=1)
    rows = x.shape[0] // ATT_CLASSES
    for j in range(3 * ATT_WIDTH // LANES):
        cols = slice(j * LANES, (j + 1) * LANES)
        stage_ref[j] = qkv[:, cols]
        for c in range(ATT_CLASSES):
            att_ref[c, :, cols] = stage_ref[j, pl.ds(c, rows, stride=ATT_CLASSES), :]


def _rope_lane_tables(S):
    half = ROT_DIM // 2
    inv = ROPE_THETA ** (-(jnp.arange(0, ROT_DIM, 2, dtype=F32) / ROT_DIM))
    ang = inv[:, None] * jnp.arange(S, dtype=F32)[None, :]
    cos, sin = jnp.cos(ang), jnp.sin(ang)
    lane = jnp.arange(LANES) % ATT_HEAD_DIM
    freq = jnp.arange(half)[:, None]
    first = ((lane[None, :] == freq)).astype(F32)
    second = ((lane[None, :] == freq + half)).astype(F32)
    expand = lambda t, sel: lax.dot_general(t, sel, (((0,), (0,)), ((), ())), precision=lax.Precision.HIGHEST)
    rest = (lane >= ROT_DIM).astype(F32)[None, :]
    return expand(cos, first + second) + rest, expand(-sin, first), expand(sin, second)


def _inproj(x2, S, norm1_w, w_in, wf, bfw, wb, bbw, tm=512):
    T = x2.shape[0]
    o_lr = 2 * GLA_KEY_WIDTH + 2 * GLA_VAL_WIDTH
    o_att = o_lr + 2 * GLA_GATE_RANK
    w_main = w_in[:, :o_lr + LANES].astype(BF16)
    wa = w_in[:, o_att:].astype(BF16)
    zeros = jnp.zeros((GLA_GATE_RANK, GLA_KEY_WIDTH), F32)
    gw = jnp.concatenate([jnp.concatenate([wf, zeros], axis=1), jnp.concatenate([zeros, wb], axis=1),
                          jnp.zeros((LANES - 2 * GLA_GATE_RANK, 2 * GLA_KEY_WIDTH), F32)], axis=0).astype(BF16)
    gb = jnp.concatenate([bfw, bbw])[None, :]
    rc, rs1, rs2 = _rope_lane_tables(S)
    nS = S // tm
    row = lambda i: (i, 0)
    const = lambda i: (0, 0)
    pos = lambda i: (i % nS, 0)
    return pl.pallas_call(
        _inproj_kernel,
        grid=(T // tm,),
        in_specs=[
            pl.BlockSpec((tm, D_MODEL), row),
            pl.BlockSpec((1, D_MODEL), const),
            pl.BlockSpec((D_MODEL, o_lr), const),
            pl.BlockSpec((D_MODEL, LANES), lambda i: (0, o_lr // LANES)),
            pl.BlockSpec((D_MODEL, 3 * ATT_WIDTH), const),
            pl.BlockSpec((LANES, 2 * GLA_KEY_WIDTH), const),
            pl.BlockSpec((1, 2 * GLA_KEY_WIDTH), const),
            pl.BlockSpec((tm, LANES), pos),
            pl.BlockSpec((tm, LANES), pos),
            pl.BlockSpec((tm, LANES), pos),
        ],
        out_specs=[
            pl.BlockSpec((tm, o_lr), row),
            pl.BlockSpec((tm, 2 * GLA_KEY_WIDTH), row),
            pl.BlockSpec((None, ATT_CLASSES, tm // ATT_CLASSES, 3 * ATT_WIDTH),
                         lambda i: (i // nS, 0, i % nS, 0)),
        ],
        out_shape=[
            jax.ShapeDtypeStruct((T, o_lr), F32),
            jax.ShapeDtypeStruct((T, 2 * GLA_KEY_WIDTH), F32),
            jax.ShapeDtypeStruct((T // S, ATT_CLASSES, S // ATT_CLASSES, 3 * ATT_WIDTH), F32),
        ],
        scratch_shapes=[pltpu.VMEM((3 * ATT_WIDTH // LANES, tm, LANES), F32)],
        compiler_params=_cparams(("arbitrary",)),
        name="inproj",
    )(x2, norm1_w[None, :], w_main, w_main, wa, gw, gb, rc, rs1, rs2)


def _gla_decays(q, k, v, la, forward, G):
    C = GLA_CHUNK
    R = G * C
    r = lax.broadcasted_iota(jnp.int32, (R, R), 0)
    c = lax.broadcasted_iota(jnp.int32, (R, R), 1)
    same = (r >> 6) == (c >> 6)
    tri = (c <= r) if forward else (c >= r)
    t_mat = jnp.where(same, jnp.where(tri, 1.0, 0.0), 0.0).astype(BF16)
    hi = la.astype(BF16)
    lo = (la - hi.astype(F32)).astype(BF16)
    b = _dot(t_mat, hi) + _dot(t_mat, lo)
    edge = C - 1 if forward else 0
    tot = jnp.concatenate([jnp.broadcast_to(b[g * C + edge:g * C + edge + 1], (C, GLA_KEY_WIDTH))
                           for g in range(G)], axis=0)
    order = list(range(G)) if forward else list(range(G - 1, -1, -1))
    return dict(q_dec=q * jnp.exp(b), k_inv=(k * jnp.exp(-b)).astype(BF16), k_end=k * jnp.exp(tot - b),
                tot=tot, vb=v.astype(BF16), order=order, forward=forward, G=G)


def _gla_scores(prep):
    C, H = GLA_CHUNK, GLA_HEADS
    lane_k = lax.broadcasted_iota(jnp.int32, (C, GLA_KEY_WIDTH), 1)
    qd_heads, scores = {}, {}
    for g in prep["order"]:
        rows = slice(g * C, (g + 1) * C)
        qd = prep["q_dec"][rows]
        qd_heads[g] = jnp.concatenate([jnp.where((lane_k >> 6) == h, qd, 0.0) for h in range(H)],
                                      axis=0).astype(BF16)
        scores[g] = _dot_nt(qd_heads[g], prep["k_inv"][rows])
    return qd_heads, scores


def _gla_chunk_updates(prep):
    C, H, G = GLA_CHUNK, GLA_HEADS, prep["G"]
    k_end, tot, vb = prep["k_end"], prep["tot"], prep["vb"]
    kv, dec_t = {}, {}
    lane = lax.broadcasted_iota(jnp.int32, (GLA_KEY_WIDTH, 2 * C), 1)
    zeros = jnp.zeros((C, GLA_DV), BF16)
    for p in range(G // 2):
        pair = slice(2 * p * C, (2 * p + 2) * C)
        ke_t = k_end[pair].T.astype(BF16)
        tot_t = tot[pair].T
        swapped = pltpu.roll(tot_t, C, 1)
        for half in range(2):
            g = 2 * p + half
            rows = slice(g * C, (g + 1) * C)
            own = (lane < C) if half == 0 else (lane >= C)
            dec_t[g] = jnp.exp(jnp.where(own, tot_t, swapped))
            parts = []
            for h in range(H):
                v_h = vb[rows, h * GLA_DV:(h + 1) * GLA_DV]
                v_pad = jnp.concatenate([v_h, zeros] if half == 0 else [zeros, v_h], axis=0)
                parts.append(_dot(ke_t[h * C:(h + 1) * C], v_pad))
            kv[g] = jnp.concatenate(parts, axis=0)
    return kv, dec_t


def _gla_states(prep, kv, dec_t, s_ref):
    st = s_ref[...]
    states = {}
    for g in prep["order"]:
        states[g] = st.astype(BF16)
        st = st * dec_t[g] + kv[g]
    s_ref[...] = st
    return states


def _gla_outputs(prep, qd_heads, scores, inter, o_ref):
    C, H = GLA_CHUNK, GLA_HEADS
    row_q = lax.broadcasted_iota(jnp.int32, (H * C, C), 0) & (C - 1)
    col_k = lax.broadcasted_iota(jnp.int32, (H * C, C), 1)
    a_mask = (col_k <= row_q) if prep["forward"] else (col_k >= row_q)
    for g in prep["order"]:
        rows = slice(g * C, (g + 1) * C)
        a = jnp.where(a_mask, scores[g], 0.0).astype(BF16)
        vv = prep["vb"][rows]
        o_ref[rows, :] = jnp.concatenate(
            [_dot(a[h * C:(h + 1) * C], vv[:, h * GLA_DV:(h + 1) * GLA_DV]) + inter[g][h * C:(h + 1) * C]
             for h in range(H)], axis=1)


def _gla_kernel(qf_ref, kf_ref, vf_ref, laf_ref, qb_ref, kb_ref, vb_ref, lab_ref,
                of_ref, ob_ref, sf_ref, sb_ref, *, G):
    @pl.when(pl.program_id(1) == 0)
    def _():
        sf_ref[...] = jnp.zeros_like(sf_ref)
        sb_ref[...] = jnp.zeros_like(sb_ref)

    dirs = [(_gla_decays(qf_ref[...], kf_ref[...], vf_ref[...], laf_ref[...], True, G), sf_ref, of_ref),
            (_gla_decays(qb_ref[...], kb_ref[...], vb_ref[...], lab_ref[...], False, G), sb_ref, ob_ref)]
    scored = [_gla_scores(prep) for prep, _, _ in dirs]
    updates = [_gla_chunk_updates(prep) for prep, _, _ in dirs]
    states = [_gla_states(prep, kv, dec_t, s_ref) for (prep, s_ref, _), (kv, dec_t) in zip(dirs, updates)]
    inters = [{g: _dot(qd_heads[g], st[g]) for g in prep["order"]}
              for (prep, _, _), (qd_heads, _), st in zip(dirs, scored, states)]
    for (prep, _, o_ref), (qd_heads, scores), inter in zip(dirs, scored, inters):
        _gla_outputs(prep, qd_heads, scores, inter, o_ref)


def _gla(gla_slab, loga, B, S, G=8):
    T = B * S
    R = G * GLA_CHUNK
    ns = S // R
    fwd = lambda col: (lambda b, i: (b * ns + i, col))
    bwd = lambda col: (lambda b, i: (b * ns + ns - 1 - i, col))
    kw, vw = GLA_KEY_WIDTH, GLA_VAL_WIDTH
    return pl.pallas_call(
        functools.partial(_gla_kernel, G=G),
        grid=(B, ns),
        in_specs=[
            pl.BlockSpec((R, kw), fwd(0)), pl.BlockSpec((R, kw), fwd(1)),
            pl.BlockSpec((R, vw), fwd(1)), pl.BlockSpec((R, kw), fwd(0)),
            pl.BlockSpec((R, kw), bwd(0)), pl.BlockSpec((R, kw), bwd(1)),
            pl.BlockSpec((R, vw), bwd(1)), pl.BlockSpec((R, kw), bwd(1)),
        ],
        out_specs=[pl.BlockSpec((R, vw), fwd(0)), pl.BlockSpec((R, vw), bwd(0))],
        out_shape=[jax.ShapeDtypeStruct((T, vw), F32), jax.ShapeDtypeStruct((T, vw), F32)],
        scratch_shapes=[pltpu.VMEM((kw, GLA_DV), F32), pltpu.VMEM((kw, GLA_DV), F32)],
        compiler_params=_cparams(("arbitrary", "arbitrary")),
        name="gla",
    )(gla_slab, gla_slab, gla_slab, loga, gla_slab, gla_slab, gla_slab, loga)


ATT_CLASSES = 4
ATT_QB = 128
ATT_KB = ATT_QB + 2 * ATT_RADIUS


ATT_UNROLL = 4


def _att_kernel(q_ref, k_ref, v_ref, o_ref, m_ref, l_ref, bias_ref, *, S):
    QB, KB, NC = ATT_QB, ATT_KB, ATT_CLASSES
    L4 = S // NC
    lane = lax.broadcasted_iota(jnp.int32, (QB, LANES), 1)
    head0 = lane < ATT_HEAD_DIM

    @pl.when((pl.program_id(0) == 0) & (pl.program_id(1) == 0))
    def _():
        rowi = lax.broadcasted_iota(jnp.int32, (2 * QB, KB), 0) & (QB - 1)
        coli = lax.broadcasted_iota(jnp.int32, (2 * QB, KB), 1)
        qpos = (rowi & (QB // NC - 1)) * NC + (rowi >> 5)
        kpos = (coli & (KB // NC - 1)) * NC + (coli >> 6)
        for case in range(3):
            bias_ref[0, case] = jnp.where(jnp.abs(rowi - coli + case * ATT_RADIUS) <= ATT_RADIUS, 0.0, NEG_INF)
            bias_ref[1, case] = jnp.where(jnp.abs(qpos - kpos + case * ATT_RADIUS) <= ATT_RADIUS, 0.0, NEG_INF)

    for pi, (_, d) in enumerate(DILATED_PATTERNS):
        L = S // d
        nb = L // QB
        shift = nb.bit_length() - 1
        first = pi == 0
        last = pi == len(DILATED_PATTERNS) - 1

        def scores(n, d=d, L=L, nb=nb, shift=shift):
            cls = n >> shift
            q0 = (n & (nb - 1)) * QB
            ws = jnp.clip(q0 - ATT_RADIUS, 0, L - KB)
            if d == 1:
                qsls = [pl.ds(pl.multiple_of(c * L4 + q0 // NC, QB // NC), QB // NC) for c in range(NC)]
                ksls = [pl.ds(pl.multiple_of(c * L4 + ws // NC, ATT_RADIUS // NC), KB // NC) for c in range(NC)]
            elif d == NC:
                qsls = [pl.ds(pl.multiple_of(cls * L4 + q0, QB), QB)]
                ksls = [pl.ds(pl.multiple_of(cls * L4 + ws, ATT_RADIUS), KB)]
            else:
                base = (cls & (NC - 1)) * L4 + (cls >> 2)
                qsls = [pl.ds(base + NC * q0, QB, stride=NC)]
                ksls = [pl.ds(base + NC * ws, KB, stride=NC)]
            q = jnp.concatenate([q_ref[sl, :] for sl in qsls], axis=0)
            kw = jnp.concatenate([k_ref[sl, :] for sl in ksls], axis=0)
            q2 = jnp.concatenate([jnp.where(head0, q, 0.0), jnp.where(head0, 0.0, q)], axis=0).astype(BF16)
            s = _dot_nt(q2, kw.astype(BF16))
            return qsls, ksls, s + bias_ref[1 if d == 1 else 0, (q0 - ws) >> 6]

        def softmax_pv(qsls, ksls, s):
            m_blk = jnp.max(s, axis=-1, keepdims=True)
            p = jnp.exp2(s - m_blk)
            vw = jnp.concatenate([v_ref[sl, :] for sl in ksls], axis=0)
            v_ones = jnp.concatenate([vw.astype(BF16), jnp.ones((KB, LANES), BF16)], axis=1)
            pv = _dot(p.astype(BF16), v_ones)
            acc_b = jnp.where(head0, pv[:QB, :LANES], pv[QB:, :LANES])
            m_b = jnp.where(head0, m_blk[:QB], m_blk[QB:])
            l_b = jnp.where(head0, pv[:QB, LANES:], pv[QB:, LANES:])
            return qsls, acc_b, m_b, l_b

        def load(ref, sls):
            return jnp.concatenate([ref[sl, :] for sl in sls], axis=0)

        def store(ref, sls, val):
            n = val.shape[0] // len(sls)
            for i, sl in enumerate(sls):
                ref[sl, :] = val[i * n:(i + 1) * n]

        def body(n, carry, first=first, last=last):
            staged = [scores(n * ATT_UNROLL + u) for u in range(ATT_UNROLL)]
            blocks = [softmax_pv(*st) for st in staged]
            for qsls, acc_b, m_b, l_b in blocks:
                if first:
                    acc, m_new, l_new = acc_b, m_b, l_b
                else:
                    m_old = load(m_ref, qsls)
                    m_new = jnp.maximum(m_old, m_b)
                    w_old = jnp.exp2(m_old - m_new)
                    w_blk = jnp.exp2(m_b - m_new)
                    acc = load(o_ref, qsls) * w_old + acc_b * w_blk
                    l_new = load(l_ref, qsls) * w_old + l_b * w_blk
                if last:
                    store(o_ref, qsls, acc / l_new)
                else:
                    store(o_ref, qsls, acc)
                    store(m_ref, qsls, m_new)
                    store(l_ref, qsls, l_new)
            return carry

        lax.fori_loop(0, S // (QB * ATT_UNROLL), body, 0)


def _attention(att_slab, B, S):
    T = B * S
    ncol = ATT_WIDTH // LANES
    return pl.pallas_call(
        functools.partial(_att_kernel, S=S),
        grid=(B, ncol),
        in_specs=[
            pl.BlockSpec((S, LANES), lambda b, h: (b, h)),
            pl.BlockSpec((S, LANES), lambda b, h: (b, ncol + h)),
            pl.BlockSpec((S, LANES), lambda b, h: (b, 2 * ncol + h)),
        ],
        out_specs=pl.BlockSpec((S, LANES), lambda b, h: (b, h)),
        out_shape=jax.ShapeDtypeStruct((T, ATT_WIDTH), F32),
        scratch_shapes=[pltpu.VMEM((S, LANES), F32), pltpu.VMEM((S, LANES), F32),
                        pltpu.VMEM((2, 3, 2 * ATT_QB, ATT_KB), F32)],
        compiler_params=_cparams(("arbitrary", "arbitrary")),
        name="dilated_attention",
    )(att_slab, att_slab, att_slab)


PACK_WORDS = D_MODEL // 2
ROW_TILE = PACK_WORDS // LANES
HIGH_HALF = -65536


def _pack_rows(x):
    bits = lambda v: lax.bitcast_convert_type(v.astype(BF16).astype(F32), jnp.int32)
    low = (bits(x[:, :PACK_WORDS]) >> 16) & 0xFFFF
    return (bits(x[:, PACK_WORDS:]) & HIGH_HALF) | low


def _unpack_rows(w):
    low = lax.bitcast_convert_type(w << 16, F32)
    high = lax.bitcast_convert_type(w & HIGH_HALF, F32)
    return jnp.concatenate([low, high], axis=1).astype(BF16)


def _to_row_tiles(ref, w):
    n = w.shape[0]
    for j in range(ROW_TILE):
        ref[pl.ds(j, n, stride=ROW_TILE), :] = w[:, j * LANES:(j + 1) * LANES]


def _from_row_tiles(ref, n):
    return jnp.concatenate([ref[pl.ds(j, n, stride=ROW_TILE), :] for j in range(ROW_TILE)], axis=1)


def _tile_copy(src_ref, src_row, dst_ref, dst_row, sem):
    src = pl.ds(pl.multiple_of(src_row * ROW_TILE, ROW_TILE), ROW_TILE)
    dst = pl.ds(pl.multiple_of(dst_row * ROW_TILE, ROW_TILE), ROW_TILE)
    return pltpu.make_async_copy(src_ref.at[src], dst_ref.at[dst], sem)


def _outproj_kernel(of_ref, ob_ref, gg_ref, att_ref, x_ref, gnw_ref, wo1_ref, wo2_ref,
                    n2_ref, wr_ref, br_ref, h_ref, u_ref, lg_ref, stage_ref):
    rows = stage_ref.shape[1] // ATT_CLASSES
    for j in range(ATT_WIDTH // LANES):
        for c in range(ATT_CLASSES):
            stage_ref[j, pl.ds(c, rows, stride=ATT_CLASSES), :] = att_ref[c, :, j * LANES:(j + 1) * LANES]
    att = jnp.concatenate([stage_ref[j] for j in range(ATT_WIDTH // LANES)], axis=1)
    o = of_ref[...] + ob_ref[...]
    gate = gg_ref[...]
    gnw = gnw_ref[...]
    parts = []
    for h in range(GLA_HEADS):
        sl = slice(h * GLA_DV, (h + 1) * GLA_DV)
        parts.append(_rms(o[:, sl], gnw))
    y = jnp.concatenate(parts, axis=1) * (gate / (1.0 + jnp.exp(-gate)))
    mix = _dot(y.astype(BF16), wo1_ref[...]) + _dot(att.astype(BF16), wo2_ref[...])
    h = x_ref[...] + mix
    h_ref[...] = h
    u = _rms(h, n2_ref[...])
    _to_row_tiles(u_ref, _pack_rows(u))
    u_hi = u.astype(BF16)
    u_lo = (u - u_hi.astype(F32)).astype(BF16)
    hi_both = _dot_nt(wr_ref[...], u_hi)
    lg_ref[...] = (hi_both[:LANES] + hi_both[LANES:] + _dot_nt(wr_ref[:LANES], u_lo)) + br_ref[...]


def _outproj(o_f, o_b, gla_slab, att_out, x2, gla_norm_w, w_out, norm2_w, wr, br, tm=512):
    T = x2.shape[0]
    nS = att_out.shape[2] * ATT_CLASSES // tm
    row = lambda i: (i, 0)
    const = lambda i: (0, 0)
    wo = w_out.astype(BF16)
    wr_hi = wr.astype(BF16)
    wr_lo = (wr - wr_hi.astype(F32)).astype(BF16)
    wr = jnp.concatenate([wr_hi, wr_lo], axis=0)
    return pl.pallas_call(
        _outproj_kernel,
        grid=(T // tm,),
        in_specs=[
            pl.BlockSpec((tm, GLA_VAL_WIDTH), row),
            pl.BlockSpec((tm, GLA_VAL_WIDTH), row),
            pl.BlockSpec((tm, GLA_VAL_WIDTH), lambda i: (i, 2)),
            pl.BlockSpec((None, ATT_CLASSES, tm // ATT_CLASSES, ATT_WIDTH), lambda i: (i // nS, 0, i % nS, 0)),
            pl.BlockSpec((tm, D_MODEL), row),
            pl.BlockSpec((1, GLA_DV), const),
            pl.BlockSpec((GLA_VAL_WIDTH, D_MODEL), const),
            pl.BlockSpec((ATT_WIDTH, D_MODEL), const),
            pl.BlockSpec((1, D_MODEL), const),
            pl.BlockSpec((2 * LANES, D_MODEL), const),
            pl.BlockSpec((LANES, 1), const),
        ],
        out_specs=[
            pl.BlockSpec((tm, D_MODEL), row),
            pl.BlockSpec((tm * ROW_TILE, LANES), row),
            pl.BlockSpec((LANES, tm), lambda i: (0, i)),
        ],
        out_shape=[
            jax.ShapeDtypeStruct((T, D_MODEL), F32),
            jax.ShapeDtypeStruct((T * ROW_TILE, LANES), jnp.int32),
            jax.ShapeDtypeStruct((LANES, T), F32),
        ],
        scratch_shapes=[pltpu.VMEM((ATT_WIDTH // LANES, tm, LANES), F32)],
        compiler_params=_cparams(("arbitrary",)),
        name="outproj",
    )(o_f, o_b, gla_slab, att_out, x2, gla_norm_w[None, :], wo[:GLA_VAL_WIDTH], wo[GLA_VAL_WIDTH:],
      norm2_w[None, :], wr, br)


INFO_E1, INFO_E2, INFO_R1, INFO_R2, INFO_W1, INFO_W2 = range(6)
ROUTE_ROWS = 40


def _route_kernel(lg_ref, info_ref, cnt_ref, carry_ref):
    @pl.when(pl.program_id(0) == 0)
    def _():
        carry_ref[...] = jnp.zeros_like(carry_ref)

    lg = lg_ref[:ROUTE_ROWS, :]
    tr = lg.shape[1]
    row = lax.broadcasted_iota(jnp.int32, (ROUTE_ROWS, tr), 0)
    big = jnp.int32(1 << 20)
    is_g = (row >= MOE_N_EXPERTS) & (row < MOE_N_EXPERTS + MOE_GROUPS)
    gl = jnp.where(is_g, lg, -jnp.inf)
    gmax = jnp.max(gl, axis=0, keepdims=True)
    gsel = jnp.min(jnp.where(gl == gmax, row - MOE_N_EXPERTS, big), axis=0, keepdims=True)
    g_w = 1.0 / jnp.sum(jnp.where(is_g, jnp.exp(lg - gmax), 0.0), axis=0, keepdims=True)
    in_grp = (row < MOE_N_EXPERTS) & ((row >> 3) == gsel)
    el = jnp.where(in_grp, lg, -jnp.inf)
    v1 = jnp.max(el, axis=0, keepdims=True)
    i1 = jnp.min(jnp.where(el == v1, row, big), axis=0, keepdims=True)
    el2 = jnp.where(row == i1, -jnp.inf, el)
    v2 = jnp.max(el2, axis=0, keepdims=True)
    i2 = jnp.min(jnp.where(el2 == v2, row, big), axis=0, keepdims=True)
    t = jnp.exp(v2 - v1)
    w1 = g_w * (1.0 / (1.0 + t))
    w2 = g_w * (t / (1.0 + t))

    erow = lax.broadcasted_iota(jnp.int32, (MOE_N_EXPERTS, tr), 0)
    hit1 = erow == i1
    hit2 = erow == i2
    member = jnp.where(hit1 | hit2, 1.0, 0.0)
    r = lax.broadcasted_iota(jnp.int32, (tr, tr), 0)
    c = lax.broadcasted_iota(jnp.int32, (tr, tr), 1)
    earlier = jnp.where(r < c, 1.0, 0.0).astype(BF16)
    carry = carry_ref[...]
    prefix = _dot(member.astype(BF16), earlier) + carry[:, 0:1]
    rank1 = jnp.sum(jnp.where(hit1, prefix, 0.0), axis=0, keepdims=True)
    rank2 = jnp.sum(jnp.where(hit2, prefix, 0.0), axis=0, keepdims=True)
    carry = carry + jnp.sum(member, axis=1, keepdims=True)
    carry_ref[...] = carry
    cnt_ref[...] = carry

    zero = jnp.zeros_like(w1)
    info_ref[...] = jnp.concatenate([i1.astype(F32), i2.astype(F32), rank1, rank2, w1, w2, zero, zero], axis=0)


def _route(logits_t, tr=512):
    T = logits_t.shape[1]
    return pl.pallas_call(
        _route_kernel,
        grid=(T // tr,),
        in_specs=[pl.BlockSpec((LANES, tr), lambda i: (0, i))],
        out_specs=[pl.BlockSpec((8, tr), lambda i: (0, i)),
                   pl.BlockSpec((MOE_N_EXPERTS, LANES), lambda i: (0, 0))],
        out_shape=[jax.ShapeDtypeStruct((8, T), F32), jax.ShapeDtypeStruct((MOE_N_EXPERTS, LANES), F32)],
        scratch_shapes=[pltpu.VMEM((MOE_N_EXPERTS, LANES), F32)],
        compiler_params=_cparams(("arbitrary",)),
        name="route",
    )(logits_t)


ROW_UNROLL = 8


def _dispatch_kernel(dest_ref, pend_ref, u_ref, xs_ref, zbuf, sem, zsem, *, td, T, nblk):
    @pl.when(pl.program_id(0) == 0)
    def _():
        zbuf[...] = jnp.zeros_like(zbuf)
        n_used = pend_ref[MOE_N_EXPERTS - 1] >> 8

        def zero_copy(blk):
            start = pl.multiple_of(blk * (MOE_ROWS * ROW_TILE), MOE_ROWS * ROW_TILE)
            return pltpu.make_async_copy(zbuf, xs_ref.at[pl.ds(start, MOE_ROWS * ROW_TILE)], zsem)

        def each_pad_block(fn):
            def per_expert(e, carry):
                prev = jnp.where(e > 0, pend_ref[jnp.maximum(e - 1, 0)], 0)

                @pl.when(pend_ref[e] > prev)
                def _():
                    fn((pend_ref[e] >> 8) - 1)
                return carry

            def per_tail(j, carry):
                @pl.when(n_used + j < nblk)
                def _():
                    fn(n_used + j)
                return carry

            lax.fori_loop(0, MOE_N_EXPERTS, per_expert, 0)
            lax.fori_loop(0, MOE_N_EXPERTS, per_tail, 0)

        each_pad_block(lambda blk: zero_copy(blk).start())
        each_pad_block(lambda blk: zero_copy(blk).wait())

    base = pl.program_id(0) * td

    def issue(g, carry):
        for j in range(ROW_UNROLL):
            r = g * ROW_UNROLL + j
            for k in range(MOE_TOP_K):
                _tile_copy(u_ref, r, xs_ref, dest_ref[k * T + base + r], sem).start(priority=k)
        return carry

    lax.fori_loop(0, td // ROW_UNROLL, issue, 0)
    for k in range(MOE_TOP_K):
        pltpu.make_async_copy(u_ref, xs_ref.at[pl.ds(0, td * ROW_TILE)], sem).wait()


def _dispatch(dest, pend, u2, cap, td=1024):
    T = u2.shape[0] // ROW_TILE
    return pl.pallas_call(
        functools.partial(_dispatch_kernel, td=td, T=T, nblk=cap // MOE_ROWS),
        grid_spec=pltpu.PrefetchScalarGridSpec(
            num_scalar_prefetch=2,
            grid=(T // td,),
            in_specs=[pl.BlockSpec((td * ROW_TILE, LANES), lambda i, d, z: (i, 0))],
            out_specs=pl.BlockSpec(memory_space=pl.ANY),
            scratch_shapes=[pltpu.VMEM((MOE_ROWS * ROW_TILE, LANES), jnp.int32),
                            pltpu.SemaphoreType.DMA(()), pltpu.SemaphoreType.DMA(())],
        ),
        out_shape=jax.ShapeDtypeStruct((cap * ROW_TILE, LANES), jnp.int32),
        compiler_params=_cparams(("arbitrary",)),
        name="dispatch",
    )(dest, pend, u2)


def _expert_kernel(pend_ref, xs_hbm, wg_hbm, wu_hbm, wd_hbm, ys_hbm,
                   xbuf, ybuf, zbuf, stage_g, stage_u, stage_d, wgb, wub, wdb, xsem, ysem, wsem, zsem, *, nblk):
    last = MOE_N_EXPERTS - 1
    n_used = pend_ref[last] >> 8
    block_rows = MOE_ROWS * ROW_TILE

    def x_copy(b, slot):
        start = pl.multiple_of(b * block_rows, block_rows)
        return pltpu.make_async_copy(xs_hbm.at[pl.ds(start, block_rows)], xbuf.at[slot], xsem.at[slot])

    def y_copy(b, slot):
        start = pl.multiple_of(b * block_rows, block_rows)
        return pltpu.make_async_copy(ybuf.at[slot], ys_hbm.at[pl.ds(start, block_rows)], ysem.at[slot])

    def zero_copy(b):
        start = pl.multiple_of(b * block_rows, block_rows)
        return pltpu.make_async_copy(zbuf, ys_hbm.at[pl.ds(start, block_rows)], zsem)

    def weight_copies(e):
        return (pltpu.make_async_copy(wg_hbm.at[e], stage_g, wsem.at[0]),
                pltpu.make_async_copy(wu_hbm.at[e], stage_u, wsem.at[1]),
                pltpu.make_async_copy(wd_hbm.at[e], stage_d, wsem.at[2]))

    def owner(start, row):
        return lax.while_loop(lambda e: (e < last) & (pend_ref[e] <= row), lambda e: e + 1, start)

    for c in weight_copies(owner(0, 0)):
        c.start()
    x_copy(0, 0).start()

    zbuf[...] = jnp.zeros_like(zbuf)

    def tail(fn):
        def step(b, carry):
            fn(b)
            return carry
        lax.fori_loop(n_used, nblk, step, 0)

    tail(lambda b: zero_copy(b).start())

    def body(b, cur):
        slot = b & 1
        e = owner(jnp.maximum(cur, 0), b * MOE_ROWS)
        x_copy(b, slot).wait()

        @pl.when(b + 1 < n_used)
        def _():
            x_copy(b + 1, 1 - slot).start()

        @pl.when(e != cur)
        def _():
            for c in weight_copies(e):
                c.wait()
            wgb[...] = stage_g[...].astype(BF16)
            wub[...] = stage_u[...].astype(BF16)
            wdb[...] = stage_d[...].astype(BF16)

            @pl.when(pend_ref[e] < pend_ref[last])
            def _():
                for c in weight_copies(owner(e + 1, pend_ref[e])):
                    c.start(priority=1)

        @pl.when(b >= 2)
        def _():
            y_copy(b - 2, slot).wait()

        xb = _unpack_rows(_from_row_tiles(xbuf.at[slot], MOE_ROWS))
        g = _dot(xb, wgb[...])
        u = _dot(xb, wub[...])
        hid = (g / (1.0 + jnp.exp(-g))) * u
        _to_row_tiles(ybuf.at[slot], _pack_rows(_dot(hid.astype(BF16), wdb[...])))
        y_copy(b, slot).start()
        return e

    lax.fori_loop(0, n_used, body, jnp.int32(-1))

    @pl.when(n_used >= 2)
    def _():
        y_copy(n_used - 2, n_used & 1).wait()
    y_copy(n_used - 1, (n_used - 1) & 1).wait()
    tail(lambda b: zero_copy(b).wait())


def _experts(pend, xs, w_gate, w_up, w_down):
    cap = xs.shape[0] // ROW_TILE
    nblk = cap // MOE_ROWS
    block = (MOE_ROWS * ROW_TILE, LANES)
    anywhere = pl.BlockSpec(memory_space=pl.ANY)
    return pl.pallas_call(
        functools.partial(_expert_kernel, nblk=nblk),
        grid_spec=pltpu.PrefetchScalarGridSpec(
            num_scalar_prefetch=1,
            grid=(1,),
            in_specs=[anywhere, anywhere, anywhere, anywhere],
            out_specs=anywhere,
            scratch_shapes=[pltpu.VMEM((2,) + block, jnp.int32),
                            pltpu.VMEM((2,) + block, jnp.int32),
                            pltpu.VMEM(block, jnp.int32),
                            pltpu.VMEM((D_MODEL, MOE_D_FF), F32),
                            pltpu.VMEM((D_MODEL, MOE_D_FF), F32),
                            pltpu.VMEM((MOE_D_FF, D_MODEL), F32),
                            pltpu.VMEM((D_MODEL, MOE_D_FF), BF16),
                            pltpu.VMEM((D_MODEL, MOE_D_FF), BF16),
                            pltpu.VMEM((MOE_D_FF, D_MODEL), BF16),
                            pltpu.SemaphoreType.DMA((2,)),
                            pltpu.SemaphoreType.DMA((2,)),
                            pltpu.SemaphoreType.DMA((3,)),
                            pltpu.SemaphoreType.DMA(())],
        ),
        out_shape=jax.ShapeDtypeStruct((cap * ROW_TILE, LANES), jnp.int32),
        compiler_params=_cparams(("arbitrary",)),
        name="experts",
    )(pend, xs, w_gate, w_up, w_down)


def _combine_kernel(dest_ref, ys_ref, info_ref, h_ref, fw_ref, o_ref, buf, sem, *, tc, T):
    i = pl.program_id(0)
    n = pl.num_programs(0)

    def issue(step, slot):
        base = step * tc

        def body(g, carry):
            for j in range(ROW_UNROLL):
                r = g * ROW_UNROLL + j
                for k in range(MOE_TOP_K):
                    _tile_copy(ys_ref, dest_ref[k * T + base + r], buf.at[slot, k], r,
                               sem.at[slot]).start(priority=k)
            return carry

        lax.fori_loop(0, tc // ROW_UNROLL, body, 0)

    @pl.when(i == 0)
    def _():
        issue(0, 0)

    slot = i % 2

    @pl.when(i + 1 < n)
    def _():
        issue(i + 1, 1 - slot)

    for k in range(MOE_TOP_K):
        pltpu.make_async_copy(ys_ref.at[pl.ds(0, tc * ROW_TILE)], buf.at[slot, k], sem.at[slot]).wait()

    info_t = jnp.concatenate([info_ref[...]] * (LANES // 8), axis=0).T
    w1 = info_t[:, INFO_W1:INFO_W1 + 1]
    w2 = info_t[:, INFO_W2:INFO_W2 + 1]
    y1 = _unpack_rows(_from_row_tiles(buf.at[slot, 0], tc)).astype(F32)
    y2 = _unpack_rows(_from_row_tiles(buf.at[slot, 1], tc)).astype(F32)
    h = h_ref[...] + (y1 * w1 + y2 * w2)
    o_ref[...] = _rms(h, fw_ref[...])


def _combine(dest, ys, info, h, final_w, tc=512):
    T = h.shape[0]
    return pl.pallas_call(
        functools.partial(_combine_kernel, tc=tc, T=T),
        grid_spec=pltpu.PrefetchScalarGridSpec(
            num_scalar_prefetch=1,
            grid=(T // tc,),
            in_specs=[pl.BlockSpec(memory_space=pl.ANY),
                      pl.BlockSpec((8, tc), lambda i, d: (0, i)),
                      pl.BlockSpec((tc, D_MODEL), lambda i, d: (i, 0)),
                      pl.BlockSpec((1, D_MODEL), lambda i, d: (0, 0))],
            out_specs=pl.BlockSpec((tc, D_MODEL), lambda i, d: (i, 0)),
            scratch_shapes=[pltpu.VMEM((2, MOE_TOP_K, tc * ROW_TILE, LANES), jnp.int32),
                            pltpu.SemaphoreType.DMA((2,))],
        ),
        out_shape=jax.ShapeDtypeStruct((T, D_MODEL), F32),
        compiler_params=_cparams(("arbitrary",)),
        name="combine",
    )(dest, ys, info, h, final_w[None, :])


def _plan_kernel(info_ref, cnt_ref, dest_ref, pend_ref):
    cnt = cnt_ref[...].astype(jnp.int32)
    nblk_e = ((cnt + (MOE_ROWS - 1)) >> 8).astype(F32)
    r = lax.broadcasted_iota(jnp.int32, (MOE_N_EXPERTS, MOE_N_EXPERTS), 0)
    c = lax.broadcasted_iota(jnp.int32, (MOE_N_EXPERTS, MOE_N_EXPERTS), 1)
    before = jnp.where(c < r, 1.0, 0.0).astype(BF16)
    first_blk = _dot(before, nblk_e.astype(BF16))
    pstart = first_blk[:, 0:1] * float(MOE_ROWS)
    pend_ref[...] = ((first_blk + nblk_e) * float(MOE_ROWS)).astype(jnp.int32)

    info = info_ref[...]
    erow = lax.broadcasted_iota(jnp.int32, (MOE_N_EXPERTS, info.shape[1]), 0)
    start_of = lambda e: jnp.sum(jnp.where(erow == e.astype(jnp.int32), pstart, 0.0), axis=0, keepdims=True)
    d1 = info[INFO_R1:INFO_R1 + 1] + start_of(info[INFO_E1:INFO_E1 + 1])
    d2 = info[INFO_R2:INFO_R2 + 1] + start_of(info[INFO_E2:INFO_E2 + 1])
    zero = jnp.zeros_like(d1)
    dest_ref[...] = jnp.concatenate([d1, d2] + [zero] * 6, axis=0).astype(jnp.int32)


def _plan(info, counts, tr=2048):
    T = info.shape[1]
    dest8, pend = pl.pallas_call(
        _plan_kernel,
        grid=(T // tr,),
        in_specs=[pl.BlockSpec((8, tr), lambda i: (0, i)),
                  pl.BlockSpec((MOE_N_EXPERTS, LANES), lambda i: (0, 0))],
        out_specs=[pl.BlockSpec((8, tr), lambda i: (0, i)),
                   pl.BlockSpec((MOE_N_EXPERTS, LANES), lambda i: (0, 0))],
        out_shape=[jax.ShapeDtypeStruct((8, T), jnp.int32),
                   jax.ShapeDtypeStruct((MOE_N_EXPERTS, LANES), jnp.int32)],
        compiler_params=_cparams(("arbitrary",)),
        name="plan",
    )(info, counts)
    return dest8[:MOE_TOP_K].reshape(-1), pend[:, 0]


def _moe_capacity(T):
    return (-(-(T * MOE_TOP_K) // MOE_ROWS) + MOE_N_EXPERTS) * MOE_ROWS


def _router_weights(router_group_w, router_group_b, router_expert_w, router_expert_b):
    we = jnp.transpose(router_expert_w, (0, 2, 1)).reshape(MOE_N_EXPERTS, D_MODEL)
    pad = LANES - MOE_N_EXPERTS - MOE_GROUPS
    wr = jnp.concatenate([we, router_group_w.T, jnp.zeros((pad, D_MODEL), F32)], axis=0)
    br = jnp.concatenate([router_expert_b.reshape(-1), router_group_b, jnp.zeros((pad,), F32)])[:, None]
    return wr, br


def kernel(x, norm1_w, w_in, gla_fwd_gate_w, gla_fwd_gate_b, gla_bwd_gate_w, gla_bwd_gate_b,
           gla_norm_w, w_out, norm2_w, router_group_w, router_group_b, router_expert_w,
           router_expert_b, expert_w_gate, expert_w_up, expert_w_down, final_norm_w):
    B, S, D = x.shape
    T = B * S
    assert norm1_w.shape[0] == 1, "single-layer trunk: the final norm is fused into the combine step"
    h = x.reshape(T, D)
    gla_slab, loga, att_slab = _inproj(h, S, norm1_w[0], w_in[0], gla_fwd_gate_w[0], gla_fwd_gate_b[0],
                                       gla_bwd_gate_w[0], gla_bwd_gate_b[0])
    o_f, o_b = _gla(gla_slab, loga, B, S)
    att_out = _attention(att_slab.reshape(T, 3 * ATT_WIDTH), B, S)
    att_out = att_out.reshape(B, ATT_CLASSES, S // ATT_CLASSES, ATT_WIDTH)
    wr, br = _router_weights(router_group_w[0], router_group_b[0], router_expert_w[0], router_expert_b[0])
    h, u2, logits = _outproj(o_f, o_b, gla_slab, att_out, h, gla_norm_w[0], w_out[0], norm2_w[0], wr, br)
    info, counts = _route(logits)
    dest, pend = _plan(info, counts)
    xs = _dispatch(dest, pend, u2, _moe_capacity(T))
    ys = _experts(pend, xs, expert_w_gate[0], expert_w_up[0], expert_w_down[0])
    out = _combine(dest, ys, info, h, final_norm_w)
    return out.reshape(B, S, D)
```

```python
import functools

import jax
import jax.numpy as jnp
from jax import lax
from jax.experimental import pallas as pl
from jax.experimental.pallas import tpu as pltpu

F32 = jnp.float32
BF16 = jnp.bfloat16

D_MODEL = 1024
GLA_HEADS = 4
GLA_DV = 128
GLA_DK = 64
GLA_KEY_WIDTH = GLA_HEADS * GLA_DK
GLA_VAL_WIDTH = GLA_HEADS * GLA_DV
GLA_GATE_RANK = 16
GLA_TAU = 16.0
GLA_CHUNK = 64
ATT_WIDTH = 512
ATT_HEAD_DIM = 64
ATT_HEADS = 8
ROT_DIM = 16
ROPE_THETA = 500000.0
DILATED_PATTERNS = ((128, 1), (512, 4), (2048, 16))
ATT_RADIUS = 64
MOE_GROUPS = 4
MOE_EXPERTS_PER_GROUP = 8
MOE_N_EXPERTS = 32
MOE_TOP_K = 2
MOE_D_FF = 512
EPS = 1e-6
NEG_INF = -1e30
LOG2E = 1.4426950408889634

LANES = 128
MOE_ROWS = 256
VMEM_LIMIT = 56 * 1024 * 1024


def _cparams(sem):
    return pltpu.CompilerParams(dimension_semantics=sem, vmem_limit_bytes=VMEM_LIMIT)


def _dot(a, b):
    return jnp.dot(a, b, preferred_element_type=F32)


def _dot_nt(a, b):
    return lax.dot_general(a, b, (((1,), (1,)), ((), ())), preferred_element_type=F32)


def _dot_tn(a, b):
    return lax.dot_general(a, b, (((0,), (0,)), ((), ())), preferred_element_type=F32)


def _rms(x, w):
    return x * lax.rsqrt(jnp.mean(x * x, axis=-1, keepdims=True) + EPS) * w


def _inproj_kernel(x_ref, n1_ref, wg_ref, wlr_ref, wa_ref, gw_ref, gb_ref,
                   rc_ref, rs1_ref, rs2_ref, gla_ref, gate_ref, loga_ref, att_ref, stage_ref, wgb, wlrb):
    @pl.when(pl.program_id(0) == 0)
    def _():
        wgb[...] = wg_ref[...].astype(BF16)
        wlrb[...] = wlr_ref[...].astype(BF16)

    x = x_ref[...]
    ub = _rms(x, n1_ref[...]).astype(BF16)
    g = _dot(ub, wgb[...])
    qkv = 2 * GLA_KEY_WIDTH + GLA_VAL_WIDTH
    gla_ref[:, :GLA_KEY_WIDTH] = g[:, :GLA_KEY_WIDTH] * (GLA_DK ** -0.5)
    gla_ref[:, GLA_KEY_WIDTH:] = g[:, GLA_KEY_WIDTH:qkv]
    gate_ref[...] = g[:, qkv:].astype(BF16)
    lr = _dot(ub, wlrb[...])
    gate = _dot(lr.astype(BF16), gw_ref[...]) + gb_ref[...]
    loga_ref[...] = (jnp.minimum(gate, 0.0) - jnp.log(1.0 + jnp.exp(-jnp.abs(gate)))) * (1.0 / GLA_TAU)
    a = _dot(ub, wa_ref[...])
    qk = a[:, :2 * ATT_WIDTH]
    reps = 2 * ATT_WIDTH // LANES
    c = jnp.concatenate([rc_ref[...]] * reps, axis=1)
    s1 = jnp.concatenate([rs1_ref[...]] * reps, axis=1)
    s2 = jnp.concatenate([rs2_ref[...]] * reps, axis=1)
    half = ROT_DIM // 2
    n = 2 * ATT_WIDTH
    roped = qk * c + pltpu.roll(qk, n - half, 1) * s1 + pltpu.roll(qk, half, 1) * s2
    qkv = jnp.concatenate([roped[:, :ATT_WIDTH] * (ATT_HEAD_DIM ** -0.5 * LOG2E), roped[:, ATT_WIDTH:],
                           a[:, 2 * ATT_WIDTH:]], axis=1)
    rows = x.shape[0] // ATT_CLASSES
    for j in range(3 * ATT_WIDTH // LANES):
        cols = slice(j * LANES, (j + 1) * LANES)
        stage_ref[j] = qkv[:, cols]
        for c in range(ATT_CLASSES):
            att_ref[c, :, cols] = stage_ref[j, pl.ds(c, rows, stride=ATT_CLASSES), :]


def _rope_lane_tables(S):
    half = ROT_DIM // 2
    inv = ROPE_THETA ** (-(jnp.arange(0, ROT_DIM, 2, dtype=F32) / ROT_DIM))
    ang = inv[:, None] * jnp.arange(S, dtype=F32)[None, :]
    cos, sin = jnp.cos(ang), jnp.sin(ang)
    lane = jnp.arange(LANES) % ATT_HEAD_DIM
    freq = jnp.arange(half)[:, None]
    first = ((lane[None, :] == freq)).astype(F32)
    second = ((lane[None, :] == freq + half)).astype(F32)
    expand = lambda t, sel: lax.dot_general(t, sel, (((0,), (0,)), ((), ())), precision=lax.Precision.HIGHEST)
    rest = (lane >= ROT_DIM).astype(F32)[None, :]
    return expand(cos, first + second) + rest, expand(-sin, first), expand(sin, second)


def _inproj(x2, S, norm1_w, w_in, wf, bfw, wb, bbw, tm=512):
    T = x2.shape[0]
    o_lr = 2 * GLA_KEY_WIDTH + 2 * GLA_VAL_WIDTH
    o_att = o_lr + 2 * GLA_GATE_RANK
    wa = w_in[:, o_att:].astype(BF16)
    zeros = jnp.zeros((GLA_GATE_RANK, GLA_KEY_WIDTH), F32)
    gw = jnp.concatenate([jnp.concatenate([wf, zeros], axis=1), jnp.concatenate([zeros, wb], axis=1),
                          jnp.zeros((LANES - 2 * GLA_GATE_RANK, 2 * GLA_KEY_WIDTH), F32)], axis=0).astype(BF16)
    gb = jnp.concatenate([bfw, bbw])[None, :]
    rc, rs1, rs2 = _rope_lane_tables(S)
    nS = S // tm
    row = lambda i: (i, 0)
    const = lambda i: (0, 0)
    pos = lambda i: (i % nS, 0)
    return pl.pallas_call(
        _inproj_kernel,
        grid=(T // tm,),
        in_specs=[
            pl.BlockSpec((tm, D_MODEL), row),
            pl.BlockSpec((1, D_MODEL), const),
            pl.BlockSpec((D_MODEL, o_lr), const),
            pl.BlockSpec((D_MODEL, LANES), lambda i: (0, o_lr // LANES)),
            pl.BlockSpec((D_MODEL, 3 * ATT_WIDTH), const),
            pl.BlockSpec((LANES, 2 * GLA_KEY_WIDTH), const),
            pl.BlockSpec((1, 2 * GLA_KEY_WIDTH), const),
            pl.BlockSpec((tm, LANES), pos),
            pl.BlockSpec((tm, LANES), pos),
            pl.BlockSpec((tm, LANES), pos),
        ],
        out_specs=[
            pl.BlockSpec((tm, o_lr - GLA_VAL_WIDTH), row),
            pl.BlockSpec((tm, GLA_VAL_WIDTH), row),
            pl.BlockSpec((tm, 2 * GLA_KEY_WIDTH), row),
            pl.BlockSpec((None, ATT_CLASSES, tm // ATT_CLASSES, 3 * ATT_WIDTH),
                         lambda i: (i // nS, 0, i % nS, 0)),
        ],
        out_shape=[
            jax.ShapeDtypeStruct((T, o_lr - GLA_VAL_WIDTH), F32),
            jax.ShapeDtypeStruct((T, GLA_VAL_WIDTH), BF16),
            jax.ShapeDtypeStruct((T, 2 * GLA_KEY_WIDTH), F32),
            jax.ShapeDtypeStruct((T // S, ATT_CLASSES, S // ATT_CLASSES, 3 * ATT_WIDTH), F32),
        ],
        scratch_shapes=[pltpu.VMEM((3 * ATT_WIDTH // LANES, tm, LANES), F32),
                        pltpu.VMEM((D_MODEL, o_lr), BF16), pltpu.VMEM((D_MODEL, LANES), BF16)],
        compiler_params=_cparams(("arbitrary",)),
        name="inproj",
    )(x2, norm1_w[None, :], w_in, w_in, wa, gw, gb, rc, rs1, rs2)


def _gla_decays(q, k, v, la, forward, G):
    C = GLA_CHUNK
    R = G * C
    r = lax.broadcasted_iota(jnp.int32, (R, R), 0)
    c = lax.broadcasted_iota(jnp.int32, (R, R), 1)
    same = (r >> 6) == (c >> 6)
    tri = (c <= r) if forward else (c >= r)
    t_mat = jnp.where(same, jnp.where(tri, 1.0, 0.0), 0.0).astype(BF16)
    hi = la.astype(BF16)
    lo = (la - hi.astype(F32)).astype(BF16)
    b = _dot(t_mat, hi) + _dot(t_mat, lo)
    edge = C - 1 if forward else 0
    tot = jnp.concatenate([jnp.broadcast_to(b[g * C + edge:g * C + edge + 1], (C, GLA_KEY_WIDTH))
                           for g in range(G)], axis=0)
    order = list(range(G)) if forward else list(range(G - 1, -1, -1))
    return dict(q_dec=q * jnp.exp(b), k_inv=(k * jnp.exp(-b)).astype(BF16), k_end=k * jnp.exp(tot - b),
                tot=tot, vb=v.astype(BF16), order=order, forward=forward, G=G)


def _gla_scores(prep):
    C, H = GLA_CHUNK, GLA_HEADS
    lane_k = lax.broadcasted_iota(jnp.int32, (C, GLA_KEY_WIDTH), 1)
    qd_heads, scores = {}, {}
    for g in prep["order"]:
        rows = slice(g * C, (g + 1) * C)
        qd = prep["q_dec"][rows]
        qd_heads[g] = jnp.concatenate([jnp.where((lane_k >> 6) == h, qd, 0.0) for h in range(H)],
                                      axis=0).astype(BF16)
        scores[g] = _dot_nt(qd_heads[g], prep["k_inv"][rows])
    return qd_heads, scores


def _gla_chunk_updates(prep):
    C, H, G = GLA_CHUNK, GLA_HEADS, prep["G"]
    k_end, tot, vb = prep["k_end"], prep["tot"], prep["vb"]
    kv, dec_t = {}, {}
    lane = lax.broadcasted_iota(jnp.int32, (GLA_KEY_WIDTH, 2 * C), 1)
    zeros = jnp.zeros((C, GLA_DV), BF16)
    for p in range(G // 2):
        pair = slice(2 * p * C, (2 * p + 2) * C)
        ke_t = k_end[pair].T.astype(BF16)
        tot_t = tot[pair].T
        swapped = pltpu.roll(tot_t, C, 1)
        for half in range(2):
            g = 2 * p + half
            rows = slice(g * C, (g + 1) * C)
            own = (lane < C) if half == 0 else (lane >= C)
            dec_t[g] = jnp.exp(jnp.where(own, tot_t, swapped))
            parts = []
            for h in range(H):
                v_h = vb[rows, h * GLA_DV:(h + 1) * GLA_DV]
                v_pad = jnp.concatenate([v_h, zeros] if half == 0 else [zeros, v_h], axis=0)
                parts.append(_dot(ke_t[h * C:(h + 1) * C], v_pad))
            kv[g] = jnp.concatenate(parts, axis=0)
    return kv, dec_t


def _gla_states(prep, kv, dec_t, s_ref):
    st = s_ref[...]
    states = {}
    for g in prep["order"]:
        states[g] = st.astype(BF16)
        st = st * dec_t[g] + kv[g]
    s_ref[...] = st
    return states


def _gla_outputs(prep, qd_heads, scores, inter, o_ref):
    C, H = GLA_CHUNK, GLA_HEADS
    row_q = lax.broadcasted_iota(jnp.int32, (H * C, C), 0) & (C - 1)
    col_k = lax.broadcasted_iota(jnp.int32, (H * C, C), 1)
    a_mask = (col_k <= row_q) if prep["forward"] else (col_k >= row_q)
    for g in prep["order"]:
        rows = slice(g * C, (g + 1) * C)
        a = jnp.where(a_mask, scores[g], 0.0).astype(BF16)
        vv = prep["vb"][rows]
        o_ref[rows, :] = jnp.concatenate(
            [_dot(a[h * C:(h + 1) * C], vv[:, h * GLA_DV:(h + 1) * GLA_DV]) + inter[g][h * C:(h + 1) * C]
             for h in range(H)], axis=1).astype(o_ref.dtype)


def _gla_kernel(qf_ref, kf_ref, vf_ref, laf_ref, qb_ref, kb_ref, vb_ref, lab_ref,
                of_ref, ob_ref, sf_ref, sb_ref, *, G):
    @pl.when(pl.program_id(1) == 0)
    def _():
        sf_ref[...] = jnp.zeros_like(sf_ref)
        sb_ref[...] = jnp.zeros_like(sb_ref)

    dirs = [(_gla_decays(qf_ref[...], kf_ref[...], vf_ref[...], laf_ref[...], True, G), sf_ref, of_ref),
            (_gla_decays(qb_ref[...], kb_ref[...], vb_ref[...], lab_ref[...], False, G), sb_ref, ob_ref)]
    scored = [_gla_scores(prep) for prep, _, _ in dirs]
    updates = [_gla_chunk_updates(prep) for prep, _, _ in dirs]
    states = [_gla_states(prep, kv, dec_t, s_ref) for (prep, s_ref, _), (kv, dec_t) in zip(dirs, updates)]
    inters = [{g: _dot(qd_heads[g], st[g]) for g in prep["order"]}
              for (prep, _, _), (qd_heads, _), st in zip(dirs, scored, states)]
    for (prep, _, o_ref), (qd_heads, scores), inter in zip(dirs, scored, inters):
        _gla_outputs(prep, qd_heads, scores, inter, o_ref)


def _gla(gla_slab, loga, B, S, G=8):
    T = B * S
    R = G * GLA_CHUNK
    ns = S // R
    fwd = lambda col: (lambda b, i: (b * ns + i, col))
    bwd = lambda col: (lambda b, i: (b * ns + ns - 1 - i, col))
    kw, vw = GLA_KEY_WIDTH, GLA_VAL_WIDTH
    return pl.pallas_call(
        functools.partial(_gla_kernel, G=G),
        grid=(B, ns),
        in_specs=[
            pl.BlockSpec((R, kw), fwd(0)), pl.BlockSpec((R, kw), fwd(1)),
            pl.BlockSpec((R, vw), fwd(1)), pl.BlockSpec((R, kw), fwd(0)),
            pl.BlockSpec((R, kw), bwd(0)), pl.BlockSpec((R, kw), bwd(1)),
            pl.BlockSpec((R, vw), bwd(1)), pl.BlockSpec((R, kw), bwd(1)),
        ],
        out_specs=[pl.BlockSpec((R, vw), fwd(0)), pl.BlockSpec((R, vw), bwd(0))],
        out_shape=[jax.ShapeDtypeStruct((T, vw), BF16), jax.ShapeDtypeStruct((T, vw), BF16)],
        scratch_shapes=[pltpu.VMEM((kw, GLA_DV), F32), pltpu.VMEM((kw, GLA_DV), F32)],
        compiler_params=_cparams(("arbitrary", "arbitrary")),
        name="gla",
    )(gla_slab, gla_slab, gla_slab, loga, gla_slab, gla_slab, gla_slab, loga)


ATT_CLASSES = 4
ATT_QB = 128
ATT_KB = ATT_QB + 2 * ATT_RADIUS


ATT_UNROLL = 4


def _att_kernel(q_ref, k_ref, v_ref, o_ref, m_ref, l_ref, bias_ref, *, S):
    QB, KB, NC = ATT_QB, ATT_KB, ATT_CLASSES
    L4 = S // NC
    lane = lax.broadcasted_iota(jnp.int32, (QB, LANES), 1)
    head0 = lane < ATT_HEAD_DIM

    @pl.when((pl.program_id(0) == 0) & (pl.program_id(1) == 0))
    def _():
        rowi = lax.broadcasted_iota(jnp.int32, (2 * QB, KB), 0) & (QB - 1)
        coli = lax.broadcasted_iota(jnp.int32, (2 * QB, KB), 1)
        qpos = (rowi & (QB // NC - 1)) * NC + (rowi >> 5)
        kpos = (coli & (KB // NC - 1)) * NC + (coli >> 6)
        for case in range(3):
            bias_ref[0, case] = jnp.where(jnp.abs(rowi - coli + case * ATT_RADIUS) <= ATT_RADIUS, 0.0, NEG_INF)
            bias_ref[1, case] = jnp.where(jnp.abs(qpos - kpos + case * ATT_RADIUS) <= ATT_RADIUS, 0.0, NEG_INF)

    for pi, (_, d) in enumerate(DILATED_PATTERNS):
        L = S // d
        nb = L // QB
        shift = nb.bit_length() - 1
        first = pi == 0
        last = pi == len(DILATED_PATTERNS) - 1

        def scores(n, d=d, L=L, nb=nb, shift=shift):
            cls = n >> shift
            q0 = (n & (nb - 1)) * QB
            ws = jnp.clip(q0 - ATT_RADIUS, 0, L - KB)
            if d == 1:
                qsls = [pl.ds(pl.multiple_of(c * L4 + q0 // NC, QB // NC), QB // NC) for c in range(NC)]
                ksls = [pl.ds(pl.multiple_of(c * L4 + ws // NC, ATT_RADIUS // NC), KB // NC) for c in range(NC)]
            elif d == NC:
                qsls = [pl.ds(pl.multiple_of(cls * L4 + q0, QB), QB)]
                ksls = [pl.ds(pl.multiple_of(cls * L4 + ws, ATT_RADIUS), KB)]
            else:
                base = (cls & (NC - 1)) * L4 + (cls >> 2)
                qsls = [pl.ds(base + NC * q0, QB, stride=NC)]
                ksls = [pl.ds(base + NC * ws, KB, stride=NC)]
            q = jnp.concatenate([q_ref[sl, :] for sl in qsls], axis=0)
            kw = jnp.concatenate([k_ref[sl, :] for sl in ksls], axis=0)
            q2 = jnp.concatenate([jnp.where(head0, q, 0.0), jnp.where(head0, 0.0, q)], axis=0).astype(BF16)
            s = _dot_nt(q2, kw.astype(BF16))
            return qsls, ksls, s + bias_ref[1 if d == 1 else 0, (q0 - ws) >> 6]

        def softmax_pv(qsls, ksls, s):
            m_blk = jnp.max(s, axis=-1, keepdims=True)
            p = jnp.exp2(s - m_blk)
            vw = jnp.concatenate([v_ref[sl, :] for sl in ksls], axis=0)
            v_ones = jnp.concatenate([vw.astype(BF16), jnp.ones((KB, LANES), BF16)], axis=1)
            pv = _dot(p.astype(BF16), v_ones)
            acc_b = jnp.where(head0, pv[:QB, :LANES], pv[QB:, :LANES])
            m_b = jnp.where(head0, m_blk[:QB], m_blk[QB:])
            l_b = jnp.where(head0, pv[:QB, LANES:], pv[QB:, LANES:])
            return qsls, acc_b, m_b, l_b

        def load(ref, sls):
            return jnp.concatenate([ref[sl, :] for sl in sls], axis=0)

        def store(ref, sls, val):
            n = val.shape[0] // len(sls)
            for i, sl in enumerate(sls):
                ref[sl, :] = val[i * n:(i + 1) * n]

        def body(n, carry, first=first, last=last):
            staged = [scores(n * ATT_UNROLL + u) for u in range(ATT_UNROLL)]
            blocks = [softmax_pv(*st) for st in staged]
            for qsls, acc_b, m_b, l_b in blocks:
                if first:
                    acc, m_new, l_new = acc_b, m_b, l_b
                else:
                    m_old = load(m_ref, qsls)
                    m_new = jnp.maximum(m_old, m_b)
                    w_old = jnp.exp2(m_old - m_new)
                    w_blk = jnp.exp2(m_b - m_new)
                    acc = load(o_ref, qsls) * w_old + acc_b * w_blk
                    l_new = load(l_ref, qsls) * w_old + l_b * w_blk
                if last:
                    store(o_ref, qsls, acc / l_new)
                else:
                    store(o_ref, qsls, acc)
                    store(m_ref, qsls, m_new)
                    store(l_ref, qsls, l_new)
            return carry

        lax.fori_loop(0, S // (QB * ATT_UNROLL), body, 0)


def _attention(att_slab, B, S):
    T = B * S
    ncol = ATT_WIDTH // LANES
    return pl.pallas_call(
        functools.partial(_att_kernel, S=S),
        grid=(B, ncol),
        in_specs=[
            pl.BlockSpec((S, LANES), lambda b, h: (b, h)),
            pl.BlockSpec((S, LANES), lambda b, h: (b, ncol + h)),
            pl.BlockSpec((S, LANES), lambda b, h: (b, 2 * ncol + h)),
        ],
        out_specs=pl.BlockSpec((S, LANES), lambda b, h: (b, h)),
        out_shape=jax.ShapeDtypeStruct((T, ATT_WIDTH), F32),
        scratch_shapes=[pltpu.VMEM((S, LANES), F32), pltpu.VMEM((S, LANES), F32),
                        pltpu.VMEM((2, 3, 2 * ATT_QB, ATT_KB), F32)],
        compiler_params=_cparams(("arbitrary", "arbitrary")),
        name="dilated_attention",
    )(att_slab, att_slab, att_slab)


PACK_WORDS = D_MODEL // 2
ROW_TILE = PACK_WORDS // LANES
HIGH_HALF = -65536


def _pack_rows(x):
    bits = lambda v: lax.bitcast_convert_type(v.astype(BF16).astype(F32), jnp.int32)
    low = (bits(x[:, :PACK_WORDS]) >> 16) & 0xFFFF
    return (bits(x[:, PACK_WORDS:]) & HIGH_HALF) | low


def _unpack_rows(w):
    low = lax.bitcast_convert_type(w << 16, F32)
    high = lax.bitcast_convert_type(w & HIGH_HALF, F32)
    return jnp.concatenate([low, high], axis=1).astype(BF16)


def _to_row_tiles(ref, w):
    n = w.shape[0]
    for j in range(ROW_TILE):
        ref[pl.ds(j, n, stride=ROW_TILE), :] = w[:, j * LANES:(j + 1) * LANES]


def _from_row_tiles(ref, n):
    return jnp.concatenate([ref[pl.ds(j, n, stride=ROW_TILE), :] for j in range(ROW_TILE)], axis=1)


def _tile_copy(src_ref, src_row, dst_ref, dst_row, sem):
    src = pl.ds(pl.multiple_of(src_row * ROW_TILE, ROW_TILE), ROW_TILE)
    dst = pl.ds(pl.multiple_of(dst_row * ROW_TILE, ROW_TILE), ROW_TILE)
    return pltpu.make_async_copy(src_ref.at[src], dst_ref.at[dst], sem)


def _outproj_kernel(of_ref, ob_ref, gg_ref, att_ref, x_ref, gnw_ref, wo1_ref, wo2_ref,
                    n2_ref, wr_ref, br_ref, h_ref, u_ref, lg_ref, stage_ref):
    rows = stage_ref.shape[1] // ATT_CLASSES
    for j in range(ATT_WIDTH // LANES):
        for c in range(ATT_CLASSES):
            stage_ref[j, pl.ds(c, rows, stride=ATT_CLASSES), :] = att_ref[c, :, j * LANES:(j + 1) * LANES]
    att = jnp.concatenate([stage_ref[j] for j in range(ATT_WIDTH // LANES)], axis=1)
    o = of_ref[...].astype(F32) + ob_ref[...].astype(F32)
    gate = gg_ref[...].astype(F32)
    gnw = gnw_ref[...]
    parts = []
    for h in range(GLA_HEADS):
        sl = slice(h * GLA_DV, (h + 1) * GLA_DV)
        parts.append(_rms(o[:, sl], gnw))
    y = jnp.concatenate(parts, axis=1) * (gate / (1.0 + jnp.exp(-gate)))
    mix = _dot(y.astype(BF16), wo1_ref[...]) + _dot(att.astype(BF16), wo2_ref[...])
    h = x_ref[...] + mix
    h_ref[...] = h
    u = _rms(h, n2_ref[...])
    _to_row_tiles(u_ref, _pack_rows(u))
    u_hi = u.astype(BF16)
    u_lo = (u - u_hi.astype(F32)).astype(BF16)
    hi_both = _dot_nt(wr_ref[...], u_hi)
    lg_ref[...] = (hi_both[:LANES] + hi_both[LANES:] + _dot_nt(wr_ref[:LANES], u_lo)) + br_ref[...]


def _outproj(o_f, o_b, gate, att_out, x2, gla_norm_w, w_out, norm2_w, wr, br, tm=512):
    T = x2.shape[0]
    nS = att_out.shape[2] * ATT_CLASSES // tm
    row = lambda i: (i, 0)
    const = lambda i: (0, 0)
    wo = w_out.astype(BF16)
    wr_hi = wr.astype(BF16)
    wr_lo = (wr - wr_hi.astype(F32)).astype(BF16)
    wr = jnp.concatenate([wr_hi, wr_lo], axis=0)
    return pl.pallas_call(
        _outproj_kernel,
        grid=(T // tm,),
        in_specs=[
            pl.BlockSpec((tm, GLA_VAL_WIDTH), row),
            pl.BlockSpec((tm, GLA_VAL_WIDTH), row),
            pl.BlockSpec((tm, GLA_VAL_WIDTH), row),
            pl.BlockSpec((None, ATT_CLASSES, tm // ATT_CLASSES, ATT_WIDTH), lambda i: (i // nS, 0, i % nS, 0)),
            pl.BlockSpec((tm, D_MODEL), row),
            pl.BlockSpec((1, GLA_DV), const),
            pl.BlockSpec((GLA_VAL_WIDTH, D_MODEL), const),
            pl.BlockSpec((ATT_WIDTH, D_MODEL), const),
            pl.BlockSpec((1, D_MODEL), const),
            pl.BlockSpec((2 * LANES, D_MODEL), const),
            pl.BlockSpec((LANES, 1), const),
        ],
        out_specs=[
            pl.BlockSpec((tm, D_MODEL), row),
            pl.BlockSpec((tm * ROW_TILE, LANES), row),
            pl.BlockSpec((LANES, tm), lambda i: (0, i)),
        ],
        out_shape=[
            jax.ShapeDtypeStruct((T, D_MODEL), F32),
            jax.ShapeDtypeStruct((T * ROW_TILE, LANES), jnp.int32),
            jax.ShapeDtypeStruct((LANES, T), F32),
        ],
        scratch_shapes=[pltpu.VMEM((ATT_WIDTH // LANES, tm, LANES), F32)],
        compiler_params=_cparams(("arbitrary",)),
        name="outproj",
    )(o_f, o_b, gate, att_out, x2, gla_norm_w[None, :], wo[:GLA_VAL_WIDTH], wo[GLA_VAL_WIDTH:],
      norm2_w[None, :], wr, br)


INFO_E1, INFO_E2, INFO_R1, INFO_R2, INFO_W1, INFO_W2 = range(6)
ROUTE_ROWS = 40


def _route_kernel(lg_ref, info_ref, cnt_ref, carry_ref):
    @pl.when(pl.program_id(0) == 0)
    def _():
        carry_ref[...] = jnp.zeros_like(carry_ref)

    lg = lg_ref[:ROUTE_ROWS, :]
    tr = lg.shape[1]
    row = lax.broadcasted_iota(jnp.int32, (ROUTE_ROWS, tr), 0)
    big = jnp.int32(1 << 20)
    is_g = (row >= MOE_N_EXPERTS) & (row < MOE_N_EXPERTS + MOE_GROUPS)
    gl = jnp.where(is_g, lg, -jnp.inf)
    gmax = jnp.max(gl, axis=0, keepdims=True)
    gsel = jnp.min(jnp.where(gl == gmax, row - MOE_N_EXPERTS, big), axis=0, keepdims=True)
    g_w = 1.0 / jnp.sum(jnp.where(is_g, jnp.exp(lg - gmax), 0.0), axis=0, keepdims=True)
    in_grp = (row < MOE_N_EXPERTS) & ((row >> 3) == gsel)
    el = jnp.where(in_grp, lg, -jnp.inf)
    v1 = jnp.max(el, axis=0, keepdims=True)
    i1 = jnp.min(jnp.where(el == v1, row, big), axis=0, keepdims=True)
    el2 = jnp.where(row == i1, -jnp.inf, el)
    v2 = jnp.max(el2, axis=0, keepdims=True)
    i2 = jnp.min(jnp.where(el2 == v2, row, big), axis=0, keepdims=True)
    t = jnp.exp(v2 - v1)
    w1 = g_w * (1.0 / (1.0 + t))
    w2 = g_w * (t / (1.0 + t))

    erow = lax.broadcasted_iota(jnp.int32, (MOE_N_EXPERTS, tr), 0)
    hit1 = erow == i1
    hit2 = erow == i2
    member = jnp.where(hit1 | hit2, 1.0, 0.0)
    r = lax.broadcasted_iota(jnp.int32, (tr, tr), 0)
    c = lax.broadcasted_iota(jnp.int32, (tr, tr), 1)
    earlier = jnp.where(r < c, 1.0, 0.0).astype(BF16)
    carry = carry_ref[...]
    prefix = _dot(member.astype(BF16), earlier) + carry[:, 0:1]
    rank1 = jnp.sum(jnp.where(hit1, prefix, 0.0), axis=0, keepdims=True)
    rank2 = jnp.sum(jnp.where(hit2, prefix, 0.0), axis=0, keepdims=True)
    carry = carry + jnp.sum(member, axis=1, keepdims=True)
    carry_ref[...] = carry
    cnt_ref[...] = carry

    zero = jnp.zeros_like(w1)
    info_ref[...] = jnp.concatenate([i1.astype(F32), i2.astype(F32), rank1, rank2, w1, w2, zero, zero], axis=0)


def _route(logits_t, tr=512):
    T = logits_t.shape[1]
    return pl.pallas_call(
        _route_kernel,
        grid=(T // tr,),
        in_specs=[pl.BlockSpec((LANES, tr), lambda i: (0, i))],
        out_specs=[pl.BlockSpec((8, tr), lambda i: (0, i)),
                   pl.BlockSpec((MOE_N_EXPERTS, LANES), lambda i: (0, 0))],
        out_shape=[jax.ShapeDtypeStruct((8, T), F32), jax.ShapeDtypeStruct((MOE_N_EXPERTS, LANES), F32)],
        scratch_shapes=[pltpu.VMEM((MOE_N_EXPERTS, LANES), F32)],
        compiler_params=_cparams(("arbitrary",)),
        name="route",
    )(logits_t)


ROW_UNROLL = 8


def _dispatch_kernel(dest_ref, pend_ref, u_ref, xs_ref, zbuf, sem, zsem, *, td, T, nblk):
    @pl.when(pl.program_id(0) == 0)
    def _():
        zbuf[...] = jnp.zeros_like(zbuf)
        n_used = pend_ref[MOE_N_EXPERTS - 1] >> 8

        def zero_copy(blk):
            start = pl.multiple_of(blk * (MOE_ROWS * ROW_TILE), MOE_ROWS * ROW_TILE)
            return pltpu.make_async_copy(zbuf, xs_ref.at[pl.ds(start, MOE_ROWS * ROW_TILE)], zsem)

        def each_pad_block(fn):
            def per_expert(e, carry):
                prev = jnp.where(e > 0, pend_ref[jnp.maximum(e - 1, 0)], 0)

                @pl.when(pend_ref[e] > prev)
                def _():
                    fn((pend_ref[e] >> 8) - 1)
                return carry

            def per_tail(j, carry):
                @pl.when(n_used + j < nblk)
                def _():
                    fn(n_used + j)
                return carry

            lax.fori_loop(0, MOE_N_EXPERTS, per_expert, 0)
            lax.fori_loop(0, MOE_N_EXPERTS, per_tail, 0)

        each_pad_block(lambda blk: zero_copy(blk).start())
        each_pad_block(lambda blk: zero_copy(blk).wait())

    base = pl.program_id(0) * td

    def issue(g, carry):
        for j in range(ROW_UNROLL):
            r = g * ROW_UNROLL + j
            for k in range(MOE_TOP_K):
                _tile_copy(u_ref, r, xs_ref, dest_ref[k * T + base + r], sem).start(priority=k)
        return carry

    lax.fori_loop(0, td // ROW_UNROLL, issue, 0)
    for k in range(MOE_TOP_K):
        pltpu.make_async_copy(u_ref, xs_ref.at[pl.ds(0, td * ROW_TILE)], sem).wait()


def _dispatch(dest, pend, u2, cap, td=1024):
    T = u2.shape[0] // ROW_TILE
    return pl.pallas_call(
        functools.partial(_dispatch_kernel, td=td, T=T, nblk=cap // MOE_ROWS),
        grid_spec=pltpu.PrefetchScalarGridSpec(
            num_scalar_prefetch=2,
            grid=(T // td,),
            in_specs=[pl.BlockSpec((td * ROW_TILE, LANES), lambda i, d, z: (i, 0))],
            out_specs=pl.BlockSpec(memory_space=pl.ANY),
            scratch_shapes=[pltpu.VMEM((MOE_ROWS * ROW_TILE, LANES), jnp.int32),
                            pltpu.SemaphoreType.DMA(()), pltpu.SemaphoreType.DMA(())],
        ),
        out_shape=jax.ShapeDtypeStruct((cap * ROW_TILE, LANES), jnp.int32),
        compiler_params=_cparams(("arbitrary",)),
        name="dispatch",
    )(dest, pend, u2)


def _expert_kernel(pend_ref, xs_hbm, wg_hbm, wu_hbm, wd_hbm, ys_hbm,
                   xbuf, ybuf, zbuf, stage_g, stage_u, stage_d, wgb, wub, wdb, xsem, ysem, wsem, zsem, *, nblk):
    last = MOE_N_EXPERTS - 1
    n_used = pend_ref[last] >> 8
    block_rows = MOE_ROWS * ROW_TILE

    def x_copy(b, slot):
        start = pl.multiple_of(b * block_rows, block_rows)
        return pltpu.make_async_copy(xs_hbm.at[pl.ds(start, block_rows)], xbuf.at[slot], xsem.at[slot])

    def y_copy(b, slot):
        start = pl.multiple_of(b * block_rows, block_rows)
        return pltpu.make_async_copy(ybuf.at[slot], ys_hbm.at[pl.ds(start, block_rows)], ysem.at[slot])

    def zero_copy(b):
        start = pl.multiple_of(b * block_rows, block_rows)
        return pltpu.make_async_copy(zbuf, ys_hbm.at[pl.ds(start, block_rows)], zsem)

    def weight_copies(e):
        return (pltpu.make_async_copy(wg_hbm.at[e], stage_g, wsem.at[0]),
                pltpu.make_async_copy(wu_hbm.at[e], stage_u, wsem.at[1]),
                pltpu.make_async_copy(wd_hbm.at[e], stage_d, wsem.at[2]))

    def owner(start, row):
        return lax.while_loop(lambda e: (e < last) & (pend_ref[e] <= row), lambda e: e + 1, start)

    for c in weight_copies(owner(0, 0)):
        c.start()
    x_copy(0, 0).start()

    zbuf[...] = jnp.zeros_like(zbuf)

    def tail(fn):
        def step(b, carry):
            fn(b)
            return carry
        lax.fori_loop(n_used, nblk, step, 0)

    tail(lambda b: zero_copy(b).start())

    def body(b, cur):
        slot = b & 1
        e = owner(jnp.maximum(cur, 0), b * MOE_ROWS)
        x_copy(b, slot).wait()

        @pl.when(b + 1 < n_used)
        def _():
            x_copy(b + 1, 1 - slot).start()

        @pl.when(e != cur)
        def _():
            for c in weight_copies(e):
                c.wait()
            wgb[...] = stage_g[...].astype(BF16)
            wub[...] = stage_u[...].astype(BF16)
            wdb[...] = stage_d[...].astype(BF16)

            @pl.when(pend_ref[e] < pend_ref[last])
            def _():
                for c in weight_copies(owner(e + 1, pend_ref[e])):
                    c.start(priority=1)

        @pl.when(b >= 2)
        def _():
            y_copy(b - 2, slot).wait()

        xb = _unpack_rows(_from_row_tiles(xbuf.at[slot], MOE_ROWS))
        g = _dot(xb, wgb[...])
        u = _dot(xb, wub[...])
        hid = (g / (1.0 + jnp.exp(-g))) * u
        _to_row_tiles(ybuf.at[slot], _pack_rows(_dot(hid.astype(BF16), wdb[...])))
        y_copy(b, slot).start()
        return e

    lax.fori_loop(0, n_used, body, jnp.int32(-1))

    @pl.when(n_used >= 2)
    def _():
        y_copy(n_used - 2, n_used & 1).wait()
    y_copy(n_used - 1, (n_used - 1) & 1).wait()
    tail(lambda b: zero_copy(b).wait())


def _experts(pend, xs, w_gate, w_up, w_down):
    cap = xs.shape[0] // ROW_TILE
    nblk = cap // MOE_ROWS
    block = (MOE_ROWS * ROW_TILE, LANES)
    anywhere = pl.BlockSpec(memory_space=pl.ANY)
    return pl.pallas_call(
        functools.partial(_expert_kernel, nblk=nblk),
        grid_spec=pltpu.PrefetchScalarGridSpec(
            num_scalar_prefetch=1,
            grid=(1,),
            in_specs=[anywhere, anywhere, anywhere, anywhere],
            out_specs=anywhere,
            scratch_shapes=[pltpu.VMEM((2,) + block, jnp.int32),
                            pltpu.VMEM((2,) + block, jnp.int32),
                            pltpu.VMEM(block, jnp.int32),
                            pltpu.VMEM((D_MODEL, MOE_D_FF), F32),
                            pltpu.VMEM((D_MODEL, MOE_D_FF), F32),
                            pltpu.VMEM((MOE_D_FF, D_MODEL), F32),
                            pltpu.VMEM((D_MODEL, MOE_D_FF), BF16),
                            pltpu.VMEM((D_MODEL, MOE_D_FF), BF16),
                            pltpu.VMEM((MOE_D_FF, D_MODEL), BF16),
                            pltpu.SemaphoreType.DMA((2,)),
                            pltpu.SemaphoreType.DMA((2,)),
                            pltpu.SemaphoreType.DMA((3,)),
                            pltpu.SemaphoreType.DMA(())],
        ),
        out_shape=jax.ShapeDtypeStruct((cap * ROW_TILE, LANES), jnp.int32),
        compiler_params=_cparams(("arbitrary",)),
        name="experts",
    )(pend, xs, w_gate, w_up, w_down)


def _combine_kernel(dest_ref, ys_ref, info_ref, h_ref, fw_ref, o_ref, buf, sem, *, tc, T):
    i = pl.program_id(0)
    n = pl.num_programs(0)

    def issue(step, slot):
        base = step * tc

        def body(g, carry):
            for j in range(ROW_UNROLL):
                r = g * ROW_UNROLL + j
                for k in range(MOE_TOP_K):
                    _tile_copy(ys_ref, dest_ref[k * T + base + r], buf.at[slot, k], r,
                               sem.at[slot]).start(priority=k)
            return carry

        lax.fori_loop(0, tc // ROW_UNROLL, body, 0)

    @pl.when(i == 0)
    def _():
        issue(0, 0)

    slot = i % 2

    @pl.when(i + 1 < n)
    def _():
        issue(i + 1, 1 - slot)

    for k in range(MOE_TOP_K):
        pltpu.make_async_copy(ys_ref.at[pl.ds(0, tc * ROW_TILE)], buf.at[slot, k], sem.at[slot]).wait()

    info_t = jnp.concatenate([info_ref[...]] * (LANES // 8), axis=0).T
    w1 = info_t[:, INFO_W1:INFO_W1 + 1]
    w2 = info_t[:, INFO_W2:INFO_W2 + 1]
    y1 = _unpack_rows(_from_row_tiles(buf.at[slot, 0], tc)).astype(F32)
    y2 = _unpack_rows(_from_row_tiles(buf.at[slot, 1], tc)).astype(F32)
    h = h_ref[...] + (y1 * w1 + y2 * w2)
    o_ref[...] = _rms(h, fw_ref[...])


def _combine(dest, ys, info, h, final_w, tc=512):
    T = h.shape[0]
    return pl.pallas_call(
        functools.partial(_combine_kernel, tc=tc, T=T),
        grid_spec=pltpu.PrefetchScalarGridSpec(
            num_scalar_prefetch=1,
            grid=(T // tc,),
            in_specs=[pl.BlockSpec(memory_space=pl.ANY),
                      pl.BlockSpec((8, tc), lambda i, d: (0, i)),
                      pl.BlockSpec((tc, D_MODEL), lambda i, d: (i, 0)),
                      pl.BlockSpec((1, D_MODEL), lambda i, d: (0, 0))],
            out_specs=pl.BlockSpec((tc, D_MODEL), lambda i, d: (i, 0)),
            scratch_shapes=[pltpu.VMEM((2, MOE_TOP_K, tc * ROW_TILE, LANES), jnp.int32),
                            pltpu.SemaphoreType.DMA((2,))],
        ),
        out_shape=jax.ShapeDtypeStruct((T, D_MODEL), F32),
        compiler_params=_cparams(("arbitrary",)),
        name="combine",
    )(dest, ys, info, h, final_w[None, :])


def _plan_kernel(info_ref, cnt_ref, dest_ref, pend_ref):
    cnt = cnt_ref[...].astype(jnp.int32)
    nblk_e = ((cnt + (MOE_ROWS - 1)) >> 8).astype(F32)
    r = lax.broadcasted_iota(jnp.int32, (MOE_N_EXPERTS, MOE_N_EXPERTS), 0)
    c = lax.broadcasted_iota(jnp.int32, (MOE_N_EXPERTS, MOE_N_EXPERTS), 1)
    before = jnp.where(c < r, 1.0, 0.0).astype(BF16)
    first_blk = _dot(before, nblk_e.astype(BF16))
    pstart = first_blk[:, 0:1] * float(MOE_ROWS)
    pend_ref[...] = ((first_blk + nblk_e) * float(MOE_ROWS)).astype(jnp.int32)

    info = info_ref[...]
    erow = lax.broadcasted_iota(jnp.int32, (MOE_N_EXPERTS, info.shape[1]), 0)
    start_of = lambda e: jnp.sum(jnp.where(erow == e.astype(jnp.int32), pstart, 0.0), axis=0, keepdims=True)
    d1 = info[INFO_R1:INFO_R1 + 1] + start_of(info[INFO_E1:INFO_E1 + 1])
    d2 = info[INFO_R2:INFO_R2 + 1] + start_of(info[INFO_E2:INFO_E2 + 1])
    zero = jnp.zeros_like(d1)
    dest_ref[...] = jnp.concatenate([d1, d2] + [zero] * 6, axis=0).astype(jnp.int32)


def _plan(info, counts, tr=2048):
    T = info.shape[1]
    dest8, pend = pl.pallas_call(
        _plan_kernel,
        grid=(T // tr,),
        in_specs=[pl.BlockSpec((8, tr), lambda i: (0, i)),
                  pl.BlockSpec((MOE_N_EXPERTS, LANES), lambda i: (0, 0))],
        out_specs=[pl.BlockSpec((8, tr), lambda i: (0, i)),
                   pl.BlockSpec((MOE_N_EXPERTS, LANES), lambda i: (0, 0))],
        out_shape=[jax.ShapeDtypeStruct((8, T), jnp.int32),
                   jax.ShapeDtypeStruct((MOE_N_EXPERTS, LANES), jnp.int32)],
        compiler_params=_cparams(("arbitrary",)),
        name="plan",
    )(info, counts)
    return dest8[:MOE_TOP_K].reshape(-1), pend[:, 0]


def _moe_capacity(T):
    return (-(-(T * MOE_TOP_K) // MOE_ROWS) + MOE_N_EXPERTS) * MOE_ROWS


def _router_weights(router_group_w, router_group_b, router_expert_w, router_expert_b):
    we = jnp.transpose(router_expert_w, (0, 2, 1)).reshape(MOE_N_EXPERTS, D_MODEL)
    pad = LANES - MOE_N_EXPERTS - MOE_GROUPS
    wr = jnp.concatenate([we, router_group_w.T, jnp.zeros((pad, D_MODEL), F32)], axis=0)
    br = jnp.concatenate([router_expert_b.reshape(-1), router_group_b, jnp.zeros((pad,), F32)])[:, None]
    return wr, br


def kernel(x, norm1_w, w_in, gla_fwd_gate_w, gla_fwd_gate_b, gla_bwd_gate_w, gla_bwd_gate_b,
           gla_norm_w, w_out, norm2_w, router_group_w, router_group_b, router_expert_w,
           router_expert_b, expert_w_gate, expert_w_up, expert_w_down, final_norm_w):
    B, S, D = x.shape
    T = B * S
    assert norm1_w.shape[0] == 1, "single-layer trunk: the final norm is fused into the combine step"
    h = x.reshape(T, D)
    gla_slab, gate, loga, att_slab = _inproj(h, S, norm1_w[0], w_in[0], gla_fwd_gate_w[0], gla_fwd_gate_b[0],
                                       gla_bwd_gate_w[0], gla_bwd_gate_b[0])
    o_f, o_b = _gla(gla_slab, loga, B, S)
    att_out = _attention(att_slab.reshape(T, 3 * ATT_WIDTH), B, S)
    att_out = att_out.reshape(B, ATT_CLASSES, S // ATT_CLASSES, ATT_WIDTH)
    wr, br = _router_weights(router_group_w[0], router_group_b[0], router_expert_w[0], router_expert_b[0])
    h, u2, logits = _outproj(o_f, o_b, gate, att_out, h, gla_norm_w[0], w_out[0], norm2_w[0], wr, br)
    info, counts = _route(logits)
    dest, pend = _plan(info, counts)
    xs = _dispatch(dest, pend, u2, _moe_capacity(T))
    ys = _experts(pend, xs, expert_w_gate[0], expert_w_up[0], expert_w_down[0])
    out = _combine(dest, ys, info, h, final_norm_w)
    return out.reshape(B, S, D)
```

```python
import functools

import jax
import jax.numpy as jnp
import numpy as np
from jax import lax
from jax.experimental import pallas as pl
from jax.experimental.pallas import tpu as pltpu

F32 = jnp.float32
BF16 = jnp.bfloat16

D_MODEL = 1024
GLA_HEADS = 4
GLA_DV = 128
GLA_DK = 64
GLA_KEY_WIDTH = GLA_HEADS * GLA_DK
GLA_VAL_WIDTH = GLA_HEADS * GLA_DV
GLA_GATE_RANK = 16
GLA_TAU = 16.0
GLA_CHUNK = 64
ATT_WIDTH = 512
ATT_HEAD_DIM = 64
ATT_HEADS = 8
ROT_DIM = 16
ROPE_THETA = 500000.0
DILATED_PATTERNS = ((128, 1), (512, 4), (2048, 16))
ATT_RADIUS = 64
MOE_GROUPS = 4
MOE_EXPERTS_PER_GROUP = 8
MOE_N_EXPERTS = 32
MOE_TOP_K = 2
MOE_D_FF = 512
EPS = 1e-6
NEG_INF = -1e30
LOG2E = 1.4426950408889634

LANES = 128
MOE_ROWS = 256
VMEM_LIMIT = 56 * 1024 * 1024


def _cparams(sem):
    return pltpu.CompilerParams(dimension_semantics=sem, vmem_limit_bytes=VMEM_LIMIT)


def _dot(a, b):
    return jnp.dot(a, b, preferred_element_type=F32)


def _dot_nt(a, b):
    return lax.dot_general(a, b, (((1,), (1,)), ((), ())), preferred_element_type=F32)


def _dot_tn(a, b):
    return lax.dot_general(a, b, (((0,), (0,)), ((), ())), preferred_element_type=F32)


def _rms(x, w):
    return x * lax.rsqrt(jnp.mean(x * x, axis=-1, keepdims=True) + EPS) * w


def _inproj_kernel(x_ref, n1_ref, wg_ref, wlr_ref, wa_ref, gw_ref, gb_ref,
                   rc_ref, rs1_ref, rs2_ref, gla_ref, gate_ref, loga_ref, att_ref, stage_ref, wgb, wlrb):
    @pl.when(pl.program_id(0) == 0)
    def _():
        wgb[...] = wg_ref[...].astype(BF16)
        wlrb[...] = wlr_ref[...].astype(BF16)

    x = x_ref[...]
    ub = _rms(x, n1_ref[...]).astype(BF16)
    g = _dot(ub, wgb[...])
    qkv = 2 * GLA_KEY_WIDTH + GLA_VAL_WIDTH
    gla_ref[:, :GLA_KEY_WIDTH] = g[:, :GLA_KEY_WIDTH] * (GLA_DK ** -0.5)
    gla_ref[:, GLA_KEY_WIDTH:] = g[:, GLA_KEY_WIDTH:qkv]
    gate_ref[...] = g[:, qkv:].astype(BF16)
    lr = _dot(ub, wlrb[...])
    gate = _dot(lr.astype(BF16), gw_ref[...]) + gb_ref[...]
    loga_ref[...] = (jnp.minimum(gate, 0.0) - jnp.log(1.0 + jnp.exp(-jnp.abs(gate)))) * (1.0 / GLA_TAU)
    a = _dot(ub, wa_ref[...])
    qk = a[:, :2 * ATT_WIDTH]
    reps = 2 * ATT_WIDTH // LANES
    c = jnp.concatenate([rc_ref[...]] * reps, axis=1)
    s1 = jnp.concatenate([rs1_ref[...]] * reps, axis=1)
    s2 = jnp.concatenate([rs2_ref[...]] * reps, axis=1)
    half = ROT_DIM // 2
    n = 2 * ATT_WIDTH
    roped = qk * c + pltpu.roll(qk, n - half, 1) * s1 + pltpu.roll(qk, half, 1) * s2
    qkv = jnp.concatenate([roped[:, :ATT_WIDTH] * (ATT_HEAD_DIM ** -0.5 * LOG2E), roped[:, ATT_WIDTH:],
                           a[:, 2 * ATT_WIDTH:]], axis=1)
    rows = x.shape[0] // ATT_CLASSES
    for j in range(3 * ATT_WIDTH // LANES):
        cols = slice(j * LANES, (j + 1) * LANES)
        stage_ref[j] = qkv[:, cols]
        for c in range(ATT_CLASSES):
            att_ref[c, :, cols] = stage_ref[j, pl.ds(c, rows, stride=ATT_CLASSES), :]


def _rope_lane_tables(S):
    half = ROT_DIM // 2
    inv = np.float32(ROPE_THETA) ** (-(np.arange(0, ROT_DIM, 2, dtype=np.float32) / np.float32(ROT_DIM)))
    ang = np.arange(S, dtype=np.float32)[:, None] * inv[None, :].astype(np.float32)
    cos, sin = np.cos(ang), np.sin(ang)
    ones = np.ones((S, ATT_HEAD_DIM - ROT_DIM), np.float32)
    zeros = np.zeros((S, ATT_HEAD_DIM - ROT_DIM), np.float32)
    zeros8 = np.zeros((S, half), np.float32)
    rep = LANES // ATT_HEAD_DIM
    c = np.tile(np.concatenate([cos, cos, ones], axis=1), (1, rep))
    s1 = np.tile(np.concatenate([-sin, zeros8, zeros], axis=1), (1, rep))
    s2 = np.tile(np.concatenate([zeros8, sin, zeros], axis=1), (1, rep))
    return jnp.asarray(c), jnp.asarray(s1), jnp.asarray(s2)


def _inproj(x2, S, norm1_w, w_in, wf, bfw, wb, bbw, tm=512):
    T = x2.shape[0]
    o_lr = 2 * GLA_KEY_WIDTH + 2 * GLA_VAL_WIDTH
    o_att = o_lr + 2 * GLA_GATE_RANK
    wa = w_in[:, o_att:].astype(BF16)
    zeros = jnp.zeros((GLA_GATE_RANK, GLA_KEY_WIDTH), F32)
    gw = jnp.concatenate([jnp.concatenate([wf, zeros], axis=1), jnp.concatenate([zeros, wb], axis=1),
                          jnp.zeros((LANES - 2 * GLA_GATE_RANK, 2 * GLA_KEY_WIDTH), F32)], axis=0).astype(BF16)
    gb = jnp.concatenate([bfw, bbw])[None, :]
    rc, rs1, rs2 = _rope_lane_tables(S)
    nS = S // tm
    row = lambda i: (i, 0)
    const = lambda i: (0, 0)
    pos = lambda i: (i % nS, 0)
    return pl.pallas_call(
        _inproj_kernel,
        grid=(T // tm,),
        in_specs=[
            pl.BlockSpec((tm, D_MODEL), row),
            pl.BlockSpec((1, D_MODEL), const),
            pl.BlockSpec((D_MODEL, o_lr), const),
            pl.BlockSpec((D_MODEL, LANES), lambda i: (0, o_lr // LANES)),
            pl.BlockSpec((D_MODEL, 3 * ATT_WIDTH), const),
            pl.BlockSpec((LANES, 2 * GLA_KEY_WIDTH), const),
            pl.BlockSpec((1, 2 * GLA_KEY_WIDTH), const),
            pl.BlockSpec((tm, LANES), pos),
            pl.BlockSpec((tm, LANES), pos),
            pl.BlockSpec((tm, LANES), pos),
        ],
        out_specs=[
            pl.BlockSpec((tm, o_lr - GLA_VAL_WIDTH), row),
            pl.BlockSpec((tm, GLA_VAL_WIDTH), row),
            pl.BlockSpec((tm, 2 * GLA_KEY_WIDTH), row),
            pl.BlockSpec((None, ATT_CLASSES, tm // ATT_CLASSES, 3 * ATT_WIDTH),
                         lambda i: (i // nS, 0, i % nS, 0)),
        ],
        out_shape=[
            jax.ShapeDtypeStruct((T, o_lr - GLA_VAL_WIDTH), F32),
            jax.ShapeDtypeStruct((T, GLA_VAL_WIDTH), BF16),
            jax.ShapeDtypeStruct((T, 2 * GLA_KEY_WIDTH), F32),
            jax.ShapeDtypeStruct((T // S, ATT_CLASSES, S // ATT_CLASSES, 3 * ATT_WIDTH), F32),
        ],
        scratch_shapes=[pltpu.VMEM((3 * ATT_WIDTH // LANES, tm, LANES), F32),
                        pltpu.VMEM((D_MODEL, o_lr), BF16), pltpu.VMEM((D_MODEL, LANES), BF16)],
        compiler_params=_cparams(("arbitrary",)),
        name="inproj",
    )(x2, norm1_w[None, :], w_in, w_in, wa, gw, gb, rc, rs1, rs2)


def _gla_decays(q, k, v, la, forward, G):
    C = GLA_CHUNK
    R = G * C
    r = lax.broadcasted_iota(jnp.int32, (R, R), 0)
    c = lax.broadcasted_iota(jnp.int32, (R, R), 1)
    same = (r >> 6) == (c >> 6)
    tri = (c <= r) if forward else (c >= r)
    t_mat = jnp.where(same, jnp.where(tri, 1.0, 0.0), 0.0).astype(BF16)
    hi = la.astype(BF16)
    lo = (la - hi.astype(F32)).astype(BF16)
    b = _dot(t_mat, hi) + _dot(t_mat, lo)
    edge = C - 1 if forward else 0
    tot = jnp.concatenate([jnp.broadcast_to(b[g * C + edge:g * C + edge + 1], (C, GLA_KEY_WIDTH))
                           for g in range(G)], axis=0)
    order = list(range(G)) if forward else list(range(G - 1, -1, -1))
    return dict(q_dec=q * jnp.exp(b), k_inv=(k * jnp.exp(-b)).astype(BF16), k_end=k * jnp.exp(tot - b),
                tot=tot, vb=v.astype(BF16), order=order, forward=forward, G=G)


def _gla_scores(prep):
    C, H = GLA_CHUNK, GLA_HEADS
    lane_k = lax.broadcasted_iota(jnp.int32, (C, GLA_KEY_WIDTH), 1)
    qd_heads, scores = {}, {}
    for g in prep["order"]:
        rows = slice(g * C, (g + 1) * C)
        qd = prep["q_dec"][rows]
        qd_heads[g] = jnp.concatenate([jnp.where((lane_k >> 6) == h, qd, 0.0) for h in range(H)],
                                      axis=0).astype(BF16)
        scores[g] = _dot_nt(qd_heads[g], prep["k_inv"][rows])
    return qd_heads, scores


def _gla_chunk_updates(prep):
    C, H, G = GLA_CHUNK, GLA_HEADS, prep["G"]
    k_end, tot, vb = prep["k_end"], prep["tot"], prep["vb"]
    kv, dec_t = {}, {}
    lane = lax.broadcasted_iota(jnp.int32, (GLA_KEY_WIDTH, 2 * C), 1)
    zeros = jnp.zeros((C, GLA_DV), BF16)
    for p in range(G // 2):
        pair = slice(2 * p * C, (2 * p + 2) * C)
        ke_t = k_end[pair].T.astype(BF16)
        tot_t = tot[pair].T
        swapped = pltpu.roll(tot_t, C, 1)
        for half in range(2):
            g = 2 * p + half
            rows = slice(g * C, (g + 1) * C)
            own = (lane < C) if half == 0 else (lane >= C)
            dec_t[g] = jnp.exp(jnp.where(own, tot_t, swapped))
            parts = []
            for h in range(H):
                v_h = vb[rows, h * GLA_DV:(h + 1) * GLA_DV]
                v_pad = jnp.concatenate([v_h, zeros] if half == 0 else [zeros, v_h], axis=0)
                parts.append(_dot(ke_t[h * C:(h + 1) * C], v_pad))
            kv[g] = jnp.concatenate(parts, axis=0)
    return kv, dec_t


def _gla_states(prep, kv, dec_t, s_ref):
    st = s_ref[...]
    states = {}
    for g in prep["order"]:
        states[g] = st.astype(BF16)
        st = st * dec_t[g] + kv[g]
    s_ref[...] = st
    return states


def _gla_outputs(prep, qd_heads, scores, inter, o_ref):
    C, H = GLA_CHUNK, GLA_HEADS
    row_q = lax.broadcasted_iota(jnp.int32, (H * C, C), 0) & (C - 1)
    col_k = lax.broadcasted_iota(jnp.int32, (H * C, C), 1)
    a_mask = (col_k <= row_q) if prep["forward"] else (col_k >= row_q)
    for g in prep["order"]:
        rows = slice(g * C, (g + 1) * C)
        a = jnp.where(a_mask, scores[g], 0.0).astype(BF16)
        vv = prep["vb"][rows]
        o_ref[rows, :] = jnp.concatenate(
            [_dot(a[h * C:(h + 1) * C], vv[:, h * GLA_DV:(h + 1) * GLA_DV]) + inter[g][h * C:(h + 1) * C]
             for h in range(H)], axis=1).astype(o_ref.dtype)


def _gla_kernel(qf_ref, kf_ref, vf_ref, laf_ref, qb_ref, kb_ref, vb_ref, lab_ref,
                of_ref, ob_ref, sf_ref, sb_ref, *, G):
    @pl.when(pl.program_id(1) == 0)
    def _():
        sf_ref[...] = jnp.zeros_like(sf_ref)
        sb_ref[...] = jnp.zeros_like(sb_ref)

    dirs = [(_gla_decays(qf_ref[...], kf_ref[...], vf_ref[...], laf_ref[...], True, G), sf_ref, of_ref),
            (_gla_decays(qb_ref[...], kb_ref[...], vb_ref[...], lab_ref[...], False, G), sb_ref, ob_ref)]
    scored = [_gla_scores(prep) for prep, _, _ in dirs]
    updates = [_gla_chunk_updates(prep) for prep, _, _ in dirs]
    states = [_gla_states(prep, kv, dec_t, s_ref) for (prep, s_ref, _), (kv, dec_t) in zip(dirs, updates)]
    inters = [{g: _dot(qd_heads[g], st[g]) for g in prep["order"]}
              for (prep, _, _), (qd_heads, _), st in zip(dirs, scored, states)]
    for (prep, _, o_ref), (qd_heads, scores), inter in zip(dirs, scored, inters):
        _gla_outputs(prep, qd_heads, scores, inter, o_ref)


def _gla(gla_slab, loga, B, S, G=8):
    T = B * S
    R = G * GLA_CHUNK
    ns = S // R
    fwd = lambda col: (lambda b, i: (b * ns + i, col))
    bwd = lambda col: (lambda b, i: (b * ns + ns - 1 - i, col))
    kw, vw = GLA_KEY_WIDTH, GLA_VAL_WIDTH
    return pl.pallas_call(
        functools.partial(_gla_kernel, G=G),
        grid=(B, ns),
        in_specs=[
            pl.BlockSpec((R, kw), fwd(0)), pl.BlockSpec((R, kw), fwd(1)),
            pl.BlockSpec((R, vw), fwd(1)), pl.BlockSpec((R, kw), fwd(0)),
            pl.BlockSpec((R, kw), bwd(0)), pl.BlockSpec((R, kw), bwd(1)),
            pl.BlockSpec((R, vw), bwd(1)), pl.BlockSpec((R, kw), bwd(1)),
        ],
        out_specs=[pl.BlockSpec((R, vw), fwd(0)), pl.BlockSpec((R, vw), bwd(0))],
        out_shape=[jax.ShapeDtypeStruct((T, vw), BF16), jax.ShapeDtypeStruct((T, vw), BF16)],
        scratch_shapes=[pltpu.VMEM((kw, GLA_DV), F32), pltpu.VMEM((kw, GLA_DV), F32)],
        compiler_params=_cparams(("arbitrary", "arbitrary")),
        name="gla",
    )(gla_slab, gla_slab, gla_slab, loga, gla_slab, gla_slab, gla_slab, loga)


ATT_CLASSES = 4
ATT_QB = 128
ATT_KB = ATT_QB + 2 * ATT_RADIUS


ATT_UNROLL = 4


def _att_kernel(q_ref, k_ref, v_ref, o_ref, m_ref, l_ref, bias_ref, *, S):
    QB, KB, NC = ATT_QB, ATT_KB, ATT_CLASSES
    L4 = S // NC
    lane = lax.broadcasted_iota(jnp.int32, (QB, LANES), 1)
    head0 = lane < ATT_HEAD_DIM

    @pl.when((pl.program_id(0) == 0) & (pl.program_id(1) == 0))
    def _():
        rowi = lax.broadcasted_iota(jnp.int32, (2 * QB, KB), 0) & (QB - 1)
        coli = lax.broadcasted_iota(jnp.int32, (2 * QB, KB), 1)
        qpos = (rowi & (QB // NC - 1)) * NC + (rowi >> 5)
        kpos = (coli & (KB // NC - 1)) * NC + (coli >> 6)
        for case in range(3):
            bias_ref[0, case] = jnp.where(jnp.abs(rowi - coli + case * ATT_RADIUS) <= ATT_RADIUS, 0.0, NEG_INF)
            bias_ref[1, case] = jnp.where(jnp.abs(qpos - kpos + case * ATT_RADIUS) <= ATT_RADIUS, 0.0, NEG_INF)

    for pi, (_, d) in enumerate(DILATED_PATTERNS):
        L = S // d
        nb = L // QB
        shift = nb.bit_length() - 1
        first = pi == 0
        last = pi == len(DILATED_PATTERNS) - 1

        def scores(n, d=d, L=L, nb=nb, shift=shift):
            cls = n >> shift
            q0 = (n & (nb - 1)) * QB
            ws = jnp.clip(q0 - ATT_RADIUS, 0, L - KB)
            if d == 1:
                qsls = [pl.ds(pl.multiple_of(c * L4 + q0 // NC, QB // NC), QB // NC) for c in range(NC)]
                ksls = [pl.ds(pl.multiple_of(c * L4 + ws // NC, ATT_RADIUS // NC), KB // NC) for c in range(NC)]
            elif d == NC:
                qsls = [pl.ds(pl.multiple_of(cls * L4 + q0, QB), QB)]
                ksls = [pl.ds(pl.multiple_of(cls * L4 + ws, ATT_RADIUS), KB)]
            else:
                base = (cls & (NC - 1)) * L4 + (cls >> 2)
                qsls = [pl.ds(base + NC * q0, QB, stride=NC)]
                ksls = [pl.ds(base + NC * ws, KB, stride=NC)]
            q = jnp.concatenate([q_ref[sl, :] for sl in qsls], axis=0)
            kw = jnp.concatenate([k_ref[sl, :] for sl in ksls], axis=0)
            q2 = jnp.concatenate([jnp.where(head0, q, 0.0), jnp.where(head0, 0.0, q)], axis=0).astype(BF16)
            s = _dot_nt(q2, kw.astype(BF16))
            return qsls, ksls, s + bias_ref[1 if d == 1 else 0, (q0 - ws) >> 6]

        def softmax_pv(qsls, ksls, s):
            m_blk = jnp.max(s, axis=-1, keepdims=True)
            p = jnp.exp2(s - m_blk)
            vw = jnp.concatenate([v_ref[sl, :] for sl in ksls], axis=0)
            v_ones = jnp.concatenate([vw.astype(BF16), jnp.ones((KB, LANES), BF16)], axis=1)
            pv = _dot(p.astype(BF16), v_ones)
            acc_b = jnp.where(head0, pv[:QB, :LANES], pv[QB:, :LANES])
            m_b = jnp.where(head0, m_blk[:QB], m_blk[QB:])
            l_b = jnp.where(head0, pv[:QB, LANES:], pv[QB:, LANES:])
            return qsls, acc_b, m_b, l_b

        def load(ref, sls):
            return jnp.concatenate([ref[sl, :] for sl in sls], axis=0)

        def store(ref, sls, val):
            n = val.shape[0] // len(sls)
            for i, sl in enumerate(sls):
                ref[sl, :] = val[i * n:(i + 1) * n]

        def body(n, carry, first=first, last=last):
            staged = [scores(n * ATT_UNROLL + u) for u in range(ATT_UNROLL)]
            blocks = [softmax_pv(*st) for st in staged]
            for qsls, acc_b, m_b, l_b in blocks:
                if first:
                    acc, m_new, l_new = acc_b, m_b, l_b
                else:
                    m_old = load(m_ref, qsls)
                    m_new = jnp.maximum(m_old, m_b)
                    w_old = jnp.exp2(m_old - m_new)
                    w_blk = jnp.exp2(m_b - m_new)
                    acc = load(o_ref, qsls) * w_old + acc_b * w_blk
                    l_new = load(l_ref, qsls) * w_old + l_b * w_blk
                if last:
                    store(o_ref, qsls, acc / l_new)
                else:
                    store(o_ref, qsls, acc)
                    store(m_ref, qsls, m_new)
                    store(l_ref, qsls, l_new)
            return carry

        lax.fori_loop(0, S // (QB * ATT_UNROLL), body, 0)


def _attention(att_slab, B, S):
    T = B * S
    ncol = ATT_WIDTH // LANES
    return pl.pallas_call(
        functools.partial(_att_kernel, S=S),
        grid=(B, ncol),
        in_specs=[
            pl.BlockSpec((S, LANES), lambda b, h: (b, h)),
            pl.BlockSpec((S, LANES), lambda b, h: (b, ncol + h)),
            pl.BlockSpec((S, LANES), lambda b, h: (b, 2 * ncol + h)),
        ],
        out_specs=pl.BlockSpec((S, LANES), lambda b, h: (b, h)),
        out_shape=jax.ShapeDtypeStruct((T, ATT_WIDTH), F32),
        scratch_shapes=[pltpu.VMEM((S, LANES), F32), pltpu.VMEM((S, LANES), F32),
                        pltpu.VMEM((2, 3, 2 * ATT_QB, ATT_KB), F32)],
        compiler_params=_cparams(("arbitrary", "arbitrary")),
        name="dilated_attention",
    )(att_slab, att_slab, att_slab)


PACK_WORDS = D_MODEL // 2
ROW_TILE = PACK_WORDS // LANES
HIGH_HALF = -65536


def _pack_rows(x):
    bits = lambda v: lax.bitcast_convert_type(v.astype(BF16).astype(F32), jnp.int32)
    low = (bits(x[:, :PACK_WORDS]) >> 16) & 0xFFFF
    return (bits(x[:, PACK_WORDS:]) & HIGH_HALF) | low


def _unpack_rows(w):
    low = lax.bitcast_convert_type(w << 16, F32)
    high = lax.bitcast_convert_type(w & HIGH_HALF, F32)
    return jnp.concatenate([low, high], axis=1).astype(BF16)


def _to_row_tiles(ref, w):
    n = w.shape[0]
    for j in range(ROW_TILE):
        ref[pl.ds(j, n, stride=ROW_TILE), :] = w[:, j * LANES:(j + 1) * LANES]


def _from_row_tiles(ref, n):
    return jnp.concatenate([ref[pl.ds(j, n, stride=ROW_TILE), :] for j in range(ROW_TILE)], axis=1)


def _tile_copy(src_ref, src_row, dst_ref, dst_row, sem):
    src = pl.ds(pl.multiple_of(src_row * ROW_TILE, ROW_TILE), ROW_TILE)
    dst = pl.ds(pl.multiple_of(dst_row * ROW_TILE, ROW_TILE), ROW_TILE)
    return pltpu.make_async_copy(src_ref.at[src], dst_ref.at[dst], sem)


def _outproj_kernel(of_ref, ob_ref, gg_ref, att_ref, x_ref, gnw_ref, wo1_ref, wo2_ref,
                    n2_ref, wr_ref, br_ref, h_ref, u_ref, lg_ref, stage_ref):
    rows = stage_ref.shape[1] // ATT_CLASSES
    for j in range(ATT_WIDTH // LANES):
        for c in range(ATT_CLASSES):
            stage_ref[j, pl.ds(c, rows, stride=ATT_CLASSES), :] = att_ref[c, :, j * LANES:(j + 1) * LANES]
    att = jnp.concatenate([stage_ref[j] for j in range(ATT_WIDTH // LANES)], axis=1)
    o = of_ref[...].astype(F32) + ob_ref[...].astype(F32)
    gate = gg_ref[...].astype(F32)
    gnw = gnw_ref[...]
    parts = []
    for h in range(GLA_HEADS):
        sl = slice(h * GLA_DV, (h + 1) * GLA_DV)
        parts.append(_rms(o[:, sl], gnw))
    y = jnp.concatenate(parts, axis=1) * (gate / (1.0 + jnp.exp(-gate)))
    mix = _dot(y.astype(BF16), wo1_ref[...]) + _dot(att.astype(BF16), wo2_ref[...])
    h = x_ref[...] + mix
    h_ref[...] = h
    u = _rms(h, n2_ref[...])
    _to_row_tiles(u_ref, _pack_rows(u))
    u_hi = u.astype(BF16)
    u_lo = (u - u_hi.astype(F32)).astype(BF16)
    hi_both = _dot_nt(wr_ref[...], u_hi)
    lg_ref[...] = (hi_both[:LANES] + hi_both[LANES:] + _dot_nt(wr_ref[:LANES], u_lo)) + br_ref[...]


def _outproj(o_f, o_b, gate, att_out, x2, gla_norm_w, w_out, norm2_w, wr, br, tm=512):
    T = x2.shape[0]
    nS = att_out.shape[2] * ATT_CLASSES // tm
    row = lambda i: (i, 0)
    const = lambda i: (0, 0)
    wo = w_out.astype(BF16)
    wr_hi = wr.astype(BF16)
    wr_lo = (wr - wr_hi.astype(F32)).astype(BF16)
    wr = jnp.concatenate([wr_hi, wr_lo], axis=0)
    return pl.pallas_call(
        _outproj_kernel,
        grid=(T // tm,),
        in_specs=[
            pl.BlockSpec((tm, GLA_VAL_WIDTH), row),
            pl.BlockSpec((tm, GLA_VAL_WIDTH), row),
            pl.BlockSpec((tm, GLA_VAL_WIDTH), row),
            pl.BlockSpec((None, ATT_CLASSES, tm // ATT_CLASSES, ATT_WIDTH), lambda i: (i // nS, 0, i % nS, 0)),
            pl.BlockSpec((tm, D_MODEL), row),
            pl.BlockSpec((1, GLA_DV), const),
            pl.BlockSpec((GLA_VAL_WIDTH, D_MODEL), const),
            pl.BlockSpec((ATT_WIDTH, D_MODEL), const),
            pl.BlockSpec((1, D_MODEL), const),
            pl.BlockSpec((2 * LANES, D_MODEL), const),
            pl.BlockSpec((LANES, 1), const),
        ],
        out_specs=[
            pl.BlockSpec((tm, D_MODEL), row),
            pl.BlockSpec((tm * ROW_TILE, LANES), row),
            pl.BlockSpec((LANES, tm), lambda i: (0, i)),
        ],
        out_shape=[
            jax.ShapeDtypeStruct((T, D_MODEL), F32),
            jax.ShapeDtypeStruct((T * ROW_TILE, LANES), jnp.int32),
            jax.ShapeDtypeStruct((LANES, T), F32),
        ],
        scratch_shapes=[pltpu.VMEM((ATT_WIDTH // LANES, tm, LANES), F32)],
        compiler_params=_cparams(("arbitrary",)),
        name="outproj",
    )(o_f, o_b, gate, att_out, x2, gla_norm_w[None, :], wo[:GLA_VAL_WIDTH], wo[GLA_VAL_WIDTH:],
      norm2_w[None, :], wr, br)


INFO_E1, INFO_E2, INFO_R1, INFO_R2, INFO_W1, INFO_W2 = range(6)
ROUTE_ROWS = 40


def _route_kernel(lg_ref, info_ref, cnt_ref, carry_ref):
    @pl.when(pl.program_id(0) == 0)
    def _():
        carry_ref[...] = jnp.zeros_like(carry_ref)

    lg = lg_ref[:ROUTE_ROWS, :]
    tr = lg.shape[1]
    row = lax.broadcasted_iota(jnp.int32, (ROUTE_ROWS, tr), 0)
    big = jnp.int32(1 << 20)
    is_g = (row >= MOE_N_EXPERTS) & (row < MOE_N_EXPERTS + MOE_GROUPS)
    gl = jnp.where(is_g, lg, -jnp.inf)
    gmax = jnp.max(gl, axis=0, keepdims=True)
    gsel = jnp.min(jnp.where(gl == gmax, row - MOE_N_EXPERTS, big), axis=0, keepdims=True)
    g_w = 1.0 / jnp.sum(jnp.where(is_g, jnp.exp(lg - gmax), 0.0), axis=0, keepdims=True)
    in_grp = (row < MOE_N_EXPERTS) & ((row >> 3) == gsel)
    el = jnp.where(in_grp, lg, -jnp.inf)
    v1 = jnp.max(el, axis=0, keepdims=True)
    i1 = jnp.min(jnp.where(el == v1, row, big), axis=0, keepdims=True)
    el2 = jnp.where(row == i1, -jnp.inf, el)
    v2 = jnp.max(el2, axis=0, keepdims=True)
    i2 = jnp.min(jnp.where(el2 == v2, row, big), axis=0, keepdims=True)
    t = jnp.exp(v2 - v1)
    w1 = g_w * (1.0 / (1.0 + t))
    w2 = g_w * (t / (1.0 + t))

    erow = lax.broadcasted_iota(jnp.int32, (MOE_N_EXPERTS, tr), 0)
    hit1 = erow == i1
    hit2 = erow == i2
    member = jnp.where(hit1 | hit2, 1.0, 0.0)
    r = lax.broadcasted_iota(jnp.int32, (tr, tr), 0)
    c = lax.broadcasted_iota(jnp.int32, (tr, tr), 1)
    earlier = jnp.where(r < c, 1.0, 0.0).astype(BF16)
    carry = carry_ref[...]
    prefix = _dot(member.astype(BF16), earlier) + carry[:, 0:1]
    rank1 = jnp.sum(jnp.where(hit1, prefix, 0.0), axis=0, keepdims=True)
    rank2 = jnp.sum(jnp.where(hit2, prefix, 0.0), axis=0, keepdims=True)
    carry = carry + jnp.sum(member, axis=1, keepdims=True)
    carry_ref[...] = carry
    cnt_ref[...] = carry

    zero = jnp.zeros_like(w1)
    info_ref[...] = jnp.concatenate([i1.astype(F32), i2.astype(F32), rank1, rank2, w1, w2, zero, zero], axis=0)


def _route(logits_t, tr=512):
    T = logits_t.shape[1]
    return pl.pallas_call(
        _route_kernel,
        grid=(T // tr,),
        in_specs=[pl.BlockSpec((LANES, tr), lambda i: (0, i))],
        out_specs=[pl.BlockSpec((8, tr), lambda i: (0, i)),
                   pl.BlockSpec((MOE_N_EXPERTS, LANES), lambda i: (0, 0))],
        out_shape=[jax.ShapeDtypeStruct((8, T), F32), jax.ShapeDtypeStruct((MOE_N_EXPERTS, LANES), F32)],
        scratch_shapes=[pltpu.VMEM((MOE_N_EXPERTS, LANES), F32)],
        compiler_params=_cparams(("arbitrary",)),
        name="route",
    )(logits_t)


ROW_UNROLL = 8


def _dispatch_kernel(dest_ref, pend_ref, u_ref, xs_ref, zbuf, sem, zsem, *, td, T, nblk):
    @pl.when(pl.program_id(0) == 0)
    def _():
        zbuf[...] = jnp.zeros_like(zbuf)
        n_used = pend_ref[MOE_N_EXPERTS - 1] >> 8

        def zero_copy(blk):
            start = pl.multiple_of(blk * (MOE_ROWS * ROW_TILE), MOE_ROWS * ROW_TILE)
            return pltpu.make_async_copy(zbuf, xs_ref.at[pl.ds(start, MOE_ROWS * ROW_TILE)], zsem)

        def each_pad_block(fn):
            def per_expert(e, carry):
                prev = jnp.where(e > 0, pend_ref[jnp.maximum(e - 1, 0)], 0)

                @pl.when(pend_ref[e] > prev)
                def _():
                    fn((pend_ref[e] >> 8) - 1)
                return carry

            def per_tail(j, carry):
                @pl.when(n_used + j < nblk)
                def _():
                    fn(n_used + j)
                return carry

            lax.fori_loop(0, MOE_N_EXPERTS, per_expert, 0)
            lax.fori_loop(0, MOE_N_EXPERTS, per_tail, 0)

        each_pad_block(lambda blk: zero_copy(blk).start())
        each_pad_block(lambda blk: zero_copy(blk).wait())

    base = pl.program_id(0) * td

    def issue(g, carry):
        for j in range(ROW_UNROLL):
            r = g * ROW_UNROLL + j
            for k in range(MOE_TOP_K):
                _tile_copy(u_ref, r, xs_ref, dest_ref[k * T + base + r], sem).start(priority=k)
        return carry

    lax.fori_loop(0, td // ROW_UNROLL, issue, 0)
    for k in range(MOE_TOP_K):
        pltpu.make_async_copy(u_ref, xs_ref.at[pl.ds(0, td * ROW_TILE)], sem).wait()


def _dispatch(dest, pend, u2, cap, td=1024):
    T = u2.shape[0] // ROW_TILE
    return pl.pallas_call(
        functools.partial(_dispatch_kernel, td=td, T=T, nblk=cap // MOE_ROWS),
        grid_spec=pltpu.PrefetchScalarGridSpec(
            num_scalar_prefetch=2,
            grid=(T // td,),
            in_specs=[pl.BlockSpec((td * ROW_TILE, LANES), lambda i, d, z: (i, 0))],
            out_specs=pl.BlockSpec(memory_space=pl.ANY),
            scratch_shapes=[pltpu.VMEM((MOE_ROWS * ROW_TILE, LANES), jnp.int32),
                            pltpu.SemaphoreType.DMA(()), pltpu.SemaphoreType.DMA(())],
        ),
        out_shape=jax.ShapeDtypeStruct((cap * ROW_TILE, LANES), jnp.int32),
        compiler_params=_cparams(("arbitrary",)),
        name="dispatch",
    )(dest, pend, u2)


def _expert_kernel(pend_ref, xs_hbm, wg_hbm, wu_hbm, wd_hbm, ys_hbm,
                   xbuf, ybuf, zbuf, stage_g, stage_u, stage_d, wgb, wub, wdb, xsem, ysem, wsem, zsem, *, nblk):
    last = MOE_N_EXPERTS - 1
    n_used = pend_ref[last] >> 8
    block_rows = MOE_ROWS * ROW_TILE

    def x_copy(b, slot):
        start = pl.multiple_of(b * block_rows, block_rows)
        return pltpu.make_async_copy(xs_hbm.at[pl.ds(start, block_rows)], xbuf.at[slot], xsem.at[slot])

    def y_copy(b, slot):
        start = pl.multiple_of(b * block_rows, block_rows)
        return pltpu.make_async_copy(ybuf.at[slot], ys_hbm.at[pl.ds(start, block_rows)], ysem.at[slot])

    def zero_copy(b):
        start = pl.multiple_of(b * block_rows, block_rows)
        return pltpu.make_async_copy(zbuf, ys_hbm.at[pl.ds(start, block_rows)], zsem)

    def weight_copies(e):
        return (pltpu.make_async_copy(wg_hbm.at[e], stage_g, wsem.at[0]),
                pltpu.make_async_copy(wu_hbm.at[e], stage_u, wsem.at[1]),
                pltpu.make_async_copy(wd_hbm.at[e], stage_d, wsem.at[2]))

    def owner(start, row):
        return lax.while_loop(lambda e: (e < last) & (pend_ref[e] <= row), lambda e: e + 1, start)

    for c in weight_copies(owner(0, 0)):
        c.start()
    x_copy(0, 0).start()

    zbuf[...] = jnp.zeros_like(zbuf)

    def tail(fn):
        def step(b, carry):
            fn(b)
            return carry
        lax.fori_loop(n_used, nblk, step, 0)

    tail(lambda b: zero_copy(b).start())

    def body(b, cur):
        slot = b & 1
        e = owner(jnp.maximum(cur, 0), b * MOE_ROWS)
        x_copy(b, slot).wait()

        @pl.when(b + 1 < n_used)
        def _():
            x_copy(b + 1, 1 - slot).start()

        @pl.when(e != cur)
        def _():
            for c in weight_copies(e):
                c.wait()
            wgb[...] = stage_g[...].astype(BF16)
            wub[...] = stage_u[...].astype(BF16)
            wdb[...] = stage_d[...].astype(BF16)

            @pl.when(pend_ref[e] < pend_ref[last])
            def _():
                for c in weight_copies(owner(e + 1, pend_ref[e])):
                    c.start(priority=1)

        @pl.when(b >= 2)
        def _():
            y_copy(b - 2, slot).wait()

        xb = _unpack_rows(_from_row_tiles(xbuf.at[slot], MOE_ROWS))
        g = _dot(xb, wgb[...])
        u = _dot(xb, wub[...])
        hid = (g / (1.0 + jnp.exp(-g))) * u
        _to_row_tiles(ybuf.at[slot], _pack_rows(_dot(hid.astype(BF16), wdb[...])))
        y_copy(b, slot).start()
        return e

    lax.fori_loop(0, n_used, body, jnp.int32(-1))

    @pl.when(n_used >= 2)
    def _():
        y_copy(n_used - 2, n_used & 1).wait()
    y_copy(n_used - 1, (n_used - 1) & 1).wait()
    tail(lambda b: zero_copy(b).wait())


def _experts(pend, xs, w_gate, w_up, w_down):
    cap = xs.shape[0] // ROW_TILE
    nblk = cap // MOE_ROWS
    block = (MOE_ROWS * ROW_TILE, LANES)
    anywhere = pl.BlockSpec(memory_space=pl.ANY)
    return pl.pallas_call(
        functools.partial(_expert_kernel, nblk=nblk),
        grid_spec=pltpu.PrefetchScalarGridSpec(
            num_scalar_prefetch=1,
            grid=(1,),
            in_specs=[anywhere, anywhere, anywhere, anywhere],
            out_specs=anywhere,
            scratch_shapes=[pltpu.VMEM((2,) + block, jnp.int32),
                            pltpu.VMEM((2,) + block, jnp.int32),
                            pltpu.VMEM(block, jnp.int32),
                            pltpu.VMEM((D_MODEL, MOE_D_FF), F32),
                            pltpu.VMEM((D_MODEL, MOE_D_FF), F32),
                            pltpu.VMEM((MOE_D_FF, D_MODEL), F32),
                            pltpu.VMEM((D_MODEL, MOE_D_FF), BF16),
                            pltpu.VMEM((D_MODEL, MOE_D_FF), BF16),
                            pltpu.VMEM((MOE_D_FF, D_MODEL), BF16),
                            pltpu.SemaphoreType.DMA((2,)),
                            pltpu.SemaphoreType.DMA((2,)),
                            pltpu.SemaphoreType.DMA((3,)),
                            pltpu.SemaphoreType.DMA(())],
        ),
        out_shape=jax.ShapeDtypeStruct((cap * ROW_TILE, LANES), jnp.int32),
        compiler_params=_cparams(("arbitrary",)),
        name="experts",
    )(pend, xs, w_gate, w_up, w_down)


def _combine_kernel(dest_ref, ys_ref, info_ref, h_ref, fw_ref, o_ref, buf, sem, *, tc, T):
    i = pl.program_id(0)
    n = pl.num_programs(0)

    def issue(step, slot):
        base = step * tc

        def body(g, carry):
            for j in range(ROW_UNROLL):
                r = g * ROW_UNROLL + j
                for k in range(MOE_TOP_K):
                    _tile_copy(ys_ref, dest_ref[k * T + base + r], buf.at[slot, k], r,
                               sem.at[slot]).start(priority=k)
            return carry

        lax.fori_loop(0, tc // ROW_UNROLL, body, 0)

    @pl.when(i == 0)
    def _():
        issue(0, 0)

    slot = i % 2

    @pl.when(i + 1 < n)
    def _():
        issue(i + 1, 1 - slot)

    for k in range(MOE_TOP_K):
        pltpu.make_async_copy(ys_ref.at[pl.ds(0, tc * ROW_TILE)], buf.at[slot, k], sem.at[slot]).wait()

    info_t = jnp.concatenate([info_ref[...]] * (LANES // 8), axis=0).T
    w1 = info_t[:, INFO_W1:INFO_W1 + 1]
    w2 = info_t[:, INFO_W2:INFO_W2 + 1]
    y1 = _unpack_rows(_from_row_tiles(buf.at[slot, 0], tc)).astype(F32)
    y2 = _unpack_rows(_from_row_tiles(buf.at[slot, 1], tc)).astype(F32)
    h = h_ref[...] + (y1 * w1 + y2 * w2)
    o_ref[...] = _rms(h, fw_ref[...])


def _combine(dest, ys, info, h, final_w, tc=512):
    T = h.shape[0]
    return pl.pallas_call(
        functools.partial(_combine_kernel, tc=tc, T=T),
        grid_spec=pltpu.PrefetchScalarGridSpec(
            num_scalar_prefetch=1,
            grid=(T // tc,),
            in_specs=[pl.BlockSpec(memory_space=pl.ANY),
                      pl.BlockSpec((8, tc), lambda i, d: (0, i)),
                      pl.BlockSpec((tc, D_MODEL), lambda i, d: (i, 0)),
                      pl.BlockSpec((1, D_MODEL), lambda i, d: (0, 0))],
            out_specs=pl.BlockSpec((tc, D_MODEL), lambda i, d: (i, 0)),
            scratch_shapes=[pltpu.VMEM((2, MOE_TOP_K, tc * ROW_TILE, LANES), jnp.int32),
                            pltpu.SemaphoreType.DMA((2,))],
        ),
        out_shape=jax.ShapeDtypeStruct((T, D_MODEL), F32),
        compiler_params=_cparams(("arbitrary",)),
        name="combine",
    )(dest, ys, info, h, final_w[None, :])


def _plan_kernel(info_ref, cnt_ref, dest_ref, pend_ref):
    cnt = cnt_ref[...].astype(jnp.int32)
    nblk_e = ((cnt + (MOE_ROWS - 1)) >> 8).astype(F32)
    r = lax.broadcasted_iota(jnp.int32, (MOE_N_EXPERTS, MOE_N_EXPERTS), 0)
    c = lax.broadcasted_iota(jnp.int32, (MOE_N_EXPERTS, MOE_N_EXPERTS), 1)
    before = jnp.where(c < r, 1.0, 0.0).astype(BF16)
    first_blk = _dot(before, nblk_e.astype(BF16))
    pstart = first_blk[:, 0:1] * float(MOE_ROWS)
    pend_ref[...] = ((first_blk + nblk_e) * float(MOE_ROWS)).astype(jnp.int32)

    info = info_ref[...]
    erow = lax.broadcasted_iota(jnp.int32, (MOE_N_EXPERTS, info.shape[1]), 0)
    start_of = lambda e: jnp.sum(jnp.where(erow == e.astype(jnp.int32), pstart, 0.0), axis=0, keepdims=True)
    d1 = info[INFO_R1:INFO_R1 + 1] + start_of(info[INFO_E1:INFO_E1 + 1])
    d2 = info[INFO_R2:INFO_R2 + 1] + start_of(info[INFO_E2:INFO_E2 + 1])
    zero = jnp.zeros_like(d1)
    dest_ref[...] = jnp.concatenate([d1, d2] + [zero] * 6, axis=0).astype(jnp.int32)


def _plan(info, counts, tr=2048):
    T = info.shape[1]
    dest8, pend = pl.pallas_call(
        _plan_kernel,
        grid=(T // tr,),
        in_specs=[pl.BlockSpec((8, tr), lambda i: (0, i)),
                  pl.BlockSpec((MOE_N_EXPERTS, LANES), lambda i: (0, 0))],
        out_specs=[pl.BlockSpec((8, tr), lambda i: (0, i)),
                   pl.BlockSpec((MOE_N_EXPERTS, LANES), lambda i: (0, 0))],
        out_shape=[jax.ShapeDtypeStruct((8, T), jnp.int32),
                   jax.ShapeDtypeStruct((MOE_N_EXPERTS, LANES), jnp.int32)],
        compiler_params=_cparams(("arbitrary",)),
        name="plan",
    )(info, counts)
    return dest8[:MOE_TOP_K].reshape(-1), pend[:, 0]


def _moe_capacity(T):
    return (-(-(T * MOE_TOP_K) // MOE_ROWS) + MOE_N_EXPERTS) * MOE_ROWS


def _router_weights(router_group_w, router_group_b, router_expert_w, router_expert_b):
    we = jnp.transpose(router_expert_w, (0, 2, 1)).reshape(MOE_N_EXPERTS, D_MODEL)
    pad = LANES - MOE_N_EXPERTS - MOE_GROUPS
    wr = jnp.concatenate([we, router_group_w.T, jnp.zeros((pad, D_MODEL), F32)], axis=0)
    br = jnp.concatenate([router_expert_b.reshape(-1), router_group_b, jnp.zeros((pad,), F32)])[:, None]
    return wr, br


def kernel(x, norm1_w, w_in, gla_fwd_gate_w, gla_fwd_gate_b, gla_bwd_gate_w, gla_bwd_gate_b,
           gla_norm_w, w_out, norm2_w, router_group_w, router_group_b, router_expert_w,
           router_expert_b, expert_w_gate, expert_w_up, expert_w_down, final_norm_w):
    B, S, D = x.shape
    T = B * S
    assert norm1_w.shape[0] == 1, "single-layer trunk: the final norm is fused into the combine step"
    h = x.reshape(T, D)
    gla_slab, gate, loga, att_slab = _inproj(h, S, norm1_w[0], w_in[0], gla_fwd_gate_w[0], gla_fwd_gate_b[0],
                                       gla_bwd_gate_w[0], gla_bwd_gate_b[0])
    o_f, o_b = _gla(gla_slab, loga, B, S)
    att_out = _attention(att_slab.reshape(T, 3 * ATT_WIDTH), B, S)
    att_out = att_out.reshape(B, ATT_CLASSES, S // ATT_CLASSES, ATT_WIDTH)
    wr, br = _router_weights(router_group_w[0], router_group_b[0], router_expert_w[0], router_expert_b[0])
    h, u2, logits = _outproj(o_f, o_b, gate, att_out, h, gla_norm_w[0], w_out[0], norm2_w[0], wr, br)
    info, counts = _route(logits)
    dest, pend = _plan(info, counts)
    xs = _dispatch(dest, pend, u2, _moe_capacity(T))
    ys = _experts(pend, xs, expert_w_gate[0], expert_w_up[0], expert_w_down[0])
    out = _combine(dest, ys, info, h, final_norm_w)
    return out.reshape(B, S, D)
```

```python
import functools

import jax
import jax.numpy as jnp
import numpy as np
from jax import lax
from jax.experimental import pallas as pl
from jax.experimental.pallas import tpu as pltpu

F32 = jnp.float32
BF16 = jnp.bfloat16

D_MODEL = 1024
GLA_HEADS = 4
GLA_DV = 128
GLA_DK = 64
GLA_KEY_WIDTH = GLA_HEADS * GLA_DK
GLA_VAL_WIDTH = GLA_HEADS * GLA_DV
GLA_GATE_RANK = 16
GLA_TAU = 16.0
GLA_CHUNK = 64
ATT_WIDTH = 512
ATT_HEAD_DIM = 64
ATT_HEADS = 8
ROT_DIM = 16
ROPE_THETA = 500000.0
DILATED_PATTERNS = ((128, 1), (512, 4), (2048, 16))
ATT_RADIUS = 64
MOE_GROUPS = 4
MOE_EXPERTS_PER_GROUP = 8
MOE_N_EXPERTS = 32
MOE_TOP_K = 2
MOE_D_FF = 512
EPS = 1e-6
NEG_INF = -1e30
LOG2E = 1.4426950408889634

LANES = 128
MOE_ROWS = 256
VMEM_LIMIT = 56 * 1024 * 1024


def _cparams(sem):
    return pltpu.CompilerParams(dimension_semantics=sem, vmem_limit_bytes=VMEM_LIMIT)


def _dot(a, b):
    return jnp.dot(a, b, preferred_element_type=F32)


def _dot_nt(a, b):
    return lax.dot_general(a, b, (((1,), (1,)), ((), ())), preferred_element_type=F32)


def _dot_tn(a, b):
    return lax.dot_general(a, b, (((0,), (0,)), ((), ())), preferred_element_type=F32)


def _rms(x, w):
    return x * lax.rsqrt(jnp.mean(x * x, axis=-1, keepdims=True) + EPS) * w


def _inproj_kernel(x_ref, n1_ref, wg_ref, wlr_ref, wa_ref, gw_ref, gb_ref,
                   rc_ref, rs1_ref, rs2_ref, gla_ref, gate_ref, loga_ref, att_ref, stage_ref, wgb, wlrb):
    @pl.when(pl.program_id(0) == 0)
    def _():
        wgb[...] = wg_ref[...].astype(BF16)
        wlrb[...] = wlr_ref[...].astype(BF16)

    x = x_ref[...]
    ub = _rms(x, n1_ref[...]).astype(BF16)
    g = _dot(ub, wgb[...])
    qkv = 2 * GLA_KEY_WIDTH + GLA_VAL_WIDTH
    gla_ref[:, :GLA_KEY_WIDTH] = g[:, :GLA_KEY_WIDTH] * (GLA_DK ** -0.5)
    gla_ref[:, GLA_KEY_WIDTH:] = g[:, GLA_KEY_WIDTH:qkv]
    gate_ref[...] = g[:, qkv:].astype(BF16)
    lr = _dot(ub, wlrb[...])
    gate = _dot(lr.astype(BF16), gw_ref[...]) + gb_ref[...]
    loga_ref[...] = (jnp.minimum(gate, 0.0) - jnp.log(1.0 + jnp.exp(-jnp.abs(gate)))) * (1.0 / GLA_TAU)
    a = _dot(ub, wa_ref[...])
    qk = a[:, :2 * ATT_WIDTH]
    reps = 2 * ATT_WIDTH // LANES
    c = jnp.concatenate([rc_ref[...]] * reps, axis=1)
    s1 = jnp.concatenate([rs1_ref[...]] * reps, axis=1)
    s2 = jnp.concatenate([rs2_ref[...]] * reps, axis=1)
    half = ROT_DIM // 2
    n = 2 * ATT_WIDTH
    roped = qk * c + pltpu.roll(qk, n - half, 1) * s1 + pltpu.roll(qk, half, 1) * s2
    qkv = jnp.concatenate([roped[:, :ATT_WIDTH] * (ATT_HEAD_DIM ** -0.5 * LOG2E), roped[:, ATT_WIDTH:],
                           a[:, 2 * ATT_WIDTH:]], axis=1)
    rows = x.shape[0] // ATT_CLASSES
    for j in range(3 * ATT_WIDTH // LANES):
        cols = slice(j * LANES, (j + 1) * LANES)
        stage_ref[j] = qkv[:, cols]
        for c in range(ATT_CLASSES):
            att_ref[c, :, cols] = stage_ref[j, pl.ds(c, rows, stride=ATT_CLASSES), :]


def _rope_lane_tables(S):
    half = ROT_DIM // 2
    inv = np.float32(ROPE_THETA) ** (-(np.arange(0, ROT_DIM, 2, dtype=np.float32) / np.float32(ROT_DIM)))
    ang = np.arange(S, dtype=np.float32)[:, None] * inv[None, :].astype(np.float32)
    cos, sin = np.cos(ang), np.sin(ang)
    ones = np.ones((S, ATT_HEAD_DIM - ROT_DIM), np.float32)
    zeros = np.zeros((S, ATT_HEAD_DIM - ROT_DIM), np.float32)
    zeros8 = np.zeros((S, half), np.float32)
    rep = LANES // ATT_HEAD_DIM
    c = np.tile(np.concatenate([cos, cos, ones], axis=1), (1, rep))
    s1 = np.tile(np.concatenate([-sin, zeros8, zeros], axis=1), (1, rep))
    s2 = np.tile(np.concatenate([zeros8, sin, zeros], axis=1), (1, rep))
    return jnp.asarray(c), jnp.asarray(s1), jnp.asarray(s2)


def _inproj(x2, S, norm1_w, w_in, wf, bfw, wb, bbw, tm=512):
    T = x2.shape[0]
    o_lr = 2 * GLA_KEY_WIDTH + 2 * GLA_VAL_WIDTH
    o_att = o_lr + 2 * GLA_GATE_RANK
    wa = w_in[:, o_att:].astype(BF16)
    zeros = jnp.zeros((GLA_GATE_RANK, GLA_KEY_WIDTH), F32)
    gw = jnp.concatenate([jnp.concatenate([wf, zeros], axis=1), jnp.concatenate([zeros, wb], axis=1),
                          jnp.zeros((LANES - 2 * GLA_GATE_RANK, 2 * GLA_KEY_WIDTH), F32)], axis=0).astype(BF16)
    gb = jnp.concatenate([bfw, bbw])[None, :]
    rc, rs1, rs2 = _rope_lane_tables(S)
    nS = S // tm
    row = lambda i: (i, 0)
    const = lambda i: (0, 0)
    pos = lambda i: (i % nS, 0)
    return pl.pallas_call(
        _inproj_kernel,
        grid=(T // tm,),
        in_specs=[
            pl.BlockSpec((tm, D_MODEL), row),
            pl.BlockSpec((1, D_MODEL), const),
            pl.BlockSpec((D_MODEL, o_lr), const),
            pl.BlockSpec((D_MODEL, LANES), lambda i: (0, o_lr // LANES)),
            pl.BlockSpec((D_MODEL, 3 * ATT_WIDTH), const),
            pl.BlockSpec((LANES, 2 * GLA_KEY_WIDTH), const),
            pl.BlockSpec((1, 2 * GLA_KEY_WIDTH), const),
            pl.BlockSpec((tm, LANES), pos),
            pl.BlockSpec((tm, LANES), pos),
            pl.BlockSpec((tm, LANES), pos),
        ],
        out_specs=[
            pl.BlockSpec((tm, o_lr - GLA_VAL_WIDTH), row),
            pl.BlockSpec((tm, GLA_VAL_WIDTH), row),
            pl.BlockSpec((tm, 2 * GLA_KEY_WIDTH), row),
            pl.BlockSpec((None, ATT_CLASSES, tm // ATT_CLASSES, 3 * ATT_WIDTH),
                         lambda i: (i // nS, 0, i % nS, 0)),
        ],
        out_shape=[
            jax.ShapeDtypeStruct((T, o_lr - GLA_VAL_WIDTH), F32),
            jax.ShapeDtypeStruct((T, GLA_VAL_WIDTH), BF16),
            jax.ShapeDtypeStruct((T, 2 * GLA_KEY_WIDTH), F32),
            jax.ShapeDtypeStruct((T // S, ATT_CLASSES, S // ATT_CLASSES, 3 * ATT_WIDTH), F32),
        ],
        scratch_shapes=[pltpu.VMEM((3 * ATT_WIDTH // LANES, tm, LANES), F32),
                        pltpu.VMEM((D_MODEL, o_lr), BF16), pltpu.VMEM((D_MODEL, LANES), BF16)],
        compiler_params=_cparams(("arbitrary",)),
        name="inproj",
    )(x2, norm1_w[None, :], w_in, w_in, wa, gw, gb, rc, rs1, rs2)


def _gla_decays(q, k, v, la, forward, G):
    C = GLA_CHUNK
    R = G * C
    r = lax.broadcasted_iota(jnp.int32, (R, R), 0)
    c = lax.broadcasted_iota(jnp.int32, (R, R), 1)
    same = (r >> 6) == (c >> 6)
    tri = (c <= r) if forward else (c >= r)
    t_mat = jnp.where(same, jnp.where(tri, 1.0, 0.0), 0.0).astype(BF16)
    hi = la.astype(BF16)
    lo = (la - hi.astype(F32)).astype(BF16)
    b = _dot(t_mat, hi) + _dot(t_mat, lo)
    edge = C - 1 if forward else 0
    tot = jnp.concatenate([jnp.broadcast_to(b[g * C + edge:g * C + edge + 1], (C, GLA_KEY_WIDTH))
                           for g in range(G)], axis=0)
    order = list(range(G)) if forward else list(range(G - 1, -1, -1))
    return dict(q_dec=q * jnp.exp(b), k_inv=(k * jnp.exp(-b)).astype(BF16), k_end=k * jnp.exp(tot - b),
                tot=tot, vb=v.astype(BF16), order=order, forward=forward, G=G)


def _gla_scores(prep):
    C, H = GLA_CHUNK, GLA_HEADS
    lane_k = lax.broadcasted_iota(jnp.int32, (C, GLA_KEY_WIDTH), 1)
    qd_heads, scores = {}, {}
    for g in prep["order"]:
        rows = slice(g * C, (g + 1) * C)
        qd = prep["q_dec"][rows]
        qd_heads[g] = jnp.concatenate([jnp.where((lane_k >> 6) == h, qd, 0.0) for h in range(H)],
                                      axis=0).astype(BF16)
        scores[g] = _dot_nt(qd_heads[g], prep["k_inv"][rows])
    return qd_heads, scores


def _gla_chunk_updates(prep):
    C, H, G = GLA_CHUNK, GLA_HEADS, prep["G"]
    k_end, tot, vb = prep["k_end"], prep["tot"], prep["vb"]
    kv, dec_t = {}, {}
    lane = lax.broadcasted_iota(jnp.int32, (GLA_KEY_WIDTH, 2 * C), 1)
    zeros = jnp.zeros((C, GLA_DV), BF16)
    for p in range(G // 2):
        pair = slice(2 * p * C, (2 * p + 2) * C)
        ke_t = k_end[pair].T.astype(BF16)
        tot_t = tot[pair].T
        swapped = pltpu.roll(tot_t, C, 1)
        for half in range(2):
            g = 2 * p + half
            rows = slice(g * C, (g + 1) * C)
            own = (lane < C) if half == 0 else (lane >= C)
            dec_t[g] = jnp.exp(jnp.where(own, tot_t, swapped))
            parts = []
            for h in range(H):
                v_h = vb[rows, h * GLA_DV:(h + 1) * GLA_DV]
                v_pad = jnp.concatenate([v_h, zeros] if half == 0 else [zeros, v_h], axis=0)
                parts.append(_dot(ke_t[h * C:(h + 1) * C], v_pad))
            kv[g] = jnp.concatenate(parts, axis=0)
    return kv, dec_t


def _gla_states(prep, kv, dec_t, s_ref):
    st = s_ref[...]
    states = {}
    for g in prep["order"]:
        states[g] = st.astype(BF16)
        st = st * dec_t[g] + kv[g]
    s_ref[...] = st
    return states


def _gla_outputs(prep, qd_heads, scores, inter, o_ref):
    C, H = GLA_CHUNK, GLA_HEADS
    row_q = lax.broadcasted_iota(jnp.int32, (H * C, C), 0) & (C - 1)
    col_k = lax.broadcasted_iota(jnp.int32, (H * C, C), 1)
    a_mask = (col_k <= row_q) if prep["forward"] else (col_k >= row_q)
    for g in prep["order"]:
        rows = slice(g * C, (g + 1) * C)
        a = jnp.where(a_mask, scores[g], 0.0).astype(BF16)
        vv = prep["vb"][rows]
        o_ref[rows, :] = jnp.concatenate(
            [_dot(a[h * C:(h + 1) * C], vv[:, h * GLA_DV:(h + 1) * GLA_DV]) + inter[g][h * C:(h + 1) * C]
             for h in range(H)], axis=1).astype(o_ref.dtype)


def _gla_kernel(qf_ref, kf_ref, vf_ref, laf_ref, qb_ref, kb_ref, vb_ref, lab_ref,
                of_ref, ob_ref, sf_ref, sb_ref, *, G):
    @pl.when(pl.program_id(1) == 0)
    def _():
        sf_ref[...] = jnp.zeros_like(sf_ref)
        sb_ref[...] = jnp.zeros_like(sb_ref)

    dirs = [(_gla_decays(qf_ref[...], kf_ref[...], vf_ref[...], laf_ref[...], True, G), sf_ref, of_ref),
            (_gla_decays(qb_ref[...], kb_ref[...], vb_ref[...], lab_ref[...], False, G), sb_ref, ob_ref)]
    scored = [_gla_scores(prep) for prep, _, _ in dirs]
    updates = [_gla_chunk_updates(prep) for prep, _, _ in dirs]
    states = [_gla_states(prep, kv, dec_t, s_ref) for (prep, s_ref, _), (kv, dec_t) in zip(dirs, updates)]
    inters = [{g: _dot(qd_heads[g], st[g]) for g in prep["order"]}
              for (prep, _, _), (qd_heads, _), st in zip(dirs, scored, states)]
    for (prep, _, o_ref), (qd_heads, scores), inter in zip(dirs, scored, inters):
        _gla_outputs(prep, qd_heads, scores, inter, o_ref)


def _gla(gla_slab, loga, B, S, G=8):
    T = B * S
    R = G * GLA_CHUNK
    ns = S // R
    fwd = lambda col: (lambda b, i: (b * ns + i, col))
    bwd = lambda col: (lambda b, i: (b * ns + ns - 1 - i, col))
    kw, vw = GLA_KEY_WIDTH, GLA_VAL_WIDTH
    return pl.pallas_call(
        functools.partial(_gla_kernel, G=G),
        grid=(B, ns),
        in_specs=[
            pl.BlockSpec((R, kw), fwd(0)), pl.BlockSpec((R, kw), fwd(1)),
            pl.BlockSpec((R, vw), fwd(1)), pl.BlockSpec((R, kw), fwd(0)),
            pl.BlockSpec((R, kw), bwd(0)), pl.BlockSpec((R, kw), bwd(1)),
            pl.BlockSpec((R, vw), bwd(1)), pl.BlockSpec((R, kw), bwd(1)),
        ],
        out_specs=[pl.BlockSpec((R, vw), fwd(0)), pl.BlockSpec((R, vw), bwd(0))],
        out_shape=[jax.ShapeDtypeStruct((T, vw), BF16), jax.ShapeDtypeStruct((T, vw), BF16)],
        scratch_shapes=[pltpu.VMEM((kw, GLA_DV), F32), pltpu.VMEM((kw, GLA_DV), F32)],
        compiler_params=_cparams(("arbitrary", "arbitrary")),
        name="gla",
    )(gla_slab, gla_slab, gla_slab, loga, gla_slab, gla_slab, gla_slab, loga)


ATT_CLASSES = 4
ATT_QB = 128
ATT_KB = ATT_QB + 2 * ATT_RADIUS


ATT_UNROLL = 4


def _att_kernel(q_ref, k_ref, v_ref, o_ref, m_ref, l_ref, bias_ref, *, S):
    QB, KB, NC = ATT_QB, ATT_KB, ATT_CLASSES
    L4 = S // NC
    lane = lax.broadcasted_iota(jnp.int32, (QB, LANES), 1)
    head0 = lane < ATT_HEAD_DIM

    @pl.when((pl.program_id(0) == 0) & (pl.program_id(1) == 0))
    def _():
        rowi = lax.broadcasted_iota(jnp.int32, (2 * QB, KB), 0) & (QB - 1)
        coli = lax.broadcasted_iota(jnp.int32, (2 * QB, KB), 1)
        qpos = (rowi & (QB // NC - 1)) * NC + (rowi >> 5)
        kpos = (coli & (KB // NC - 1)) * NC + (coli >> 6)
        for case in range(3):
            bias_ref[0, case] = jnp.where(jnp.abs(rowi - coli + case * ATT_RADIUS) <= ATT_RADIUS, 0.0, NEG_INF)
            bias_ref[1, case] = jnp.where(jnp.abs(qpos - kpos + case * ATT_RADIUS) <= ATT_RADIUS, 0.0, NEG_INF)

    for pi, (_, d) in enumerate(DILATED_PATTERNS):
        L = S // d
        nb = L // QB
        shift = nb.bit_length() - 1
        first = pi == 0
        last = pi == len(DILATED_PATTERNS) - 1

        def scores(n, d=d, L=L, nb=nb, shift=shift):
            cls = n >> shift
            q0 = (n & (nb - 1)) * QB
            ws = jnp.clip(q0 - ATT_RADIUS, 0, L - KB)
            if d == 1:
                qsls = [pl.ds(pl.multiple_of(c * L4 + q0 // NC, QB // NC), QB // NC) for c in range(NC)]
                ksls = [pl.ds(pl.multiple_of(c * L4 + ws // NC, ATT_RADIUS // NC), KB // NC) for c in range(NC)]
            elif d == NC:
                qsls = [pl.ds(pl.multiple_of(cls * L4 + q0, QB), QB)]
                ksls = [pl.ds(pl.multiple_of(cls * L4 + ws, ATT_RADIUS), KB)]
            else:
                base = (cls & (NC - 1)) * L4 + (cls >> 2)
                qsls = [pl.ds(base + NC * q0, QB, stride=NC)]
                ksls = [pl.ds(base + NC * ws, KB, stride=NC)]
            q = jnp.concatenate([q_ref[sl, :] for sl in qsls], axis=0)
            kw = jnp.concatenate([k_ref[sl, :] for sl in ksls], axis=0)
            q2 = jnp.concatenate([jnp.where(head0, q, 0.0), jnp.where(head0, 0.0, q)], axis=0).astype(BF16)
            s = _dot_nt(q2, kw.astype(BF16))
            return qsls, ksls, s + bias_ref[1 if d == 1 else 0, (q0 - ws) >> 6]

        def softmax_pv(qsls, ksls, s):
            m_blk = jnp.max(s, axis=-1, keepdims=True)
            p = jnp.exp2(s - m_blk)
            vw = jnp.concatenate([v_ref[sl, :] for sl in ksls], axis=0)
            v_ones = jnp.concatenate([vw.astype(BF16), jnp.ones((KB, LANES), BF16)], axis=1)
            pv = _dot(p.astype(BF16), v_ones)
            acc_b = jnp.where(head0, pv[:QB, :LANES], pv[QB:, :LANES])
            m_b = jnp.where(head0, m_blk[:QB], m_blk[QB:])
            l_b = jnp.where(head0, pv[:QB, LANES:], pv[QB:, LANES:])
            return qsls, acc_b, m_b, l_b

        def load(ref, sls):
            return jnp.concatenate([ref[sl, :] for sl in sls], axis=0)

        def store(ref, sls, val):
            n = val.shape[0] // len(sls)
            for i, sl in enumerate(sls):
                ref[sl, :] = val[i * n:(i + 1) * n]

        def body(n, carry, first=first, last=last):
            staged = [scores(n * ATT_UNROLL + u) for u in range(ATT_UNROLL)]
            blocks = [softmax_pv(*st) for st in staged]
            for qsls, acc_b, m_b, l_b in blocks:
                if first:
                    acc, m_new, l_new = acc_b, m_b, l_b
                else:
                    m_old = load(m_ref, qsls)
                    m_new = jnp.maximum(m_old, m_b)
                    w_old = jnp.exp2(m_old - m_new)
                    w_blk = jnp.exp2(m_b - m_new)
                    acc = load(o_ref, qsls) * w_old + acc_b * w_blk
                    l_new = load(l_ref, qsls) * w_old + l_b * w_blk
                if last:
                    store(o_ref, qsls, acc / l_new)
                else:
                    store(o_ref, qsls, acc)
                    store(m_ref, qsls, m_new)
                    store(l_ref, qsls, l_new)
            return carry

        lax.fori_loop(0, S // (QB * ATT_UNROLL), body, 0)


def _attention(att_slab, B, S):
    T = B * S
    ncol = ATT_WIDTH // LANES
    return pl.pallas_call(
        functools.partial(_att_kernel, S=S),
        grid=(B, ncol),
        in_specs=[
            pl.BlockSpec((S, LANES), lambda b, h: (b, h)),
            pl.BlockSpec((S, LANES), lambda b, h: (b, ncol + h)),
            pl.BlockSpec((S, LANES), lambda b, h: (b, 2 * ncol + h)),
        ],
        out_specs=pl.BlockSpec((S, LANES), lambda b, h: (b, h)),
        out_shape=jax.ShapeDtypeStruct((T, ATT_WIDTH), F32),
        scratch_shapes=[pltpu.VMEM((S, LANES), F32), pltpu.VMEM((S, LANES), F32),
                        pltpu.VMEM((2, 3, 2 * ATT_QB, ATT_KB), F32)],
        compiler_params=_cparams(("arbitrary", "arbitrary")),
        name="dilated_attention",
    )(att_slab, att_slab, att_slab)


PACK_WORDS = D_MODEL // 2
ROW_TILE = PACK_WORDS // LANES
HIGH_HALF = -65536


def _pack_rows(x):
    bits = lambda v: lax.bitcast_convert_type(v.astype(BF16).astype(F32), jnp.int32)
    low = (bits(x[:, :PACK_WORDS]) >> 16) & 0xFFFF
    return (bits(x[:, PACK_WORDS:]) & HIGH_HALF) | low


def _unpack_rows(w):
    low = lax.bitcast_convert_type(w << 16, F32)
    high = lax.bitcast_convert_type(w & HIGH_HALF, F32)
    return jnp.concatenate([low, high], axis=1).astype(BF16)


def _to_row_tiles(ref, w):
    n = w.shape[0]
    for j in range(ROW_TILE):
        ref[pl.ds(j, n, stride=ROW_TILE), :] = w[:, j * LANES:(j + 1) * LANES]


def _from_row_tiles(ref, n):
    return jnp.concatenate([ref[pl.ds(j, n, stride=ROW_TILE), :] for j in range(ROW_TILE)], axis=1)


def _tile_copy(src_ref, src_row, dst_ref, dst_row, sem):
    src = pl.ds(pl.multiple_of(src_row * ROW_TILE, ROW_TILE), ROW_TILE)
    dst = pl.ds(pl.multiple_of(dst_row * ROW_TILE, ROW_TILE), ROW_TILE)
    return pltpu.make_async_copy(src_ref.at[src], dst_ref.at[dst], sem)


def _outproj_kernel(of_ref, ob_ref, gg_ref, att_ref, x_ref, gnw_ref, wo1_ref, wo2_ref,
                    n2_ref, wr_ref, br_ref, h_ref, u_ref, lg_ref, stage_ref):
    rows = stage_ref.shape[1] // ATT_CLASSES
    for j in range(ATT_WIDTH // LANES):
        for c in range(ATT_CLASSES):
            stage_ref[j, pl.ds(c, rows, stride=ATT_CLASSES), :] = att_ref[c, :, j * LANES:(j + 1) * LANES]
    att = jnp.concatenate([stage_ref[j] for j in range(ATT_WIDTH // LANES)], axis=1)
    o = of_ref[...].astype(F32) + ob_ref[...].astype(F32)
    gate = gg_ref[...].astype(F32)
    gnw = gnw_ref[...]
    parts = []
    for h in range(GLA_HEADS):
        sl = slice(h * GLA_DV, (h + 1) * GLA_DV)
        parts.append(_rms(o[:, sl], gnw))
    y = jnp.concatenate(parts, axis=1) * (gate / (1.0 + jnp.exp(-gate)))
    mix = _dot(y.astype(BF16), wo1_ref[...]) + _dot(att.astype(BF16), wo2_ref[...])
    h = x_ref[...] + mix
    h_ref[...] = h
    u = _rms(h, n2_ref[...])
    _to_row_tiles(u_ref, _pack_rows(u))
    u_hi = u.astype(BF16)
    u_lo = (u - u_hi.astype(F32)).astype(BF16)
    hi_both = _dot_nt(wr_ref[...], u_hi)
    lg_ref[...] = (hi_both[:LANES] + hi_both[LANES:] + _dot_nt(wr_ref[:LANES], u_lo)) + br_ref[...]


def _outproj(o_f, o_b, gate, att_out, x2, gla_norm_w, w_out, norm2_w, wr, br, tm=512):
    T = x2.shape[0]
    nS = att_out.shape[2] * ATT_CLASSES // tm
    row = lambda i: (i, 0)
    const = lambda i: (0, 0)
    wo = w_out.astype(BF16)
    wr_hi = wr.astype(BF16)
    wr_lo = (wr - wr_hi.astype(F32)).astype(BF16)
    wr = jnp.concatenate([wr_hi, wr_lo], axis=0)
    return pl.pallas_call(
        _outproj_kernel,
        grid=(T // tm,),
        in_specs=[
            pl.BlockSpec((tm, GLA_VAL_WIDTH), row),
            pl.BlockSpec((tm, GLA_VAL_WIDTH), row),
            pl.BlockSpec((tm, GLA_VAL_WIDTH), row),
            pl.BlockSpec((None, ATT_CLASSES, tm // ATT_CLASSES, ATT_WIDTH), lambda i: (i // nS, 0, i % nS, 0)),
            pl.BlockSpec((tm, D_MODEL), row),
            pl.BlockSpec((1, GLA_DV), const),
            pl.BlockSpec((GLA_VAL_WIDTH, D_MODEL), const),
            pl.BlockSpec((ATT_WIDTH, D_MODEL), lambda i: (GLA_VAL_WIDTH // ATT_WIDTH, 0)),
            pl.BlockSpec((1, D_MODEL), const),
            pl.BlockSpec((2 * LANES, D_MODEL), const),
            pl.BlockSpec((LANES, 1), const),
        ],
        out_specs=[
            pl.BlockSpec((tm, D_MODEL), row),
            pl.BlockSpec((tm * ROW_TILE, LANES), row),
            pl.BlockSpec((LANES, tm), lambda i: (0, i)),
        ],
        out_shape=[
            jax.ShapeDtypeStruct((T, D_MODEL), F32),
            jax.ShapeDtypeStruct((T * ROW_TILE, LANES), jnp.int32),
            jax.ShapeDtypeStruct((LANES, T), F32),
        ],
        scratch_shapes=[pltpu.VMEM((ATT_WIDTH // LANES, tm, LANES), F32)],
        compiler_params=_cparams(("arbitrary",)),
        name="outproj",
    )(o_f, o_b, gate, att_out, x2, gla_norm_w[None, :], wo, wo,
      norm2_w[None, :], wr, br)


INFO_E1, INFO_E2, INFO_R1, INFO_R2, INFO_W1, INFO_W2 = range(6)
ROUTE_ROWS = 40


def _route_kernel(lg_ref, info_ref, cnt_ref, carry_ref):
    @pl.when(pl.program_id(0) == 0)
    def _():
        carry_ref[...] = jnp.zeros_like(carry_ref)

    lg = lg_ref[:ROUTE_ROWS, :]
    tr = lg.shape[1]
    row = lax.broadcasted_iota(jnp.int32, (ROUTE_ROWS, tr), 0)
    big = jnp.int32(1 << 20)
    is_g = (row >= MOE_N_EXPERTS) & (row < MOE_N_EXPERTS + MOE_GROUPS)
    gl = jnp.where(is_g, lg, -jnp.inf)
    gmax = jnp.max(gl, axis=0, keepdims=True)
    gsel = jnp.min(jnp.where(gl == gmax, row - MOE_N_EXPERTS, big), axis=0, keepdims=True)
    g_w = 1.0 / jnp.sum(jnp.where(is_g, jnp.exp(lg - gmax), 0.0), axis=0, keepdims=True)
    in_grp = (row < MOE_N_EXPERTS) & ((row >> 3) == gsel)
    el = jnp.where(in_grp, lg, -jnp.inf)
    v1 = jnp.max(el, axis=0, keepdims=True)
    i1 = jnp.min(jnp.where(el == v1, row, big), axis=0, keepdims=True)
    el2 = jnp.where(row == i1, -jnp.inf, el)
    v2 = jnp.max(el2, axis=0, keepdims=True)
    i2 = jnp.min(jnp.where(el2 == v2, row, big), axis=0, keepdims=True)
    t = jnp.exp(v2 - v1)
    w1 = g_w * (1.0 / (1.0 + t))
    w2 = g_w * (t / (1.0 + t))

    erow = lax.broadcasted_iota(jnp.int32, (MOE_N_EXPERTS, tr), 0)
    hit1 = erow == i1
    hit2 = erow == i2
    member = jnp.where(hit1 | hit2, 1.0, 0.0)
    r = lax.broadcasted_iota(jnp.int32, (tr, tr), 0)
    c = lax.broadcasted_iota(jnp.int32, (tr, tr), 1)
    earlier = jnp.where(r < c, 1.0, 0.0).astype(BF16)
    carry = carry_ref[...]
    prefix = _dot(member.astype(BF16), earlier) + carry[:, 0:1]
    rank1 = jnp.sum(jnp.where(hit1, prefix, 0.0), axis=0, keepdims=True)
    rank2 = jnp.sum(jnp.where(hit2, prefix, 0.0), axis=0, keepdims=True)
    carry = carry + jnp.sum(member, axis=1, keepdims=True)
    carry_ref[...] = carry
    cnt_ref[...] = carry

    zero = jnp.zeros_like(w1)
    info_ref[...] = jnp.concatenate([i1.astype(F32), i2.astype(F32), rank1, rank2, w1, w2, zero, zero], axis=0)


def _route(logits_t, tr=1024):
    T = logits_t.shape[1]
    return pl.pallas_call(
        _route_kernel,
        grid=(T // tr,),
        in_specs=[pl.BlockSpec((LANES, tr), lambda i: (0, i))],
        out_specs=[pl.BlockSpec((8, tr), lambda i: (0, i)),
                   pl.BlockSpec((MOE_N_EXPERTS, LANES), lambda i: (0, 0))],
        out_shape=[jax.ShapeDtypeStruct((8, T), F32), jax.ShapeDtypeStruct((MOE_N_EXPERTS, LANES), F32)],
        scratch_shapes=[pltpu.VMEM((MOE_N_EXPERTS, LANES), F32)],
        compiler_params=_cparams(("arbitrary",)),
        name="route",
    )(logits_t)


ROW_UNROLL = 8


def _dispatch_kernel(dest_ref, pend_ref, u_ref, xs_ref, zbuf, sem, zsem, *, td, T, nblk):
    @pl.when(pl.program_id(0) == 0)
    def _():
        zbuf[...] = jnp.zeros_like(zbuf)
        n_used = pend_ref[MOE_N_EXPERTS - 1] >> 8

        def zero_copy(blk):
            start = pl.multiple_of(blk * (MOE_ROWS * ROW_TILE), MOE_ROWS * ROW_TILE)
            return pltpu.make_async_copy(zbuf, xs_ref.at[pl.ds(start, MOE_ROWS * ROW_TILE)], zsem)

        def each_pad_block(fn):
            def per_expert(e, carry):
                prev = jnp.where(e > 0, pend_ref[jnp.maximum(e - 1, 0)], 0)

                @pl.when(pend_ref[e] > prev)
                def _():
                    fn((pend_ref[e] >> 8) - 1)
                return carry

            def per_tail(j, carry):
                @pl.when(n_used + j < nblk)
                def _():
                    fn(n_used + j)
                return carry

            lax.fori_loop(0, MOE_N_EXPERTS, per_expert, 0)
            lax.fori_loop(0, MOE_N_EXPERTS, per_tail, 0)

        each_pad_block(lambda blk: zero_copy(blk).start())
        each_pad_block(lambda blk: zero_copy(blk).wait())

    base = pl.program_id(0) * td

    def issue(g, carry):
        for j in range(ROW_UNROLL):
            r = g * ROW_UNROLL + j
            for k in range(MOE_TOP_K):
                _tile_copy(u_ref, r, xs_ref, dest_ref[k * T + base + r], sem).start(priority=k)
        return carry

    lax.fori_loop(0, td // ROW_UNROLL, issue, 0)
    for k in range(MOE_TOP_K):
        pltpu.make_async_copy(u_ref, xs_ref.at[pl.ds(0, td * ROW_TILE)], sem).wait()


def _dispatch(dest, pend, u2, cap, td=1024):
    T = u2.shape[0] // ROW_TILE
    return pl.pallas_call(
        functools.partial(_dispatch_kernel, td=td, T=T, nblk=cap // MOE_ROWS),
        grid_spec=pltpu.PrefetchScalarGridSpec(
            num_scalar_prefetch=2,
            grid=(T // td,),
            in_specs=[pl.BlockSpec((td * ROW_TILE, LANES), lambda i, d, z: (i, 0))],
            out_specs=pl.BlockSpec(memory_space=pl.ANY),
            scratch_shapes=[pltpu.VMEM((MOE_ROWS * ROW_TILE, LANES), jnp.int32),
                            pltpu.SemaphoreType.DMA(()), pltpu.SemaphoreType.DMA(())],
        ),
        out_shape=jax.ShapeDtypeStruct((cap * ROW_TILE, LANES), jnp.int32),
        compiler_params=_cparams(("arbitrary",)),
        name="dispatch",
    )(dest, pend, u2)


def _expert_kernel(pend_ref, xs_hbm, wg_hbm, wu_hbm, wd_hbm, ys_hbm,
                   xbuf, ybuf, zbuf, stage_g, stage_u, stage_d, wgb, wub, wdb, xsem, ysem, wsem, zsem, *, nblk):
    last = MOE_N_EXPERTS - 1
    n_used = pend_ref[last] >> 8
    block_rows = MOE_ROWS * ROW_TILE

    def x_copy(b, slot):
        start = pl.multiple_of(b * block_rows, block_rows)
        return pltpu.make_async_copy(xs_hbm.at[pl.ds(start, block_rows)], xbuf.at[slot], xsem.at[slot])

    def y_copy(b, slot):
        start = pl.multiple_of(b * block_rows, block_rows)
        return pltpu.make_async_copy(ybuf.at[slot], ys_hbm.at[pl.ds(start, block_rows)], ysem.at[slot])

    def zero_copy(b):
        start = pl.multiple_of(b * block_rows, block_rows)
        return pltpu.make_async_copy(zbuf, ys_hbm.at[pl.ds(start, block_rows)], zsem)

    def weight_copies(e):
        return (pltpu.make_async_copy(wg_hbm.at[e], stage_g, wsem.at[0]),
                pltpu.make_async_copy(wu_hbm.at[e], stage_u, wsem.at[1]),
                pltpu.make_async_copy(wd_hbm.at[e], stage_d, wsem.at[2]))

    def owner(start, row):
        return lax.while_loop(lambda e: (e < last) & (pend_ref[e] <= row), lambda e: e + 1, start)

    for c in weight_copies(owner(0, 0)):
        c.start()
    x_copy(0, 0).start()

    zbuf[...] = jnp.zeros_like(zbuf)

    def tail(fn):
        def step(b, carry):
            fn(b)
            return carry
        lax.fori_loop(n_used, nblk, step, 0)

    tail(lambda b: zero_copy(b).start())

    def body(b, cur):
        slot = b & 1
        e = owner(jnp.maximum(cur, 0), b * MOE_ROWS)
        x_copy(b, slot).wait()

        @pl.when(b + 1 < n_used)
        def _():
            x_copy(b + 1, 1 - slot).start()

        @pl.when(e != cur)
        def _():
            for c in weight_copies(e):
                c.wait()
            wgb[...] = stage_g[...].astype(BF16)
            wub[...] = stage_u[...].astype(BF16)
            wdb[...] = stage_d[...].astype(BF16)

            @pl.when(pend_ref[e] < pend_ref[last])
            def _():
                for c in weight_copies(owner(e + 1, pend_ref[e])):
                    c.start(priority=1)

        @pl.when(b >= 2)
        def _():
            y_copy(b - 2, slot).wait()

        xb = _unpack_rows(_from_row_tiles(xbuf.at[slot], MOE_ROWS))
        g = _dot(xb, wgb[...])
        u = _dot(xb, wub[...])
        hid = (g / (1.0 + jnp.exp(-g))) * u
        _to_row_tiles(ybuf.at[slot], _pack_rows(_dot(hid.astype(BF16), wdb[...])))
        y_copy(b, slot).start()
        return e

    lax.fori_loop(0, n_used, body, jnp.int32(-1))

    @pl.when(n_used >= 2)
    def _():
        y_copy(n_used - 2, n_used & 1).wait()
    y_copy(n_used - 1, (n_used - 1) & 1).wait()
    tail(lambda b: zero_copy(b).wait())


def _experts(pend, xs, w_gate, w_up, w_down):
    cap = xs.shape[0] // ROW_TILE
    nblk = cap // MOE_ROWS
    block = (MOE_ROWS * ROW_TILE, LANES)
    anywhere = pl.BlockSpec(memory_space=pl.ANY)
    return pl.pallas_call(
        functools.partial(_expert_kernel, nblk=nblk),
        grid_spec=pltpu.PrefetchScalarGridSpec(
            num_scalar_prefetch=1,
            grid=(1,),
            in_specs=[anywhere, anywhere, anywhere, anywhere],
            out_specs=anywhere,
            scratch_shapes=[pltpu.VMEM((2,) + block, jnp.int32),
                            pltpu.VMEM((2,) + block, jnp.int32),
                            pltpu.VMEM(block, jnp.int32),
                            pltpu.VMEM((D_MODEL, MOE_D_FF), F32),
                            pltpu.VMEM((D_MODEL, MOE_D_FF), F32),
                            pltpu.VMEM((MOE_D_FF, D_MODEL), F32),
                            pltpu.VMEM((D_MODEL, MOE_D_FF), BF16),
                            pltpu.VMEM((D_MODEL, MOE_D_FF), BF16),
                            pltpu.VMEM((MOE_D_FF, D_MODEL), BF16),
                            pltpu.SemaphoreType.DMA((2,)),
                            pltpu.SemaphoreType.DMA((2,)),
                            pltpu.SemaphoreType.DMA((3,)),
                            pltpu.SemaphoreType.DMA(())],
        ),
        out_shape=jax.ShapeDtypeStruct((cap * ROW_TILE, LANES), jnp.int32),
        compiler_params=_cparams(("arbitrary",)),
        name="experts",
    )(pend, xs, w_gate, w_up, w_down)


def _combine_kernel(dest_ref, ys_ref, info_ref, h_ref, fw_ref, o_ref, buf, sem, *, tc, T):
    i = pl.program_id(0)
    n = pl.num_programs(0)

    def issue(step, slot):
        base = step * tc

        def body(g, carry):
            for j in range(ROW_UNROLL):
                r = g * ROW_UNROLL + j
                for k in range(MOE_TOP_K):
                    _tile_copy(ys_ref, dest_ref[k * T + base + r], buf.at[slot, k], r,
                               sem.at[slot]).start(priority=k)
            return carry

        lax.fori_loop(0, tc // ROW_UNROLL, body, 0)

    @pl.when(i == 0)
    def _():
        issue(0, 0)

    slot = i % 2

    @pl.when(i + 1 < n)
    def _():
        issue(i + 1, 1 - slot)

    for k in range(MOE_TOP_K):
        pltpu.make_async_copy(ys_ref.at[pl.ds(0, tc * ROW_TILE)], buf.at[slot, k], sem.at[slot]).wait()

    info_t = jnp.concatenate([info_ref[...]] * (LANES // 8), axis=0).T
    w1 = info_t[:, INFO_W1:INFO_W1 + 1]
    w2 = info_t[:, INFO_W2:INFO_W2 + 1]
    y1 = _unpack_rows(_from_row_tiles(buf.at[slot, 0], tc)).astype(F32)
    y2 = _unpack_rows(_from_row_tiles(buf.at[slot, 1], tc)).astype(F32)
    h = h_ref[...] + (y1 * w1 + y2 * w2)
    o_ref[...] = _rms(h, fw_ref[...])


def _combine(dest, ys, info, h, final_w, tc=512):
    T = h.shape[0]
    return pl.pallas_call(
        functools.partial(_combine_kernel, tc=tc, T=T),
        grid_spec=pltpu.PrefetchScalarGridSpec(
            num_scalar_prefetch=1,
            grid=(T // tc,),
            in_specs=[pl.BlockSpec(memory_space=pl.ANY),
                      pl.BlockSpec((8, tc), lambda i, d: (0, i)),
                      pl.BlockSpec((tc, D_MODEL), lambda i, d: (i, 0)),
                      pl.BlockSpec((1, D_MODEL), lambda i, d: (0, 0))],
            out_specs=pl.BlockSpec((tc, D_MODEL), lambda i, d: (i, 0)),
            scratch_shapes=[pltpu.VMEM((2, MOE_TOP_K, tc * ROW_TILE, LANES), jnp.int32),
                            pltpu.SemaphoreType.DMA((2,))],
        ),
        out_shape=jax.ShapeDtypeStruct((T, D_MODEL), F32),
        compiler_params=_cparams(("arbitrary",)),
        name="combine",
    )(dest, ys, info, h, final_w[None, :])


def _plan_kernel(info_ref, cnt_ref, dest_ref, pend_ref):
    cnt = cnt_ref[...].astype(jnp.int32)
    nblk_e = ((cnt + (MOE_ROWS - 1)) >> 8).astype(F32)
    r = lax.broadcasted_iota(jnp.int32, (MOE_N_EXPERTS, MOE_N_EXPERTS), 0)
    c = lax.broadcasted_iota(jnp.int32, (MOE_N_EXPERTS, MOE_N_EXPERTS), 1)
    before = jnp.where(c < r, 1.0, 0.0).astype(BF16)
    first_blk = _dot(before, nblk_e.astype(BF16))
    pstart = first_blk[:, 0:1] * float(MOE_ROWS)
    pend_ref[...] = ((first_blk + nblk_e) * float(MOE_ROWS)).astype(jnp.int32)

    info = info_ref[...]
    erow = lax.broadcasted_iota(jnp.int32, (MOE_N_EXPERTS, info.shape[1]), 0)
    start_of = lambda e: jnp.sum(jnp.where(erow == e.astype(jnp.int32), pstart, 0.0), axis=0, keepdims=True)
    d1 = info[INFO_R1:INFO_R1 + 1] + start_of(info[INFO_E1:INFO_E1 + 1])
    d2 = info[INFO_R2:INFO_R2 + 1] + start_of(info[INFO_E2:INFO_E2 + 1])
    zero = jnp.zeros_like(d1)
    dest_ref[...] = jnp.concatenate([d1, d2] + [zero] * 6, axis=0).astype(jnp.int32)


def _plan(info, counts, tr=2048):
    T = info.shape[1]
    dest8, pend = pl.pallas_call(
        _plan_kernel,
        grid=(T // tr,),
        in_specs=[pl.BlockSpec((8, tr), lambda i: (0, i)),
                  pl.BlockSpec((MOE_N_EXPERTS, LANES), lambda i: (0, 0))],
        out_specs=[pl.BlockSpec((8, tr), lambda i: (0, i)),
                   pl.BlockSpec((MOE_N_EXPERTS, LANES), lambda i: (0, 0))],
        out_shape=[jax.ShapeDtypeStruct((8, T), jnp.int32),
                   jax.ShapeDtypeStruct((MOE_N_EXPERTS, LANES), jnp.int32)],
        compiler_params=_cparams(("arbitrary",)),
        name="plan",
    )(info, counts)
    return dest8[:MOE_TOP_K].reshape(-1), pend[:, 0]


def _moe_capacity(T):
    return (-(-(T * MOE_TOP_K) // MOE_ROWS) + MOE_N_EXPERTS) * MOE_ROWS


def _router_weights(router_group_w, router_group_b, router_expert_w, router_expert_b):
    we = jnp.transpose(router_expert_w, (0, 2, 1)).reshape(MOE_N_EXPERTS, D_MODEL)
    pad = LANES - MOE_N_EXPERTS - MOE_GROUPS
    wr = jnp.concatenate([we, router_group_w.T, jnp.zeros((pad, D_MODEL), F32)], axis=0)
    br = jnp.concatenate([router_expert_b.reshape(-1), router_group_b, jnp.zeros((pad,), F32)])[:, None]
    return wr, br


def kernel(x, norm1_w, w_in, gla_fwd_gate_w, gla_fwd_gate_b, gla_bwd_gate_w, gla_bwd_gate_b,
           gla_norm_w, w_out, norm2_w, router_group_w, router_group_b, router_expert_w,
           router_expert_b, expert_w_gate, expert_w_up, expert_w_down, final_norm_w):
    B, S, D = x.shape
    T = B * S
    assert norm1_w.shape[0] == 1, "single-layer trunk: the final norm is fused into the combine step"
    h = x.reshape(T, D)
    gla_slab, gate, loga, att_slab = _inproj(h, S, norm1_w[0], w_in[0], gla_fwd_gate_w[0], gla_fwd_gate_b[0],
                                       gla_bwd_gate_w[0], gla_bwd_gate_b[0])
    o_f, o_b = _gla(gla_slab, loga, B, S)
    att_out = _attention(att_slab.reshape(T, 3 * ATT_WIDTH), B, S)
    att_out = att_out.reshape(B, ATT_CLASSES, S // ATT_CLASSES, ATT_WIDTH)
    wr, br = _router_weights(router_group_w[0], router_group_b[0], router_expert_w[0], router_expert_b[0])
    h, u2, logits = _outproj(o_f, o_b, gate, att_out, h, gla_norm_w[0], w_out[0], norm2_w[0], wr, br)
    info, counts = _route(logits)
    dest, pend = _plan(info, counts)
    xs = _dispatch(dest, pend, u2, _moe_capacity(T))
    ys = _experts(pend, xs, expert_w_gate[0], expert_w_up[0], expert_w_down[0])
    out = _combine(dest, ys, info, h, final_norm_w)
    return out.reshape(B, S, D)
```

```python
import functools

import jax
import jax.numpy as jnp
import numpy as np
from jax import lax
from jax.experimental import pallas as pl
from jax.experimental.pallas import tpu as pltpu

F32 = jnp.float32
BF16 = jnp.bfloat16

D_MODEL = 1024
GLA_HEADS = 4
GLA_DV = 128
GLA_DK = 64
GLA_KEY_WIDTH = GLA_HEADS * GLA_DK
GLA_VAL_WIDTH = GLA_HEADS * GLA_DV
GLA_GATE_RANK = 16
GLA_TAU = 16.0
GLA_CHUNK = 64
ATT_WIDTH = 512
ATT_HEAD_DIM = 64
ATT_HEADS = 8
ROT_DIM = 16
ROPE_THETA = 500000.0
DILATED_PATTERNS = ((128, 1), (512, 4), (2048, 16))
ATT_RADIUS = 64
MOE_GROUPS = 4
MOE_EXPERTS_PER_GROUP = 8
MOE_N_EXPERTS = 32
MOE_TOP_K = 2
MOE_D_FF = 512
EPS = 1e-6
NEG_INF = -1e30
LOG2E = 1.4426950408889634

LANES = 128
MOE_ROWS = 256
VMEM_LIMIT = 56 * 1024 * 1024


def _cparams(sem):
    return pltpu.CompilerParams(dimension_semantics=sem, vmem_limit_bytes=VMEM_LIMIT)


def _dot(a, b):
    return jnp.dot(a, b, preferred_element_type=F32)


def _dot_nt(a, b):
    return lax.dot_general(a, b, (((1,), (1,)), ((), ())), preferred_element_type=F32)


def _dot_tn(a, b):
    return lax.dot_general(a, b, (((0,), (0,)), ((), ())), preferred_element_type=F32)


def _rms(x, w):
    return x * lax.rsqrt(jnp.mean(x * x, axis=-1, keepdims=True) + EPS) * w


def _inproj_kernel(x_ref, n1_ref, wg_ref, wlr_ref, wa_ref, gw_ref, gb_ref,
                   rc_ref, rs1_ref, rs2_ref, gla_ref, gate_ref, loga_ref, att_ref, stage_ref, wgb, wlrb):
    @pl.when(pl.program_id(0) == 0)
    def _():
        wgb[...] = wg_ref[...].astype(BF16)
        wlrb[...] = wlr_ref[...].astype(BF16)

    x = x_ref[...]
    ub = _rms(x, n1_ref[...]).astype(BF16)
    g = _dot(ub, wgb[...])
    qkv = 2 * GLA_KEY_WIDTH + GLA_VAL_WIDTH
    gla_ref[:, :GLA_KEY_WIDTH] = g[:, :GLA_KEY_WIDTH] * (GLA_DK ** -0.5)
    gla_ref[:, GLA_KEY_WIDTH:] = g[:, GLA_KEY_WIDTH:qkv]
    gate_ref[...] = g[:, qkv:].astype(BF16)
    lr = _dot(ub, wlrb[...])
    gate = _dot(lr.astype(BF16), gw_ref[...]) + gb_ref[...]
    loga_ref[...] = (jnp.minimum(gate, 0.0) - jnp.log(1.0 + jnp.exp(-jnp.abs(gate)))) * (1.0 / GLA_TAU)
    a = _dot(ub, wa_ref[...])
    qk = a[:, :2 * ATT_WIDTH]
    reps = 2 * ATT_WIDTH // LANES
    c = jnp.concatenate([rc_ref[...]] * reps, axis=1)
    s1 = jnp.concatenate([rs1_ref[...]] * reps, axis=1)
    s2 = jnp.concatenate([rs2_ref[...]] * reps, axis=1)
    half = ROT_DIM // 2
    n = 2 * ATT_WIDTH
    roped = qk * c + pltpu.roll(qk, n - half, 1) * s1 + pltpu.roll(qk, half, 1) * s2
    qkv = jnp.concatenate([roped[:, :ATT_WIDTH] * (ATT_HEAD_DIM ** -0.5 * LOG2E), roped[:, ATT_WIDTH:],
                           a[:, 2 * ATT_WIDTH:]], axis=1)
    rows = x.shape[0] // ATT_CLASSES
    for j in range(3 * ATT_WIDTH // LANES):
        cols = slice(j * LANES, (j + 1) * LANES)
        stage_ref[j] = qkv[:, cols]
        for c in range(ATT_CLASSES):
            att_ref[c, :, cols] = stage_ref[j, pl.ds(c, rows, stride=ATT_CLASSES), :]


def _rope_lane_tables(S):
    half = ROT_DIM // 2
    inv = np.float32(ROPE_THETA) ** (-(np.arange(0, ROT_DIM, 2, dtype=np.float32) / np.float32(ROT_DIM)))
    ang = np.arange(S, dtype=np.float32)[:, None] * inv[None, :].astype(np.float32)
    cos, sin = np.cos(ang), np.sin(ang)
    ones = np.ones((S, ATT_HEAD_DIM - ROT_DIM), np.float32)
    zeros = np.zeros((S, ATT_HEAD_DIM - ROT_DIM), np.float32)
    zeros8 = np.zeros((S, half), np.float32)
    rep = LANES // ATT_HEAD_DIM
    c = np.tile(np.concatenate([cos, cos, ones], axis=1), (1, rep))
    s1 = np.tile(np.concatenate([-sin, zeros8, zeros], axis=1), (1, rep))
    s2 = np.tile(np.concatenate([zeros8, sin, zeros], axis=1), (1, rep))
    return jnp.asarray(c), jnp.asarray(s1), jnp.asarray(s2)


def _inproj(x2, S, norm1_w, w_in, wf, bfw, wb, bbw, tm=512):
    T = x2.shape[0]
    o_lr = 2 * GLA_KEY_WIDTH + 2 * GLA_VAL_WIDTH
    o_att = o_lr + 2 * GLA_GATE_RANK
    wa = w_in[:, o_att:].astype(BF16)
    zeros = jnp.zeros((GLA_GATE_RANK, GLA_KEY_WIDTH), F32)
    gw = jnp.concatenate([jnp.concatenate([wf, zeros], axis=1), jnp.concatenate([zeros, wb], axis=1),
                          jnp.zeros((LANES - 2 * GLA_GATE_RANK, 2 * GLA_KEY_WIDTH), F32)], axis=0).astype(BF16)
    gb = jnp.concatenate([bfw, bbw])[None, :]
    rc, rs1, rs2 = _rope_lane_tables(S)
    nS = S // tm
    row = lambda i: (i, 0)
    const = lambda i: (0, 0)
    pos = lambda i: (i % nS, 0)
    return pl.pallas_call(
        _inproj_kernel,
        grid=(T // tm,),
        in_specs=[
            pl.BlockSpec((tm, D_MODEL), row),
            pl.BlockSpec((1, D_MODEL), const),
            pl.BlockSpec((D_MODEL, o_lr), const),
            pl.BlockSpec((D_MODEL, LANES), lambda i: (0, o_lr // LANES)),
            pl.BlockSpec((D_MODEL, 3 * ATT_WIDTH), const),
            pl.BlockSpec((LANES, 2 * GLA_KEY_WIDTH), const),
            pl.BlockSpec((1, 2 * GLA_KEY_WIDTH), const),
            pl.BlockSpec((tm, LANES), pos),
            pl.BlockSpec((tm, LANES), pos),
            pl.BlockSpec((tm, LANES), pos),
        ],
        out_specs=[
            pl.BlockSpec((tm, o_lr - GLA_VAL_WIDTH), row),
            pl.BlockSpec((tm, GLA_VAL_WIDTH), row),
            pl.BlockSpec((tm, 2 * GLA_KEY_WIDTH), row),
            pl.BlockSpec((None, ATT_CLASSES, tm // ATT_CLASSES, 3 * ATT_WIDTH),
                         lambda i: (i // nS, 0, i % nS, 0)),
        ],
        out_shape=[
            jax.ShapeDtypeStruct((T, o_lr - GLA_VAL_WIDTH), F32),
            jax.ShapeDtypeStruct((T, GLA_VAL_WIDTH), BF16),
            jax.ShapeDtypeStruct((T, 2 * GLA_KEY_WIDTH), F32),
            jax.ShapeDtypeStruct((T // S, ATT_CLASSES, S // ATT_CLASSES, 3 * ATT_WIDTH), F32),
        ],
        scratch_shapes=[pltpu.VMEM((3 * ATT_WIDTH // LANES, tm, LANES), F32),
                        pltpu.VMEM((D_MODEL, o_lr), BF16), pltpu.VMEM((D_MODEL, LANES), BF16)],
        compiler_params=_cparams(("arbitrary",)),
        name="inproj",
    )(x2, norm1_w[None, :], w_in, w_in, wa, gw, gb, rc, rs1, rs2)


def _gla_decays(q, k, v, la, forward, G):
    C = GLA_CHUNK
    R = G * C
    r = lax.broadcasted_iota(jnp.int32, (R, R), 0)
    c = lax.broadcasted_iota(jnp.int32, (R, R), 1)
    same = (r >> 6) == (c >> 6)
    tri = (c <= r) if forward else (c >= r)
    t_mat = jnp.where(same, jnp.where(tri, 1.0, 0.0), 0.0).astype(BF16)
    hi = la.astype(BF16)
    lo = (la - hi.astype(F32)).astype(BF16)
    b = _dot(t_mat, hi) + _dot(t_mat, lo)
    edge = C - 1 if forward else 0
    tot = jnp.concatenate([jnp.broadcast_to(b[g * C + edge:g * C + edge + 1], (C, GLA_KEY_WIDTH))
                           for g in range(G)], axis=0)
    order = list(range(G)) if forward else list(range(G - 1, -1, -1))
    return dict(q_dec=q * jnp.exp(b), k_inv=(k * jnp.exp(-b)).astype(BF16), k_end=k * jnp.exp(tot - b),
                tot=tot, vb=v.astype(BF16), order=order, forward=forward, G=G)


def _gla_scores(prep):
    C, H = GLA_CHUNK, GLA_HEADS
    lane_k = lax.broadcasted_iota(jnp.int32, (C, GLA_KEY_WIDTH), 1)
    qd_heads, scores = {}, {}
    for g in prep["order"]:
        rows = slice(g * C, (g + 1) * C)
        qd = prep["q_dec"][rows]
        qd_heads[g] = jnp.concatenate([jnp.where((lane_k >> 6) == h, qd, 0.0) for h in range(H)],
                                      axis=0).astype(BF16)
        scores[g] = _dot_nt(qd_heads[g], prep["k_inv"][rows])
    return qd_heads, scores


def _gla_chunk_updates(prep):
    C, H, G = GLA_CHUNK, GLA_HEADS, prep["G"]
    k_end, tot, vb = prep["k_end"], prep["tot"], prep["vb"]
    kv, dec_t = {}, {}
    lane = lax.broadcasted_iota(jnp.int32, (GLA_KEY_WIDTH, 2 * C), 1)
    zeros = jnp.zeros((C, GLA_DV), BF16)
    for p in range(G // 2):
        pair = slice(2 * p * C, (2 * p + 2) * C)
        ke_t = k_end[pair].T.astype(BF16)
        tot_t = tot[pair].T
        swapped = pltpu.roll(tot_t, C, 1)
        for half in range(2):
            g = 2 * p + half
            rows = slice(g * C, (g + 1) * C)
            own = (lane < C) if half == 0 else (lane >= C)
            dec_t[g] = jnp.exp(jnp.where(own, tot_t, swapped))
            parts = []
            for h in range(H):
                v_h = vb[rows, h * GLA_DV:(h + 1) * GLA_DV]
                v_pad = jnp.concatenate([v_h, zeros] if half == 0 else [zeros, v_h], axis=0)
                parts.append(_dot(ke_t[h * C:(h + 1) * C], v_pad))
            kv[g] = jnp.concatenate(parts, axis=0)
    return kv, dec_t


def _gla_states(prep, kv, dec_t, s_ref):
    st = s_ref[...]
    states = {}
    for g in prep["order"]:
        states[g] = st.astype(BF16)
        st = st * dec_t[g] + kv[g]
    s_ref[...] = st
    return states


def _gla_outputs(prep, qd_heads, scores, inter, o_ref):
    C, H = GLA_CHUNK, GLA_HEADS
    row_q = lax.broadcasted_iota(jnp.int32, (H * C, C), 0) & (C - 1)
    col_k = lax.broadcasted_iota(jnp.int32, (H * C, C), 1)
    a_mask = (col_k <= row_q) if prep["forward"] else (col_k >= row_q)
    for g in prep["order"]:
        rows = slice(g * C, (g + 1) * C)
        a = jnp.where(a_mask, scores[g], 0.0).astype(BF16)
        vv = prep["vb"][rows]
        o_ref[rows, :] = jnp.concatenate(
            [_dot(a[h * C:(h + 1) * C], vv[:, h * GLA_DV:(h + 1) * GLA_DV]) + inter[g][h * C:(h + 1) * C]
             for h in range(H)], axis=1).astype(o_ref.dtype)


def _gla_kernel(qf_ref, kf_ref, vf_ref, laf_ref, qb_ref, kb_ref, vb_ref, lab_ref,
                of_ref, ob_ref, sf_ref, sb_ref, *, G):
    @pl.when(pl.program_id(1) == 0)
    def _():
        sf_ref[...] = jnp.zeros_like(sf_ref)
        sb_ref[...] = jnp.zeros_like(sb_ref)

    dirs = [(_gla_decays(qf_ref[...], kf_ref[...], vf_ref[...], laf_ref[...], True, G), sf_ref, of_ref),
            (_gla_decays(qb_ref[...], kb_ref[...], vb_ref[...], lab_ref[...], False, G), sb_ref, ob_ref)]
    scored = [_gla_scores(prep) for prep, _, _ in dirs]
    updates = [_gla_chunk_updates(prep) for prep, _, _ in dirs]
    states = [_gla_states(prep, kv, dec_t, s_ref) for (prep, s_ref, _), (kv, dec_t) in zip(dirs, updates)]
    inters = [{g: _dot(qd_heads[g], st[g]) for g in prep["order"]}
              for (prep, _, _), (qd_heads, _), st in zip(dirs, scored, states)]
    for (prep, _, o_ref), (qd_heads, scores), inter in zip(dirs, scored, inters):
        _gla_outputs(prep, qd_heads, scores, inter, o_ref)


def _gla(gla_slab, loga, B, S, G=8):
    T = B * S
    R = G * GLA_CHUNK
    ns = S // R
    fwd = lambda col: (lambda b, i: (b * ns + i, col))
    bwd = lambda col: (lambda b, i: (b * ns + ns - 1 - i, col))
    kw, vw = GLA_KEY_WIDTH, GLA_VAL_WIDTH
    return pl.pallas_call(
        functools.partial(_gla_kernel, G=G),
        grid=(B, ns),
        in_specs=[
            pl.BlockSpec((R, kw), fwd(0)), pl.BlockSpec((R, kw), fwd(1)),
            pl.BlockSpec((R, vw), fwd(1)), pl.BlockSpec((R, kw), fwd(0)),
            pl.BlockSpec((R, kw), bwd(0)), pl.BlockSpec((R, kw), bwd(1)),
            pl.BlockSpec((R, vw), bwd(1)), pl.BlockSpec((R, kw), bwd(1)),
        ],
        out_specs=[pl.BlockSpec((R, vw), fwd(0)), pl.BlockSpec((R, vw), bwd(0))],
        out_shape=[jax.ShapeDtypeStruct((T, vw), BF16), jax.ShapeDtypeStruct((T, vw), BF16)],
        scratch_shapes=[pltpu.VMEM((kw, GLA_DV), F32), pltpu.VMEM((kw, GLA_DV), F32)],
        compiler_params=_cparams(("arbitrary", "arbitrary")),
        name="gla",
    )(gla_slab, gla_slab, gla_slab, loga, gla_slab, gla_slab, gla_slab, loga)


ATT_CLASSES = 4
ATT_QB = 128
ATT_KB = ATT_QB + 2 * ATT_RADIUS


ATT_UNROLL = 4


def _att_kernel(q_ref, k_ref, v_ref, o_ref, m_ref, l_ref, bias_ref, *, S):
    QB, KB, NC = ATT_QB, ATT_KB, ATT_CLASSES
    L4 = S // NC
    lane = lax.broadcasted_iota(jnp.int32, (QB, LANES), 1)
    head0 = lane < ATT_HEAD_DIM

    @pl.when((pl.program_id(0) == 0) & (pl.program_id(1) == 0))
    def _():
        rowi = lax.broadcasted_iota(jnp.int32, (2 * QB, KB), 0) & (QB - 1)
        coli = lax.broadcasted_iota(jnp.int32, (2 * QB, KB), 1)
        qpos = (rowi & (QB // NC - 1)) * NC + (rowi >> 5)
        kpos = (coli & (KB // NC - 1)) * NC + (coli >> 6)
        for case in range(3):
            bias_ref[0, case] = jnp.where(jnp.abs(rowi - coli + case * ATT_RADIUS) <= ATT_RADIUS, 0.0, NEG_INF)
            bias_ref[1, case] = jnp.where(jnp.abs(qpos - kpos + case * ATT_RADIUS) <= ATT_RADIUS, 0.0, NEG_INF)

    for pi, (_, d) in enumerate(DILATED_PATTERNS):
        L = S // d
        nb = L // QB
        shift = nb.bit_length() - 1
        first = pi == 0
        last = pi == len(DILATED_PATTERNS) - 1

        def scores(n, d=d, L=L, nb=nb, shift=shift):
            cls = n >> shift
            q0 = (n & (nb - 1)) * QB
            ws = jnp.clip(q0 - ATT_RADIUS, 0, L - KB)
            if d == 1:
                qsls = [pl.ds(pl.multiple_of(c * L4 + q0 // NC, QB // NC), QB // NC) for c in range(NC)]
                ksls = [pl.ds(pl.multiple_of(c * L4 + ws // NC, ATT_RADIUS // NC), KB // NC) for c in range(NC)]
            elif d == NC:
                qsls = [pl.ds(pl.multiple_of(cls * L4 + q0, QB), QB)]
                ksls = [pl.ds(pl.multiple_of(cls * L4 + ws, ATT_RADIUS), KB)]
            else:
                base = (cls & (NC - 1)) * L4 + (cls >> 2)
                qsls = [pl.ds(base + NC * q0, QB, stride=NC)]
                ksls = [pl.ds(base + NC * ws, KB, stride=NC)]
            q = jnp.concatenate([q_ref[sl, :] for sl in qsls], axis=0)
            kw = jnp.concatenate([k_ref[sl, :] for sl in ksls], axis=0)
            q2 = jnp.concatenate([jnp.where(head0, q, 0.0), jnp.where(head0, 0.0, q)], axis=0).astype(BF16)
            s = _dot_nt(q2, kw.astype(BF16))
            return qsls, ksls, s + bias_ref[1 if d == 1 else 0, (q0 - ws) >> 6]

        def softmax_pv(qsls, ksls, s):
            m_blk = jnp.max(s, axis=-1, keepdims=True)
            p = jnp.exp2(s - m_blk)
            vw = jnp.concatenate([v_ref[sl, :] for sl in ksls], axis=0)
            v_ones = jnp.concatenate([vw.astype(BF16), jnp.ones((KB, LANES), BF16)], axis=1)
            pv = _dot(p.astype(BF16), v_ones)
            acc_b = jnp.where(head0, pv[:QB, :LANES], pv[QB:, :LANES])
            m_b = jnp.where(head0, m_blk[:QB], m_blk[QB:])
            l_b = jnp.where(head0, pv[:QB, LANES:], pv[QB:, LANES:])
            return qsls, acc_b, m_b, l_b

        def load(ref, sls):
            return jnp.concatenate([ref[sl, :] for sl in sls], axis=0)

        def store(ref, sls, val):
            n = val.shape[0] // len(sls)
            for i, sl in enumerate(sls):
                ref[sl, :] = val[i * n:(i + 1) * n]

        def body(n, carry, first=first, last=last):
            staged = [scores(n * ATT_UNROLL + u) for u in range(ATT_UNROLL)]
            blocks = [softmax_pv(*st) for st in staged]
            for qsls, acc_b, m_b, l_b in blocks:
                if first:
                    acc, m_new, l_new = acc_b, m_b, l_b
                else:
                    m_old = load(m_ref, qsls)
                    m_new = jnp.maximum(m_old, m_b)
                    w_old = jnp.exp2(m_old - m_new)
                    w_blk = jnp.exp2(m_b - m_new)
                    acc = load(o_ref, qsls) * w_old + acc_b * w_blk
                    l_new = load(l_ref, qsls) * w_old + l_b * w_blk
                if last:
                    store(o_ref, qsls, acc / l_new)
                else:
                    store(o_ref, qsls, acc)
                    store(m_ref, qsls, m_new)
                    store(l_ref, qsls, l_new)
            return carry

        lax.fori_loop(0, S // (QB * ATT_UNROLL), body, 0)


def _attention(att_slab, B, S):
    T = B * S
    ncol = ATT_WIDTH // LANES
    return pl.pallas_call(
        functools.partial(_att_kernel, S=S),
        grid=(B, ncol),
        in_specs=[
            pl.BlockSpec((S, LANES), lambda b, h: (b, h)),
            pl.BlockSpec((S, LANES), lambda b, h: (b, ncol + h)),
            pl.BlockSpec((S, LANES), lambda b, h: (b, 2 * ncol + h)),
        ],
        out_specs=pl.BlockSpec((S, LANES), lambda b, h: (b, h)),
        out_shape=jax.ShapeDtypeStruct((T, ATT_WIDTH), F32),
        scratch_shapes=[pltpu.VMEM((S, LANES), F32), pltpu.VMEM((S, LANES), F32),
                        pltpu.VMEM((2, 3, 2 * ATT_QB, ATT_KB), F32)],
        compiler_params=_cparams(("arbitrary", "arbitrary")),
        name="dilated_attention",
    )(att_slab, att_slab, att_slab)


PACK_WORDS = D_MODEL // 2
ROW_TILE = PACK_WORDS // LANES
HIGH_HALF = -65536


def _pack_rows(x):
    bits = lambda v: lax.bitcast_convert_type(v.astype(BF16).astype(F32), jnp.int32)
    low = (bits(x[:, :PACK_WORDS]) >> 16) & 0xFFFF
    return (bits(x[:, PACK_WORDS:]) & HIGH_HALF) | low


def _unpack_rows(w):
    low = lax.bitcast_convert_type(w << 16, F32)
    high = lax.bitcast_convert_type(w & HIGH_HALF, F32)
    return jnp.concatenate([low, high], axis=1).astype(BF16)


def _to_row_tiles(ref, w):
    n = w.shape[0]
    for j in range(ROW_TILE):
        ref[pl.ds(j, n, stride=ROW_TILE), :] = w[:, j * LANES:(j + 1) * LANES]


def _from_row_tiles(ref, n):
    return jnp.concatenate([ref[pl.ds(j, n, stride=ROW_TILE), :] for j in range(ROW_TILE)], axis=1)


def _tile_copy(src_ref, src_row, dst_ref, dst_row, sem):
    src = pl.ds(pl.multiple_of(src_row * ROW_TILE, ROW_TILE), ROW_TILE)
    dst = pl.ds(pl.multiple_of(dst_row * ROW_TILE, ROW_TILE), ROW_TILE)
    return pltpu.make_async_copy(src_ref.at[src], dst_ref.at[dst], sem)


def _outproj_kernel(of_ref, ob_ref, gg_ref, att_ref, x_ref, gnw_ref, wo1_ref, wo2_ref,
                    n2_ref, wr_ref, br_ref, h_ref, u_ref, lg_ref, stage_ref):
    rows = stage_ref.shape[1] // ATT_CLASSES
    for j in range(ATT_WIDTH // LANES):
        for c in range(ATT_CLASSES):
            stage_ref[j, pl.ds(c, rows, stride=ATT_CLASSES), :] = att_ref[c, :, j * LANES:(j + 1) * LANES]
    att = jnp.concatenate([stage_ref[j] for j in range(ATT_WIDTH // LANES)], axis=1)
    o = of_ref[...].astype(F32) + ob_ref[...].astype(F32)
    gate = gg_ref[...].astype(F32)
    gnw = gnw_ref[...]
    parts = []
    for h in range(GLA_HEADS):
        sl = slice(h * GLA_DV, (h + 1) * GLA_DV)
        parts.append(_rms(o[:, sl], gnw))
    y = jnp.concatenate(parts, axis=1) * (gate / (1.0 + jnp.exp(-gate)))
    mix = _dot(y.astype(BF16), wo1_ref[...]) + _dot(att.astype(BF16), wo2_ref[...])
    h = x_ref[...] + mix
    h_ref[...] = h
    u = _rms(h, n2_ref[...])
    _to_row_tiles(u_ref, _pack_rows(u))
    u_hi = u.astype(BF16)
    u_lo = (u - u_hi.astype(F32)).astype(BF16)
    hi_both = _dot_nt(wr_ref[...], u_hi)
    lg_ref[...] = (hi_both[:LANES] + hi_both[LANES:] + _dot_nt(wr_ref[:LANES], u_lo)) + br_ref[...]


def _outproj(o_f, o_b, gate, att_out, x2, gla_norm_w, w_out, norm2_w, wr, br, tm=512):
    T = x2.shape[0]
    nS = att_out.shape[2] * ATT_CLASSES // tm
    row = lambda i: (i, 0)
    const = lambda i: (0, 0)
    wo = w_out.astype(BF16)
    wr_hi = wr.astype(BF16)
    wr_lo = (wr - wr_hi.astype(F32)).astype(BF16)
    wr = jnp.concatenate([wr_hi, wr_lo], axis=0)
    return pl.pallas_call(
        _outproj_kernel,
        grid=(T // tm,),
        in_specs=[
            pl.BlockSpec((tm, GLA_VAL_WIDTH), row),
            pl.BlockSpec((tm, GLA_VAL_WIDTH), row),
            pl.BlockSpec((tm, GLA_VAL_WIDTH), row),
            pl.BlockSpec((None, ATT_CLASSES, tm // ATT_CLASSES, ATT_WIDTH), lambda i: (i // nS, 0, i % nS, 0)),
            pl.BlockSpec((tm, D_MODEL), row),
            pl.BlockSpec((1, GLA_DV), const),
            pl.BlockSpec((GLA_VAL_WIDTH, D_MODEL), const),
            pl.BlockSpec((ATT_WIDTH, D_MODEL), lambda i: (GLA_VAL_WIDTH // ATT_WIDTH, 0)),
            pl.BlockSpec((1, D_MODEL), const),
            pl.BlockSpec((2 * LANES, D_MODEL), const),
            pl.BlockSpec((LANES, 1), const),
        ],
        out_specs=[
            pl.BlockSpec((tm, D_MODEL), row),
            pl.BlockSpec((tm * ROW_TILE, LANES), row),
            pl.BlockSpec((LANES, tm), lambda i: (0, i)),
        ],
        out_shape=[
            jax.ShapeDtypeStruct((T, D_MODEL), F32),
            jax.ShapeDtypeStruct((T * ROW_TILE, LANES), jnp.int32),
            jax.ShapeDtypeStruct((LANES, T), F32),
        ],
        scratch_shapes=[pltpu.VMEM((ATT_WIDTH // LANES, tm, LANES), F32)],
        compiler_params=_cparams(("arbitrary",)),
        name="outproj",
    )(o_f, o_b, gate, att_out, x2, gla_norm_w[None, :], wo, wo,
      norm2_w[None, :], wr, br)


INFO_E1, INFO_E2, INFO_R1, INFO_R2, INFO_W1, INFO_W2 = range(6)
ROUTE_ROWS = 40


def _route_kernel(lg_ref, info_ref, cnt_ref, carry_ref):
    @pl.when(pl.program_id(0) == 0)
    def _():
        carry_ref[...] = jnp.zeros_like(carry_ref)

    lg = lg_ref[:ROUTE_ROWS, :]
    tr = lg.shape[1]
    row = lax.broadcasted_iota(jnp.int32, (ROUTE_ROWS, tr), 0)
    big = jnp.int32(1 << 20)
    is_g = (row >= MOE_N_EXPERTS) & (row < MOE_N_EXPERTS + MOE_GROUPS)
    gl = jnp.where(is_g, lg, -jnp.inf)
    gmax = jnp.max(gl, axis=0, keepdims=True)
    gsel = jnp.min(jnp.where(gl == gmax, row - MOE_N_EXPERTS, big), axis=0, keepdims=True)
    g_w = 1.0 / jnp.sum(jnp.where(is_g, jnp.exp(lg - gmax), 0.0), axis=0, keepdims=True)
    in_grp = (row < MOE_N_EXPERTS) & ((row >> 3) == gsel)
    el = jnp.where(in_grp, lg, -jnp.inf)
    v1 = jnp.max(el, axis=0, keepdims=True)
    i1 = jnp.min(jnp.where(el == v1, row, big), axis=0, keepdims=True)
    el2 = jnp.where(row == i1, -jnp.inf, el)
    v2 = jnp.max(el2, axis=0, keepdims=True)
    i2 = jnp.min(jnp.where(el2 == v2, row, big), axis=0, keepdims=True)
    t = jnp.exp(v2 - v1)
    w1 = g_w * (1.0 / (1.0 + t))
    w2 = g_w * (t / (1.0 + t))

    erow = lax.broadcasted_iota(jnp.int32, (MOE_N_EXPERTS, tr), 0)
    hit1 = erow == i1
    hit2 = erow == i2
    member = jnp.where(hit1 | hit2, 1.0, 0.0)
    r = lax.broadcasted_iota(jnp.int32, (tr, tr), 0)
    c = lax.broadcasted_iota(jnp.int32, (tr, tr), 1)
    earlier = jnp.where(r < c, 1.0, 0.0).astype(BF16)
    carry = carry_ref[...]
    prefix = _dot(member.astype(BF16), earlier) + carry[:, 0:1]
    rank1 = jnp.sum(jnp.where(hit1, prefix, 0.0), axis=0, keepdims=True)
    rank2 = jnp.sum(jnp.where(hit2, prefix, 0.0), axis=0, keepdims=True)
    carry = carry + jnp.sum(member, axis=1, keepdims=True)
    carry_ref[...] = carry
    cnt_ref[...] = carry

    zero = jnp.zeros_like(w1)
    info_ref[...] = jnp.concatenate([i1.astype(F32), i2.astype(F32), rank1, rank2, w1, w2, zero, zero], axis=0)


def _route(logits_t, tr=1024):
    T = logits_t.shape[1]
    return pl.pallas_call(
        _route_kernel,
        grid=(T // tr,),
        in_specs=[pl.BlockSpec((LANES, tr), lambda i: (0, i))],
        out_specs=[pl.BlockSpec((8, tr), lambda i: (0, i)),
                   pl.BlockSpec((MOE_N_EXPERTS, LANES), lambda i: (0, 0))],
        out_shape=[jax.ShapeDtypeStruct((8, T), F32), jax.ShapeDtypeStruct((MOE_N_EXPERTS, LANES), F32)],
        scratch_shapes=[pltpu.VMEM((MOE_N_EXPERTS, LANES), F32)],
        compiler_params=_cparams(("arbitrary",)),
        name="route",
    )(logits_t)


ROW_UNROLL = 16


def _dispatch_kernel(dest_ref, pend_ref, u_ref, xs_ref, zbuf, sem, zsem, *, td, T, nblk):
    @pl.when(pl.program_id(0) == 0)
    def _():
        zbuf[...] = jnp.zeros_like(zbuf)
        n_used = pend_ref[MOE_N_EXPERTS - 1] >> 8

        def zero_copy(blk):
            start = pl.multiple_of(blk * (MOE_ROWS * ROW_TILE), MOE_ROWS * ROW_TILE)
            return pltpu.make_async_copy(zbuf, xs_ref.at[pl.ds(start, MOE_ROWS * ROW_TILE)], zsem)

        def each_pad_block(fn):
            def per_expert(e, carry):
                prev = jnp.where(e > 0, pend_ref[jnp.maximum(e - 1, 0)], 0)

                @pl.when(pend_ref[e] > prev)
                def _():
                    fn((pend_ref[e] >> 8) - 1)
                return carry

            def per_tail(j, carry):
                @pl.when(n_used + j < nblk)
                def _():
                    fn(n_used + j)
                return carry

            lax.fori_loop(0, MOE_N_EXPERTS, per_expert, 0)
            lax.fori_loop(0, MOE_N_EXPERTS, per_tail, 0)

        each_pad_block(lambda blk: zero_copy(blk).start())
        each_pad_block(lambda blk: zero_copy(blk).wait())

    base = pl.program_id(0) * td

    def issue(g, carry):
        for j in range(ROW_UNROLL):
            r = g * ROW_UNROLL + j
            for k in range(MOE_TOP_K):
                _tile_copy(u_ref, r, xs_ref, dest_ref[k * T + base + r], sem).start(priority=k)
        return carry

    lax.fori_loop(0, td // ROW_UNROLL, issue, 0)
    for k in range(MOE_TOP_K):
        pltpu.make_async_copy(u_ref, xs_ref.at[pl.ds(0, td * ROW_TILE)], sem).wait()


def _dispatch(dest, pend, u2, cap, td=1024):
    T = u2.shape[0] // ROW_TILE
    return pl.pallas_call(
        functools.partial(_dispatch_kernel, td=td, T=T, nblk=cap // MOE_ROWS),
        grid_spec=pltpu.PrefetchScalarGridSpec(
            num_scalar_prefetch=2,
            grid=(T // td,),
            in_specs=[pl.BlockSpec((td * ROW_TILE, LANES), lambda i, d, z: (i, 0))],
            out_specs=pl.BlockSpec(memory_space=pl.ANY),
            scratch_shapes=[pltpu.VMEM((MOE_ROWS * ROW_TILE, LANES), jnp.int32),
                            pltpu.SemaphoreType.DMA(()), pltpu.SemaphoreType.DMA(())],
        ),
        out_shape=jax.ShapeDtypeStruct((cap * ROW_TILE, LANES), jnp.int32),
        compiler_params=_cparams(("arbitrary",)),
        name="dispatch",
    )(dest, pend, u2)


def _expert_kernel(pend_ref, xs_hbm, wg_hbm, wu_hbm, wd_hbm, ys_hbm,
                   xbuf, ybuf, zbuf, stage_g, stage_u, stage_d, wgb, wub, wdb, xsem, ysem, wsem, zsem, *, nblk):
    last = MOE_N_EXPERTS - 1
    n_used = pend_ref[last] >> 8
    block_rows = MOE_ROWS * ROW_TILE

    def x_copy(b, slot):
        start = pl.multiple_of(b * block_rows, block_rows)
        return pltpu.make_async_copy(xs_hbm.at[pl.ds(start, block_rows)], xbuf.at[slot], xsem.at[slot])

    def y_copy(b, slot):
        start = pl.multiple_of(b * block_rows, block_rows)
        return pltpu.make_async_copy(ybuf.at[slot], ys_hbm.at[pl.ds(start, block_rows)], ysem.at[slot])

    def zero_copy(b):
        start = pl.multiple_of(b * block_rows, block_rows)
        return pltpu.make_async_copy(zbuf, ys_hbm.at[pl.ds(start, block_rows)], zsem)

    def weight_copies(e):
        return (pltpu.make_async_copy(wg_hbm.at[e], stage_g, wsem.at[0]),
                pltpu.make_async_copy(wu_hbm.at[e], stage_u, wsem.at[1]),
                pltpu.make_async_copy(wd_hbm.at[e], stage_d, wsem.at[2]))

    def owner(start, row):
        return lax.while_loop(lambda e: (e < last) & (pend_ref[e] <= row), lambda e: e + 1, start)

    for c in weight_copies(owner(0, 0)):
        c.start()
    x_copy(0, 0).start()

    zbuf[...] = jnp.zeros_like(zbuf)

    def tail(fn):
        def step(b, carry):
            fn(b)
            return carry
        lax.fori_loop(n_used, nblk, step, 0)

    tail(lambda b: zero_copy(b).start())

    def body(b, cur):
        slot = b & 1
        e = owner(jnp.maximum(cur, 0), b * MOE_ROWS)
        x_copy(b, slot).wait()

        @pl.when(b + 1 < n_used)
        def _():
            x_copy(b + 1, 1 - slot).start()

        @pl.when(e != cur)
        def _():
            for c in weight_copies(e):
                c.wait()
            wgb[...] = stage_g[...].astype(BF16)
            wub[...] = stage_u[...].astype(BF16)
            wdb[...] = stage_d[...].astype(BF16)

            @pl.when(pend_ref[e] < pend_ref[last])
            def _():
                for c in weight_copies(owner(e + 1, pend_ref[e])):
                    c.start(priority=1)

        @pl.when(b >= 2)
        def _():
            y_copy(b - 2, slot).wait()

        xb = _unpack_rows(_from_row_tiles(xbuf.at[slot], MOE_ROWS))
        g = _dot(xb, wgb[...])
        u = _dot(xb, wub[...])
        hid = (g / (1.0 + jnp.exp(-g))) * u
        _to_row_tiles(ybuf.at[slot], _pack_rows(_dot(hid.astype(BF16), wdb[...])))
        y_copy(b, slot).start()
        return e

    lax.fori_loop(0, n_used, body, jnp.int32(-1))

    @pl.when(n_used >= 2)
    def _():
        y_copy(n_used - 2, n_used & 1).wait()
    y_copy(n_used - 1, (n_used - 1) & 1).wait()
    tail(lambda b: zero_copy(b).wait())


def _experts(pend, xs, w_gate, w_up, w_down):
    cap = xs.shape[0] // ROW_TILE
    nblk = cap // MOE_ROWS
    block = (MOE_ROWS * ROW_TILE, LANES)
    anywhere = pl.BlockSpec(memory_space=pl.ANY)
    return pl.pallas_call(
        functools.partial(_expert_kernel, nblk=nblk),
        grid_spec=pltpu.PrefetchScalarGridSpec(
            num_scalar_prefetch=1,
            grid=(1,),
            in_specs=[anywhere, anywhere, anywhere, anywhere],
            out_specs=anywhere,
            scratch_shapes=[pltpu.VMEM((2,) + block, jnp.int32),
                            pltpu.VMEM((2,) + block, jnp.int32),
                            pltpu.VMEM(block, jnp.int32),
                            pltpu.VMEM((D_MODEL, MOE_D_FF), F32),
                            pltpu.VMEM((D_MODEL, MOE_D_FF), F32),
                            pltpu.VMEM((MOE_D_FF, D_MODEL), F32),
                            pltpu.VMEM((D_MODEL, MOE_D_FF), BF16),
                            pltpu.VMEM((D_MODEL, MOE_D_FF), BF16),
                            pltpu.VMEM((MOE_D_FF, D_MODEL), BF16),
                            pltpu.SemaphoreType.DMA((2,)),
                            pltpu.SemaphoreType.DMA((2,)),
                            pltpu.SemaphoreType.DMA((3,)),
                            pltpu.SemaphoreType.DMA(())],
        ),
        out_shape=jax.ShapeDtypeStruct((cap * ROW_TILE, LANES), jnp.int32),
        compiler_params=_cparams(("arbitrary",)),
        name="experts",
    )(pend, xs, w_gate, w_up, w_down)


def _combine_kernel(dest_ref, ys_ref, info_ref, h_ref, fw_ref, o_ref, buf, sem, *, tc, T):
    i = pl.program_id(0)
    n = pl.num_programs(0)

    def issue(step, slot):
        base = step * tc

        def body(g, carry):
            for j in range(ROW_UNROLL):
                r = g * ROW_UNROLL + j
                for k in range(MOE_TOP_K):
                    _tile_copy(ys_ref, dest_ref[k * T + base + r], buf.at[slot, k], r,
                               sem.at[slot]).start(priority=k)
            return carry

        lax.fori_loop(0, tc // ROW_UNROLL, body, 0)

    @pl.when(i == 0)
    def _():
        issue(0, 0)

    slot = i % 2

    @pl.when(i + 1 < n)
    def _():
        issue(i + 1, 1 - slot)

    for k in range(MOE_TOP_K):
        pltpu.make_async_copy(ys_ref.at[pl.ds(0, tc * ROW_TILE)], buf.at[slot, k], sem.at[slot]).wait()

    info_t = jnp.concatenate([info_ref[...]] * (LANES // 8), axis=0).T
    w1 = info_t[:, INFO_W1:INFO_W1 + 1]
    w2 = info_t[:, INFO_W2:INFO_W2 + 1]
    y1 = _unpack_rows(_from_row_tiles(buf.at[slot, 0], tc)).astype(F32)
    y2 = _unpack_rows(_from_row_tiles(buf.at[slot, 1], tc)).astype(F32)
    h = h_ref[...] + (y1 * w1 + y2 * w2)
    o_ref[...] = _rms(h, fw_ref[...])


def _combine(dest, ys, info, h, final_w, tc=512):
    T = h.shape[0]
    return pl.pallas_call(
        functools.partial(_combine_kernel, tc=tc, T=T),
        grid_spec=pltpu.PrefetchScalarGridSpec(
            num_scalar_prefetch=1,
            grid=(T // tc,),
            in_specs=[pl.BlockSpec(memory_space=pl.ANY),
                      pl.BlockSpec((8, tc), lambda i, d: (0, i)),
                      pl.BlockSpec((tc, D_MODEL), lambda i, d: (i, 0)),
                      pl.BlockSpec((1, D_MODEL), lambda i, d: (0, 0))],
            out_specs=pl.BlockSpec((tc, D_MODEL), lambda i, d: (i, 0)),
            scratch_shapes=[pltpu.VMEM((2, MOE_TOP_K, tc * ROW_TILE, LANES), jnp.int32),
                            pltpu.SemaphoreType.DMA((2,))],
        ),
        out_shape=jax.ShapeDtypeStruct((T, D_MODEL), F32),
        compiler_params=_cparams(("arbitrary",)),
        name="combine",
    )(dest, ys, info, h, final_w[None, :])


def _plan_kernel(info_ref, cnt_ref, dest_ref, pend_ref):
    cnt = cnt_ref[...].astype(jnp.int32)
    nblk_e = ((cnt + (MOE_ROWS - 1)) >> 8).astype(F32)
    r = lax.broadcasted_iota(jnp.int32, (MOE_N_EXPERTS, MOE_N_EXPERTS), 0)
    c = lax.broadcasted_iota(jnp.int32, (MOE_N_EXPERTS, MOE_N_EXPERTS), 1)
    before = jnp.where(c < r, 1.0, 0.0).astype(BF16)
    first_blk = _dot(before, nblk_e.astype(BF16))
    pstart = first_blk[:, 0:1] * float(MOE_ROWS)
    pend_ref[...] = ((first_blk + nblk_e) * float(MOE_ROWS)).astype(jnp.int32)

    info = info_ref[...]
    erow = lax.broadcasted_iota(jnp.int32, (MOE_N_EXPERTS, info.shape[1]), 0)
    start_of = lambda e: jnp.sum(jnp.where(erow == e.astype(jnp.int32), pstart, 0.0), axis=0, keepdims=True)
    d1 = info[INFO_R1:INFO_R1 + 1] + start_of(info[INFO_E1:INFO_E1 + 1])
    d2 = info[INFO_R2:INFO_R2 + 1] + start_of(info[INFO_E2:INFO_E2 + 1])
    zero = jnp.zeros_like(d1)
    dest_ref[...] = jnp.concatenate([d1, d2] + [zero] * 6, axis=0).astype(jnp.int32)


def _plan(info, counts, tr=2048):
    T = info.shape[1]
    dest8, pend = pl.pallas_call(
        _plan_kernel,
        grid=(T // tr,),
        in_specs=[pl.BlockSpec((8, tr), lambda i: (0, i)),
                  pl.BlockSpec((MOE_N_EXPERTS, LANES), lambda i: (0, 0))],
        out_specs=[pl.BlockSpec((8, tr), lambda i: (0, i)),
                   pl.BlockSpec((MOE_N_EXPERTS, LANES), lambda i: (0, 0))],
        out_shape=[jax.ShapeDtypeStruct((8, T), jnp.int32),
                   jax.ShapeDtypeStruct((MOE_N_EXPERTS, LANES), jnp.int32)],
        compiler_params=_cparams(("arbitrary",)),
        name="plan",
    )(info, counts)
    return dest8[:MOE_TOP_K].reshape(-1), pend[:, 0]


def _moe_capacity(T):
    return (-(-(T * MOE_TOP_K) // MOE_ROWS) + MOE_N_EXPERTS) * MOE_ROWS


def _router_weights(router_group_w, router_group_b, router_expert_w, router_expert_b):
    we = jnp.transpose(router_expert_w, (0, 2, 1)).reshape(MOE_N_EXPERTS, D_MODEL)
    pad = LANES - MOE_N_EXPERTS - MOE_GROUPS
    wr = jnp.concatenate([we, router_group_w.T, jnp.zeros((pad, D_MODEL), F32)], axis=0)
    br = jnp.concatenate([router_expert_b.reshape(-1), router_group_b, jnp.zeros((pad,), F32)])[:, None]
    return wr, br


def kernel(x, norm1_w, w_in, gla_fwd_gate_w, gla_fwd_gate_b, gla_bwd_gate_w, gla_bwd_gate_b,
           gla_norm_w, w_out, norm2_w, router_group_w, router_group_b, router_expert_w,
           router_expert_b, expert_w_gate, expert_w_up, expert_w_down, final_norm_w):
    B, S, D = x.shape
    T = B * S
    assert norm1_w.shape[0] == 1, "single-layer trunk: the final norm is fused into the combine step"
    h = x.reshape(T, D)
    gla_slab, gate, loga, att_slab = _inproj(h, S, norm1_w[0], w_in[0], gla_fwd_gate_w[0], gla_fwd_gate_b[0],
                                       gla_bwd_gate_w[0], gla_bwd_gate_b[0])
    o_f, o_b = _gla(gla_slab, loga, B, S)
    att_out = _attention(att_slab.reshape(T, 3 * ATT_WIDTH), B, S)
    att_out = att_out.reshape(B, ATT_CLASSES, S // ATT_CLASSES, ATT_WIDTH)
    wr, br = _router_weights(router_group_w[0], router_group_b[0], router_expert_w[0], router_expert_b[0])
    h, u2, logits = _outproj(o_f, o_b, gate, att_out, h, gla_norm_w[0], w_out[0], norm2_w[0], wr, br)
    info, counts = _route(logits)
    dest, pend = _plan(info, counts)
    xs = _dispatch(dest, pend, u2, _moe_capacity(T))
    ys = _experts(pend, xs, expert_w_gate[0], expert_w_up[0], expert_w_down[0])
    out = _combine(dest, ys, info, h, final_norm_w)
    return out.reshape(B, S, D)
```

```python
import functools

import jax
import jax.numpy as jnp
import numpy as np
from jax import lax
from jax.experimental import pallas as pl
from jax.experimental.pallas import tpu as pltpu

F32 = jnp.float32
BF16 = jnp.bfloat16

D_MODEL = 1024
GLA_HEADS = 4
GLA_DV = 128
GLA_DK = 64
GLA_KEY_WIDTH = GLA_HEADS * GLA_DK
GLA_VAL_WIDTH = GLA_HEADS * GLA_DV
GLA_GATE_RANK = 16
GLA_TAU = 16.0
GLA_CHUNK = 64
ATT_WIDTH = 512
ATT_HEAD_DIM = 64
ATT_HEADS = 8
ROT_DIM = 16
ROPE_THETA = 500000.0
DILATED_PATTERNS = ((128, 1), (512, 4), (2048, 16))
ATT_RADIUS = 64
MOE_GROUPS = 4
MOE_EXPERTS_PER_GROUP = 8
MOE_N_EXPERTS = 32
MOE_TOP_K = 2
MOE_D_FF = 512
EPS = 1e-6
NEG_INF = -1e30
LOG2E = 1.4426950408889634

LANES = 128
MOE_ROWS = 256


def _log2(n):
    assert n & (n - 1) == 0, n
    return n.bit_length() - 1


GLA_CHUNK_SHIFT = _log2(GLA_CHUNK)
GLA_DK_SHIFT = _log2(GLA_DK)
MOE_ROWS_SHIFT = _log2(MOE_ROWS)
MOE_GROUP_SHIFT = _log2(MOE_EXPERTS_PER_GROUP)
VMEM_LIMIT = 56 * 1024 * 1024


def _cparams(sem):
    return pltpu.CompilerParams(dimension_semantics=sem, vmem_limit_bytes=VMEM_LIMIT)


def _dot(a, b):
    return jnp.dot(a, b, preferred_element_type=F32)


def _dot_nt(a, b):
    return lax.dot_general(a, b, (((1,), (1,)), ((), ())), preferred_element_type=F32)


def _dot_tn(a, b):
    return lax.dot_general(a, b, (((0,), (0,)), ((), ())), preferred_element_type=F32)


def _rms(x, w):
    return x * lax.rsqrt(jnp.mean(x * x, axis=-1, keepdims=True) + EPS) * w


def _inproj_kernel(x_ref, n1_ref, wg_ref, wlr_ref, wa_ref, gw_ref, gb_ref,
                   rc_ref, rs1_ref, rs2_ref, gla_ref, gate_ref, loga_ref, att_ref, stage_ref, wgb, wlrb):
    @pl.when(pl.program_id(0) == 0)
    def _():
        wgb[...] = wg_ref[...].astype(BF16)
        wlrb[...] = wlr_ref[...].astype(BF16)

    x = x_ref[...]
    ub = _rms(x, n1_ref[...]).astype(BF16)
    g = _dot(ub, wgb[...])
    qkv = 2 * GLA_KEY_WIDTH + GLA_VAL_WIDTH
    gla_ref[:, :GLA_KEY_WIDTH] = g[:, :GLA_KEY_WIDTH] * (GLA_DK ** -0.5)
    gla_ref[:, GLA_KEY_WIDTH:] = g[:, GLA_KEY_WIDTH:qkv]
    gate_ref[...] = g[:, qkv:].astype(BF16)
    lr = _dot(ub, wlrb[...])
    gate = _dot(lr.astype(BF16), gw_ref[...]) + gb_ref[...]
    loga_ref[...] = (jnp.minimum(gate, 0.0) - jnp.log(1.0 + jnp.exp(-jnp.abs(gate)))) * (1.0 / GLA_TAU)
    a = _dot(ub, wa_ref[...])
    qk = a[:, :2 * ATT_WIDTH]
    reps = 2 * ATT_WIDTH // LANES
    c = jnp.concatenate([rc_ref[...]] * reps, axis=1)
    s1 = jnp.concatenate([rs1_ref[...]] * reps, axis=1)
    s2 = jnp.concatenate([rs2_ref[...]] * reps, axis=1)
    half = ROT_DIM // 2
    n = 2 * ATT_WIDTH
    roped = qk * c + pltpu.roll(qk, n - half, 1) * s1 + pltpu.roll(qk, half, 1) * s2
    qkv = jnp.concatenate([roped[:, :ATT_WIDTH] * (ATT_HEAD_DIM ** -0.5 * LOG2E), roped[:, ATT_WIDTH:],
                           a[:, 2 * ATT_WIDTH:]], axis=1)
    rows = x.shape[0] // ATT_CLASSES
    for j in range(3 * ATT_WIDTH // LANES):
        cols = slice(j * LANES, (j + 1) * LANES)
        stage_ref[j] = qkv[:, cols]
        for c in range(ATT_CLASSES):
            att_ref[c, :, cols] = stage_ref[j, pl.ds(c, rows, stride=ATT_CLASSES), :]


def _rope_lane_tables(S):
    half = ROT_DIM // 2
    inv = np.float32(ROPE_THETA) ** (-(np.arange(0, ROT_DIM, 2, dtype=np.float32) / np.float32(ROT_DIM)))
    ang = np.arange(S, dtype=np.float32)[:, None] * inv[None, :].astype(np.float32)
    cos, sin = np.cos(ang), np.sin(ang)
    ones = np.ones((S, ATT_HEAD_DIM - ROT_DIM), np.float32)
    zeros = np.zeros((S, ATT_HEAD_DIM - ROT_DIM), np.float32)
    zeros8 = np.zeros((S, half), np.float32)
    rep = LANES // ATT_HEAD_DIM
    c = np.tile(np.concatenate([cos, cos, ones], axis=1), (1, rep))
    s1 = np.tile(np.concatenate([-sin, zeros8, zeros], axis=1), (1, rep))
    s2 = np.tile(np.concatenate([zeros8, sin, zeros], axis=1), (1, rep))
    return jnp.asarray(c), jnp.asarray(s1), jnp.asarray(s2)


def _inproj(x2, S, norm1_w, w_in, wf, bfw, wb, bbw, tm=512):
    T = x2.shape[0]
    o_lr = 2 * GLA_KEY_WIDTH + 2 * GLA_VAL_WIDTH
    o_att = o_lr + 2 * GLA_GATE_RANK
    wa = w_in[:, o_att:].astype(BF16)
    zeros = jnp.zeros((GLA_GATE_RANK, GLA_KEY_WIDTH), F32)
    gw = jnp.concatenate([jnp.concatenate([wf, zeros], axis=1), jnp.concatenate([zeros, wb], axis=1),
                          jnp.zeros((LANES - 2 * GLA_GATE_RANK, 2 * GLA_KEY_WIDTH), F32)], axis=0).astype(BF16)
    gb = jnp.concatenate([bfw, bbw])[None, :]
    rc, rs1, rs2 = _rope_lane_tables(S)
    nS = S // tm
    row = lambda i: (i, 0)
    const = lambda i: (0, 0)
    pos = lambda i: (i % nS, 0)
    return pl.pallas_call(
        _inproj_kernel,
        grid=(T // tm,),
        in_specs=[
            pl.BlockSpec((tm, D_MODEL), row),
            pl.BlockSpec((1, D_MODEL), const),
            pl.BlockSpec((D_MODEL, o_lr), const),
            pl.BlockSpec((D_MODEL, LANES), lambda i: (0, o_lr // LANES)),
            pl.BlockSpec((D_MODEL, 3 * ATT_WIDTH), const),
            pl.BlockSpec((LANES, 2 * GLA_KEY_WIDTH), const),
            pl.BlockSpec((1, 2 * GLA_KEY_WIDTH), const),
            pl.BlockSpec((tm, LANES), pos),
            pl.BlockSpec((tm, LANES), pos),
            pl.BlockSpec((tm, LANES), pos),
        ],
        out_specs=[
            pl.BlockSpec((tm, o_lr - GLA_VAL_WIDTH), row),
            pl.BlockSpec((tm, GLA_VAL_WIDTH), row),
            pl.BlockSpec((tm, 2 * GLA_KEY_WIDTH), row),
            pl.BlockSpec((None, ATT_CLASSES, tm // ATT_CLASSES, 3 * ATT_WIDTH),
                         lambda i: (i // nS, 0, i % nS, 0)),
        ],
        out_shape=[
            jax.ShapeDtypeStruct((T, o_lr - GLA_VAL_WIDTH), F32),
            jax.ShapeDtypeStruct((T, GLA_VAL_WIDTH), BF16),
            jax.ShapeDtypeStruct((T, 2 * GLA_KEY_WIDTH), F32),
            jax.ShapeDtypeStruct((T // S, ATT_CLASSES, S // ATT_CLASSES, 3 * ATT_WIDTH), F32),
        ],
        scratch_shapes=[pltpu.VMEM((3 * ATT_WIDTH // LANES, tm, LANES), F32),
                        pltpu.VMEM((D_MODEL, o_lr), BF16), pltpu.VMEM((D_MODEL, LANES), BF16)],
        compiler_params=_cparams(("arbitrary",)),
        name="inproj",
    )(x2, norm1_w[None, :], w_in, w_in, wa, gw, gb, rc, rs1, rs2)


def _gla_decays(q, k, v, la, forward, G):
    C = GLA_CHUNK
    R = G * C
    r = lax.broadcasted_iota(jnp.int32, (R, R), 0)
    c = lax.broadcasted_iota(jnp.int32, (R, R), 1)
    same = (r >> GLA_CHUNK_SHIFT) == (c >> GLA_CHUNK_SHIFT)
    tri = (c <= r) if forward else (c >= r)
    t_mat = jnp.where(same, jnp.where(tri, 1.0, 0.0), 0.0).astype(BF16)
    hi = la.astype(BF16)
    lo = (la - hi.astype(F32)).astype(BF16)
    b = _dot(t_mat, hi) + _dot(t_mat, lo)
    edge = C - 1 if forward else 0
    tot = jnp.concatenate([jnp.broadcast_to(b[g * C + edge:g * C + edge + 1], (C, GLA_KEY_WIDTH))
                           for g in range(G)], axis=0)
    order = list(range(G)) if forward else list(range(G - 1, -1, -1))
    return dict(q_dec=q * jnp.exp(b), k_inv=(k * jnp.exp(-b)).astype(BF16), k_end=k * jnp.exp(tot - b),
                tot=tot, vb=v.astype(BF16), order=order, forward=forward, G=G)


def _gla_scores(prep):
    C, H = GLA_CHUNK, GLA_HEADS
    lane_k = lax.broadcasted_iota(jnp.int32, (C, GLA_KEY_WIDTH), 1)
    qd_heads, scores = {}, {}
    for g in prep["order"]:
        rows = slice(g * C, (g + 1) * C)
        qd = prep["q_dec"][rows]
        qd_heads[g] = jnp.concatenate([jnp.where((lane_k >> GLA_DK_SHIFT) == h, qd, 0.0) for h in range(H)],
                                      axis=0).astype(BF16)
        scores[g] = _dot_nt(qd_heads[g], prep["k_inv"][rows])
    return qd_heads, scores


def _gla_chunk_updates(prep):
    C, H, G = GLA_CHUNK, GLA_HEADS, prep["G"]
    k_end, tot, vb = prep["k_end"], prep["tot"], prep["vb"]
    kv, dec_t = {}, {}
    lane = lax.broadcasted_iota(jnp.int32, (GLA_KEY_WIDTH, 2 * C), 1)
    zeros = jnp.zeros((C, GLA_DV), BF16)
    for p in range(G // 2):
        pair = slice(2 * p * C, (2 * p + 2) * C)
        ke_t = k_end[pair].T.astype(BF16)
        tot_t = tot[pair].T
        swapped = pltpu.roll(tot_t, C, 1)
        for half in range(2):
            g = 2 * p + half
            rows = slice(g * C, (g + 1) * C)
            own = (lane < C) if half == 0 else (lane >= C)
            dec_t[g] = jnp.exp(jnp.where(own, tot_t, swapped))
            parts = []
            for h in range(H):
                v_h = vb[rows, h * GLA_DV:(h + 1) * GLA_DV]
                v_pad = jnp.concatenate([v_h, zeros] if half == 0 else [zeros, v_h], axis=0)
                parts.append(_dot(ke_t[h * C:(h + 1) * C], v_pad))
            kv[g] = jnp.concatenate(parts, axis=0)
    return kv, dec_t


def _gla_states(prep, kv, dec_t, s_ref):
    st = s_ref[...]
    states = {}
    for g in prep["order"]:
        states[g] = st.astype(BF16)
        st = st * dec_t[g] + kv[g]
    s_ref[...] = st
    return states


def _gla_outputs(prep, qd_heads, scores, inter, o_ref):
    C, H = GLA_CHUNK, GLA_HEADS
    row_q = lax.broadcasted_iota(jnp.int32, (H * C, C), 0) & (C - 1)
    col_k = lax.broadcasted_iota(jnp.int32, (H * C, C), 1)
    a_mask = (col_k <= row_q) if prep["forward"] else (col_k >= row_q)
    for g in prep["order"]:
        rows = slice(g * C, (g + 1) * C)
        a = jnp.where(a_mask, scores[g], 0.0).astype(BF16)
        vv = prep["vb"][rows]
        o_ref[rows, :] = jnp.concatenate(
            [_dot(a[h * C:(h + 1) * C], vv[:, h * GLA_DV:(h + 1) * GLA_DV]) + inter[g][h * C:(h + 1) * C]
             for h in range(H)], axis=1).astype(o_ref.dtype)


def _gla_kernel(qf_ref, kf_ref, vf_ref, laf_ref, qb_ref, kb_ref, vb_ref, lab_ref,
                of_ref, ob_ref, sf_ref, sb_ref, *, G):
    @pl.when(pl.program_id(1) == 0)
    def _():
        sf_ref[...] = jnp.zeros_like(sf_ref)
        sb_ref[...] = jnp.zeros_like(sb_ref)

    dirs = [(_gla_decays(qf_ref[...], kf_ref[...], vf_ref[...], laf_ref[...], True, G), sf_ref, of_ref),
            (_gla_decays(qb_ref[...], kb_ref[...], vb_ref[...], lab_ref[...], False, G), sb_ref, ob_ref)]
    scored = [_gla_scores(prep) for prep, _, _ in dirs]
    updates = [_gla_chunk_updates(prep) for prep, _, _ in dirs]
    states = [_gla_states(prep, kv, dec_t, s_ref) for (prep, s_ref, _), (kv, dec_t) in zip(dirs, updates)]
    inters = [{g: _dot(qd_heads[g], st[g]) for g in prep["order"]}
              for (prep, _, _), (qd_heads, _), st in zip(dirs, scored, states)]
    for (prep, _, o_ref), (qd_heads, scores), inter in zip(dirs, scored, inters):
        _gla_outputs(prep, qd_heads, scores, inter, o_ref)


def _gla(gla_slab, loga, B, S, G=8):
    T = B * S
    R = G * GLA_CHUNK
    ns = S // R
    fwd = lambda col: (lambda b, i: (b * ns + i, col))
    bwd = lambda col: (lambda b, i: (b * ns + ns - 1 - i, col))
    kw, vw = GLA_KEY_WIDTH, GLA_VAL_WIDTH
    return pl.pallas_call(
        functools.partial(_gla_kernel, G=G),
        grid=(B, ns),
        in_specs=[
            pl.BlockSpec((R, kw), fwd(0)), pl.BlockSpec((R, kw), fwd(1)),
            pl.BlockSpec((R, vw), fwd(1)), pl.BlockSpec((R, kw), fwd(0)),
            pl.BlockSpec((R, kw), bwd(0)), pl.BlockSpec((R, kw), bwd(1)),
            pl.BlockSpec((R, vw), bwd(1)), pl.BlockSpec((R, kw), bwd(1)),
        ],
        out_specs=[pl.BlockSpec((R, vw), fwd(0)), pl.BlockSpec((R, vw), bwd(0))],
        out_shape=[jax.ShapeDtypeStruct((T, vw), BF16), jax.ShapeDtypeStruct((T, vw), BF16)],
        scratch_shapes=[pltpu.VMEM((kw, GLA_DV), F32), pltpu.VMEM((kw, GLA_DV), F32)],
        compiler_params=_cparams(("arbitrary", "arbitrary")),
        name="gla",
    )(gla_slab, gla_slab, gla_slab, loga, gla_slab, gla_slab, gla_slab, loga)


ATT_CLASSES = 4
ATT_QB = 128
ATT_KB = ATT_QB + 2 * ATT_RADIUS


ATT_UNROLL = 4


def _att_kernel(q_ref, k_ref, v_ref, o_ref, m_ref, l_ref, bias_ref, *, S):
    QB, KB, NC = ATT_QB, ATT_KB, ATT_CLASSES
    L4 = S // NC
    lane = lax.broadcasted_iota(jnp.int32, (QB, LANES), 1)
    head0 = lane < ATT_HEAD_DIM

    @pl.when((pl.program_id(0) == 0) & (pl.program_id(1) == 0))
    def _():
        rowi = lax.broadcasted_iota(jnp.int32, (2 * QB, KB), 0) & (QB - 1)
        coli = lax.broadcasted_iota(jnp.int32, (2 * QB, KB), 1)
        qpos = (rowi & (QB // NC - 1)) * NC + (rowi >> _log2(QB // NC))
        kpos = (coli & (KB // NC - 1)) * NC + (coli >> _log2(KB // NC))
        for case in range(3):
            bias_ref[0, case] = jnp.where(jnp.abs(rowi - coli + case * ATT_RADIUS) <= ATT_RADIUS, 0.0, NEG_INF)
            bias_ref[1, case] = jnp.where(jnp.abs(qpos - kpos + case * ATT_RADIUS) <= ATT_RADIUS, 0.0, NEG_INF)

    for pi, (_, d) in enumerate(DILATED_PATTERNS):
        L = S // d
        nb = L // QB
        shift = nb.bit_length() - 1
        first = pi == 0
        last = pi == len(DILATED_PATTERNS) - 1

        def scores(n, d=d, L=L, nb=nb, shift=shift):
            cls = n >> shift
            q0 = (n & (nb - 1)) * QB
            ws = jnp.clip(q0 - ATT_RADIUS, 0, L - KB)
            if d == 1:
                qsls = [pl.ds(pl.multiple_of(c * L4 + q0 // NC, QB // NC), QB // NC) for c in range(NC)]
                ksls = [pl.ds(pl.multiple_of(c * L4 + ws // NC, ATT_RADIUS // NC), KB // NC) for c in range(NC)]
            elif d == NC:
                qsls = [pl.ds(pl.multiple_of(cls * L4 + q0, QB), QB)]
                ksls = [pl.ds(pl.multiple_of(cls * L4 + ws, ATT_RADIUS), KB)]
            else:
                base = (cls & (NC - 1)) * L4 + (cls >> _log2(NC))
                qsls = [pl.ds(base + NC * q0, QB, stride=NC)]
                ksls = [pl.ds(base + NC * ws, KB, stride=NC)]
            q = jnp.concatenate([q_ref[sl, :] for sl in qsls], axis=0)
            kw = jnp.concatenate([k_ref[sl, :] for sl in ksls], axis=0)
            q2 = jnp.concatenate([jnp.where(head0, q, 0.0), jnp.where(head0, 0.0, q)], axis=0).astype(BF16)
            s = _dot_nt(q2, kw.astype(BF16))
            return qsls, ksls, s + bias_ref[1 if d == 1 else 0, (q0 - ws) >> _log2(ATT_RADIUS)]

        def softmax_pv(qsls, ksls, s):
            m_blk = jnp.max(s, axis=-1, keepdims=True)
            p = jnp.exp2(s - m_blk)
            vw = jnp.concatenate([v_ref[sl, :] for sl in ksls], axis=0)
            v_ones = jnp.concatenate([vw.astype(BF16), jnp.ones((KB, LANES), BF16)], axis=1)
            pv = _dot(p.astype(BF16), v_ones)
            acc_b = jnp.where(head0, pv[:QB, :LANES], pv[QB:, :LANES])
            m_b = jnp.where(head0, m_blk[:QB], m_blk[QB:])
            l_b = jnp.where(head0, pv[:QB, LANES:], pv[QB:, LANES:])
            return qsls, acc_b, m_b, l_b

        def load(ref, sls):
            return jnp.concatenate([ref[sl, :] for sl in sls], axis=0)

        def store(ref, sls, val):
            n = val.shape[0] // len(sls)
            for i, sl in enumerate(sls):
                ref[sl, :] = val[i * n:(i + 1) * n]

        def body(n, carry, first=first, last=last):
            staged = [scores(n * ATT_UNROLL + u) for u in range(ATT_UNROLL)]
            blocks = [softmax_pv(*st) for st in staged]
            for qsls, acc_b, m_b, l_b in blocks:
                if first:
                    acc, m_new, l_new = acc_b, m_b, l_b
                else:
                    m_old = load(m_ref, qsls)
                    m_new = jnp.maximum(m_old, m_b)
                    w_old = jnp.exp2(m_old - m_new)
                    w_blk = jnp.exp2(m_b - m_new)
                    acc = load(o_ref, qsls) * w_old + acc_b * w_blk
                    l_new = load(l_ref, qsls) * w_old + l_b * w_blk
                if last:
                    store(o_ref, qsls, acc / l_new)
                else:
                    store(o_ref, qsls, acc)
                    store(m_ref, qsls, m_new)
                    store(l_ref, qsls, l_new)
            return carry

        lax.fori_loop(0, S // (QB * ATT_UNROLL), body, 0)


def _attention(att_slab, B, S):
    T = B * S
    ncol = ATT_WIDTH // LANES
    return pl.pallas_call(
        functools.partial(_att_kernel, S=S),
        grid=(B, ncol),
        in_specs=[
            pl.BlockSpec((S, LANES), lambda b, h: (b, h)),
            pl.BlockSpec((S, LANES), lambda b, h: (b, ncol + h)),
            pl.BlockSpec((S, LANES), lambda b, h: (b, 2 * ncol + h)),
        ],
        out_specs=pl.BlockSpec((S, LANES), lambda b, h: (b, h)),
        out_shape=jax.ShapeDtypeStruct((T, ATT_WIDTH), F32),
        scratch_shapes=[pltpu.VMEM((S, LANES), F32), pltpu.VMEM((S, LANES), F32),
                        pltpu.VMEM((2, 3, 2 * ATT_QB, ATT_KB), F32)],
        compiler_params=_cparams(("arbitrary", "arbitrary")),
        name="dilated_attention",
    )(att_slab, att_slab, att_slab)


PACK_WORDS = D_MODEL // 2
ROW_TILE = PACK_WORDS // LANES
HIGH_HALF = -65536


def _pack_rows(x):
    bits = lambda v: lax.bitcast_convert_type(v.astype(BF16).astype(F32), jnp.int32)
    low = (bits(x[:, :PACK_WORDS]) >> 16) & 0xFFFF
    return (bits(x[:, PACK_WORDS:]) & HIGH_HALF) | low


def _unpack_rows(w):
    low = lax.bitcast_convert_type(w << 16, F32)
    high = lax.bitcast_convert_type(w & HIGH_HALF, F32)
    return jnp.concatenate([low, high], axis=1).astype(BF16)


def _to_row_tiles(ref, w):
    n = w.shape[0]
    for j in range(ROW_TILE):
        ref[pl.ds(j, n, stride=ROW_TILE), :] = w[:, j * LANES:(j + 1) * LANES]


def _from_row_tiles(ref, n):
    return jnp.concatenate([ref[pl.ds(j, n, stride=ROW_TILE), :] for j in range(ROW_TILE)], axis=1)


def _tile_copy(src_ref, src_row, dst_ref, dst_row, sem):
    src = pl.ds(pl.multiple_of(src_row * ROW_TILE, ROW_TILE), ROW_TILE)
    dst = pl.ds(pl.multiple_of(dst_row * ROW_TILE, ROW_TILE), ROW_TILE)
    return pltpu.make_async_copy(src_ref.at[src], dst_ref.at[dst], sem)


def _outproj_kernel(of_ref, ob_ref, gg_ref, att_ref, x_ref, gnw_ref, wo1_ref, wo2_ref,
                    n2_ref, wr_ref, br_ref, h_ref, u_ref, lg_ref, stage_ref):
    rows = stage_ref.shape[1] // ATT_CLASSES
    for j in range(ATT_WIDTH // LANES):
        for c in range(ATT_CLASSES):
            stage_ref[j, pl.ds(c, rows, stride=ATT_CLASSES), :] = att_ref[c, :, j * LANES:(j + 1) * LANES]
    att = jnp.concatenate([stage_ref[j] for j in range(ATT_WIDTH // LANES)], axis=1)
    o = of_ref[...].astype(F32) + ob_ref[...].astype(F32)
    gate = gg_ref[...].astype(F32)
    gnw = gnw_ref[...]
    parts = []
    for h in range(GLA_HEADS):
        sl = slice(h * GLA_DV, (h + 1) * GLA_DV)
        parts.append(_rms(o[:, sl], gnw))
    y = jnp.concatenate(parts, axis=1) * (gate / (1.0 + jnp.exp(-gate)))
    mix = _dot(y.astype(BF16), wo1_ref[...]) + _dot(att.astype(BF16), wo2_ref[...])
    h = x_ref[...] + mix
    h_ref[...] = h
    u = _rms(h, n2_ref[...])
    _to_row_tiles(u_ref, _pack_rows(u))
    u_hi = u.astype(BF16)
    u_lo = (u - u_hi.astype(F32)).astype(BF16)
    hi_both = _dot_nt(wr_ref[...], u_hi)
    lg_ref[...] = (hi_both[:LANES] + hi_both[LANES:] + _dot_nt(wr_ref[:LANES], u_lo)) + br_ref[...]


def _outproj(o_f, o_b, gate, att_out, x2, gla_norm_w, w_out, norm2_w, wr, br, tm=512):
    T = x2.shape[0]
    nS = att_out.shape[2] * ATT_CLASSES // tm
    row = lambda i: (i, 0)
    const = lambda i: (0, 0)
    wo = w_out.astype(BF16)
    wr_hi = wr.astype(BF16)
    wr_lo = (wr - wr_hi.astype(F32)).astype(BF16)
    wr = jnp.concatenate([wr_hi, wr_lo], axis=0)
    return pl.pallas_call(
        _outproj_kernel,
        grid=(T // tm,),
        in_specs=[
            pl.BlockSpec((tm, GLA_VAL_WIDTH), row),
            pl.BlockSpec((tm, GLA_VAL_WIDTH), row),
            pl.BlockSpec((tm, GLA_VAL_WIDTH), row),
            pl.BlockSpec((None, ATT_CLASSES, tm // ATT_CLASSES, ATT_WIDTH), lambda i: (i // nS, 0, i % nS, 0)),
            pl.BlockSpec((tm, D_MODEL), row),
            pl.BlockSpec((1, GLA_DV), const),
            pl.BlockSpec((GLA_VAL_WIDTH, D_MODEL), const),
            pl.BlockSpec((ATT_WIDTH, D_MODEL), lambda i: (GLA_VAL_WIDTH // ATT_WIDTH, 0)),
            pl.BlockSpec((1, D_MODEL), const),
            pl.BlockSpec((2 * LANES, D_MODEL), const),
            pl.BlockSpec((LANES, 1), const),
        ],
        out_specs=[
            pl.BlockSpec((tm, D_MODEL), row),
            pl.BlockSpec((tm * ROW_TILE, LANES), row),
            pl.BlockSpec((LANES, tm), lambda i: (0, i)),
        ],
        out_shape=[
            jax.ShapeDtypeStruct((T, D_MODEL), F32),
            jax.ShapeDtypeStruct((T * ROW_TILE, LANES), jnp.int32),
            jax.ShapeDtypeStruct((LANES, T), F32),
        ],
        scratch_shapes=[pltpu.VMEM((ATT_WIDTH // LANES, tm, LANES), F32)],
        compiler_params=_cparams(("arbitrary",)),
        name="outproj",
    )(o_f, o_b, gate, att_out, x2, gla_norm_w[None, :], wo, wo,
      norm2_w[None, :], wr, br)


INFO_E1, INFO_E2, INFO_R1, INFO_R2, INFO_W1, INFO_W2 = range(6)
ROUTE_ROWS = 40


def _route_kernel(lg_ref, info_ref, cnt_ref, carry_ref):
    @pl.when(pl.program_id(0) == 0)
    def _():
        carry_ref[...] = jnp.zeros_like(carry_ref)

    lg = lg_ref[:ROUTE_ROWS, :]
    tr = lg.shape[1]
    row = lax.broadcasted_iota(jnp.int32, (ROUTE_ROWS, tr), 0)
    big = jnp.int32(1 << 20)
    is_g = (row >= MOE_N_EXPERTS) & (row < MOE_N_EXPERTS + MOE_GROUPS)
    gl = jnp.where(is_g, lg, -jnp.inf)
    gmax = jnp.max(gl, axis=0, keepdims=True)
    gsel = jnp.min(jnp.where(gl == gmax, row - MOE_N_EXPERTS, big), axis=0, keepdims=True)
    g_w = 1.0 / jnp.sum(jnp.where(is_g, jnp.exp(lg - gmax), 0.0), axis=0, keepdims=True)
    in_grp = (row < MOE_N_EXPERTS) & ((row >> MOE_GROUP_SHIFT) == gsel)
    el = jnp.where(in_grp, lg, -jnp.inf)
    v1 = jnp.max(el, axis=0, keepdims=True)
    i1 = jnp.min(jnp.where(el == v1, row, big), axis=0, keepdims=True)
    el2 = jnp.where(row == i1, -jnp.inf, el)
    v2 = jnp.max(el2, axis=0, keepdims=True)
    i2 = jnp.min(jnp.where(el2 == v2, row, big), axis=0, keepdims=True)
    t = jnp.exp(v2 - v1)
    w1 = g_w * (1.0 / (1.0 + t))
    w2 = g_w * (t / (1.0 + t))

    erow = lax.broadcasted_iota(jnp.int32, (MOE_N_EXPERTS, tr), 0)
    hit1 = erow == i1
    hit2 = erow == i2
    member = jnp.where(hit1 | hit2, 1.0, 0.0)
    r = lax.broadcasted_iota(jnp.int32, (tr, tr), 0)
    c = lax.broadcasted_iota(jnp.int32, (tr, tr), 1)
    earlier = jnp.where(r < c, 1.0, 0.0).astype(BF16)
    carry = carry_ref[...]
    prefix = _dot(member.astype(BF16), earlier) + carry[:, 0:1]
    rank1 = jnp.sum(jnp.where(hit1, prefix, 0.0), axis=0, keepdims=True)
    rank2 = jnp.sum(jnp.where(hit2, prefix, 0.0), axis=0, keepdims=True)
    carry = carry + jnp.sum(member, axis=1, keepdims=True)
    carry_ref[...] = carry
    cnt_ref[...] = carry

    zero = jnp.zeros_like(w1)
    info_ref[...] = jnp.concatenate([i1.astype(F32), i2.astype(F32), rank1, rank2, w1, w2, zero, zero], axis=0)


def _route(logits_t, tr=1024):
    T = logits_t.shape[1]
    return pl.pallas_call(
        _route_kernel,
        grid=(T // tr,),
        in_specs=[pl.BlockSpec((LANES, tr), lambda i: (0, i))],
        out_specs=[pl.BlockSpec((8, tr), lambda i: (0, i)),
                   pl.BlockSpec((MOE_N_EXPERTS, LANES), lambda i: (0, 0))],
        out_shape=[jax.ShapeDtypeStruct((8, T), F32), jax.ShapeDtypeStruct((MOE_N_EXPERTS, LANES), F32)],
        scratch_shapes=[pltpu.VMEM((MOE_N_EXPERTS, LANES), F32)],
        compiler_params=_cparams(("arbitrary",)),
        name="route",
    )(logits_t)


ROW_UNROLL = 16


def _dispatch_kernel(dest_ref, pend_ref, u_ref, xs_ref, zbuf, sem, zsem, *, td, T, nblk):
    @pl.when(pl.program_id(0) == 0)
    def _():
        zbuf[...] = jnp.zeros_like(zbuf)
        n_used = pend_ref[MOE_N_EXPERTS - 1] >> MOE_ROWS_SHIFT

        def zero_copy(blk):
            start = pl.multiple_of(blk * (MOE_ROWS * ROW_TILE), MOE_ROWS * ROW_TILE)
            return pltpu.make_async_copy(zbuf, xs_ref.at[pl.ds(start, MOE_ROWS * ROW_TILE)], zsem)

        def each_pad_block(fn):
            def per_expert(e, carry):
                prev = jnp.where(e > 0, pend_ref[jnp.maximum(e - 1, 0)], 0)

                @pl.when(pend_ref[e] > prev)
                def _():
                    fn((pend_ref[e] >> MOE_ROWS_SHIFT) - 1)
                return carry

            def per_tail(j, carry):
                @pl.when(n_used + j < nblk)
                def _():
                    fn(n_used + j)
                return carry

            lax.fori_loop(0, MOE_N_EXPERTS, per_expert, 0)
            lax.fori_loop(0, MOE_N_EXPERTS, per_tail, 0)

        each_pad_block(lambda blk: zero_copy(blk).start())
        each_pad_block(lambda blk: zero_copy(blk).wait())

    base = pl.program_id(0) * td

    def issue(g, carry):
        for j in range(ROW_UNROLL):
            r = g * ROW_UNROLL + j
            for k in range(MOE_TOP_K):
                _tile_copy(u_ref, r, xs_ref, dest_ref[k * T + base + r], sem).start(priority=k)
        return carry

    lax.fori_loop(0, td // ROW_UNROLL, issue, 0)
    for k in range(MOE_TOP_K):
        pltpu.make_async_copy(u_ref, xs_ref.at[pl.ds(0, td * ROW_TILE)], sem).wait()


def _dispatch(dest, pend, u2, cap, td=2048):
    T = u2.shape[0] // ROW_TILE
    return pl.pallas_call(
        functools.partial(_dispatch_kernel, td=td, T=T, nblk=cap // MOE_ROWS),
        grid_spec=pltpu.PrefetchScalarGridSpec(
            num_scalar_prefetch=2,
            grid=(T // td,),
            in_specs=[pl.BlockSpec((td * ROW_TILE, LANES), lambda i, d, z: (i, 0))],
            out_specs=pl.BlockSpec(memory_space=pl.ANY),
            scratch_shapes=[pltpu.VMEM((MOE_ROWS * ROW_TILE, LANES), jnp.int32),
                            pltpu.SemaphoreType.DMA(()), pltpu.SemaphoreType.DMA(())],
        ),
        out_shape=jax.ShapeDtypeStruct((cap * ROW_TILE, LANES), jnp.int32),
        compiler_params=_cparams(("arbitrary",)),
        name="dispatch",
    )(dest, pend, u2)


def _expert_kernel(pend_ref, xs_hbm, wg_hbm, wu_hbm, wd_hbm, ys_hbm,
                   xbuf, ybuf, zbuf, stage_g, stage_u, stage_d, wgb, wub, wdb, xsem, ysem, wsem, zsem, *, nblk):
    last = MOE_N_EXPERTS - 1
    n_used = pend_ref[last] >> MOE_ROWS_SHIFT
    block_rows = MOE_ROWS * ROW_TILE

    def x_copy(b, slot):
        start = pl.multiple_of(b * block_rows, block_rows)
        return pltpu.make_async_copy(xs_hbm.at[pl.ds(start, block_rows)], xbuf.at[slot], xsem.at[slot])

    def y_copy(b, slot):
        start = pl.multiple_of(b * block_rows, block_rows)
        return pltpu.make_async_copy(ybuf.at[slot], ys_hbm.at[pl.ds(start, block_rows)], ysem.at[slot])

    def zero_copy(b):
        start = pl.multiple_of(b * block_rows, block_rows)
        return pltpu.make_async_copy(zbuf, ys_hbm.at[pl.ds(start, block_rows)], zsem)

    def weight_copies(e):
        return (pltpu.make_async_copy(wg_hbm.at[e], stage_g, wsem.at[0]),
                pltpu.make_async_copy(wu_hbm.at[e], stage_u, wsem.at[1]),
                pltpu.make_async_copy(wd_hbm.at[e], stage_d, wsem.at[2]))

    def owner(start, row):
        return lax.while_loop(lambda e: (e < last) & (pend_ref[e] <= row), lambda e: e + 1, start)

    for c in weight_copies(owner(0, 0)):
        c.start()
    x_copy(0, 0).start()

    zbuf[...] = jnp.zeros_like(zbuf)

    def tail(fn):
        def step(b, carry):
            fn(b)
            return carry
        lax.fori_loop(n_used, nblk, step, 0)

    tail(lambda b: zero_copy(b).start())

    def body(b, cur):
        slot = b & 1
        e = owner(jnp.maximum(cur, 0), b * MOE_ROWS)
        x_copy(b, slot).wait()

        @pl.when(b + 1 < n_used)
        def _():
            x_copy(b + 1, 1 - slot).start()

        @pl.when(e != cur)
        def _():
            for c in weight_copies(e):
                c.wait()
            wgb[...] = stage_g[...].astype(BF16)
            wub[...] = stage_u[...].astype(BF16)
            wdb[...] = stage_d[...].astype(BF16)

            @pl.when(pend_ref[e] < pend_ref[last])
            def _():
                for c in weight_copies(owner(e + 1, pend_ref[e])):
                    c.start(priority=1)

        @pl.when(b >= 2)
        def _():
            y_copy(b - 2, slot).wait()

        xb = _unpack_rows(_from_row_tiles(xbuf.at[slot], MOE_ROWS))
        g = _dot(xb, wgb[...])
        u = _dot(xb, wub[...])
        hid = (g / (1.0 + jnp.exp(-g))) * u
        _to_row_tiles(ybuf.at[slot], _pack_rows(_dot(hid.astype(BF16), wdb[...])))
        y_copy(b, slot).start()
        return e

    lax.fori_loop(0, n_used, body, jnp.int32(-1))

    @pl.when(n_used >= 2)
    def _():
        y_copy(n_used - 2, n_used & 1).wait()
    y_copy(n_used - 1, (n_used - 1) & 1).wait()
    tail(lambda b: zero_copy(b).wait())


def _experts(pend, xs, w_gate, w_up, w_down):
    cap = xs.shape[0] // ROW_TILE
    nblk = cap // MOE_ROWS
    block = (MOE_ROWS * ROW_TILE, LANES)
    anywhere = pl.BlockSpec(memory_space=pl.ANY)
    return pl.pallas_call(
        functools.partial(_expert_kernel, nblk=nblk),
        grid_spec=pltpu.PrefetchScalarGridSpec(
            num_scalar_prefetch=1,
            grid=(1,),
            in_specs=[anywhere, anywhere, anywhere, anywhere],
            out_specs=anywhere,
            scratch_shapes=[pltpu.VMEM((2,) + block, jnp.int32),
                            pltpu.VMEM((2,) + block, jnp.int32),
                            pltpu.VMEM(block, jnp.int32),
                            pltpu.VMEM((D_MODEL, MOE_D_FF), F32),
                            pltpu.VMEM((D_MODEL, MOE_D_FF), F32),
                            pltpu.VMEM((MOE_D_FF, D_MODEL), F32),
                            pltpu.VMEM((D_MODEL, MOE_D_FF), BF16),
                            pltpu.VMEM((D_MODEL, MOE_D_FF), BF16),
                            pltpu.VMEM((MOE_D_FF, D_MODEL), BF16),
                            pltpu.SemaphoreType.DMA((2,)),
                            pltpu.SemaphoreType.DMA((2,)),
                            pltpu.SemaphoreType.DMA((3,)),
                            pltpu.SemaphoreType.DMA(())],
        ),
        out_shape=jax.ShapeDtypeStruct((cap * ROW_TILE, LANES), jnp.int32),
        compiler_params=_cparams(("arbitrary",)),
        name="experts",
    )(pend, xs, w_gate, w_up, w_down)


def _combine_kernel(dest_ref, ys_ref, info_ref, h_ref, fw_ref, o_ref, buf, sem, *, tc, T):
    i = pl.program_id(0)
    n = pl.num_programs(0)

    def issue(step, slot):
        base = step * tc

        def body(g, carry):
            for j in range(ROW_UNROLL):
                r = g * ROW_UNROLL + j
                for k in range(MOE_TOP_K):
                    _tile_copy(ys_ref, dest_ref[k * T + base + r], buf.at[slot, k], r,
                               sem.at[slot]).start(priority=k)
            return carry

        lax.fori_loop(0, tc // ROW_UNROLL, body, 0)

    @pl.when(i == 0)
    def _():
        issue(0, 0)

    slot = i % 2

    @pl.when(i + 1 < n)
    def _():
        issue(i + 1, 1 - slot)

    for k in range(MOE_TOP_K):
        pltpu.make_async_copy(ys_ref.at[pl.ds(0, tc * ROW_TILE)], buf.at[slot, k], sem.at[slot]).wait()

    info_t = jnp.concatenate([info_ref[...]] * (LANES // 8), axis=0).T
    w1 = info_t[:, INFO_W1:INFO_W1 + 1]
    w2 = info_t[:, INFO_W2:INFO_W2 + 1]
    y1 = _unpack_rows(_from_row_tiles(buf.at[slot, 0], tc)).astype(F32)
    y2 = _unpack_rows(_from_row_tiles(buf.at[slot, 1], tc)).astype(F32)
    h = h_ref[...] + (y1 * w1 + y2 * w2)
    o_ref[...] = _rms(h, fw_ref[...])


def _combine(dest, ys, info, h, final_w, tc=512):
    T = h.shape[0]
    return pl.pallas_call(
        functools.partial(_combine_kernel, tc=tc, T=T),
        grid_spec=pltpu.PrefetchScalarGridSpec(
            num_scalar_prefetch=1,
            grid=(T // tc,),
            in_specs=[pl.BlockSpec(memory_space=pl.ANY),
                      pl.BlockSpec((8, tc), lambda i, d: (0, i)),
                      pl.BlockSpec((tc, D_MODEL), lambda i, d: (i, 0)),
                      pl.BlockSpec((1, D_MODEL), lambda i, d: (0, 0))],
            out_specs=pl.BlockSpec((tc, D_MODEL), lambda i, d: (i, 0)),
            scratch_shapes=[pltpu.VMEM((2, MOE_TOP_K, tc * ROW_TILE, LANES), jnp.int32),
                            pltpu.SemaphoreType.DMA((2,))],
        ),
        out_shape=jax.ShapeDtypeStruct((T, D_MODEL), F32),
        compiler_params=_cparams(("arbitrary",)),
        name="combine",
    )(dest, ys, info, h, final_w[None, :])


def _plan_kernel(info_ref, cnt_ref, dest_ref, pend_ref):
    cnt = cnt_ref[...].astype(jnp.int32)
    nblk_e = ((cnt + (MOE_ROWS - 1)) >> MOE_ROWS_SHIFT).astype(F32)
    r = lax.broadcasted_iota(jnp.int32, (MOE_N_EXPERTS, MOE_N_EXPERTS), 0)
    c = lax.broadcasted_iota(jnp.int32, (MOE_N_EXPERTS, MOE_N_EXPERTS), 1)
    before = jnp.where(c < r, 1.0, 0.0).astype(BF16)
    first_blk = _dot(before, nblk_e.astype(BF16))
    pstart = first_blk[:, 0:1] * float(MOE_ROWS)
    pend_ref[...] = ((first_blk + nblk_e) * float(MOE_ROWS)).astype(jnp.int32)

    info = info_ref[...]
    erow = lax.broadcasted_iota(jnp.int32, (MOE_N_EXPERTS, info.shape[1]), 0)
    start_of = lambda e: jnp.sum(jnp.where(erow == e.astype(jnp.int32), pstart, 0.0), axis=0, keepdims=True)
    d1 = info[INFO_R1:INFO_R1 + 1] + start_of(info[INFO_E1:INFO_E1 + 1])
    d2 = info[INFO_R2:INFO_R2 + 1] + start_of(info[INFO_E2:INFO_E2 + 1])
    zero = jnp.zeros_like(d1)
    dest_ref[...] = jnp.concatenate([d1, d2] + [zero] * 6, axis=0).astype(jnp.int32)


def _plan(info, counts, tr=2048):
    T = info.shape[1]
    dest8, pend = pl.pallas_call(
        _plan_kernel,
        grid=(T // tr,),
        in_specs=[pl.BlockSpec((8, tr), lambda i: (0, i)),
                  pl.BlockSpec((MOE_N_EXPERTS, LANES), lambda i: (0, 0))],
        out_specs=[pl.BlockSpec((8, tr), lambda i: (0, i)),
                   pl.BlockSpec((MOE_N_EXPERTS, LANES), lambda i: (0, 0))],
        out_shape=[jax.ShapeDtypeStruct((8, T), jnp.int32),
                   jax.ShapeDtypeStruct((MOE_N_EXPERTS, LANES), jnp.int32)],
        compiler_params=_cparams(("arbitrary",)),
        name="plan",
    )(info, counts)
    return dest8[:MOE_TOP_K].reshape(-1), pend[:, 0]


def _moe_capacity(T):
    return (-(-(T * MOE_TOP_K) // MOE_ROWS) + MOE_N_EXPERTS) * MOE_ROWS


def _router_weights(router_group_w, router_group_b, router_expert_w, router_expert_b):
    we = jnp.transpose(router_expert_w, (0, 2, 1)).reshape(MOE_N_EXPERTS, D_MODEL)
    pad = LANES - MOE_N_EXPERTS - MOE_GROUPS
    wr = jnp.concatenate([we, router_group_w.T, jnp.zeros((pad, D_MODEL), F32)], axis=0)
    br = jnp.concatenate([router_expert_b.reshape(-1), router_group_b, jnp.zeros((pad,), F32)])[:, None]
    return wr, br


def kernel(x, norm1_w, w_in, gla_fwd_gate_w, gla_fwd_gate_b, gla_bwd_gate_w, gla_bwd_gate_b,
           gla_norm_w, w_out, norm2_w, router_group_w, router_group_b, router_expert_w,
           router_expert_b, expert_w_gate, expert_w_up, expert_w_down, final_norm_w):
    B, S, D = x.shape
    T = B * S
    assert norm1_w.shape[0] == 1, "single-layer trunk: the final norm is fused into the combine step"
    h = x.reshape(T, D)
    gla_slab, gate, loga, att_slab = _inproj(h, S, norm1_w[0], w_in[0], gla_fwd_gate_w[0], gla_fwd_gate_b[0],
                                       gla_bwd_gate_w[0], gla_bwd_gate_b[0])
    o_f, o_b = _gla(gla_slab, loga, B, S)
    att_out = _attention(att_slab.reshape(T, 3 * ATT_WIDTH), B, S)
    att_out = att_out.reshape(B, ATT_CLASSES, S // ATT_CLASSES, ATT_WIDTH)
    wr, br = _router_weights(router_group_w[0], router_group_b[0], router_expert_w[0], router_expert_b[0])
    h, u2, logits = _outproj(o_f, o_b, gate, att_out, h, gla_norm_w[0], w_out[0], norm2_w[0], wr, br)
    info, counts = _route(logits)
    dest, pend = _plan(info, counts)
    xs = _dispatch(dest, pend, u2, _moe_capacity(T))
    ys = _experts(pend, xs, expert_w_gate[0], expert_w_up[0], expert_w_down[0])
    out = _combine(dest, ys, info, h, final_norm_w)
    return out.reshape(B, S, D)
```

```python
import functools

import jax
import jax.numpy as jnp
import numpy as np
from jax import lax
from jax.experimental import pallas as pl
from jax.experimental.pallas import tpu as pltpu

F32 = jnp.float32
BF16 = jnp.bfloat16

D_MODEL = 1024
GLA_HEADS = 4
GLA_DV = 128
GLA_DK = 64
GLA_KEY_WIDTH = GLA_HEADS * GLA_DK
GLA_VAL_WIDTH = GLA_HEADS * GLA_DV
GLA_GATE_RANK = 16
GLA_TAU = 16.0
GLA_CHUNK = 64
ATT_WIDTH = 512
ATT_HEAD_DIM = 64
ATT_HEADS = 8
ROT_DIM = 16
ROPE_THETA = 500000.0
DILATED_PATTERNS = ((128, 1), (512, 4), (2048, 16))
ATT_RADIUS = 64
MOE_GROUPS = 4
MOE_EXPERTS_PER_GROUP = 8
MOE_N_EXPERTS = 32
MOE_TOP_K = 2
MOE_D_FF = 512
EPS = 1e-6
NEG_INF = -1e30
LOG2E = 1.4426950408889634

LANES = 128
MOE_ROWS = 256


def _log2(n):
    assert n & (n - 1) == 0, n
    return n.bit_length() - 1


GLA_CHUNK_SHIFT = _log2(GLA_CHUNK)
GLA_DK_SHIFT = _log2(GLA_DK)
MOE_ROWS_SHIFT = _log2(MOE_ROWS)
MOE_GROUP_SHIFT = _log2(MOE_EXPERTS_PER_GROUP)
VMEM_LIMIT = 56 * 1024 * 1024


def _cparams(sem):
    return pltpu.CompilerParams(dimension_semantics=sem, vmem_limit_bytes=VMEM_LIMIT)


def _dot(a, b):
    return jnp.dot(a, b, preferred_element_type=F32)


def _dot_nt(a, b):
    return lax.dot_general(a, b, (((1,), (1,)), ((), ())), preferred_element_type=F32)


def _dot_tn(a, b):
    return lax.dot_general(a, b, (((0,), (0,)), ((), ())), preferred_element_type=F32)


def _rms(x, w):
    return x * lax.rsqrt(jnp.mean(x * x, axis=-1, keepdims=True) + EPS) * w


def _inproj_kernel(x_ref, n1_ref, wg_ref, wlr_ref, wa_ref, gw_ref, gb_ref,
                   rc_ref, rs1_ref, rs2_ref, gla_ref, gate_ref, loga_ref, att_ref, stage_ref, wgb, wlrb):
    @pl.when(pl.program_id(0) == 0)
    def _():
        wgb[...] = wg_ref[...].astype(BF16)
        wlrb[...] = wlr_ref[...].astype(BF16)

    x = x_ref[...]
    ub = _rms(x, n1_ref[...]).astype(BF16)
    g = _dot(ub, wgb[...])
    qkv = 2 * GLA_KEY_WIDTH + GLA_VAL_WIDTH
    gla_ref[:, :GLA_KEY_WIDTH] = g[:, :GLA_KEY_WIDTH] * (GLA_DK ** -0.5)
    gla_ref[:, GLA_KEY_WIDTH:] = g[:, GLA_KEY_WIDTH:qkv]
    gate_ref[...] = g[:, qkv:].astype(BF16)
    lr = _dot(ub, wlrb[...])
    gate = _dot(lr.astype(BF16), gw_ref[...]) + gb_ref[...]
    loga_ref[...] = (jnp.minimum(gate, 0.0) - jnp.log(1.0 + jnp.exp(-jnp.abs(gate)))) * (1.0 / GLA_TAU)
    a = _dot(ub, wa_ref[...])
    qk = a[:, :2 * ATT_WIDTH]
    reps = 2 * ATT_WIDTH // LANES
    c = jnp.concatenate([rc_ref[...]] * reps, axis=1)
    s1 = jnp.concatenate([rs1_ref[...]] * reps, axis=1)
    s2 = jnp.concatenate([rs2_ref[...]] * reps, axis=1)
    half = ROT_DIM // 2
    n = 2 * ATT_WIDTH
    roped = qk * c + pltpu.roll(qk, n - half, 1) * s1 + pltpu.roll(qk, half, 1) * s2
    qkv = jnp.concatenate([roped[:, :ATT_WIDTH] * (ATT_HEAD_DIM ** -0.5 * LOG2E), roped[:, ATT_WIDTH:],
                           a[:, 2 * ATT_WIDTH:]], axis=1)
    rows = x.shape[0] // ATT_CLASSES
    for j in range(3 * ATT_WIDTH // LANES):
        cols = slice(j * LANES, (j + 1) * LANES)
        stage_ref[j] = qkv[:, cols]
        for c in range(ATT_CLASSES):
            att_ref[c, :, cols] = stage_ref[j, pl.ds(c, rows, stride=ATT_CLASSES), :]


def _rope_lane_tables(S):
    half = ROT_DIM // 2
    inv = np.float32(ROPE_THETA) ** (-(np.arange(0, ROT_DIM, 2, dtype=np.float32) / np.float32(ROT_DIM)))
    ang = np.arange(S, dtype=np.float32)[:, None] * inv[None, :].astype(np.float32)
    cos, sin = np.cos(ang), np.sin(ang)
    ones = np.ones((S, ATT_HEAD_DIM - ROT_DIM), np.float32)
    zeros = np.zeros((S, ATT_HEAD_DIM - ROT_DIM), np.float32)
    zeros8 = np.zeros((S, half), np.float32)
    rep = LANES // ATT_HEAD_DIM
    c = np.tile(np.concatenate([cos, cos, ones], axis=1), (1, rep))
    s1 = np.tile(np.concatenate([-sin, zeros8, zeros], axis=1), (1, rep))
    s2 = np.tile(np.concatenate([zeros8, sin, zeros], axis=1), (1, rep))
    return jnp.asarray(c), jnp.asarray(s1), jnp.asarray(s2)


def _inproj(x2, S, norm1_w, w_in, wf, bfw, wb, bbw, tm=512):
    T = x2.shape[0]
    o_lr = 2 * GLA_KEY_WIDTH + 2 * GLA_VAL_WIDTH
    o_att = o_lr + 2 * GLA_GATE_RANK
    wa = w_in[:, o_att:].astype(BF16)
    zeros = jnp.zeros((GLA_GATE_RANK, GLA_KEY_WIDTH), F32)
    gw = jnp.concatenate([jnp.concatenate([wf, zeros], axis=1), jnp.concatenate([zeros, wb], axis=1),
                          jnp.zeros((LANES - 2 * GLA_GATE_RANK, 2 * GLA_KEY_WIDTH), F32)], axis=0).astype(BF16)
    gb = jnp.concatenate([bfw, bbw])[None, :]
    rc, rs1, rs2 = _rope_lane_tables(S)
    nS = S // tm
    row = lambda i: (i, 0)
    const = lambda i: (0, 0)
    pos = lambda i: (i % nS, 0)
    return pl.pallas_call(
        _inproj_kernel,
        grid=(T // tm,),
        in_specs=[
            pl.BlockSpec((tm, D_MODEL), row),
            pl.BlockSpec((1, D_MODEL), const),
            pl.BlockSpec((D_MODEL, o_lr), const),
            pl.BlockSpec((D_MODEL, LANES), lambda i: (0, o_lr // LANES)),
            pl.BlockSpec((D_MODEL, 3 * ATT_WIDTH), const),
            pl.BlockSpec((LANES, 2 * GLA_KEY_WIDTH), const),
            pl.BlockSpec((1, 2 * GLA_KEY_WIDTH), const),
            pl.BlockSpec((tm, LANES), pos),
            pl.BlockSpec((tm, LANES), pos),
            pl.BlockSpec((tm, LANES), pos),
        ],
        out_specs=[
            pl.BlockSpec((tm, o_lr - GLA_VAL_WIDTH), row),
            pl.BlockSpec((tm, GLA_VAL_WIDTH), row),
            pl.BlockSpec((tm, 2 * GLA_KEY_WIDTH), row),
            pl.BlockSpec((None, ATT_CLASSES, tm // ATT_CLASSES, 3 * ATT_WIDTH),
                         lambda i: (i // nS, 0, i % nS, 0)),
        ],
        out_shape=[
            jax.ShapeDtypeStruct((T, o_lr - GLA_VAL_WIDTH), F32),
            jax.ShapeDtypeStruct((T, GLA_VAL_WIDTH), BF16),
            jax.ShapeDtypeStruct((T, 2 * GLA_KEY_WIDTH), F32),
            jax.ShapeDtypeStruct((T // S, ATT_CLASSES, S // ATT_CLASSES, 3 * ATT_WIDTH), F32),
        ],
        scratch_shapes=[pltpu.VMEM((3 * ATT_WIDTH // LANES, tm, LANES), F32),
                        pltpu.VMEM((D_MODEL, o_lr), BF16), pltpu.VMEM((D_MODEL, LANES), BF16)],
        compiler_params=_cparams(("arbitrary",)),
        name="inproj",
    )(x2, norm1_w[None, :], w_in, w_in, wa, gw, gb, rc, rs1, rs2)


def _gla_decays(q, k, v, la, forward, G):
    C = GLA_CHUNK
    R = G * C
    r = lax.broadcasted_iota(jnp.int32, (R, R), 0)
    c = lax.broadcasted_iota(jnp.int32, (R, R), 1)
    same = (r >> GLA_CHUNK_SHIFT) == (c >> GLA_CHUNK_SHIFT)
    tri = (c <= r) if forward else (c >= r)
    t_mat = jnp.where(same, jnp.where(tri, 1.0, 0.0), 0.0).astype(BF16)
    hi = la.astype(BF16)
    lo = (la - hi.astype(F32)).astype(BF16)
    b = _dot(t_mat, hi) + _dot(t_mat, lo)
    edge = C - 1 if forward else 0
    tot = jnp.concatenate([jnp.broadcast_to(b[g * C + edge:g * C + edge + 1], (C, GLA_KEY_WIDTH))
                           for g in range(G)], axis=0)
    order = list(range(G)) if forward else list(range(G - 1, -1, -1))
    return dict(q_dec=q * jnp.exp(b), k_inv=(k * jnp.exp(-b)).astype(BF16), k_end=k * jnp.exp(tot - b),
                tot=tot, vb=v.astype(BF16), order=order, forward=forward, G=G)


def _gla_scores(prep):
    C, H = GLA_CHUNK, GLA_HEADS
    lane_k = lax.broadcasted_iota(jnp.int32, (C, GLA_KEY_WIDTH), 1)
    qd_heads, scores = {}, {}
    for g in prep["order"]:
        rows = slice(g * C, (g + 1) * C)
        qd = prep["q_dec"][rows]
        qd_heads[g] = jnp.concatenate([jnp.where((lane_k >> GLA_DK_SHIFT) == h, qd, 0.0) for h in range(H)],
                                      axis=0).astype(BF16)
        scores[g] = _dot_nt(qd_heads[g], prep["k_inv"][rows])
    return qd_heads, scores


def _gla_chunk_updates(prep):
    C, H, G = GLA_CHUNK, GLA_HEADS, prep["G"]
    k_end, tot, vb = prep["k_end"], prep["tot"], prep["vb"]
    kv, dec_t = {}, {}
    lane = lax.broadcasted_iota(jnp.int32, (GLA_KEY_WIDTH, 2 * C), 1)
    zeros = jnp.zeros((C, GLA_DV), BF16)
    for p in range(G // 2):
        pair = slice(2 * p * C, (2 * p + 2) * C)
        ke_t = k_end[pair].T.astype(BF16)
        tot_t = tot[pair].T
        swapped = pltpu.roll(tot_t, C, 1)
        for half in range(2):
            g = 2 * p + half
            rows = slice(g * C, (g + 1) * C)
            own = (lane < C) if half == 0 else (lane >= C)
            dec_t[g] = jnp.exp(jnp.where(own, tot_t, swapped))
            parts = []
            for h in range(H):
                v_h = vb[rows, h * GLA_DV:(h + 1) * GLA_DV]
                v_pad = jnp.concatenate([v_h, zeros] if half == 0 else [zeros, v_h], axis=0)
                parts.append(_dot(ke_t[h * C:(h + 1) * C], v_pad))
            kv[g] = jnp.concatenate(parts, axis=0)
    return kv, dec_t


def _gla_states(prep, kv, dec_t, s_ref):
    st = s_ref[...]
    states = {}
    for g in prep["order"]:
        states[g] = st.astype(BF16)
        st = st * dec_t[g] + kv[g]
    s_ref[...] = st
    return states


def _gla_outputs(prep, qd_heads, scores, inter, o_ref):
    C, H = GLA_CHUNK, GLA_HEADS
    row_q = lax.broadcasted_iota(jnp.int32, (H * C, C), 0) & (C - 1)
    col_k = lax.broadcasted_iota(jnp.int32, (H * C, C), 1)
    a_mask = (col_k <= row_q) if prep["forward"] else (col_k >= row_q)
    for g in prep["order"]:
        rows = slice(g * C, (g + 1) * C)
        a = jnp.where(a_mask, scores[g], 0.0).astype(BF16)
        vv = prep["vb"][rows]
        o_ref[rows, :] = jnp.concatenate(
            [_dot(a[h * C:(h + 1) * C], vv[:, h * GLA_DV:(h + 1) * GLA_DV]) + inter[g][h * C:(h + 1) * C]
             for h in range(H)], axis=1).astype(o_ref.dtype)


def _gla_kernel(qf_ref, kf_ref, vf_ref, laf_ref, qb_ref, kb_ref, vb_ref, lab_ref,
                of_ref, ob_ref, sf_ref, sb_ref, *, G):
    @pl.when(pl.program_id(1) == 0)
    def _():
        sf_ref[...] = jnp.zeros_like(sf_ref)
        sb_ref[...] = jnp.zeros_like(sb_ref)

    dirs = [(_gla_decays(qf_ref[...], kf_ref[...], vf_ref[...], laf_ref[...], True, G), sf_ref, of_ref),
            (_gla_decays(qb_ref[...], kb_ref[...], vb_ref[...], lab_ref[...], False, G), sb_ref, ob_ref)]
    scored = [_gla_scores(prep) for prep, _, _ in dirs]
    updates = [_gla_chunk_updates(prep) for prep, _, _ in dirs]
    states = [_gla_states(prep, kv, dec_t, s_ref) for (prep, s_ref, _), (kv, dec_t) in zip(dirs, updates)]
    inters = [{g: _dot(qd_heads[g], st[g]) for g in prep["order"]}
              for (prep, _, _), (qd_heads, _), st in zip(dirs, scored, states)]
    for (prep, _, o_ref), (qd_heads, scores), inter in zip(dirs, scored, inters):
        _gla_outputs(prep, qd_heads, scores, inter, o_ref)


def _gla(gla_slab, loga, B, S, G=8):
    T = B * S
    R = G * GLA_CHUNK
    ns = S // R
    fwd = lambda col: (lambda b, i: (b * ns + i, col))
    bwd = lambda col: (lambda b, i: (b * ns + ns - 1 - i, col))
    kw, vw = GLA_KEY_WIDTH, GLA_VAL_WIDTH
    return pl.pallas_call(
        functools.partial(_gla_kernel, G=G),
        grid=(B, ns),
        in_specs=[
            pl.BlockSpec((R, kw), fwd(0)), pl.BlockSpec((R, kw), fwd(1)),
            pl.BlockSpec((R, vw), fwd(1)), pl.BlockSpec((R, kw), fwd(0)),
            pl.BlockSpec((R, kw), bwd(0)), pl.BlockSpec((R, kw), bwd(1)),
            pl.BlockSpec((R, vw), bwd(1)), pl.BlockSpec((R, kw), bwd(1)),
        ],
        out_specs=[pl.BlockSpec((R, vw), fwd(0)), pl.BlockSpec((R, vw), bwd(0))],
        out_shape=[jax.ShapeDtypeStruct((T, vw), BF16), jax.ShapeDtypeStruct((T, vw), BF16)],
        scratch_shapes=[pltpu.VMEM((kw, GLA_DV), F32), pltpu.VMEM((kw, GLA_DV), F32)],
        compiler_params=_cparams(("arbitrary", "arbitrary")),
        name="gla",
    )(gla_slab, gla_slab, gla_slab, loga, gla_slab, gla_slab, gla_slab, loga)


ATT_CLASSES = 4
ATT_QB = 128
ATT_KB = ATT_QB + 2 * ATT_RADIUS


ATT_UNROLL = 4


def _att_kernel(q_ref, k_ref, v_ref, o_ref, m_ref, l_ref, bias_ref, *, S):
    QB, KB, NC = ATT_QB, ATT_KB, ATT_CLASSES
    L4 = S // NC
    lane = lax.broadcasted_iota(jnp.int32, (QB, LANES), 1)
    head0 = lane < ATT_HEAD_DIM

    @pl.when((pl.program_id(0) == 0) & (pl.program_id(1) == 0))
    def _():
        rowi = lax.broadcasted_iota(jnp.int32, (2 * QB, KB), 0) & (QB - 1)
        coli = lax.broadcasted_iota(jnp.int32, (2 * QB, KB), 1)
        qpos = (rowi & (QB // NC - 1)) * NC + (rowi >> _log2(QB // NC))
        kpos = (coli & (KB // NC - 1)) * NC + (coli >> _log2(KB // NC))
        for case in range(3):
            bias_ref[0, case] = jnp.where(jnp.abs(rowi - coli + case * ATT_RADIUS) <= ATT_RADIUS, 0.0, NEG_INF)
            bias_ref[1, case] = jnp.where(jnp.abs(qpos - kpos + case * ATT_RADIUS) <= ATT_RADIUS, 0.0, NEG_INF)

    for pi, (_, d) in enumerate(DILATED_PATTERNS):
        L = S // d
        nb = L // QB
        shift = nb.bit_length() - 1
        first = pi == 0
        last = pi == len(DILATED_PATTERNS) - 1

        def scores(n, d=d, L=L, nb=nb, shift=shift):
            cls = n >> shift
            q0 = (n & (nb - 1)) * QB
            ws = jnp.clip(q0 - ATT_RADIUS, 0, L - KB)
            if d == 1:
                qsls = [pl.ds(pl.multiple_of(c * L4 + q0 // NC, QB // NC), QB // NC) for c in range(NC)]
                ksls = [pl.ds(pl.multiple_of(c * L4 + ws // NC, ATT_RADIUS // NC), KB // NC) for c in range(NC)]
            elif d == NC:
                qsls = [pl.ds(pl.multiple_of(cls * L4 + q0, QB), QB)]
                ksls = [pl.ds(pl.multiple_of(cls * L4 + ws, ATT_RADIUS), KB)]
            else:
                base = (cls & (NC - 1)) * L4 + (cls >> _log2(NC))
                qsls = [pl.ds(base + NC * q0, QB, stride=NC)]
                ksls = [pl.ds(base + NC * ws, KB, stride=NC)]
            q = jnp.concatenate([q_ref[sl, :] for sl in qsls], axis=0)
            kw = jnp.concatenate([k_ref[sl, :] for sl in ksls], axis=0)
            q2 = jnp.concatenate([jnp.where(head0, q, 0.0), jnp.where(head0, 0.0, q)], axis=0).astype(BF16)
            s = _dot_nt(q2, kw.astype(BF16))
            return qsls, ksls, s + bias_ref[1 if d == 1 else 0, (q0 - ws) >> _log2(ATT_RADIUS)]

        def softmax_pv(qsls, ksls, s):
            m_blk = jnp.max(s, axis=-1, keepdims=True)
            p = jnp.exp2(s - m_blk)
            vw = jnp.concatenate([v_ref[sl, :] for sl in ksls], axis=0)
            v_ones = jnp.concatenate([vw.astype(BF16), jnp.ones((KB, LANES), BF16)], axis=1)
            pv = _dot(p.astype(BF16), v_ones)
            acc_b = jnp.where(head0, pv[:QB, :LANES], pv[QB:, :LANES])
            m_b = jnp.where(head0, m_blk[:QB], m_blk[QB:])
            l_b = jnp.where(head0, pv[:QB, LANES:], pv[QB:, LANES:])
            return qsls, acc_b, m_b, l_b

        def load(ref, sls):
            return jnp.concatenate([ref[sl, :] for sl in sls], axis=0)

        def store(ref, sls, val):
            n = val.shape[0] // len(sls)
            for i, sl in enumerate(sls):
                ref[sl, :] = val[i * n:(i + 1) * n]

        def body(n, carry, first=first, last=last):
            staged = [scores(n * ATT_UNROLL + u) for u in range(ATT_UNROLL)]
            blocks = [softmax_pv(*st) for st in staged]
            for qsls, acc_b, m_b, l_b in blocks:
                if first:
                    acc, m_new, l_new = acc_b, m_b, l_b
                else:
                    m_old = load(m_ref, qsls)
                    m_new = jnp.maximum(m_old, m_b)
                    w_old = jnp.exp2(m_old - m_new)
                    w_blk = jnp.exp2(m_b - m_new)
                    acc = load(o_ref, qsls) * w_old + acc_b * w_blk
                    l_new = load(l_ref, qsls) * w_old + l_b * w_blk
                if last:
                    store(o_ref, qsls, acc / l_new)
                else:
                    store(o_ref, qsls, acc)
                    store(m_ref, qsls, m_new)
                    store(l_ref, qsls, l_new)
            return carry

        lax.fori_loop(0, S // (QB * ATT_UNROLL), body, 0)


def _attention(att_slab, B, S):
    T = B * S
    ncol = ATT_WIDTH // LANES
    return pl.pallas_call(
        functools.partial(_att_kernel, S=S),
        grid=(B, ncol),
        in_specs=[
            pl.BlockSpec((S, LANES), lambda b, h: (b, h)),
            pl.BlockSpec((S, LANES), lambda b, h: (b, ncol + h)),
            pl.BlockSpec((S, LANES), lambda b, h: (b, 2 * ncol + h)),
        ],
        out_specs=pl.BlockSpec((S, LANES), lambda b, h: (b, h)),
        out_shape=jax.ShapeDtypeStruct((T, ATT_WIDTH), F32),
        scratch_shapes=[pltpu.VMEM((S, LANES), F32), pltpu.VMEM((S, LANES), F32),
                        pltpu.VMEM((2, 3, 2 * ATT_QB, ATT_KB), F32)],
        compiler_params=_cparams(("arbitrary", "arbitrary")),
        name="dilated_attention",
    )(att_slab, att_slab, att_slab)


PACK_WORDS = D_MODEL // 2
ROW_TILE = PACK_WORDS // LANES
HIGH_HALF = -65536


def _pack_rows(x):
    bits = lambda v: lax.bitcast_convert_type(v.astype(BF16).astype(F32), jnp.int32)
    low = (bits(x[:, :PACK_WORDS]) >> 16) & 0xFFFF
    return (bits(x[:, PACK_WORDS:]) & HIGH_HALF) | low


def _unpack_rows(w):
    low = lax.bitcast_convert_type(w << 16, F32)
    high = lax.bitcast_convert_type(w & HIGH_HALF, F32)
    return jnp.concatenate([low, high], axis=1).astype(BF16)


def _to_row_tiles(ref, w):
    n = w.shape[0]
    for j in range(ROW_TILE):
        ref[pl.ds(j, n, stride=ROW_TILE), :] = w[:, j * LANES:(j + 1) * LANES]


def _from_row_tiles(ref, n):
    return jnp.concatenate([ref[pl.ds(j, n, stride=ROW_TILE), :] for j in range(ROW_TILE)], axis=1)


def _tile_copy(src_ref, src_row, dst_ref, dst_row, sem):
    src = pl.ds(pl.multiple_of(src_row * ROW_TILE, ROW_TILE), ROW_TILE)
    dst = pl.ds(pl.multiple_of(dst_row * ROW_TILE, ROW_TILE), ROW_TILE)
    return pltpu.make_async_copy(src_ref.at[src], dst_ref.at[dst], sem)


def _outproj_kernel(of_ref, ob_ref, gg_ref, att_ref, x_ref, gnw_ref, wo1_ref, wo2_ref,
                    n2_ref, wr_ref, br_ref, h_ref, u_ref, lg_ref, stage_ref):
    rows = stage_ref.shape[1] // ATT_CLASSES
    for j in range(ATT_WIDTH // LANES):
        for c in range(ATT_CLASSES):
            stage_ref[j, pl.ds(c, rows, stride=ATT_CLASSES), :] = att_ref[c, :, j * LANES:(j + 1) * LANES]
    att = jnp.concatenate([stage_ref[j] for j in range(ATT_WIDTH // LANES)], axis=1)
    o = of_ref[...].astype(F32) + ob_ref[...].astype(F32)
    gate = gg_ref[...].astype(F32)
    gnw = gnw_ref[...]
    parts = []
    for h in range(GLA_HEADS):
        sl = slice(h * GLA_DV, (h + 1) * GLA_DV)
        parts.append(_rms(o[:, sl], gnw))
    y = jnp.concatenate(parts, axis=1) * (gate / (1.0 + jnp.exp(-gate)))
    mix = _dot(y.astype(BF16), wo1_ref[...]) + _dot(att.astype(BF16), wo2_ref[...])
    h = x_ref[...] + mix
    h_ref[...] = h
    u = _rms(h, n2_ref[...])
    _to_row_tiles(u_ref, _pack_rows(u))
    u_hi = u.astype(BF16)
    u_lo = (u - u_hi.astype(F32)).astype(BF16)
    hi_both = _dot_nt(wr_ref[...], u_hi)
    lg_ref[...] = (hi_both[:LANES] + hi_both[LANES:] + _dot_nt(wr_ref[:LANES], u_lo)) + br_ref[...]


def _outproj(o_f, o_b, gate, att_out, x2, gla_norm_w, w_out, norm2_w, wr, br, tm=512):
    T = x2.shape[0]
    nS = att_out.shape[2] * ATT_CLASSES // tm
    row = lambda i: (i, 0)
    const = lambda i: (0, 0)
    wo = w_out.astype(BF16)
    wr_hi = wr.astype(BF16)
    wr_lo = (wr - wr_hi.astype(F32)).astype(BF16)
    wr = jnp.concatenate([wr_hi, wr_lo], axis=0)
    return pl.pallas_call(
        _outproj_kernel,
        grid=(T // tm,),
        in_specs=[
            pl.BlockSpec((tm, GLA_VAL_WIDTH), row),
            pl.BlockSpec((tm, GLA_VAL_WIDTH), row),
            pl.BlockSpec((tm, GLA_VAL_WIDTH), row),
            pl.BlockSpec((None, ATT_CLASSES, tm // ATT_CLASSES, ATT_WIDTH), lambda i: (i // nS, 0, i % nS, 0)),
            pl.BlockSpec((tm, D_MODEL), row),
            pl.BlockSpec((1, GLA_DV), const),
            pl.BlockSpec((GLA_VAL_WIDTH, D_MODEL), const),
            pl.BlockSpec((ATT_WIDTH, D_MODEL), lambda i: (GLA_VAL_WIDTH // ATT_WIDTH, 0)),
            pl.BlockSpec((1, D_MODEL), const),
            pl.BlockSpec((2 * LANES, D_MODEL), const),
            pl.BlockSpec((LANES, 1), const),
        ],
        out_specs=[
            pl.BlockSpec((tm, D_MODEL), row),
            pl.BlockSpec((tm * ROW_TILE, LANES), row),
            pl.BlockSpec((LANES, tm), lambda i: (0, i)),
        ],
        out_shape=[
            jax.ShapeDtypeStruct((T, D_MODEL), F32),
            jax.ShapeDtypeStruct((T * ROW_TILE, LANES), jnp.int32),
            jax.ShapeDtypeStruct((LANES, T), F32),
        ],
        scratch_shapes=[pltpu.VMEM((ATT_WIDTH // LANES, tm, LANES), F32)],
        compiler_params=_cparams(("arbitrary",)),
        name="outproj",
    )(o_f, o_b, gate, att_out, x2, gla_norm_w[None, :], wo, wo,
      norm2_w[None, :], wr, br)


INFO_E1, INFO_E2, INFO_R1, INFO_R2, INFO_W1, INFO_W2 = range(6)
ROUTE_ROWS = 40


def _route_kernel(lg_ref, info_ref, cnt_ref, carry_ref):
    @pl.when(pl.program_id(0) == 0)
    def _():
        carry_ref[...] = jnp.zeros_like(carry_ref)

    lg = lg_ref[:ROUTE_ROWS, :]
    tr = lg.shape[1]
    row = lax.broadcasted_iota(jnp.int32, (ROUTE_ROWS, tr), 0)
    big = jnp.int32(1 << 20)
    is_g = (row >= MOE_N_EXPERTS) & (row < MOE_N_EXPERTS + MOE_GROUPS)
    gl = jnp.where(is_g, lg, -jnp.inf)
    gmax = jnp.max(gl, axis=0, keepdims=True)
    gsel = jnp.min(jnp.where(gl == gmax, row - MOE_N_EXPERTS, big), axis=0, keepdims=True)
    g_w = 1.0 / jnp.sum(jnp.where(is_g, jnp.exp(lg - gmax), 0.0), axis=0, keepdims=True)
    in_grp = (row < MOE_N_EXPERTS) & ((row >> MOE_GROUP_SHIFT) == gsel)
    el = jnp.where(in_grp, lg, -jnp.inf)
    v1 = jnp.max(el, axis=0, keepdims=True)
    i1 = jnp.min(jnp.where(el == v1, row, big), axis=0, keepdims=True)
    el2 = jnp.where(row == i1, -jnp.inf, el)
    v2 = jnp.max(el2, axis=0, keepdims=True)
    i2 = jnp.min(jnp.where(el2 == v2, row, big), axis=0, keepdims=True)
    t = jnp.exp(v2 - v1)
    w1 = g_w * (1.0 / (1.0 + t))
    w2 = g_w * (t / (1.0 + t))

    erow = lax.broadcasted_iota(jnp.int32, (MOE_N_EXPERTS, tr), 0)
    hit1 = erow == i1
    hit2 = erow == i2
    member = jnp.where(hit1 | hit2, 1.0, 0.0)
    r = lax.broadcasted_iota(jnp.int32, (tr, tr), 0)
    c = lax.broadcasted_iota(jnp.int32, (tr, tr), 1)
    earlier = jnp.where(r < c, 1.0, 0.0).astype(BF16)
    carry = carry_ref[...]
    prefix = _dot(member.astype(BF16), earlier) + carry[:, 0:1]
    rank1 = jnp.sum(jnp.where(hit1, prefix, 0.0), axis=0, keepdims=True)
    rank2 = jnp.sum(jnp.where(hit2, prefix, 0.0), axis=0, keepdims=True)
    carry = carry + jnp.sum(member, axis=1, keepdims=True)
    carry_ref[...] = carry
    cnt_ref[...] = carry

    zero = jnp.zeros_like(w1)
    info_ref[...] = jnp.concatenate([i1.astype(F32), i2.astype(F32), rank1, rank2, w1, w2, zero, zero], axis=0)


def _route(logits_t, tr=1024):
    T = logits_t.shape[1]
    return pl.pallas_call(
        _route_kernel,
        grid=(T // tr,),
        in_specs=[pl.BlockSpec((LANES, tr), lambda i: (0, i))],
        out_specs=[pl.BlockSpec((8, tr), lambda i: (0, i)),
                   pl.BlockSpec((MOE_N_EXPERTS, LANES), lambda i: (0, 0))],
        out_shape=[jax.ShapeDtypeStruct((8, T), F32), jax.ShapeDtypeStruct((MOE_N_EXPERTS, LANES), F32)],
        scratch_shapes=[pltpu.VMEM((MOE_N_EXPERTS, LANES), F32)],
        compiler_params=_cparams(("arbitrary",)),
        name="route",
    )(logits_t)


ROW_UNROLL = 16


def _dispatch_kernel(dest_ref, pend_ref, u_ref, xs_ref, zbuf, sem, zsem, *, td, nblk):
    @pl.when(pl.program_id(0) == 0)
    def _():
        zbuf[...] = jnp.zeros_like(zbuf)
        n_used = pend_ref[MOE_N_EXPERTS - 1] >> MOE_ROWS_SHIFT

        def zero_copy(blk):
            start = pl.multiple_of(blk * (MOE_ROWS * ROW_TILE), MOE_ROWS * ROW_TILE)
            return pltpu.make_async_copy(zbuf, xs_ref.at[pl.ds(start, MOE_ROWS * ROW_TILE)], zsem)

        def each_pad_block(fn):
            def per_expert(e, carry):
                prev = jnp.where(e > 0, pend_ref[jnp.maximum(e - 1, 0)], 0)

                @pl.when(pend_ref[e] > prev)
                def _():
                    fn((pend_ref[e] >> MOE_ROWS_SHIFT) - 1)
                return carry

            def per_tail(j, carry):
                @pl.when(n_used + j < nblk)
                def _():
                    fn(n_used + j)
                return carry

            lax.fori_loop(0, MOE_N_EXPERTS, per_expert, 0)
            lax.fori_loop(0, MOE_N_EXPERTS, per_tail, 0)

        each_pad_block(lambda blk: zero_copy(blk).start())
        each_pad_block(lambda blk: zero_copy(blk).wait())

    base = pl.program_id(0) * td

    def issue(g, carry):
        for j in range(ROW_UNROLL):
            r = g * ROW_UNROLL + j
            for k in range(MOE_TOP_K):
                _tile_copy(u_ref, r, xs_ref, dest_ref[k, base + r], sem).start(priority=k)
        return carry

    lax.fori_loop(0, td // ROW_UNROLL, issue, 0)
    for k in range(MOE_TOP_K):
        pltpu.make_async_copy(u_ref, xs_ref.at[pl.ds(0, td * ROW_TILE)], sem).wait()


def _dispatch(dest, pend, u2, cap, td=2048):
    T = u2.shape[0] // ROW_TILE
    return pl.pallas_call(
        functools.partial(_dispatch_kernel, td=td, nblk=cap // MOE_ROWS),
        grid_spec=pltpu.PrefetchScalarGridSpec(
            num_scalar_prefetch=2,
            grid=(T // td,),
            in_specs=[pl.BlockSpec((td * ROW_TILE, LANES), lambda i, d, z: (i, 0))],
            out_specs=pl.BlockSpec(memory_space=pl.ANY),
            scratch_shapes=[pltpu.VMEM((MOE_ROWS * ROW_TILE, LANES), jnp.int32),
                            pltpu.SemaphoreType.DMA(()), pltpu.SemaphoreType.DMA(())],
        ),
        out_shape=jax.ShapeDtypeStruct((cap * ROW_TILE, LANES), jnp.int32),
        compiler_params=_cparams(("arbitrary",)),
        name="dispatch",
    )(dest, pend, u2)


def _expert_kernel(pend_ref, xs_hbm, wg_hbm, wu_hbm, wd_hbm, ys_hbm,
                   xbuf, ybuf, zbuf, stage_g, stage_u, stage_d, wgb, wub, wdb, xsem, ysem, wsem, zsem, *, nblk):
    last = MOE_N_EXPERTS - 1
    n_used = pend_ref[last] >> MOE_ROWS_SHIFT
    block_rows = MOE_ROWS * ROW_TILE

    def x_copy(b, slot):
        start = pl.multiple_of(b * block_rows, block_rows)
        return pltpu.make_async_copy(xs_hbm.at[pl.ds(start, block_rows)], xbuf.at[slot], xsem.at[slot])

    def y_copy(b, slot):
        start = pl.multiple_of(b * block_rows, block_rows)
        return pltpu.make_async_copy(ybuf.at[slot], ys_hbm.at[pl.ds(start, block_rows)], ysem.at[slot])

    def zero_copy(b):
        start = pl.multiple_of(b * block_rows, block_rows)
        return pltpu.make_async_copy(zbuf, ys_hbm.at[pl.ds(start, block_rows)], zsem)

    def weight_copies(e):
        return (pltpu.make_async_copy(wg_hbm.at[e], stage_g, wsem.at[0]),
                pltpu.make_async_copy(wu_hbm.at[e], stage_u, wsem.at[1]),
                pltpu.make_async_copy(wd_hbm.at[e], stage_d, wsem.at[2]))

    def owner(start, row):
        return lax.while_loop(lambda e: (e < last) & (pend_ref[e] <= row), lambda e: e + 1, start)

    for c in weight_copies(owner(0, 0)):
        c.start()
    x_copy(0, 0).start()

    zbuf[...] = jnp.zeros_like(zbuf)

    def tail(fn):
        def step(b, carry):
            fn(b)
            return carry
        lax.fori_loop(n_used, nblk, step, 0)

    tail(lambda b: zero_copy(b).start())

    def body(b, cur):
        slot = b & 1
        e = owner(jnp.maximum(cur, 0), b * MOE_ROWS)
        x_copy(b, slot).wait()

        @pl.when(b + 1 < n_used)
        def _():
            x_copy(b + 1, 1 - slot).start()

        @pl.when(e != cur)
        def _():
            for c in weight_copies(e):
                c.wait()
            wgb[...] = stage_g[...].astype(BF16)
            wub[...] = stage_u[...].astype(BF16)
            wdb[...] = stage_d[...].astype(BF16)

            @pl.when(pend_ref[e] < pend_ref[last])
            def _():
                for c in weight_copies(owner(e + 1, pend_ref[e])):
                    c.start(priority=1)

        @pl.when(b >= 2)
        def _():
            y_copy(b - 2, slot).wait()

        xb = _unpack_rows(_from_row_tiles(xbuf.at[slot], MOE_ROWS))
        g = _dot(xb, wgb[...])
        u = _dot(xb, wub[...])
        hid = (g / (1.0 + jnp.exp(-g))) * u
        _to_row_tiles(ybuf.at[slot], _pack_rows(_dot(hid.astype(BF16), wdb[...])))
        y_copy(b, slot).start()
        return e

    lax.fori_loop(0, n_used, body, jnp.int32(-1))

    @pl.when(n_used >= 2)
    def _():
        y_copy(n_used - 2, n_used & 1).wait()
    y_copy(n_used - 1, (n_used - 1) & 1).wait()
    tail(lambda b: zero_copy(b).wait())


def _experts(pend, xs, w_gate, w_up, w_down):
    cap = xs.shape[0] // ROW_TILE
    nblk = cap // MOE_ROWS
    block = (MOE_ROWS * ROW_TILE, LANES)
    anywhere = pl.BlockSpec(memory_space=pl.ANY)
    return pl.pallas_call(
        functools.partial(_expert_kernel, nblk=nblk),
        grid_spec=pltpu.PrefetchScalarGridSpec(
            num_scalar_prefetch=1,
            grid=(1,),
            in_specs=[anywhere, anywhere, anywhere, anywhere],
            out_specs=anywhere,
            scratch_shapes=[pltpu.VMEM((2,) + block, jnp.int32),
                            pltpu.VMEM((2,) + block, jnp.int32),
                            pltpu.VMEM(block, jnp.int32),
                            pltpu.VMEM((D_MODEL, MOE_D_FF), F32),
                            pltpu.VMEM((D_MODEL, MOE_D_FF), F32),
                            pltpu.VMEM((MOE_D_FF, D_MODEL), F32),
                            pltpu.VMEM((D_MODEL, MOE_D_FF), BF16),
                            pltpu.VMEM((D_MODEL, MOE_D_FF), BF16),
                            pltpu.VMEM((MOE_D_FF, D_MODEL), BF16),
                            pltpu.SemaphoreType.DMA((2,)),
                            pltpu.SemaphoreType.DMA((2,)),
                            pltpu.SemaphoreType.DMA((3,)),
                            pltpu.SemaphoreType.DMA(())],
        ),
        out_shape=jax.ShapeDtypeStruct((cap * ROW_TILE, LANES), jnp.int32),
        compiler_params=_cparams(("arbitrary",)),
        name="experts",
    )(pend, xs, w_gate, w_up, w_down)


def _combine_kernel(dest_ref, ys_ref, info_ref, h_ref, fw_ref, o_ref, buf, sem, *, tc):
    i = pl.program_id(0)
    n = pl.num_programs(0)

    def issue(step, slot):
        base = step * tc

        def body(g, carry):
            for j in range(ROW_UNROLL):
                r = g * ROW_UNROLL + j
                for k in range(MOE_TOP_K):
                    _tile_copy(ys_ref, dest_ref[k, base + r], buf.at[slot, k], r,
                               sem.at[slot]).start(priority=k)
            return carry

        lax.fori_loop(0, tc // ROW_UNROLL, body, 0)

    @pl.when(i == 0)
    def _():
        issue(0, 0)

    slot = i % 2

    @pl.when(i + 1 < n)
    def _():
        issue(i + 1, 1 - slot)

    for k in range(MOE_TOP_K):
        pltpu.make_async_copy(ys_ref.at[pl.ds(0, tc * ROW_TILE)], buf.at[slot, k], sem.at[slot]).wait()

    info_t = jnp.concatenate([info_ref[...]] * (LANES // 8), axis=0).T
    w1 = info_t[:, INFO_W1:INFO_W1 + 1]
    w2 = info_t[:, INFO_W2:INFO_W2 + 1]
    y1 = _unpack_rows(_from_row_tiles(buf.at[slot, 0], tc)).astype(F32)
    y2 = _unpack_rows(_from_row_tiles(buf.at[slot, 1], tc)).astype(F32)
    h = h_ref[...] + (y1 * w1 + y2 * w2)
    o_ref[...] = _rms(h, fw_ref[...])


def _combine(dest, ys, info, h, final_w, tc=512):
    T = h.shape[0]
    return pl.pallas_call(
        functools.partial(_combine_kernel, tc=tc),
        grid_spec=pltpu.PrefetchScalarGridSpec(
            num_scalar_prefetch=1,
            grid=(T // tc,),
            in_specs=[pl.BlockSpec(memory_space=pl.ANY),
                      pl.BlockSpec((8, tc), lambda i, d: (0, i)),
                      pl.BlockSpec((tc, D_MODEL), lambda i, d: (i, 0)),
                      pl.BlockSpec((1, D_MODEL), lambda i, d: (0, 0))],
            out_specs=pl.BlockSpec((tc, D_MODEL), lambda i, d: (i, 0)),
            scratch_shapes=[pltpu.VMEM((2, MOE_TOP_K, tc * ROW_TILE, LANES), jnp.int32),
                            pltpu.SemaphoreType.DMA((2,))],
        ),
        out_shape=jax.ShapeDtypeStruct((T, D_MODEL), F32),
        compiler_params=_cparams(("arbitrary",)),
        name="combine",
    )(dest, ys, info, h, final_w[None, :])


def _plan_kernel(info_ref, cnt_ref, dest_ref, pend_ref):
    cnt = cnt_ref[...].astype(jnp.int32)
    nblk_e = ((cnt + (MOE_ROWS - 1)) >> MOE_ROWS_SHIFT).astype(F32)
    r = lax.broadcasted_iota(jnp.int32, (MOE_N_EXPERTS, MOE_N_EXPERTS), 0)
    c = lax.broadcasted_iota(jnp.int32, (MOE_N_EXPERTS, MOE_N_EXPERTS), 1)
    before = jnp.where(c < r, 1.0, 0.0).astype(BF16)
    first_blk = _dot(before, nblk_e.astype(BF16))
    pstart = first_blk[:, 0:1] * float(MOE_ROWS)
    pend_ref[...] = ((first_blk + nblk_e) * float(MOE_ROWS)).astype(jnp.int32)

    info = info_ref[...]
    erow = lax.broadcasted_iota(jnp.int32, (MOE_N_EXPERTS, info.shape[1]), 0)
    start_of = lambda e: jnp.sum(jnp.where(erow == e.astype(jnp.int32), pstart, 0.0), axis=0, keepdims=True)
    d1 = info[INFO_R1:INFO_R1 + 1] + start_of(info[INFO_E1:INFO_E1 + 1])
    d2 = info[INFO_R2:INFO_R2 + 1] + start_of(info[INFO_E2:INFO_E2 + 1])
    zero = jnp.zeros_like(d1)
    dest_ref[...] = jnp.concatenate([d1, d2] + [zero] * 6, axis=0).astype(jnp.int32)


def _plan(info, counts, tr=2048):
    T = info.shape[1]
    dest8, pend = pl.pallas_call(
        _plan_kernel,
        grid=(T // tr,),
        in_specs=[pl.BlockSpec((8, tr), lambda i: (0, i)),
                  pl.BlockSpec((MOE_N_EXPERTS, LANES), lambda i: (0, 0))],
        out_specs=[pl.BlockSpec((8, tr), lambda i: (0, i)),
                   pl.BlockSpec((MOE_N_EXPERTS, LANES), lambda i: (0, 0))],
        out_shape=[jax.ShapeDtypeStruct((8, T), jnp.int32),
                   jax.ShapeDtypeStruct((MOE_N_EXPERTS, LANES), jnp.int32)],
        compiler_params=_cparams(("arbitrary",)),
        name="plan",
    )(info, counts)
    return dest8, pend[:, 0]


def _moe_capacity(T):
    return (-(-(T * MOE_TOP_K) // MOE_ROWS) + MOE_N_EXPERTS) * MOE_ROWS


def _router_weights(router_group_w, router_group_b, router_expert_w, router_expert_b):
    we = jnp.transpose(router_expert_w, (0, 2, 1)).reshape(MOE_N_EXPERTS, D_MODEL)
    pad = LANES - MOE_N_EXPERTS - MOE_GROUPS
    wr = jnp.concatenate([we, router_group_w.T, jnp.zeros((pad, D_MODEL), F32)], axis=0)
    br = jnp.concatenate([router_expert_b.reshape(-1), router_group_b, jnp.zeros((pad,), F32)])[:, None]
    return wr, br


def kernel(x, norm1_w, w_in, gla_fwd_gate_w, gla_fwd_gate_b, gla_bwd_gate_w, gla_bwd_gate_b,
           gla_norm_w, w_out, norm2_w, router_group_w, router_group_b, router_expert_w,
           router_expert_b, expert_w_gate, expert_w_up, expert_w_down, final_norm_w):
    B, S, D = x.shape
    T = B * S
    assert norm1_w.shape[0] == 1, "single-layer trunk: the final norm is fused into the combine step"
    h = x.reshape(T, D)
    gla_slab, gate, loga, att_slab = _inproj(h, S, norm1_w[0], w_in[0], gla_fwd_gate_w[0], gla_fwd_gate_b[0],
                                       gla_bwd_gate_w[0], gla_bwd_gate_b[0])
    o_f, o_b = _gla(gla_slab, loga, B, S)
    att_out = _attention(att_slab.reshape(T, 3 * ATT_WIDTH), B, S)
    att_out = att_out.reshape(B, ATT_CLASSES, S // ATT_CLASSES, ATT_WIDTH)
    wr, br = _router_weights(router_group_w[0], router_group_b[0], router_expert_w[0], router_expert_b[0])
    h, u2, logits = _outproj(o_f, o_b, gate, att_out, h, gla_norm_w[0], w_out[0], norm2_w[0], wr, br)
    info, counts = _route(logits)
    dest, pend = _plan(info, counts)
    xs = _dispatch(dest, pend, u2, _moe_capacity(T))
    ys = _experts(pend, xs, expert_w_gate[0], expert_w_up[0], expert_w_down[0])
    out = _combine(dest, ys, info, h, final_norm_w)
    return out.reshape(B, S, D)
```

```python
import functools

import jax
import jax.numpy as jnp
import numpy as np
from jax import lax
from jax.experimental import pallas as pl
from jax.experimental.pallas import tpu as pltpu

F32 = jnp.float32
BF16 = jnp.bfloat16

D_MODEL = 1024
GLA_HEADS = 4
GLA_DV = 128
GLA_DK = 64
GLA_KEY_WIDTH = GLA_HEADS * GLA_DK
GLA_VAL_WIDTH = GLA_HEADS * GLA_DV
GLA_GATE_RANK = 16
GLA_TAU = 16.0
GLA_CHUNK = 64
ATT_WIDTH = 512
ATT_HEAD_DIM = 64
ATT_HEADS = 8
ROT_DIM = 16
ROPE_THETA = 500000.0
DILATED_PATTERNS = ((128, 1), (512, 4), (2048, 16))
ATT_RADIUS = 64
MOE_GROUPS = 4
MOE_EXPERTS_PER_GROUP = 8
MOE_N_EXPERTS = 32
MOE_TOP_K = 2
MOE_D_FF = 512
EPS = 1e-6
NEG_INF = -1e30
LOG2E = 1.4426950408889634

LANES = 128
MOE_ROWS = 256


def _log2(n):
    assert n & (n - 1) == 0, n
    return n.bit_length() - 1


GLA_CHUNK_SHIFT = _log2(GLA_CHUNK)
GLA_DK_SHIFT = _log2(GLA_DK)
MOE_ROWS_SHIFT = _log2(MOE_ROWS)
MOE_GROUP_SHIFT = _log2(MOE_EXPERTS_PER_GROUP)
VMEM_LIMIT = 56 * 1024 * 1024


def _cparams(sem):
    return pltpu.CompilerParams(dimension_semantics=sem, vmem_limit_bytes=VMEM_LIMIT)


def _dot(a, b):
    return jnp.dot(a, b, preferred_element_type=F32)


def _dot_nt(a, b):
    return lax.dot_general(a, b, (((1,), (1,)), ((), ())), preferred_element_type=F32)


def _dot_tn(a, b):
    return lax.dot_general(a, b, (((0,), (0,)), ((), ())), preferred_element_type=F32)


def _rms(x, w):
    return x * lax.rsqrt(jnp.mean(x * x, axis=-1, keepdims=True) + EPS) * w


def _inproj_kernel(x_ref, n1_ref, wg_ref, wlr_ref, wa_ref, gw_ref, gb_ref,
                   rc_ref, rs1_ref, rs2_ref, gla_ref, gate_ref, loga_ref, att_ref, stage_ref, wgb, wlrb):
    @pl.when(pl.program_id(0) == 0)
    def _():
        wgb[...] = wg_ref[...].astype(BF16)
        wlrb[...] = wlr_ref[...].astype(BF16)

    x = x_ref[...]
    ub = _rms(x, n1_ref[...]).astype(BF16)
    g = _dot(ub, wgb[...])
    qkv = 2 * GLA_KEY_WIDTH + GLA_VAL_WIDTH
    gla_ref[:, :GLA_KEY_WIDTH] = g[:, :GLA_KEY_WIDTH] * (GLA_DK ** -0.5)
    gla_ref[:, GLA_KEY_WIDTH:] = g[:, GLA_KEY_WIDTH:qkv]
    gate_ref[...] = g[:, qkv:].astype(BF16)
    lr = _dot(ub, wlrb[...])
    gate = _dot(lr.astype(BF16), gw_ref[...]) + gb_ref[...]
    loga_ref[...] = (jnp.minimum(gate, 0.0) - jnp.log(1.0 + jnp.exp(-jnp.abs(gate)))) * (1.0 / GLA_TAU)
    a = _dot(ub, wa_ref[...])
    qk = a[:, :2 * ATT_WIDTH]
    reps = 2 * ATT_WIDTH // LANES
    c = jnp.concatenate([rc_ref[...]] * reps, axis=1)
    s1 = jnp.concatenate([rs1_ref[...]] * reps, axis=1)
    s2 = jnp.concatenate([rs2_ref[...]] * reps, axis=1)
    half = ROT_DIM // 2
    n = 2 * ATT_WIDTH
    roped = qk * c + pltpu.roll(qk, n - half, 1) * s1 + pltpu.roll(qk, half, 1) * s2
    qkv = jnp.concatenate([roped[:, :ATT_WIDTH] * (ATT_HEAD_DIM ** -0.5 * LOG2E), roped[:, ATT_WIDTH:],
                           a[:, 2 * ATT_WIDTH:]], axis=1)
    rows = x.shape[0] // ATT_CLASSES
    for j in range(3 * ATT_WIDTH // LANES):
        cols = slice(j * LANES, (j + 1) * LANES)
        stage_ref[j] = qkv[:, cols]
        for c in range(ATT_CLASSES):
            att_ref[c, :, cols] = stage_ref[j, pl.ds(c, rows, stride=ATT_CLASSES), :]


def _rope_lane_tables(S):
    half = ROT_DIM // 2
    inv = np.float32(ROPE_THETA) ** (-(np.arange(0, ROT_DIM, 2, dtype=np.float32) / np.float32(ROT_DIM)))
    ang = np.arange(S, dtype=np.float32)[:, None] * inv[None, :].astype(np.float32)
    cos, sin = np.cos(ang), np.sin(ang)
    ones = np.ones((S, ATT_HEAD_DIM - ROT_DIM), np.float32)
    zeros = np.zeros((S, ATT_HEAD_DIM - ROT_DIM), np.float32)
    zeros8 = np.zeros((S, half), np.float32)
    rep = LANES // ATT_HEAD_DIM
    c = np.tile(np.concatenate([cos, cos, ones], axis=1), (1, rep))
    s1 = np.tile(np.concatenate([-sin, zeros8, zeros], axis=1), (1, rep))
    s2 = np.tile(np.concatenate([zeros8, sin, zeros], axis=1), (1, rep))
    return jnp.asarray(c), jnp.asarray(s1), jnp.asarray(s2)


def _inproj(x2, S, norm1_w, w_in, wf, bfw, wb, bbw, tm=512):
    T = x2.shape[0]
    o_lr = 2 * GLA_KEY_WIDTH + 2 * GLA_VAL_WIDTH
    o_att = o_lr + 2 * GLA_GATE_RANK
    wa = w_in[:, o_att:].astype(BF16)
    zeros = jnp.zeros((GLA_GATE_RANK, GLA_KEY_WIDTH), F32)
    gw = jnp.concatenate([jnp.concatenate([wf, zeros], axis=1), jnp.concatenate([zeros, wb], axis=1),
                          jnp.zeros((LANES - 2 * GLA_GATE_RANK, 2 * GLA_KEY_WIDTH), F32)], axis=0).astype(BF16)
    gb = jnp.concatenate([bfw, bbw])[None, :]
    rc, rs1, rs2 = _rope_lane_tables(S)
    nS = S // tm
    row = lambda i: (i, 0)
    const = lambda i: (0, 0)
    pos = lambda i: (i % nS, 0)
    return pl.pallas_call(
        _inproj_kernel,
        grid=(T // tm,),
        in_specs=[
            pl.BlockSpec((tm, D_MODEL), row),
            pl.BlockSpec((1, D_MODEL), const),
            pl.BlockSpec((D_MODEL, o_lr), const),
            pl.BlockSpec((D_MODEL, LANES), lambda i: (0, o_lr // LANES)),
            pl.BlockSpec((D_MODEL, 3 * ATT_WIDTH), const),
            pl.BlockSpec((LANES, 2 * GLA_KEY_WIDTH), const),
            pl.BlockSpec((1, 2 * GLA_KEY_WIDTH), const),
            pl.BlockSpec((tm, LANES), pos),
            pl.BlockSpec((tm, LANES), pos),
            pl.BlockSpec((tm, LANES), pos),
        ],
        out_specs=[
            pl.BlockSpec((tm, o_lr - GLA_VAL_WIDTH), row),
            pl.BlockSpec((tm, GLA_VAL_WIDTH), row),
            pl.BlockSpec((tm, 2 * GLA_KEY_WIDTH), row),
            pl.BlockSpec((None, ATT_CLASSES, tm // ATT_CLASSES, 3 * ATT_WIDTH),
                         lambda i: (i // nS, 0, i % nS, 0)),
        ],
        out_shape=[
            jax.ShapeDtypeStruct((T, o_lr - GLA_VAL_WIDTH), F32),
            jax.ShapeDtypeStruct((T, GLA_VAL_WIDTH), BF16),
            jax.ShapeDtypeStruct((T, 2 * GLA_KEY_WIDTH), F32),
            jax.ShapeDtypeStruct((T // S, ATT_CLASSES, S // ATT_CLASSES, 3 * ATT_WIDTH), F32),
        ],
        scratch_shapes=[pltpu.VMEM((3 * ATT_WIDTH // LANES, tm, LANES), F32),
                        pltpu.VMEM((D_MODEL, o_lr), BF16), pltpu.VMEM((D_MODEL, LANES), BF16)],
        compiler_params=_cparams(("arbitrary",)),
        name="inproj",
    )(x2, norm1_w[None, :], w_in, w_in, wa, gw, gb, rc, rs1, rs2)


def _gla_decays(q, k, v, la, forward, G):
    C = GLA_CHUNK
    R = G * C
    r = lax.broadcasted_iota(jnp.int32, (R, R), 0)
    c = lax.broadcasted_iota(jnp.int32, (R, R), 1)
    same = (r >> GLA_CHUNK_SHIFT) == (c >> GLA_CHUNK_SHIFT)
    tri = (c <= r) if forward else (c >= r)
    t_mat = jnp.where(same, jnp.where(tri, 1.0, 0.0), 0.0).astype(BF16)
    hi = la.astype(BF16)
    lo = (la - hi.astype(F32)).astype(BF16)
    b = _dot(t_mat, hi) + _dot(t_mat, lo)
    edge = C - 1 if forward else 0
    tot = jnp.concatenate([jnp.broadcast_to(b[g * C + edge:g * C + edge + 1], (C, GLA_KEY_WIDTH))
                           for g in range(G)], axis=0)
    order = list(range(G)) if forward else list(range(G - 1, -1, -1))
    return dict(q_dec=q * jnp.exp(b), k_inv=(k * jnp.exp(-b)).astype(BF16), k_end=k * jnp.exp(tot - b),
                tot=tot, vb=v.astype(BF16), order=order, forward=forward, G=G)


def _gla_scores(prep):
    C, H = GLA_CHUNK, GLA_HEADS
    lane_k = lax.broadcasted_iota(jnp.int32, (C, GLA_KEY_WIDTH), 1)
    qd_heads, scores = {}, {}
    for g in prep["order"]:
        rows = slice(g * C, (g + 1) * C)
        qd = prep["q_dec"][rows]
        qd_heads[g] = jnp.concatenate([jnp.where((lane_k >> GLA_DK_SHIFT) == h, qd, 0.0) for h in range(H)],
                                      axis=0).astype(BF16)
        scores[g] = _dot_nt(qd_heads[g], prep["k_inv"][rows])
    return qd_heads, scores


def _gla_chunk_updates(prep):
    C, H, G = GLA_CHUNK, GLA_HEADS, prep["G"]
    k_end, tot, vb = prep["k_end"], prep["tot"], prep["vb"]
    kv, dec_t = {}, {}
    lane = lax.broadcasted_iota(jnp.int32, (GLA_KEY_WIDTH, 2 * C), 1)
    zeros = jnp.zeros((C, GLA_DV), BF16)
    for p in range(G // 2):
        pair = slice(2 * p * C, (2 * p + 2) * C)
        ke_t = k_end[pair].T.astype(BF16)
        tot_t = tot[pair].T
        swapped = pltpu.roll(tot_t, C, 1)
        for half in range(2):
            g = 2 * p + half
            rows = slice(g * C, (g + 1) * C)
            own = (lane < C) if half == 0 else (lane >= C)
            dec_t[g] = jnp.exp(jnp.where(own, tot_t, swapped))
            parts = []
            for h in range(H):
                v_h = vb[rows, h * GLA_DV:(h + 1) * GLA_DV]
                v_pad = jnp.concatenate([v_h, zeros] if half == 0 else [zeros, v_h], axis=0)
                parts.append(_dot(ke_t[h * C:(h + 1) * C], v_pad))
            kv[g] = jnp.concatenate(parts, axis=0)
    return kv, dec_t


def _gla_states(prep, kv, dec_t, s_ref):
    st = s_ref[...]
    states = {}
    for g in prep["order"]:
        states[g] = st.astype(BF16)
        st = st * dec_t[g] + kv[g]
    s_ref[...] = st
    return states


def _gla_outputs(prep, qd_heads, scores, inter, o_ref):
    C, H = GLA_CHUNK, GLA_HEADS
    row_q = lax.broadcasted_iota(jnp.int32, (H * C, C), 0) & (C - 1)
    col_k = lax.broadcasted_iota(jnp.int32, (H * C, C), 1)
    a_mask = (col_k <= row_q) if prep["forward"] else (col_k >= row_q)
    for g in prep["order"]:
        rows = slice(g * C, (g + 1) * C)
        a = jnp.where(a_mask, scores[g], 0.0).astype(BF16)
        vv = prep["vb"][rows]
        o_ref[rows, :] = jnp.concatenate(
            [_dot(a[h * C:(h + 1) * C], vv[:, h * GLA_DV:(h + 1) * GLA_DV]) + inter[g][h * C:(h + 1) * C]
             for h in range(H)], axis=1).astype(o_ref.dtype)


def _gla_kernel(qf_ref, kf_ref, vf_ref, laf_ref, qb_ref, kb_ref, vb_ref, lab_ref,
                of_ref, ob_ref, sf_ref, sb_ref, *, G):
    @pl.when(pl.program_id(1) == 0)
    def _():
        sf_ref[...] = jnp.zeros_like(sf_ref)
        sb_ref[...] = jnp.zeros_like(sb_ref)

    dirs = [(_gla_decays(qf_ref[...], kf_ref[...], vf_ref[...], laf_ref[...], True, G), sf_ref, of_ref),
            (_gla_decays(qb_ref[...], kb_ref[...], vb_ref[...], lab_ref[...], False, G), sb_ref, ob_ref)]
    scored = [_gla_scores(prep) for prep, _, _ in dirs]
    updates = [_gla_chunk_updates(prep) for prep, _, _ in dirs]
    states = [_gla_states(prep, kv, dec_t, s_ref) for (prep, s_ref, _), (kv, dec_t) in zip(dirs, updates)]
    inters = [{g: _dot(qd_heads[g], st[g]) for g in prep["order"]}
              for (prep, _, _), (qd_heads, _), st in zip(dirs, scored, states)]
    for (prep, _, o_ref), (qd_heads, scores), inter in zip(dirs, scored, inters):
        _gla_outputs(prep, qd_heads, scores, inter, o_ref)


def _gla(gla_slab, loga, B, S, G=8):
    T = B * S
    R = G * GLA_CHUNK
    ns = S // R
    fwd = lambda col: (lambda b, i: (b * ns + i, col))
    bwd = lambda col: (lambda b, i: (b * ns + ns - 1 - i, col))
    kw, vw = GLA_KEY_WIDTH, GLA_VAL_WIDTH
    return pl.pallas_call(
        functools.partial(_gla_kernel, G=G),
        grid=(B, ns),
        in_specs=[
            pl.BlockSpec((R, kw), fwd(0)), pl.BlockSpec((R, kw), fwd(1)),
            pl.BlockSpec((R, vw), fwd(1)), pl.BlockSpec((R, kw), fwd(0)),
            pl.BlockSpec((R, kw), bwd(0)), pl.BlockSpec((R, kw), bwd(1)),
            pl.BlockSpec((R, vw), bwd(1)), pl.BlockSpec((R, kw), bwd(1)),
        ],
        out_specs=[pl.BlockSpec((R, vw), fwd(0)), pl.BlockSpec((R, vw), bwd(0))],
        out_shape=[jax.ShapeDtypeStruct((T, vw), BF16), jax.ShapeDtypeStruct((T, vw), BF16)],
        scratch_shapes=[pltpu.VMEM((kw, GLA_DV), F32), pltpu.VMEM((kw, GLA_DV), F32)],
        compiler_params=_cparams(("arbitrary", "arbitrary")),
        name="gla",
    )(gla_slab, gla_slab, gla_slab, loga, gla_slab, gla_slab, gla_slab, loga)


ATT_CLASSES = 4
ATT_QB = 128
ATT_KB = ATT_QB + 2 * ATT_RADIUS


ATT_UNROLL = 8


def _att_kernel(q_ref, k_ref, v_ref, o_ref, m_ref, l_ref, bias_ref, *, S):
    QB, KB, NC = ATT_QB, ATT_KB, ATT_CLASSES
    L4 = S // NC
    lane = lax.broadcasted_iota(jnp.int32, (QB, LANES), 1)
    head0 = lane < ATT_HEAD_DIM

    @pl.when((pl.program_id(0) == 0) & (pl.program_id(1) == 0))
    def _():
        rowi = lax.broadcasted_iota(jnp.int32, (2 * QB, KB), 0) & (QB - 1)
        coli = lax.broadcasted_iota(jnp.int32, (2 * QB, KB), 1)
        qpos = (rowi & (QB // NC - 1)) * NC + (rowi >> _log2(QB // NC))
        kpos = (coli & (KB // NC - 1)) * NC + (coli >> _log2(KB // NC))
        for case in range(3):
            bias_ref[0, case] = jnp.where(jnp.abs(rowi - coli + case * ATT_RADIUS) <= ATT_RADIUS, 0.0, NEG_INF)
            bias_ref[1, case] = jnp.where(jnp.abs(qpos - kpos + case * ATT_RADIUS) <= ATT_RADIUS, 0.0, NEG_INF)

    for pi, (_, d) in enumerate(DILATED_PATTERNS):
        L = S // d
        nb = L // QB
        shift = nb.bit_length() - 1
        first = pi == 0
        last = pi == len(DILATED_PATTERNS) - 1

        def scores(n, d=d, L=L, nb=nb, shift=shift):
            cls = n >> shift
            q0 = (n & (nb - 1)) * QB
            ws = jnp.clip(q0 - ATT_RADIUS, 0, L - KB)
            if d == 1:
                qsls = [pl.ds(pl.multiple_of(c * L4 + q0 // NC, QB // NC), QB // NC) for c in range(NC)]
                ksls = [pl.ds(pl.multiple_of(c * L4 + ws // NC, ATT_RADIUS // NC), KB // NC) for c in range(NC)]
            elif d == NC:
                qsls = [pl.ds(pl.multiple_of(cls * L4 + q0, QB), QB)]
                ksls = [pl.ds(pl.multiple_of(cls * L4 + ws, ATT_RADIUS), KB)]
            else:
                base = (cls & (NC - 1)) * L4 + (cls >> _log2(NC))
                qsls = [pl.ds(base + NC * q0, QB, stride=NC)]
                ksls = [pl.ds(base + NC * ws, KB, stride=NC)]
            q = jnp.concatenate([q_ref[sl, :] for sl in qsls], axis=0)
            kw = jnp.concatenate([k_ref[sl, :] for sl in ksls], axis=0)
            kb = kw.astype(BF16)
            bias = bias_ref[1 if d == 1 else 0, (q0 - ws) >> _log2(ATT_RADIUS), :QB]
            q_heads = (jnp.where(head0, q, 0.0), jnp.where(head0, 0.0, q))
            s = [_dot_nt(qh.astype(BF16), kb) + bias for qh in q_heads]
            return qsls, ksls, s

        def softmax_pv(qsls, ksls, s):
            vw = jnp.concatenate([v_ref[sl, :] for sl in ksls], axis=0)
            v_ones = jnp.concatenate([vw.astype(BF16), jnp.ones((KB, LANES), BF16)], axis=1)
            m_h = [jnp.max(t, axis=-1, keepdims=True) for t in s]
            pv = [_dot(jnp.exp2(t - m).astype(BF16), v_ones) for t, m in zip(s, m_h)]
            acc_b = jnp.where(head0, pv[0][:, :LANES], pv[1][:, :LANES])
            m_b = jnp.where(head0, m_h[0], m_h[1])
            l_b = jnp.where(head0, pv[0][:, LANES:], pv[1][:, LANES:])
            return qsls, acc_b, m_b, l_b

        def load(ref, sls):
            return jnp.concatenate([ref[sl, :] for sl in sls], axis=0)

        def store(ref, sls, val):
            n = val.shape[0] // len(sls)
            for i, sl in enumerate(sls):
                ref[sl, :] = val[i * n:(i + 1) * n]

        def body(n, carry, first=first, last=last):
            staged = [scores(n * ATT_UNROLL + u) for u in range(ATT_UNROLL)]
            blocks = [softmax_pv(*st) for st in staged]
            for qsls, acc_b, m_b, l_b in blocks:
                if first:
                    acc, m_new, l_new = acc_b, m_b, l_b
                else:
                    m_old = load(m_ref, qsls)
                    m_new = jnp.maximum(m_old, m_b)
                    w_old = jnp.exp2(m_old - m_new)
                    w_blk = jnp.exp2(m_b - m_new)
                    acc = load(o_ref, qsls) * w_old + acc_b * w_blk
                    l_new = load(l_ref, qsls) * w_old + l_b * w_blk
                if last:
                    store(o_ref, qsls, acc / l_new)
                else:
                    store(o_ref, qsls, acc)
                    store(m_ref, qsls, m_new)
                    store(l_ref, qsls, l_new)
            return carry

        lax.fori_loop(0, S // (QB * ATT_UNROLL), body, 0)


def _attention(att_slab, B, S):
    T = B * S
    ncol = ATT_WIDTH // LANES
    return pl.pallas_call(
        functools.partial(_att_kernel, S=S),
        grid=(B, ncol),
        in_specs=[
            pl.BlockSpec((S, LANES), lambda b, h: (b, h)),
            pl.BlockSpec((S, LANES), lambda b, h: (b, ncol + h)),
            pl.BlockSpec((S, LANES), lambda b, h: (b, 2 * ncol + h)),
        ],
        out_specs=pl.BlockSpec((S, LANES), lambda b, h: (b, h)),
        out_shape=jax.ShapeDtypeStruct((T, ATT_WIDTH), F32),
        scratch_shapes=[pltpu.VMEM((S, LANES), F32), pltpu.VMEM((S, LANES), F32),
                        pltpu.VMEM((2, 3, 2 * ATT_QB, ATT_KB), F32)],
        compiler_params=_cparams(("arbitrary", "arbitrary")),
        name="dilated_attention",
    )(att_slab, att_slab, att_slab)


PACK_WORDS = D_MODEL // 2
ROW_TILE = PACK_WORDS // LANES
HIGH_HALF = -65536


def _pack_rows(x):
    bits = lambda v: lax.bitcast_convert_type(v.astype(BF16).astype(F32), jnp.int32)
    low = (bits(x[:, :PACK_WORDS]) >> 16) & 0xFFFF
    return (bits(x[:, PACK_WORDS:]) & HIGH_HALF) | low


def _unpack_rows(w):
    low = lax.bitcast_convert_type(w << 16, F32)
    high = lax.bitcast_convert_type(w & HIGH_HALF, F32)
    return jnp.concatenate([low, high], axis=1).astype(BF16)


def _to_row_tiles(ref, w):
    n = w.shape[0]
    for j in range(ROW_TILE):
        ref[pl.ds(j, n, stride=ROW_TILE), :] = w[:, j * LANES:(j + 1) * LANES]


def _from_row_tiles(ref, n):
    return jnp.concatenate([ref[pl.ds(j, n, stride=ROW_TILE), :] for j in range(ROW_TILE)], axis=1)


def _tile_copy(src_ref, src_row, dst_ref, dst_row, sem):
    src = pl.ds(pl.multiple_of(src_row * ROW_TILE, ROW_TILE), ROW_TILE)
    dst = pl.ds(pl.multiple_of(dst_row * ROW_TILE, ROW_TILE), ROW_TILE)
    return pltpu.make_async_copy(src_ref.at[src], dst_ref.at[dst], sem)


def _outproj_kernel(of_ref, ob_ref, gg_ref, att_ref, x_ref, gnw_ref, wo1_ref, wo2_ref,
                    n2_ref, wr_ref, br_ref, h_ref, u_ref, lg_ref, stage_ref):
    rows = stage_ref.shape[1] // ATT_CLASSES
    for j in range(ATT_WIDTH // LANES):
        for c in range(ATT_CLASSES):
            stage_ref[j, pl.ds(c, rows, stride=ATT_CLASSES), :] = att_ref[c, :, j * LANES:(j + 1) * LANES]
    att = jnp.concatenate([stage_ref[j] for j in range(ATT_WIDTH // LANES)], axis=1)
    o = of_ref[...].astype(F32) + ob_ref[...].astype(F32)
    gate = gg_ref[...].astype(F32)
    gnw = gnw_ref[...]
    parts = []
    for h in range(GLA_HEADS):
        sl = slice(h * GLA_DV, (h + 1) * GLA_DV)
        parts.append(_rms(o[:, sl], gnw))
    y = jnp.concatenate(parts, axis=1) * (gate / (1.0 + jnp.exp(-gate)))
    mix = _dot(y.astype(BF16), wo1_ref[...]) + _dot(att.astype(BF16), wo2_ref[...])
    h = x_ref[...] + mix
    h_ref[...] = h
    u = _rms(h, n2_ref[...])
    _to_row_tiles(u_ref, _pack_rows(u))
    u_hi = u.astype(BF16)
    u_lo = (u - u_hi.astype(F32)).astype(BF16)
    hi_both = _dot_nt(wr_ref[...], u_hi)
    lg_ref[...] = (hi_both[:LANES] + hi_both[LANES:] + _dot_nt(wr_ref[:LANES], u_lo)) + br_ref[...]


def _outproj(o_f, o_b, gate, att_out, x2, gla_norm_w, w_out, norm2_w, wr, br, tm=512):
    T = x2.shape[0]
    nS = att_out.shape[2] * ATT_CLASSES // tm
    row = lambda i: (i, 0)
    const = lambda i: (0, 0)
    wo = w_out.astype(BF16)
    wr_hi = wr.astype(BF16)
    wr_lo = (wr - wr_hi.astype(F32)).astype(BF16)
    wr = jnp.concatenate([wr_hi, wr_lo], axis=0)
    return pl.pallas_call(
        _outproj_kernel,
        grid=(T // tm,),
        in_specs=[
            pl.BlockSpec((tm, GLA_VAL_WIDTH), row),
            pl.BlockSpec((tm, GLA_VAL_WIDTH), row),
            pl.BlockSpec((tm, GLA_VAL_WIDTH), row),
            pl.BlockSpec((None, ATT_CLASSES, tm // ATT_CLASSES, ATT_WIDTH), lambda i: (i // nS, 0, i % nS, 0)),
            pl.BlockSpec((tm, D_MODEL), row),
            pl.BlockSpec((1, GLA_DV), const),
            pl.BlockSpec((GLA_VAL_WIDTH, D_MODEL), const),
            pl.BlockSpec((ATT_WIDTH, D_MODEL), lambda i: (GLA_VAL_WIDTH // ATT_WIDTH, 0)),
            pl.BlockSpec((1, D_MODEL), const),
            pl.BlockSpec((2 * LANES, D_MODEL), const),
            pl.BlockSpec((LANES, 1), const),
        ],
        out_specs=[
            pl.BlockSpec((tm, D_MODEL), row),
            pl.BlockSpec((tm * ROW_TILE, LANES), row),
            pl.BlockSpec((LANES, tm), lambda i: (0, i)),
        ],
        out_shape=[
            jax.ShapeDtypeStruct((T, D_MODEL), F32),
            jax.ShapeDtypeStruct((T * ROW_TILE, LANES), jnp.int32),
            jax.ShapeDtypeStruct((LANES, T), F32),
        ],
        scratch_shapes=[pltpu.VMEM((ATT_WIDTH // LANES, tm, LANES), F32)],
        compiler_params=_cparams(("arbitrary",)),
        name="outproj",
    )(o_f, o_b, gate, att_out, x2, gla_norm_w[None, :], wo, wo,
      norm2_w[None, :], wr, br)


INFO_E1, INFO_E2, INFO_R1, INFO_R2, INFO_W1, INFO_W2 = range(6)
ROUTE_ROWS = 40


def _route_kernel(lg_ref, info_ref, cnt_ref, carry_ref):
    @pl.when(pl.program_id(0) == 0)
    def _():
        carry_ref[...] = jnp.zeros_like(carry_ref)

    lg = lg_ref[:ROUTE_ROWS, :]
    tr = lg.shape[1]
    row = lax.broadcasted_iota(jnp.int32, (ROUTE_ROWS, tr), 0)
    big = jnp.int32(1 << 20)
    is_g = (row >= MOE_N_EXPERTS) & (row < MOE_N_EXPERTS + MOE_GROUPS)
    gl = jnp.where(is_g, lg, -jnp.inf)
    gmax = jnp.max(gl, axis=0, keepdims=True)
    gsel = jnp.min(jnp.where(gl == gmax, row - MOE_N_EXPERTS, big), axis=0, keepdims=True)
    g_w = 1.0 / jnp.sum(jnp.where(is_g, jnp.exp(lg - gmax), 0.0), axis=0, keepdims=True)
    in_grp = (row < MOE_N_EXPERTS) & ((row >> MOE_GROUP_SHIFT) == gsel)
    el = jnp.where(in_grp, lg, -jnp.inf)
    v1 = jnp.max(el, axis=0, keepdims=True)
    i1 = jnp.min(jnp.where(el == v1, row, big), axis=0, keepdims=True)
    el2 = jnp.where(row == i1, -jnp.inf, el)
    v2 = jnp.max(el2, axis=0, keepdims=True)
    i2 = jnp.min(jnp.where(el2 == v2, row, big), axis=0, keepdims=True)
    t = jnp.exp(v2 - v1)
    w1 = g_w * (1.0 / (1.0 + t))
    w2 = g_w * (t / (1.0 + t))

    erow = lax.broadcasted_iota(jnp.int32, (MOE_N_EXPERTS, tr), 0)
    hit1 = erow == i1
    hit2 = erow == i2
    member = jnp.where(hit1 | hit2, 1.0, 0.0)
    r = lax.broadcasted_iota(jnp.int32, (tr, tr), 0)
    c = lax.broadcasted_iota(jnp.int32, (tr, tr), 1)
    earlier = jnp.where(r < c, 1.0, 0.0).astype(BF16)
    carry = carry_ref[...]
    prefix = _dot(member.astype(BF16), earlier) + carry[:, 0:1]
    rank1 = jnp.sum(jnp.where(hit1, prefix, 0.0), axis=0, keepdims=True)
    rank2 = jnp.sum(jnp.where(hit2, prefix, 0.0), axis=0, keepdims=True)
    carry = carry + jnp.sum(member, axis=1, keepdims=True)
    carry_ref[...] = carry
    cnt_ref[...] = carry

    zero = jnp.zeros_like(w1)
    info_ref[...] = jnp.concatenate([i1.astype(F32), i2.astype(F32), rank1, rank2, w1, w2, zero, zero], axis=0)


def _route(logits_t, tr=1024):
    T = logits_t.shape[1]
    return pl.pallas_call(
        _route_kernel,
        grid=(T // tr,),
        in_specs=[pl.BlockSpec((LANES, tr), lambda i: (0, i))],
        out_specs=[pl.BlockSpec((8, tr), lambda i: (0, i)),
                   pl.BlockSpec((MOE_N_EXPERTS, LANES), lambda i: (0, 0))],
        out_shape=[jax.ShapeDtypeStruct((8, T), F32), jax.ShapeDtypeStruct((MOE_N_EXPERTS, LANES), F32)],
        scratch_shapes=[pltpu.VMEM((MOE_N_EXPERTS, LANES), F32)],
        compiler_params=_cparams(("arbitrary",)),
        name="route",
    )(logits_t)


ROW_UNROLL = 16


def _dispatch_kernel(dest_ref, pend_ref, u_ref, xs_ref, zbuf, sem, zsem, *, td, T, nblk):
    @pl.when(pl.program_id(0) == 0)
    def _():
        zbuf[...] = jnp.zeros_like(zbuf)
        n_used = pend_ref[MOE_N_EXPERTS - 1] >> MOE_ROWS_SHIFT

        def zero_copy(blk):
            start = pl.multiple_of(blk * (MOE_ROWS * ROW_TILE), MOE_ROWS * ROW_TILE)
            return pltpu.make_async_copy(zbuf, xs_ref.at[pl.ds(start, MOE_ROWS * ROW_TILE)], zsem)

        def each_pad_block(fn):
            def per_expert(e, carry):
                prev = jnp.where(e > 0, pend_ref[jnp.maximum(e - 1, 0)], 0)

                @pl.when(pend_ref[e] > prev)
                def _():
                    fn((pend_ref[e] >> MOE_ROWS_SHIFT) - 1)
                return carry

            def per_tail(j, carry):
                @pl.when(n_used + j < nblk)
                def _():
                    fn(n_used + j)
                return carry

            lax.fori_loop(0, MOE_N_EXPERTS, per_expert, 0)
            lax.fori_loop(0, MOE_N_EXPERTS, per_tail, 0)

        each_pad_block(lambda blk: zero_copy(blk).start())
        each_pad_block(lambda blk: zero_copy(blk).wait())

    base = pl.program_id(0) * td

    def issue(g, carry):
        for j in range(ROW_UNROLL):
            r = g * ROW_UNROLL + j
            for k in range(MOE_TOP_K):
                _tile_copy(u_ref, r, xs_ref, dest_ref[k * T + base + r], sem).start(priority=k)
        return carry

    lax.fori_loop(0, td // ROW_UNROLL, issue, 0)
    for k in range(MOE_TOP_K):
        pltpu.make_async_copy(u_ref, xs_ref.at[pl.ds(0, td * ROW_TILE)], sem).wait()


def _dispatch(dest, pend, u2, cap, td=2048):
    T = u2.shape[0] // ROW_TILE
    return pl.pallas_call(
        functools.partial(_dispatch_kernel, td=td, T=T, nblk=cap // MOE_ROWS),
        grid_spec=pltpu.PrefetchScalarGridSpec(
            num_scalar_prefetch=2,
            grid=(T // td,),
            in_specs=[pl.BlockSpec((td * ROW_TILE, LANES), lambda i, d, z: (i, 0))],
            out_specs=pl.BlockSpec(memory_space=pl.ANY),
            scratch_shapes=[pltpu.VMEM((MOE_ROWS * ROW_TILE, LANES), jnp.int32),
                            pltpu.SemaphoreType.DMA(()), pltpu.SemaphoreType.DMA(())],
        ),
        out_shape=jax.ShapeDtypeStruct((cap * ROW_TILE, LANES), jnp.int32),
        compiler_params=_cparams(("arbitrary",)),
        name="dispatch",
    )(dest, pend, u2)


def _expert_kernel(pend_ref, xs_hbm, wg_hbm, wu_hbm, wd_hbm, ys_hbm,
                   xbuf, ybuf, zbuf, stage_g, stage_u, stage_d, wgb, wub, wdb, xsem, ysem, wsem, zsem, *, nblk):
    last = MOE_N_EXPERTS - 1
    n_used = pend_ref[last] >> MOE_ROWS_SHIFT
    block_rows = MOE_ROWS * ROW_TILE

    def x_copy(b, slot):
        start = pl.multiple_of(b * block_rows, block_rows)
        return pltpu.make_async_copy(xs_hbm.at[pl.ds(start, block_rows)], xbuf.at[slot], xsem.at[slot])

    def y_copy(b, slot):
        start = pl.multiple_of(b * block_rows, block_rows)
        return pltpu.make_async_copy(ybuf.at[slot], ys_hbm.at[pl.ds(start, block_rows)], ysem.at[slot])

    def zero_copy(b):
        start = pl.multiple_of(b * block_rows, block_rows)
        return pltpu.make_async_copy(zbuf, ys_hbm.at[pl.ds(start, block_rows)], zsem)

    def weight_copies(e):
        return (pltpu.make_async_copy(wg_hbm.at[e], stage_g, wsem.at[0]),
                pltpu.make_async_copy(wu_hbm.at[e], stage_u, wsem.at[1]),
                pltpu.make_async_copy(wd_hbm.at[e], stage_d, wsem.at[2]))

    def owner(start, row):
        return lax.while_loop(lambda e: (e < last) & (pend_ref[e] <= row), lambda e: e + 1, start)

    for c in weight_copies(owner(0, 0)):
        c.start()
    x_copy(0, 0).start()

    zbuf[...] = jnp.zeros_like(zbuf)

    def tail(fn):
        def step(b, carry):
            fn(b)
            return carry
        lax.fori_loop(n_used, nblk, step, 0)

    tail(lambda b: zero_copy(b).start())

    def body(b, cur):
        slot = b & 1
        e = owner(jnp.maximum(cur, 0), b * MOE_ROWS)
        x_copy(b, slot).wait()

        @pl.when(b + 1 < n_used)
        def _():
            x_copy(b + 1, 1 - slot).start()

        @pl.when(e != cur)
        def _():
            for c in weight_copies(e):
                c.wait()
            wgb[...] = stage_g[...].astype(BF16)
            wub[...] = stage_u[...].astype(BF16)
            wdb[...] = stage_d[...].astype(BF16)

            @pl.when(pend_ref[e] < pend_ref[last])
            def _():
                for c in weight_copies(owner(e + 1, pend_ref[e])):
                    c.start(priority=1)

        @pl.when(b >= 2)
        def _():
            y_copy(b - 2, slot).wait()

        xb = _unpack_rows(_from_row_tiles(xbuf.at[slot], MOE_ROWS))
        g = _dot(xb, wgb[...])
        u = _dot(xb, wub[...])
        hid = (g / (1.0 + jnp.exp(-g))) * u
        _to_row_tiles(ybuf.at[slot], _pack_rows(_dot(hid.astype(BF16), wdb[...])))
        y_copy(b, slot).start()
        return e

    lax.fori_loop(0, n_used, body, jnp.int32(-1))

    @pl.when(n_used >= 2)
    def _():
        y_copy(n_used - 2, n_used & 1).wait()
    y_copy(n_used - 1, (n_used - 1) & 1).wait()
    tail(lambda b: zero_copy(b).wait())


def _experts(pend, xs, w_gate, w_up, w_down):
    cap = xs.shape[0] // ROW_TILE
    nblk = cap // MOE_ROWS
    block = (MOE_ROWS * ROW_TILE, LANES)
    anywhere = pl.BlockSpec(memory_space=pl.ANY)
    return pl.pallas_call(
        functools.partial(_expert_kernel, nblk=nblk),
        grid_spec=pltpu.PrefetchScalarGridSpec(
            num_scalar_prefetch=1,
            grid=(1,),
            in_specs=[anywhere, anywhere, anywhere, anywhere],
            out_specs=anywhere,
            scratch_shapes=[pltpu.VMEM((2,) + block, jnp.int32),
                            pltpu.VMEM((2,) + block, jnp.int32),
                            pltpu.VMEM(block, jnp.int32),
                            pltpu.VMEM((D_MODEL, MOE_D_FF), F32),
                            pltpu.VMEM((D_MODEL, MOE_D_FF), F32),
                            pltpu.VMEM((MOE_D_FF, D_MODEL), F32),
                            pltpu.VMEM((D_MODEL, MOE_D_FF), BF16),
                            pltpu.VMEM((D_MODEL, MOE_D_FF), BF16),
                            pltpu.VMEM((MOE_D_FF, D_MODEL), BF16),
                            pltpu.SemaphoreType.DMA((2,)),
                            pltpu.SemaphoreType.DMA((2,)),
                            pltpu.SemaphoreType.DMA((3,)),
                            pltpu.SemaphoreType.DMA(())],
        ),
        out_shape=jax.ShapeDtypeStruct((cap * ROW_TILE, LANES), jnp.int32),
        compiler_params=_cparams(("arbitrary",)),
        name="experts",
    )(pend, xs, w_gate, w_up, w_down)


def _combine_kernel(dest_ref, ys_ref, info_ref, h_ref, fw_ref, o_ref, buf, sem, *, tc, T):
    i = pl.program_id(0)
    n = pl.num_programs(0)

    def issue(step, slot):
        base = step * tc

        def body(g, carry):
            for j in range(ROW_UNROLL):
                r = g * ROW_UNROLL + j
                for k in range(MOE_TOP_K):
                    _tile_copy(ys_ref, dest_ref[k * T + base + r], buf.at[slot, k], r,
                               sem.at[slot]).start(priority=k)
            return carry

        lax.fori_loop(0, tc // ROW_UNROLL, body, 0)

    @pl.when(i == 0)
    def _():
        issue(0, 0)

    slot = i % 2

    @pl.when(i + 1 < n)
    def _():
        issue(i + 1, 1 - slot)

    for k in range(MOE_TOP_K):
        pltpu.make_async_copy(ys_ref.at[pl.ds(0, tc * ROW_TILE)], buf.at[slot, k], sem.at[slot]).wait()

    info_t = jnp.concatenate([info_ref[...]] * (LANES // 8), axis=0).T
    w1 = info_t[:, INFO_W1:INFO_W1 + 1]
    w2 = info_t[:, INFO_W2:INFO_W2 + 1]
    y1 = _unpack_rows(_from_row_tiles(buf.at[slot, 0], tc)).astype(F32)
    y2 = _unpack_rows(_from_row_tiles(buf.at[slot, 1], tc)).astype(F32)
    h = h_ref[...] + (y1 * w1 + y2 * w2)
    o_ref[...] = _rms(h, fw_ref[...])


def _combine(dest, ys, info, h, final_w, tc=512):
    T = h.shape[0]
    return pl.pallas_call(
        functools.partial(_combine_kernel, tc=tc, T=T),
        grid_spec=pltpu.PrefetchScalarGridSpec(
            num_scalar_prefetch=1,
            grid=(T // tc,),
            in_specs=[pl.BlockSpec(memory_space=pl.ANY),
                      pl.BlockSpec((8, tc), lambda i, d: (0, i)),
                      pl.BlockSpec((tc, D_MODEL), lambda i, d: (i, 0)),
                      pl.BlockSpec((1, D_MODEL), lambda i, d: (0, 0))],
            out_specs=pl.BlockSpec((tc, D_MODEL), lambda i, d: (i, 0)),
            scratch_shapes=[pltpu.VMEM((2, MOE_TOP_K, tc * ROW_TILE, LANES), jnp.int32),
                            pltpu.SemaphoreType.DMA((2,))],
        ),
        out_shape=jax.ShapeDtypeStruct((T, D_MODEL), F32),
        compiler_params=_cparams(("arbitrary",)),
        name="combine",
    )(dest, ys, info, h, final_w[None, :])


def _plan_kernel(info_ref, cnt_ref, dest_ref, pend_ref):
    cnt = cnt_ref[...].astype(jnp.int32)
    nblk_e = ((cnt + (MOE_ROWS - 1)) >> MOE_ROWS_SHIFT).astype(F32)
    r = lax.broadcasted_iota(jnp.int32, (MOE_N_EXPERTS, MOE_N_EXPERTS), 0)
    c = lax.broadcasted_iota(jnp.int32, (MOE_N_EXPERTS, MOE_N_EXPERTS), 1)
    before = jnp.where(c < r, 1.0, 0.0).astype(BF16)
    first_blk = _dot(before, nblk_e.astype(BF16))
    pstart = first_blk[:, 0:1] * float(MOE_ROWS)
    pend_ref[...] = ((first_blk + nblk_e) * float(MOE_ROWS)).astype(jnp.int32)

    info = info_ref[...]
    erow = lax.broadcasted_iota(jnp.int32, (MOE_N_EXPERTS, info.shape[1]), 0)
    start_of = lambda e: jnp.sum(jnp.where(erow == e.astype(jnp.int32), pstart, 0.0), axis=0, keepdims=True)
    d1 = info[INFO_R1:INFO_R1 + 1] + start_of(info[INFO_E1:INFO_E1 + 1])
    d2 = info[INFO_R2:INFO_R2 + 1] + start_of(info[INFO_E2:INFO_E2 + 1])
    zero = jnp.zeros_like(d1)
    dest_ref[...] = jnp.concatenate([d1, d2] + [zero] * 6, axis=0).astype(jnp.int32)


def _plan(info, counts, tr=2048):
    T = info.shape[1]
    dest8, pend = pl.pallas_call(
        _plan_kernel,
        grid=(T // tr,),
        in_specs=[pl.BlockSpec((8, tr), lambda i: (0, i)),
                  pl.BlockSpec((MOE_N_EXPERTS, LANES), lambda i: (0, 0))],
        out_specs=[pl.BlockSpec((8, tr), lambda i: (0, i)),
                   pl.BlockSpec((MOE_N_EXPERTS, LANES), lambda i: (0, 0))],
        out_shape=[jax.ShapeDtypeStruct((8, T), jnp.int32),
                   jax.ShapeDtypeStruct((MOE_N_EXPERTS, LANES), jnp.int32)],
        compiler_params=_cparams(("arbitrary",)),
        name="plan",
    )(info, counts)
    return dest8[:MOE_TOP_K].reshape(-1), pend[:, 0]


def _moe_capacity(T):
    return (-(-(T * MOE_TOP_K) // MOE_ROWS) + MOE_N_EXPERTS) * MOE_ROWS


def _router_weights(router_group_w, router_group_b, router_expert_w, router_expert_b):
    we = jnp.transpose(router_expert_w, (0, 2, 1)).reshape(MOE_N_EXPERTS, D_MODEL)
    pad = LANES - MOE_N_EXPERTS - MOE_GROUPS
    wr = jnp.concatenate([we, router_group_w.T, jnp.zeros((pad, D_MODEL), F32)], axis=0)
    br = jnp.concatenate([router_expert_b.reshape(-1), router_group_b, jnp.zeros((pad,), F32)])[:, None]
    return wr, br


def kernel(x, norm1_w, w_in, gla_fwd_gate_w, gla_fwd_gate_b, gla_bwd_gate_w, gla_bwd_gate_b,
           gla_norm_w, w_out, norm2_w, router_group_w, router_group_b, router_expert_w,
           router_expert_b, expert_w_gate, expert_w_up, expert_w_down, final_norm_w):
    B, S, D = x.shape
    T = B * S
    assert norm1_w.shape[0] == 1, "single-layer trunk: the final norm is fused into the combine step"
    h = x.reshape(T, D)
    gla_slab, gate, loga, att_slab = _inproj(h, S, norm1_w[0], w_in[0], gla_fwd_gate_w[0], gla_fwd_gate_b[0],
                                       gla_bwd_gate_w[0], gla_bwd_gate_b[0])
    o_f, o_b = _gla(gla_slab, loga, B, S)
    att_out = _attention(att_slab.reshape(T, 3 * ATT_WIDTH), B, S)
    att_out = att_out.reshape(B, ATT_CLASSES, S // ATT_CLASSES, ATT_WIDTH)
    wr, br = _router_weights(router_group_w[0], router_group_b[0], router_expert_w[0], router_expert_b[0])
    h, u2, logits = _outproj(o_f, o_b, gate, att_out, h, gla_norm_w[0], w_out[0], norm2_w[0], wr, br)
    info, counts = _route(logits)
    dest, pend = _plan(info, counts)
    xs = _dispatch(dest, pend, u2, _moe_capacity(T))
    ys = _experts(pend, xs, expert_w_gate[0], expert_w_up[0], expert_w_down[0])
    out = _combine(dest, ys, info, h, final_norm_w)
    return out.reshape(B, S, D)
```

```python
import functools

import jax
import jax.numpy as jnp
import numpy as np
from jax import lax
from jax.experimental import pallas as pl
from jax.experimental.pallas import tpu as pltpu

F32 = jnp.float32
BF16 = jnp.bfloat16

D_MODEL = 1024
GLA_HEADS = 4
GLA_DV = 128
GLA_DK = 64
GLA_KEY_WIDTH = GLA_HEADS * GLA_DK
GLA_VAL_WIDTH = GLA_HEADS * GLA_DV
GLA_GATE_RANK = 16
GLA_TAU = 16.0
GLA_CHUNK = 64
ATT_WIDTH = 512
ATT_HEAD_DIM = 64
ATT_HEADS = 8
ROT_DIM = 16
ROPE_THETA = 500000.0
DILATED_PATTERNS = ((128, 1), (512, 4), (2048, 16))
ATT_RADIUS = 64
MOE_GROUPS = 4
MOE_EXPERTS_PER_GROUP = 8
MOE_N_EXPERTS = 32
MOE_TOP_K = 2
MOE_D_FF = 512
EPS = 1e-6
NEG_INF = -1e30
LOG2E = 1.4426950408889634

LANES = 128
MOE_ROWS = 256


def _log2(n):
    assert n & (n - 1) == 0, n
    return n.bit_length() - 1


GLA_CHUNK_SHIFT = _log2(GLA_CHUNK)
GLA_DK_SHIFT = _log2(GLA_DK)
MOE_ROWS_SHIFT = _log2(MOE_ROWS)
MOE_GROUP_SHIFT = _log2(MOE_EXPERTS_PER_GROUP)
VMEM_LIMIT = 56 * 1024 * 1024


def _cparams(sem):
    return pltpu.CompilerParams(dimension_semantics=sem, vmem_limit_bytes=VMEM_LIMIT)


def _dot(a, b):
    return jnp.dot(a, b, preferred_element_type=F32)


def _dot_nt(a, b):
    return lax.dot_general(a, b, (((1,), (1,)), ((), ())), preferred_element_type=F32)


def _dot_tn(a, b):
    return lax.dot_general(a, b, (((0,), (0,)), ((), ())), preferred_element_type=F32)


def _rms(x, w):
    return x * lax.rsqrt(jnp.mean(x * x, axis=-1, keepdims=True) + EPS) * w


def _inproj_kernel(x_ref, n1_ref, wg_ref, wlr_ref, wa_ref, gw_ref, gb_ref,
                   rc_ref, rs1_ref, rs2_ref, gla_ref, gate_ref, loga_ref, att_ref, stage_ref, wgb, wlrb):
    @pl.when(pl.program_id(0) == 0)
    def _():
        wgb[...] = wg_ref[...].astype(BF16)
        wlrb[...] = wlr_ref[...].astype(BF16)

    x = x_ref[...]
    ub = _rms(x, n1_ref[...]).astype(BF16)
    g = _dot(ub, wgb[...])
    qkv = 2 * GLA_KEY_WIDTH + GLA_VAL_WIDTH
    gla_ref[:, :GLA_KEY_WIDTH] = g[:, :GLA_KEY_WIDTH] * (GLA_DK ** -0.5)
    gla_ref[:, GLA_KEY_WIDTH:] = g[:, GLA_KEY_WIDTH:qkv]
    gate_ref[...] = g[:, qkv:].astype(BF16)
    lr = _dot(ub, wlrb[...])
    gate = _dot(lr.astype(BF16), gw_ref[...]) + gb_ref[...]
    loga_ref[...] = (jnp.minimum(gate, 0.0) - jnp.log(1.0 + jnp.exp(-jnp.abs(gate)))) * (1.0 / GLA_TAU)
    a = _dot(ub, wa_ref[...])
    qk = a[:, :2 * ATT_WIDTH]
    reps = 2 * ATT_WIDTH // LANES
    c = jnp.concatenate([rc_ref[...]] * reps, axis=1)
    s1 = jnp.concatenate([rs1_ref[...]] * reps, axis=1)
    s2 = jnp.concatenate([rs2_ref[...]] * reps, axis=1)
    half = ROT_DIM // 2
    n = 2 * ATT_WIDTH
    roped = qk * c + pltpu.roll(qk, n - half, 1) * s1 + pltpu.roll(qk, half, 1) * s2
    qkv = jnp.concatenate([roped[:, :ATT_WIDTH] * (ATT_HEAD_DIM ** -0.5 * LOG2E), roped[:, ATT_WIDTH:],
                           a[:, 2 * ATT_WIDTH:]], axis=1)
    rows = x.shape[0] // ATT_CLASSES
    for j in range(3 * ATT_WIDTH // LANES):
        cols = slice(j * LANES, (j + 1) * LANES)
        stage_ref[j] = qkv[:, cols]
        for c in range(ATT_CLASSES):
            att_ref[c, :, cols] = stage_ref[j, pl.ds(c, rows, stride=ATT_CLASSES), :]


def _rope_lane_tables(S):
    half = ROT_DIM // 2
    inv = np.float32(ROPE_THETA) ** (-(np.arange(0, ROT_DIM, 2, dtype=np.float32) / np.float32(ROT_DIM)))
    ang = np.arange(S, dtype=np.float32)[:, None] * inv[None, :].astype(np.float32)
    cos, sin = np.cos(ang), np.sin(ang)
    ones = np.ones((S, ATT_HEAD_DIM - ROT_DIM), np.float32)
    zeros = np.zeros((S, ATT_HEAD_DIM - ROT_DIM), np.float32)
    zeros8 = np.zeros((S, half), np.float32)
    rep = LANES // ATT_HEAD_DIM
    c = np.tile(np.concatenate([cos, cos, ones], axis=1), (1, rep))
    s1 = np.tile(np.concatenate([-sin, zeros8, zeros], axis=1), (1, rep))
    s2 = np.tile(np.concatenate([zeros8, sin, zeros], axis=1), (1, rep))
    return jnp.asarray(c), jnp.asarray(s1), jnp.asarray(s2)


def _inproj(x2, S, norm1_w, w_in, wf, bfw, wb, bbw, tm=512):
    T = x2.shape[0]
    o_lr = 2 * GLA_KEY_WIDTH + 2 * GLA_VAL_WIDTH
    o_att = o_lr + 2 * GLA_GATE_RANK
    wa = w_in[:, o_att:].astype(BF16)
    zeros = jnp.zeros((GLA_GATE_RANK, GLA_KEY_WIDTH), F32)
    gw = jnp.concatenate([jnp.concatenate([wf, zeros], axis=1), jnp.concatenate([zeros, wb], axis=1),
                          jnp.zeros((LANES - 2 * GLA_GATE_RANK, 2 * GLA_KEY_WIDTH), F32)], axis=0).astype(BF16)
    gb = jnp.concatenate([bfw, bbw])[None, :]
    rc, rs1, rs2 = _rope_lane_tables(S)
    nS = S // tm
    row = lambda i: (i, 0)
    const = lambda i: (0, 0)
    pos = lambda i: (i % nS, 0)
    return pl.pallas_call(
        _inproj_kernel,
        grid=(T // tm,),
        in_specs=[
            pl.BlockSpec((tm, D_MODEL), row),
            pl.BlockSpec((1, D_MODEL), const),
            pl.BlockSpec((D_MODEL, o_lr), const),
            pl.BlockSpec((D_MODEL, LANES), lambda i: (0, o_lr // LANES)),
            pl.BlockSpec((D_MODEL, 3 * ATT_WIDTH), const),
            pl.BlockSpec((LANES, 2 * GLA_KEY_WIDTH), const),
            pl.BlockSpec((1, 2 * GLA_KEY_WIDTH), const),
            pl.BlockSpec((tm, LANES), pos),
            pl.BlockSpec((tm, LANES), pos),
            pl.BlockSpec((tm, LANES), pos),
        ],
        out_specs=[
            pl.BlockSpec((tm, o_lr - GLA_VAL_WIDTH), row),
            pl.BlockSpec((tm, GLA_VAL_WIDTH), row),
            pl.BlockSpec((tm, 2 * GLA_KEY_WIDTH), row),
            pl.BlockSpec((None, ATT_CLASSES, tm // ATT_CLASSES, 3 * ATT_WIDTH),
                         lambda i: (i // nS, 0, i % nS, 0)),
        ],
        out_shape=[
            jax.ShapeDtypeStruct((T, o_lr - GLA_VAL_WIDTH), F32),
            jax.ShapeDtypeStruct((T, GLA_VAL_WIDTH), BF16),
            jax.ShapeDtypeStruct((T, 2 * GLA_KEY_WIDTH), F32),
            jax.ShapeDtypeStruct((T // S, ATT_CLASSES, S // ATT_CLASSES, 3 * ATT_WIDTH), F32),
        ],
        scratch_shapes=[pltpu.VMEM((3 * ATT_WIDTH // LANES, tm, LANES), F32),
                        pltpu.VMEM((D_MODEL, o_lr), BF16), pltpu.VMEM((D_MODEL, LANES), BF16)],
        compiler_params=_cparams(("arbitrary",)),
        name="inproj",
    )(x2, norm1_w[None, :], w_in, w_in, wa, gw, gb, rc, rs1, rs2)


def _gla_decays(q, k, v, la, forward, G):
    C = GLA_CHUNK
    R = G * C
    r = lax.broadcasted_iota(jnp.int32, (R, R), 0)
    c = lax.broadcasted_iota(jnp.int32, (R, R), 1)
    same = (r >> GLA_CHUNK_SHIFT) == (c >> GLA_CHUNK_SHIFT)
    tri = (c <= r) if forward else (c >= r)
    t_mat = jnp.where(same, jnp.where(tri, 1.0, 0.0), 0.0).astype(BF16)
    hi = la.astype(BF16)
    lo = (la - hi.astype(F32)).astype(BF16)
    b = _dot(t_mat, hi) + _dot(t_mat, lo)
    edge = C - 1 if forward else 0
    tot = jnp.concatenate([jnp.broadcast_to(b[g * C + edge:g * C + edge + 1], (C, GLA_KEY_WIDTH))
                           for g in range(G)], axis=0)
    order = list(range(G)) if forward else list(range(G - 1, -1, -1))
    return dict(q_dec=q * jnp.exp(b), k_inv=(k * jnp.exp(-b)).astype(BF16), k_end=k * jnp.exp(tot - b),
                tot=tot, vb=v.astype(BF16), order=order, forward=forward, G=G)


def _gla_scores(prep):
    C, H = GLA_CHUNK, GLA_HEADS
    lane_k = lax.broadcasted_iota(jnp.int32, (C, GLA_KEY_WIDTH), 1)
    qd_heads, scores = {}, {}
    for g in prep["order"]:
        rows = slice(g * C, (g + 1) * C)
        qd = prep["q_dec"][rows]
        qd_heads[g] = jnp.concatenate([jnp.where((lane_k >> GLA_DK_SHIFT) == h, qd, 0.0) for h in range(H)],
                                      axis=0).astype(BF16)
        scores[g] = _dot_nt(qd_heads[g], prep["k_inv"][rows])
    return qd_heads, scores


def _gla_chunk_updates(prep):
    C, H, G = GLA_CHUNK, GLA_HEADS, prep["G"]
    k_end, tot, vb = prep["k_end"], prep["tot"], prep["vb"]
    kv, dec_t = {}, {}
    lane = lax.broadcasted_iota(jnp.int32, (GLA_KEY_WIDTH, 2 * C), 1)
    zeros = jnp.zeros((C, GLA_DV), BF16)
    for p in range(G // 2):
        pair = slice(2 * p * C, (2 * p + 2) * C)
        ke_t = k_end[pair].T.astype(BF16)
        tot_t = tot[pair].T
        swapped = pltpu.roll(tot_t, C, 1)
        for half in range(2):
            g = 2 * p + half
            rows = slice(g * C, (g + 1) * C)
            own = (lane < C) if half == 0 else (lane >= C)
            dec_t[g] = jnp.exp(jnp.where(own, tot_t, swapped))
            parts = []
            for h in range(H):
                v_h = vb[rows, h * GLA_DV:(h + 1) * GLA_DV]
                v_pad = jnp.concatenate([v_h, zeros] if half == 0 else [zeros, v_h], axis=0)
                parts.append(_dot(ke_t[h * C:(h + 1) * C], v_pad))
            kv[g] = jnp.concatenate(parts, axis=0)
    return kv, dec_t


def _gla_states(prep, kv, dec_t, s_ref):
    st = s_ref[...]
    states = {}
    for g in prep["order"]:
        states[g] = st.astype(BF16)
        st = st * dec_t[g] + kv[g]
    s_ref[...] = st
    return states


def _gla_outputs(prep, qd_heads, scores, inter, o_ref):
    C, H = GLA_CHUNK, GLA_HEADS
    row_q = lax.broadcasted_iota(jnp.int32, (H * C, C), 0) & (C - 1)
    col_k = lax.broadcasted_iota(jnp.int32, (H * C, C), 1)
    a_mask = (col_k <= row_q) if prep["forward"] else (col_k >= row_q)
    for g in prep["order"]:
        rows = slice(g * C, (g + 1) * C)
        a = jnp.where(a_mask, scores[g], 0.0).astype(BF16)
        vv = prep["vb"][rows]
        o_ref[rows, :] = jnp.concatenate(
            [_dot(a[h * C:(h + 1) * C], vv[:, h * GLA_DV:(h + 1) * GLA_DV]) + inter[g][h * C:(h + 1) * C]
             for h in range(H)], axis=1).astype(o_ref.dtype)


def _gla_kernel(qf_ref, kf_ref, vf_ref, laf_ref, qb_ref, kb_ref, vb_ref, lab_ref,
                of_ref, ob_ref, sf_ref, sb_ref, *, G):
    @pl.when(pl.program_id(1) == 0)
    def _():
        sf_ref[...] = jnp.zeros_like(sf_ref)
        sb_ref[...] = jnp.zeros_like(sb_ref)

    dirs = [(_gla_decays(qf_ref[...], kf_ref[...], vf_ref[...], laf_ref[...], True, G), sf_ref, of_ref),
            (_gla_decays(qb_ref[...], kb_ref[...], vb_ref[...], lab_ref[...], False, G), sb_ref, ob_ref)]
    scored = [_gla_scores(prep) for prep, _, _ in dirs]
    updates = [_gla_chunk_updates(prep) for prep, _, _ in dirs]
    states = [_gla_states(prep, kv, dec_t, s_ref) for (prep, s_ref, _), (kv, dec_t) in zip(dirs, updates)]
    inters = [{g: _dot(qd_heads[g], st[g]) for g in prep["order"]}
              for (prep, _, _), (qd_heads, _), st in zip(dirs, scored, states)]
    for (prep, _, o_ref), (qd_heads, scores), inter in zip(dirs, scored, inters):
        _gla_outputs(prep, qd_heads, scores, inter, o_ref)


def _gla(gla_slab, loga, B, S, G=8):
    T = B * S
    R = G * GLA_CHUNK
    ns = S // R
    fwd = lambda col: (lambda b, i: (b * ns + i, col))
    bwd = lambda col: (lambda b, i: (b * ns + ns - 1 - i, col))
    kw, vw = GLA_KEY_WIDTH, GLA_VAL_WIDTH
    return pl.pallas_call(
        functools.partial(_gla_kernel, G=G),
        grid=(B, ns),
        in_specs=[
            pl.BlockSpec((R, kw), fwd(0)), pl.BlockSpec((R, kw), fwd(1)),
            pl.BlockSpec((R, vw), fwd(1)), pl.BlockSpec((R, kw), fwd(0)),
            pl.BlockSpec((R, kw), bwd(0)), pl.BlockSpec((R, kw), bwd(1)),
            pl.BlockSpec((R, vw), bwd(1)), pl.BlockSpec((R, kw), bwd(1)),
        ],
        out_specs=[pl.BlockSpec((R, vw), fwd(0)), pl.BlockSpec((R, vw), bwd(0))],
        out_shape=[jax.ShapeDtypeStruct((T, vw), BF16), jax.ShapeDtypeStruct((T, vw), BF16)],
        scratch_shapes=[pltpu.VMEM((kw, GLA_DV), F32), pltpu.VMEM((kw, GLA_DV), F32)],
        compiler_params=_cparams(("arbitrary", "arbitrary")),
        name="gla",
    )(gla_slab, gla_slab, gla_slab, loga, gla_slab, gla_slab, gla_slab, loga)


ATT_CLASSES = 4
ATT_QB = 128
ATT_KB = ATT_QB + 2 * ATT_RADIUS


ATT_UNROLL = (32, 16, 16)


def _att_kernel(q_ref, k_ref, v_ref, o_ref, m_ref, l_ref, bias_ref, *, S):
    QB, KB, NC = ATT_QB, ATT_KB, ATT_CLASSES
    L4 = S // NC
    lane = lax.broadcasted_iota(jnp.int32, (QB, LANES), 1)
    head0 = lane < ATT_HEAD_DIM

    @pl.when((pl.program_id(0) == 0) & (pl.program_id(1) == 0))
    def _():
        rowi = lax.broadcasted_iota(jnp.int32, (2 * QB, KB), 0) & (QB - 1)
        coli = lax.broadcasted_iota(jnp.int32, (2 * QB, KB), 1)
        qpos = (rowi & (QB // NC - 1)) * NC + (rowi >> _log2(QB // NC))
        kpos = (coli & (KB // NC - 1)) * NC + (coli >> _log2(KB // NC))
        for case in range(3):
            bias_ref[0, case] = jnp.where(jnp.abs(rowi - coli + case * ATT_RADIUS) <= ATT_RADIUS, 0.0, NEG_INF)
            bias_ref[1, case] = jnp.where(jnp.abs(qpos - kpos + case * ATT_RADIUS) <= ATT_RADIUS, 0.0, NEG_INF)

    for pi, (_, d) in enumerate(DILATED_PATTERNS):
        L = S // d
        nb = L // QB
        shift = nb.bit_length() - 1
        first = pi == 0
        last = pi == len(DILATED_PATTERNS) - 1

        def scores(n, d=d, L=L, nb=nb, shift=shift):
            cls = n >> shift
            q0 = (n & (nb - 1)) * QB
            ws = jnp.clip(q0 - ATT_RADIUS, 0, L - KB)
            if d == 1:
                qsls = [pl.ds(pl.multiple_of(c * L4 + q0 // NC, QB // NC), QB // NC) for c in range(NC)]
                ksls = [pl.ds(pl.multiple_of(c * L4 + ws // NC, ATT_RADIUS // NC), KB // NC) for c in range(NC)]
            elif d == NC:
                qsls = [pl.ds(pl.multiple_of(cls * L4 + q0, QB), QB)]
                ksls = [pl.ds(pl.multiple_of(cls * L4 + ws, ATT_RADIUS), KB)]
            else:
                base = (cls & (NC - 1)) * L4 + (cls >> _log2(NC))
                qsls = [pl.ds(base + NC * q0, QB, stride=NC)]
                ksls = [pl.ds(base + NC * ws, KB, stride=NC)]
            q = jnp.concatenate([q_ref[sl, :] for sl in qsls], axis=0)
            kw = jnp.concatenate([k_ref[sl, :] for sl in ksls], axis=0)
            kb = kw.astype(BF16)
            bias = bias_ref[1 if d == 1 else 0, (q0 - ws) >> _log2(ATT_RADIUS), :QB]
            q_heads = (jnp.where(head0, q, 0.0), jnp.where(head0, 0.0, q))
            s = [_dot_nt(qh.astype(BF16), kb) + bias for qh in q_heads]
            return qsls, ksls, s

        def softmax_pv(qsls, ksls, s):
            vw = jnp.concatenate([v_ref[sl, :] for sl in ksls], axis=0)
            v_ones = jnp.concatenate([vw.astype(BF16), jnp.ones((KB, LANES), BF16)], axis=1)
            m_h = [jnp.max(t, axis=-1, keepdims=True) for t in s]
            pv = [_dot(jnp.exp2(t - m).astype(BF16), v_ones) for t, m in zip(s, m_h)]
            acc_b = jnp.where(head0, pv[0][:, :LANES], pv[1][:, :LANES])
            m_b = jnp.where(head0, m_h[0], m_h[1])
            l_b = jnp.where(head0, pv[0][:, LANES:], pv[1][:, LANES:])
            return qsls, acc_b, m_b, l_b

        def load(ref, sls):
            return jnp.concatenate([ref[sl, :] for sl in sls], axis=0)

        def store(ref, sls, val):
            n = val.shape[0] // len(sls)
            for i, sl in enumerate(sls):
                ref[sl, :] = val[i * n:(i + 1) * n]

        unroll = ATT_UNROLL[pi]

        def body(n, carry, first=first, last=last, unroll=unroll):
            staged = [scores(n * unroll + u) for u in range(unroll)]
            blocks = [softmax_pv(*st) for st in staged]
            for qsls, acc_b, m_b, l_b in blocks:
                if first:
                    acc, m_new, l_new = acc_b, m_b, l_b
                else:
                    m_old = load(m_ref, qsls)
                    m_new = jnp.maximum(m_old, m_b)
                    w_old = jnp.exp2(m_old - m_new)
                    w_blk = jnp.exp2(m_b - m_new)
                    acc = load(o_ref, qsls) * w_old + acc_b * w_blk
                    l_new = load(l_ref, qsls) * w_old + l_b * w_blk
                if last:
                    store(o_ref, qsls, acc / l_new)
                else:
                    store(o_ref, qsls, acc)
                    store(m_ref, qsls, m_new)
                    store(l_ref, qsls, l_new)
            return carry

        lax.fori_loop(0, S // (QB * unroll), body, 0)


def _attention(att_slab, B, S):
    T = B * S
    ncol = ATT_WIDTH // LANES
    return pl.pallas_call(
        functools.partial(_att_kernel, S=S),
        grid=(B, ncol),
        in_specs=[
            pl.BlockSpec((S, LANES), lambda b, h: (b, h)),
            pl.BlockSpec((S, LANES), lambda b, h: (b, ncol + h)),
            pl.BlockSpec((S, LANES), lambda b, h: (b, 2 * ncol + h)),
        ],
        out_specs=pl.BlockSpec((S, LANES), lambda b, h: (b, h)),
        out_shape=jax.ShapeDtypeStruct((T, ATT_WIDTH), F32),
        scratch_shapes=[pltpu.VMEM((S, LANES), F32), pltpu.VMEM((S, LANES), F32),
                        pltpu.VMEM((2, 3, 2 * ATT_QB, ATT_KB), F32)],
        compiler_params=_cparams(("arbitrary", "arbitrary")),
        name="dilated_attention",
    )(att_slab, att_slab, att_slab)


PACK_WORDS = D_MODEL // 2
ROW_TILE = PACK_WORDS // LANES
HIGH_HALF = -65536


def _pack_rows(x):
    bits = lambda v: lax.bitcast_convert_type(v.astype(BF16).astype(F32), jnp.int32)
    low = (bits(x[:, :PACK_WORDS]) >> 16) & 0xFFFF
    return (bits(x[:, PACK_WORDS:]) & HIGH_HALF) | low


def _unpack_rows(w):
    low = lax.bitcast_convert_type(w << 16, F32)
    high = lax.bitcast_convert_type(w & HIGH_HALF, F32)
    return jnp.concatenate([low, high], axis=1).astype(BF16)


def _to_row_tiles(ref, w):
    n = w.shape[0]
    for j in range(ROW_TILE):
        ref[pl.ds(j, n, stride=ROW_TILE), :] = w[:, j * LANES:(j + 1) * LANES]


def _from_row_tiles(ref, n):
    return jnp.concatenate([ref[pl.ds(j, n, stride=ROW_TILE), :] for j in range(ROW_TILE)], axis=1)


def _tile_copy(src_ref, src_row, dst_ref, dst_row, sem):
    src = pl.ds(pl.multiple_of(src_row * ROW_TILE, ROW_TILE), ROW_TILE)
    dst = pl.ds(pl.multiple_of(dst_row * ROW_TILE, ROW_TILE), ROW_TILE)
    return pltpu.make_async_copy(src_ref.at[src], dst_ref.at[dst], sem)


def _outproj_kernel(of_ref, ob_ref, gg_ref, att_ref, x_ref, gnw_ref, wo1_ref, wo2_ref,
                    n2_ref, wr_ref, br_ref, h_ref, u_ref, lg_ref, stage_ref):
    rows = stage_ref.shape[1] // ATT_CLASSES
    for j in range(ATT_WIDTH // LANES):
        for c in range(ATT_CLASSES):
            stage_ref[j, pl.ds(c, rows, stride=ATT_CLASSES), :] = att_ref[c, :, j * LANES:(j + 1) * LANES]
    att = jnp.concatenate([stage_ref[j] for j in range(ATT_WIDTH // LANES)], axis=1)
    o = of_ref[...].astype(F32) + ob_ref[...].astype(F32)
    gate = gg_ref[...].astype(F32)
    gnw = gnw_ref[...]
    parts = []
    for h in range(GLA_HEADS):
        sl = slice(h * GLA_DV, (h + 1) * GLA_DV)
        parts.append(_rms(o[:, sl], gnw))
    y = jnp.concatenate(parts, axis=1) * (gate / (1.0 + jnp.exp(-gate)))
    mix = _dot(y.astype(BF16), wo1_ref[...]) + _dot(att.astype(BF16), wo2_ref[...])
    h = x_ref[...] + mix
    h_ref[...] = h
    u = _rms(h, n2_ref[...])
    _to_row_tiles(u_ref, _pack_rows(u))
    u_hi = u.astype(BF16)
    u_lo = (u - u_hi.astype(F32)).astype(BF16)
    hi_both = _dot_nt(wr_ref[...], u_hi)
    lg_ref[...] = (hi_both[:LANES] + hi_both[LANES:] + _dot_nt(wr_ref[:LANES], u_lo)) + br_ref[...]


def _outproj(o_f, o_b, gate, att_out, x2, gla_norm_w, w_out, norm2_w, wr, br, tm=512):
    T = x2.shape[0]
    nS = att_out.shape[2] * ATT_CLASSES // tm
    row = lambda i: (i, 0)
    const = lambda i: (0, 0)
    wo = w_out.astype(BF16)
    wr_hi = wr.astype(BF16)
    wr_lo = (wr - wr_hi.astype(F32)).astype(BF16)
    wr = jnp.concatenate([wr_hi, wr_lo], axis=0)
    return pl.pallas_call(
        _outproj_kernel,
        grid=(T // tm,),
        in_specs=[
            pl.BlockSpec((tm, GLA_VAL_WIDTH), row),
            pl.BlockSpec((tm, GLA_VAL_WIDTH), row),
            pl.BlockSpec((tm, GLA_VAL_WIDTH), row),
            pl.BlockSpec((None, ATT_CLASSES, tm // ATT_CLASSES, ATT_WIDTH), lambda i: (i // nS, 0, i % nS, 0)),
            pl.BlockSpec((tm, D_MODEL), row),
            pl.BlockSpec((1, GLA_DV), const),
            pl.BlockSpec((GLA_VAL_WIDTH, D_MODEL), const),
            pl.BlockSpec((ATT_WIDTH, D_MODEL), lambda i: (GLA_VAL_WIDTH // ATT_WIDTH, 0)),
            pl.BlockSpec((1, D_MODEL), const),
            pl.BlockSpec((2 * LANES, D_MODEL), const),
            pl.BlockSpec((LANES, 1), const),
        ],
        out_specs=[
            pl.BlockSpec((tm, D_MODEL), row),
            pl.BlockSpec((tm * ROW_TILE, LANES), row),
            pl.BlockSpec((LANES, tm), lambda i: (0, i)),
        ],
        out_shape=[
            jax.ShapeDtypeStruct((T, D_MODEL), F32),
            jax.ShapeDtypeStruct((T * ROW_TILE, LANES), jnp.int32),
            jax.ShapeDtypeStruct((LANES, T), F32),
        ],
        scratch_shapes=[pltpu.VMEM((ATT_WIDTH // LANES, tm, LANES), F32)],
        compiler_params=_cparams(("arbitrary",)),
        name="outproj",
    )(o_f, o_b, gate, att_out, x2, gla_norm_w[None, :], wo, wo,
      norm2_w[None, :], wr, br)


INFO_E1, INFO_E2, INFO_R1, INFO_R2, INFO_W1, INFO_W2 = range(6)
ROUTE_ROWS = 40


def _route_kernel(lg_ref, info_ref, cnt_ref, carry_ref):
    @pl.when(pl.program_id(0) == 0)
    def _():
        carry_ref[...] = jnp.zeros_like(carry_ref)

    lg = lg_ref[:ROUTE_ROWS, :]
    tr = lg.shape[1]
    row = lax.broadcasted_iota(jnp.int32, (ROUTE_ROWS, tr), 0)
    big = jnp.int32(1 << 20)
    is_g = (row >= MOE_N_EXPERTS) & (row < MOE_N_EXPERTS + MOE_GROUPS)
    gl = jnp.where(is_g, lg, -jnp.inf)
    gmax = jnp.max(gl, axis=0, keepdims=True)
    gsel = jnp.min(jnp.where(gl == gmax, row - MOE_N_EXPERTS, big), axis=0, keepdims=True)
    g_w = 1.0 / jnp.sum(jnp.where(is_g, jnp.exp(lg - gmax), 0.0), axis=0, keepdims=True)
    in_grp = (row < MOE_N_EXPERTS) & ((row >> MOE_GROUP_SHIFT) == gsel)
    el = jnp.where(in_grp, lg, -jnp.inf)
    v1 = jnp.max(el, axis=0, keepdims=True)
    i1 = jnp.min(jnp.where(el == v1, row, big), axis=0, keepdims=True)
    el2 = jnp.where(row == i1, -jnp.inf, el)
    v2 = jnp.max(el2, axis=0, keepdims=True)
    i2 = jnp.min(jnp.where(el2 == v2, row, big), axis=0, keepdims=True)
    t = jnp.exp(v2 - v1)
    w1 = g_w * (1.0 / (1.0 + t))
    w2 = g_w * (t / (1.0 + t))

    erow = lax.broadcasted_iota(jnp.int32, (MOE_N_EXPERTS, tr), 0)
    hit1 = erow == i1
    hit2 = erow == i2
    member = jnp.where(hit1 | hit2, 1.0, 0.0)
    r = lax.broadcasted_iota(jnp.int32, (tr, tr), 0)
    c = lax.broadcasted_iota(jnp.int32, (tr, tr), 1)
    earlier = jnp.where(r < c, 1.0, 0.0).astype(BF16)
    carry = carry_ref[...]
    prefix = _dot(member.astype(BF16), earlier) + carry[:, 0:1]
    rank1 = jnp.sum(jnp.where(hit1, prefix, 0.0), axis=0, keepdims=True)
    rank2 = jnp.sum(jnp.where(hit2, prefix, 0.0), axis=0, keepdims=True)
    carry = carry + jnp.sum(member, axis=1, keepdims=True)
    carry_ref[...] = carry
    cnt_ref[...] = carry

    zero = jnp.zeros_like(w1)
    info_ref[...] = jnp.concatenate([i1.astype(F32), i2.astype(F32), rank1, rank2, w1, w2, zero, zero], axis=0)


def _route(logits_t, tr=1024):
    T = logits_t.shape[1]
    return pl.pallas_call(
        _route_kernel,
        grid=(T // tr,),
        in_specs=[pl.BlockSpec((LANES, tr), lambda i: (0, i))],
        out_specs=[pl.BlockSpec((8, tr), lambda i: (0, i)),
                   pl.BlockSpec((MOE_N_EXPERTS, LANES), lambda i: (0, 0))],
        out_shape=[jax.ShapeDtypeStruct((8, T), F32), jax.ShapeDtypeStruct((MOE_N_EXPERTS, LANES), F32)],
        scratch_shapes=[pltpu.VMEM((MOE_N_EXPERTS, LANES), F32)],
        compiler_params=_cparams(("arbitrary",)),
        name="route",
    )(logits_t)


ROW_UNROLL = 16


def _dispatch_kernel(dest_ref, pend_ref, u_ref, xs_ref, zbuf, sem, zsem, *, td, T, nblk):
    @pl.when(pl.program_id(0) == 0)
    def _():
        zbuf[...] = jnp.zeros_like(zbuf)
        n_used = pend_ref[MOE_N_EXPERTS - 1] >> MOE_ROWS_SHIFT

        def zero_copy(blk):
            start = pl.multiple_of(blk * (MOE_ROWS * ROW_TILE), MOE_ROWS * ROW_TILE)
            return pltpu.make_async_copy(zbuf, xs_ref.at[pl.ds(start, MOE_ROWS * ROW_TILE)], zsem)

        def each_pad_block(fn):
            def per_expert(e, carry):
                prev = jnp.where(e > 0, pend_ref[jnp.maximum(e - 1, 0)], 0)

                @pl.when(pend_ref[e] > prev)
                def _():
                    fn((pend_ref[e] >> MOE_ROWS_SHIFT) - 1)
                return carry

            def per_tail(j, carry):
                @pl.when(n_used + j < nblk)
                def _():
                    fn(n_used + j)
                return carry

            lax.fori_loop(0, MOE_N_EXPERTS, per_expert, 0)
            lax.fori_loop(0, MOE_N_EXPERTS, per_tail, 0)

        each_pad_block(lambda blk: zero_copy(blk).start())
        each_pad_block(lambda blk: zero_copy(blk).wait())

    base = pl.program_id(0) * td

    def issue(g, carry):
        for j in range(ROW_UNROLL):
            r = g * ROW_UNROLL + j
            for k in range(MOE_TOP_K):
                _tile_copy(u_ref, r, xs_ref, dest_ref[k * T + base + r], sem).start(priority=k)
        return carry

    lax.fori_loop(0, td // ROW_UNROLL, issue, 0)
    for k in range(MOE_TOP_K):
        pltpu.make_async_copy(u_ref, xs_ref.at[pl.ds(0, td * ROW_TILE)], sem).wait()


def _dispatch(dest, pend, u2, cap, td=2048):
    T = u2.shape[0] // ROW_TILE
    return pl.pallas_call(
        functools.partial(_dispatch_kernel, td=td, T=T, nblk=cap // MOE_ROWS),
        grid_spec=pltpu.PrefetchScalarGridSpec(
            num_scalar_prefetch=2,
            grid=(T // td,),
            in_specs=[pl.BlockSpec((td * ROW_TILE, LANES), lambda i, d, z: (i, 0))],
            out_specs=pl.BlockSpec(memory_space=pl.ANY),
            scratch_shapes=[pltpu.VMEM((MOE_ROWS * ROW_TILE, LANES), jnp.int32),
                            pltpu.SemaphoreType.DMA(()), pltpu.SemaphoreType.DMA(())],
        ),
        out_shape=jax.ShapeDtypeStruct((cap * ROW_TILE, LANES), jnp.int32),
        compiler_params=_cparams(("arbitrary",)),
        name="dispatch",
    )(dest, pend, u2)


def _expert_kernel(pend_ref, xs_hbm, wg_hbm, wu_hbm, wd_hbm, ys_hbm,
                   xbuf, ybuf, zbuf, stage_g, stage_u, stage_d, wgb, wub, wdb, xsem, ysem, wsem, zsem, *, nblk):
    last = MOE_N_EXPERTS - 1
    n_used = pend_ref[last] >> MOE_ROWS_SHIFT
    block_rows = MOE_ROWS * ROW_TILE

    def x_copy(b, slot):
        start = pl.multiple_of(b * block_rows, block_rows)
        return pltpu.make_async_copy(xs_hbm.at[pl.ds(start, block_rows)], xbuf.at[slot], xsem.at[slot])

    def y_copy(b, slot):
        start = pl.multiple_of(b * block_rows, block_rows)
        return pltpu.make_async_copy(ybuf.at[slot], ys_hbm.at[pl.ds(start, block_rows)], ysem.at[slot])

    def zero_copy(b):
        start = pl.multiple_of(b * block_rows, block_rows)
        return pltpu.make_async_copy(zbuf, ys_hbm.at[pl.ds(start, block_rows)], zsem)

    def weight_copies(e):
        return (pltpu.make_async_copy(wg_hbm.at[e], stage_g, wsem.at[0]),
                pltpu.make_async_copy(wu_hbm.at[e], stage_u, wsem.at[1]),
                pltpu.make_async_copy(wd_hbm.at[e], stage_d, wsem.at[2]))

    def owner(start, row):
        return lax.while_loop(lambda e: (e < last) & (pend_ref[e] <= row), lambda e: e + 1, start)

    for c in weight_copies(owner(0, 0)):
        c.start()
    x_copy(0, 0).start()

    zbuf[...] = jnp.zeros_like(zbuf)

    def tail(fn):
        def step(b, carry):
            fn(b)
            return carry
        lax.fori_loop(n_used, nblk, step, 0)

    tail(lambda b: zero_copy(b).start())

    def body(b, cur):
        slot = b & 1
        e = owner(jnp.maximum(cur, 0), b * MOE_ROWS)
        x_copy(b, slot).wait()

        @pl.when(b + 1 < n_used)
        def _():
            x_copy(b + 1, 1 - slot).start()

        @pl.when(e != cur)
        def _():
            for c in weight_copies(e):
                c.wait()
            wgb[...] = stage_g[...].astype(BF16)
            wub[...] = stage_u[...].astype(BF16)
            wdb[...] = stage_d[...].astype(BF16)

            @pl.when(pend_ref[e] < pend_ref[last])
            def _():
                for c in weight_copies(owner(e + 1, pend_ref[e])):
                    c.start(priority=1)

        @pl.when(b >= 2)
        def _():
            y_copy(b - 2, slot).wait()

        xb = _unpack_rows(_from_row_tiles(xbuf.at[slot], MOE_ROWS))
        g = _dot(xb, wgb[...])
        u = _dot(xb, wub[...])
        hid = (g / (1.0 + jnp.exp(-g))) * u
        _to_row_tiles(ybuf.at[slot], _pack_rows(_dot(hid.astype(BF16), wdb[...])))
        y_copy(b, slot).start()
        return e

    lax.fori_loop(0, n_used, body, jnp.int32(-1))

    @pl.when(n_used >= 2)
    def _():
        y_copy(n_used - 2, n_used & 1).wait()
    y_copy(n_used - 1, (n_used - 1) & 1).wait()
    tail(lambda b: zero_copy(b).wait())


def _experts(pend, xs, w_gate, w_up, w_down):
    cap = xs.shape[0] // ROW_TILE
    nblk = cap // MOE_ROWS
    block = (MOE_ROWS * ROW_TILE, LANES)
    anywhere = pl.BlockSpec(memory_space=pl.ANY)
    return pl.pallas_call(
        functools.partial(_expert_kernel, nblk=nblk),
        grid_spec=pltpu.PrefetchScalarGridSpec(
            num_scalar_prefetch=1,
            grid=(1,),
            in_specs=[anywhere, anywhere, anywhere, anywhere],
            out_specs=anywhere,
            scratch_shapes=[pltpu.VMEM((2,) + block, jnp.int32),
                            pltpu.VMEM((2,) + block, jnp.int32),
                            pltpu.VMEM(block, jnp.int32),
                            pltpu.VMEM((D_MODEL, MOE_D_FF), F32),
                            pltpu.VMEM((D_MODEL, MOE_D_FF), F32),
                            pltpu.VMEM((MOE_D_FF, D_MODEL), F32),
                            pltpu.VMEM((D_MODEL, MOE_D_FF), BF16),
                            pltpu.VMEM((D_MODEL, MOE_D_FF), BF16),
                            pltpu.VMEM((MOE_D_FF, D_MODEL), BF16),
                            pltpu.SemaphoreType.DMA((2,)),
                            pltpu.SemaphoreType.DMA((2,)),
                            pltpu.SemaphoreType.DMA((3,)),
                            pltpu.SemaphoreType.DMA(())],
        ),
        out_shape=jax.ShapeDtypeStruct((cap * ROW_TILE, LANES), jnp.int32),
        compiler_params=_cparams(("arbitrary",)),
        name="experts",
    )(pend, xs, w_gate, w_up, w_down)


def _combine_kernel(dest_ref, ys_ref, info_ref, h_ref, fw_ref, o_ref, buf, sem, *, tc, T):
    i = pl.program_id(0)
    n = pl.num_programs(0)

    def issue(step, slot):
        base = step * tc

        def body(g, carry):
            for j in range(ROW_UNROLL):
                r = g * ROW_UNROLL + j
                for k in range(MOE_TOP_K):
                    _tile_copy(ys_ref, dest_ref[k * T + base + r], buf.at[slot, k], r,
                               sem.at[slot]).start(priority=k)
            return carry

        lax.fori_loop(0, tc // ROW_UNROLL, body, 0)

    @pl.when(i == 0)
    def _():
        issue(0, 0)

    slot = i % 2

    @pl.when(i + 1 < n)
    def _():
        issue(i + 1, 1 - slot)

    for k in range(MOE_TOP_K):
        pltpu.make_async_copy(ys_ref.at[pl.ds(0, tc * ROW_TILE)], buf.at[slot, k], sem.at[slot]).wait()

    info_t = jnp.concatenate([info_ref[...]] * (LANES // 8), axis=0).T
    w1 = info_t[:, INFO_W1:INFO_W1 + 1]
    w2 = info_t[:, INFO_W2:INFO_W2 + 1]
    y1 = _unpack_rows(_from_row_tiles(buf.at[slot, 0], tc)).astype(F32)
    y2 = _unpack_rows(_from_row_tiles(buf.at[slot, 1], tc)).astype(F32)
    h = h_ref[...] + (y1 * w1 + y2 * w2)
    o_ref[...] = _rms(h, fw_ref[...])


def _combine(dest, ys, info, h, final_w, tc=512):
    T = h.shape[0]
    return pl.pallas_call(
        functools.partial(_combine_kernel, tc=tc, T=T),
        grid_spec=pltpu.PrefetchScalarGridSpec(
            num_scalar_prefetch=1,
            grid=(T // tc,),
            in_specs=[pl.BlockSpec(memory_space=pl.ANY),
                      pl.BlockSpec((8, tc), lambda i, d: (0, i)),
                      pl.BlockSpec((tc, D_MODEL), lambda i, d: (i, 0)),
                      pl.BlockSpec((1, D_MODEL), lambda i, d: (0, 0))],
            out_specs=pl.BlockSpec((tc, D_MODEL), lambda i, d: (i, 0)),
            scratch_shapes=[pltpu.VMEM((2, MOE_TOP_K, tc * ROW_TILE, LANES), jnp.int32),
                            pltpu.SemaphoreType.DMA((2,))],
        ),
        out_shape=jax.ShapeDtypeStruct((T, D_MODEL), F32),
        compiler_params=_cparams(("arbitrary",)),
        name="combine",
    )(dest, ys, info, h, final_w[None, :])


def _plan_kernel(info_ref, cnt_ref, dest_ref, pend_ref):
    cnt = cnt_ref[...].astype(jnp.int32)
    nblk_e = ((cnt + (MOE_ROWS - 1)) >> MOE_ROWS_SHIFT).astype(F32)
    r = lax.broadcasted_iota(jnp.int32, (MOE_N_EXPERTS, MOE_N_EXPERTS), 0)
    c = lax.broadcasted_iota(jnp.int32, (MOE_N_EXPERTS, MOE_N_EXPERTS), 1)
    before = jnp.where(c < r, 1.0, 0.0).astype(BF16)
    first_blk = _dot(before, nblk_e.astype(BF16))
    pstart = first_blk[:, 0:1] * float(MOE_ROWS)
    pend_ref[...] = ((first_blk + nblk_e) * float(MOE_ROWS)).astype(jnp.int32)

    info = info_ref[...]
    erow = lax.broadcasted_iota(jnp.int32, (MOE_N_EXPERTS, info.shape[1]), 0)
    start_of = lambda e: jnp.sum(jnp.where(erow == e.astype(jnp.int32), pstart, 0.0), axis=0, keepdims=True)
    d1 = info[INFO_R1:INFO_R1 + 1] + start_of(info[INFO_E1:INFO_E1 + 1])
    d2 = info[INFO_R2:INFO_R2 + 1] + start_of(info[INFO_E2:INFO_E2 + 1])
    zero = jnp.zeros_like(d1)
    dest_ref[...] = jnp.concatenate([d1, d2] + [zero] * 6, axis=0).astype(jnp.int32)


def _plan(info, counts, tr=2048):
    T = info.shape[1]
    dest8, pend = pl.pallas_call(
        _plan_kernel,
        grid=(T // tr,),
        in_specs=[pl.BlockSpec((8, tr), lambda i: (0, i)),
                  pl.BlockSpec((MOE_N_EXPERTS, LANES), lambda i: (0, 0))],
        out_specs=[pl.BlockSpec((8, tr), lambda i: (0, i)),
                   pl.BlockSpec((MOE_N_EXPERTS, LANES), lambda i: (0, 0))],
        out_shape=[jax.ShapeDtypeStruct((8, T), jnp.int32),
                   jax.ShapeDtypeStruct((MOE_N_EXPERTS, LANES), jnp.int32)],
        compiler_params=_cparams(("arbitrary",)),
        name="plan",
    )(info, counts)
    return dest8[:MOE_TOP_K].reshape(-1), pend[:, 0]


def _moe_capacity(T):
    return (-(-(T * MOE_TOP_K) // MOE_ROWS) + MOE_N_EXPERTS) * MOE_ROWS


def _router_weights(router_group_w, router_group_b, router_expert_w, router_expert_b):
    we = jnp.transpose(router_expert_w, (0, 2, 1)).reshape(MOE_N_EXPERTS, D_MODEL)
    pad = LANES - MOE_N_EXPERTS - MOE_GROUPS
    wr = jnp.concatenate([we, router_group_w.T, jnp.zeros((pad, D_MODEL), F32)], axis=0)
    br = jnp.concatenate([router_expert_b.reshape(-1), router_group_b, jnp.zeros((pad,), F32)])[:, None]
    return wr, br


def kernel(x, norm1_w, w_in, gla_fwd_gate_w, gla_fwd_gate_b, gla_bwd_gate_w, gla_bwd_gate_b,
           gla_norm_w, w_out, norm2_w, router_group_w, router_group_b, router_expert_w,
           router_expert_b, expert_w_gate, expert_w_up, expert_w_down, final_norm_w):
    B, S, D = x.shape
    T = B * S
    assert norm1_w.shape[0] == 1, "single-layer trunk: the final norm is fused into the combine step"
    h = x.reshape(T, D)
    gla_slab, gate, loga, att_slab = _inproj(h, S, norm1_w[0], w_in[0], gla_fwd_gate_w[0], gla_fwd_gate_b[0],
                                       gla_bwd_gate_w[0], gla_bwd_gate_b[0])
    o_f, o_b = _gla(gla_slab, loga, B, S)
    att_out = _attention(att_slab.reshape(T, 3 * ATT_WIDTH), B, S)
    att_out = att_out.reshape(B, ATT_CLASSES, S // ATT_CLASSES, ATT_WIDTH)
    wr, br = _router_weights(router_group_w[0], router_group_b[0], router_expert_w[0], router_expert_b[0])
    h, u2, logits = _outproj(o_f, o_b, gate, att_out, h, gla_norm_w[0], w_out[0], norm2_w[0], wr, br)
    info, counts = _route(logits)
    dest, pend = _plan(info, counts)
    xs = _dispatch(dest, pend, u2, _moe_capacity(T))
    ys = _experts(pend, xs, expert_w_gate[0], expert_w_up[0], expert_w_down[0])
    out = _combine(dest, ys, info, h, final_norm_w)
    return out.reshape(B, S, D)
```

```python
import functools

import jax
import jax.numpy as jnp
import numpy as np
from jax import lax
from jax.experimental import pallas as pl
from jax.experimental.pallas import tpu as pltpu

F32 = jnp.float32
BF16 = jnp.bfloat16

D_MODEL = 1024
GLA_HEADS = 4
GLA_DV = 128
GLA_DK = 64
GLA_KEY_WIDTH = GLA_HEADS * GLA_DK
GLA_VAL_WIDTH = GLA_HEADS * GLA_DV
GLA_GATE_RANK = 16
GLA_TAU = 16.0
GLA_CHUNK = 64
ATT_WIDTH = 512
ATT_HEAD_DIM = 64
ATT_HEADS = 8
ROT_DIM = 16
ROPE_THETA = 500000.0
DILATED_PATTERNS = ((128, 1), (512, 4), (2048, 16))
ATT_RADIUS = 64
MOE_GROUPS = 4
MOE_EXPERTS_PER_GROUP = 8
MOE_N_EXPERTS = 32
MOE_TOP_K = 2
MOE_D_FF = 512
EPS = 1e-6
NEG_INF = -1e30
LOG2E = 1.4426950408889634

LANES = 128
MOE_ROWS = 256


def _log2(n):
    assert n & (n - 1) == 0, n
    return n.bit_length() - 1


GLA_CHUNK_SHIFT = _log2(GLA_CHUNK)
GLA_DK_SHIFT = _log2(GLA_DK)
MOE_ROWS_SHIFT = _log2(MOE_ROWS)
MOE_GROUP_SHIFT = _log2(MOE_EXPERTS_PER_GROUP)
VMEM_LIMIT = 56 * 1024 * 1024


def _cparams(sem):
    return pltpu.CompilerParams(dimension_semantics=sem, vmem_limit_bytes=VMEM_LIMIT)


def _dot(a, b):
    return jnp.dot(a, b, preferred_element_type=F32)


def _dot_nt(a, b):
    return lax.dot_general(a, b, (((1,), (1,)), ((), ())), preferred_element_type=F32)


def _dot_tn(a, b):
    return lax.dot_general(a, b, (((0,), (0,)), ((), ())), preferred_element_type=F32)


def _rms(x, w):
    return x * lax.rsqrt(jnp.mean(x * x, axis=-1, keepdims=True) + EPS) * w


def _inproj_kernel(x_ref, n1_ref, wg_ref, wlr_ref, wa_ref, gw_ref, gb_ref,
                   rc_ref, rs1_ref, rs2_ref, gla_ref, gate_ref, loga_ref, att_ref, stage_ref, wgb, wlrb):
    @pl.when(pl.program_id(0) == 0)
    def _():
        wgb[...] = wg_ref[...].astype(BF16)
        wlrb[...] = wlr_ref[...].astype(BF16)

    x = x_ref[...]
    ub = _rms(x, n1_ref[...]).astype(BF16)
    g = _dot(ub, wgb[...])
    qkv = 2 * GLA_KEY_WIDTH + GLA_VAL_WIDTH
    gla_ref[:, :GLA_KEY_WIDTH] = g[:, :GLA_KEY_WIDTH] * (GLA_DK ** -0.5)
    gla_ref[:, GLA_KEY_WIDTH:] = g[:, GLA_KEY_WIDTH:qkv]
    gate_ref[...] = g[:, qkv:].astype(BF16)
    lr = _dot(ub, wlrb[...])
    gate = _dot(lr.astype(BF16), gw_ref[...]) + gb_ref[...]
    loga_ref[...] = (jnp.minimum(gate, 0.0) - jnp.log(1.0 + jnp.exp(-jnp.abs(gate)))) * (1.0 / GLA_TAU)
    a = _dot(ub, wa_ref[...])
    qk = a[:, :2 * ATT_WIDTH]
    reps = 2 * ATT_WIDTH // LANES
    c = jnp.concatenate([rc_ref[...]] * reps, axis=1)
    s1 = jnp.concatenate([rs1_ref[...]] * reps, axis=1)
    s2 = jnp.concatenate([rs2_ref[...]] * reps, axis=1)
    half = ROT_DIM // 2
    n = 2 * ATT_WIDTH
    roped = qk * c + pltpu.roll(qk, n - half, 1) * s1 + pltpu.roll(qk, half, 1) * s2
    qkv = jnp.concatenate([roped[:, :ATT_WIDTH] * (ATT_HEAD_DIM ** -0.5 * LOG2E), roped[:, ATT_WIDTH:],
                           a[:, 2 * ATT_WIDTH:]], axis=1)
    rows = x.shape[0] // ATT_CLASSES
    for j in range(3 * ATT_WIDTH // LANES):
        cols = slice(j * LANES, (j + 1) * LANES)
        stage_ref[j] = qkv[:, cols]
        for c in range(ATT_CLASSES):
            att_ref[c, :, cols] = stage_ref[j, pl.ds(c, rows, stride=ATT_CLASSES), :]


def _rope_lane_tables(S):
    half = ROT_DIM // 2
    inv = np.float32(ROPE_THETA) ** (-(np.arange(0, ROT_DIM, 2, dtype=np.float32) / np.float32(ROT_DIM)))
    ang = np.arange(S, dtype=np.float32)[:, None] * inv[None, :].astype(np.float32)
    cos, sin = np.cos(ang), np.sin(ang)
    ones = np.ones((S, ATT_HEAD_DIM - ROT_DIM), np.float32)
    zeros = np.zeros((S, ATT_HEAD_DIM - ROT_DIM), np.float32)
    zeros8 = np.zeros((S, half), np.float32)
    rep = LANES // ATT_HEAD_DIM
    c = np.tile(np.concatenate([cos, cos, ones], axis=1), (1, rep))
    s1 = np.tile(np.concatenate([-sin, zeros8, zeros], axis=1), (1, rep))
    s2 = np.tile(np.concatenate([zeros8, sin, zeros], axis=1), (1, rep))
    return jnp.asarray(c), jnp.asarray(s1), jnp.asarray(s2)


def _inproj(x2, S, norm1_w, w_in, wf, bfw, wb, bbw, tm=512):
    T = x2.shape[0]
    o_lr = 2 * GLA_KEY_WIDTH + 2 * GLA_VAL_WIDTH
    o_att = o_lr + 2 * GLA_GATE_RANK
    wa = w_in[:, o_att:].astype(BF16)
    zeros = jnp.zeros((GLA_GATE_RANK, GLA_KEY_WIDTH), F32)
    gw = jnp.concatenate([jnp.concatenate([wf, zeros], axis=1), jnp.concatenate([zeros, wb], axis=1),
                          jnp.zeros((LANES - 2 * GLA_GATE_RANK, 2 * GLA_KEY_WIDTH), F32)], axis=0).astype(BF16)
    gb = jnp.concatenate([bfw, bbw])[None, :]
    rc, rs1, rs2 = _rope_lane_tables(S)
    nS = S // tm
    row = lambda i: (i, 0)
    const = lambda i: (0, 0)
    pos = lambda i: (i % nS, 0)
    return pl.pallas_call(
        _inproj_kernel,
        grid=(T // tm,),
        in_specs=[
            pl.BlockSpec((tm, D_MODEL), row),
            pl.BlockSpec((1, D_MODEL), const),
            pl.BlockSpec((D_MODEL, o_lr), const),
            pl.BlockSpec((D_MODEL, LANES), lambda i: (0, o_lr // LANES)),
            pl.BlockSpec((D_MODEL, 3 * ATT_WIDTH), const),
            pl.BlockSpec((LANES, 2 * GLA_KEY_WIDTH), const),
            pl.BlockSpec((1, 2 * GLA_KEY_WIDTH), const),
            pl.BlockSpec((tm, LANES), pos),
            pl.BlockSpec((tm, LANES), pos),
            pl.BlockSpec((tm, LANES), pos),
        ],
        out_specs=[
            pl.BlockSpec((tm, o_lr - GLA_VAL_WIDTH), row),
            pl.BlockSpec((tm, GLA_VAL_WIDTH), row),
            pl.BlockSpec((tm, 2 * GLA_KEY_WIDTH), row),
            pl.BlockSpec((None, ATT_CLASSES, tm // ATT_CLASSES, 3 * ATT_WIDTH),
                         lambda i: (i // nS, 0, i % nS, 0)),
        ],
        out_shape=[
            jax.ShapeDtypeStruct((T, o_lr - GLA_VAL_WIDTH), F32),
            jax.ShapeDtypeStruct((T, GLA_VAL_WIDTH), BF16),
            jax.ShapeDtypeStruct((T, 2 * GLA_KEY_WIDTH), F32),
            jax.ShapeDtypeStruct((T // S, ATT_CLASSES, S // ATT_CLASSES, 3 * ATT_WIDTH), F32),
        ],
        scratch_shapes=[pltpu.VMEM((3 * ATT_WIDTH // LANES, tm, LANES), F32),
                        pltpu.VMEM((D_MODEL, o_lr), BF16), pltpu.VMEM((D_MODEL, LANES), BF16)],
        compiler_params=_cparams(("arbitrary",)),
        name="inproj",
    )(x2, norm1_w[None, :], w_in, w_in, wa, gw, gb, rc, rs1, rs2)


def _gla_decays(q, k, v, la, forward, G):
    C = GLA_CHUNK
    R = G * C
    r = lax.broadcasted_iota(jnp.int32, (R, R), 0)
    c = lax.broadcasted_iota(jnp.int32, (R, R), 1)
    same = (r >> GLA_CHUNK_SHIFT) == (c >> GLA_CHUNK_SHIFT)
    tri = (c <= r) if forward else (c >= r)
    t_mat = jnp.where(same, jnp.where(tri, 1.0, 0.0), 0.0).astype(BF16)
    hi = la.astype(BF16)
    lo = (la - hi.astype(F32)).astype(BF16)
    b = _dot(t_mat, hi) + _dot(t_mat, lo)
    edge = C - 1 if forward else 0
    tot = jnp.concatenate([jnp.broadcast_to(b[g * C + edge:g * C + edge + 1], (C, GLA_KEY_WIDTH))
                           for g in range(G)], axis=0)
    order = list(range(G)) if forward else list(range(G - 1, -1, -1))
    return dict(q_dec=q * jnp.exp(b), k_inv=(k * jnp.exp(-b)).astype(BF16), k_end=k * jnp.exp(tot - b),
                tot=tot, vb=v.astype(BF16), order=order, forward=forward, G=G)


def _gla_scores(prep):
    C, H = GLA_CHUNK, GLA_HEADS
    lane_k = lax.broadcasted_iota(jnp.int32, (C, GLA_KEY_WIDTH), 1)
    qd_heads, scores = {}, {}
    for g in prep["order"]:
        rows = slice(g * C, (g + 1) * C)
        qd = prep["q_dec"][rows]
        qd_heads[g] = jnp.concatenate([jnp.where((lane_k >> GLA_DK_SHIFT) == h, qd, 0.0) for h in range(H)],
                                      axis=0).astype(BF16)
        scores[g] = _dot_nt(qd_heads[g], prep["k_inv"][rows])
    return qd_heads, scores


def _gla_chunk_updates(prep):
    C, H, G = GLA_CHUNK, GLA_HEADS, prep["G"]
    k_end, tot, vb = prep["k_end"], prep["tot"], prep["vb"]
    kv, dec_t = {}, {}
    lane = lax.broadcasted_iota(jnp.int32, (GLA_KEY_WIDTH, 2 * C), 1)
    zeros = jnp.zeros((C, GLA_DV), BF16)
    for p in range(G // 2):
        pair = slice(2 * p * C, (2 * p + 2) * C)
        ke_t = k_end[pair].T.astype(BF16)
        tot_t = tot[pair].T
        swapped = pltpu.roll(tot_t, C, 1)
        for half in range(2):
            g = 2 * p + half
            rows = slice(g * C, (g + 1) * C)
            own = (lane < C) if half == 0 else (lane >= C)
            dec_t[g] = jnp.exp(jnp.where(own, tot_t, swapped))
            parts = []
            for h in range(H):
                v_h = vb[rows, h * GLA_DV:(h + 1) * GLA_DV]
                v_pad = jnp.concatenate([v_h, zeros] if half == 0 else [zeros, v_h], axis=0)
                parts.append(_dot(ke_t[h * C:(h + 1) * C], v_pad))
            kv[g] = jnp.concatenate(parts, axis=0)
    return kv, dec_t


def _gla_states(prep, kv, dec_t, s_ref):
    st = s_ref[...]
    states = {}
    for g in prep["order"]:
        states[g] = st.astype(BF16)
        st = st * dec_t[g] + kv[g]
    s_ref[...] = st
    return states


def _gla_outputs(prep, qd_heads, scores, inter, o_ref):
    C, H = GLA_CHUNK, GLA_HEADS
    row_q = lax.broadcasted_iota(jnp.int32, (H * C, C), 0) & (C - 1)
    col_k = lax.broadcasted_iota(jnp.int32, (H * C, C), 1)
    a_mask = (col_k <= row_q) if prep["forward"] else (col_k >= row_q)
    for g in prep["order"]:
        rows = slice(g * C, (g + 1) * C)
        a = jnp.where(a_mask, scores[g], 0.0).astype(BF16)
        vv = prep["vb"][rows]
        o_ref[rows, :] = jnp.concatenate(
            [_dot(a[h * C:(h + 1) * C], vv[:, h * GLA_DV:(h + 1) * GLA_DV]) + inter[g][h * C:(h + 1) * C]
             for h in range(H)], axis=1).astype(o_ref.dtype)


def _gla_kernel(qf_ref, kf_ref, vf_ref, laf_ref, qb_ref, kb_ref, vb_ref, lab_ref,
                of_ref, ob_ref, sf_ref, sb_ref, *, G):
    @pl.when(pl.program_id(1) == 0)
    def _():
        sf_ref[...] = jnp.zeros_like(sf_ref)
        sb_ref[...] = jnp.zeros_like(sb_ref)

    dirs = [(_gla_decays(qf_ref[...], kf_ref[...], vf_ref[...], laf_ref[...], True, G), sf_ref, of_ref),
            (_gla_decays(qb_ref[...], kb_ref[...], vb_ref[...], lab_ref[...], False, G), sb_ref, ob_ref)]
    scored = [_gla_scores(prep) for prep, _, _ in dirs]
    updates = [_gla_chunk_updates(prep) for prep, _, _ in dirs]
    states = [_gla_states(prep, kv, dec_t, s_ref) for (prep, s_ref, _), (kv, dec_t) in zip(dirs, updates)]
    inters = [{g: _dot(qd_heads[g], st[g]) for g in prep["order"]}
              for (prep, _, _), (qd_heads, _), st in zip(dirs, scored, states)]
    for (prep, _, o_ref), (qd_heads, scores), inter in zip(dirs, scored, inters):
        _gla_outputs(prep, qd_heads, scores, inter, o_ref)


def _gla(gla_slab, loga, B, S, G=8):
    T = B * S
    R = G * GLA_CHUNK
    ns = S // R
    fwd = lambda col: (lambda b, i: (b * ns + i, col))
    bwd = lambda col: (lambda b, i: (b * ns + ns - 1 - i, col))
    kw, vw = GLA_KEY_WIDTH, GLA_VAL_WIDTH
    return pl.pallas_call(
        functools.partial(_gla_kernel, G=G),
        grid=(B, ns),
        in_specs=[
            pl.BlockSpec((R, kw), fwd(0)), pl.BlockSpec((R, kw), fwd(1)),
            pl.BlockSpec((R, vw), fwd(1)), pl.BlockSpec((R, kw), fwd(0)),
            pl.BlockSpec((R, kw), bwd(0)), pl.BlockSpec((R, kw), bwd(1)),
            pl.BlockSpec((R, vw), bwd(1)), pl.BlockSpec((R, kw), bwd(1)),
        ],
        out_specs=[pl.BlockSpec((R, vw), fwd(0)), pl.BlockSpec((R, vw), bwd(0))],
        out_shape=[jax.ShapeDtypeStruct((T, vw), BF16), jax.ShapeDtypeStruct((T, vw), BF16)],
        scratch_shapes=[pltpu.VMEM((kw, GLA_DV), F32), pltpu.VMEM((kw, GLA_DV), F32)],
        compiler_params=_cparams(("arbitrary", "arbitrary")),
        name="gla",
    )(gla_slab, gla_slab, gla_slab, loga, gla_slab, gla_slab, gla_slab, loga)


ATT_CLASSES = 4
ATT_QB = 128
ATT_KB = ATT_QB + 2 * ATT_RADIUS


ATT_UNROLL = (32, 16, 16)


def _att_kernel(q_ref, k_ref, v_ref, o_ref, m_ref, l_ref, bias_ref, *, S):
    QB, KB, NC = ATT_QB, ATT_KB, ATT_CLASSES
    L4 = S // NC
    lane = lax.broadcasted_iota(jnp.int32, (QB, LANES), 1)
    head0 = lane < ATT_HEAD_DIM

    @pl.when((pl.program_id(0) == 0) & (pl.program_id(1) == 0))
    def _():
        rowi = lax.broadcasted_iota(jnp.int32, (2 * QB, KB), 0) & (QB - 1)
        coli = lax.broadcasted_iota(jnp.int32, (2 * QB, KB), 1)
        qpos = (rowi & (QB // NC - 1)) * NC + (rowi >> _log2(QB // NC))
        kpos = (coli & (KB // NC - 1)) * NC + (coli >> _log2(KB // NC))
        for case in range(3):
            bias_ref[0, case] = jnp.where(jnp.abs(rowi - coli + case * ATT_RADIUS) <= ATT_RADIUS, 0.0, NEG_INF)
            bias_ref[1, case] = jnp.where(jnp.abs(qpos - kpos + case * ATT_RADIUS) <= ATT_RADIUS, 0.0, NEG_INF)

    for pi, (_, d) in enumerate(DILATED_PATTERNS):
        L = S // d
        nb = L // QB
        shift = nb.bit_length() - 1
        first = pi == 0
        last = pi == len(DILATED_PATTERNS) - 1

        def scores(n, d=d, L=L, nb=nb, shift=shift):
            cls = n >> shift
            q0 = (n & (nb - 1)) * QB
            ws = jnp.clip(q0 - ATT_RADIUS, 0, L - KB)
            if d == 1:
                qsls = [pl.ds(pl.multiple_of(c * L4 + q0 // NC, QB // NC), QB // NC) for c in range(NC)]
                ksls = [pl.ds(pl.multiple_of(c * L4 + ws // NC, ATT_RADIUS // NC), KB // NC) for c in range(NC)]
            elif d == NC:
                qsls = [pl.ds(pl.multiple_of(cls * L4 + q0, QB), QB)]
                ksls = [pl.ds(pl.multiple_of(cls * L4 + ws, ATT_RADIUS), KB)]
            else:
                base = (cls & (NC - 1)) * L4 + (cls >> _log2(NC))
                qsls = [pl.ds(base + NC * q0, QB, stride=NC)]
                ksls = [pl.ds(base + NC * ws, KB, stride=NC)]
            q = jnp.concatenate([q_ref[sl, :] for sl in qsls], axis=0)
            kw = jnp.concatenate([k_ref[sl, :] for sl in ksls], axis=0)
            kb = kw.astype(BF16)
            bias = bias_ref[1 if d == 1 else 0, (q0 - ws) >> _log2(ATT_RADIUS), :QB]
            q_heads = (jnp.where(head0, q, 0.0), jnp.where(head0, 0.0, q))
            s = [_dot_nt(qh.astype(BF16), kb) + bias for qh in q_heads]
            return qsls, ksls, s

        def softmax_pv(qsls, ksls, s):
            vw = jnp.concatenate([v_ref[sl, :] for sl in ksls], axis=0)
            v_ones = jnp.concatenate([vw.astype(BF16), jnp.ones((KB, LANES), BF16)], axis=1)
            m_h = [jnp.max(t, axis=-1, keepdims=True) for t in s]
            pv = [_dot(jnp.exp2(t - m).astype(BF16), v_ones) for t, m in zip(s, m_h)]
            acc_b = jnp.where(head0, pv[0][:, :LANES], pv[1][:, :LANES])
            m_b = jnp.where(head0, m_h[0], m_h[1])
            l_b = jnp.where(head0, pv[0][:, LANES:], pv[1][:, LANES:])
            return qsls, acc_b, m_b, l_b

        def load(ref, sls):
            return jnp.concatenate([ref[sl, :] for sl in sls], axis=0)

        def store(ref, sls, val):
            n = val.shape[0] // len(sls)
            for i, sl in enumerate(sls):
                ref[sl, :] = val[i * n:(i + 1) * n]

        unroll = ATT_UNROLL[pi]

        def body(n, carry, first=first, last=last, unroll=unroll):
            staged = [scores(n * unroll + u) for u in range(unroll)]
            blocks = [softmax_pv(*st) for st in staged]
            for qsls, acc_b, m_b, l_b in blocks:
                if first:
                    acc, m_new, l_new = acc_b, m_b, l_b
                else:
                    m_old = load(m_ref, qsls)
                    m_new = jnp.maximum(m_old, m_b)
                    w_old = jnp.exp2(m_old - m_new)
                    w_blk = jnp.exp2(m_b - m_new)
                    acc = load(o_ref, qsls) * w_old + acc_b * w_blk
                    l_new = load(l_ref, qsls) * w_old + l_b * w_blk
                if last:
                    store(o_ref, qsls, acc / l_new)
                else:
                    store(o_ref, qsls, acc)
                    store(m_ref, qsls, m_new)
                    store(l_ref, qsls, l_new)
            return carry

        lax.fori_loop(0, S // (QB * unroll), body, 0)


def _attention(att_slab, B, S):
    T = B * S
    ncol = ATT_WIDTH // LANES
    return pl.pallas_call(
        functools.partial(_att_kernel, S=S),
        grid=(B, ncol),
        in_specs=[
            pl.BlockSpec((S, LANES), lambda b, h: (b, h)),
            pl.BlockSpec((S, LANES), lambda b, h: (b, ncol + h)),
            pl.BlockSpec((S, LANES), lambda b, h: (b, 2 * ncol + h)),
        ],
        out_specs=pl.BlockSpec((S, LANES), lambda b, h: (b, h)),
        out_shape=jax.ShapeDtypeStruct((T, ATT_WIDTH), F32),
        scratch_shapes=[pltpu.VMEM((S, LANES), F32), pltpu.VMEM((S, LANES), F32),
                        pltpu.VMEM((2, 3, 2 * ATT_QB, ATT_KB), F32)],
        compiler_params=_cparams(("arbitrary", "arbitrary")),
        name="dilated_attention",
    )(att_slab, att_slab, att_slab)


PACK_WORDS = D_MODEL // 2
ROW_TILE = PACK_WORDS // LANES
HIGH_HALF = -65536


def _pack_rows(x):
    bits = lambda v: lax.bitcast_convert_type(v.astype(BF16).astype(F32), jnp.int32)
    low = (bits(x[:, :PACK_WORDS]) >> 16) & 0xFFFF
    return (bits(x[:, PACK_WORDS:]) & HIGH_HALF) | low


def _unpack_rows(w):
    low = lax.bitcast_convert_type(w << 16, F32)
    high = lax.bitcast_convert_type(w & HIGH_HALF, F32)
    return jnp.concatenate([low, high], axis=1).astype(BF16)


def _to_row_tiles(ref, w):
    n = w.shape[0]
    for j in range(ROW_TILE):
        ref[pl.ds(j, n, stride=ROW_TILE), :] = w[:, j * LANES:(j + 1) * LANES]


def _from_row_tiles(ref, n):
    return jnp.concatenate([ref[pl.ds(j, n, stride=ROW_TILE), :] for j in range(ROW_TILE)], axis=1)


def _tile_copy(src_ref, src_row, dst_ref, dst_row, sem):
    src = pl.ds(pl.multiple_of(src_row * ROW_TILE, ROW_TILE), ROW_TILE)
    dst = pl.ds(pl.multiple_of(dst_row * ROW_TILE, ROW_TILE), ROW_TILE)
    return pltpu.make_async_copy(src_ref.at[src], dst_ref.at[dst], sem)


def _outproj_kernel(of_ref, ob_ref, gg_ref, att_ref, x_ref, gnw_ref, wo1_ref, wo2_ref,
                    n2_ref, wr_ref, br_ref, h_ref, u_ref, lg_ref, stage_ref):
    rows = stage_ref.shape[1] // ATT_CLASSES
    for j in range(ATT_WIDTH // LANES):
        for c in range(ATT_CLASSES):
            stage_ref[j, pl.ds(c, rows, stride=ATT_CLASSES), :] = att_ref[c, :, j * LANES:(j + 1) * LANES]
    att = jnp.concatenate([stage_ref[j] for j in range(ATT_WIDTH // LANES)], axis=1)
    o = of_ref[...].astype(F32) + ob_ref[...].astype(F32)
    gate = gg_ref[...].astype(F32)
    gnw = gnw_ref[...]
    parts = []
    for h in range(GLA_HEADS):
        sl = slice(h * GLA_DV, (h + 1) * GLA_DV)
        parts.append(_rms(o[:, sl], gnw))
    y = jnp.concatenate(parts, axis=1) * (gate / (1.0 + jnp.exp(-gate)))
    mix = _dot(y.astype(BF16), wo1_ref[...]) + _dot(att.astype(BF16), wo2_ref[...])
    h = x_ref[...] + mix
    h_ref[...] = h
    u = _rms(h, n2_ref[...])
    _to_row_tiles(u_ref, _pack_rows(u))
    u_hi = u.astype(BF16)
    u_lo = (u - u_hi.astype(F32)).astype(BF16)
    hi_both = _dot_nt(wr_ref[...], u_hi)
    lg_ref[...] = (hi_both[:LANES] + hi_both[LANES:] + _dot_nt(wr_ref[:LANES], u_lo)) + br_ref[...]


def _outproj(o_f, o_b, gate, att_out, x2, gla_norm_w, w_out, norm2_w, wr, br, tm=512):
    T = x2.shape[0]
    nS = att_out.shape[2] * ATT_CLASSES // tm
    row = lambda i: (i, 0)
    const = lambda i: (0, 0)
    wo = w_out.astype(BF16)
    wr_hi = wr.astype(BF16)
    wr_lo = (wr - wr_hi.astype(F32)).astype(BF16)
    wr = jnp.concatenate([wr_hi, wr_lo], axis=0)
    return pl.pallas_call(
        _outproj_kernel,
        grid=(T // tm,),
        in_specs=[
            pl.BlockSpec((tm, GLA_VAL_WIDTH), row),
            pl.BlockSpec((tm, GLA_VAL_WIDTH), row),
            pl.BlockSpec((tm, GLA_VAL_WIDTH), row),
            pl.BlockSpec((None, ATT_CLASSES, tm // ATT_CLASSES, ATT_WIDTH), lambda i: (i // nS, 0, i % nS, 0)),
            pl.BlockSpec((tm, D_MODEL), row),
            pl.BlockSpec((1, GLA_DV), const),
            pl.BlockSpec((GLA_VAL_WIDTH, D_MODEL), const),
            pl.BlockSpec((ATT_WIDTH, D_MODEL), lambda i: (GLA_VAL_WIDTH // ATT_WIDTH, 0)),
            pl.BlockSpec((1, D_MODEL), const),
            pl.BlockSpec((2 * LANES, D_MODEL), const),
            pl.BlockSpec((LANES, 1), const),
        ],
        out_specs=[
            pl.BlockSpec((tm, D_MODEL), row),
            pl.BlockSpec((tm * ROW_TILE, LANES), row),
            pl.BlockSpec((LANES, tm), lambda i: (0, i)),
        ],
        out_shape=[
            jax.ShapeDtypeStruct((T, D_MODEL), F32),
            jax.ShapeDtypeStruct((T * ROW_TILE, LANES), jnp.int32),
            jax.ShapeDtypeStruct((LANES, T), F32),
        ],
        scratch_shapes=[pltpu.VMEM((ATT_WIDTH // LANES, tm, LANES), F32)],
        compiler_params=_cparams(("arbitrary",)),
        name="outproj",
    )(o_f, o_b, gate, att_out, x2, gla_norm_w[None, :], wo, wo,
      norm2_w[None, :], wr, br)


INFO_E1, INFO_E2, INFO_R1, INFO_R2, INFO_W1, INFO_W2 = range(6)
ROUTE_ROWS = 40


def _route_kernel(lg_ref, info_ref, cnt_ref, carry_ref):
    @pl.when(pl.program_id(0) == 0)
    def _():
        carry_ref[...] = jnp.zeros_like(carry_ref)

    lg = lg_ref[:ROUTE_ROWS, :]
    tr = lg.shape[1]
    row = lax.broadcasted_iota(jnp.int32, (ROUTE_ROWS, tr), 0)
    big = jnp.int32(1 << 20)
    is_g = (row >= MOE_N_EXPERTS) & (row < MOE_N_EXPERTS + MOE_GROUPS)
    gl = jnp.where(is_g, lg, -jnp.inf)
    gmax = jnp.max(gl, axis=0, keepdims=True)
    gsel = jnp.min(jnp.where(gl == gmax, row - MOE_N_EXPERTS, big), axis=0, keepdims=True)
    g_w = 1.0 / jnp.sum(jnp.where(is_g, jnp.exp(lg - gmax), 0.0), axis=0, keepdims=True)
    in_grp = (row < MOE_N_EXPERTS) & ((row >> MOE_GROUP_SHIFT) == gsel)
    el = jnp.where(in_grp, lg, -jnp.inf)
    v1 = jnp.max(el, axis=0, keepdims=True)
    i1 = jnp.min(jnp.where(el == v1, row, big), axis=0, keepdims=True)
    el2 = jnp.where(row == i1, -jnp.inf, el)
    v2 = jnp.max(el2, axis=0, keepdims=True)
    i2 = jnp.min(jnp.where(el2 == v2, row, big), axis=0, keepdims=True)
    t = jnp.exp(v2 - v1)
    w1 = g_w * (1.0 / (1.0 + t))
    w2 = g_w * (t / (1.0 + t))

    erow = lax.broadcasted_iota(jnp.int32, (MOE_N_EXPERTS, tr), 0)
    hit1 = erow == i1
    hit2 = erow == i2
    member = jnp.where(hit1 | hit2, 1.0, 0.0)
    r = lax.broadcasted_iota(jnp.int32, (tr, tr), 0)
    c = lax.broadcasted_iota(jnp.int32, (tr, tr), 1)
    earlier = jnp.where(r < c, 1.0, 0.0).astype(BF16)
    carry = carry_ref[...]
    prefix = _dot(member.astype(BF16), earlier) + carry[:, 0:1]
    rank1 = jnp.sum(jnp.where(hit1, prefix, 0.0), axis=0, keepdims=True)
    rank2 = jnp.sum(jnp.where(hit2, prefix, 0.0), axis=0, keepdims=True)
    carry = carry + jnp.sum(member, axis=1, keepdims=True)
    carry_ref[...] = carry
    cnt_ref[...] = carry

    zero = jnp.zeros_like(w1)
    info_ref[...] = jnp.concatenate([i1.astype(F32), i2.astype(F32), rank1, rank2, w1, w2, zero, zero], axis=0)


def _route(logits_t, tr=1024):
    T = logits_t.shape[1]
    return pl.pallas_call(
        _route_kernel,
        grid=(T // tr,),
        in_specs=[pl.BlockSpec((LANES, tr), lambda i: (0, i))],
        out_specs=[pl.BlockSpec((8, tr), lambda i: (0, i)),
                   pl.BlockSpec((MOE_N_EXPERTS, LANES), lambda i: (0, 0))],
        out_shape=[jax.ShapeDtypeStruct((8, T), F32), jax.ShapeDtypeStruct((MOE_N_EXPERTS, LANES), F32)],
        scratch_shapes=[pltpu.VMEM((MOE_N_EXPERTS, LANES), F32)],
        compiler_params=_cparams(("arbitrary",)),
        name="route",
    )(logits_t)


ROW_UNROLL = 16


def _dispatch_kernel(dest_ref, pend_ref, u_ref, xs_ref, zbuf, sem, zsem, *, td, T, nblk):
    @pl.when(pl.program_id(0) == 0)
    def _():
        zbuf[...] = jnp.zeros_like(zbuf)
        n_used = pend_ref[MOE_N_EXPERTS - 1] >> MOE_ROWS_SHIFT

        def zero_copy(blk):
            start = pl.multiple_of(blk * (MOE_ROWS * ROW_TILE), MOE_ROWS * ROW_TILE)
            return pltpu.make_async_copy(zbuf, xs_ref.at[pl.ds(start, MOE_ROWS * ROW_TILE)], zsem)

        def each_pad_block(fn):
            def per_expert(e, carry):
                prev = jnp.where(e > 0, pend_ref[jnp.maximum(e - 1, 0)], 0)

                @pl.when(pend_ref[e] > prev)
                def _():
                    fn((pend_ref[e] >> MOE_ROWS_SHIFT) - 1)
                return carry

            def per_tail(j, carry):
                @pl.when(n_used + j < nblk)
                def _():
                    fn(n_used + j)
                return carry

            lax.fori_loop(0, MOE_N_EXPERTS, per_expert, 0)
            lax.fori_loop(0, MOE_N_EXPERTS, per_tail, 0)

        each_pad_block(lambda blk: zero_copy(blk).start())
        each_pad_block(lambda blk: zero_copy(blk).wait())

    base = pl.program_id(0) * td

    def issue(g, carry):
        for j in range(ROW_UNROLL):
            r = g * ROW_UNROLL + j
            for k in range(MOE_TOP_K):
                _tile_copy(u_ref, r, xs_ref, dest_ref[k * T + base + r], sem).start(priority=k)
        return carry

    lax.fori_loop(0, td // ROW_UNROLL, issue, 0)
    for k in range(MOE_TOP_K):
        pltpu.make_async_copy(u_ref, xs_ref.at[pl.ds(0, td * ROW_TILE)], sem).wait()


def _dispatch(dest, pend, u2, cap, td=2048):
    T = u2.shape[0] // ROW_TILE
    return pl.pallas_call(
        functools.partial(_dispatch_kernel, td=td, T=T, nblk=cap // MOE_ROWS),
        grid_spec=pltpu.PrefetchScalarGridSpec(
            num_scalar_prefetch=2,
            grid=(T // td,),
            in_specs=[pl.BlockSpec((td * ROW_TILE, LANES), lambda i, d, z: (i, 0))],
            out_specs=pl.BlockSpec(memory_space=pl.ANY),
            scratch_shapes=[pltpu.VMEM((MOE_ROWS * ROW_TILE, LANES), jnp.int32),
                            pltpu.SemaphoreType.DMA(()), pltpu.SemaphoreType.DMA(())],
        ),
        out_shape=jax.ShapeDtypeStruct((cap * ROW_TILE, LANES), jnp.int32),
        compiler_params=_cparams(("arbitrary",)),
        name="dispatch",
    )(dest, pend, u2)


EXPERT_PAIR = 2


def _expert_kernel(pend_ref, xs_hbm, wg_hbm, wu_hbm, wd_hbm, ys_hbm,
                   xbuf, ybuf, zbuf, stage_g, stage_u, stage_d, wgb, wub, wdb, xsem, ysem, wsem, zsem, *, nblk):
    last = MOE_N_EXPERTS - 1
    n_used = pend_ref[last] >> MOE_ROWS_SHIFT
    n_pairs = (n_used + 1) >> 1
    block_rows = MOE_ROWS * ROW_TILE

    def rows_of(b):
        return pl.ds(pl.multiple_of(b * block_rows, block_rows), block_rows)

    def x_copy(b, slot):
        return pltpu.make_async_copy(xs_hbm.at[rows_of(b)], xbuf.at[slot], xsem.at[slot])

    def y_copy(b, slot):
        return pltpu.make_async_copy(ybuf.at[slot], ys_hbm.at[rows_of(b)], ysem.at[slot])

    def zero_copy(b):
        return pltpu.make_async_copy(zbuf, ys_hbm.at[rows_of(b)], zsem)

    def weight_copies(e):
        return (pltpu.make_async_copy(wg_hbm.at[e], stage_g, wsem.at[0]),
                pltpu.make_async_copy(wu_hbm.at[e], stage_u, wsem.at[1]),
                pltpu.make_async_copy(wd_hbm.at[e], stage_d, wsem.at[2]))

    def owner(start, row):
        return lax.while_loop(lambda e: (e < last) & (pend_ref[e] <= row), lambda e: e + 1, start)

    for c in weight_copies(owner(0, 0)):
        c.start()
    for i in range(EXPERT_PAIR):
        x_copy(i, i).start()

    zbuf[...] = jnp.zeros_like(zbuf)

    def tail(fn):
        def step(b, carry):
            fn(b)
            return carry
        lax.fori_loop(n_pairs * EXPERT_PAIR, nblk, step, 0)

    tail(lambda b: zero_copy(b).start())

    def body(p, carry):
        cur, run = carry
        half = (p & 1) * EXPERT_PAIR
        for i in range(EXPERT_PAIR):
            x_copy(p * EXPERT_PAIR + i, half + i).wait()

        @pl.when(p + 1 < n_pairs)
        def _():
            for i in range(EXPERT_PAIR):
                x_copy((p + 1) * EXPERT_PAIR + i, EXPERT_PAIR - half + i).start()

        slots = []
        for i in range(EXPERT_PAIR):
            b = p * EXPERT_PAIR + i
            e = jnp.where(b < n_used, owner(jnp.maximum(cur, 0), b * MOE_ROWS), cur)
            fresh = e != cur
            run = run + fresh.astype(jnp.int32)
            slot = run & 1

            @pl.when(fresh)
            def _(e=e, slot=slot):
                for c in weight_copies(e):
                    c.wait()
                wgb[slot] = stage_g[...].astype(BF16)
                wub[slot] = stage_u[...].astype(BF16)
                wdb[slot] = stage_d[...].astype(BF16)

                @pl.when(pend_ref[e] < pend_ref[last])
                def _():
                    for c in weight_copies(owner(e + 1, pend_ref[e])):
                        c.start(priority=1)

            cur = e
            slots.append(slot)

        @pl.when(p >= 2)
        def _():
            for i in range(EXPERT_PAIR):
                y_copy((p - 2) * EXPERT_PAIR + i, half + i).wait()

        for i in range(EXPERT_PAIR):
            xb = _unpack_rows(_from_row_tiles(xbuf.at[half + i], MOE_ROWS))
            g = _dot(xb, wgb[slots[i]])
            u = _dot(xb, wub[slots[i]])
            hid = (g / (1.0 + jnp.exp(-g))) * u
            _to_row_tiles(ybuf.at[half + i], _pack_rows(_dot(hid.astype(BF16), wdb[slots[i]])))
        for i in range(EXPERT_PAIR):
            y_copy(p * EXPERT_PAIR + i, half + i).start()
        return cur, run

    lax.fori_loop(0, n_pairs, body, (jnp.int32(-1), jnp.int32(-1)))

    def drain(p):
        for i in range(EXPERT_PAIR):
            y_copy(p * EXPERT_PAIR + i, (p & 1) * EXPERT_PAIR + i).wait()

    @pl.when(n_pairs >= 2)
    def _():
        drain(n_pairs - 2)
    drain(n_pairs - 1)
    tail(lambda b: zero_copy(b).wait())


def _experts(pend, xs, w_gate, w_up, w_down):
    cap = xs.shape[0] // ROW_TILE
    nblk = cap // MOE_ROWS
    assert nblk % EXPERT_PAIR == 0
    block = (MOE_ROWS * ROW_TILE, LANES)
    anywhere = pl.BlockSpec(memory_space=pl.ANY)
    return pl.pallas_call(
        functools.partial(_expert_kernel, nblk=nblk),
        grid_spec=pltpu.PrefetchScalarGridSpec(
            num_scalar_prefetch=1,
            grid=(1,),
            in_specs=[anywhere, anywhere, anywhere, anywhere],
            out_specs=anywhere,
            scratch_shapes=[pltpu.VMEM((2 * EXPERT_PAIR,) + block, jnp.int32),
                            pltpu.VMEM((2 * EXPERT_PAIR,) + block, jnp.int32),
                            pltpu.VMEM(block, jnp.int32),
                            pltpu.VMEM((D_MODEL, MOE_D_FF), F32),
                            pltpu.VMEM((D_MODEL, MOE_D_FF), F32),
                            pltpu.VMEM((MOE_D_FF, D_MODEL), F32),
                            pltpu.VMEM((2, D_MODEL, MOE_D_FF), BF16),
                            pltpu.VMEM((2, D_MODEL, MOE_D_FF), BF16),
                            pltpu.VMEM((2, MOE_D_FF, D_MODEL), BF16),
                            pltpu.SemaphoreType.DMA((2 * EXPERT_PAIR,)),
                            pltpu.SemaphoreType.DMA((2 * EXPERT_PAIR,)),
                            pltpu.SemaphoreType.DMA((3,)),
                            pltpu.SemaphoreType.DMA(())],
        ),
        out_shape=jax.ShapeDtypeStruct((cap * ROW_TILE, LANES), jnp.int32),
        compiler_params=_cparams(("arbitrary",)),
        name="experts",
    )(pend, xs, w_gate, w_up, w_down)


def _combine_kernel(dest_ref, ys_ref, info_ref, h_ref, fw_ref, o_ref, buf, sem, *, tc, T):
    i = pl.program_id(0)
    n = pl.num_programs(0)

    def issue(step, slot):
        base = step * tc

        def body(g, carry):
            for j in range(ROW_UNROLL):
                r = g * ROW_UNROLL + j
                for k in range(MOE_TOP_K):
                    _tile_copy(ys_ref, dest_ref[k * T + base + r], buf.at[slot, k], r,
                               sem.at[slot]).start(priority=k)
            return carry

        lax.fori_loop(0, tc // ROW_UNROLL, body, 0)

    @pl.when(i == 0)
    def _():
        issue(0, 0)

    slot = i % 2

    @pl.when(i + 1 < n)
    def _():
        issue(i + 1, 1 - slot)

    for k in range(MOE_TOP_K):
        pltpu.make_async_copy(ys_ref.at[pl.ds(0, tc * ROW_TILE)], buf.at[slot, k], sem.at[slot]).wait()

    info_t = jnp.concatenate([info_ref[...]] * (LANES // 8), axis=0).T
    w1 = info_t[:, INFO_W1:INFO_W1 + 1]
    w2 = info_t[:, INFO_W2:INFO_W2 + 1]
    y1 = _unpack_rows(_from_row_tiles(buf.at[slot, 0], tc)).astype(F32)
    y2 = _unpack_rows(_from_row_tiles(buf.at[slot, 1], tc)).astype(F32)
    h = h_ref[...] + (y1 * w1 + y2 * w2)
    o_ref[...] = _rms(h, fw_ref[...])


def _combine(dest, ys, info, h, final_w, tc=512):
    T = h.shape[0]
    return pl.pallas_call(
        functools.partial(_combine_kernel, tc=tc, T=T),
        grid_spec=pltpu.PrefetchScalarGridSpec(
            num_scalar_prefetch=1,
            grid=(T // tc,),
            in_specs=[pl.BlockSpec(memory_space=pl.ANY),
                      pl.BlockSpec((8, tc), lambda i, d: (0, i)),
                      pl.BlockSpec((tc, D_MODEL), lambda i, d: (i, 0)),
                      pl.BlockSpec((1, D_MODEL), lambda i, d: (0, 0))],
            out_specs=pl.BlockSpec((tc, D_MODEL), lambda i, d: (i, 0)),
            scratch_shapes=[pltpu.VMEM((2, MOE_TOP_K, tc * ROW_TILE, LANES), jnp.int32),
                            pltpu.SemaphoreType.DMA((2,))],
        ),
        out_shape=jax.ShapeDtypeStruct((T, D_MODEL), F32),
        compiler_params=_cparams(("arbitrary",)),
        name="combine",
    )(dest, ys, info, h, final_w[None, :])


def _plan_kernel(info_ref, cnt_ref, dest_ref, pend_ref):
    cnt = cnt_ref[...].astype(jnp.int32)
    nblk_e = ((cnt + (MOE_ROWS - 1)) >> MOE_ROWS_SHIFT).astype(F32)
    r = lax.broadcasted_iota(jnp.int32, (MOE_N_EXPERTS, MOE_N_EXPERTS), 0)
    c = lax.broadcasted_iota(jnp.int32, (MOE_N_EXPERTS, MOE_N_EXPERTS), 1)
    before = jnp.where(c < r, 1.0, 0.0).astype(BF16)
    first_blk = _dot(before, nblk_e.astype(BF16))
    pstart = first_blk[:, 0:1] * float(MOE_ROWS)
    pend_ref[...] = ((first_blk + nblk_e) * float(MOE_ROWS)).astype(jnp.int32)

    info = info_ref[...]
    erow = lax.broadcasted_iota(jnp.int32, (MOE_N_EXPERTS, info.shape[1]), 0)
    start_of = lambda e: jnp.sum(jnp.where(erow == e.astype(jnp.int32), pstart, 0.0), axis=0, keepdims=True)
    d1 = info[INFO_R1:INFO_R1 + 1] + start_of(info[INFO_E1:INFO_E1 + 1])
    d2 = info[INFO_R2:INFO_R2 + 1] + start_of(info[INFO_E2:INFO_E2 + 1])
    zero = jnp.zeros_like(d1)
    dest_ref[...] = jnp.concatenate([d1, d2] + [zero] * 6, axis=0).astype(jnp.int32)


def _plan(info, counts, tr=2048):
    T = info.shape[1]
    dest8, pend = pl.pallas_call(
        _plan_kernel,
        grid=(T // tr,),
        in_specs=[pl.BlockSpec((8, tr), lambda i: (0, i)),
                  pl.BlockSpec((MOE_N_EXPERTS, LANES), lambda i: (0, 0))],
        out_specs=[pl.BlockSpec((8, tr), lambda i: (0, i)),
                   pl.BlockSpec((MOE_N_EXPERTS, LANES), lambda i: (0, 0))],
        out_shape=[jax.ShapeDtypeStruct((8, T), jnp.int32),
                   jax.ShapeDtypeStruct((MOE_N_EXPERTS, LANES), jnp.int32)],
        compiler_params=_cparams(("arbitrary",)),
        name="plan",
    )(info, counts)
    return dest8[:MOE_TOP_K].reshape(-1), pend[:, 0]


def _moe_capacity(T):
    return (-(-(T * MOE_TOP_K) // MOE_ROWS) + MOE_N_EXPERTS) * MOE_ROWS


def _router_weights(router_group_w, router_group_b, router_expert_w, router_expert_b):
    we = jnp.transpose(router_expert_w, (0, 2, 1)).reshape(MOE_N_EXPERTS, D_MODEL)
    pad = LANES - MOE_N_EXPERTS - MOE_GROUPS
    wr = jnp.concatenate([we, router_group_w.T, jnp.zeros((pad, D_MODEL), F32)], axis=0)
    br = jnp.concatenate([router_expert_b.reshape(-1), router_group_b, jnp.zeros((pad,), F32)])[:, None]
    return wr, br


def kernel(x, norm1_w, w_in, gla_fwd_gate_w, gla_fwd_gate_b, gla_bwd_gate_w, gla_bwd_gate_b,
           gla_norm_w, w_out, norm2_w, router_group_w, router_group_b, router_expert_w,
           router_expert_b, expert_w_gate, expert_w_up, expert_w_down, final_norm_w):
    B, S, D = x.shape
    T = B * S
    assert norm1_w.shape[0] == 1, "single-layer trunk: the final norm is fused into the combine step"
    h = x.reshape(T, D)
    gla_slab, gate, loga, att_slab = _inproj(h, S, norm1_w[0], w_in[0], gla_fwd_gate_w[0], gla_fwd_gate_b[0],
                                       gla_bwd_gate_w[0], gla_bwd_gate_b[0])
    o_f, o_b = _gla(gla_slab, loga, B, S)
    att_out = _attention(att_slab.reshape(T, 3 * ATT_WIDTH), B, S)
    att_out = att_out.reshape(B, ATT_CLASSES, S // ATT_CLASSES, ATT_WIDTH)
    wr, br = _router_weights(router_group_w[0], router_group_b[0], router_expert_w[0], router_expert_b[0])
    h, u2, logits = _outproj(o_f, o_b, gate, att_out, h, gla_norm_w[0], w_out[0], norm2_w[0], wr, br)
    info, counts = _route(logits)
    dest, pend = _plan(info, counts)
    xs = _dispatch(dest, pend, u2, _moe_capacity(T))
    ys = _experts(pend, xs, expert_w_gate[0], expert_w_up[0], expert_w_down[0])
    out = _combine(dest, ys, info, h, final_norm_w)
    return out.reshape(B, S, D)
```

```python
import functools

import jax
import jax.numpy as jnp
import numpy as np
from jax import lax
from jax.experimental import pallas as pl
from jax.experimental.pallas import tpu as pltpu

F32 = jnp.float32
BF16 = jnp.bfloat16

D_MODEL = 1024
GLA_HEADS = 4
GLA_DV = 128
GLA_DK = 64
GLA_KEY_WIDTH = GLA_HEADS * GLA_DK
GLA_VAL_WIDTH = GLA_HEADS * GLA_DV
GLA_GATE_RANK = 16
GLA_TAU = 16.0
GLA_CHUNK = 64
ATT_WIDTH = 512
ATT_HEAD_DIM = 64
ATT_HEADS = 8
ROT_DIM = 16
ROPE_THETA = 500000.0
DILATED_PATTERNS = ((128, 1), (512, 4), (2048, 16))
ATT_RADIUS = 64
MOE_GROUPS = 4
MOE_EXPERTS_PER_GROUP = 8
MOE_N_EXPERTS = 32
MOE_TOP_K = 2
MOE_D_FF = 512
EPS = 1e-6
NEG_INF = -1e30
LOG2E = 1.4426950408889634

LANES = 128
MOE_ROWS = 256


def _log2(n):
    assert n & (n - 1) == 0, n
    return n.bit_length() - 1


GLA_CHUNK_SHIFT = _log2(GLA_CHUNK)
GLA_DK_SHIFT = _log2(GLA_DK)
MOE_ROWS_SHIFT = _log2(MOE_ROWS)
MOE_GROUP_SHIFT = _log2(MOE_EXPERTS_PER_GROUP)
VMEM_LIMIT = 56 * 1024 * 1024


def _cparams(sem):
    return pltpu.CompilerParams(dimension_semantics=sem, vmem_limit_bytes=VMEM_LIMIT)


def _dot(a, b):
    return jnp.dot(a, b, preferred_element_type=F32)


def _dot_nt(a, b):
    return lax.dot_general(a, b, (((1,), (1,)), ((), ())), preferred_element_type=F32)


def _dot_tn(a, b):
    return lax.dot_general(a, b, (((0,), (0,)), ((), ())), preferred_element_type=F32)


def _rms(x, w):
    return x * lax.rsqrt(jnp.mean(x * x, axis=-1, keepdims=True) + EPS) * w


def _inproj_kernel(x_ref, n1_ref, wg_ref, wlr_ref, wa_ref, gw_ref, gb_ref,
                   rc_ref, rs1_ref, rs2_ref, gla_ref, gate_ref, loga_ref, att_ref, stage_ref, wgb, wlrb):
    @pl.when(pl.program_id(0) == 0)
    def _():
        wgb[...] = wg_ref[...].astype(BF16)
        wlrb[...] = wlr_ref[...].astype(BF16)

    x = x_ref[...]
    ub = _rms(x, n1_ref[...]).astype(BF16)
    g = _dot(ub, wgb[...])
    qkv = 2 * GLA_KEY_WIDTH + GLA_VAL_WIDTH
    gla_ref[:, :GLA_KEY_WIDTH] = g[:, :GLA_KEY_WIDTH] * (GLA_DK ** -0.5)
    gla_ref[:, GLA_KEY_WIDTH:] = g[:, GLA_KEY_WIDTH:qkv]
    gate_ref[...] = g[:, qkv:].astype(BF16)
    lr = _dot(ub, wlrb[...])
    gate = _dot(lr.astype(BF16), gw_ref[...]) + gb_ref[...]
    loga_ref[...] = (jnp.minimum(gate, 0.0) - jnp.log(1.0 + jnp.exp(-jnp.abs(gate)))) * (1.0 / GLA_TAU)
    a = _dot(ub, wa_ref[...])
    qk = a[:, :2 * ATT_WIDTH]
    reps = 2 * ATT_WIDTH // LANES
    c = jnp.concatenate([rc_ref[...]] * reps, axis=1)
    s1 = jnp.concatenate([rs1_ref[...]] * reps, axis=1)
    s2 = jnp.concatenate([rs2_ref[...]] * reps, axis=1)
    half = ROT_DIM // 2
    n = 2 * ATT_WIDTH
    roped = qk * c + pltpu.roll(qk, n - half, 1) * s1 + pltpu.roll(qk, half, 1) * s2
    qkv = jnp.concatenate([roped[:, :ATT_WIDTH] * (ATT_HEAD_DIM ** -0.5 * LOG2E), roped[:, ATT_WIDTH:],
                           a[:, 2 * ATT_WIDTH:]], axis=1)
    rows = x.shape[0] // ATT_CLASSES
    for j in range(3 * ATT_WIDTH // LANES):
        cols = slice(j * LANES, (j + 1) * LANES)
        stage_ref[j] = qkv[:, cols]
        for c in range(ATT_CLASSES):
            att_ref[c, :, cols] = stage_ref[j, pl.ds(c, rows, stride=ATT_CLASSES), :]


def _rope_lane_tables(S):
    half = ROT_DIM // 2
    inv = np.float32(ROPE_THETA) ** (-(np.arange(0, ROT_DIM, 2, dtype=np.float32) / np.float32(ROT_DIM)))
    ang = np.arange(S, dtype=np.float32)[:, None] * inv[None, :].astype(np.float32)
    cos, sin = np.cos(ang), np.sin(ang)
    ones = np.ones((S, ATT_HEAD_DIM - ROT_DIM), np.float32)
    zeros = np.zeros((S, ATT_HEAD_DIM - ROT_DIM), np.float32)
    zeros8 = np.zeros((S, half), np.float32)
    rep = LANES // ATT_HEAD_DIM
    c = np.tile(np.concatenate([cos, cos, ones], axis=1), (1, rep))
    s1 = np.tile(np.concatenate([-sin, zeros8, zeros], axis=1), (1, rep))
    s2 = np.tile(np.concatenate([zeros8, sin, zeros], axis=1), (1, rep))
    return jnp.asarray(c), jnp.asarray(s1), jnp.asarray(s2)


def _inproj(x2, S, norm1_w, w_in, wf, bfw, wb, bbw, tm=512):
    T = x2.shape[0]
    o_lr = 2 * GLA_KEY_WIDTH + 2 * GLA_VAL_WIDTH
    o_att = o_lr + 2 * GLA_GATE_RANK
    wa = w_in[:, o_att:].astype(BF16)
    zeros = jnp.zeros((GLA_GATE_RANK, GLA_KEY_WIDTH), F32)
    gw = jnp.concatenate([jnp.concatenate([wf, zeros], axis=1), jnp.concatenate([zeros, wb], axis=1),
                          jnp.zeros((LANES - 2 * GLA_GATE_RANK, 2 * GLA_KEY_WIDTH), F32)], axis=0).astype(BF16)
    gb = jnp.concatenate([bfw, bbw])[None, :]
    rc, rs1, rs2 = _rope_lane_tables(S)
    nS = S // tm
    row = lambda i: (i, 0)
    const = lambda i: (0, 0)
    pos = lambda i: (i % nS, 0)
    return pl.pallas_call(
        _inproj_kernel,
        grid=(T // tm,),
        in_specs=[
            pl.BlockSpec((tm, D_MODEL), row),
            pl.BlockSpec((1, D_MODEL), const),
            pl.BlockSpec((D_MODEL, o_lr), const),
            pl.BlockSpec((D_MODEL, LANES), lambda i: (0, o_lr // LANES)),
            pl.BlockSpec((D_MODEL, 3 * ATT_WIDTH), const),
            pl.BlockSpec((LANES, 2 * GLA_KEY_WIDTH), const),
            pl.BlockSpec((1, 2 * GLA_KEY_WIDTH), const),
            pl.BlockSpec((tm, LANES), pos),
            pl.BlockSpec((tm, LANES), pos),
            pl.BlockSpec((tm, LANES), pos),
        ],
        out_specs=[
            pl.BlockSpec((tm, o_lr - GLA_VAL_WIDTH), row),
            pl.BlockSpec((tm, GLA_VAL_WIDTH), row),
            pl.BlockSpec((tm, 2 * GLA_KEY_WIDTH), row),
            pl.BlockSpec((None, ATT_CLASSES, tm // ATT_CLASSES, 3 * ATT_WIDTH),
                         lambda i: (i // nS, 0, i % nS, 0)),
        ],
        out_shape=[
            jax.ShapeDtypeStruct((T, o_lr - GLA_VAL_WIDTH), F32),
            jax.ShapeDtypeStruct((T, GLA_VAL_WIDTH), BF16),
            jax.ShapeDtypeStruct((T, 2 * GLA_KEY_WIDTH), F32),
            jax.ShapeDtypeStruct((T // S, ATT_CLASSES, S // ATT_CLASSES, 3 * ATT_WIDTH), F32),
        ],
        scratch_shapes=[pltpu.VMEM((3 * ATT_WIDTH // LANES, tm, LANES), F32),
                        pltpu.VMEM((D_MODEL, o_lr), BF16), pltpu.VMEM((D_MODEL, LANES), BF16)],
        compiler_params=_cparams(("arbitrary",)),
        name="inproj",
    )(x2, norm1_w[None, :], w_in, w_in, wa, gw, gb, rc, rs1, rs2)


def _gla_decays(q, k, v, la, forward, G):
    C = GLA_CHUNK
    R = G * C
    r = lax.broadcasted_iota(jnp.int32, (R, R), 0)
    c = lax.broadcasted_iota(jnp.int32, (R, R), 1)
    same = (r >> GLA_CHUNK_SHIFT) == (c >> GLA_CHUNK_SHIFT)
    tri = (c <= r) if forward else (c >= r)
    t_mat = jnp.where(same, jnp.where(tri, 1.0, 0.0), 0.0).astype(BF16)
    hi = la.astype(BF16)
    lo = (la - hi.astype(F32)).astype(BF16)
    b = _dot(t_mat, hi) + _dot(t_mat, lo)
    edge = C - 1 if forward else 0
    tot = jnp.concatenate([jnp.broadcast_to(b[g * C + edge:g * C + edge + 1], (C, GLA_KEY_WIDTH))
                           for g in range(G)], axis=0)
    order = list(range(G)) if forward else list(range(G - 1, -1, -1))
    return dict(q_dec=q * jnp.exp(b), k_inv=(k * jnp.exp(-b)).astype(BF16), k_end=k * jnp.exp(tot - b),
                tot=tot, vb=v.astype(BF16), order=order, forward=forward, G=G)


def _gla_scores(prep):
    C, H = GLA_CHUNK, GLA_HEADS
    lane_k = lax.broadcasted_iota(jnp.int32, (C, GLA_KEY_WIDTH), 1)
    qd_heads, scores = {}, {}
    for g in prep["order"]:
        rows = slice(g * C, (g + 1) * C)
        qd = prep["q_dec"][rows]
        qd_heads[g] = jnp.concatenate([jnp.where((lane_k >> GLA_DK_SHIFT) == h, qd, 0.0) for h in range(H)],
                                      axis=0).astype(BF16)
        scores[g] = _dot_nt(qd_heads[g], prep["k_inv"][rows])
    return qd_heads, scores


def _gla_chunk_updates(prep):
    C, H, G = GLA_CHUNK, GLA_HEADS, prep["G"]
    k_end, tot, vb = prep["k_end"], prep["tot"], prep["vb"]
    kv, dec_t = {}, {}
    lane = lax.broadcasted_iota(jnp.int32, (GLA_KEY_WIDTH, 2 * C), 1)
    zeros = jnp.zeros((C, GLA_DV), BF16)
    for p in range(G // 2):
        pair = slice(2 * p * C, (2 * p + 2) * C)
        ke_t = k_end[pair].T.astype(BF16)
        tot_t = tot[pair].T
        swapped = pltpu.roll(tot_t, C, 1)
        for half in range(2):
            g = 2 * p + half
            rows = slice(g * C, (g + 1) * C)
            own = (lane < C) if half == 0 else (lane >= C)
            dec_t[g] = jnp.exp(jnp.where(own, tot_t, swapped))
            parts = []
            for h in range(H):
                v_h = vb[rows, h * GLA_DV:(h + 1) * GLA_DV]
                v_pad = jnp.concatenate([v_h, zeros] if half == 0 else [zeros, v_h], axis=0)
                parts.append(_dot(ke_t[h * C:(h + 1) * C], v_pad))
            kv[g] = jnp.concatenate(parts, axis=0)
    return kv, dec_t


def _gla_states(prep, kv, dec_t, s_ref):
    st = s_ref[...]
    states = {}
    for g in prep["order"]:
        states[g] = st.astype(BF16)
        st = st * dec_t[g] + kv[g]
    s_ref[...] = st
    return states


def _gla_outputs(prep, qd_heads, scores, inter, o_ref):
    C, H = GLA_CHUNK, GLA_HEADS
    row_q = lax.broadcasted_iota(jnp.int32, (H * C, C), 0) & (C - 1)
    col_k = lax.broadcasted_iota(jnp.int32, (H * C, C), 1)
    a_mask = (col_k <= row_q) if prep["forward"] else (col_k >= row_q)
    for g in prep["order"]:
        rows = slice(g * C, (g + 1) * C)
        a = jnp.where(a_mask, scores[g], 0.0).astype(BF16)
        vv = prep["vb"][rows]
        o_ref[rows, :] = jnp.concatenate(
            [_dot(a[h * C:(h + 1) * C], vv[:, h * GLA_DV:(h + 1) * GLA_DV]) + inter[g][h * C:(h + 1) * C]
             for h in range(H)], axis=1).astype(o_ref.dtype)


def _gla_kernel(qf_ref, kf_ref, vf_ref, laf_ref, qb_ref, kb_ref, vb_ref, lab_ref,
                of_ref, ob_ref, sf_ref, sb_ref, *, G):
    @pl.when(pl.program_id(1) == 0)
    def _():
        sf_ref[...] = jnp.zeros_like(sf_ref)
        sb_ref[...] = jnp.zeros_like(sb_ref)

    dirs = [(_gla_decays(qf_ref[...], kf_ref[...], vf_ref[...], laf_ref[...], True, G), sf_ref, of_ref),
            (_gla_decays(qb_ref[...], kb_ref[...], vb_ref[...], lab_ref[...], False, G), sb_ref, ob_ref)]
    scored = [_gla_scores(prep) for prep, _, _ in dirs]
    updates = [_gla_chunk_updates(prep) for prep, _, _ in dirs]
    states = [_gla_states(prep, kv, dec_t, s_ref) for (prep, s_ref, _), (kv, dec_t) in zip(dirs, updates)]
    inters = [{g: _dot(qd_heads[g], st[g]) for g in prep["order"]}
              for (prep, _, _), (qd_heads, _), st in zip(dirs, scored, states)]
    for (prep, _, o_ref), (qd_heads, scores), inter in zip(dirs, scored, inters):
        _gla_outputs(prep, qd_heads, scores, inter, o_ref)


def _gla(gla_slab, loga, B, S, G=8):
    T = B * S
    R = G * GLA_CHUNK
    ns = S // R
    fwd = lambda col: (lambda b, i: (b * ns + i, col))
    bwd = lambda col: (lambda b, i: (b * ns + ns - 1 - i, col))
    kw, vw = GLA_KEY_WIDTH, GLA_VAL_WIDTH
    return pl.pallas_call(
        functools.partial(_gla_kernel, G=G),
        grid=(B, ns),
        in_specs=[
            pl.BlockSpec((R, kw), fwd(0)), pl.BlockSpec((R, kw), fwd(1)),
            pl.BlockSpec((R, vw), fwd(1)), pl.BlockSpec((R, kw), fwd(0)),
            pl.BlockSpec((R, kw), bwd(0)), pl.BlockSpec((R, kw), bwd(1)),
            pl.BlockSpec((R, vw), bwd(1)), pl.BlockSpec((R, kw), bwd(1)),
        ],
        out_specs=[pl.BlockSpec((R, vw), fwd(0)), pl.BlockSpec((R, vw), bwd(0))],
        out_shape=[jax.ShapeDtypeStruct((T, vw), BF16), jax.ShapeDtypeStruct((T, vw), BF16)],
        scratch_shapes=[pltpu.VMEM((kw, GLA_DV), F32), pltpu.VMEM((kw, GLA_DV), F32)],
        compiler_params=_cparams(("arbitrary", "arbitrary")),
        name="gla",
    )(gla_slab, gla_slab, gla_slab, loga, gla_slab, gla_slab, gla_slab, loga)


ATT_CLASSES = 4
ATT_QB = 128
ATT_KB = ATT_QB + 2 * ATT_RADIUS


ATT_UNROLL = (32, 16, 16)


def _att_kernel(q_ref, k_ref, v_ref, o_ref, m_ref, l_ref, bias_ref, *, S):
    QB, KB, NC = ATT_QB, ATT_KB, ATT_CLASSES
    L4 = S // NC
    lane = lax.broadcasted_iota(jnp.int32, (QB, LANES), 1)
    head0 = lane < ATT_HEAD_DIM

    @pl.when((pl.program_id(0) == 0) & (pl.program_id(1) == 0))
    def _():
        rowi = lax.broadcasted_iota(jnp.int32, (2 * QB, KB), 0) & (QB - 1)
        coli = lax.broadcasted_iota(jnp.int32, (2 * QB, KB), 1)
        qpos = (rowi & (QB // NC - 1)) * NC + (rowi >> _log2(QB // NC))
        kpos = (coli & (KB // NC - 1)) * NC + (coli >> _log2(KB // NC))
        for case in range(3):
            bias_ref[0, case] = jnp.where(jnp.abs(rowi - coli + case * ATT_RADIUS) <= ATT_RADIUS, 0.0, NEG_INF)
            bias_ref[1, case] = jnp.where(jnp.abs(qpos - kpos + case * ATT_RADIUS) <= ATT_RADIUS, 0.0, NEG_INF)

    for pi, (_, d) in enumerate(DILATED_PATTERNS):
        L = S // d
        nb = L // QB
        shift = nb.bit_length() - 1
        first = pi == 0
        last = pi == len(DILATED_PATTERNS) - 1

        def scores(n, d=d, L=L, nb=nb, shift=shift):
            cls = n >> shift
            q0 = (n & (nb - 1)) * QB
            ws = jnp.clip(q0 - ATT_RADIUS, 0, L - KB)
            if d == 1:
                qsls = [pl.ds(pl.multiple_of(c * L4 + q0 // NC, QB // NC), QB // NC) for c in range(NC)]
                ksls = [pl.ds(pl.multiple_of(c * L4 + ws // NC, ATT_RADIUS // NC), KB // NC) for c in range(NC)]
            elif d == NC:
                qsls = [pl.ds(pl.multiple_of(cls * L4 + q0, QB), QB)]
                ksls = [pl.ds(pl.multiple_of(cls * L4 + ws, ATT_RADIUS), KB)]
            else:
                base = (cls & (NC - 1)) * L4 + (cls >> _log2(NC))
                qsls = [pl.ds(base + NC * q0, QB, stride=NC)]
                ksls = [pl.ds(base + NC * ws, KB, stride=NC)]
            q = jnp.concatenate([q_ref[sl, :] for sl in qsls], axis=0)
            kw = jnp.concatenate([k_ref[sl, :] for sl in ksls], axis=0)
            kb = kw.astype(BF16)
            bias = bias_ref[1 if d == 1 else 0, (q0 - ws) >> _log2(ATT_RADIUS), :QB]
            q_heads = (jnp.where(head0, q, 0.0), jnp.where(head0, 0.0, q))
            s = [_dot_nt(qh.astype(BF16), kb) + bias for qh in q_heads]
            return qsls, ksls, s

        def softmax_pv(qsls, ksls, s):
            vw = jnp.concatenate([v_ref[sl, :] for sl in ksls], axis=0)
            v_ones = jnp.concatenate([vw.astype(BF16), jnp.ones((KB, LANES), BF16)], axis=1)
            m_h = [jnp.max(t, axis=-1, keepdims=True) for t in s]
            pv = [_dot(jnp.exp2(t - m).astype(BF16), v_ones) for t, m in zip(s, m_h)]
            acc_b = jnp.where(head0, pv[0][:, :LANES], pv[1][:, :LANES])
            m_b = jnp.where(head0, m_h[0], m_h[1])
            l_b = jnp.where(head0, pv[0][:, LANES:], pv[1][:, LANES:])
            return qsls, acc_b, m_b, l_b

        def load(ref, sls):
            return jnp.concatenate([ref[sl, :] for sl in sls], axis=0)

        def store(ref, sls, val):
            n = val.shape[0] // len(sls)
            for i, sl in enumerate(sls):
                ref[sl, :] = val[i * n:(i + 1) * n]

        unroll = ATT_UNROLL[pi]

        def body(n, carry, first=first, last=last, unroll=unroll):
            staged = [scores(n * unroll + u) for u in range(unroll)]
            blocks = [softmax_pv(*st) for st in staged]
            for qsls, acc_b, m_b, l_b in blocks:
                if first:
                    acc, m_new, l_new = acc_b, m_b, l_b
                else:
                    m_old = load(m_ref, qsls)
                    m_new = jnp.maximum(m_old, m_b)
                    w_old = jnp.exp2(m_old - m_new)
                    w_blk = jnp.exp2(m_b - m_new)
                    acc = load(o_ref, qsls) * w_old + acc_b * w_blk
                    l_new = load(l_ref, qsls) * w_old + l_b * w_blk
                if last:
                    store(o_ref, qsls, acc / l_new)
                else:
                    store(o_ref, qsls, acc)
                    store(m_ref, qsls, m_new)
                    store(l_ref, qsls, l_new)
            return carry

        lax.fori_loop(0, S // (QB * unroll), body, 0)


def _attention(att_slab, B, S):
    T = B * S
    ncol = ATT_WIDTH // LANES
    return pl.pallas_call(
        functools.partial(_att_kernel, S=S),
        grid=(B, ncol),
        in_specs=[
            pl.BlockSpec((S, LANES), lambda b, h: (b, h)),
            pl.BlockSpec((S, LANES), lambda b, h: (b, ncol + h)),
            pl.BlockSpec((S, LANES), lambda b, h: (b, 2 * ncol + h)),
        ],
        out_specs=pl.BlockSpec((S, LANES), lambda b, h: (b, h)),
        out_shape=jax.ShapeDtypeStruct((T, ATT_WIDTH), F32),
        scratch_shapes=[pltpu.VMEM((S, LANES), F32), pltpu.VMEM((S, LANES), F32),
                        pltpu.VMEM((2, 3, 2 * ATT_QB, ATT_KB), F32)],
        compiler_params=_cparams(("arbitrary", "arbitrary")),
        name="dilated_attention",
    )(att_slab, att_slab, att_slab)


PACK_WORDS = D_MODEL // 2
ROW_TILE = PACK_WORDS // LANES
HIGH_HALF = -65536


def _pack_rows(x):
    bits = lambda v: lax.bitcast_convert_type(v.astype(BF16).astype(F32), jnp.int32)
    low = (bits(x[:, :PACK_WORDS]) >> 16) & 0xFFFF
    return (bits(x[:, PACK_WORDS:]) & HIGH_HALF) | low


def _unpack_rows(w):
    low = lax.bitcast_convert_type(w << 16, F32)
    high = lax.bitcast_convert_type(w & HIGH_HALF, F32)
    return jnp.concatenate([low, high], axis=1).astype(BF16)


def _to_row_tiles(ref, w):
    n = w.shape[0]
    for j in range(ROW_TILE):
        ref[pl.ds(j, n, stride=ROW_TILE), :] = w[:, j * LANES:(j + 1) * LANES]


def _from_row_tiles(ref, n):
    return jnp.concatenate([ref[pl.ds(j, n, stride=ROW_TILE), :] for j in range(ROW_TILE)], axis=1)


def _tile_copy(src_ref, src_row, dst_ref, dst_row, sem):
    src = pl.ds(pl.multiple_of(src_row * ROW_TILE, ROW_TILE), ROW_TILE)
    dst = pl.ds(pl.multiple_of(dst_row * ROW_TILE, ROW_TILE), ROW_TILE)
    return pltpu.make_async_copy(src_ref.at[src], dst_ref.at[dst], sem)


def _outproj_kernel(of_ref, ob_ref, gg_ref, att_ref, x_ref, gnw_ref, wo1_ref, wo2_ref,
                    n2_ref, wr_ref, br_ref, h_ref, u_ref, lg_ref, stage_ref):
    rows = stage_ref.shape[1] // ATT_CLASSES
    for j in range(ATT_WIDTH // LANES):
        for c in range(ATT_CLASSES):
            stage_ref[j, pl.ds(c, rows, stride=ATT_CLASSES), :] = att_ref[c, :, j * LANES:(j + 1) * LANES]
    att = jnp.concatenate([stage_ref[j] for j in range(ATT_WIDTH // LANES)], axis=1)
    o = of_ref[...].astype(F32) + ob_ref[...].astype(F32)
    gate = gg_ref[...].astype(F32)
    gnw = gnw_ref[...]
    parts = []
    for h in range(GLA_HEADS):
        sl = slice(h * GLA_DV, (h + 1) * GLA_DV)
        parts.append(_rms(o[:, sl], gnw))
    y = jnp.concatenate(parts, axis=1) * (gate / (1.0 + jnp.exp(-gate)))
    mix = _dot(y.astype(BF16), wo1_ref[...]) + _dot(att.astype(BF16), wo2_ref[...])
    h = x_ref[...] + mix
    h_ref[...] = h
    u = _rms(h, n2_ref[...])
    _to_row_tiles(u_ref, _pack_rows(u))
    u_hi = u.astype(BF16)
    u_lo = (u - u_hi.astype(F32)).astype(BF16)
    hi_both = _dot_nt(wr_ref[...], u_hi)
    lg_ref[...] = (hi_both[:LANES] + hi_both[LANES:] + _dot_nt(wr_ref[:LANES], u_lo)) + br_ref[...]


def _outproj(o_f, o_b, gate, att_out, x2, gla_norm_w, w_out, norm2_w, wr, br, tm=512):
    T = x2.shape[0]
    nS = att_out.shape[2] * ATT_CLASSES // tm
    row = lambda i: (i, 0)
    const = lambda i: (0, 0)
    wo = w_out.astype(BF16)
    wr_hi = wr.astype(BF16)
    wr_lo = (wr - wr_hi.astype(F32)).astype(BF16)
    wr = jnp.concatenate([wr_hi, wr_lo], axis=0)
    return pl.pallas_call(
        _outproj_kernel,
        grid=(T // tm,),
        in_specs=[
            pl.BlockSpec((tm, GLA_VAL_WIDTH), row),
            pl.BlockSpec((tm, GLA_VAL_WIDTH), row),
            pl.BlockSpec((tm, GLA_VAL_WIDTH), row),
            pl.BlockSpec((None, ATT_CLASSES, tm // ATT_CLASSES, ATT_WIDTH), lambda i: (i // nS, 0, i % nS, 0)),
            pl.BlockSpec((tm, D_MODEL), row),
            pl.BlockSpec((1, GLA_DV), const),
            pl.BlockSpec((GLA_VAL_WIDTH, D_MODEL), const),
            pl.BlockSpec((ATT_WIDTH, D_MODEL), lambda i: (GLA_VAL_WIDTH // ATT_WIDTH, 0)),
            pl.BlockSpec((1, D_MODEL), const),
            pl.BlockSpec((2 * LANES, D_MODEL), const),
            pl.BlockSpec((LANES, 1), const),
        ],
        out_specs=[
            pl.BlockSpec((tm, D_MODEL), row),
            pl.BlockSpec((tm * ROW_TILE, LANES), row),
            pl.BlockSpec((LANES, tm), lambda i: (0, i)),
        ],
        out_shape=[
            jax.ShapeDtypeStruct((T, D_MODEL), F32),
            jax.ShapeDtypeStruct((T * ROW_TILE, LANES), jnp.int32),
            jax.ShapeDtypeStruct((LANES, T), F32),
        ],
        scratch_shapes=[pltpu.VMEM((ATT_WIDTH // LANES, tm, LANES), F32)],
        compiler_params=_cparams(("arbitrary",)),
        name="outproj",
    )(o_f, o_b, gate, att_out, x2, gla_norm_w[None, :], wo, wo,
      norm2_w[None, :], wr, br)


INFO_E1, INFO_E2, INFO_R1, INFO_R2, INFO_W1, INFO_W2 = range(6)
ROUTE_ROWS = 40


def _route_kernel(lg_ref, info_ref, cnt_ref, carry_ref):
    @pl.when(pl.program_id(0) == 0)
    def _():
        carry_ref[...] = jnp.zeros_like(carry_ref)

    lg = lg_ref[:ROUTE_ROWS, :]
    tr = lg.shape[1]
    row = lax.broadcasted_iota(jnp.int32, (ROUTE_ROWS, tr), 0)
    big = jnp.int32(1 << 20)
    is_g = (row >= MOE_N_EXPERTS) & (row < MOE_N_EXPERTS + MOE_GROUPS)
    gl = jnp.where(is_g, lg, -jnp.inf)
    gmax = jnp.max(gl, axis=0, keepdims=True)
    gsel = jnp.min(jnp.where(gl == gmax, row - MOE_N_EXPERTS, big), axis=0, keepdims=True)
    g_w = 1.0 / jnp.sum(jnp.where(is_g, jnp.exp(lg - gmax), 0.0), axis=0, keepdims=True)
    in_grp = (row < MOE_N_EXPERTS) & ((row >> MOE_GROUP_SHIFT) == gsel)
    el = jnp.where(in_grp, lg, -jnp.inf)
    v1 = jnp.max(el, axis=0, keepdims=True)
    i1 = jnp.min(jnp.where(el == v1, row, big), axis=0, keepdims=True)
    el2 = jnp.where(row == i1, -jnp.inf, el)
    v2 = jnp.max(el2, axis=0, keepdims=True)
    i2 = jnp.min(jnp.where(el2 == v2, row, big), axis=0, keepdims=True)
    t = jnp.exp(v2 - v1)
    w1 = g_w * (1.0 / (1.0 + t))
    w2 = g_w * (t / (1.0 + t))

    erow = lax.broadcasted_iota(jnp.int32, (MOE_N_EXPERTS, tr), 0)
    hit1 = erow == i1
    hit2 = erow == i2
    member = jnp.where(hit1 | hit2, 1.0, 0.0)
    r = lax.broadcasted_iota(jnp.int32, (tr, tr), 0)
    c = lax.broadcasted_iota(jnp.int32, (tr, tr), 1)
    earlier = jnp.where(r < c, 1.0, 0.0).astype(BF16)
    carry = carry_ref[...]
    prefix = _dot(member.astype(BF16), earlier) + carry[:, 0:1]
    rank1 = jnp.sum(jnp.where(hit1, prefix, 0.0), axis=0, keepdims=True)
    rank2 = jnp.sum(jnp.where(hit2, prefix, 0.0), axis=0, keepdims=True)
    carry = carry + jnp.sum(member, axis=1, keepdims=True)
    carry_ref[...] = carry
    cnt_ref[...] = carry

    zero = jnp.zeros_like(w1)
    info_ref[...] = jnp.concatenate([i1.astype(F32), i2.astype(F32), rank1, rank2, w1, w2, zero, zero], axis=0)


def _route(logits_t, tr=1024):
    T = logits_t.shape[1]
    return pl.pallas_call(
        _route_kernel,
        grid=(T // tr,),
        in_specs=[pl.BlockSpec((LANES, tr), lambda i: (0, i))],
        out_specs=[pl.BlockSpec((8, tr), lambda i: (0, i)),
                   pl.BlockSpec((MOE_N_EXPERTS, LANES), lambda i: (0, 0))],
        out_shape=[jax.ShapeDtypeStruct((8, T), F32), jax.ShapeDtypeStruct((MOE_N_EXPERTS, LANES), F32)],
        scratch_shapes=[pltpu.VMEM((MOE_N_EXPERTS, LANES), F32)],
        compiler_params=_cparams(("arbitrary",)),
        name="route",
    )(logits_t)


ROW_UNROLL = 16


def _dispatch_kernel(dest_ref, pend_ref, u_ref, xs_ref, zbuf, sem, zsem, *, td, T, nblk):
    @pl.when(pl.program_id(0) == 0)
    def _():
        zbuf[...] = jnp.zeros_like(zbuf)
        n_used = pend_ref[MOE_N_EXPERTS - 1] >> MOE_ROWS_SHIFT

        def zero_copy(blk):
            start = pl.multiple_of(blk * (MOE_ROWS * ROW_TILE), MOE_ROWS * ROW_TILE)
            return pltpu.make_async_copy(zbuf, xs_ref.at[pl.ds(start, MOE_ROWS * ROW_TILE)], zsem)

        def each_pad_block(fn):
            def per_expert(e, carry):
                prev = jnp.where(e > 0, pend_ref[jnp.maximum(e - 1, 0)], 0)

                @pl.when(pend_ref[e] > prev)
                def _():
                    fn((pend_ref[e] >> MOE_ROWS_SHIFT) - 1)
                return carry

            def per_tail(j, carry):
                @pl.when(n_used + j < nblk)
                def _():
                    fn(n_used + j)
                return carry

            lax.fori_loop(0, MOE_N_EXPERTS, per_expert, 0)
            lax.fori_loop(0, MOE_N_EXPERTS, per_tail, 0)

        each_pad_block(lambda blk: zero_copy(blk).start())
        each_pad_block(lambda blk: zero_copy(blk).wait())

    base = pl.program_id(0) * td

    def issue(g, carry):
        for j in range(ROW_UNROLL):
            r = g * ROW_UNROLL + j
            for k in range(MOE_TOP_K):
                _tile_copy(u_ref, r, xs_ref, dest_ref[k * T + base + r], sem).start(priority=k)
        return carry

    lax.fori_loop(0, td // ROW_UNROLL, issue, 0)
    for k in range(MOE_TOP_K):
        pltpu.make_async_copy(u_ref, xs_ref.at[pl.ds(0, td * ROW_TILE)], sem).wait()


def _dispatch(dest, pend, u2, cap, td=2048):
    T = u2.shape[0] // ROW_TILE
    return pl.pallas_call(
        functools.partial(_dispatch_kernel, td=td, T=T, nblk=cap // MOE_ROWS),
        grid_spec=pltpu.PrefetchScalarGridSpec(
            num_scalar_prefetch=2,
            grid=(T // td,),
            in_specs=[pl.BlockSpec((td * ROW_TILE, LANES), lambda i, d, z: (i, 0))],
            out_specs=pl.BlockSpec(memory_space=pl.ANY),
            scratch_shapes=[pltpu.VMEM((MOE_ROWS * ROW_TILE, LANES), jnp.int32),
                            pltpu.SemaphoreType.DMA(()), pltpu.SemaphoreType.DMA(())],
        ),
        out_shape=jax.ShapeDtypeStruct((cap * ROW_TILE, LANES), jnp.int32),
        compiler_params=_cparams(("arbitrary",)),
        name="dispatch",
    )(dest, pend, u2)


EXPERT_GROUP = 4


def _expert_kernel(pend_ref, xs_hbm, wg_hbm, wu_hbm, wd_hbm, ys_hbm,
                   xbuf, ybuf, zbuf, stage_g, stage_u, stage_d, wgb, wub, wdb, xsem, ysem, wsem, zsem, *, nblk):
    last = MOE_N_EXPERTS - 1
    n_used = pend_ref[last] >> MOE_ROWS_SHIFT
    n_pairs = (n_used + EXPERT_GROUP - 1) >> _log2(EXPERT_GROUP)
    block_rows = MOE_ROWS * ROW_TILE

    def rows_of(b):
        return pl.ds(pl.multiple_of(b * block_rows, block_rows), block_rows)

    def x_copy(b, slot):
        return pltpu.make_async_copy(xs_hbm.at[rows_of(b)], xbuf.at[slot], xsem.at[slot])

    def y_copy(b, slot):
        return pltpu.make_async_copy(ybuf.at[slot], ys_hbm.at[rows_of(b)], ysem.at[slot])

    def zero_copy(b):
        return pltpu.make_async_copy(zbuf, ys_hbm.at[rows_of(b)], zsem)

    def weight_copies(e):
        return (pltpu.make_async_copy(wg_hbm.at[e], stage_g, wsem.at[0]),
                pltpu.make_async_copy(wu_hbm.at[e], stage_u, wsem.at[1]),
                pltpu.make_async_copy(wd_hbm.at[e], stage_d, wsem.at[2]))

    def owner(start, row):
        return lax.while_loop(lambda e: (e < last) & (pend_ref[e] <= row), lambda e: e + 1, start)

    for c in weight_copies(owner(0, 0)):
        c.start()
    for i in range(EXPERT_GROUP):
        x_copy(i, i).start()

    zbuf[...] = jnp.zeros_like(zbuf)

    def tail(fn):
        def step(b, carry):
            fn(b)
            return carry
        lax.fori_loop(n_pairs * EXPERT_GROUP, nblk, step, 0)

    tail(lambda b: zero_copy(b).start())

    def body(p, carry):
        cur, run = carry
        half = (p & 1) * EXPERT_GROUP
        for i in range(EXPERT_GROUP):
            x_copy(p * EXPERT_GROUP + i, half + i).wait()

        @pl.when(p + 1 < n_pairs)
        def _():
            for i in range(EXPERT_GROUP):
                x_copy((p + 1) * EXPERT_GROUP + i, EXPERT_GROUP - half + i).start()

        slots = []
        for i in range(EXPERT_GROUP):
            b = p * EXPERT_GROUP + i
            e = jnp.where(b < n_used, owner(jnp.maximum(cur, 0), b * MOE_ROWS), cur)
            fresh = e != cur
            run = run + fresh.astype(jnp.int32)
            slot = run & (EXPERT_GROUP - 1)

            @pl.when(fresh)
            def _(e=e, slot=slot):
                for c in weight_copies(e):
                    c.wait()
                wgb[slot] = stage_g[...].astype(BF16)
                wub[slot] = stage_u[...].astype(BF16)
                wdb[slot] = stage_d[...].astype(BF16)

                @pl.when(pend_ref[e] < pend_ref[last])
                def _():
                    for c in weight_copies(owner(e + 1, pend_ref[e])):
                        c.start(priority=1)

            cur = e
            slots.append(slot)

        @pl.when(p >= 2)
        def _():
            for i in range(EXPERT_GROUP):
                y_copy((p - 2) * EXPERT_GROUP + i, half + i).wait()

        for i in range(EXPERT_GROUP):
            xb = _unpack_rows(_from_row_tiles(xbuf.at[half + i], MOE_ROWS))
            g = _dot(xb, wgb[slots[i]])
            u = _dot(xb, wub[slots[i]])
            hid = (g / (1.0 + jnp.exp(-g))) * u
            _to_row_tiles(ybuf.at[half + i], _pack_rows(_dot(hid.astype(BF16), wdb[slots[i]])))
        for i in range(EXPERT_GROUP):
            y_copy(p * EXPERT_GROUP + i, half + i).start()
        return cur, run

    lax.fori_loop(0, n_pairs, body, (jnp.int32(-1), jnp.int32(-1)))

    def drain(p):
        for i in range(EXPERT_GROUP):
            y_copy(p * EXPERT_GROUP + i, (p & 1) * EXPERT_GROUP + i).wait()

    @pl.when(n_pairs >= 2)
    def _():
        drain(n_pairs - 2)
    drain(n_pairs - 1)
    tail(lambda b: zero_copy(b).wait())


def _experts(pend, xs, w_gate, w_up, w_down):
    cap = xs.shape[0] // ROW_TILE
    nblk = cap // MOE_ROWS
    assert nblk % EXPERT_GROUP == 0
    block = (MOE_ROWS * ROW_TILE, LANES)
    anywhere = pl.BlockSpec(memory_space=pl.ANY)
    return pl.pallas_call(
        functools.partial(_expert_kernel, nblk=nblk),
        grid_spec=pltpu.PrefetchScalarGridSpec(
            num_scalar_prefetch=1,
            grid=(1,),
            in_specs=[anywhere, anywhere, anywhere, anywhere],
            out_specs=anywhere,
            scratch_shapes=[pltpu.VMEM((2 * EXPERT_GROUP,) + block, jnp.int32),
                            pltpu.VMEM((2 * EXPERT_GROUP,) + block, jnp.int32),
                            pltpu.VMEM(block, jnp.int32),
                            pltpu.VMEM((D_MODEL, MOE_D_FF), F32),
                            pltpu.VMEM((D_MODEL, MOE_D_FF), F32),
                            pltpu.VMEM((MOE_D_FF, D_MODEL), F32),
                            pltpu.VMEM((EXPERT_GROUP, D_MODEL, MOE_D_FF), BF16),
                            pltpu.VMEM((EXPERT_GROUP, D_MODEL, MOE_D_FF), BF16),
                            pltpu.VMEM((EXPERT_GROUP, MOE_D_FF, D_MODEL), BF16),
                            pltpu.SemaphoreType.DMA((2 * EXPERT_GROUP,)),
                            pltpu.SemaphoreType.DMA((2 * EXPERT_GROUP,)),
                            pltpu.SemaphoreType.DMA((3,)),
                            pltpu.SemaphoreType.DMA(())],
        ),
        out_shape=jax.ShapeDtypeStruct((cap * ROW_TILE, LANES), jnp.int32),
        compiler_params=_cparams(("arbitrary",)),
        name="experts",
    )(pend, xs, w_gate, w_up, w_down)


def _combine_kernel(dest_ref, ys_ref, info_ref, h_ref, fw_ref, o_ref, buf, sem, *, tc, T):
    i = pl.program_id(0)
    n = pl.num_programs(0)

    def issue(step, slot):
        base = step * tc

        def body(g, carry):
            for j in range(ROW_UNROLL):
                r = g * ROW_UNROLL + j
                for k in range(MOE_TOP_K):
                    _tile_copy(ys_ref, dest_ref[k * T + base + r], buf.at[slot, k], r,
                               sem.at[slot]).start(priority=k)
            return carry

        lax.fori_loop(0, tc // ROW_UNROLL, body, 0)

    @pl.when(i == 0)
    def _():
        issue(0, 0)

    slot = i % 2

    @pl.when(i + 1 < n)
    def _():
        issue(i + 1, 1 - slot)

    for k in range(MOE_TOP_K):
        pltpu.make_async_copy(ys_ref.at[pl.ds(0, tc * ROW_TILE)], buf.at[slot, k], sem.at[slot]).wait()

    info_t = jnp.concatenate([info_ref[...]] * (LANES // 8), axis=0).T
    w1 = info_t[:, INFO_W1:INFO_W1 + 1]
    w2 = info_t[:, INFO_W2:INFO_W2 + 1]
    y1 = _unpack_rows(_from_row_tiles(buf.at[slot, 0], tc)).astype(F32)
    y2 = _unpack_rows(_from_row_tiles(buf.at[slot, 1], tc)).astype(F32)
    h = h_ref[...] + (y1 * w1 + y2 * w2)
    o_ref[...] = _rms(h, fw_ref[...])


def _combine(dest, ys, info, h, final_w, tc=512):
    T = h.shape[0]
    return pl.pallas_call(
        functools.partial(_combine_kernel, tc=tc, T=T),
        grid_spec=pltpu.PrefetchScalarGridSpec(
            num_scalar_prefetch=1,
            grid=(T // tc,),
            in_specs=[pl.BlockSpec(memory_space=pl.ANY),
                      pl.BlockSpec((8, tc), lambda i, d: (0, i)),
                      pl.BlockSpec((tc, D_MODEL), lambda i, d: (i, 0)),
                      pl.BlockSpec((1, D_MODEL), lambda i, d: (0, 0))],
            out_specs=pl.BlockSpec((tc, D_MODEL), lambda i, d: (i, 0)),
            scratch_shapes=[pltpu.VMEM((2, MOE_TOP_K, tc * ROW_TILE, LANES), jnp.int32),
                            pltpu.SemaphoreType.DMA((2,))],
        ),
        out_shape=jax.ShapeDtypeStruct((T, D_MODEL), F32),
        compiler_params=_cparams(("arbitrary",)),
        name="combine",
    )(dest, ys, info, h, final_w[None, :])


def _plan_kernel(info_ref, cnt_ref, dest_ref, pend_ref):
    cnt = cnt_ref[...].astype(jnp.int32)
    nblk_e = ((cnt + (MOE_ROWS - 1)) >> MOE_ROWS_SHIFT).astype(F32)
    r = lax.broadcasted_iota(jnp.int32, (MOE_N_EXPERTS, MOE_N_EXPERTS), 0)
    c = lax.broadcasted_iota(jnp.int32, (MOE_N_EXPERTS, MOE_N_EXPERTS), 1)
    before = jnp.where(c < r, 1.0, 0.0).astype(BF16)
    first_blk = _dot(before, nblk_e.astype(BF16))
    pstart = first_blk[:, 0:1] * float(MOE_ROWS)
    pend_ref[...] = ((first_blk + nblk_e) * float(MOE_ROWS)).astype(jnp.int32)

    info = info_ref[...]
    erow = lax.broadcasted_iota(jnp.int32, (MOE_N_EXPERTS, info.shape[1]), 0)
    start_of = lambda e: jnp.sum(jnp.where(erow == e.astype(jnp.int32), pstart, 0.0), axis=0, keepdims=True)
    d1 = info[INFO_R1:INFO_R1 + 1] + start_of(info[INFO_E1:INFO_E1 + 1])
    d2 = info[INFO_R2:INFO_R2 + 1] + start_of(info[INFO_E2:INFO_E2 + 1])
    zero = jnp.zeros_like(d1)
    dest_ref[...] = jnp.concatenate([d1, d2] + [zero] * 6, axis=0).astype(jnp.int32)


def _plan(info, counts, tr=2048):
    T = info.shape[1]
    dest8, pend = pl.pallas_call(
        _plan_kernel,
        grid=(T // tr,),
        in_specs=[pl.BlockSpec((8, tr), lambda i: (0, i)),
                  pl.BlockSpec((MOE_N_EXPERTS, LANES), lambda i: (0, 0))],
        out_specs=[pl.BlockSpec((8, tr), lambda i: (0, i)),
                   pl.BlockSpec((MOE_N_EXPERTS, LANES), lambda i: (0, 0))],
        out_shape=[jax.ShapeDtypeStruct((8, T), jnp.int32),
                   jax.ShapeDtypeStruct((MOE_N_EXPERTS, LANES), jnp.int32)],
        compiler_params=_cparams(("arbitrary",)),
        name="plan",
    )(info, counts)
    return dest8[:MOE_TOP_K].reshape(-1), pend[:, 0]


def _moe_capacity(T):
    return (-(-(T * MOE_TOP_K) // MOE_ROWS) + MOE_N_EXPERTS) * MOE_ROWS


def _router_weights(router_group_w, router_group_b, router_expert_w, router_expert_b):
    we = jnp.transpose(router_expert_w, (0, 2, 1)).reshape(MOE_N_EXPERTS, D_MODEL)
    pad = LANES - MOE_N_EXPERTS - MOE_GROUPS
    wr = jnp.concatenate([we, router_group_w.T, jnp.zeros((pad, D_MODEL), F32)], axis=0)
    br = jnp.concatenate([router_expert_b.reshape(-1), router_group_b, jnp.zeros((pad,), F32)])[:, None]
    return wr, br


def kernel(x, norm1_w, w_in, gla_fwd_gate_w, gla_fwd_gate_b, gla_bwd_gate_w, gla_bwd_gate_b,
           gla_norm_w, w_out, norm2_w, router_group_w, router_group_b, router_expert_w,
           router_expert_b, expert_w_gate, expert_w_up, expert_w_down, final_norm_w):
    B, S, D = x.shape
    T = B * S
    assert norm1_w.shape[0] == 1, "single-layer trunk: the final norm is fused into the combine step"
    h = x.reshape(T, D)
    gla_slab, gate, loga, att_slab = _inproj(h, S, norm1_w[0], w_in[0], gla_fwd_gate_w[0], gla_fwd_gate_b[0],
                                       gla_bwd_gate_w[0], gla_bwd_gate_b[0])
    o_f, o_b = _gla(gla_slab, loga, B, S)
    att_out = _attention(att_slab.reshape(T, 3 * ATT_WIDTH), B, S)
    att_out = att_out.reshape(B, ATT_CLASSES, S // ATT_CLASSES, ATT_WIDTH)
    wr, br = _router_weights(router_group_w[0], router_group_b[0], router_expert_w[0], router_expert_b[0])
    h, u2, logits = _outproj(o_f, o_b, gate, att_out, h, gla_norm_w[0], w_out[0], norm2_w[0], wr, br)
    info, counts = _route(logits)
    dest, pend = _plan(info, counts)
    xs = _dispatch(dest, pend, u2, _moe_capacity(T))
    ys = _experts(pend, xs, expert_w_gate[0], expert_w_up[0], expert_w_down[0])
    out = _combine(dest, ys, info, h, final_norm_w)
    return out.reshape(B, S, D)
```

```python
import functools

import jax
import jax.numpy as jnp
import numpy as np
from jax import lax
from jax.experimental import pallas as pl
from jax.experimental.pallas import tpu as pltpu

F32 = jnp.float32
BF16 = jnp.bfloat16

D_MODEL = 1024
GLA_HEADS = 4
GLA_DV = 128
GLA_DK = 64
GLA_KEY_WIDTH = GLA_HEADS * GLA_DK
GLA_VAL_WIDTH = GLA_HEADS * GLA_DV
GLA_GATE_RANK = 16
GLA_TAU = 16.0
GLA_CHUNK = 64
ATT_WIDTH = 512
ATT_HEAD_DIM = 64
ATT_HEADS = 8
ROT_DIM = 16
ROPE_THETA = 500000.0
DILATED_PATTERNS = ((128, 1), (512, 4), (2048, 16))
ATT_RADIUS = 64
MOE_GROUPS = 4
MOE_EXPERTS_PER_GROUP = 8
MOE_N_EXPERTS = 32
MOE_TOP_K = 2
MOE_D_FF = 512
EPS = 1e-6
NEG_INF = -1e30
LOG2E = 1.4426950408889634

LANES = 128
MOE_ROWS = 256


def _log2(n):
    assert n & (n - 1) == 0, n
    return n.bit_length() - 1


GLA_CHUNK_SHIFT = _log2(GLA_CHUNK)
GLA_DK_SHIFT = _log2(GLA_DK)
MOE_ROWS_SHIFT = _log2(MOE_ROWS)
MOE_GROUP_SHIFT = _log2(MOE_EXPERTS_PER_GROUP)
VMEM_LIMIT = 56 * 1024 * 1024


def _cparams(sem):
    return pltpu.CompilerParams(dimension_semantics=sem, vmem_limit_bytes=VMEM_LIMIT)


def _dot(a, b):
    return jnp.dot(a, b, preferred_element_type=F32)


def _dot_nt(a, b):
    return lax.dot_general(a, b, (((1,), (1,)), ((), ())), preferred_element_type=F32)


def _dot_tn(a, b):
    return lax.dot_general(a, b, (((0,), (0,)), ((), ())), preferred_element_type=F32)


def _rms(x, w):
    return x * lax.rsqrt(jnp.mean(x * x, axis=-1, keepdims=True) + EPS) * w


def _inproj_kernel(x_ref, n1_ref, wg_ref, wlr_ref, wa_ref, gw_ref, gb_ref,
                   rc_ref, rs1_ref, rs2_ref, gla_ref, gate_ref, loga_ref, att_ref, stage_ref, wgb, wlrb):
    @pl.when(pl.program_id(0) == 0)
    def _():
        wgb[...] = wg_ref[...].astype(BF16)
        wlrb[...] = wlr_ref[...].astype(BF16)

    x = x_ref[...]
    ub = _rms(x, n1_ref[...]).astype(BF16)
    g = _dot(ub, wgb[...])
    qkv = 2 * GLA_KEY_WIDTH + GLA_VAL_WIDTH
    gla_ref[:, :GLA_KEY_WIDTH] = g[:, :GLA_KEY_WIDTH] * (GLA_DK ** -0.5)
    gla_ref[:, GLA_KEY_WIDTH:] = g[:, GLA_KEY_WIDTH:qkv]
    gate_ref[...] = g[:, qkv:].astype(BF16)
    lr = _dot(ub, wlrb[...])
    gate = _dot(lr.astype(BF16), gw_ref[...]) + gb_ref[...]
    loga_ref[...] = (jnp.minimum(gate, 0.0) - jnp.log(1.0 + jnp.exp(-jnp.abs(gate)))) * (1.0 / GLA_TAU)
    a = _dot(ub, wa_ref[...])
    qk = a[:, :2 * ATT_WIDTH]
    reps = 2 * ATT_WIDTH // LANES
    c = jnp.concatenate([rc_ref[...]] * reps, axis=1)
    s1 = jnp.concatenate([rs1_ref[...]] * reps, axis=1)
    s2 = jnp.concatenate([rs2_ref[...]] * reps, axis=1)
    half = ROT_DIM // 2
    n = 2 * ATT_WIDTH
    roped = qk * c + pltpu.roll(qk, n - half, 1) * s1 + pltpu.roll(qk, half, 1) * s2
    qkv = jnp.concatenate([roped[:, :ATT_WIDTH] * (ATT_HEAD_DIM ** -0.5 * LOG2E), roped[:, ATT_WIDTH:],
                           a[:, 2 * ATT_WIDTH:]], axis=1)
    rows = x.shape[0] // ATT_CLASSES
    for j in range(3 * ATT_WIDTH // LANES):
        cols = slice(j * LANES, (j + 1) * LANES)
        stage_ref[j] = qkv[:, cols]
        for c in range(ATT_CLASSES):
            att_ref[c, :, cols] = stage_ref[j, pl.ds(c, rows, stride=ATT_CLASSES), :]


def _rope_lane_tables(S):
    half = ROT_DIM // 2
    inv = np.float32(ROPE_THETA) ** (-(np.arange(0, ROT_DIM, 2, dtype=np.float32) / np.float32(ROT_DIM)))
    ang = np.arange(S, dtype=np.float32)[:, None] * inv[None, :].astype(np.float32)
    cos, sin = np.cos(ang), np.sin(ang)
    ones = np.ones((S, ATT_HEAD_DIM - ROT_DIM), np.float32)
    zeros = np.zeros((S, ATT_HEAD_DIM - ROT_DIM), np.float32)
    zeros8 = np.zeros((S, half), np.float32)
    rep = LANES // ATT_HEAD_DIM
    c = np.tile(np.concatenate([cos, cos, ones], axis=1), (1, rep))
    s1 = np.tile(np.concatenate([-sin, zeros8, zeros], axis=1), (1, rep))
    s2 = np.tile(np.concatenate([zeros8, sin, zeros], axis=1), (1, rep))
    return jnp.asarray(c), jnp.asarray(s1), jnp.asarray(s2)


def _inproj(x2, S, norm1_w, w_in, wf, bfw, wb, bbw, tm=512):
    T = x2.shape[0]
    o_lr = 2 * GLA_KEY_WIDTH + 2 * GLA_VAL_WIDTH
    o_att = o_lr + 2 * GLA_GATE_RANK
    wa = w_in[:, o_att:].astype(BF16)
    zeros = jnp.zeros((GLA_GATE_RANK, GLA_KEY_WIDTH), F32)
    gw = jnp.concatenate([jnp.concatenate([wf, zeros], axis=1), jnp.concatenate([zeros, wb], axis=1),
                          jnp.zeros((LANES - 2 * GLA_GATE_RANK, 2 * GLA_KEY_WIDTH), F32)], axis=0).astype(BF16)
    gb = jnp.concatenate([bfw, bbw])[None, :]
    rc, rs1, rs2 = _rope_lane_tables(S)
    nS = S // tm
    row = lambda i: (i, 0)
    const = lambda i: (0, 0)
    pos = lambda i: (i % nS, 0)
    return pl.pallas_call(
        _inproj_kernel,
        grid=(T // tm,),
        in_specs=[
            pl.BlockSpec((tm, D_MODEL), row),
            pl.BlockSpec((1, D_MODEL), const),
            pl.BlockSpec((D_MODEL, o_lr), const),
            pl.BlockSpec((D_MODEL, LANES), lambda i: (0, o_lr // LANES)),
            pl.BlockSpec((D_MODEL, 3 * ATT_WIDTH), const),
            pl.BlockSpec((LANES, 2 * GLA_KEY_WIDTH), const),
            pl.BlockSpec((1, 2 * GLA_KEY_WIDTH), const),
            pl.BlockSpec((tm, LANES), pos),
            pl.BlockSpec((tm, LANES), pos),
            pl.BlockSpec((tm, LANES), pos),
        ],
        out_specs=[
            pl.BlockSpec((tm, o_lr - GLA_VAL_WIDTH), row),
            pl.BlockSpec((tm, GLA_VAL_WIDTH), row),
            pl.BlockSpec((tm, 2 * GLA_KEY_WIDTH), row),
            pl.BlockSpec((None, ATT_CLASSES, tm // ATT_CLASSES, 3 * ATT_WIDTH),
                         lambda i: (i // nS, 0, i % nS, 0)),
        ],
        out_shape=[
            jax.ShapeDtypeStruct((T, o_lr - GLA_VAL_WIDTH), F32),
            jax.ShapeDtypeStruct((T, GLA_VAL_WIDTH), BF16),
            jax.ShapeDtypeStruct((T, 2 * GLA_KEY_WIDTH), F32),
            jax.ShapeDtypeStruct((T // S, ATT_CLASSES, S // ATT_CLASSES, 3 * ATT_WIDTH), F32),
        ],
        scratch_shapes=[pltpu.VMEM((3 * ATT_WIDTH // LANES, tm, LANES), F32),
                        pltpu.VMEM((D_MODEL, o_lr), BF16), pltpu.VMEM((D_MODEL, LANES), BF16)],
        compiler_params=_cparams(("arbitrary",)),
        name="inproj",
    )(x2, norm1_w[None, :], w_in, w_in, wa, gw, gb, rc, rs1, rs2)


def _gla_decays(q, k, v, la, forward, G):
    C = GLA_CHUNK
    R = G * C
    r = lax.broadcasted_iota(jnp.int32, (R, R), 0)
    c = lax.broadcasted_iota(jnp.int32, (R, R), 1)
    same = (r >> GLA_CHUNK_SHIFT) == (c >> GLA_CHUNK_SHIFT)
    tri = (c <= r) if forward else (c >= r)
    t_mat = jnp.where(same, jnp.where(tri, 1.0, 0.0), 0.0).astype(BF16)
    hi = la.astype(BF16)
    lo = (la - hi.astype(F32)).astype(BF16)
    b = _dot(t_mat, hi) + _dot(t_mat, lo)
    edge = C - 1 if forward else 0
    tot = jnp.concatenate([jnp.broadcast_to(b[g * C + edge:g * C + edge + 1], (C, GLA_KEY_WIDTH))
                           for g in range(G)], axis=0)
    order = list(range(G)) if forward else list(range(G - 1, -1, -1))
    return dict(q_dec=q * jnp.exp(b), k_inv=(k * jnp.exp(-b)).astype(BF16), k_end=k * jnp.exp(tot - b),
                tot=tot, vb=v.astype(BF16), order=order, forward=forward, G=G)


def _gla_scores(prep):
    C, H = GLA_CHUNK, GLA_HEADS
    lane_k = lax.broadcasted_iota(jnp.int32, (C, GLA_KEY_WIDTH), 1)
    qd_heads, scores = {}, {}
    for g in prep["order"]:
        rows = slice(g * C, (g + 1) * C)
        qd = prep["q_dec"][rows]
        qd_heads[g] = jnp.concatenate([jnp.where((lane_k >> GLA_DK_SHIFT) == h, qd, 0.0) for h in range(H)],
                                      axis=0).astype(BF16)
        scores[g] = _dot_nt(qd_heads[g], prep["k_inv"][rows])
    return qd_heads, scores


def _gla_chunk_updates(prep):
    C, H, G = GLA_CHUNK, GLA_HEADS, prep["G"]
    k_end, tot, vb = prep["k_end"], prep["tot"], prep["vb"]
    kv, dec_t = {}, {}
    lane = lax.broadcasted_iota(jnp.int32, (GLA_KEY_WIDTH, 2 * C), 1)
    zeros = jnp.zeros((C, GLA_DV), BF16)
    for p in range(G // 2):
        pair = slice(2 * p * C, (2 * p + 2) * C)
        ke_t = k_end[pair].T.astype(BF16)
        tot_t = tot[pair].T
        swapped = pltpu.roll(tot_t, C, 1)
        for half in range(2):
            g = 2 * p + half
            rows = slice(g * C, (g + 1) * C)
            own = (lane < C) if half == 0 else (lane >= C)
            dec_t[g] = jnp.exp(jnp.where(own, tot_t, swapped))
            parts = []
            for h in range(H):
                v_h = vb[rows, h * GLA_DV:(h + 1) * GLA_DV]
                v_pad = jnp.concatenate([v_h, zeros] if half == 0 else [zeros, v_h], axis=0)
                parts.append(_dot(ke_t[h * C:(h + 1) * C], v_pad))
            kv[g] = jnp.concatenate(parts, axis=0)
    return kv, dec_t


def _gla_states(prep, kv, dec_t, s_ref):
    st = s_ref[...]
    states = {}
    for g in prep["order"]:
        states[g] = st.astype(BF16)
        st = st * dec_t[g] + kv[g]
    s_ref[...] = st
    return states


def _gla_outputs(prep, qd_heads, scores, inter, o_ref):
    C, H = GLA_CHUNK, GLA_HEADS
    row_q = lax.broadcasted_iota(jnp.int32, (H * C, C), 0) & (C - 1)
    col_k = lax.broadcasted_iota(jnp.int32, (H * C, C), 1)
    a_mask = (col_k <= row_q) if prep["forward"] else (col_k >= row_q)
    for g in prep["order"]:
        rows = slice(g * C, (g + 1) * C)
        a = jnp.where(a_mask, scores[g], 0.0).astype(BF16)
        vv = prep["vb"][rows]
        o_ref[rows, :] = jnp.concatenate(
            [_dot(a[h * C:(h + 1) * C], vv[:, h * GLA_DV:(h + 1) * GLA_DV]) + inter[g][h * C:(h + 1) * C]
             for h in range(H)], axis=1).astype(o_ref.dtype)


def _gla_kernel(qf_ref, kf_ref, vf_ref, laf_ref, qb_ref, kb_ref, vb_ref, lab_ref,
                of_ref, ob_ref, sf_ref, sb_ref, *, G):
    @pl.when(pl.program_id(1) == 0)
    def _():
        sf_ref[...] = jnp.zeros_like(sf_ref)
        sb_ref[...] = jnp.zeros_like(sb_ref)

    dirs = [(_gla_decays(qf_ref[...], kf_ref[...], vf_ref[...], laf_ref[...], True, G), sf_ref, of_ref),
            (_gla_decays(qb_ref[...], kb_ref[...], vb_ref[...], lab_ref[...], False, G), sb_ref, ob_ref)]
    scored = [_gla_scores(prep) for prep, _, _ in dirs]
    updates = [_gla_chunk_updates(prep) for prep, _, _ in dirs]
    states = [_gla_states(prep, kv, dec_t, s_ref) for (prep, s_ref, _), (kv, dec_t) in zip(dirs, updates)]
    inters = [{g: _dot(qd_heads[g], st[g]) for g in prep["order"]}
              for (prep, _, _), (qd_heads, _), st in zip(dirs, scored, states)]
    for (prep, _, o_ref), (qd_heads, scores), inter in zip(dirs, scored, inters):
        _gla_outputs(prep, qd_heads, scores, inter, o_ref)


def _gla(gla_slab, loga, B, S, G=8):
    T = B * S
    R = G * GLA_CHUNK
    ns = S // R
    fwd = lambda col: (lambda b, i: (b * ns + i, col))
    bwd = lambda col: (lambda b, i: (b * ns + ns - 1 - i, col))
    kw, vw = GLA_KEY_WIDTH, GLA_VAL_WIDTH
    return pl.pallas_call(
        functools.partial(_gla_kernel, G=G),
        grid=(B, ns),
        in_specs=[
            pl.BlockSpec((R, kw), fwd(0)), pl.BlockSpec((R, kw), fwd(1)),
            pl.BlockSpec((R, vw), fwd(1)), pl.BlockSpec((R, kw), fwd(0)),
            pl.BlockSpec((R, kw), bwd(0)), pl.BlockSpec((R, kw), bwd(1)),
            pl.BlockSpec((R, vw), bwd(1)), pl.BlockSpec((R, kw), bwd(1)),
        ],
        out_specs=[pl.BlockSpec((R, vw), fwd(0)), pl.BlockSpec((R, vw), bwd(0))],
        out_shape=[jax.ShapeDtypeStruct((T, vw), BF16), jax.ShapeDtypeStruct((T, vw), BF16)],
        scratch_shapes=[pltpu.VMEM((kw, GLA_DV), F32), pltpu.VMEM((kw, GLA_DV), F32)],
        compiler_params=_cparams(("arbitrary", "arbitrary")),
        name="gla",
    )(gla_slab, gla_slab, gla_slab, loga, gla_slab, gla_slab, gla_slab, loga)


ATT_CLASSES = 4
ATT_QB = 128
ATT_KB = ATT_QB + 2 * ATT_RADIUS


ATT_UNROLL = (32, 16, 16)


def _att_kernel(q_ref, k_ref, v_ref, o_ref, m_ref, l_ref, bias_ref, *, S):
    QB, KB, NC = ATT_QB, ATT_KB, ATT_CLASSES
    L4 = S // NC
    lane = lax.broadcasted_iota(jnp.int32, (QB, LANES), 1)
    head0 = lane < ATT_HEAD_DIM

    @pl.when((pl.program_id(0) == 0) & (pl.program_id(1) == 0))
    def _():
        rowi = lax.broadcasted_iota(jnp.int32, (2 * QB, KB), 0) & (QB - 1)
        coli = lax.broadcasted_iota(jnp.int32, (2 * QB, KB), 1)
        qpos = (rowi & (QB // NC - 1)) * NC + (rowi >> _log2(QB // NC))
        kpos = (coli & (KB // NC - 1)) * NC + (coli >> _log2(KB // NC))
        for case in range(3):
            bias_ref[0, case] = jnp.where(jnp.abs(rowi - coli + case * ATT_RADIUS) <= ATT_RADIUS, 0.0, NEG_INF)
            bias_ref[1, case] = jnp.where(jnp.abs(qpos - kpos + case * ATT_RADIUS) <= ATT_RADIUS, 0.0, NEG_INF)

    for pi, (_, d) in enumerate(DILATED_PATTERNS):
        L = S // d
        nb = L // QB
        shift = nb.bit_length() - 1
        first = pi == 0
        last = pi == len(DILATED_PATTERNS) - 1

        def scores(n, d=d, L=L, nb=nb, shift=shift):
            cls = n >> shift
            q0 = (n & (nb - 1)) * QB
            ws = jnp.clip(q0 - ATT_RADIUS, 0, L - KB)
            if d == 1:
                qsls = [pl.ds(pl.multiple_of(c * L4 + q0 // NC, QB // NC), QB // NC) for c in range(NC)]
                ksls = [pl.ds(pl.multiple_of(c * L4 + ws // NC, ATT_RADIUS // NC), KB // NC) for c in range(NC)]
            elif d == NC:
                qsls = [pl.ds(pl.multiple_of(cls * L4 + q0, QB), QB)]
                ksls = [pl.ds(pl.multiple_of(cls * L4 + ws, ATT_RADIUS), KB)]
            else:
                base = (cls & (NC - 1)) * L4 + (cls >> _log2(NC))
                qsls = [pl.ds(base + NC * q0, QB, stride=NC)]
                ksls = [pl.ds(base + NC * ws, KB, stride=NC)]
            q = jnp.concatenate([q_ref[sl, :] for sl in qsls], axis=0)
            kw = jnp.concatenate([k_ref[sl, :] for sl in ksls], axis=0)
            kb = kw.astype(BF16)
            bias = bias_ref[1 if d == 1 else 0, (q0 - ws) >> _log2(ATT_RADIUS), :QB]
            q_heads = (jnp.where(head0, q, 0.0), jnp.where(head0, 0.0, q))
            s = [_dot_nt(qh.astype(BF16), kb) + bias for qh in q_heads]
            return qsls, ksls, s

        def softmax_pv(qsls, ksls, s):
            vw = jnp.concatenate([v_ref[sl, :] for sl in ksls], axis=0)
            v_ones = jnp.concatenate([vw.astype(BF16), jnp.ones((KB, LANES), BF16)], axis=1)
            m_h = [jnp.max(t, axis=-1, keepdims=True) for t in s]
            pv = [_dot(jnp.exp2(t - m).astype(BF16), v_ones) for t, m in zip(s, m_h)]
            acc_b = jnp.where(head0, pv[0][:, :LANES], pv[1][:, :LANES])
            m_b = jnp.where(head0, m_h[0], m_h[1])
            l_b = jnp.where(head0, pv[0][:, LANES:], pv[1][:, LANES:])
            return qsls, acc_b, m_b, l_b

        def load(ref, sls):
            return jnp.concatenate([ref[sl, :] for sl in sls], axis=0)

        def store(ref, sls, val):
            n = val.shape[0] // len(sls)
            for i, sl in enumerate(sls):
                ref[sl, :] = val[i * n:(i + 1) * n]

        unroll = ATT_UNROLL[pi]

        def body(n, carry, first=first, last=last, unroll=unroll):
            staged = [scores(n * unroll + u) for u in range(unroll)]
            blocks = [softmax_pv(*st) for st in staged]
            for qsls, acc_b, m_b, l_b in blocks:
                if first:
                    acc, m_new, l_new = acc_b, m_b, l_b
                else:
                    m_old = load(m_ref, qsls)
                    m_new = jnp.maximum(m_old, m_b)
                    w_old = jnp.exp2(m_old - m_new)
                    w_blk = jnp.exp2(m_b - m_new)
                    acc = load(o_ref, qsls) * w_old + acc_b * w_blk
                    l_new = load(l_ref, qsls) * w_old + l_b * w_blk
                if last:
                    store(o_ref, qsls, acc / l_new)
                else:
                    store(o_ref, qsls, acc)
                    store(m_ref, qsls, m_new)
                    store(l_ref, qsls, l_new)
            return carry

        lax.fori_loop(0, S // (QB * unroll), body, 0)


def _attention(att_slab, B, S):
    T = B * S
    ncol = ATT_WIDTH // LANES
    return pl.pallas_call(
        functools.partial(_att_kernel, S=S),
        grid=(B, ncol),
        in_specs=[
            pl.BlockSpec((S, LANES), lambda b, h: (b, h)),
            pl.BlockSpec((S, LANES), lambda b, h: (b, ncol + h)),
            pl.BlockSpec((S, LANES), lambda b, h: (b, 2 * ncol + h)),
        ],
        out_specs=pl.BlockSpec((S, LANES), lambda b, h: (b, h)),
        out_shape=jax.ShapeDtypeStruct((T, ATT_WIDTH), F32),
        scratch_shapes=[pltpu.VMEM((S, LANES), F32), pltpu.VMEM((S, LANES), F32),
                        pltpu.VMEM((2, 3, 2 * ATT_QB, ATT_KB), F32)],
        compiler_params=_cparams(("arbitrary", "arbitrary")),
        name="dilated_attention",
    )(att_slab, att_slab, att_slab)


PACK_WORDS = D_MODEL // 2
ROW_TILE = PACK_WORDS // LANES
HIGH_HALF = -65536


def _pack_rows(x):
    bits = lambda v: lax.bitcast_convert_type(v.astype(BF16).astype(F32), jnp.int32)
    low = (bits(x[:, :PACK_WORDS]) >> 16) & 0xFFFF
    return (bits(x[:, PACK_WORDS:]) & HIGH_HALF) | low


def _unpack_rows(w):
    low = lax.bitcast_convert_type(w << 16, F32)
    high = lax.bitcast_convert_type(w & HIGH_HALF, F32)
    return jnp.concatenate([low, high], axis=1).astype(BF16)


def _to_row_tiles(ref, w):
    n = w.shape[0]
    for j in range(ROW_TILE):
        ref[pl.ds(j, n, stride=ROW_TILE), :] = w[:, j * LANES:(j + 1) * LANES]


def _from_row_tiles(ref, n):
    return jnp.concatenate([ref[pl.ds(j, n, stride=ROW_TILE), :] for j in range(ROW_TILE)], axis=1)


def _tile_copy(src_ref, src_row, dst_ref, dst_row, sem):
    src = pl.ds(pl.multiple_of(src_row * ROW_TILE, ROW_TILE), ROW_TILE)
    dst = pl.ds(pl.multiple_of(dst_row * ROW_TILE, ROW_TILE), ROW_TILE)
    return pltpu.make_async_copy(src_ref.at[src], dst_ref.at[dst], sem)


def _outproj_kernel(of_ref, ob_ref, gg_ref, att_ref, x_ref, gnw_ref, wo1_ref, wo2_ref,
                    n2_ref, wr_ref, br_ref, h_ref, u_ref, lg_ref, stage_ref):
    rows = stage_ref.shape[1] // ATT_CLASSES
    for j in range(ATT_WIDTH // LANES):
        for c in range(ATT_CLASSES):
            stage_ref[j, pl.ds(c, rows, stride=ATT_CLASSES), :] = att_ref[c, :, j * LANES:(j + 1) * LANES]
    att = jnp.concatenate([stage_ref[j] for j in range(ATT_WIDTH // LANES)], axis=1)
    o = of_ref[...].astype(F32) + ob_ref[...].astype(F32)
    gate = gg_ref[...].astype(F32)
    gnw = gnw_ref[...]
    parts = []
    for h in range(GLA_HEADS):
        sl = slice(h * GLA_DV, (h + 1) * GLA_DV)
        parts.append(_rms(o[:, sl], gnw))
    y = jnp.concatenate(parts, axis=1) * (gate / (1.0 + jnp.exp(-gate)))
    mix = _dot(y.astype(BF16), wo1_ref[...]) + _dot(att.astype(BF16), wo2_ref[...])
    h = x_ref[...] + mix
    h_ref[...] = h
    u = _rms(h, n2_ref[...])
    _to_row_tiles(u_ref, _pack_rows(u))
    u_hi = u.astype(BF16)
    u_lo = (u - u_hi.astype(F32)).astype(BF16)
    hi_both = _dot_nt(wr_ref[...], u_hi)
    lg_ref[...] = (hi_both[:LANES] + hi_both[LANES:] + _dot_nt(wr_ref[:LANES], u_lo)) + br_ref[...]


def _outproj(o_f, o_b, gate, att_out, x2, gla_norm_w, w_out, norm2_w, wr, br, tm=512):
    T = x2.shape[0]
    nS = att_out.shape[2] * ATT_CLASSES // tm
    row = lambda i: (i, 0)
    const = lambda i: (0, 0)
    wo = w_out.astype(BF16)
    wr_hi = wr.astype(BF16)
    wr_lo = (wr - wr_hi.astype(F32)).astype(BF16)
    wr = jnp.concatenate([wr_hi, wr_lo], axis=0)
    return pl.pallas_call(
        _outproj_kernel,
        grid=(T // tm,),
        in_specs=[
            pl.BlockSpec((tm, GLA_VAL_WIDTH), row),
            pl.BlockSpec((tm, GLA_VAL_WIDTH), row),
            pl.BlockSpec((tm, GLA_VAL_WIDTH), row),
            pl.BlockSpec((None, ATT_CLASSES, tm // ATT_CLASSES, ATT_WIDTH), lambda i: (i // nS, 0, i % nS, 0)),
            pl.BlockSpec((tm, D_MODEL), row),
            pl.BlockSpec((1, GLA_DV), const),
            pl.BlockSpec((GLA_VAL_WIDTH, D_MODEL), const),
            pl.BlockSpec((ATT_WIDTH, D_MODEL), lambda i: (GLA_VAL_WIDTH // ATT_WIDTH, 0)),
            pl.BlockSpec((1, D_MODEL), const),
            pl.BlockSpec((2 * LANES, D_MODEL), const),
            pl.BlockSpec((LANES, 1), const),
        ],
        out_specs=[
            pl.BlockSpec((tm, D_MODEL), row),
            pl.BlockSpec((tm * ROW_TILE, LANES), row),
            pl.BlockSpec((LANES, tm), lambda i: (0, i)),
        ],
        out_shape=[
            jax.ShapeDtypeStruct((T, D_MODEL), F32),
            jax.ShapeDtypeStruct((T * ROW_TILE, LANES), jnp.int32),
            jax.ShapeDtypeStruct((LANES, T), F32),
        ],
        scratch_shapes=[pltpu.VMEM((ATT_WIDTH // LANES, tm, LANES), F32)],
        compiler_params=_cparams(("arbitrary",)),
        name="outproj",
    )(o_f, o_b, gate, att_out, x2, gla_norm_w[None, :], wo, wo,
      norm2_w[None, :], wr, br)


INFO_E1, INFO_E2, INFO_R1, INFO_R2, INFO_W1, INFO_W2 = range(6)
ROUTE_ROWS = 40


def _route_kernel(lg_ref, info_ref, cnt_ref, carry_ref):
    @pl.when(pl.program_id(0) == 0)
    def _():
        carry_ref[...] = jnp.zeros_like(carry_ref)

    lg = lg_ref[:ROUTE_ROWS, :]
    tr = lg.shape[1]
    row = lax.broadcasted_iota(jnp.int32, (ROUTE_ROWS, tr), 0)
    big = jnp.int32(1 << 20)
    is_g = (row >= MOE_N_EXPERTS) & (row < MOE_N_EXPERTS + MOE_GROUPS)
    gl = jnp.where(is_g, lg, -jnp.inf)
    gmax = jnp.max(gl, axis=0, keepdims=True)
    gsel = jnp.min(jnp.where(gl == gmax, row - MOE_N_EXPERTS, big), axis=0, keepdims=True)
    g_w = 1.0 / jnp.sum(jnp.where(is_g, jnp.exp(lg - gmax), 0.0), axis=0, keepdims=True)
    in_grp = (row < MOE_N_EXPERTS) & ((row >> MOE_GROUP_SHIFT) == gsel)
    el = jnp.where(in_grp, lg, -jnp.inf)
    v1 = jnp.max(el, axis=0, keepdims=True)
    i1 = jnp.min(jnp.where(el == v1, row, big), axis=0, keepdims=True)
    el2 = jnp.where(row == i1, -jnp.inf, el)
    v2 = jnp.max(el2, axis=0, keepdims=True)
    i2 = jnp.min(jnp.where(el2 == v2, row, big), axis=0, keepdims=True)
    t = jnp.exp(v2 - v1)
    w1 = g_w * (1.0 / (1.0 + t))
    w2 = g_w * (t / (1.0 + t))

    erow = lax.broadcasted_iota(jnp.int32, (MOE_N_EXPERTS, tr), 0)
    hit1 = erow == i1
    hit2 = erow == i2
    member = jnp.where(hit1 | hit2, 1.0, 0.0)
    r = lax.broadcasted_iota(jnp.int32, (tr, tr), 0)
    c = lax.broadcasted_iota(jnp.int32, (tr, tr), 1)
    earlier = jnp.where(r < c, 1.0, 0.0).astype(BF16)
    carry = carry_ref[...]
    prefix = _dot(member.astype(BF16), earlier) + carry[:, 0:1]
    rank1 = jnp.sum(jnp.where(hit1, prefix, 0.0), axis=0, keepdims=True)
    rank2 = jnp.sum(jnp.where(hit2, prefix, 0.0), axis=0, keepdims=True)
    carry = carry + jnp.sum(member, axis=1, keepdims=True)
    carry_ref[...] = carry
    cnt_ref[...] = carry

    zero = jnp.zeros_like(w1)
    info_ref[...] = jnp.concatenate([i1.astype(F32), i2.astype(F32), rank1, rank2, w1, w2, zero, zero], axis=0)


def _route(logits_t, tr=1024):
    T = logits_t.shape[1]
    return pl.pallas_call(
        _route_kernel,
        grid=(T // tr,),
        in_specs=[pl.BlockSpec((LANES, tr), lambda i: (0, i))],
        out_specs=[pl.BlockSpec((8, tr), lambda i: (0, i)),
                   pl.BlockSpec((MOE_N_EXPERTS, LANES), lambda i: (0, 0))],
        out_shape=[jax.ShapeDtypeStruct((8, T), F32), jax.ShapeDtypeStruct((MOE_N_EXPERTS, LANES), F32)],
        scratch_shapes=[pltpu.VMEM((MOE_N_EXPERTS, LANES), F32)],
        compiler_params=_cparams(("arbitrary",)),
        name="route",
    )(logits_t)


ROW_UNROLL = 16


def _dispatch_kernel(dest_ref, pend_ref, u_ref, xs_ref, zbuf, sem, zsem, *, td, T, nblk):
    @pl.when(pl.program_id(0) == 0)
    def _():
        zbuf[...] = jnp.zeros_like(zbuf)
        n_used = pend_ref[MOE_N_EXPERTS - 1] >> MOE_ROWS_SHIFT

        def zero_copy(blk):
            start = pl.multiple_of(blk * (MOE_ROWS * ROW_TILE), MOE_ROWS * ROW_TILE)
            return pltpu.make_async_copy(zbuf, xs_ref.at[pl.ds(start, MOE_ROWS * ROW_TILE)], zsem)

        def each_pad_block(fn):
            def per_expert(e, carry):
                prev = jnp.where(e > 0, pend_ref[jnp.maximum(e - 1, 0)], 0)

                @pl.when(pend_ref[e] > prev)
                def _():
                    fn((pend_ref[e] >> MOE_ROWS_SHIFT) - 1)
                return carry

            def per_tail(j, carry):
                @pl.when(n_used + j < nblk)
                def _():
                    fn(n_used + j)
                return carry

            lax.fori_loop(0, MOE_N_EXPERTS, per_expert, 0)
            lax.fori_loop(0, MOE_N_EXPERTS, per_tail, 0)

        each_pad_block(lambda blk: zero_copy(blk).start())
        each_pad_block(lambda blk: zero_copy(blk).wait())

    base = pl.program_id(0) * td

    def issue(g, carry):
        for j in range(ROW_UNROLL):
            r = g * ROW_UNROLL + j
            for k in range(MOE_TOP_K):
                _tile_copy(u_ref, r, xs_ref, dest_ref[k * T + base + r], sem).start(priority=k)
        return carry

    lax.fori_loop(0, td // ROW_UNROLL, issue, 0)
    for k in range(MOE_TOP_K):
        pltpu.make_async_copy(u_ref, xs_ref.at[pl.ds(0, td * ROW_TILE)], sem).wait()


def _dispatch(dest, pend, u2, cap, td=2048):
    T = u2.shape[0] // ROW_TILE
    return pl.pallas_call(
        functools.partial(_dispatch_kernel, td=td, T=T, nblk=cap // MOE_ROWS),
        grid_spec=pltpu.PrefetchScalarGridSpec(
            num_scalar_prefetch=2,
            grid=(T // td,),
            in_specs=[pl.BlockSpec((td * ROW_TILE, LANES), lambda i, d, z: (i, 0))],
            out_specs=pl.BlockSpec(memory_space=pl.ANY),
            scratch_shapes=[pltpu.VMEM((MOE_ROWS * ROW_TILE, LANES), jnp.int32),
                            pltpu.SemaphoreType.DMA(()), pltpu.SemaphoreType.DMA(())],
        ),
        out_shape=jax.ShapeDtypeStruct((cap * ROW_TILE, LANES), jnp.int32),
        compiler_params=_cparams(("arbitrary",)),
        name="dispatch",
    )(dest, pend, u2)


EXPERT_GROUP = 4


def _expert_kernel(pend_ref, xs_hbm, wg_hbm, wu_hbm, wd_hbm, ys_hbm,
                   xbuf, ybuf, zbuf, stage_g, stage_u, stage_d, wgb, wub, wdb, xsem, ysem, wsem, zsem, *, nblk):
    last = MOE_N_EXPERTS - 1
    n_used = pend_ref[last] >> MOE_ROWS_SHIFT
    n_pairs = (n_used + EXPERT_GROUP - 1) >> _log2(EXPERT_GROUP)
    block_rows = MOE_ROWS * ROW_TILE

    def rows_of(b):
        return pl.ds(pl.multiple_of(b * block_rows, block_rows), block_rows)

    def x_copy(b, slot):
        return pltpu.make_async_copy(xs_hbm.at[rows_of(b)], xbuf.at[slot], xsem.at[slot])

    def y_copy(b, slot):
        return pltpu.make_async_copy(ybuf.at[slot], ys_hbm.at[rows_of(b)], ysem.at[slot])

    def zero_copy(b):
        return pltpu.make_async_copy(zbuf, ys_hbm.at[rows_of(b)], zsem)

    def weight_copies(e):
        return (pltpu.make_async_copy(wg_hbm.at[e], stage_g, wsem.at[0]),
                pltpu.make_async_copy(wu_hbm.at[e], stage_u, wsem.at[1]),
                pltpu.make_async_copy(wd_hbm.at[e], stage_d, wsem.at[2]))

    def owner(start, row):
        return lax.while_loop(lambda e: (e < last) & (pend_ref[e] <= row), lambda e: e + 1, start)

    for c in weight_copies(owner(0, 0)):
        c.start()
    for i in range(EXPERT_GROUP):
        x_copy(i, i).start()

    zbuf[...] = jnp.zeros_like(zbuf)

    def tail(fn):
        def step(b, carry):
            fn(b)
            return carry
        lax.fori_loop(n_pairs * EXPERT_GROUP, nblk, step, 0)

    tail(lambda b: zero_copy(b).start())

    def body(p, carry):
        cur, run = carry
        half = (p & 1) * EXPERT_GROUP
        for i in range(EXPERT_GROUP):
            x_copy(p * EXPERT_GROUP + i, half + i).wait()

        @pl.when(p + 1 < n_pairs)
        def _():
            for i in range(EXPERT_GROUP):
                x_copy((p + 1) * EXPERT_GROUP + i, EXPERT_GROUP - half + i).start()

        slots = []
        for i in range(EXPERT_GROUP):
            b = p * EXPERT_GROUP + i
            e = jnp.where(b < n_used, owner(jnp.maximum(cur, 0), b * MOE_ROWS), cur)
            fresh = e != cur
            run = run + fresh.astype(jnp.int32)
            slot = run & (EXPERT_GROUP - 1)

            @pl.when(fresh)
            def _(e=e, slot=slot):
                for c in weight_copies(e):
                    c.wait()
                wgb[slot] = stage_g[...].astype(BF16)
                wub[slot] = stage_u[...].astype(BF16)
                wdb[slot] = stage_d[...].astype(BF16)

                @pl.when(pend_ref[e] < pend_ref[last])
                def _():
                    for c in weight_copies(owner(e + 1, pend_ref[e])):
                        c.start(priority=1)

            cur = e
            slots.append(slot)

        @pl.when(p >= 2)
        def _():
            for i in range(EXPERT_GROUP):
                y_copy((p - 2) * EXPERT_GROUP + i, half + i).wait()

        for i in range(EXPERT_GROUP):
            xb = _unpack_rows(_from_row_tiles(xbuf.at[half + i], MOE_ROWS))
            g = _dot(xb, wgb[slots[i]])
            u = _dot(xb, wub[slots[i]])
            hid = (g / (1.0 + jnp.exp(-g))) * u
            _to_row_tiles(ybuf.at[half + i], _pack_rows(_dot(hid.astype(BF16), wdb[slots[i]])))
        for i in range(EXPERT_GROUP):
            y_copy(p * EXPERT_GROUP + i, half + i).start()
        return cur, run

    lax.fori_loop(0, n_pairs, body, (jnp.int32(-1), jnp.int32(-1)))

    def drain(p):
        for i in range(EXPERT_GROUP):
            y_copy(p * EXPERT_GROUP + i, (p & 1) * EXPERT_GROUP + i).wait()

    @pl.when(n_pairs >= 2)
    def _():
        drain(n_pairs - 2)
    drain(n_pairs - 1)
    tail(lambda b: zero_copy(b).wait())


def _experts(pend, xs, w_gate, w_up, w_down):
    cap = xs.shape[0] // ROW_TILE
    nblk = cap // MOE_ROWS
    assert nblk % EXPERT_GROUP == 0
    block = (MOE_ROWS * ROW_TILE, LANES)
    anywhere = pl.BlockSpec(memory_space=pl.ANY)
    return pl.pallas_call(
        functools.partial(_expert_kernel, nblk=nblk),
        grid_spec=pltpu.PrefetchScalarGridSpec(
            num_scalar_prefetch=1,
            grid=(1,),
            in_specs=[anywhere, anywhere, anywhere, anywhere],
            out_specs=anywhere,
            scratch_shapes=[pltpu.VMEM((2 * EXPERT_GROUP,) + block, jnp.int32),
                            pltpu.VMEM((2 * EXPERT_GROUP,) + block, jnp.int32),
                            pltpu.VMEM(block, jnp.int32),
                            pltpu.VMEM((D_MODEL, MOE_D_FF), F32),
                            pltpu.VMEM((D_MODEL, MOE_D_FF), F32),
                            pltpu.VMEM((MOE_D_FF, D_MODEL), F32),
                            pltpu.VMEM((EXPERT_GROUP, D_MODEL, MOE_D_FF), BF16),
                            pltpu.VMEM((EXPERT_GROUP, D_MODEL, MOE_D_FF), BF16),
                            pltpu.VMEM((EXPERT_GROUP, MOE_D_FF, D_MODEL), BF16),
                            pltpu.SemaphoreType.DMA((2 * EXPERT_GROUP,)),
                            pltpu.SemaphoreType.DMA((2 * EXPERT_GROUP,)),
                            pltpu.SemaphoreType.DMA((3,)),
                            pltpu.SemaphoreType.DMA(())],
        ),
        out_shape=jax.ShapeDtypeStruct((cap * ROW_TILE, LANES), jnp.int32),
        compiler_params=_cparams(("arbitrary",)),
        name="experts",
    )(pend, xs, w_gate, w_up, w_down)


def _combine_kernel(dest_ref, ys_ref, info_ref, h_ref, fw_ref, o_ref, buf, sem, *, tc, T):
    i = pl.program_id(0)
    n = pl.num_programs(0)

    def issue(step, slot):
        base = step * tc

        def body(g, carry):
            for j in range(ROW_UNROLL):
                r = g * ROW_UNROLL + j
                for k in range(MOE_TOP_K):
                    _tile_copy(ys_ref, dest_ref[k * T + base + r], buf.at[slot], MOE_TOP_K * r + k,
                               sem.at[slot]).start(priority=k)
            return carry

        lax.fori_loop(0, tc // ROW_UNROLL, body, 0)

    @pl.when(i == 0)
    def _():
        issue(0, 0)

    slot = i % 2

    @pl.when(i + 1 < n)
    def _():
        issue(i + 1, 1 - slot)

    pltpu.make_async_copy(ys_ref.at[pl.ds(0, MOE_TOP_K * tc * ROW_TILE)], buf.at[slot], sem.at[slot]).wait()

    info_t = jnp.concatenate([info_ref[...]] * (LANES // 8), axis=0).T
    w1 = info_t[:, INFO_W1:INFO_W1 + 1]
    w2 = info_t[:, INFO_W2:INFO_W2 + 1]
    def gathered(k):
        step = MOE_TOP_K * ROW_TILE
        return jnp.concatenate([buf[slot, pl.ds(k * ROW_TILE + j, tc, stride=step), :] for j in range(ROW_TILE)],
                               axis=1)

    y1 = _unpack_rows(gathered(0)).astype(F32)
    y2 = _unpack_rows(gathered(1)).astype(F32)
    h = h_ref[...] + (y1 * w1 + y2 * w2)
    o_ref[...] = _rms(h, fw_ref[...])


def _combine(dest, ys, info, h, final_w, tc=512):
    T = h.shape[0]
    return pl.pallas_call(
        functools.partial(_combine_kernel, tc=tc, T=T),
        grid_spec=pltpu.PrefetchScalarGridSpec(
            num_scalar_prefetch=1,
            grid=(T // tc,),
            in_specs=[pl.BlockSpec(memory_space=pl.ANY),
                      pl.BlockSpec((8, tc), lambda i, d: (0, i)),
                      pl.BlockSpec((tc, D_MODEL), lambda i, d: (i, 0)),
                      pl.BlockSpec((1, D_MODEL), lambda i, d: (0, 0))],
            out_specs=pl.BlockSpec((tc, D_MODEL), lambda i, d: (i, 0)),
            scratch_shapes=[pltpu.VMEM((2, MOE_TOP_K * tc * ROW_TILE, LANES), jnp.int32),
                            pltpu.SemaphoreType.DMA((2,))],
        ),
        out_shape=jax.ShapeDtypeStruct((T, D_MODEL), F32),
        compiler_params=_cparams(("arbitrary",)),
        name="combine",
    )(dest, ys, info, h, final_w[None, :])


def _plan_kernel(info_ref, cnt_ref, dest_ref, pend_ref):
    cnt = cnt_ref[...].astype(jnp.int32)
    nblk_e = ((cnt + (MOE_ROWS - 1)) >> MOE_ROWS_SHIFT).astype(F32)
    r = lax.broadcasted_iota(jnp.int32, (MOE_N_EXPERTS, MOE_N_EXPERTS), 0)
    c = lax.broadcasted_iota(jnp.int32, (MOE_N_EXPERTS, MOE_N_EXPERTS), 1)
    before = jnp.where(c < r, 1.0, 0.0).astype(BF16)
    first_blk = _dot(before, nblk_e.astype(BF16))
    pstart = first_blk[:, 0:1] * float(MOE_ROWS)
    pend_ref[...] = ((first_blk + nblk_e) * float(MOE_ROWS)).astype(jnp.int32)

    info = info_ref[...]
    erow = lax.broadcasted_iota(jnp.int32, (MOE_N_EXPERTS, info.shape[1]), 0)
    start_of = lambda e: jnp.sum(jnp.where(erow == e.astype(jnp.int32), pstart, 0.0), axis=0, keepdims=True)
    d1 = info[INFO_R1:INFO_R1 + 1] + start_of(info[INFO_E1:INFO_E1 + 1])
    d2 = info[INFO_R2:INFO_R2 + 1] + start_of(info[INFO_E2:INFO_E2 + 1])
    zero = jnp.zeros_like(d1)
    dest_ref[...] = jnp.concatenate([d1, d2] + [zero] * 6, axis=0).astype(jnp.int32)


def _plan(info, counts, tr=2048):
    T = info.shape[1]
    dest8, pend = pl.pallas_call(
        _plan_kernel,
        grid=(T // tr,),
        in_specs=[pl.BlockSpec((8, tr), lambda i: (0, i)),
                  pl.BlockSpec((MOE_N_EXPERTS, LANES), lambda i: (0, 0))],
        out_specs=[pl.BlockSpec((8, tr), lambda i: (0, i)),
                   pl.BlockSpec((MOE_N_EXPERTS, LANES), lambda i: (0, 0))],
        out_shape=[jax.ShapeDtypeStruct((8, T), jnp.int32),
                   jax.ShapeDtypeStruct((MOE_N_EXPERTS, LANES), jnp.int32)],
        compiler_params=_cparams(("arbitrary",)),
        name="plan",
    )(info, counts)
    return dest8[:MOE_TOP_K].reshape(-1), pend[:, 0]


def _moe_capacity(T):
    return (-(-(T * MOE_TOP_K) // MOE_ROWS) + MOE_N_EXPERTS) * MOE_ROWS


def _router_weights(router_group_w, router_group_b, router_expert_w, router_expert_b):
    we = jnp.transpose(router_expert_w, (0, 2, 1)).reshape(MOE_N_EXPERTS, D_MODEL)
    pad = LANES - MOE_N_EXPERTS - MOE_GROUPS
    wr = jnp.concatenate([we, router_group_w.T, jnp.zeros((pad, D_MODEL), F32)], axis=0)
    br = jnp.concatenate([router_expert_b.reshape(-1), router_group_b, jnp.zeros((pad,), F32)])[:, None]
    return wr, br


def kernel(x, norm1_w, w_in, gla_fwd_gate_w, gla_fwd_gate_b, gla_bwd_gate_w, gla_bwd_gate_b,
           gla_norm_w, w_out, norm2_w, router_group_w, router_group_b, router_expert_w,
           router_expert_b, expert_w_gate, expert_w_up, expert_w_down, final_norm_w):
    B, S, D = x.shape
    T = B * S
    assert norm1_w.shape[0] == 1, "single-layer trunk: the final norm is fused into the combine step"
    h = x.reshape(T, D)
    gla_slab, gate, loga, att_slab = _inproj(h, S, norm1_w[0], w_in[0], gla_fwd_gate_w[0], gla_fwd_gate_b[0],
                                       gla_bwd_gate_w[0], gla_bwd_gate_b[0])
    o_f, o_b = _gla(gla_slab, loga, B, S)
    att_out = _attention(att_slab.reshape(T, 3 * ATT_WIDTH), B, S)
    att_out = att_out.reshape(B, ATT_CLASSES, S // ATT_CLASSES, ATT_WIDTH)
    wr, br = _router_weights(router_group_w[0], router_group_b[0], router_expert_w[0], router_expert_b[0])
    h, u2, logits = _outproj(o_f, o_b, gate, att_out, h, gla_norm_w[0], w_out[0], norm2_w[0], wr, br)
    info, counts = _route(logits)
    dest, pend = _plan(info, counts)
    xs = _dispatch(dest, pend, u2, _moe_capacity(T))
    ys = _experts(pend, xs, expert_w_gate[0], expert_w_up[0], expert_w_down[0])
    out = _combine(dest, ys, info, h, final_norm_w)
    return out.reshape(B, S, D)
```

```python
import functools

import jax
import jax.numpy as jnp
import numpy as np
from jax import lax
from jax.experimental import pallas as pl
from jax.experimental.pallas import tpu as pltpu

F32 = jnp.float32
BF16 = jnp.bfloat16

D_MODEL = 1024
GLA_HEADS = 4
GLA_DV = 128
GLA_DK = 64
GLA_KEY_WIDTH = GLA_HEADS * GLA_DK
GLA_VAL_WIDTH = GLA_HEADS * GLA_DV
GLA_GATE_RANK = 16
GLA_TAU = 16.0
GLA_CHUNK = 64
ATT_WIDTH = 512
ATT_HEAD_DIM = 64
ATT_HEADS = 8
ROT_DIM = 16
ROPE_THETA = 500000.0
DILATED_PATTERNS = ((128, 1), (512, 4), (2048, 16))
ATT_RADIUS = 64
MOE_GROUPS = 4
MOE_EXPERTS_PER_GROUP = 8
MOE_N_EXPERTS = 32
MOE_TOP_K = 2
MOE_D_FF = 512
EPS = 1e-6
NEG_INF = -1e30
LOG2E = 1.4426950408889634

LANES = 128
MOE_ROWS = 256


def _log2(n):
    assert n & (n - 1) == 0, n
    return n.bit_length() - 1


GLA_CHUNK_SHIFT = _log2(GLA_CHUNK)
GLA_DK_SHIFT = _log2(GLA_DK)
MOE_ROWS_SHIFT = _log2(MOE_ROWS)
MOE_GROUP_SHIFT = _log2(MOE_EXPERTS_PER_GROUP)
VMEM_LIMIT = 56 * 1024 * 1024


def _cparams(sem):
    return pltpu.CompilerParams(dimension_semantics=sem, vmem_limit_bytes=VMEM_LIMIT)


def _dot(a, b):
    return jnp.dot(a, b, preferred_element_type=F32)


def _dot_nt(a, b):
    return lax.dot_general(a, b, (((1,), (1,)), ((), ())), preferred_element_type=F32)


def _dot_tn(a, b):
    return lax.dot_general(a, b, (((0,), (0,)), ((), ())), preferred_element_type=F32)


def _rms(x, w):
    return x * lax.rsqrt(jnp.mean(x * x, axis=-1, keepdims=True) + EPS) * w


def _inproj_kernel(x_ref, n1_ref, wg_ref, wlr_ref, wa_ref, gw_ref, gb_ref,
                   rc_ref, rs1_ref, rs2_ref, gla_ref, gate_ref, loga_ref, att_ref, stage_ref, wgb, wlrb):
    @pl.when(pl.program_id(0) == 0)
    def _():
        wgb[...] = wg_ref[...].astype(BF16)
        wlrb[...] = wlr_ref[...].astype(BF16)

    x = x_ref[...]
    ub = _rms(x, n1_ref[...]).astype(BF16)
    g = _dot(ub, wgb[...])
    qkv = 2 * GLA_KEY_WIDTH + GLA_VAL_WIDTH
    gla_ref[:, :GLA_KEY_WIDTH] = g[:, :GLA_KEY_WIDTH] * (GLA_DK ** -0.5)
    gla_ref[:, GLA_KEY_WIDTH:] = g[:, GLA_KEY_WIDTH:qkv]
    gate_ref[...] = g[:, qkv:].astype(BF16)
    lr = _dot(ub, wlrb[...])
    gate = _dot(lr.astype(BF16), gw_ref[...]) + gb_ref[...]
    loga_ref[...] = (jnp.minimum(gate, 0.0) - jnp.log(1.0 + jnp.exp(-jnp.abs(gate)))) * (1.0 / GLA_TAU)
    a = _dot(ub, wa_ref[...])
    qk = a[:, :2 * ATT_WIDTH]
    reps = 2 * ATT_WIDTH // LANES
    c = jnp.concatenate([rc_ref[...]] * reps, axis=1)
    s1 = jnp.concatenate([rs1_ref[...]] * reps, axis=1)
    s2 = jnp.concatenate([rs2_ref[...]] * reps, axis=1)
    half = ROT_DIM // 2
    n = 2 * ATT_WIDTH
    roped = qk * c + pltpu.roll(qk, n - half, 1) * s1 + pltpu.roll(qk, half, 1) * s2
    qkv = jnp.concatenate([roped[:, :ATT_WIDTH] * (ATT_HEAD_DIM ** -0.5 * LOG2E), roped[:, ATT_WIDTH:],
                           a[:, 2 * ATT_WIDTH:]], axis=1)
    rows = x.shape[0] // ATT_CLASSES
    for j in range(3 * ATT_WIDTH // LANES):
        cols = slice(j * LANES, (j + 1) * LANES)
        stage_ref[j] = qkv[:, cols]
        for c in range(ATT_CLASSES):
            att_ref[c, :, cols] = stage_ref[j, pl.ds(c, rows, stride=ATT_CLASSES), :]


def _rope_lane_tables(S):
    half = ROT_DIM // 2
    inv = np.float32(ROPE_THETA) ** (-(np.arange(0, ROT_DIM, 2, dtype=np.float32) / np.float32(ROT_DIM)))
    ang = np.arange(S, dtype=np.float32)[:, None] * inv[None, :].astype(np.float32)
    cos, sin = np.cos(ang), np.sin(ang)
    ones = np.ones((S, ATT_HEAD_DIM - ROT_DIM), np.float32)
    zeros = np.zeros((S, ATT_HEAD_DIM - ROT_DIM), np.float32)
    zeros8 = np.zeros((S, half), np.float32)
    rep = LANES // ATT_HEAD_DIM
    c = np.tile(np.concatenate([cos, cos, ones], axis=1), (1, rep))
    s1 = np.tile(np.concatenate([-sin, zeros8, zeros], axis=1), (1, rep))
    s2 = np.tile(np.concatenate([zeros8, sin, zeros], axis=1), (1, rep))
    return jnp.asarray(c), jnp.asarray(s1), jnp.asarray(s2)


def _inproj(x2, S, norm1_w, w_in, wf, bfw, wb, bbw, tm=512):
    T = x2.shape[0]
    o_lr = 2 * GLA_KEY_WIDTH + 2 * GLA_VAL_WIDTH
    o_att = o_lr + 2 * GLA_GATE_RANK
    wa = w_in[:, o_att:].astype(BF16)
    zeros = jnp.zeros((GLA_GATE_RANK, GLA_KEY_WIDTH), F32)
    gw = jnp.concatenate([jnp.concatenate([wf, zeros], axis=1), jnp.concatenate([zeros, wb], axis=1),
                          jnp.zeros((LANES - 2 * GLA_GATE_RANK, 2 * GLA_KEY_WIDTH), F32)], axis=0).astype(BF16)
    gb = jnp.concatenate([bfw, bbw])[None, :]
    rc, rs1, rs2 = _rope_lane_tables(S)
    nS = S // tm
    row = lambda i: (i, 0)
    const = lambda i: (0, 0)
    pos = lambda i: (i % nS, 0)
    return pl.pallas_call(
        _inproj_kernel,
        grid=(T // tm,),
        in_specs=[
            pl.BlockSpec((tm, D_MODEL), row),
            pl.BlockSpec((1, D_MODEL), const),
            pl.BlockSpec((D_MODEL, o_lr), const),
            pl.BlockSpec((D_MODEL, LANES), lambda i: (0, o_lr // LANES)),
            pl.BlockSpec((D_MODEL, 3 * ATT_WIDTH), const),
            pl.BlockSpec((LANES, 2 * GLA_KEY_WIDTH), const),
            pl.BlockSpec((1, 2 * GLA_KEY_WIDTH), const),
            pl.BlockSpec((tm, LANES), pos),
            pl.BlockSpec((tm, LANES), pos),
            pl.BlockSpec((tm, LANES), pos),
        ],
        out_specs=[
            pl.BlockSpec((tm, o_lr - GLA_VAL_WIDTH), row),
            pl.BlockSpec((tm, GLA_VAL_WIDTH), row),
            pl.BlockSpec((tm, 2 * GLA_KEY_WIDTH), row),
            pl.BlockSpec((None, ATT_CLASSES, tm // ATT_CLASSES, 3 * ATT_WIDTH),
                         lambda i: (i // nS, 0, i % nS, 0)),
        ],
        out_shape=[
            jax.ShapeDtypeStruct((T, o_lr - GLA_VAL_WIDTH), F32),
            jax.ShapeDtypeStruct((T, GLA_VAL_WIDTH), BF16),
            jax.ShapeDtypeStruct((T, 2 * GLA_KEY_WIDTH), F32),
            jax.ShapeDtypeStruct((T // S, ATT_CLASSES, S // ATT_CLASSES, 3 * ATT_WIDTH), F32),
        ],
        scratch_shapes=[pltpu.VMEM((3 * ATT_WIDTH // LANES, tm, LANES), F32),
                        pltpu.VMEM((D_MODEL, o_lr), BF16), pltpu.VMEM((D_MODEL, LANES), BF16)],
        compiler_params=_cparams(("arbitrary",)),
        name="inproj",
    )(x2, norm1_w[None, :], w_in, w_in, wa, gw, gb, rc, rs1, rs2)


def _gla_decays(q, k, v, la, forward, G):
    C = GLA_CHUNK
    R = G * C
    r = lax.broadcasted_iota(jnp.int32, (R, R), 0)
    c = lax.broadcasted_iota(jnp.int32, (R, R), 1)
    same = (r >> GLA_CHUNK_SHIFT) == (c >> GLA_CHUNK_SHIFT)
    tri = (c <= r) if forward else (c >= r)
    t_mat = jnp.where(same, jnp.where(tri, 1.0, 0.0), 0.0).astype(BF16)
    hi = la.astype(BF16)
    lo = (la - hi.astype(F32)).astype(BF16)
    b = _dot(t_mat, hi) + _dot(t_mat, lo)
    edge = C - 1 if forward else 0
    tot = jnp.concatenate([jnp.broadcast_to(b[g * C + edge:g * C + edge + 1], (C, GLA_KEY_WIDTH))
                           for g in range(G)], axis=0)
    order = list(range(G)) if forward else list(range(G - 1, -1, -1))
    return dict(q_dec=q * jnp.exp(b), k_inv=(k * jnp.exp(-b)).astype(BF16), k_end=k * jnp.exp(tot - b),
                tot=tot, vb=v.astype(BF16), order=order, forward=forward, G=G)


def _gla_scores(prep):
    C, H = GLA_CHUNK, GLA_HEADS
    lane_k = lax.broadcasted_iota(jnp.int32, (C, GLA_KEY_WIDTH), 1)
    qd_heads, scores = {}, {}
    for g in prep["order"]:
        rows = slice(g * C, (g + 1) * C)
        qd = prep["q_dec"][rows]
        qd_heads[g] = jnp.concatenate([jnp.where((lane_k >> GLA_DK_SHIFT) == h, qd, 0.0) for h in range(H)],
                                      axis=0).astype(BF16)
        scores[g] = _dot_nt(qd_heads[g], prep["k_inv"][rows])
    return qd_heads, scores


def _gla_chunk_updates(prep):
    C, H, G = GLA_CHUNK, GLA_HEADS, prep["G"]
    k_end, tot, vb = prep["k_end"], prep["tot"], prep["vb"]
    kv, dec_t = {}, {}
    lane = lax.broadcasted_iota(jnp.int32, (GLA_KEY_WIDTH, 2 * C), 1)
    zeros = jnp.zeros((C, GLA_DV), BF16)
    for p in range(G // 2):
        pair = slice(2 * p * C, (2 * p + 2) * C)
        ke_t = k_end[pair].T.astype(BF16)
        tot_t = tot[pair].T
        swapped = pltpu.roll(tot_t, C, 1)
        for half in range(2):
            g = 2 * p + half
            rows = slice(g * C, (g + 1) * C)
            own = (lane < C) if half == 0 else (lane >= C)
            dec_t[g] = jnp.exp(jnp.where(own, tot_t, swapped))
            parts = []
            for h in range(H):
                v_h = vb[rows, h * GLA_DV:(h + 1) * GLA_DV]
                v_pad = jnp.concatenate([v_h, zeros] if half == 0 else [zeros, v_h], axis=0)
                parts.append(_dot(ke_t[h * C:(h + 1) * C], v_pad))
            kv[g] = jnp.concatenate(parts, axis=0)
    return kv, dec_t


def _gla_states(prep, kv, dec_t, s_ref):
    st = s_ref[...]
    states = {}
    for g in prep["order"]:
        states[g] = st.astype(BF16)
        st = st * dec_t[g] + kv[g]
    s_ref[...] = st
    return states


def _gla_outputs(prep, qd_heads, scores, inter, o_ref):
    C, H = GLA_CHUNK, GLA_HEADS
    row_q = lax.broadcasted_iota(jnp.int32, (H * C, C), 0) & (C - 1)
    col_k = lax.broadcasted_iota(jnp.int32, (H * C, C), 1)
    a_mask = (col_k <= row_q) if prep["forward"] else (col_k >= row_q)
    for g in prep["order"]:
        rows = slice(g * C, (g + 1) * C)
        a = jnp.where(a_mask, scores[g], 0.0).astype(BF16)
        vv = prep["vb"][rows]
        o_ref[rows, :] = jnp.concatenate(
            [_dot(a[h * C:(h + 1) * C], vv[:, h * GLA_DV:(h + 1) * GLA_DV]) + inter[g][h * C:(h + 1) * C]
             for h in range(H)], axis=1).astype(o_ref.dtype)


def _gla_kernel(qf_ref, kf_ref, vf_ref, laf_ref, qb_ref, kb_ref, vb_ref, lab_ref,
                of_ref, ob_ref, sf_ref, sb_ref, *, G):
    @pl.when(pl.program_id(1) == 0)
    def _():
        sf_ref[...] = jnp.zeros_like(sf_ref)
        sb_ref[...] = jnp.zeros_like(sb_ref)

    dirs = [(_gla_decays(qf_ref[...], kf_ref[...], vf_ref[...], laf_ref[...], True, G), sf_ref, of_ref),
            (_gla_decays(qb_ref[...], kb_ref[...], vb_ref[...], lab_ref[...], False, G), sb_ref, ob_ref)]
    scored = [_gla_scores(prep) for prep, _, _ in dirs]
    updates = [_gla_chunk_updates(prep) for prep, _, _ in dirs]
    states = [_gla_states(prep, kv, dec_t, s_ref) for (prep, s_ref, _), (kv, dec_t) in zip(dirs, updates)]
    inters = [{g: _dot(qd_heads[g], st[g]) for g in prep["order"]}
              for (prep, _, _), (qd_heads, _), st in zip(dirs, scored, states)]
    for (prep, _, o_ref), (qd_heads, scores), inter in zip(dirs, scored, inters):
        _gla_outputs(prep, qd_heads, scores, inter, o_ref)


def _gla(gla_slab, loga, B, S, G=8):
    T = B * S
    R = G * GLA_CHUNK
    ns = S // R
    fwd = lambda col: (lambda b, i: (b * ns + i, col))
    bwd = lambda col: (lambda b, i: (b * ns + ns - 1 - i, col))
    kw, vw = GLA_KEY_WIDTH, GLA_VAL_WIDTH
    return pl.pallas_call(
        functools.partial(_gla_kernel, G=G),
        grid=(B, ns),
        in_specs=[
            pl.BlockSpec((R, kw), fwd(0)), pl.BlockSpec((R, kw), fwd(1)),
            pl.BlockSpec((R, vw), fwd(1)), pl.BlockSpec((R, kw), fwd(0)),
            pl.BlockSpec((R, kw), bwd(0)), pl.BlockSpec((R, kw), bwd(1)),
            pl.BlockSpec((R, vw), bwd(1)), pl.BlockSpec((R, kw), bwd(1)),
        ],
        out_specs=[pl.BlockSpec((R, vw), fwd(0)), pl.BlockSpec((R, vw), bwd(0))],
        out_shape=[jax.ShapeDtypeStruct((T, vw), BF16), jax.ShapeDtypeStruct((T, vw), BF16)],
        scratch_shapes=[pltpu.VMEM((kw, GLA_DV), F32), pltpu.VMEM((kw, GLA_DV), F32)],
        compiler_params=_cparams(("arbitrary", "arbitrary")),
        name="gla",
    )(gla_slab, gla_slab, gla_slab, loga, gla_slab, gla_slab, gla_slab, loga)


ATT_CLASSES = 4
ATT_QB = 128
ATT_KB = ATT_QB + 2 * ATT_RADIUS


ATT_UNROLL = (32, 16, 16)


def _att_kernel(q_ref, k_ref, v_ref, o_ref, m_ref, l_ref, bias_ref, *, S):
    QB, KB, NC = ATT_QB, ATT_KB, ATT_CLASSES
    L4 = S // NC
    lane = lax.broadcasted_iota(jnp.int32, (QB, LANES), 1)
    head0 = lane < ATT_HEAD_DIM

    @pl.when((pl.program_id(0) == 0) & (pl.program_id(1) == 0))
    def _():
        rowi = lax.broadcasted_iota(jnp.int32, (2 * QB, KB), 0) & (QB - 1)
        coli = lax.broadcasted_iota(jnp.int32, (2 * QB, KB), 1)
        qpos = (rowi & (QB // NC - 1)) * NC + (rowi >> _log2(QB // NC))
        kpos = (coli & (KB // NC - 1)) * NC + (coli >> _log2(KB // NC))
        for case in range(3):
            bias_ref[0, case] = jnp.where(jnp.abs(rowi - coli + case * ATT_RADIUS) <= ATT_RADIUS, 0.0, NEG_INF)
            bias_ref[1, case] = jnp.where(jnp.abs(qpos - kpos + case * ATT_RADIUS) <= ATT_RADIUS, 0.0, NEG_INF)

    for pi, (_, d) in enumerate(DILATED_PATTERNS):
        L = S // d
        nb = L // QB
        shift = nb.bit_length() - 1
        first = pi == 0
        last = pi == len(DILATED_PATTERNS) - 1

        def scores(n, d=d, L=L, nb=nb, shift=shift):
            cls = n >> shift
            q0 = (n & (nb - 1)) * QB
            ws = jnp.clip(q0 - ATT_RADIUS, 0, L - KB)
            if d == 1:
                qsls = [pl.ds(pl.multiple_of(c * L4 + q0 // NC, QB // NC), QB // NC) for c in range(NC)]
                ksls = [pl.ds(pl.multiple_of(c * L4 + ws // NC, ATT_RADIUS // NC), KB // NC) for c in range(NC)]
            elif d == NC:
                qsls = [pl.ds(pl.multiple_of(cls * L4 + q0, QB), QB)]
                ksls = [pl.ds(pl.multiple_of(cls * L4 + ws, ATT_RADIUS), KB)]
            else:
                base = (cls & (NC - 1)) * L4 + (cls >> _log2(NC))
                qsls = [pl.ds(base + NC * q0, QB, stride=NC)]
                ksls = [pl.ds(base + NC * ws, KB, stride=NC)]
            q = jnp.concatenate([q_ref[sl, :] for sl in qsls], axis=0)
            kw = jnp.concatenate([k_ref[sl, :] for sl in ksls], axis=0)
            kb = kw.astype(BF16)
            bias = bias_ref[1 if d == 1 else 0, (q0 - ws) >> _log2(ATT_RADIUS), :QB]
            q_heads = (jnp.where(head0, q, 0.0), jnp.where(head0, 0.0, q))
            s = [_dot_nt(qh.astype(BF16), kb) + bias for qh in q_heads]
            return qsls, ksls, s

        def softmax_pv(qsls, ksls, s):
            vw = jnp.concatenate([v_ref[sl, :] for sl in ksls], axis=0)
            v_ones = jnp.concatenate([vw.astype(BF16), jnp.ones((KB, LANES), BF16)], axis=1)
            m_h = [jnp.max(t, axis=-1, keepdims=True) for t in s]
            pv = [_dot(jnp.exp2(t - m).astype(BF16), v_ones) for t, m in zip(s, m_h)]
            acc_b = jnp.where(head0, pv[0][:, :LANES], pv[1][:, :LANES])
            m_b = jnp.where(head0, m_h[0], m_h[1])
            l_b = jnp.where(head0, pv[0][:, LANES:], pv[1][:, LANES:])
            return qsls, acc_b, m_b, l_b

        def load(ref, sls):
            return jnp.concatenate([ref[sl, :] for sl in sls], axis=0)

        def store(ref, sls, val):
            n = val.shape[0] // len(sls)
            for i, sl in enumerate(sls):
                ref[sl, :] = val[i * n:(i + 1) * n]

        unroll = ATT_UNROLL[pi]

        def body(n, carry, first=first, last=last, unroll=unroll):
            staged = [scores(n * unroll + u) for u in range(unroll)]
            blocks = [softmax_pv(*st) for st in staged]
            for qsls, acc_b, m_b, l_b in blocks:
                if first:
                    acc, m_new, l_new = acc_b, m_b, l_b
                else:
                    m_old = load(m_ref, qsls)
                    m_new = jnp.maximum(m_old, m_b)
                    w_old = jnp.exp2(m_old - m_new)
                    w_blk = jnp.exp2(m_b - m_new)
                    acc = load(o_ref, qsls) * w_old + acc_b * w_blk
                    l_new = load(l_ref, qsls) * w_old + l_b * w_blk
                if last:
                    store(o_ref, qsls, acc / l_new)
                else:
                    store(o_ref, qsls, acc)
                    store(m_ref, qsls, m_new)
                    store(l_ref, qsls, l_new)
            return carry

        lax.fori_loop(0, S // (QB * unroll), body, 0)


def _attention(att_slab, B, S):
    T = B * S
    ncol = ATT_WIDTH // LANES
    return pl.pallas_call(
        functools.partial(_att_kernel, S=S),
        grid=(B, ncol),
        in_specs=[
            pl.BlockSpec((S, LANES), lambda b, h: (b, h)),
            pl.BlockSpec((S, LANES), lambda b, h: (b, ncol + h)),
            pl.BlockSpec((S, LANES), lambda b, h: (b, 2 * ncol + h)),
        ],
        out_specs=pl.BlockSpec((S, LANES), lambda b, h: (b, h)),
        out_shape=jax.ShapeDtypeStruct((T, ATT_WIDTH), F32),
        scratch_shapes=[pltpu.VMEM((S, LANES), F32), pltpu.VMEM((S, LANES), F32),
                        pltpu.VMEM((2, 3, 2 * ATT_QB, ATT_KB), F32)],
        compiler_params=_cparams(("arbitrary", "arbitrary")),
        name="dilated_attention",
    )(att_slab, att_slab, att_slab)


PACK_WORDS = D_MODEL // 2
ROW_TILE = PACK_WORDS // LANES
HIGH_HALF = -65536


def _pack_rows(x):
    bits = lambda v: lax.bitcast_convert_type(v.astype(BF16).astype(F32), jnp.int32)
    low = (bits(x[:, :PACK_WORDS]) >> 16) & 0xFFFF
    return (bits(x[:, PACK_WORDS:]) & HIGH_HALF) | low


def _unpack_rows(w):
    low = lax.bitcast_convert_type(w << 16, F32)
    high = lax.bitcast_convert_type(w & HIGH_HALF, F32)
    return jnp.concatenate([low, high], axis=1).astype(BF16)


def _to_row_tiles(ref, w):
    n = w.shape[0]
    for j in range(ROW_TILE):
        ref[pl.ds(j, n, stride=ROW_TILE), :] = w[:, j * LANES:(j + 1) * LANES]


def _from_row_tiles(ref, n):
    return jnp.concatenate([ref[pl.ds(j, n, stride=ROW_TILE), :] for j in range(ROW_TILE)], axis=1)


def _tile_copy(src_ref, src_row, dst_ref, dst_row, sem):
    src = pl.ds(pl.multiple_of(src_row * ROW_TILE, ROW_TILE), ROW_TILE)
    dst = pl.ds(pl.multiple_of(dst_row * ROW_TILE, ROW_TILE), ROW_TILE)
    return pltpu.make_async_copy(src_ref.at[src], dst_ref.at[dst], sem)


def _outproj_kernel(of_ref, ob_ref, gg_ref, att_ref, x_ref, gnw_ref, wo1_ref, wo2_ref,
                    n2_ref, wr_ref, br_ref, h_ref, u_ref, lg_ref, stage_ref):
    rows = stage_ref.shape[1] // ATT_CLASSES
    for j in range(ATT_WIDTH // LANES):
        for c in range(ATT_CLASSES):
            stage_ref[j, pl.ds(c, rows, stride=ATT_CLASSES), :] = att_ref[c, :, j * LANES:(j + 1) * LANES]
    att = jnp.concatenate([stage_ref[j] for j in range(ATT_WIDTH // LANES)], axis=1)
    o = of_ref[...].astype(F32) + ob_ref[...].astype(F32)
    gate = gg_ref[...].astype(F32)
    gnw = gnw_ref[...]
    parts = []
    for h in range(GLA_HEADS):
        sl = slice(h * GLA_DV, (h + 1) * GLA_DV)
        parts.append(_rms(o[:, sl], gnw))
    y = jnp.concatenate(parts, axis=1) * (gate / (1.0 + jnp.exp(-gate)))
    mix = _dot(y.astype(BF16), wo1_ref[...]) + _dot(att.astype(BF16), wo2_ref[...])
    h = x_ref[...] + mix
    h_ref[...] = h
    u = _rms(h, n2_ref[...])
    _to_row_tiles(u_ref, _pack_rows(u))
    u_hi = u.astype(BF16)
    u_lo = (u - u_hi.astype(F32)).astype(BF16)
    hi_both = _dot_nt(wr_ref[...], u_hi)
    lg_ref[...] = (hi_both[:LANES] + hi_both[LANES:] + _dot_nt(wr_ref[:LANES], u_lo)) + br_ref[...]


def _outproj(o_f, o_b, gate, att_out, x2, gla_norm_w, w_out, norm2_w, wr, br, tm=512):
    T = x2.shape[0]
    nS = att_out.shape[2] * ATT_CLASSES // tm
    row = lambda i: (i, 0)
    const = lambda i: (0, 0)
    wo = w_out.astype(BF16)
    wr_hi = wr.astype(BF16)
    wr_lo = (wr - wr_hi.astype(F32)).astype(BF16)
    wr = jnp.concatenate([wr_hi, wr_lo], axis=0)
    return pl.pallas_call(
        _outproj_kernel,
        grid=(T // tm,),
        in_specs=[
            pl.BlockSpec((tm, GLA_VAL_WIDTH), row),
            pl.BlockSpec((tm, GLA_VAL_WIDTH), row),
            pl.BlockSpec((tm, GLA_VAL_WIDTH), row),
            pl.BlockSpec((None, ATT_CLASSES, tm // ATT_CLASSES, ATT_WIDTH), lambda i: (i // nS, 0, i % nS, 0)),
            pl.BlockSpec((tm, D_MODEL), row),
            pl.BlockSpec((1, GLA_DV), const),
            pl.BlockSpec((GLA_VAL_WIDTH, D_MODEL), const),
            pl.BlockSpec((ATT_WIDTH, D_MODEL), lambda i: (GLA_VAL_WIDTH // ATT_WIDTH, 0)),
            pl.BlockSpec((1, D_MODEL), const),
            pl.BlockSpec((2 * LANES, D_MODEL), const),
            pl.BlockSpec((LANES, 1), const),
        ],
        out_specs=[
            pl.BlockSpec((tm, D_MODEL), row),
            pl.BlockSpec((tm * ROW_TILE, LANES), row),
            pl.BlockSpec((LANES, tm), lambda i: (0, i)),
        ],
        out_shape=[
            jax.ShapeDtypeStruct((T, D_MODEL), F32),
            jax.ShapeDtypeStruct((T * ROW_TILE, LANES), jnp.int32),
            jax.ShapeDtypeStruct((LANES, T), F32),
        ],
        scratch_shapes=[pltpu.VMEM((ATT_WIDTH // LANES, tm, LANES), F32)],
        compiler_params=_cparams(("arbitrary",)),
        name="outproj",
    )(o_f, o_b, gate, att_out, x2, gla_norm_w[None, :], wo, wo,
      norm2_w[None, :], wr, br)


INFO_E1, INFO_E2, INFO_R1, INFO_R2, INFO_W1, INFO_W2 = range(6)
ROUTE_ROWS = 40


def _route_kernel(lg_ref, info_ref, cnt_ref, carry_ref):
    @pl.when(pl.program_id(0) == 0)
    def _():
        carry_ref[...] = jnp.zeros_like(carry_ref)

    lg = lg_ref[:ROUTE_ROWS, :]
    tr = lg.shape[1]
    row = lax.broadcasted_iota(jnp.int32, (ROUTE_ROWS, tr), 0)
    big = jnp.int32(1 << 20)
    is_g = (row >= MOE_N_EXPERTS) & (row < MOE_N_EXPERTS + MOE_GROUPS)
    gl = jnp.where(is_g, lg, -jnp.inf)
    gmax = jnp.max(gl, axis=0, keepdims=True)
    gsel = jnp.min(jnp.where(gl == gmax, row - MOE_N_EXPERTS, big), axis=0, keepdims=True)
    g_w = 1.0 / jnp.sum(jnp.where(is_g, jnp.exp(lg - gmax), 0.0), axis=0, keepdims=True)
    in_grp = (row < MOE_N_EXPERTS) & ((row >> MOE_GROUP_SHIFT) == gsel)
    el = jnp.where(in_grp, lg, -jnp.inf)
    v1 = jnp.max(el, axis=0, keepdims=True)
    i1 = jnp.min(jnp.where(el == v1, row, big), axis=0, keepdims=True)
    el2 = jnp.where(row == i1, -jnp.inf, el)
    v2 = jnp.max(el2, axis=0, keepdims=True)
    i2 = jnp.min(jnp.where(el2 == v2, row, big), axis=0, keepdims=True)
    t = jnp.exp(v2 - v1)
    w1 = g_w * (1.0 / (1.0 + t))
    w2 = g_w * (t / (1.0 + t))

    erow = lax.broadcasted_iota(jnp.int32, (MOE_N_EXPERTS, tr), 0)
    hit1 = erow == i1
    hit2 = erow == i2
    member = jnp.where(hit1 | hit2, 1.0, 0.0)
    r = lax.broadcasted_iota(jnp.int32, (tr, tr), 0)
    c = lax.broadcasted_iota(jnp.int32, (tr, tr), 1)
    earlier = jnp.where(r < c, 1.0, 0.0).astype(BF16)
    carry = carry_ref[...]
    prefix = _dot(member.astype(BF16), earlier) + carry[:, 0:1]
    rank1 = jnp.sum(jnp.where(hit1, prefix, 0.0), axis=0, keepdims=True)
    rank2 = jnp.sum(jnp.where(hit2, prefix, 0.0), axis=0, keepdims=True)
    carry = carry + jnp.sum(member, axis=1, keepdims=True)
    carry_ref[...] = carry
    cnt_ref[...] = carry

    zero = jnp.zeros_like(w1)
    info_ref[...] = jnp.concatenate([i1.astype(F32), i2.astype(F32), rank1, rank2, w1, w2, zero, zero], axis=0)


def _route(logits_t, tr=1024):
    T = logits_t.shape[1]
    return pl.pallas_call(
        _route_kernel,
        grid=(T // tr,),
        in_specs=[pl.BlockSpec((LANES, tr), lambda i: (0, i))],
        out_specs=[pl.BlockSpec((8, tr), lambda i: (0, i)),
                   pl.BlockSpec((MOE_N_EXPERTS, LANES), lambda i: (0, 0))],
        out_shape=[jax.ShapeDtypeStruct((8, T), F32), jax.ShapeDtypeStruct((MOE_N_EXPERTS, LANES), F32)],
        scratch_shapes=[pltpu.VMEM((MOE_N_EXPERTS, LANES), F32)],
        compiler_params=_cparams(("arbitrary",)),
        name="route",
    )(logits_t)


ROW_UNROLL = 16


def _dispatch_kernel(dest_ref, pend_ref, u_ref, xs_ref, zbuf, sem, zsem, *, td, T, nblk):
    @pl.when(pl.program_id(0) == 0)
    def _():
        zbuf[...] = jnp.zeros_like(zbuf)
        n_used = pend_ref[MOE_N_EXPERTS - 1] >> MOE_ROWS_SHIFT

        def zero_copy(blk):
            start = pl.multiple_of(blk * (MOE_ROWS * ROW_TILE), MOE_ROWS * ROW_TILE)
            return pltpu.make_async_copy(zbuf, xs_ref.at[pl.ds(start, MOE_ROWS * ROW_TILE)], zsem)

        def each_pad_block(fn):
            def per_expert(e, carry):
                prev = jnp.where(e > 0, pend_ref[jnp.maximum(e - 1, 0)], 0)

                @pl.when(pend_ref[e] > prev)
                def _():
                    fn((pend_ref[e] >> MOE_ROWS_SHIFT) - 1)
                return carry

            def per_tail(j, carry):
                @pl.when(n_used + j < nblk)
                def _():
                    fn(n_used + j)
                return carry

            lax.fori_loop(0, MOE_N_EXPERTS, per_expert, 0)
            lax.fori_loop(0, MOE_N_EXPERTS, per_tail, 0)

        each_pad_block(lambda blk: zero_copy(blk).start())
        each_pad_block(lambda blk: zero_copy(blk).wait())

    base = pl.program_id(0) * td

    def issue(g, carry):
        for j in range(ROW_UNROLL):
            r = g * ROW_UNROLL + j
            for k in range(MOE_TOP_K):
                _tile_copy(u_ref, r, xs_ref, dest_ref[k * T + base + r], sem).start(priority=k)
        return carry

    lax.fori_loop(0, td // ROW_UNROLL, issue, 0)
    for k in range(MOE_TOP_K):
        pltpu.make_async_copy(u_ref, xs_ref.at[pl.ds(0, td * ROW_TILE)], sem).wait()


def _dispatch(dest, pend, u2, cap, td=2048):
    T = u2.shape[0] // ROW_TILE
    return pl.pallas_call(
        functools.partial(_dispatch_kernel, td=td, T=T, nblk=cap // MOE_ROWS),
        grid_spec=pltpu.PrefetchScalarGridSpec(
            num_scalar_prefetch=2,
            grid=(T // td,),
            in_specs=[pl.BlockSpec((td * ROW_TILE, LANES), lambda i, d, z: (i, 0))],
            out_specs=pl.BlockSpec(memory_space=pl.ANY),
            scratch_shapes=[pltpu.VMEM((MOE_ROWS * ROW_TILE, LANES), jnp.int32),
                            pltpu.SemaphoreType.DMA(()), pltpu.SemaphoreType.DMA(())],
        ),
        out_shape=jax.ShapeDtypeStruct((cap * ROW_TILE, LANES), jnp.int32),
        compiler_params=_cparams(("arbitrary",)),
        name="dispatch",
    )(dest, pend, u2)


EXPERT_GROUP = 8


def _expert_kernel(pend_ref, xs_hbm, wg_hbm, wu_hbm, wd_hbm, ys_hbm,
                   xbuf, ybuf, zbuf, stage_g, stage_u, stage_d, wgb, wub, wdb, xsem, ysem, wsem, zsem, *, nblk):
    last = MOE_N_EXPERTS - 1
    n_used = pend_ref[last] >> MOE_ROWS_SHIFT
    n_pairs = (n_used + EXPERT_GROUP - 1) >> _log2(EXPERT_GROUP)
    block_rows = MOE_ROWS * ROW_TILE

    def rows_of(b):
        return pl.ds(pl.multiple_of(b * block_rows, block_rows), block_rows)

    def x_copy(b, slot):
        return pltpu.make_async_copy(xs_hbm.at[rows_of(b)], xbuf.at[slot], xsem.at[slot])

    def y_copy(b, slot):
        return pltpu.make_async_copy(ybuf.at[slot], ys_hbm.at[rows_of(b)], ysem.at[slot])

    def zero_copy(b):
        return pltpu.make_async_copy(zbuf, ys_hbm.at[rows_of(b)], zsem)

    def weight_copies(e):
        return (pltpu.make_async_copy(wg_hbm.at[e], stage_g, wsem.at[0]),
                pltpu.make_async_copy(wu_hbm.at[e], stage_u, wsem.at[1]),
                pltpu.make_async_copy(wd_hbm.at[e], stage_d, wsem.at[2]))

    def owner(start, row):
        return lax.while_loop(lambda e: (e < last) & (pend_ref[e] <= row), lambda e: e + 1, start)

    for c in weight_copies(owner(0, 0)):
        c.start()
    for i in range(EXPERT_GROUP):
        x_copy(i, i).start()

    zbuf[...] = jnp.zeros_like(zbuf)

    def tail(fn):
        def step(b, carry):
            fn(b)
            return carry
        lax.fori_loop(n_pairs * EXPERT_GROUP, nblk, step, 0)

    tail(lambda b: zero_copy(b).start())

    def body(p, carry):
        cur, run = carry
        half = (p & 1) * EXPERT_GROUP
        for i in range(EXPERT_GROUP):
            x_copy(p * EXPERT_GROUP + i, half + i).wait()

        @pl.when(p + 1 < n_pairs)
        def _():
            for i in range(EXPERT_GROUP):
                x_copy((p + 1) * EXPERT_GROUP + i, EXPERT_GROUP - half + i).start()

        slots = []
        for i in range(EXPERT_GROUP):
            b = p * EXPERT_GROUP + i
            e = jnp.where(b < n_used, owner(jnp.maximum(cur, 0), b * MOE_ROWS), cur)
            fresh = e != cur
            run = run + fresh.astype(jnp.int32)
            slot = run & (EXPERT_GROUP - 1)

            @pl.when(fresh)
            def _(e=e, slot=slot):
                for c in weight_copies(e):
                    c.wait()
                wgb[slot] = stage_g[...].astype(BF16)
                wub[slot] = stage_u[...].astype(BF16)
                wdb[slot] = stage_d[...].astype(BF16)

                @pl.when(pend_ref[e] < pend_ref[last])
                def _():
                    for c in weight_copies(owner(e + 1, pend_ref[e])):
                        c.start(priority=1)

            cur = e
            slots.append(slot)

        @pl.when(p >= 2)
        def _():
            for i in range(EXPERT_GROUP):
                y_copy((p - 2) * EXPERT_GROUP + i, half + i).wait()

        for i in range(EXPERT_GROUP):
            xb = _unpack_rows(_from_row_tiles(xbuf.at[half + i], MOE_ROWS))
            g = _dot(xb, wgb[slots[i]])
            u = _dot(xb, wub[slots[i]])
            hid = (g / (1.0 + jnp.exp(-g))) * u
            _to_row_tiles(ybuf.at[half + i], _pack_rows(_dot(hid.astype(BF16), wdb[slots[i]])))
        for i in range(EXPERT_GROUP):
            y_copy(p * EXPERT_GROUP + i, half + i).start()
        return cur, run

    lax.fori_loop(0, n_pairs, body, (jnp.int32(-1), jnp.int32(-1)))

    def drain(p):
        for i in range(EXPERT_GROUP):
            y_copy(p * EXPERT_GROUP + i, (p & 1) * EXPERT_GROUP + i).wait()

    @pl.when(n_pairs >= 2)
    def _():
        drain(n_pairs - 2)
    drain(n_pairs - 1)
    tail(lambda b: zero_copy(b).wait())


def _experts(pend, xs, w_gate, w_up, w_down):
    cap = xs.shape[0] // ROW_TILE
    nblk = cap // MOE_ROWS
    assert nblk % EXPERT_GROUP == 0
    block = (MOE_ROWS * ROW_TILE, LANES)
    anywhere = pl.BlockSpec(memory_space=pl.ANY)
    return pl.pallas_call(
        functools.partial(_expert_kernel, nblk=nblk),
        grid_spec=pltpu.PrefetchScalarGridSpec(
            num_scalar_prefetch=1,
            grid=(1,),
            in_specs=[anywhere, anywhere, anywhere, anywhere],
            out_specs=anywhere,
            scratch_shapes=[pltpu.VMEM((2 * EXPERT_GROUP,) + block, jnp.int32),
                            pltpu.VMEM((2 * EXPERT_GROUP,) + block, jnp.int32),
                            pltpu.VMEM(block, jnp.int32),
                            pltpu.VMEM((D_MODEL, MOE_D_FF), F32),
                            pltpu.VMEM((D_MODEL, MOE_D_FF), F32),
                            pltpu.VMEM((MOE_D_FF, D_MODEL), F32),
                            pltpu.VMEM((EXPERT_GROUP, D_MODEL, MOE_D_FF), BF16),
                            pltpu.VMEM((EXPERT_GROUP, D_MODEL, MOE_D_FF), BF16),
                            pltpu.VMEM((EXPERT_GROUP, MOE_D_FF, D_MODEL), BF16),
                            pltpu.SemaphoreType.DMA((2 * EXPERT_GROUP,)),
                            pltpu.SemaphoreType.DMA((2 * EXPERT_GROUP,)),
                            pltpu.SemaphoreType.DMA((3,)),
                            pltpu.SemaphoreType.DMA(())],
        ),
        out_shape=jax.ShapeDtypeStruct((cap * ROW_TILE, LANES), jnp.int32),
        compiler_params=_cparams(("arbitrary",)),
        name="experts",
    )(pend, xs, w_gate, w_up, w_down)


def _combine_kernel(dest_ref, ys_ref, info_ref, h_ref, fw_ref, o_ref, buf, sem, *, tc, T):
    i = pl.program_id(0)
    n = pl.num_programs(0)

    def issue(step, slot):
        base = step * tc

        def body(g, carry):
            for j in range(ROW_UNROLL):
                r = g * ROW_UNROLL + j
                for k in range(MOE_TOP_K):
                    _tile_copy(ys_ref, dest_ref[k * T + base + r], buf.at[slot, k], r,
                               sem.at[slot]).start(priority=k)
            return carry

        lax.fori_loop(0, tc // ROW_UNROLL, body, 0)

    @pl.when(i == 0)
    def _():
        issue(0, 0)

    slot = i % 2

    @pl.when(i + 1 < n)
    def _():
        issue(i + 1, 1 - slot)

    for k in range(MOE_TOP_K):
        pltpu.make_async_copy(ys_ref.at[pl.ds(0, tc * ROW_TILE)], buf.at[slot, k], sem.at[slot]).wait()

    info_t = jnp.concatenate([info_ref[...]] * (LANES // 8), axis=0).T
    w1 = info_t[:, INFO_W1:INFO_W1 + 1]
    w2 = info_t[:, INFO_W2:INFO_W2 + 1]
    y1 = _unpack_rows(_from_row_tiles(buf.at[slot, 0], tc)).astype(F32)
    y2 = _unpack_rows(_from_row_tiles(buf.at[slot, 1], tc)).astype(F32)
    h = h_ref[...] + (y1 * w1 + y2 * w2)
    o_ref[...] = _rms(h, fw_ref[...])


def _combine(dest, ys, info, h, final_w, tc=512):
    T = h.shape[0]
    return pl.pallas_call(
        functools.partial(_combine_kernel, tc=tc, T=T),
        grid_spec=pltpu.PrefetchScalarGridSpec(
            num_scalar_prefetch=1,
            grid=(T // tc,),
            in_specs=[pl.BlockSpec(memory_space=pl.ANY),
                      pl.BlockSpec((8, tc), lambda i, d: (0, i)),
                      pl.BlockSpec((tc, D_MODEL), lambda i, d: (i, 0)),
                      pl.BlockSpec((1, D_MODEL), lambda i, d: (0, 0))],
            out_specs=pl.BlockSpec((tc, D_MODEL), lambda i, d: (i, 0)),
            scratch_shapes=[pltpu.VMEM((2, MOE_TOP_K, tc * ROW_TILE, LANES), jnp.int32),
                            pltpu.SemaphoreType.DMA((2,))],
        ),
        out_shape=jax.ShapeDtypeStruct((T, D_MODEL), F32),
        compiler_params=_cparams(("arbitrary",)),
        name="combine",
    )(dest, ys, info, h, final_w[None, :])


def _plan_kernel(info_ref, cnt_ref, dest_ref, pend_ref):
    cnt = cnt_ref[...].astype(jnp.int32)
    nblk_e = ((cnt + (MOE_ROWS - 1)) >> MOE_ROWS_SHIFT).astype(F32)
    r = lax.broadcasted_iota(jnp.int32, (MOE_N_EXPERTS, MOE_N_EXPERTS), 0)
    c = lax.broadcasted_iota(jnp.int32, (MOE_N_EXPERTS, MOE_N_EXPERTS), 1)
    before = jnp.where(c < r, 1.0, 0.0).astype(BF16)
    first_blk = _dot(before, nblk_e.astype(BF16))
    pstart = first_blk[:, 0:1] * float(MOE_ROWS)
    pend_ref[...] = ((first_blk + nblk_e) * float(MOE_ROWS)).astype(jnp.int32)

    info = info_ref[...]
    erow = lax.broadcasted_iota(jnp.int32, (MOE_N_EXPERTS, info.shape[1]), 0)
    start_of = lambda e: jnp.sum(jnp.where(erow == e.astype(jnp.int32), pstart, 0.0), axis=0, keepdims=True)
    d1 = info[INFO_R1:INFO_R1 + 1] + start_of(info[INFO_E1:INFO_E1 + 1])
    d2 = info[INFO_R2:INFO_R2 + 1] + start_of(info[INFO_E2:INFO_E2 + 1])
    zero = jnp.zeros_like(d1)
    dest_ref[...] = jnp.concatenate([d1, d2] + [zero] * 6, axis=0).astype(jnp.int32)


def _plan(info, counts, tr=2048):
    T = info.shape[1]
    dest8, pend = pl.pallas_call(
        _plan_kernel,
        grid=(T // tr,),
        in_specs=[pl.BlockSpec((8, tr), lambda i: (0, i)),
                  pl.BlockSpec((MOE_N_EXPERTS, LANES), lambda i: (0, 0))],
        out_specs=[pl.BlockSpec((8, tr), lambda i: (0, i)),
                   pl.BlockSpec((MOE_N_EXPERTS, LANES), lambda i: (0, 0))],
        out_shape=[jax.ShapeDtypeStruct((8, T), jnp.int32),
                   jax.ShapeDtypeStruct((MOE_N_EXPERTS, LANES), jnp.int32)],
        compiler_params=_cparams(("arbitrary",)),
        name="plan",
    )(info, counts)
    return dest8[:MOE_TOP_K].reshape(-1), pend[:, 0]


def _moe_capacity(T):
    return (-(-(T * MOE_TOP_K) // MOE_ROWS) + MOE_N_EXPERTS) * MOE_ROWS


def _router_weights(router_group_w, router_group_b, router_expert_w, router_expert_b):
    we = jnp.transpose(router_expert_w, (0, 2, 1)).reshape(MOE_N_EXPERTS, D_MODEL)
    pad = LANES - MOE_N_EXPERTS - MOE_GROUPS
    wr = jnp.concatenate([we, router_group_w.T, jnp.zeros((pad, D_MODEL), F32)], axis=0)
    br = jnp.concatenate([router_expert_b.reshape(-1), router_group_b, jnp.zeros((pad,), F32)])[:, None]
    return wr, br


def kernel(x, norm1_w, w_in, gla_fwd_gate_w, gla_fwd_gate_b, gla_bwd_gate_w, gla_bwd_gate_b,
           gla_norm_w, w_out, norm2_w, router_group_w, router_group_b, router_expert_w,
           router_expert_b, expert_w_gate, expert_w_up, expert_w_down, final_norm_w):
    B, S, D = x.shape
    T = B * S
    assert norm1_w.shape[0] == 1, "single-layer trunk: the final norm is fused into the combine step"
    h = x.reshape(T, D)
    gla_slab, gate, loga, att_slab = _inproj(h, S, norm1_w[0], w_in[0], gla_fwd_gate_w[0], gla_fwd_gate_b[0],
                                       gla_bwd_gate_w[0], gla_bwd_gate_b[0])
    o_f, o_b = _gla(gla_slab, loga, B, S)
    att_out = _attention(att_slab.reshape(T, 3 * ATT_WIDTH), B, S)
    att_out = att_out.reshape(B, ATT_CLASSES, S // ATT_CLASSES, ATT_WIDTH)
    wr, br = _router_weights(router_group_w[0], router_group_b[0], router_expert_w[0], router_expert_b[0])
    h, u2, logits = _outproj(o_f, o_b, gate, att_out, h, gla_norm_w[0], w_out[0], norm2_w[0], wr, br)
    info, counts = _route(logits)
    dest, pend = _plan(info, counts)
    xs = _dispatch(dest, pend, u2, _moe_capacity(T))
    ys = _experts(pend, xs, expert_w_gate[0], expert_w_up[0], expert_w_down[0])
    out = _combine(dest, ys, info, h, final_norm_w)
    return out.reshape(B, S, D)
```

```python
import functools

import jax
import jax.numpy as jnp
import numpy as np
from jax import lax
from jax.experimental import pallas as pl
from jax.experimental.pallas import tpu as pltpu

F32 = jnp.float32
BF16 = jnp.bfloat16

D_MODEL = 1024
GLA_HEADS = 4
GLA_DV = 128
GLA_DK = 64
GLA_KEY_WIDTH = GLA_HEADS * GLA_DK
GLA_VAL_WIDTH = GLA_HEADS * GLA_DV
GLA_GATE_RANK = 16
GLA_TAU = 16.0
GLA_CHUNK = 64
ATT_WIDTH = 512
ATT_HEAD_DIM = 64
ATT_HEADS = 8
ROT_DIM = 16
ROPE_THETA = 500000.0
DILATED_PATTERNS = ((128, 1), (512, 4), (2048, 16))
ATT_RADIUS = 64
MOE_GROUPS = 4
MOE_EXPERTS_PER_GROUP = 8
MOE_N_EXPERTS = 32
MOE_TOP_K = 2
MOE_D_FF = 512
EPS = 1e-6
NEG_INF = -1e30
LOG2E = 1.4426950408889634

LANES = 128
MOE_ROWS = 256


def _log2(n):
    assert n & (n - 1) == 0, n
    return n.bit_length() - 1


GLA_CHUNK_SHIFT = _log2(GLA_CHUNK)
GLA_DK_SHIFT = _log2(GLA_DK)
MOE_ROWS_SHIFT = _log2(MOE_ROWS)
MOE_GROUP_SHIFT = _log2(MOE_EXPERTS_PER_GROUP)
VMEM_LIMIT = 56 * 1024 * 1024


def _cparams(sem):
    return pltpu.CompilerParams(dimension_semantics=sem, vmem_limit_bytes=VMEM_LIMIT)


def _dot(a, b):
    return jnp.dot(a, b, preferred_element_type=F32)


def _dot_nt(a, b):
    return lax.dot_general(a, b, (((1,), (1,)), ((), ())), preferred_element_type=F32)


def _dot_tn(a, b):
    return lax.dot_general(a, b, (((0,), (0,)), ((), ())), preferred_element_type=F32)


def _rms(x, w):
    return x * lax.rsqrt(jnp.mean(x * x, axis=-1, keepdims=True) + EPS) * w


def _inproj_kernel(x_ref, n1_ref, wg_ref, wlr_ref, wa_ref, gw_ref, gb_ref,
                   rc_ref, rs1_ref, rs2_ref, gla_ref, gate_ref, loga_ref, att_ref, stage_ref, wgb, wlrb):
    @pl.when(pl.program_id(0) == 0)
    def _():
        wgb[...] = wg_ref[...].astype(BF16)
        wlrb[...] = wlr_ref[...].astype(BF16)

    x = x_ref[...]
    ub = _rms(x, n1_ref[...]).astype(BF16)
    g = _dot(ub, wgb[...])
    qkv = 2 * GLA_KEY_WIDTH + GLA_VAL_WIDTH
    gla_ref[:, :GLA_KEY_WIDTH] = g[:, :GLA_KEY_WIDTH] * (GLA_DK ** -0.5)
    gla_ref[:, GLA_KEY_WIDTH:] = g[:, GLA_KEY_WIDTH:qkv]
    gate_ref[...] = g[:, qkv:].astype(BF16)
    lr = _dot(ub, wlrb[...])
    gate = _dot(lr.astype(BF16), gw_ref[...]) + gb_ref[...]
    loga_ref[...] = (jnp.minimum(gate, 0.0) - jnp.log(1.0 + jnp.exp(-jnp.abs(gate)))) * (1.0 / GLA_TAU)
    a = _dot(ub, wa_ref[...])
    qk = a[:, :2 * ATT_WIDTH]
    reps = 2 * ATT_WIDTH // LANES
    c = jnp.concatenate([rc_ref[...]] * reps, axis=1)
    s1 = jnp.concatenate([rs1_ref[...]] * reps, axis=1)
    s2 = jnp.concatenate([rs2_ref[...]] * reps, axis=1)
    half = ROT_DIM // 2
    n = 2 * ATT_WIDTH
    roped = qk * c + pltpu.roll(qk, n - half, 1) * s1 + pltpu.roll(qk, half, 1) * s2
    qkv = jnp.concatenate([roped[:, :ATT_WIDTH] * (ATT_HEAD_DIM ** -0.5 * LOG2E), roped[:, ATT_WIDTH:],
                           a[:, 2 * ATT_WIDTH:]], axis=1)
    rows = x.shape[0] // ATT_CLASSES
    for j in range(3 * ATT_WIDTH // LANES):
        cols = slice(j * LANES, (j + 1) * LANES)
        stage_ref[j] = qkv[:, cols]
        for c in range(ATT_CLASSES):
            att_ref[c, :, cols] = stage_ref[j, pl.ds(c, rows, stride=ATT_CLASSES), :]


def _rope_lane_tables(S):
    half = ROT_DIM // 2
    inv = np.float32(ROPE_THETA) ** (-(np.arange(0, ROT_DIM, 2, dtype=np.float32) / np.float32(ROT_DIM)))
    ang = np.arange(S, dtype=np.float32)[:, None] * inv[None, :].astype(np.float32)
    cos, sin = np.cos(ang), np.sin(ang)
    ones = np.ones((S, ATT_HEAD_DIM - ROT_DIM), np.float32)
    zeros = np.zeros((S, ATT_HEAD_DIM - ROT_DIM), np.float32)
    zeros8 = np.zeros((S, half), np.float32)
    rep = LANES // ATT_HEAD_DIM
    c = np.tile(np.concatenate([cos, cos, ones], axis=1), (1, rep))
    s1 = np.tile(np.concatenate([-sin, zeros8, zeros], axis=1), (1, rep))
    s2 = np.tile(np.concatenate([zeros8, sin, zeros], axis=1), (1, rep))
    return jnp.asarray(c), jnp.asarray(s1), jnp.asarray(s2)


def _inproj(x2, S, norm1_w, w_in, wf, bfw, wb, bbw, tm=512):
    T = x2.shape[0]
    o_lr = 2 * GLA_KEY_WIDTH + 2 * GLA_VAL_WIDTH
    o_att = o_lr + 2 * GLA_GATE_RANK
    wa = w_in[:, o_att:].astype(BF16)
    zeros = jnp.zeros((GLA_GATE_RANK, GLA_KEY_WIDTH), F32)
    gw = jnp.concatenate([jnp.concatenate([wf, zeros], axis=1), jnp.concatenate([zeros, wb], axis=1),
                          jnp.zeros((LANES - 2 * GLA_GATE_RANK, 2 * GLA_KEY_WIDTH), F32)], axis=0).astype(BF16)
    gb = jnp.concatenate([bfw, bbw])[None, :]
    rc, rs1, rs2 = _rope_lane_tables(S)
    nS = S // tm
    row = lambda i: (i, 0)
    const = lambda i: (0, 0)
    pos = lambda i: (i % nS, 0)
    return pl.pallas_call(
        _inproj_kernel,
        grid=(T // tm,),
        in_specs=[
            pl.BlockSpec((tm, D_MODEL), row),
            pl.BlockSpec((1, D_MODEL), const),
            pl.BlockSpec((D_MODEL, o_lr), const),
            pl.BlockSpec((D_MODEL, LANES), lambda i: (0, o_lr // LANES)),
            pl.BlockSpec((D_MODEL, 3 * ATT_WIDTH), const),
            pl.BlockSpec((LANES, 2 * GLA_KEY_WIDTH), const),
            pl.BlockSpec((1, 2 * GLA_KEY_WIDTH), const),
            pl.BlockSpec((tm, LANES), pos),
            pl.BlockSpec((tm, LANES), pos),
            pl.BlockSpec((tm, LANES), pos),
        ],
        out_specs=[
            pl.BlockSpec((tm, o_lr - GLA_VAL_WIDTH), row),
            pl.BlockSpec((tm, GLA_VAL_WIDTH), row),
            pl.BlockSpec((tm, 2 * GLA_KEY_WIDTH), row),
            pl.BlockSpec((None, ATT_CLASSES, tm // ATT_CLASSES, 3 * ATT_WIDTH),
                         lambda i: (i // nS, 0, i % nS, 0)),
        ],
        out_shape=[
            jax.ShapeDtypeStruct((T, o_lr - GLA_VAL_WIDTH), F32),
            jax.ShapeDtypeStruct((T, GLA_VAL_WIDTH), BF16),
            jax.ShapeDtypeStruct((T, 2 * GLA_KEY_WIDTH), F32),
            jax.ShapeDtypeStruct((T // S, ATT_CLASSES, S // ATT_CLASSES, 3 * ATT_WIDTH), F32),
        ],
        scratch_shapes=[pltpu.VMEM((3 * ATT_WIDTH // LANES, tm, LANES), F32),
                        pltpu.VMEM((D_MODEL, o_lr), BF16), pltpu.VMEM((D_MODEL, LANES), BF16)],
        compiler_params=_cparams(("arbitrary",)),
        name="inproj",
    )(x2, norm1_w[None, :], w_in, w_in, wa, gw, gb, rc, rs1, rs2)


def _gla_decays(q, k, v, la, forward, G):
    C = GLA_CHUNK
    R = G * C
    r = lax.broadcasted_iota(jnp.int32, (R, R), 0)
    c = lax.broadcasted_iota(jnp.int32, (R, R), 1)
    same = (r >> GLA_CHUNK_SHIFT) == (c >> GLA_CHUNK_SHIFT)
    tri = (c <= r) if forward else (c >= r)
    t_mat = jnp.where(same, jnp.where(tri, 1.0, 0.0), 0.0).astype(BF16)
    hi = la.astype(BF16)
    lo = (la - hi.astype(F32)).astype(BF16)
    b = _dot(t_mat, hi) + _dot(t_mat, lo)
    edge = C - 1 if forward else 0
    tot = jnp.concatenate([jnp.broadcast_to(b[g * C + edge:g * C + edge + 1], (C, GLA_KEY_WIDTH))
                           for g in range(G)], axis=0)
    order = list(range(G)) if forward else list(range(G - 1, -1, -1))
    return dict(q_dec=q * jnp.exp(b), k_inv=(k * jnp.exp(-b)).astype(BF16), k_end=k * jnp.exp(tot - b),
                tot=tot, vb=v.astype(BF16), order=order, forward=forward, G=G)


def _gla_scores(prep):
    C, H = GLA_CHUNK, GLA_HEADS
    lane_k = lax.broadcasted_iota(jnp.int32, (C, GLA_KEY_WIDTH), 1)
    qd_heads, scores = {}, {}
    for g in prep["order"]:
        rows = slice(g * C, (g + 1) * C)
        qd = prep["q_dec"][rows]
        qd_heads[g] = jnp.concatenate([jnp.where((lane_k >> GLA_DK_SHIFT) == h, qd, 0.0) for h in range(H)],
                                      axis=0).astype(BF16)
        scores[g] = _dot_nt(qd_heads[g], prep["k_inv"][rows])
    return qd_heads, scores


def _gla_chunk_updates(prep):
    C, H, G = GLA_CHUNK, GLA_HEADS, prep["G"]
    k_end, tot, vb = prep["k_end"], prep["tot"], prep["vb"]
    kv, dec_t = {}, {}
    lane = lax.broadcasted_iota(jnp.int32, (GLA_KEY_WIDTH, 2 * C), 1)
    zeros = jnp.zeros((C, GLA_DV), BF16)
    for p in range(G // 2):
        pair = slice(2 * p * C, (2 * p + 2) * C)
        ke_t = k_end[pair].T.astype(BF16)
        tot_t = tot[pair].T
        swapped = pltpu.roll(tot_t, C, 1)
        for half in range(2):
            g = 2 * p + half
            rows = slice(g * C, (g + 1) * C)
            own = (lane < C) if half == 0 else (lane >= C)
            dec_t[g] = jnp.exp(jnp.where(own, tot_t, swapped))
            parts = []
            for h in range(H):
                v_h = vb[rows, h * GLA_DV:(h + 1) * GLA_DV]
                v_pad = jnp.concatenate([v_h, zeros] if half == 0 else [zeros, v_h], axis=0)
                parts.append(_dot(ke_t[h * C:(h + 1) * C], v_pad))
            kv[g] = jnp.concatenate(parts, axis=0)
    return kv, dec_t


def _gla_states(prep, kv, dec_t, s_ref):
    st = s_ref[...]
    states = {}
    for g in prep["order"]:
        states[g] = st.astype(BF16)
        st = st * dec_t[g] + kv[g]
    s_ref[...] = st
    return states


def _gla_outputs(prep, qd_heads, scores, inter, o_ref):
    C, H = GLA_CHUNK, GLA_HEADS
    row_q = lax.broadcasted_iota(jnp.int32, (H * C, C), 0) & (C - 1)
    col_k = lax.broadcasted_iota(jnp.int32, (H * C, C), 1)
    a_mask = (col_k <= row_q) if prep["forward"] else (col_k >= row_q)
    for g in prep["order"]:
        rows = slice(g * C, (g + 1) * C)
        a = jnp.where(a_mask, scores[g], 0.0).astype(BF16)
        vv = prep["vb"][rows]
        o_ref[rows, :] = jnp.concatenate(
            [_dot(a[h * C:(h + 1) * C], vv[:, h * GLA_DV:(h + 1) * GLA_DV]) + inter[g][h * C:(h + 1) * C]
             for h in range(H)], axis=1).astype(o_ref.dtype)


def _gla_kernel(qf_ref, kf_ref, vf_ref, laf_ref, qb_ref, kb_ref, vb_ref, lab_ref,
                of_ref, ob_ref, sf_ref, sb_ref, *, G):
    @pl.when(pl.program_id(1) == 0)
    def _():
        sf_ref[...] = jnp.zeros_like(sf_ref)
        sb_ref[...] = jnp.zeros_like(sb_ref)

    dirs = [(_gla_decays(qf_ref[...], kf_ref[...], vf_ref[...], laf_ref[...], True, G), sf_ref, of_ref),
            (_gla_decays(qb_ref[...], kb_ref[...], vb_ref[...], lab_ref[...], False, G), sb_ref, ob_ref)]
    scored = [_gla_scores(prep) for prep, _, _ in dirs]
    updates = [_gla_chunk_updates(prep) for prep, _, _ in dirs]
    states = [_gla_states(prep, kv, dec_t, s_ref) for (prep, s_ref, _), (kv, dec_t) in zip(dirs, updates)]
    inters = [{g: _dot(qd_heads[g], st[g]) for g in prep["order"]}
              for (prep, _, _), (qd_heads, _), st in zip(dirs, scored, states)]
    for (prep, _, o_ref), (qd_heads, scores), inter in zip(dirs, scored, inters):
        _gla_outputs(prep, qd_heads, scores, inter, o_ref)


def _gla(gla_slab, loga, B, S, G=8):
    T = B * S
    R = G * GLA_CHUNK
    ns = S // R
    fwd = lambda col: (lambda b, i: (b * ns + i, col))
    bwd = lambda col: (lambda b, i: (b * ns + ns - 1 - i, col))
    kw, vw = GLA_KEY_WIDTH, GLA_VAL_WIDTH
    return pl.pallas_call(
        functools.partial(_gla_kernel, G=G),
        grid=(B, ns),
        in_specs=[
            pl.BlockSpec((R, kw), fwd(0)), pl.BlockSpec((R, kw), fwd(1)),
            pl.BlockSpec((R, vw), fwd(1)), pl.BlockSpec((R, kw), fwd(0)),
            pl.BlockSpec((R, kw), bwd(0)), pl.BlockSpec((R, kw), bwd(1)),
            pl.BlockSpec((R, vw), bwd(1)), pl.BlockSpec((R, kw), bwd(1)),
        ],
        out_specs=[pl.BlockSpec((R, vw), fwd(0)), pl.BlockSpec((R, vw), bwd(0))],
        out_shape=[jax.ShapeDtypeStruct((T, vw), BF16), jax.ShapeDtypeStruct((T, vw), BF16)],
        scratch_shapes=[pltpu.VMEM((kw, GLA_DV), F32), pltpu.VMEM((kw, GLA_DV), F32)],
        compiler_params=_cparams(("arbitrary", "arbitrary")),
        name="gla",
    )(gla_slab, gla_slab, gla_slab, loga, gla_slab, gla_slab, gla_slab, loga)


ATT_CLASSES = 4
ATT_QB = 128
ATT_KB = ATT_QB + 2 * ATT_RADIUS


ATT_UNROLL = (32, 16, 16)


def _att_kernel(q_ref, k_ref, v_ref, o_ref, m_ref, l_ref, bias_ref, *, S):
    QB, KB, NC = ATT_QB, ATT_KB, ATT_CLASSES
    L4 = S // NC
    lane = lax.broadcasted_iota(jnp.int32, (QB, LANES), 1)
    head0 = lane < ATT_HEAD_DIM

    @pl.when((pl.program_id(0) == 0) & (pl.program_id(1) == 0))
    def _():
        rowi = lax.broadcasted_iota(jnp.int32, (2 * QB, KB), 0) & (QB - 1)
        coli = lax.broadcasted_iota(jnp.int32, (2 * QB, KB), 1)
        qpos = (rowi & (QB // NC - 1)) * NC + (rowi >> _log2(QB // NC))
        kpos = (coli & (KB // NC - 1)) * NC + (coli >> _log2(KB // NC))
        for case in range(3):
            bias_ref[0, case] = jnp.where(jnp.abs(rowi - coli + case * ATT_RADIUS) <= ATT_RADIUS, 0.0, NEG_INF)
            bias_ref[1, case] = jnp.where(jnp.abs(qpos - kpos + case * ATT_RADIUS) <= ATT_RADIUS, 0.0, NEG_INF)

    for pi, (_, d) in enumerate(DILATED_PATTERNS):
        L = S // d
        nb = L // QB
        shift = nb.bit_length() - 1
        first = pi == 0
        last = pi == len(DILATED_PATTERNS) - 1

        def scores(n, d=d, L=L, nb=nb, shift=shift):
            cls = n >> shift
            q0 = (n & (nb - 1)) * QB
            ws = jnp.clip(q0 - ATT_RADIUS, 0, L - KB)
            if d == 1:
                qsls = [pl.ds(pl.multiple_of(c * L4 + q0 // NC, QB // NC), QB // NC) for c in range(NC)]
                ksls = [pl.ds(pl.multiple_of(c * L4 + ws // NC, ATT_RADIUS // NC), KB // NC) for c in range(NC)]
            elif d == NC:
                qsls = [pl.ds(pl.multiple_of(cls * L4 + q0, QB), QB)]
                ksls = [pl.ds(pl.multiple_of(cls * L4 + ws, ATT_RADIUS), KB)]
            else:
                base = (cls & (NC - 1)) * L4 + (cls >> _log2(NC))
                qsls = [pl.ds(base + NC * q0, QB, stride=NC)]
                ksls = [pl.ds(base + NC * ws, KB, stride=NC)]
            q = jnp.concatenate([q_ref[sl, :] for sl in qsls], axis=0)
            kw = jnp.concatenate([k_ref[sl, :] for sl in ksls], axis=0)
            kb = kw.astype(BF16)
            bias = bias_ref[1 if d == 1 else 0, (q0 - ws) >> _log2(ATT_RADIUS), :QB]
            q_heads = (jnp.where(head0, q, 0.0), jnp.where(head0, 0.0, q))
            s = [_dot_nt(qh.astype(BF16), kb) + bias for qh in q_heads]
            return qsls, ksls, s

        def softmax_pv(qsls, ksls, s):
            vw = jnp.concatenate([v_ref[sl, :] for sl in ksls], axis=0)
            v_ones = jnp.concatenate([vw.astype(BF16), jnp.ones((KB, LANES), BF16)], axis=1)
            m_h = [jnp.max(t, axis=-1, keepdims=True) for t in s]
            pv = [_dot(jnp.exp2(t - m).astype(BF16), v_ones) for t, m in zip(s, m_h)]
            acc_b = jnp.where(head0, pv[0][:, :LANES], pv[1][:, :LANES])
            m_b = jnp.where(head0, m_h[0], m_h[1])
            l_b = jnp.where(head0, pv[0][:, LANES:], pv[1][:, LANES:])
            return qsls, acc_b, m_b, l_b

        def load(ref, sls):
            return jnp.concatenate([ref[sl, :] for sl in sls], axis=0)

        def store(ref, sls, val):
            n = val.shape[0] // len(sls)
            for i, sl in enumerate(sls):
                ref[sl, :] = val[i * n:(i + 1) * n]

        unroll = ATT_UNROLL[pi]

        def body(n, carry, first=first, last=last, unroll=unroll):
            staged = [scores(n * unroll + u) for u in range(unroll)]
            blocks = [softmax_pv(*st) for st in staged]
            for qsls, acc_b, m_b, l_b in blocks:
                if first:
                    acc, m_new, l_new = acc_b, m_b, l_b
                else:
                    m_old = load(m_ref, qsls)
                    m_new = jnp.maximum(m_old, m_b)
                    w_old = jnp.exp2(m_old - m_new)
                    w_blk = jnp.exp2(m_b - m_new)
                    acc = load(o_ref, qsls) * w_old + acc_b * w_blk
                    l_new = load(l_ref, qsls) * w_old + l_b * w_blk
                if last:
                    store(o_ref, qsls, acc / l_new)
                else:
                    store(o_ref, qsls, acc)
                    store(m_ref, qsls, m_new)
                    store(l_ref, qsls, l_new)
            return carry

        lax.fori_loop(0, S // (QB * unroll), body, 0)


def _attention(att_slab, B, S):
    T = B * S
    ncol = ATT_WIDTH // LANES
    return pl.pallas_call(
        functools.partial(_att_kernel, S=S),
        grid=(B, ncol),
        in_specs=[
            pl.BlockSpec((S, LANES), lambda b, h: (b, h)),
            pl.BlockSpec((S, LANES), lambda b, h: (b, ncol + h)),
            pl.BlockSpec((S, LANES), lambda b, h: (b, 2 * ncol + h)),
        ],
        out_specs=pl.BlockSpec((S, LANES), lambda b, h: (b, h)),
        out_shape=jax.ShapeDtypeStruct((T, ATT_WIDTH), F32),
        scratch_shapes=[pltpu.VMEM((S, LANES), F32), pltpu.VMEM((S, LANES), F32),
                        pltpu.VMEM((2, 3, 2 * ATT_QB, ATT_KB), F32)],
        compiler_params=_cparams(("arbitrary", "arbitrary")),
        name="dilated_attention",
    )(att_slab, att_slab, att_slab)


PACK_WORDS = D_MODEL // 2
ROW_TILE = PACK_WORDS // LANES
HIGH_HALF = -65536


def _pack_rows(x):
    bits = lambda v: lax.bitcast_convert_type(v.astype(BF16).astype(F32), jnp.int32)
    low = (bits(x[:, :PACK_WORDS]) >> 16) & 0xFFFF
    return (bits(x[:, PACK_WORDS:]) & HIGH_HALF) | low


def _unpack_rows(w):
    low = lax.bitcast_convert_type(w << 16, F32)
    high = lax.bitcast_convert_type(w & HIGH_HALF, F32)
    return jnp.concatenate([low, high], axis=1).astype(BF16)


def _to_row_tiles(ref, w):
    n = w.shape[0]
    for j in range(ROW_TILE):
        ref[pl.ds(j, n, stride=ROW_TILE), :] = w[:, j * LANES:(j + 1) * LANES]


def _from_row_tiles(ref, n):
    return jnp.concatenate([ref[pl.ds(j, n, stride=ROW_TILE), :] for j in range(ROW_TILE)], axis=1)


def _tile_copy(src_ref, src_row, dst_ref, dst_row, sem):
    src = pl.ds(pl.multiple_of(src_row * ROW_TILE, ROW_TILE), ROW_TILE)
    dst = pl.ds(pl.multiple_of(dst_row * ROW_TILE, ROW_TILE), ROW_TILE)
    return pltpu.make_async_copy(src_ref.at[src], dst_ref.at[dst], sem)


def _outproj_kernel(of_ref, ob_ref, gg_ref, att_ref, x_ref, gnw_ref, wo1_ref, wo2_ref,
                    n2_ref, wr_ref, br_ref, h_ref, u_ref, lg_ref, stage_ref):
    rows = stage_ref.shape[1] // ATT_CLASSES
    for j in range(ATT_WIDTH // LANES):
        for c in range(ATT_CLASSES):
            stage_ref[j, pl.ds(c, rows, stride=ATT_CLASSES), :] = att_ref[c, :, j * LANES:(j + 1) * LANES]
    att = jnp.concatenate([stage_ref[j] for j in range(ATT_WIDTH // LANES)], axis=1)
    o = of_ref[...].astype(F32) + ob_ref[...].astype(F32)
    gate = gg_ref[...].astype(F32)
    gnw = gnw_ref[...]
    parts = []
    for h in range(GLA_HEADS):
        sl = slice(h * GLA_DV, (h + 1) * GLA_DV)
        parts.append(_rms(o[:, sl], gnw))
    y = jnp.concatenate(parts, axis=1) * (gate / (1.0 + jnp.exp(-gate)))
    mix = _dot(y.astype(BF16), wo1_ref[...]) + _dot(att.astype(BF16), wo2_ref[...])
    h = x_ref[...] + mix
    h_ref[...] = h
    u = _rms(h, n2_ref[...])
    _to_row_tiles(u_ref, _pack_rows(u))
    u_hi = u.astype(BF16)
    u_lo = (u - u_hi.astype(F32)).astype(BF16)
    hi_both = _dot_nt(wr_ref[...], u_hi)
    lg_ref[...] = (hi_both[:LANES] + hi_both[LANES:] + _dot_nt(wr_ref[:LANES], u_lo)) + br_ref[...]


def _outproj(o_f, o_b, gate, att_out, x2, gla_norm_w, w_out, norm2_w, wr, br, tm=512):
    T = x2.shape[0]
    nS = att_out.shape[2] * ATT_CLASSES // tm
    row = lambda i: (i, 0)
    const = lambda i: (0, 0)
    wo = w_out.astype(BF16)
    wr_hi = wr.astype(BF16)
    wr_lo = (wr - wr_hi.astype(F32)).astype(BF16)
    wr = jnp.concatenate([wr_hi, wr_lo], axis=0)
    return pl.pallas_call(
        _outproj_kernel,
        grid=(T // tm,),
        in_specs=[
            pl.BlockSpec((tm, GLA_VAL_WIDTH), row),
            pl.BlockSpec((tm, GLA_VAL_WIDTH), row),
            pl.BlockSpec((tm, GLA_VAL_WIDTH), row),
            pl.BlockSpec((None, ATT_CLASSES, tm // ATT_CLASSES, ATT_WIDTH), lambda i: (i // nS, 0, i % nS, 0)),
            pl.BlockSpec((tm, D_MODEL), row),
            pl.BlockSpec((1, GLA_DV), const),
            pl.BlockSpec((GLA_VAL_WIDTH, D_MODEL), const),
            pl.BlockSpec((ATT_WIDTH, D_MODEL), lambda i: (GLA_VAL_WIDTH // ATT_WIDTH, 0)),
            pl.BlockSpec((1, D_MODEL), const),
            pl.BlockSpec((2 * LANES, D_MODEL), const),
            pl.BlockSpec((LANES, 1), const),
        ],
        out_specs=[
            pl.BlockSpec((tm, D_MODEL), row),
            pl.BlockSpec((tm * ROW_TILE, LANES), row),
            pl.BlockSpec((LANES, tm), lambda i: (0, i)),
        ],
        out_shape=[
            jax.ShapeDtypeStruct((T, D_MODEL), F32),
            jax.ShapeDtypeStruct((T * ROW_TILE, LANES), jnp.int32),
            jax.ShapeDtypeStruct((LANES, T), F32),
        ],
        scratch_shapes=[pltpu.VMEM((ATT_WIDTH // LANES, tm, LANES), F32)],
        compiler_params=_cparams(("arbitrary",)),
        name="outproj",
    )(o_f, o_b, gate, att_out, x2, gla_norm_w[None, :], wo, wo,
      norm2_w[None, :], wr, br)


INFO_E1, INFO_E2, INFO_R1, INFO_R2, INFO_W1, INFO_W2 = range(6)
ROUTE_ROWS = 40


def _route_kernel(lg_ref, info_ref, cnt_ref, carry_ref):
    @pl.when(pl.program_id(0) == 0)
    def _():
        carry_ref[...] = jnp.zeros_like(carry_ref)

    lg = lg_ref[:ROUTE_ROWS, :]
    tr = lg.shape[1]
    row = lax.broadcasted_iota(jnp.int32, (ROUTE_ROWS, tr), 0)
    big = jnp.int32(1 << 20)
    is_g = (row >= MOE_N_EXPERTS) & (row < MOE_N_EXPERTS + MOE_GROUPS)
    gl = jnp.where(is_g, lg, -jnp.inf)
    gmax = jnp.max(gl, axis=0, keepdims=True)
    gsel = jnp.min(jnp.where(gl == gmax, row - MOE_N_EXPERTS, big), axis=0, keepdims=True)
    g_w = 1.0 / jnp.sum(jnp.where(is_g, jnp.exp(lg - gmax), 0.0), axis=0, keepdims=True)
    in_grp = (row < MOE_N_EXPERTS) & ((row >> MOE_GROUP_SHIFT) == gsel)
    el = jnp.where(in_grp, lg, -jnp.inf)
    v1 = jnp.max(el, axis=0, keepdims=True)
    i1 = jnp.min(jnp.where(el == v1, row, big), axis=0, keepdims=True)
    el2 = jnp.where(row == i1, -jnp.inf, el)
    v2 = jnp.max(el2, axis=0, keepdims=True)
    i2 = jnp.min(jnp.where(el2 == v2, row, big), axis=0, keepdims=True)
    t = jnp.exp(v2 - v1)
    w1 = g_w * (1.0 / (1.0 + t))
    w2 = g_w * (t / (1.0 + t))

    erow = lax.broadcasted_iota(jnp.int32, (MOE_N_EXPERTS, tr), 0)
    hit1 = erow == i1
    hit2 = erow == i2
    member = jnp.where(hit1 | hit2, 1.0, 0.0)
    r = lax.broadcasted_iota(jnp.int32, (tr, tr), 0)
    c = lax.broadcasted_iota(jnp.int32, (tr, tr), 1)
    earlier = jnp.where(r < c, 1.0, 0.0).astype(BF16)
    carry = carry_ref[...]
    prefix = _dot(member.astype(BF16), earlier) + carry[:, 0:1]
    rank1 = jnp.sum(jnp.where(hit1, prefix, 0.0), axis=0, keepdims=True)
    rank2 = jnp.sum(jnp.where(hit2, prefix, 0.0), axis=0, keepdims=True)
    carry = carry + jnp.sum(member, axis=1, keepdims=True)
    carry_ref[...] = carry
    cnt_ref[...] = carry

    zero = jnp.zeros_like(w1)
    info_ref[...] = jnp.concatenate([i1.astype(F32), i2.astype(F32), rank1, rank2, w1, w2, zero, zero], axis=0)


def _route(logits_t, tr=1024):
    T = logits_t.shape[1]
    return pl.pallas_call(
        _route_kernel,
        grid=(T // tr,),
        in_specs=[pl.BlockSpec((LANES, tr), lambda i: (0, i))],
        out_specs=[pl.BlockSpec((8, tr), lambda i: (0, i)),
                   pl.BlockSpec((MOE_N_EXPERTS, LANES), lambda i: (0, 0))],
        out_shape=[jax.ShapeDtypeStruct((8, T), F32), jax.ShapeDtypeStruct((MOE_N_EXPERTS, LANES), F32)],
        scratch_shapes=[pltpu.VMEM((MOE_N_EXPERTS, LANES), F32)],
        compiler_params=_cparams(("arbitrary",)),
        name="route",
    )(logits_t)


ROW_UNROLL = 16


def _dispatch_kernel(dest_ref, pend_ref, u_ref, xs_ref, zbuf, sem, zsem, *, td, T, nblk):
    @pl.when(pl.program_id(0) == 0)
    def _():
        zbuf[...] = jnp.zeros_like(zbuf)
        n_used = pend_ref[MOE_N_EXPERTS - 1] >> MOE_ROWS_SHIFT

        def zero_copy(blk):
            start = pl.multiple_of(blk * (MOE_ROWS * ROW_TILE), MOE_ROWS * ROW_TILE)
            return pltpu.make_async_copy(zbuf, xs_ref.at[pl.ds(start, MOE_ROWS * ROW_TILE)], zsem)

        def each_pad_block(fn):
            def per_expert(e, carry):
                prev = jnp.where(e > 0, pend_ref[jnp.maximum(e - 1, 0)], 0)

                @pl.when(pend_ref[e] > prev)
                def _():
                    fn((pend_ref[e] >> MOE_ROWS_SHIFT) - 1)
                return carry

            def per_tail(j, carry):
                @pl.when(n_used + j < nblk)
                def _():
                    fn(n_used + j)
                return carry

            lax.fori_loop(0, MOE_N_EXPERTS, per_expert, 0)
            lax.fori_loop(0, MOE_N_EXPERTS, per_tail, 0)

        each_pad_block(lambda blk: zero_copy(blk).start())
        each_pad_block(lambda blk: zero_copy(blk).wait())

    base = pl.program_id(0) * td

    def issue(g, carry):
        for j in range(ROW_UNROLL):
            r = g * ROW_UNROLL + j
            for k in range(MOE_TOP_K):
                _tile_copy(u_ref, r, xs_ref, dest_ref[k * T + base + r], sem).start(priority=k)
        return carry

    lax.fori_loop(0, td // ROW_UNROLL, issue, 0)
    for k in range(MOE_TOP_K):
        pltpu.make_async_copy(u_ref, xs_ref.at[pl.ds(0, td * ROW_TILE)], sem).wait()


def _dispatch(dest, pend, u2, cap, td=2048):
    T = u2.shape[0] // ROW_TILE
    return pl.pallas_call(
        functools.partial(_dispatch_kernel, td=td, T=T, nblk=cap // MOE_ROWS),
        grid_spec=pltpu.PrefetchScalarGridSpec(
            num_scalar_prefetch=2,
            grid=(T // td,),
            in_specs=[pl.BlockSpec((td * ROW_TILE, LANES), lambda i, d, z: (i, 0))],
            out_specs=pl.BlockSpec(memory_space=pl.ANY),
            scratch_shapes=[pltpu.VMEM((MOE_ROWS * ROW_TILE, LANES), jnp.int32),
                            pltpu.SemaphoreType.DMA(()), pltpu.SemaphoreType.DMA(())],
        ),
        out_shape=jax.ShapeDtypeStruct((cap * ROW_TILE, LANES), jnp.int32),
        compiler_params=_cparams(("arbitrary",)),
        name="dispatch",
    )(dest, pend, u2)


EXPERT_GROUP = 4


def _expert_kernel(pend_ref, xs_hbm, wg_hbm, wu_hbm, wd_hbm, ys_hbm,
                   xbuf, ybuf, zbuf, stage_g, stage_u, stage_d, wgb, wub, wdb, xsem, ysem, wsem, zsem, *, nblk):
    last = MOE_N_EXPERTS - 1
    n_used = pend_ref[last] >> MOE_ROWS_SHIFT
    n_pairs = (n_used + EXPERT_GROUP - 1) >> _log2(EXPERT_GROUP)
    block_rows = MOE_ROWS * ROW_TILE

    def rows_of(b):
        return pl.ds(pl.multiple_of(b * block_rows, block_rows), block_rows)

    def x_copy(b, slot):
        return pltpu.make_async_copy(xs_hbm.at[rows_of(b)], xbuf.at[slot], xsem.at[slot])

    def y_copy(b, slot):
        return pltpu.make_async_copy(ybuf.at[slot], ys_hbm.at[rows_of(b)], ysem.at[slot])

    def zero_copy(b):
        return pltpu.make_async_copy(zbuf, ys_hbm.at[rows_of(b)], zsem)

    def weight_copies(e):
        return (pltpu.make_async_copy(wg_hbm.at[e], stage_g, wsem.at[0]),
                pltpu.make_async_copy(wu_hbm.at[e], stage_u, wsem.at[1]),
                pltpu.make_async_copy(wd_hbm.at[e], stage_d, wsem.at[2]))

    def owner(start, row):
        return lax.while_loop(lambda e: (e < last) & (pend_ref[e] <= row), lambda e: e + 1, start)

    for c in weight_copies(owner(0, 0)):
        c.start()
    for i in range(EXPERT_GROUP):
        x_copy(i, i).start()

    zbuf[...] = jnp.zeros_like(zbuf)

    def tail(fn):
        def step(b, carry):
            fn(b)
            return carry
        lax.fori_loop(n_pairs * EXPERT_GROUP, nblk, step, 0)

    tail(lambda b: zero_copy(b).start())

    def body(p, carry):
        cur, run = carry
        half = (p & 1) * EXPERT_GROUP
        for i in range(EXPERT_GROUP):
            x_copy(p * EXPERT_GROUP + i, half + i).wait()

        @pl.when(p + 1 < n_pairs)
        def _():
            for i in range(EXPERT_GROUP):
                x_copy((p + 1) * EXPERT_GROUP + i, EXPERT_GROUP - half + i).start()

        slots = []
        for i in range(EXPERT_GROUP):
            b = p * EXPERT_GROUP + i
            e = jnp.where(b < n_used, owner(jnp.maximum(cur, 0), b * MOE_ROWS), cur)
            fresh = e != cur
            run = run + fresh.astype(jnp.int32)
            slot = run & (EXPERT_GROUP - 1)

            @pl.when(fresh)
            def _(e=e, slot=slot):
                for c in weight_copies(e):
                    c.wait()
                wgb[slot] = stage_g[...].astype(BF16)
                wub[slot] = stage_u[...].astype(BF16)
                wdb[slot] = stage_d[...].astype(BF16)

                @pl.when(pend_ref[e] < pend_ref[last])
                def _():
                    for c in weight_copies(owner(e + 1, pend_ref[e])):
                        c.start(priority=1)

            cur = e
            slots.append(slot)

        @pl.when(p >= 2)
        def _():
            for i in range(EXPERT_GROUP):
                y_copy((p - 2) * EXPERT_GROUP + i, half + i).wait()

        for i in range(EXPERT_GROUP):
            xb = _unpack_rows(_from_row_tiles(xbuf.at[half + i], MOE_ROWS))
            g = _dot(xb, wgb[slots[i]])
            u = _dot(xb, wub[slots[i]])
            hid = (g / (1.0 + jnp.exp(-g))) * u
            _to_row_tiles(ybuf.at[half + i], _pack_rows(_dot(hid.astype(BF16), wdb[slots[i]])))
        for i in range(EXPERT_GROUP):
            y_copy(p * EXPERT_GROUP + i, half + i).start()
        return cur, run

    lax.fori_loop(0, n_pairs, body, (jnp.int32(-1), jnp.int32(-1)))

    def drain(p):
        for i in range(EXPERT_GROUP):
            y_copy(p * EXPERT_GROUP + i, (p & 1) * EXPERT_GROUP + i).wait()

    @pl.when(n_pairs >= 2)
    def _():
        drain(n_pairs - 2)
    drain(n_pairs - 1)
    tail(lambda b: zero_copy(b).wait())


def _experts(pend, xs, w_gate, w_up, w_down):
    cap = xs.shape[0] // ROW_TILE
    nblk = cap // MOE_ROWS
    assert nblk % EXPERT_GROUP == 0
    block = (MOE_ROWS * ROW_TILE, LANES)
    anywhere = pl.BlockSpec(memory_space=pl.ANY)
    return pl.pallas_call(
        functools.partial(_expert_kernel, nblk=nblk),
        grid_spec=pltpu.PrefetchScalarGridSpec(
            num_scalar_prefetch=1,
            grid=(1,),
            in_specs=[anywhere, anywhere, anywhere, anywhere],
            out_specs=anywhere,
            scratch_shapes=[pltpu.VMEM((2 * EXPERT_GROUP,) + block, jnp.int32),
                            pltpu.VMEM((2 * EXPERT_GROUP,) + block, jnp.int32),
                            pltpu.VMEM(block, jnp.int32),
                            pltpu.VMEM((D_MODEL, MOE_D_FF), F32),
                            pltpu.VMEM((D_MODEL, MOE_D_FF), F32),
                            pltpu.VMEM((MOE_D_FF, D_MODEL), F32),
                            pltpu.VMEM((EXPERT_GROUP, D_MODEL, MOE_D_FF), BF16),
                            pltpu.VMEM((EXPERT_GROUP, D_MODEL, MOE_D_FF), BF16),
                            pltpu.VMEM((EXPERT_GROUP, MOE_D_FF, D_MODEL), BF16),
                            pltpu.SemaphoreType.DMA((2 * EXPERT_GROUP,)),
                            pltpu.SemaphoreType.DMA((2 * EXPERT_GROUP,)),
                            pltpu.SemaphoreType.DMA((3,)),
                            pltpu.SemaphoreType.DMA(())],
        ),
        out_shape=jax.ShapeDtypeStruct((cap * ROW_TILE, LANES), jnp.int32),
        compiler_params=_cparams(("arbitrary",)),
        name="experts",
    )(pend, xs, w_gate, w_up, w_down)


def _combine_kernel(dest_ref, ys_ref, info_ref, h_ref, fw_ref, o_ref, buf, sem, *, tc, T):
    i = pl.program_id(0)
    n = pl.num_programs(0)

    def issue(step, slot):
        base = step * tc

        def body(g, carry):
            for j in range(ROW_UNROLL):
                r = g * ROW_UNROLL + j
                for k in range(MOE_TOP_K):
                    _tile_copy(ys_ref, dest_ref[k * T + base + r], buf.at[slot, k], r,
                               sem.at[slot]).start(priority=k)
            return carry

        lax.fori_loop(0, tc // ROW_UNROLL, body, 0)

    @pl.when(i == 0)
    def _():
        issue(0, 0)

    slot = i % 2

    @pl.when(i + 1 < n)
    def _():
        issue(i + 1, 1 - slot)

    for k in range(MOE_TOP_K):
        pltpu.make_async_copy(ys_ref.at[pl.ds(0, tc * ROW_TILE)], buf.at[slot, k], sem.at[slot]).wait()

    info_t = jnp.concatenate([info_ref[...]] * (LANES // 8), axis=0).T
    w1 = info_t[:, INFO_W1:INFO_W1 + 1]
    w2 = info_t[:, INFO_W2:INFO_W2 + 1]
    y1 = _unpack_rows(_from_row_tiles(buf.at[slot, 0], tc)).astype(F32)
    y2 = _unpack_rows(_from_row_tiles(buf.at[slot, 1], tc)).astype(F32)
    h = h_ref[...] + (y1 * w1 + y2 * w2)
    o_ref[...] = _rms(h, fw_ref[...])


def _combine(dest, ys, info, h, final_w, tc=512):
    T = h.shape[0]
    return pl.pallas_call(
        functools.partial(_combine_kernel, tc=tc, T=T),
        grid_spec=pltpu.PrefetchScalarGridSpec(
            num_scalar_prefetch=1,
            grid=(T // tc,),
            in_specs=[pl.BlockSpec(memory_space=pl.ANY),
                      pl.BlockSpec((8, tc), lambda i, d: (0, i)),
                      pl.BlockSpec((tc, D_MODEL), lambda i, d: (i, 0)),
                      pl.BlockSpec((1, D_MODEL), lambda i, d: (0, 0))],
            out_specs=pl.BlockSpec((tc, D_MODEL), lambda i, d: (i, 0)),
            scratch_shapes=[pltpu.VMEM((2, MOE_TOP_K, tc * ROW_TILE, LANES), jnp.int32),
                            pltpu.SemaphoreType.DMA((2,))],
        ),
        out_shape=jax.ShapeDtypeStruct((T, D_MODEL), F32),
        compiler_params=_cparams(("arbitrary",)),
        name="combine",
    )(dest, ys, info, h, final_w[None, :])


def _plan_kernel(info_ref, cnt_ref, dest_ref, pend_ref):
    cnt = cnt_ref[...].astype(jnp.int32)
    nblk_e = ((cnt + (MOE_ROWS - 1)) >> MOE_ROWS_SHIFT).astype(F32)
    r = lax.broadcasted_iota(jnp.int32, (MOE_N_EXPERTS, MOE_N_EXPERTS), 0)
    c = lax.broadcasted_iota(jnp.int32, (MOE_N_EXPERTS, MOE_N_EXPERTS), 1)
    before = jnp.where(c < r, 1.0, 0.0).astype(BF16)
    first_blk = _dot(before, nblk_e.astype(BF16))
    pstart = first_blk[:, 0:1] * float(MOE_ROWS)
    pend_ref[...] = ((first_blk + nblk_e) * float(MOE_ROWS)).astype(jnp.int32)

    info = info_ref[...]
    erow = lax.broadcasted_iota(jnp.int32, (MOE_N_EXPERTS, info.shape[1]), 0)
    start_of = lambda e: jnp.sum(jnp.where(erow == e.astype(jnp.int32), pstart, 0.0), axis=0, keepdims=True)
    d1 = info[INFO_R1:INFO_R1 + 1] + start_of(info[INFO_E1:INFO_E1 + 1])
    d2 = info[INFO_R2:INFO_R2 + 1] + start_of(info[INFO_E2:INFO_E2 + 1])
    zero = jnp.zeros_like(d1)
    dest_ref[...] = jnp.concatenate([d1, d2] + [zero] * 6, axis=0).astype(jnp.int32)


def _plan(info, counts, tr=2048):
    T = info.shape[1]
    dest8, pend = pl.pallas_call(
        _plan_kernel,
        grid=(T // tr,),
        in_specs=[pl.BlockSpec((8, tr), lambda i: (0, i)),
                  pl.BlockSpec((MOE_N_EXPERTS, LANES), lambda i: (0, 0))],
        out_specs=[pl.BlockSpec((8, tr), lambda i: (0, i)),
                   pl.BlockSpec((MOE_N_EXPERTS, LANES), lambda i: (0, 0))],
        out_shape=[jax.ShapeDtypeStruct((8, T), jnp.int32),
                   jax.ShapeDtypeStruct((MOE_N_EXPERTS, LANES), jnp.int32)],
        compiler_params=_cparams(("arbitrary",)),
        name="plan",
    )(info, counts)
    return dest8[:MOE_TOP_K].reshape(-1), pend[:, 0]


def _moe_capacity(T):
    return (-(-(T * MOE_TOP_K) // MOE_ROWS) + MOE_N_EXPERTS) * MOE_ROWS


def _router_weights(router_group_w, router_group_b, router_expert_w, router_expert_b):
    we = jnp.transpose(router_expert_w, (0, 2, 1)).reshape(MOE_N_EXPERTS, D_MODEL)
    pad = LANES - MOE_N_EXPERTS - MOE_GROUPS
    wr = jnp.concatenate([we, router_group_w.T, jnp.zeros((pad, D_MODEL), F32)], axis=0)
    br = jnp.concatenate([router_expert_b.reshape(-1), router_group_b, jnp.zeros((pad,), F32)])[:, None]
    return wr, br


def kernel(x, norm1_w, w_in, gla_fwd_gate_w, gla_fwd_gate_b, gla_bwd_gate_w, gla_bwd_gate_b,
           gla_norm_w, w_out, norm2_w, router_group_w, router_group_b, router_expert_w,
           router_expert_b, expert_w_gate, expert_w_up, expert_w_down, final_norm_w):
    B, S, D = x.shape
    T = B * S
    assert norm1_w.shape[0] == 1, "single-layer trunk: the final norm is fused into the combine step"
    h = x.reshape(T, D)
    gla_slab, gate, loga, att_slab = _inproj(h, S, norm1_w[0], w_in[0], gla_fwd_gate_w[0], gla_fwd_gate_b[0],
                                       gla_bwd_gate_w[0], gla_bwd_gate_b[0])
    o_f, o_b = _gla(gla_slab, loga, B, S)
    att_out = _attention(att_slab.reshape(T, 3 * ATT_WIDTH), B, S)
    att_out = att_out.reshape(B, ATT_CLASSES, S // ATT_CLASSES, ATT_WIDTH)
    wr, br = _router_weights(router_group_w[0], router_group_b[0], router_expert_w[0], router_expert_b[0])
    h, u2, logits = _outproj(o_f, o_b, gate, att_out, h, gla_norm_w[0], w_out[0], norm2_w[0], wr, br)
    info, counts = _route(logits)
    dest, pend = _plan(info, counts)
    xs = _dispatch(dest, pend, u2, _moe_capacity(T))
    ys = _experts(pend, xs, expert_w_gate[0], expert_w_up[0], expert_w_down[0])
    out = _combine(dest, ys, info, h, final_norm_w)
    return out.reshape(B, S, D)
```

```python
import functools

import jax
import jax.numpy as jnp
import numpy as np
from jax import lax
from jax.experimental import pallas as pl
from jax.experimental.pallas import tpu as pltpu

F32 = jnp.float32
BF16 = jnp.bfloat16

D_MODEL = 1024
GLA_HEADS = 4
GLA_DV = 128
GLA_DK = 64
GLA_KEY_WIDTH = GLA_HEADS * GLA_DK
GLA_VAL_WIDTH = GLA_HEADS * GLA_DV
GLA_GATE_RANK = 16
GLA_TAU = 16.0
GLA_CHUNK = 64
ATT_WIDTH = 512
ATT_HEAD_DIM = 64
ATT_HEADS = 8
ROT_DIM = 16
ROPE_THETA = 500000.0
DILATED_PATTERNS = ((128, 1), (512, 4), (2048, 16))
ATT_RADIUS = 64
MOE_GROUPS = 4
MOE_EXPERTS_PER_GROUP = 8
MOE_N_EXPERTS = 32
MOE_TOP_K = 2
MOE_D_FF = 512
EPS = 1e-6
NEG_INF = -1e30
LOG2E = 1.4426950408889634

LANES = 128
MOE_ROWS = 256


def _log2(n):
    assert n & (n - 1) == 0, n
    return n.bit_length() - 1


GLA_CHUNK_SHIFT = _log2(GLA_CHUNK)
GLA_DK_SHIFT = _log2(GLA_DK)
MOE_ROWS_SHIFT = _log2(MOE_ROWS)
MOE_GROUP_SHIFT = _log2(MOE_EXPERTS_PER_GROUP)
VMEM_LIMIT = 56 * 1024 * 1024


def _cparams(sem):
    return pltpu.CompilerParams(dimension_semantics=sem, vmem_limit_bytes=VMEM_LIMIT)


def _dot(a, b):
    return jnp.dot(a, b, preferred_element_type=F32)


def _dot_nt(a, b):
    return lax.dot_general(a, b, (((1,), (1,)), ((), ())), preferred_element_type=F32)


def _dot_tn(a, b):
    return lax.dot_general(a, b, (((0,), (0,)), ((), ())), preferred_element_type=F32)


def _rms(x, w):
    return x * lax.rsqrt(jnp.mean(x * x, axis=-1, keepdims=True) + EPS) * w


def _inproj_kernel(x_ref, n1_ref, wg_ref, wlr_ref, wa_ref, gw_ref, gb_ref,
                   rc_ref, rs1_ref, rs2_ref, gla_ref, gate_ref, loga_ref, att_ref, stage_ref, wgb, wlrb):
    @pl.when(pl.program_id(0) == 0)
    def _():
        wgb[...] = wg_ref[...].astype(BF16)
        wlrb[...] = wlr_ref[...].astype(BF16)

    x = x_ref[...]
    ub = _rms(x, n1_ref[...]).astype(BF16)
    g = _dot(ub, wgb[...])
    qkv = 2 * GLA_KEY_WIDTH + GLA_VAL_WIDTH
    gla_ref[:, :GLA_KEY_WIDTH] = g[:, :GLA_KEY_WIDTH] * (GLA_DK ** -0.5)
    gla_ref[:, GLA_KEY_WIDTH:] = g[:, GLA_KEY_WIDTH:qkv]
    gate_ref[...] = g[:, qkv:].astype(BF16)
    lr = _dot(ub, wlrb[...])
    gate = _dot(lr.astype(BF16), gw_ref[...]) + gb_ref[...]
    loga_ref[...] = (jnp.minimum(gate, 0.0) - jnp.log(1.0 + jnp.exp(-jnp.abs(gate)))) * (1.0 / GLA_TAU)
    a = _dot(ub, wa_ref[...])
    qk = a[:, :2 * ATT_WIDTH]
    reps = 2 * ATT_WIDTH // LANES
    c = jnp.concatenate([rc_ref[...]] * reps, axis=1)
    s1 = jnp.concatenate([rs1_ref[...]] * reps, axis=1)
    s2 = jnp.concatenate([rs2_ref[...]] * reps, axis=1)
    half = ROT_DIM // 2
    n = 2 * ATT_WIDTH
    roped = qk * c + pltpu.roll(qk, n - half, 1) * s1 + pltpu.roll(qk, half, 1) * s2
    qkv = jnp.concatenate([roped[:, :ATT_WIDTH] * (ATT_HEAD_DIM ** -0.5 * LOG2E), roped[:, ATT_WIDTH:],
                           a[:, 2 * ATT_WIDTH:]], axis=1)
    rows = x.shape[0] // ATT_CLASSES
    for j in range(3 * ATT_WIDTH // LANES):
        cols = slice(j * LANES, (j + 1) * LANES)
        stage_ref[j] = qkv[:, cols]
        for c in range(ATT_CLASSES):
            att_ref[c, :, cols] = stage_ref[j, pl.ds(c, rows, stride=ATT_CLASSES), :]


def _rope_lane_tables(S):
    half = ROT_DIM // 2
    inv = np.float32(ROPE_THETA) ** (-(np.arange(0, ROT_DIM, 2, dtype=np.float32) / np.float32(ROT_DIM)))
    ang = np.arange(S, dtype=np.float32)[:, None] * inv[None, :].astype(np.float32)
    cos, sin = np.cos(ang), np.sin(ang)
    ones = np.ones((S, ATT_HEAD_DIM - ROT_DIM), np.float32)
    zeros = np.zeros((S, ATT_HEAD_DIM - ROT_DIM), np.float32)
    zeros8 = np.zeros((S, half), np.float32)
    rep = LANES // ATT_HEAD_DIM
    c = np.tile(np.concatenate([cos, cos, ones], axis=1), (1, rep))
    s1 = np.tile(np.concatenate([-sin, zeros8, zeros], axis=1), (1, rep))
    s2 = np.tile(np.concatenate([zeros8, sin, zeros], axis=1), (1, rep))
    return jnp.asarray(c), jnp.asarray(s1), jnp.asarray(s2)


def _inproj(x2, S, norm1_w, w_in, wf, bfw, wb, bbw, tm=512):
    T = x2.shape[0]
    o_lr = 2 * GLA_KEY_WIDTH + 2 * GLA_VAL_WIDTH
    o_att = o_lr + 2 * GLA_GATE_RANK
    wa = w_in[:, o_att:].astype(BF16)
    zeros = jnp.zeros((GLA_GATE_RANK, GLA_KEY_WIDTH), F32)
    gw = jnp.concatenate([jnp.concatenate([wf, zeros], axis=1), jnp.concatenate([zeros, wb], axis=1),
                          jnp.zeros((LANES - 2 * GLA_GATE_RANK, 2 * GLA_KEY_WIDTH), F32)], axis=0).astype(BF16)
    gb = jnp.concatenate([bfw, bbw])[None, :]
    rc, rs1, rs2 = _rope_lane_tables(S)
    nS = S // tm
    row = lambda i: (i, 0)
    const = lambda i: (0, 0)
    pos = lambda i: (i % nS, 0)
    return pl.pallas_call(
        _inproj_kernel,
        grid=(T // tm,),
        in_specs=[
            pl.BlockSpec((tm, D_MODEL), row),
            pl.BlockSpec((1, D_MODEL), const),
            pl.BlockSpec((D_MODEL, o_lr), const),
            pl.BlockSpec((D_MODEL, LANES), lambda i: (0, o_lr // LANES)),
            pl.BlockSpec((D_MODEL, 3 * ATT_WIDTH), const),
            pl.BlockSpec((LANES, 2 * GLA_KEY_WIDTH), const),
            pl.BlockSpec((1, 2 * GLA_KEY_WIDTH), const),
            pl.BlockSpec((tm, LANES), pos),
            pl.BlockSpec((tm, LANES), pos),
            pl.BlockSpec((tm, LANES), pos),
        ],
        out_specs=[
            pl.BlockSpec((tm, o_lr - GLA_VAL_WIDTH), row),
            pl.BlockSpec((tm, GLA_VAL_WIDTH), row),
            pl.BlockSpec((tm, 2 * GLA_KEY_WIDTH), row),
            pl.BlockSpec((None, ATT_CLASSES, tm // ATT_CLASSES, 3 * ATT_WIDTH),
                         lambda i: (i // nS, 0, i % nS, 0)),
        ],
        out_shape=[
            jax.ShapeDtypeStruct((T, o_lr - GLA_VAL_WIDTH), F32),
            jax.ShapeDtypeStruct((T, GLA_VAL_WIDTH), BF16),
            jax.ShapeDtypeStruct((T, 2 * GLA_KEY_WIDTH), F32),
            jax.ShapeDtypeStruct((T // S, ATT_CLASSES, S // ATT_CLASSES, 3 * ATT_WIDTH), F32),
        ],
        scratch_shapes=[pltpu.VMEM((3 * ATT_WIDTH // LANES, tm, LANES), F32),
                        pltpu.VMEM((D_MODEL, o_lr), BF16), pltpu.VMEM((D_MODEL, LANES), BF16)],
        compiler_params=_cparams(("arbitrary",)),
        name="inproj",
    )(x2, norm1_w[None, :], w_in, w_in, wa, gw, gb, rc, rs1, rs2)


def _gla_decays(q, k, v, la, forward, G):
    C = GLA_CHUNK
    R = G * C
    r = lax.broadcasted_iota(jnp.int32, (R, R), 0)
    c = lax.broadcasted_iota(jnp.int32, (R, R), 1)
    same = (r >> GLA_CHUNK_SHIFT) == (c >> GLA_CHUNK_SHIFT)
    tri = (c <= r) if forward else (c >= r)
    t_mat = jnp.where(same, jnp.where(tri, 1.0, 0.0), 0.0).astype(BF16)
    hi = la.astype(BF16)
    lo = (la - hi.astype(F32)).astype(BF16)
    b = _dot(t_mat, hi) + _dot(t_mat, lo)
    edge = C - 1 if forward else 0
    tot = jnp.concatenate([jnp.broadcast_to(b[g * C + edge:g * C + edge + 1], (C, GLA_KEY_WIDTH))
                           for g in range(G)], axis=0)
    order = list(range(G)) if forward else list(range(G - 1, -1, -1))
    return dict(q_dec=q * jnp.exp(b), k_inv=(k * jnp.exp(-b)).astype(BF16), k_end=k * jnp.exp(tot - b),
                tot=tot, vb=v.astype(BF16), order=order, forward=forward, G=G)


def _gla_scores(prep):
    C, H = GLA_CHUNK, GLA_HEADS
    lane_k = lax.broadcasted_iota(jnp.int32, (C, GLA_KEY_WIDTH), 1)
    qd_heads, scores = {}, {}
    for g in prep["order"]:
        rows = slice(g * C, (g + 1) * C)
        qd = prep["q_dec"][rows]
        qd_heads[g] = jnp.concatenate([jnp.where((lane_k >> GLA_DK_SHIFT) == h, qd, 0.0) for h in range(H)],
                                      axis=0).astype(BF16)
        scores[g] = _dot_nt(qd_heads[g], prep["k_inv"][rows])
    return qd_heads, scores


def _gla_chunk_updates(prep):
    C, H, G = GLA_CHUNK, GLA_HEADS, prep["G"]
    k_end, tot, vb = prep["k_end"], prep["tot"], prep["vb"]
    kv, dec_t = {}, {}
    lane = lax.broadcasted_iota(jnp.int32, (GLA_KEY_WIDTH, 2 * C), 1)
    zeros = jnp.zeros((C, GLA_DV), BF16)
    for p in range(G // 2):
        pair = slice(2 * p * C, (2 * p + 2) * C)
        ke_t = k_end[pair].T.astype(BF16)
        tot_t = tot[pair].T
        swapped = pltpu.roll(tot_t, C, 1)
        for half in range(2):
            g = 2 * p + half
            rows = slice(g * C, (g + 1) * C)
            own = (lane < C) if half == 0 else (lane >= C)
            dec_t[g] = jnp.exp(jnp.where(own, tot_t, swapped))
            parts = []
            for h in range(H):
                v_h = vb[rows, h * GLA_DV:(h + 1) * GLA_DV]
                v_pad = jnp.concatenate([v_h, zeros] if half == 0 else [zeros, v_h], axis=0)
                parts.append(_dot(ke_t[h * C:(h + 1) * C], v_pad))
            kv[g] = jnp.concatenate(parts, axis=0)
    return kv, dec_t


def _gla_states(prep, kv, dec_t, s_ref):
    st = s_ref[...]
    states = {}
    for g in prep["order"]:
        states[g] = st.astype(BF16)
        st = st * dec_t[g] + kv[g]
    s_ref[...] = st
    return states


def _gla_outputs(prep, qd_heads, scores, inter, o_ref):
    C, H = GLA_CHUNK, GLA_HEADS
    row_q = lax.broadcasted_iota(jnp.int32, (H * C, C), 0) & (C - 1)
    col_k = lax.broadcasted_iota(jnp.int32, (H * C, C), 1)
    a_mask = (col_k <= row_q) if prep["forward"] else (col_k >= row_q)
    for g in prep["order"]:
        rows = slice(g * C, (g + 1) * C)
        a = jnp.where(a_mask, scores[g], 0.0).astype(BF16)
        vv = prep["vb"][rows]
        o_ref[rows, :] = jnp.concatenate(
            [_dot(a[h * C:(h + 1) * C], vv[:, h * GLA_DV:(h + 1) * GLA_DV]) + inter[g][h * C:(h + 1) * C]
             for h in range(H)], axis=1).astype(o_ref.dtype)


def _gla_kernel(qf_ref, kf_ref, vf_ref, laf_ref, qb_ref, kb_ref, vb_ref, lab_ref,
                of_ref, ob_ref, sf_ref, sb_ref, *, G):
    @pl.when(pl.program_id(1) == 0)
    def _():
        sf_ref[...] = jnp.zeros_like(sf_ref)
        sb_ref[...] = jnp.zeros_like(sb_ref)

    dirs = [(_gla_decays(qf_ref[...], kf_ref[...], vf_ref[...], laf_ref[...], True, G), sf_ref, of_ref),
            (_gla_decays(qb_ref[...], kb_ref[...], vb_ref[...], lab_ref[...], False, G), sb_ref, ob_ref)]
    scored = [_gla_scores(prep) for prep, _, _ in dirs]
    updates = [_gla_chunk_updates(prep) for prep, _, _ in dirs]
    states = [_gla_states(prep, kv, dec_t, s_ref) for (prep, s_ref, _), (kv, dec_t) in zip(dirs, updates)]
    inters = [{g: _dot(qd_heads[g], st[g]) for g in prep["order"]}
              for (prep, _, _), (qd_heads, _), st in zip(dirs, scored, states)]
    for (prep, _, o_ref), (qd_heads, scores), inter in zip(dirs, scored, inters):
        _gla_outputs(prep, qd_heads, scores, inter, o_ref)


def _gla(gla_slab, loga, B, S, G=8):
    T = B * S
    R = G * GLA_CHUNK
    ns = S // R
    fwd = lambda col: (lambda b, i: (b * ns + i, col))
    bwd = lambda col: (lambda b, i: (b * ns + ns - 1 - i, col))
    kw, vw = GLA_KEY_WIDTH, GLA_VAL_WIDTH
    return pl.pallas_call(
        functools.partial(_gla_kernel, G=G),
        grid=(B, ns),
        in_specs=[
            pl.BlockSpec((R, kw), fwd(0)), pl.BlockSpec((R, kw), fwd(1)),
            pl.BlockSpec((R, vw), fwd(1)), pl.BlockSpec((R, kw), fwd(0)),
            pl.BlockSpec((R, kw), bwd(0)), pl.BlockSpec((R, kw), bwd(1)),
            pl.BlockSpec((R, vw), bwd(1)), pl.BlockSpec((R, kw), bwd(1)),
        ],
        out_specs=[pl.BlockSpec((R, vw), fwd(0)), pl.BlockSpec((R, vw), bwd(0))],
        out_shape=[jax.ShapeDtypeStruct((T, vw), BF16), jax.ShapeDtypeStruct((T, vw), BF16)],
        scratch_shapes=[pltpu.VMEM((kw, GLA_DV), F32), pltpu.VMEM((kw, GLA_DV), F32)],
        compiler_params=_cparams(("arbitrary", "arbitrary")),
        name="gla",
    )(gla_slab, gla_slab, gla_slab, loga, gla_slab, gla_slab, gla_slab, loga)


ATT_CLASSES = 4
ATT_QB = 128
ATT_KB = ATT_QB + 2 * ATT_RADIUS


ATT_UNROLL = (32, 16, 16)


def _att_kernel(q_ref, k_ref, v_ref, o_ref, m_ref, l_ref, bias_ref, *, S):
    QB, KB, NC = ATT_QB, ATT_KB, ATT_CLASSES
    L4 = S // NC
    lane = lax.broadcasted_iota(jnp.int32, (QB, LANES), 1)
    head0 = lane < ATT_HEAD_DIM

    @pl.when((pl.program_id(0) == 0) & (pl.program_id(1) == 0))
    def _():
        rowi = lax.broadcasted_iota(jnp.int32, (2 * QB, KB), 0) & (QB - 1)
        coli = lax.broadcasted_iota(jnp.int32, (2 * QB, KB), 1)
        qpos = (rowi & (QB // NC - 1)) * NC + (rowi >> _log2(QB // NC))
        kpos = (coli & (KB // NC - 1)) * NC + (coli >> _log2(KB // NC))
        for case in range(3):
            bias_ref[0, case] = jnp.where(jnp.abs(rowi - coli + case * ATT_RADIUS) <= ATT_RADIUS, 0.0, NEG_INF)
            bias_ref[1, case] = jnp.where(jnp.abs(qpos - kpos + case * ATT_RADIUS) <= ATT_RADIUS, 0.0, NEG_INF)

    for pi, (_, d) in enumerate(DILATED_PATTERNS):
        L = S // d
        nb = L // QB
        shift = nb.bit_length() - 1
        first = pi == 0
        last = pi == len(DILATED_PATTERNS) - 1

        def scores(n, d=d, L=L, nb=nb, shift=shift):
            cls = n >> shift
            q0 = (n & (nb - 1)) * QB
            ws = jnp.clip(q0 - ATT_RADIUS, 0, L - KB)
            if d == 1:
                qsls = [pl.ds(pl.multiple_of(c * L4 + q0 // NC, QB // NC), QB // NC) for c in range(NC)]
                ksls = [pl.ds(pl.multiple_of(c * L4 + ws // NC, ATT_RADIUS // NC), KB // NC) for c in range(NC)]
            elif d == NC:
                qsls = [pl.ds(pl.multiple_of(cls * L4 + q0, QB), QB)]
                ksls = [pl.ds(pl.multiple_of(cls * L4 + ws, ATT_RADIUS), KB)]
            else:
                base = (cls & (NC - 1)) * L4 + (cls >> _log2(NC))
                qsls = [pl.ds(base + NC * q0, QB, stride=NC)]
                ksls = [pl.ds(base + NC * ws, KB, stride=NC)]
            q = jnp.concatenate([q_ref[sl, :] for sl in qsls], axis=0)
            kw = jnp.concatenate([k_ref[sl, :] for sl in ksls], axis=0)
            kb = kw.astype(BF16)
            bias = bias_ref[1 if d == 1 else 0, (q0 - ws) >> _log2(ATT_RADIUS), :QB]
            q_heads = (jnp.where(head0, q, 0.0), jnp.where(head0, 0.0, q))
            s = [_dot_nt(qh.astype(BF16), kb) + bias for qh in q_heads]
            return qsls, ksls, s

        def softmax_pv(qsls, ksls, s):
            vw = jnp.concatenate([v_ref[sl, :] for sl in ksls], axis=0)
            v_ones = jnp.concatenate([vw.astype(BF16), jnp.ones((KB, LANES), BF16)], axis=1)
            m_h = [jnp.max(t, axis=-1, keepdims=True) for t in s]
            pv = [_dot(jnp.exp2(t - m).astype(BF16), v_ones) for t, m in zip(s, m_h)]
            acc_b = jnp.where(head0, pv[0][:, :LANES], pv[1][:, :LANES])
            m_b = jnp.where(head0, m_h[0], m_h[1])
            l_b = jnp.where(head0, pv[0][:, LANES:], pv[1][:, LANES:])
            return qsls, acc_b, m_b, l_b

        def load(ref, sls):
            return jnp.concatenate([ref[sl, :] for sl in sls], axis=0)

        def store(ref, sls, val):
            n = val.shape[0] // len(sls)
            for i, sl in enumerate(sls):
                ref[sl, :] = val[i * n:(i + 1) * n]

        unroll = ATT_UNROLL[pi]

        def body(n, carry, first=first, last=last, unroll=unroll):
            staged = [scores(n * unroll + u) for u in range(unroll)]
            blocks = [softmax_pv(*st) for st in staged]
            for qsls, acc_b, m_b, l_b in blocks:
                if first:
                    acc, m_new, l_new = acc_b, m_b, l_b
                else:
                    m_old = load(m_ref, qsls)
                    m_new = jnp.maximum(m_old, m_b)
                    w_old = jnp.exp2(m_old - m_new)
                    w_blk = jnp.exp2(m_b - m_new)
                    acc = load(o_ref, qsls) * w_old + acc_b * w_blk
                    l_new = load(l_ref, qsls) * w_old + l_b * w_blk
                if last:
                    store(o_ref, qsls, acc / l_new)
                else:
                    store(o_ref, qsls, acc)
                    store(m_ref, qsls, m_new)
                    store(l_ref, qsls, l_new)
            return carry

        lax.fori_loop(0, S // (QB * unroll), body, 0)


def _attention(att_slab, B, S):
    T = B * S
    ncol = ATT_WIDTH // LANES
    return pl.pallas_call(
        functools.partial(_att_kernel, S=S),
        grid=(B, ncol),
        in_specs=[
            pl.BlockSpec((S, LANES), lambda b, h: (b, h)),
            pl.BlockSpec((S, LANES), lambda b, h: (b, ncol + h)),
            pl.BlockSpec((S, LANES), lambda b, h: (b, 2 * ncol + h)),
        ],
        out_specs=pl.BlockSpec((S, LANES), lambda b, h: (b, h)),
        out_shape=jax.ShapeDtypeStruct((T, ATT_WIDTH), F32),
        scratch_shapes=[pltpu.VMEM((S, LANES), F32), pltpu.VMEM((S, LANES), F32),
                        pltpu.VMEM((2, 3, 2 * ATT_QB, ATT_KB), F32)],
        compiler_params=_cparams(("arbitrary", "arbitrary")),
        name="dilated_attention",
    )(att_slab, att_slab, att_slab)


PACK_WORDS = D_MODEL // 2
ROW_TILE = PACK_WORDS // LANES
HIGH_HALF = -65536


def _pack_rows(x):
    bits = lambda v: lax.bitcast_convert_type(v.astype(BF16).astype(F32), jnp.int32)
    low = (bits(x[:, :PACK_WORDS]) >> 16) & 0xFFFF
    return (bits(x[:, PACK_WORDS:]) & HIGH_HALF) | low


def _unpack_rows(w):
    low = lax.bitcast_convert_type(w << 16, F32)
    high = lax.bitcast_convert_type(w & HIGH_HALF, F32)
    return jnp.concatenate([low, high], axis=1).astype(BF16)


def _to_row_tiles(ref, w):
    n = w.shape[0]
    for j in range(ROW_TILE):
        ref[pl.ds(j, n, stride=ROW_TILE), :] = w[:, j * LANES:(j + 1) * LANES]


def _from_row_tiles(ref, n):
    return jnp.concatenate([ref[pl.ds(j, n, stride=ROW_TILE), :] for j in range(ROW_TILE)], axis=1)


def _tile_copy(src_ref, src_row, dst_ref, dst_row, sem):
    src = pl.ds(pl.multiple_of(src_row * ROW_TILE, ROW_TILE), ROW_TILE)
    dst = pl.ds(pl.multiple_of(dst_row * ROW_TILE, ROW_TILE), ROW_TILE)
    return pltpu.make_async_copy(src_ref.at[src], dst_ref.at[dst], sem)


def _outproj_kernel(of_ref, ob_ref, gg_ref, att_ref, x_ref, gnw_ref, wo1_ref, wo2_ref,
                    n2_ref, wr_ref, br_ref, h_ref, u_ref, lg_ref, stage_ref):
    rows = stage_ref.shape[1] // ATT_CLASSES
    for j in range(ATT_WIDTH // LANES):
        for c in range(ATT_CLASSES):
            stage_ref[j, pl.ds(c, rows, stride=ATT_CLASSES), :] = att_ref[c, :, j * LANES:(j + 1) * LANES]
    att = jnp.concatenate([stage_ref[j] for j in range(ATT_WIDTH // LANES)], axis=1)
    o = of_ref[...].astype(F32) + ob_ref[...].astype(F32)
    gate = gg_ref[...].astype(F32)
    gnw = gnw_ref[...]
    parts = []
    for h in range(GLA_HEADS):
        sl = slice(h * GLA_DV, (h + 1) * GLA_DV)
        parts.append(_rms(o[:, sl], gnw))
    y = jnp.concatenate(parts, axis=1) * (gate / (1.0 + jnp.exp(-gate)))
    mix = _dot(y.astype(BF16), wo1_ref[...]) + _dot(att.astype(BF16), wo2_ref[...])
    h = x_ref[...] + mix
    h_ref[...] = h
    u = _rms(h, n2_ref[...])
    _to_row_tiles(u_ref, _pack_rows(u))
    u_hi = u.astype(BF16)
    u_lo = (u - u_hi.astype(F32)).astype(BF16)
    hi_both = _dot_nt(wr_ref[...], u_hi)
    lg_ref[...] = (hi_both[:LANES] + hi_both[LANES:] + _dot_nt(wr_ref[:LANES], u_lo)) + br_ref[...]


def _outproj(o_f, o_b, gate, att_out, x2, gla_norm_w, w_out, norm2_w, wr, br, tm=512):
    T = x2.shape[0]
    nS = att_out.shape[2] * ATT_CLASSES // tm
    row = lambda i: (i, 0)
    const = lambda i: (0, 0)
    wo = w_out.astype(BF16)
    wr_hi = wr.astype(BF16)
    wr_lo = (wr - wr_hi.astype(F32)).astype(BF16)
    wr = jnp.concatenate([wr_hi, wr_lo], axis=0)
    return pl.pallas_call(
        _outproj_kernel,
        grid=(T // tm,),
        in_specs=[
            pl.BlockSpec((tm, GLA_VAL_WIDTH), row),
            pl.BlockSpec((tm, GLA_VAL_WIDTH), row),
            pl.BlockSpec((tm, GLA_VAL_WIDTH), row),
            pl.BlockSpec((None, ATT_CLASSES, tm // ATT_CLASSES, ATT_WIDTH), lambda i: (i // nS, 0, i % nS, 0)),
            pl.BlockSpec((tm, D_MODEL), row),
            pl.BlockSpec((1, GLA_DV), const),
            pl.BlockSpec((GLA_VAL_WIDTH, D_MODEL), const),
            pl.BlockSpec((ATT_WIDTH, D_MODEL), lambda i: (GLA_VAL_WIDTH // ATT_WIDTH, 0)),
            pl.BlockSpec((1, D_MODEL), const),
            pl.BlockSpec((2 * LANES, D_MODEL), const),
            pl.BlockSpec((LANES, 1), const),
        ],
        out_specs=[
            pl.BlockSpec((tm, D_MODEL), row),
            pl.BlockSpec((tm * ROW_TILE, LANES), row),
            pl.BlockSpec((LANES, tm), lambda i: (0, i)),
        ],
        out_shape=[
            jax.ShapeDtypeStruct((T, D_MODEL), F32),
            jax.ShapeDtypeStruct((T * ROW_TILE, LANES), jnp.int32),
            jax.ShapeDtypeStruct((LANES, T), F32),
        ],
        scratch_shapes=[pltpu.VMEM((ATT_WIDTH // LANES, tm, LANES), F32)],
        compiler_params=_cparams(("arbitrary",)),
        name="outproj",
    )(o_f, o_b, gate, att_out, x2, gla_norm_w[None, :], wo, wo,
      norm2_w[None, :], wr, br)


INFO_E1, INFO_E2, INFO_R1, INFO_R2, INFO_W1, INFO_W2 = range(6)
ROUTE_ROWS = 40


def _route_kernel(lg_ref, info_ref, cnt_ref, carry_ref):
    @pl.when(pl.program_id(0) == 0)
    def _():
        carry_ref[...] = jnp.zeros_like(carry_ref)

    lg = lg_ref[:ROUTE_ROWS, :]
    tr = lg.shape[1]
    row = lax.broadcasted_iota(jnp.int32, (ROUTE_ROWS, tr), 0)
    big = jnp.int32(1 << 20)
    is_g = (row >= MOE_N_EXPERTS) & (row < MOE_N_EXPERTS + MOE_GROUPS)
    gl = jnp.where(is_g, lg, -jnp.inf)
    gmax = jnp.max(gl, axis=0, keepdims=True)
    gsel = jnp.min(jnp.where(gl == gmax, row - MOE_N_EXPERTS, big), axis=0, keepdims=True)
    g_w = 1.0 / jnp.sum(jnp.where(is_g, jnp.exp(lg - gmax), 0.0), axis=0, keepdims=True)
    in_grp = (row < MOE_N_EXPERTS) & ((row >> MOE_GROUP_SHIFT) == gsel)
    el = jnp.where(in_grp, lg, -jnp.inf)
    v1 = jnp.max(el, axis=0, keepdims=True)
    i1 = jnp.min(jnp.where(el == v1, row, big), axis=0, keepdims=True)
    el2 = jnp.where(row == i1, -jnp.inf, el)
    v2 = jnp.max(el2, axis=0, keepdims=True)
    i2 = jnp.min(jnp.where(el2 == v2, row, big), axis=0, keepdims=True)
    t = jnp.exp(v2 - v1)
    w1 = g_w * (1.0 / (1.0 + t))
    w2 = g_w * (t / (1.0 + t))

    erow = lax.broadcasted_iota(jnp.int32, (MOE_N_EXPERTS, tr), 0)
    hit1 = erow == i1
    hit2 = erow == i2
    member = jnp.where(hit1 | hit2, 1.0, 0.0)
    r = lax.broadcasted_iota(jnp.int32, (tr, tr), 0)
    c = lax.broadcasted_iota(jnp.int32, (tr, tr), 1)
    earlier = jnp.where(r < c, 1.0, 0.0).astype(BF16)
    carry = carry_ref[...]
    prefix = _dot(member.astype(BF16), earlier) + carry[:, 0:1]
    rank1 = jnp.sum(jnp.where(hit1, prefix, 0.0), axis=0, keepdims=True)
    rank2 = jnp.sum(jnp.where(hit2, prefix, 0.0), axis=0, keepdims=True)
    carry = carry + jnp.sum(member, axis=1, keepdims=True)
    carry_ref[...] = carry
    cnt_ref[...] = carry

    zero = jnp.zeros_like(w1)
    info_ref[...] = jnp.concatenate([i1.astype(F32), i2.astype(F32), rank1, rank2, w1, w2, zero, zero], axis=0)


def _route(logits_t, tr=1024):
    T = logits_t.shape[1]
    return pl.pallas_call(
        _route_kernel,
        grid=(T // tr,),
        in_specs=[pl.BlockSpec((LANES, tr), lambda i: (0, i))],
        out_specs=[pl.BlockSpec((8, tr), lambda i: (0, i)),
                   pl.BlockSpec((MOE_N_EXPERTS, LANES), lambda i: (0, 0))],
        out_shape=[jax.ShapeDtypeStruct((8, T), F32), jax.ShapeDtypeStruct((MOE_N_EXPERTS, LANES), F32)],
        scratch_shapes=[pltpu.VMEM((MOE_N_EXPERTS, LANES), F32)],
        compiler_params=_cparams(("arbitrary",)),
        name="route",
    )(logits_t)


ROW_UNROLL = 32


def _dispatch_kernel(dest_ref, pend_ref, u_ref, xs_ref, zbuf, sem, zsem, *, td, T, nblk):
    @pl.when(pl.program_id(0) == 0)
    def _():
        zbuf[...] = jnp.zeros_like(zbuf)
        n_used = pend_ref[MOE_N_EXPERTS - 1] >> MOE_ROWS_SHIFT

        def zero_copy(blk):
            start = pl.multiple_of(blk * (MOE_ROWS * ROW_TILE), MOE_ROWS * ROW_TILE)
            return pltpu.make_async_copy(zbuf, xs_ref.at[pl.ds(start, MOE_ROWS * ROW_TILE)], zsem)

        def each_pad_block(fn):
            def per_expert(e, carry):
                prev = jnp.where(e > 0, pend_ref[jnp.maximum(e - 1, 0)], 0)

                @pl.when(pend_ref[e] > prev)
                def _():
                    fn((pend_ref[e] >> MOE_ROWS_SHIFT) - 1)
                return carry

            def per_tail(j, carry):
                @pl.when(n_used + j < nblk)
                def _():
                    fn(n_used + j)
                return carry

            lax.fori_loop(0, MOE_N_EXPERTS, per_expert, 0)
            lax.fori_loop(0, MOE_N_EXPERTS, per_tail, 0)

        each_pad_block(lambda blk: zero_copy(blk).start())
        each_pad_block(lambda blk: zero_copy(blk).wait())

    base = pl.program_id(0) * td

    def issue(g, carry):
        for j in range(ROW_UNROLL):
            r = g * ROW_UNROLL + j
            for k in range(MOE_TOP_K):
                _tile_copy(u_ref, r, xs_ref, dest_ref[k * T + base + r], sem).start(priority=k)
        return carry

    lax.fori_loop(0, td // ROW_UNROLL, issue, 0)
    for k in range(MOE_TOP_K):
        pltpu.make_async_copy(u_ref, xs_ref.at[pl.ds(0, td * ROW_TILE)], sem).wait()


def _dispatch(dest, pend, u2, cap, td=2048):
    T = u2.shape[0] // ROW_TILE
    return pl.pallas_call(
        functools.partial(_dispatch_kernel, td=td, T=T, nblk=cap // MOE_ROWS),
        grid_spec=pltpu.PrefetchScalarGridSpec(
            num_scalar_prefetch=2,
            grid=(T // td,),
            in_specs=[pl.BlockSpec((td * ROW_TILE, LANES), lambda i, d, z: (i, 0))],
            out_specs=pl.BlockSpec(memory_space=pl.ANY),
            scratch_shapes=[pltpu.VMEM((MOE_ROWS * ROW_TILE, LANES), jnp.int32),
                            pltpu.SemaphoreType.DMA(()), pltpu.SemaphoreType.DMA(())],
        ),
        out_shape=jax.ShapeDtypeStruct((cap * ROW_TILE, LANES), jnp.int32),
        compiler_params=_cparams(("arbitrary",)),
        name="dispatch",
    )(dest, pend, u2)


EXPERT_GROUP = 4


def _expert_kernel(pend_ref, xs_hbm, wg_hbm, wu_hbm, wd_hbm, ys_hbm,
                   xbuf, ybuf, zbuf, stage_g, stage_u, stage_d, wgb, wub, wdb, xsem, ysem, wsem, zsem, *, nblk):
    last = MOE_N_EXPERTS - 1
    n_used = pend_ref[last] >> MOE_ROWS_SHIFT
    n_pairs = (n_used + EXPERT_GROUP - 1) >> _log2(EXPERT_GROUP)
    block_rows = MOE_ROWS * ROW_TILE

    def rows_of(b):
        return pl.ds(pl.multiple_of(b * block_rows, block_rows), block_rows)

    def x_copy(b, slot):
        return pltpu.make_async_copy(xs_hbm.at[rows_of(b)], xbuf.at[slot], xsem.at[slot])

    def y_copy(b, slot):
        return pltpu.make_async_copy(ybuf.at[slot], ys_hbm.at[rows_of(b)], ysem.at[slot])

    def zero_copy(b):
        return pltpu.make_async_copy(zbuf, ys_hbm.at[rows_of(b)], zsem)

    def weight_copies(e):
        return (pltpu.make_async_copy(wg_hbm.at[e], stage_g, wsem.at[0]),
                pltpu.make_async_copy(wu_hbm.at[e], stage_u, wsem.at[1]),
                pltpu.make_async_copy(wd_hbm.at[e], stage_d, wsem.at[2]))

    def owner(start, row):
        return lax.while_loop(lambda e: (e < last) & (pend_ref[e] <= row), lambda e: e + 1, start)

    for c in weight_copies(owner(0, 0)):
        c.start()
    for i in range(EXPERT_GROUP):
        x_copy(i, i).start()

    zbuf[...] = jnp.zeros_like(zbuf)

    def tail(fn):
        def step(b, carry):
            fn(b)
            return carry
        lax.fori_loop(n_pairs * EXPERT_GROUP, nblk, step, 0)

    tail(lambda b: zero_copy(b).start())

    def body(p, carry):
        cur, run = carry
        half = (p & 1) * EXPERT_GROUP
        for i in range(EXPERT_GROUP):
            x_copy(p * EXPERT_GROUP + i, half + i).wait()

        @pl.when(p + 1 < n_pairs)
        def _():
            for i in range(EXPERT_GROUP):
                x_copy((p + 1) * EXPERT_GROUP + i, EXPERT_GROUP - half + i).start()

        slots = []
        for i in range(EXPERT_GROUP):
            b = p * EXPERT_GROUP + i
            e = jnp.where(b < n_used, owner(jnp.maximum(cur, 0), b * MOE_ROWS), cur)
            fresh = e != cur
            run = run + fresh.astype(jnp.int32)
            slot = run & (EXPERT_GROUP - 1)

            @pl.when(fresh)
            def _(e=e, slot=slot):
                for c in weight_copies(e):
                    c.wait()
                wgb[slot] = stage_g[...].astype(BF16)
                wub[slot] = stage_u[...].astype(BF16)
                wdb[slot] = stage_d[...].astype(BF16)

                @pl.when(pend_ref[e] < pend_ref[last])
                def _():
                    for c in weight_copies(owner(e + 1, pend_ref[e])):
                        c.start(priority=1)

            cur = e
            slots.append(slot)

        @pl.when(p >= 2)
        def _():
            for i in range(EXPERT_GROUP):
                y_copy((p - 2) * EXPERT_GROUP + i, half + i).wait()

        for i in range(EXPERT_GROUP):
            xb = _unpack_rows(_from_row_tiles(xbuf.at[half + i], MOE_ROWS))
            g = _dot(xb, wgb[slots[i]])
            u = _dot(xb, wub[slots[i]])
            hid = (g / (1.0 + jnp.exp(-g))) * u
            _to_row_tiles(ybuf.at[half + i], _pack_rows(_dot(hid.astype(BF16), wdb[slots[i]])))
        for i in range(EXPERT_GROUP):
            y_copy(p * EXPERT_GROUP + i, half + i).start()
        return cur, run

    lax.fori_loop(0, n_pairs, body, (jnp.int32(-1), jnp.int32(-1)))

    def drain(p):
        for i in range(EXPERT_GROUP):
            y_copy(p * EXPERT_GROUP + i, (p & 1) * EXPERT_GROUP + i).wait()

    @pl.when(n_pairs >= 2)
    def _():
        drain(n_pairs - 2)
    drain(n_pairs - 1)
    tail(lambda b: zero_copy(b).wait())


def _experts(pend, xs, w_gate, w_up, w_down):
    cap = xs.shape[0] // ROW_TILE
    nblk = cap // MOE_ROWS
    assert nblk % EXPERT_GROUP == 0
    block = (MOE_ROWS * ROW_TILE, LANES)
    anywhere = pl.BlockSpec(memory_space=pl.ANY)
    return pl.pallas_call(
        functools.partial(_expert_kernel, nblk=nblk),
        grid_spec=pltpu.PrefetchScalarGridSpec(
            num_scalar_prefetch=1,
            grid=(1,),
            in_specs=[anywhere, anywhere, anywhere, anywhere],
            out_specs=anywhere,
            scratch_shapes=[pltpu.VMEM((2 * EXPERT_GROUP,) + block, jnp.int32),
                            pltpu.VMEM((2 * EXPERT_GROUP,) + block, jnp.int32),
                            pltpu.VMEM(block, jnp.int32),
                            pltpu.VMEM((D_MODEL, MOE_D_FF), F32),
                            pltpu.VMEM((D_MODEL, MOE_D_FF), F32),
                            pltpu.VMEM((MOE_D_FF, D_MODEL), F32),
                            pltpu.VMEM((EXPERT_GROUP, D_MODEL, MOE_D_FF), BF16),
                            pltpu.VMEM((EXPERT_GROUP, D_MODEL, MOE_D_FF), BF16),
                            pltpu.VMEM((EXPERT_GROUP, MOE_D_FF, D_MODEL), BF16),
                            pltpu.SemaphoreType.DMA((2 * EXPERT_GROUP,)),
                            pltpu.SemaphoreType.DMA((2 * EXPERT_GROUP,)),
                            pltpu.SemaphoreType.DMA((3,)),
                            pltpu.SemaphoreType.DMA(())],
        ),
        out_shape=jax.ShapeDtypeStruct((cap * ROW_TILE, LANES), jnp.int32),
        compiler_params=_cparams(("arbitrary",)),
        name="experts",
    )(pend, xs, w_gate, w_up, w_down)


def _combine_kernel(dest_ref, ys_ref, info_ref, h_ref, fw_ref, o_ref, buf, sem, *, tc, T):
    i = pl.program_id(0)
    n = pl.num_programs(0)

    def issue(step, slot):
        base = step * tc

        def body(g, carry):
            for j in range(ROW_UNROLL):
                r = g * ROW_UNROLL + j
                for k in range(MOE_TOP_K):
                    _tile_copy(ys_ref, dest_ref[k * T + base + r], buf.at[slot, k], r,
                               sem.at[slot]).start(priority=k)
            return carry

        lax.fori_loop(0, tc // ROW_UNROLL, body, 0)

    @pl.when(i == 0)
    def _():
        issue(0, 0)

    slot = i % 2

    @pl.when(i + 1 < n)
    def _():
        issue(i + 1, 1 - slot)

    for k in range(MOE_TOP_K):
        pltpu.make_async_copy(ys_ref.at[pl.ds(0, tc * ROW_TILE)], buf.at[slot, k], sem.at[slot]).wait()

    info_t = jnp.concatenate([info_ref[...]] * (LANES // 8), axis=0).T
    w1 = info_t[:, INFO_W1:INFO_W1 + 1]
    w2 = info_t[:, INFO_W2:INFO_W2 + 1]
    y1 = _unpack_rows(_from_row_tiles(buf.at[slot, 0], tc)).astype(F32)
    y2 = _unpack_rows(_from_row_tiles(buf.at[slot, 1], tc)).astype(F32)
    h = h_ref[...] + (y1 * w1 + y2 * w2)
    o_ref[...] = _rms(h, fw_ref[...])


def _combine(dest, ys, info, h, final_w, tc=512):
    T = h.shape[0]
    return pl.pallas_call(
        functools.partial(_combine_kernel, tc=tc, T=T),
        grid_spec=pltpu.PrefetchScalarGridSpec(
            num_scalar_prefetch=1,
            grid=(T // tc,),
            in_specs=[pl.BlockSpec(memory_space=pl.ANY),
                      pl.BlockSpec((8, tc), lambda i, d: (0, i)),
                      pl.BlockSpec((tc, D_MODEL), lambda i, d: (i, 0)),
                      pl.BlockSpec((1, D_MODEL), lambda i, d: (0, 0))],
            out_specs=pl.BlockSpec((tc, D_MODEL), lambda i, d: (i, 0)),
            scratch_shapes=[pltpu.VMEM((2, MOE_TOP_K, tc * ROW_TILE, LANES), jnp.int32),
                            pltpu.SemaphoreType.DMA((2,))],
        ),
        out_shape=jax.ShapeDtypeStruct((T, D_MODEL), F32),
        compiler_params=_cparams(("arbitrary",)),
        name="combine",
    )(dest, ys, info, h, final_w[None, :])


def _plan_kernel(info_ref, cnt_ref, dest_ref, pend_ref):
    cnt = cnt_ref[...].astype(jnp.int32)
    nblk_e = ((cnt + (MOE_ROWS - 1)) >> MOE_ROWS_SHIFT).astype(F32)
    r = lax.broadcasted_iota(jnp.int32, (MOE_N_EXPERTS, MOE_N_EXPERTS), 0)
    c = lax.broadcasted_iota(jnp.int32, (MOE_N_EXPERTS, MOE_N_EXPERTS), 1)
    before = jnp.where(c < r, 1.0, 0.0).astype(BF16)
    first_blk = _dot(before, nblk_e.astype(BF16))
    pstart = first_blk[:, 0:1] * float(MOE_ROWS)
    pend_ref[...] = ((first_blk + nblk_e) * float(MOE_ROWS)).astype(jnp.int32)

    info = info_ref[...]
    erow = lax.broadcasted_iota(jnp.int32, (MOE_N_EXPERTS, info.shape[1]), 0)
    start_of = lambda e: jnp.sum(jnp.where(erow == e.astype(jnp.int32), pstart, 0.0), axis=0, keepdims=True)
    d1 = info[INFO_R1:INFO_R1 + 1] + start_of(info[INFO_E1:INFO_E1 + 1])
    d2 = info[INFO_R2:INFO_R2 + 1] + start_of(info[INFO_E2:INFO_E2 + 1])
    zero = jnp.zeros_like(d1)
    dest_ref[...] = jnp.concatenate([d1, d2] + [zero] * 6, axis=0).astype(jnp.int32)


def _plan(info, counts, tr=2048):
    T = info.shape[1]
    dest8, pend = pl.pallas_call(
        _plan_kernel,
        grid=(T // tr,),
        in_specs=[pl.BlockSpec((8, tr), lambda i: (0, i)),
                  pl.BlockSpec((MOE_N_EXPERTS, LANES), lambda i: (0, 0))],
        out_specs=[pl.BlockSpec((8, tr), lambda i: (0, i)),
                   pl.BlockSpec((MOE_N_EXPERTS, LANES), lambda i: (0, 0))],
        out_shape=[jax.ShapeDtypeStruct((8, T), jnp.int32),
                   jax.ShapeDtypeStruct((MOE_N_EXPERTS, LANES), jnp.int32)],
        compiler_params=_cparams(("arbitrary",)),
        name="plan",
    )(info, counts)
    return dest8[:MOE_TOP_K].reshape(-1), pend[:, 0]


def _moe_capacity(T):
    return (-(-(T * MOE_TOP_K) // MOE_ROWS) + MOE_N_EXPERTS) * MOE_ROWS


def _router_weights(router_group_w, router_group_b, router_expert_w, router_expert_b):
    we = jnp.transpose(router_expert_w, (0, 2, 1)).reshape(MOE_N_EXPERTS, D_MODEL)
    pad = LANES - MOE_N_EXPERTS - MOE_GROUPS
    wr = jnp.concatenate([we, router_group_w.T, jnp.zeros((pad, D_MODEL), F32)], axis=0)
    br = jnp.concatenate([router_expert_b.reshape(-1), router_group_b, jnp.zeros((pad,), F32)])[:, None]
    return wr, br


def kernel(x, norm1_w, w_in, gla_fwd_gate_w, gla_fwd_gate_b, gla_bwd_gate_w, gla_bwd_gate_b,
           gla_norm_w, w_out, norm2_w, router_group_w, router_group_b, router_expert_w,
           router_expert_b, expert_w_gate, expert_w_up, expert_w_down, final_norm_w):
    B, S, D = x.shape
    T = B * S
    assert norm1_w.shape[0] == 1, "single-layer trunk: the final norm is fused into the combine step"
    h = x.reshape(T, D)
    gla_slab, gate, loga, att_slab = _inproj(h, S, norm1_w[0], w_in[0], gla_fwd_gate_w[0], gla_fwd_gate_b[0],
                                       gla_bwd_gate_w[0], gla_bwd_gate_b[0])
    o_f, o_b = _gla(gla_slab, loga, B, S)
    att_out = _attention(att_slab.reshape(T, 3 * ATT_WIDTH), B, S)
    att_out = att_out.reshape(B, ATT_CLASSES, S // ATT_CLASSES, ATT_WIDTH)
    wr, br = _router_weights(router_group_w[0], router_group_b[0], router_expert_w[0], router_expert_b[0])
    h, u2, logits = _outproj(o_f, o_b, gate, att_out, h, gla_norm_w[0], w_out[0], norm2_w[0], wr, br)
    info, counts = _route(logits)
    dest, pend = _plan(info, counts)
    xs = _dispatch(dest, pend, u2, _moe_capacity(T))
    ys = _experts(pend, xs, expert_w_gate[0], expert_w_up[0], expert_w_down[0])
    out = _combine(dest, ys, info, h, final_norm_w)
    return out.reshape(B, S, D)
```

```python
import functools

import jax
import jax.numpy as jnp
import numpy as np
from jax import lax
from jax.experimental import pallas as pl
from jax.experimental.pallas import tpu as pltpu

F32 = jnp.float32
BF16 = jnp.bfloat16

D_MODEL = 1024
GLA_HEADS = 4
GLA_DV = 128
GLA_DK = 64
GLA_KEY_WIDTH = GLA_HEADS * GLA_DK
GLA_VAL_WIDTH = GLA_HEADS * GLA_DV
GLA_GATE_RANK = 16
GLA_TAU = 16.0
GLA_CHUNK = 64
ATT_WIDTH = 512
ATT_HEAD_DIM = 64
ATT_HEADS = 8
ROT_DIM = 16
ROPE_THETA = 500000.0
DILATED_PATTERNS = ((128, 1), (512, 4), (2048, 16))
ATT_RADIUS = 64
MOE_GROUPS = 4
MOE_EXPERTS_PER_GROUP = 8
MOE_N_EXPERTS = 32
MOE_TOP_K = 2
MOE_D_FF = 512
EPS = 1e-6
NEG_INF = -1e30
LOG2E = 1.4426950408889634

LANES = 128
MOE_ROWS = 256


def _log2(n):
    assert n & (n - 1) == 0, n
    return n.bit_length() - 1


GLA_CHUNK_SHIFT = _log2(GLA_CHUNK)
GLA_DK_SHIFT = _log2(GLA_DK)
MOE_ROWS_SHIFT = _log2(MOE_ROWS)
MOE_GROUP_SHIFT = _log2(MOE_EXPERTS_PER_GROUP)
VMEM_LIMIT = 56 * 1024 * 1024


def _cparams(sem):
    return pltpu.CompilerParams(dimension_semantics=sem, vmem_limit_bytes=VMEM_LIMIT)


def _dot(a, b):
    return jnp.dot(a, b, preferred_element_type=F32)


def _dot_nt(a, b):
    return lax.dot_general(a, b, (((1,), (1,)), ((), ())), preferred_element_type=F32)


def _dot_tn(a, b):
    return lax.dot_general(a, b, (((0,), (0,)), ((), ())), preferred_element_type=F32)


def _rms(x, w):
    return x * lax.rsqrt(jnp.mean(x * x, axis=-1, keepdims=True) + EPS) * w


def _inproj_kernel(x_ref, n1_ref, wg_ref, wlr_ref, wa_ref, gw_ref, gb_ref,
                   rc_ref, rs1_ref, rs2_ref, gla_ref, gate_ref, loga_ref, att_ref, stage_ref, wgb, wlrb):
    @pl.when(pl.program_id(0) == 0)
    def _():
        wgb[...] = wg_ref[...].astype(BF16)
        wlrb[...] = wlr_ref[...].astype(BF16)

    x = x_ref[...]
    ub = _rms(x, n1_ref[...]).astype(BF16)
    g = _dot(ub, wgb[...])
    qkv = 2 * GLA_KEY_WIDTH + GLA_VAL_WIDTH
    gla_ref[:, :GLA_KEY_WIDTH] = g[:, :GLA_KEY_WIDTH] * (GLA_DK ** -0.5)
    gla_ref[:, GLA_KEY_WIDTH:] = g[:, GLA_KEY_WIDTH:qkv]
    gate_ref[...] = g[:, qkv:].astype(BF16)
    lr = _dot(ub, wlrb[...])
    gate = _dot(lr.astype(BF16), gw_ref[...]) + gb_ref[...]
    loga_ref[...] = (jnp.minimum(gate, 0.0) - jnp.log(1.0 + jnp.exp(-jnp.abs(gate)))) * (1.0 / GLA_TAU)
    a = _dot(ub, wa_ref[...])
    qk = a[:, :2 * ATT_WIDTH]
    reps = 2 * ATT_WIDTH // LANES
    c = jnp.concatenate([rc_ref[...]] * reps, axis=1)
    s1 = jnp.concatenate([rs1_ref[...]] * reps, axis=1)
    s2 = jnp.concatenate([rs2_ref[...]] * reps, axis=1)
    half = ROT_DIM // 2
    n = 2 * ATT_WIDTH
    roped = qk * c + pltpu.roll(qk, n - half, 1) * s1 + pltpu.roll(qk, half, 1) * s2
    qkv = jnp.concatenate([roped[:, :ATT_WIDTH] * (ATT_HEAD_DIM ** -0.5 * LOG2E), roped[:, ATT_WIDTH:],
                           a[:, 2 * ATT_WIDTH:]], axis=1)
    rows = x.shape[0] // ATT_CLASSES
    for j in range(3 * ATT_WIDTH // LANES):
        cols = slice(j * LANES, (j + 1) * LANES)
        stage_ref[j] = qkv[:, cols]
        for c in range(ATT_CLASSES):
            att_ref[c, :, cols] = stage_ref[j, pl.ds(c, rows, stride=ATT_CLASSES), :]


def _rope_lane_tables(S):
    half = ROT_DIM // 2
    inv = np.float32(ROPE_THETA) ** (-(np.arange(0, ROT_DIM, 2, dtype=np.float32) / np.float32(ROT_DIM)))
    ang = np.arange(S, dtype=np.float32)[:, None] * inv[None, :].astype(np.float32)
    cos, sin = np.cos(ang), np.sin(ang)
    ones = np.ones((S, ATT_HEAD_DIM - ROT_DIM), np.float32)
    zeros = np.zeros((S, ATT_HEAD_DIM - ROT_DIM), np.float32)
    zeros8 = np.zeros((S, half), np.float32)
    rep = LANES // ATT_HEAD_DIM
    c = np.tile(np.concatenate([cos, cos, ones], axis=1), (1, rep))
    s1 = np.tile(np.concatenate([-sin, zeros8, zeros], axis=1), (1, rep))
    s2 = np.tile(np.concatenate([zeros8, sin, zeros], axis=1), (1, rep))
    return jnp.asarray(c), jnp.asarray(s1), jnp.asarray(s2)


def _inproj(x2, S, norm1_w, w_in, wf, bfw, wb, bbw, tm=512):
    T = x2.shape[0]
    o_lr = 2 * GLA_KEY_WIDTH + 2 * GLA_VAL_WIDTH
    o_att = o_lr + 2 * GLA_GATE_RANK
    wa = w_in[:, o_att:].astype(BF16)
    zeros = jnp.zeros((GLA_GATE_RANK, GLA_KEY_WIDTH), F32)
    gw = jnp.concatenate([jnp.concatenate([wf, zeros], axis=1), jnp.concatenate([zeros, wb], axis=1),
                          jnp.zeros((LANES - 2 * GLA_GATE_RANK, 2 * GLA_KEY_WIDTH), F32)], axis=0).astype(BF16)
    gb = jnp.concatenate([bfw, bbw])[None, :]
    rc, rs1, rs2 = _rope_lane_tables(S)
    nS = S // tm
    row = lambda i: (i, 0)
    const = lambda i: (0, 0)
    pos = lambda i: (i % nS, 0)
    return pl.pallas_call(
        _inproj_kernel,
        grid=(T // tm,),
        in_specs=[
            pl.BlockSpec((tm, D_MODEL), row),
            pl.BlockSpec((1, D_MODEL), const),
            pl.BlockSpec((D_MODEL, o_lr), const),
            pl.BlockSpec((D_MODEL, LANES), lambda i: (0, o_lr // LANES)),
            pl.BlockSpec((D_MODEL, 3 * ATT_WIDTH), const),
            pl.BlockSpec((LANES, 2 * GLA_KEY_WIDTH), const),
            pl.BlockSpec((1, 2 * GLA_KEY_WIDTH), const),
            pl.BlockSpec((tm, LANES), pos),
            pl.BlockSpec((tm, LANES), pos),
            pl.BlockSpec((tm, LANES), pos),
        ],
        out_specs=[
            pl.BlockSpec((tm, o_lr - GLA_VAL_WIDTH), row),
            pl.BlockSpec((tm, GLA_VAL_WIDTH), row),
            pl.BlockSpec((tm, 2 * GLA_KEY_WIDTH), row),
            pl.BlockSpec((None, ATT_CLASSES, tm // ATT_CLASSES, 3 * ATT_WIDTH),
                         lambda i: (i // nS, 0, i % nS, 0)),
        ],
        out_shape=[
            jax.ShapeDtypeStruct((T, o_lr - GLA_VAL_WIDTH), F32),
            jax.ShapeDtypeStruct((T, GLA_VAL_WIDTH), BF16),
            jax.ShapeDtypeStruct((T, 2 * GLA_KEY_WIDTH), F32),
            jax.ShapeDtypeStruct((T // S, ATT_CLASSES, S // ATT_CLASSES, 3 * ATT_WIDTH), F32),
        ],
        scratch_shapes=[pltpu.VMEM((3 * ATT_WIDTH // LANES, tm, LANES), F32),
                        pltpu.VMEM((D_MODEL, o_lr), BF16), pltpu.VMEM((D_MODEL, LANES), BF16)],
        compiler_params=_cparams(("arbitrary",)),
        name="inproj",
    )(x2, norm1_w[None, :], w_in, w_in, wa, gw, gb, rc, rs1, rs2)


def _gla_decays(q, k, v, la, forward, G):
    C = GLA_CHUNK
    R = G * C
    r = lax.broadcasted_iota(jnp.int32, (R, R), 0)
    c = lax.broadcasted_iota(jnp.int32, (R, R), 1)
    same = (r >> GLA_CHUNK_SHIFT) == (c >> GLA_CHUNK_SHIFT)
    tri = (c <= r) if forward else (c >= r)
    t_mat = jnp.where(same, jnp.where(tri, 1.0, 0.0), 0.0).astype(BF16)
    hi = la.astype(BF16)
    lo = (la - hi.astype(F32)).astype(BF16)
    b = _dot(t_mat, hi) + _dot(t_mat, lo)
    edge = C - 1 if forward else 0
    tot = jnp.concatenate([jnp.broadcast_to(b[g * C + edge:g * C + edge + 1], (C, GLA_KEY_WIDTH))
                           for g in range(G)], axis=0)
    order = list(range(G)) if forward else list(range(G - 1, -1, -1))
    return dict(q_dec=q * jnp.exp(b), k_inv=(k * jnp.exp(-b)).astype(BF16), k_end=k * jnp.exp(tot - b),
                tot=tot, vb=v.astype(BF16), order=order, forward=forward, G=G)


def _gla_scores(prep):
    C, H = GLA_CHUNK, GLA_HEADS
    lane_k = lax.broadcasted_iota(jnp.int32, (C, GLA_KEY_WIDTH), 1)
    qd_heads, scores = {}, {}
    for g in prep["order"]:
        rows = slice(g * C, (g + 1) * C)
        qd = prep["q_dec"][rows]
        qd_heads[g] = jnp.concatenate([jnp.where((lane_k >> GLA_DK_SHIFT) == h, qd, 0.0) for h in range(H)],
                                      axis=0).astype(BF16)
        scores[g] = _dot_nt(qd_heads[g], prep["k_inv"][rows])
    return qd_heads, scores


def _gla_chunk_updates(prep):
    C, H, G = GLA_CHUNK, GLA_HEADS, prep["G"]
    k_end, tot, vb = prep["k_end"], prep["tot"], prep["vb"]
    kv, dec_t = {}, {}
    lane = lax.broadcasted_iota(jnp.int32, (GLA_KEY_WIDTH, 2 * C), 1)
    zeros = jnp.zeros((C, GLA_DV), BF16)
    for p in range(G // 2):
        pair = slice(2 * p * C, (2 * p + 2) * C)
        ke_t = k_end[pair].T.astype(BF16)
        tot_t = tot[pair].T
        swapped = pltpu.roll(tot_t, C, 1)
        for half in range(2):
            g = 2 * p + half
            rows = slice(g * C, (g + 1) * C)
            own = (lane < C) if half == 0 else (lane >= C)
            dec_t[g] = jnp.exp(jnp.where(own, tot_t, swapped))
            parts = []
            for h in range(H):
                v_h = vb[rows, h * GLA_DV:(h + 1) * GLA_DV]
                v_pad = jnp.concatenate([v_h, zeros] if half == 0 else [zeros, v_h], axis=0)
                parts.append(_dot(ke_t[h * C:(h + 1) * C], v_pad))
            kv[g] = jnp.concatenate(parts, axis=0)
    return kv, dec_t


def _gla_states(prep, kv, dec_t, s_ref):
    st = s_ref[...]
    states = {}
    for g in prep["order"]:
        states[g] = st.astype(BF16)
        st = st * dec_t[g] + kv[g]
    s_ref[...] = st
    return states


def _gla_outputs(prep, qd_heads, scores, inter, o_ref):
    C, H = GLA_CHUNK, GLA_HEADS
    row_q = lax.broadcasted_iota(jnp.int32, (H * C, C), 0) & (C - 1)
    col_k = lax.broadcasted_iota(jnp.int32, (H * C, C), 1)
    a_mask = (col_k <= row_q) if prep["forward"] else (col_k >= row_q)
    for g in prep["order"]:
        rows = slice(g * C, (g + 1) * C)
        a = jnp.where(a_mask, scores[g], 0.0).astype(BF16)
        vv = prep["vb"][rows]
        o_ref[rows, :] = jnp.concatenate(
            [_dot(a[h * C:(h + 1) * C], vv[:, h * GLA_DV:(h + 1) * GLA_DV]) + inter[g][h * C:(h + 1) * C]
             for h in range(H)], axis=1).astype(o_ref.dtype)


def _gla_kernel(qf_ref, kf_ref, vf_ref, laf_ref, qb_ref, kb_ref, vb_ref, lab_ref,
                of_ref, ob_ref, sf_ref, sb_ref, *, G):
    @pl.when(pl.program_id(1) == 0)
    def _():
        sf_ref[...] = jnp.zeros_like(sf_ref)
        sb_ref[...] = jnp.zeros_like(sb_ref)

    dirs = [(_gla_decays(qf_ref[...], kf_ref[...], vf_ref[...], laf_ref[...], True, G), sf_ref, of_ref),
            (_gla_decays(qb_ref[...], kb_ref[...], vb_ref[...], lab_ref[...], False, G), sb_ref, ob_ref)]
    scored = [_gla_scores(prep) for prep, _, _ in dirs]
    updates = [_gla_chunk_updates(prep) for prep, _, _ in dirs]
    states = [_gla_states(prep, kv, dec_t, s_ref) for (prep, s_ref, _), (kv, dec_t) in zip(dirs, updates)]
    inters = [{g: _dot(qd_heads[g], st[g]) for g in prep["order"]}
              for (prep, _, _), (qd_heads, _), st in zip(dirs, scored, states)]
    for (prep, _, o_ref), (qd_heads, scores), inter in zip(dirs, scored, inters):
        _gla_outputs(prep, qd_heads, scores, inter, o_ref)


def _gla(gla_slab, loga, B, S, G=8):
    T = B * S
    R = G * GLA_CHUNK
    ns = S // R
    fwd = lambda col: (lambda b, i: (b * ns + i, col))
    bwd = lambda col: (lambda b, i: (b * ns + ns - 1 - i, col))
    kw, vw = GLA_KEY_WIDTH, GLA_VAL_WIDTH
    return pl.pallas_call(
        functools.partial(_gla_kernel, G=G),
        grid=(B, ns),
        in_specs=[
            pl.BlockSpec((R, kw), fwd(0)), pl.BlockSpec((R, kw), fwd(1)),
            pl.BlockSpec((R, vw), fwd(1)), pl.BlockSpec((R, kw), fwd(0)),
            pl.BlockSpec((R, kw), bwd(0)), pl.BlockSpec((R, kw), bwd(1)),
            pl.BlockSpec((R, vw), bwd(1)), pl.BlockSpec((R, kw), bwd(1)),
        ],
        out_specs=[pl.BlockSpec((R, vw), fwd(0)), pl.BlockSpec((R, vw), bwd(0))],
        out_shape=[jax.ShapeDtypeStruct((T, vw), BF16), jax.ShapeDtypeStruct((T, vw), BF16)],
        scratch_shapes=[pltpu.VMEM((kw, GLA_DV), F32), pltpu.VMEM((kw, GLA_DV), F32)],
        compiler_params=_cparams(("arbitrary", "arbitrary")),
        name="gla",
    )(gla_slab, gla_slab, gla_slab, loga, gla_slab, gla_slab, gla_slab, loga)


ATT_CLASSES = 4
ATT_QB = 128
ATT_KB = ATT_QB + 2 * ATT_RADIUS


ATT_UNROLL = (32, 16, 16)


def _att_kernel(q_ref, k_ref, v_ref, o_ref, acc_ref, m_ref, l_ref, bias_ref, *, S):
    QB, KB, NC = ATT_QB, ATT_KB, ATT_CLASSES
    L4 = S // NC
    lane = lax.broadcasted_iota(jnp.int32, (QB, LANES), 1)
    head0 = lane < ATT_HEAD_DIM

    @pl.when((pl.program_id(0) == 0) & (pl.program_id(1) == 0))
    def _():
        rowi = lax.broadcasted_iota(jnp.int32, (2 * QB, KB), 0) & (QB - 1)
        coli = lax.broadcasted_iota(jnp.int32, (2 * QB, KB), 1)
        qpos = (rowi & (QB // NC - 1)) * NC + (rowi >> _log2(QB // NC))
        kpos = (coli & (KB // NC - 1)) * NC + (coli >> _log2(KB // NC))
        for case in range(3):
            bias_ref[0, case] = jnp.where(jnp.abs(rowi - coli + case * ATT_RADIUS) <= ATT_RADIUS, 0.0, NEG_INF)
            bias_ref[1, case] = jnp.where(jnp.abs(qpos - kpos + case * ATT_RADIUS) <= ATT_RADIUS, 0.0, NEG_INF)

    for pi, (_, d) in enumerate(DILATED_PATTERNS):
        L = S // d
        nb = L // QB
        shift = nb.bit_length() - 1
        first = pi == 0
        last = pi == len(DILATED_PATTERNS) - 1

        def scores(n, d=d, L=L, nb=nb, shift=shift):
            cls = n >> shift
            q0 = (n & (nb - 1)) * QB
            ws = jnp.clip(q0 - ATT_RADIUS, 0, L - KB)
            if d == 1:
                qsls = [pl.ds(pl.multiple_of(c * L4 + q0 // NC, QB // NC), QB // NC) for c in range(NC)]
                ksls = [pl.ds(pl.multiple_of(c * L4 + ws // NC, ATT_RADIUS // NC), KB // NC) for c in range(NC)]
            elif d == NC:
                qsls = [pl.ds(pl.multiple_of(cls * L4 + q0, QB), QB)]
                ksls = [pl.ds(pl.multiple_of(cls * L4 + ws, ATT_RADIUS), KB)]
            else:
                base = (cls & (NC - 1)) * L4 + (cls >> _log2(NC))
                qsls = [pl.ds(base + NC * q0, QB, stride=NC)]
                ksls = [pl.ds(base + NC * ws, KB, stride=NC)]
            q = jnp.concatenate([q_ref[sl, :] for sl in qsls], axis=0)
            kw = jnp.concatenate([k_ref[sl, :] for sl in ksls], axis=0)
            kb = kw.astype(BF16)
            bias = bias_ref[1 if d == 1 else 0, (q0 - ws) >> _log2(ATT_RADIUS), :QB]
            q_heads = (jnp.where(head0, q, 0.0), jnp.where(head0, 0.0, q))
            s = [_dot_nt(qh.astype(BF16), kb) + bias for qh in q_heads]
            return qsls, ksls, s

        def softmax_pv(qsls, ksls, s):
            vw = jnp.concatenate([v_ref[sl, :] for sl in ksls], axis=0)
            v_ones = jnp.concatenate([vw.astype(BF16), jnp.ones((KB, LANES), BF16)], axis=1)
            m_h = [jnp.max(t, axis=-1, keepdims=True) for t in s]
            pv = [_dot(jnp.exp2(t - m).astype(BF16), v_ones) for t, m in zip(s, m_h)]
            acc_b = jnp.where(head0, pv[0][:, :LANES], pv[1][:, :LANES])
            m_b = jnp.where(head0, m_h[0], m_h[1])
            l_b = jnp.where(head0, pv[0][:, LANES:], pv[1][:, LANES:])
            return qsls, acc_b, m_b, l_b

        def load(ref, sls):
            return jnp.concatenate([ref[sl, :] for sl in sls], axis=0)

        def store(ref, sls, val):
            n = val.shape[0] // len(sls)
            for i, sl in enumerate(sls):
                ref[sl, :] = val[i * n:(i + 1) * n]

        unroll = ATT_UNROLL[pi]

        def body(n, carry, first=first, last=last, unroll=unroll):
            staged = [scores(n * unroll + u) for u in range(unroll)]
            blocks = [softmax_pv(*st) for st in staged]
            for qsls, acc_b, m_b, l_b in blocks:
                if first:
                    acc, m_new, l_new = acc_b, m_b, l_b
                else:
                    m_old = load(m_ref, qsls)
                    m_new = jnp.maximum(m_old, m_b)
                    w_old = jnp.exp2(m_old - m_new)
                    w_blk = jnp.exp2(m_b - m_new)
                    acc = load(acc_ref, qsls) * w_old + acc_b * w_blk
                    l_new = load(l_ref, qsls) * w_old + l_b * w_blk
                if last:
                    store(acc_ref, qsls, acc / l_new)
                else:
                    store(acc_ref, qsls, acc)
                    store(m_ref, qsls, m_new)
                    store(l_ref, qsls, l_new)
            return carry

        lax.fori_loop(0, S // (QB * unroll), body, 0)

    def emit(n, carry):
        rows = pl.ds(pl.multiple_of(n * 512, 512), 512)
        o_ref[rows, :] = acc_ref[rows, :].astype(o_ref.dtype)
        return carry

    lax.fori_loop(0, S // 512, emit, 0)


def _attention(att_slab, B, S):
    T = B * S
    ncol = ATT_WIDTH // LANES
    return pl.pallas_call(
        functools.partial(_att_kernel, S=S),
        grid=(B, ncol),
        in_specs=[
            pl.BlockSpec((S, LANES), lambda b, h: (b, h)),
            pl.BlockSpec((S, LANES), lambda b, h: (b, ncol + h)),
            pl.BlockSpec((S, LANES), lambda b, h: (b, 2 * ncol + h)),
        ],
        out_specs=pl.BlockSpec((S, LANES), lambda b, h: (b, h)),
        out_shape=jax.ShapeDtypeStruct((T, ATT_WIDTH), BF16),
        scratch_shapes=[pltpu.VMEM((S, LANES), F32), pltpu.VMEM((S, LANES), F32), pltpu.VMEM((S, LANES), F32),
                        pltpu.VMEM((2, 3, 2 * ATT_QB, ATT_KB), F32)],
        compiler_params=_cparams(("arbitrary", "arbitrary")),
        name="dilated_attention",
    )(att_slab, att_slab, att_slab)


PACK_WORDS = D_MODEL // 2
ROW_TILE = PACK_WORDS // LANES
HIGH_HALF = -65536


def _pack_rows(x):
    bits = lambda v: lax.bitcast_convert_type(v.astype(BF16).astype(F32), jnp.int32)
    low = (bits(x[:, :PACK_WORDS]) >> 16) & 0xFFFF
    return (bits(x[:, PACK_WORDS:]) & HIGH_HALF) | low


def _unpack_rows(w):
    low = lax.bitcast_convert_type(w << 16, F32)
    high = lax.bitcast_convert_type(w & HIGH_HALF, F32)
    return jnp.concatenate([low, high], axis=1).astype(BF16)


def _to_row_tiles(ref, w):
    n = w.shape[0]
    for j in range(ROW_TILE):
        ref[pl.ds(j, n, stride=ROW_TILE), :] = w[:, j * LANES:(j + 1) * LANES]


def _from_row_tiles(ref, n):
    return jnp.concatenate([ref[pl.ds(j, n, stride=ROW_TILE), :] for j in range(ROW_TILE)], axis=1)


def _tile_copy(src_ref, src_row, dst_ref, dst_row, sem):
    src = pl.ds(pl.multiple_of(src_row * ROW_TILE, ROW_TILE), ROW_TILE)
    dst = pl.ds(pl.multiple_of(dst_row * ROW_TILE, ROW_TILE), ROW_TILE)
    return pltpu.make_async_copy(src_ref.at[src], dst_ref.at[dst], sem)


def _outproj_kernel(of_ref, ob_ref, gg_ref, att_ref, x_ref, gnw_ref, wo1_ref, wo2_ref,
                    n2_ref, wr_ref, br_ref, h_ref, u_ref, lg_ref, stage_ref):
    rows = stage_ref.shape[1] // ATT_CLASSES
    for j in range(ATT_WIDTH // LANES):
        for c in range(ATT_CLASSES):
            stage_ref[j, pl.ds(c, rows, stride=ATT_CLASSES), :] = att_ref[c, :, j * LANES:(j + 1) * LANES].astype(F32)
    att = jnp.concatenate([stage_ref[j] for j in range(ATT_WIDTH // LANES)], axis=1)
    o = of_ref[...].astype(F32) + ob_ref[...].astype(F32)
    gate = gg_ref[...].astype(F32)
    gnw = gnw_ref[...]
    parts = []
    for h in range(GLA_HEADS):
        sl = slice(h * GLA_DV, (h + 1) * GLA_DV)
        parts.append(_rms(o[:, sl], gnw))
    y = jnp.concatenate(parts, axis=1) * (gate / (1.0 + jnp.exp(-gate)))
    mix = _dot(y.astype(BF16), wo1_ref[...]) + _dot(att.astype(BF16), wo2_ref[...])
    h = x_ref[...] + mix
    h_ref[...] = h
    u = _rms(h, n2_ref[...])
    _to_row_tiles(u_ref, _pack_rows(u))
    u_hi = u.astype(BF16)
    u_lo = (u - u_hi.astype(F32)).astype(BF16)
    hi_both = _dot_nt(wr_ref[...], u_hi)
    lg_ref[...] = (hi_both[:LANES] + hi_both[LANES:] + _dot_nt(wr_ref[:LANES], u_lo)) + br_ref[...]


def _outproj(o_f, o_b, gate, att_out, x2, gla_norm_w, w_out, norm2_w, wr, br, tm=512):
    T = x2.shape[0]
    nS = att_out.shape[2] * ATT_CLASSES // tm
    row = lambda i: (i, 0)
    const = lambda i: (0, 0)
    wo = w_out.astype(BF16)
    wr_hi = wr.astype(BF16)
    wr_lo = (wr - wr_hi.astype(F32)).astype(BF16)
    wr = jnp.concatenate([wr_hi, wr_lo], axis=0)
    return pl.pallas_call(
        _outproj_kernel,
        grid=(T // tm,),
        in_specs=[
            pl.BlockSpec((tm, GLA_VAL_WIDTH), row),
            pl.BlockSpec((tm, GLA_VAL_WIDTH), row),
            pl.BlockSpec((tm, GLA_VAL_WIDTH), row),
            pl.BlockSpec((None, ATT_CLASSES, tm // ATT_CLASSES, ATT_WIDTH), lambda i: (i // nS, 0, i % nS, 0)),
            pl.BlockSpec((tm, D_MODEL), row),
            pl.BlockSpec((1, GLA_DV), const),
            pl.BlockSpec((GLA_VAL_WIDTH, D_MODEL), const),
            pl.BlockSpec((ATT_WIDTH, D_MODEL), lambda i: (GLA_VAL_WIDTH // ATT_WIDTH, 0)),
            pl.BlockSpec((1, D_MODEL), const),
            pl.BlockSpec((2 * LANES, D_MODEL), const),
            pl.BlockSpec((LANES, 1), const),
        ],
        out_specs=[
            pl.BlockSpec((tm, D_MODEL), row),
            pl.BlockSpec((tm * ROW_TILE, LANES), row),
            pl.BlockSpec((LANES, tm), lambda i: (0, i)),
        ],
        out_shape=[
            jax.ShapeDtypeStruct((T, D_MODEL), F32),
            jax.ShapeDtypeStruct((T * ROW_TILE, LANES), jnp.int32),
            jax.ShapeDtypeStruct((LANES, T), F32),
        ],
        scratch_shapes=[pltpu.VMEM((ATT_WIDTH // LANES, tm, LANES), F32)],
        compiler_params=_cparams(("arbitrary",)),
        name="outproj",
    )(o_f, o_b, gate, att_out, x2, gla_norm_w[None, :], wo, wo,
      norm2_w[None, :], wr, br)


INFO_E1, INFO_E2, INFO_R1, INFO_R2, INFO_W1, INFO_W2 = range(6)
ROUTE_ROWS = 40


def _route_kernel(lg_ref, info_ref, cnt_ref, carry_ref):
    @pl.when(pl.program_id(0) == 0)
    def _():
        carry_ref[...] = jnp.zeros_like(carry_ref)

    lg = lg_ref[:ROUTE_ROWS, :]
    tr = lg.shape[1]
    row = lax.broadcasted_iota(jnp.int32, (ROUTE_ROWS, tr), 0)
    big = jnp.int32(1 << 20)
    is_g = (row >= MOE_N_EXPERTS) & (row < MOE_N_EXPERTS + MOE_GROUPS)
    gl = jnp.where(is_g, lg, -jnp.inf)
    gmax = jnp.max(gl, axis=0, keepdims=True)
    gsel = jnp.min(jnp.where(gl == gmax, row - MOE_N_EXPERTS, big), axis=0, keepdims=True)
    g_w = 1.0 / jnp.sum(jnp.where(is_g, jnp.exp(lg - gmax), 0.0), axis=0, keepdims=True)
    in_grp = (row < MOE_N_EXPERTS) & ((row >> MOE_GROUP_SHIFT) == gsel)
    el = jnp.where(in_grp, lg, -jnp.inf)
    v1 = jnp.max(el, axis=0, keepdims=True)
    i1 = jnp.min(jnp.where(el == v1, row, big), axis=0, keepdims=True)
    el2 = jnp.where(row == i1, -jnp.inf, el)
    v2 = jnp.max(el2, axis=0, keepdims=True)
    i2 = jnp.min(jnp.where(el2 == v2, row, big), axis=0, keepdims=True)
    t = jnp.exp(v2 - v1)
    w1 = g_w * (1.0 / (1.0 + t))
    w2 = g_w * (t / (1.0 + t))

    erow = lax.broadcasted_iota(jnp.int32, (MOE_N_EXPERTS, tr), 0)
    hit1 = erow == i1
    hit2 = erow == i2
    member = jnp.where(hit1 | hit2, 1.0, 0.0)
    r = lax.broadcasted_iota(jnp.int32, (tr, tr), 0)
    c = lax.broadcasted_iota(jnp.int32, (tr, tr), 1)
    earlier = jnp.where(r < c, 1.0, 0.0).astype(BF16)
    carry = carry_ref[...]
    prefix = _dot(member.astype(BF16), earlier) + carry[:, 0:1]
    rank1 = jnp.sum(jnp.where(hit1, prefix, 0.0), axis=0, keepdims=True)
    rank2 = jnp.sum(jnp.where(hit2, prefix, 0.0), axis=0, keepdims=True)
    carry = carry + jnp.sum(member, axis=1, keepdims=True)
    carry_ref[...] = carry
    cnt_ref[...] = carry

    zero = jnp.zeros_like(w1)
    info_ref[...] = jnp.concatenate([i1.astype(F32), i2.astype(F32), rank1, rank2, w1, w2, zero, zero], axis=0)


def _route(logits_t, tr=1024):
    T = logits_t.shape[1]
    return pl.pallas_call(
        _route_kernel,
        grid=(T // tr,),
        in_specs=[pl.BlockSpec((LANES, tr), lambda i: (0, i))],
        out_specs=[pl.BlockSpec((8, tr), lambda i: (0, i)),
                   pl.BlockSpec((MOE_N_EXPERTS, LANES), lambda i: (0, 0))],
        out_shape=[jax.ShapeDtypeStruct((8, T), F32), jax.ShapeDtypeStruct((MOE_N_EXPERTS, LANES), F32)],
        scratch_shapes=[pltpu.VMEM((MOE_N_EXPERTS, LANES), F32)],
        compiler_params=_cparams(("arbitrary",)),
        name="route",
    )(logits_t)


ROW_UNROLL = 16


def _dispatch_kernel(dest_ref, pend_ref, u_ref, xs_ref, zbuf, sem, zsem, *, td, T, nblk):
    @pl.when(pl.program_id(0) == 0)
    def _():
        zbuf[...] = jnp.zeros_like(zbuf)
        n_used = pend_ref[MOE_N_EXPERTS - 1] >> MOE_ROWS_SHIFT

        def zero_copy(blk):
            start = pl.multiple_of(blk * (MOE_ROWS * ROW_TILE), MOE_ROWS * ROW_TILE)
            return pltpu.make_async_copy(zbuf, xs_ref.at[pl.ds(start, MOE_ROWS * ROW_TILE)], zsem)

        def each_pad_block(fn):
            def per_expert(e, carry):
                prev = jnp.where(e > 0, pend_ref[jnp.maximum(e - 1, 0)], 0)

                @pl.when(pend_ref[e] > prev)
                def _():
                    fn((pend_ref[e] >> MOE_ROWS_SHIFT) - 1)
                return carry

            def per_tail(j, carry):
                @pl.when(n_used + j < nblk)
                def _():
                    fn(n_used + j)
                return carry

            lax.fori_loop(0, MOE_N_EXPERTS, per_expert, 0)
            lax.fori_loop(0, MOE_N_EXPERTS, per_tail, 0)

        each_pad_block(lambda blk: zero_copy(blk).start())
        each_pad_block(lambda blk: zero_copy(blk).wait())

    base = pl.program_id(0) * td

    def issue(g, carry):
        for j in range(ROW_UNROLL):
            r = g * ROW_UNROLL + j
            for k in range(MOE_TOP_K):
                _tile_copy(u_ref, r, xs_ref, dest_ref[k * T + base + r], sem).start(priority=k)
        return carry

    lax.fori_loop(0, td // ROW_UNROLL, issue, 0)
    for k in range(MOE_TOP_K):
        pltpu.make_async_copy(u_ref, xs_ref.at[pl.ds(0, td * ROW_TILE)], sem).wait()


def _dispatch(dest, pend, u2, cap, td=2048):
    T = u2.shape[0] // ROW_TILE
    return pl.pallas_call(
        functools.partial(_dispatch_kernel, td=td, T=T, nblk=cap // MOE_ROWS),
        grid_spec=pltpu.PrefetchScalarGridSpec(
            num_scalar_prefetch=2,
            grid=(T // td,),
            in_specs=[pl.BlockSpec((td * ROW_TILE, LANES), lambda i, d, z: (i, 0))],
            out_specs=pl.BlockSpec(memory_space=pl.ANY),
            scratch_shapes=[pltpu.VMEM((MOE_ROWS * ROW_TILE, LANES), jnp.int32),
                            pltpu.SemaphoreType.DMA(()), pltpu.SemaphoreType.DMA(())],
        ),
        out_shape=jax.ShapeDtypeStruct((cap * ROW_TILE, LANES), jnp.int32),
        compiler_params=_cparams(("arbitrary",)),
        name="dispatch",
    )(dest, pend, u2)


EXPERT_GROUP = 4


def _expert_kernel(pend_ref, xs_hbm, wg_hbm, wu_hbm, wd_hbm, ys_hbm,
                   xbuf, ybuf, zbuf, stage_g, stage_u, stage_d, wgb, wub, wdb, xsem, ysem, wsem, zsem, *, nblk):
    last = MOE_N_EXPERTS - 1
    n_used = pend_ref[last] >> MOE_ROWS_SHIFT
    n_pairs = (n_used + EXPERT_GROUP - 1) >> _log2(EXPERT_GROUP)
    block_rows = MOE_ROWS * ROW_TILE

    def rows_of(b):
        return pl.ds(pl.multiple_of(b * block_rows, block_rows), block_rows)

    def x_copy(b, slot):
        return pltpu.make_async_copy(xs_hbm.at[rows_of(b)], xbuf.at[slot], xsem.at[slot])

    def y_copy(b, slot):
        return pltpu.make_async_copy(ybuf.at[slot], ys_hbm.at[rows_of(b)], ysem.at[slot])

    def zero_copy(b):
        return pltpu.make_async_copy(zbuf, ys_hbm.at[rows_of(b)], zsem)

    def weight_copies(e):
        return (pltpu.make_async_copy(wg_hbm.at[e], stage_g, wsem.at[0]),
                pltpu.make_async_copy(wu_hbm.at[e], stage_u, wsem.at[1]),
                pltpu.make_async_copy(wd_hbm.at[e], stage_d, wsem.at[2]))

    def owner(start, row):
        return lax.while_loop(lambda e: (e < last) & (pend_ref[e] <= row), lambda e: e + 1, start)

    for c in weight_copies(owner(0, 0)):
        c.start()
    for i in range(EXPERT_GROUP):
        x_copy(i, i).start()

    zbuf[...] = jnp.zeros_like(zbuf)

    def tail(fn):
        def step(b, carry):
            fn(b)
            return carry
        lax.fori_loop(n_pairs * EXPERT_GROUP, nblk, step, 0)

    tail(lambda b: zero_copy(b).start())

    def body(p, carry):
        cur, run = carry
        half = (p & 1) * EXPERT_GROUP
        for i in range(EXPERT_GROUP):
            x_copy(p * EXPERT_GROUP + i, half + i).wait()

        @pl.when(p + 1 < n_pairs)
        def _():
            for i in range(EXPERT_GROUP):
                x_copy((p + 1) * EXPERT_GROUP + i, EXPERT_GROUP - half + i).start()

        slots = []
        for i in range(EXPERT_GROUP):
            b = p * EXPERT_GROUP + i
            e = jnp.where(b < n_used, owner(jnp.maximum(cur, 0), b * MOE_ROWS), cur)
            fresh = e != cur
            run = run + fresh.astype(jnp.int32)
            slot = run & (EXPERT_GROUP - 1)

            @pl.when(fresh)
            def _(e=e, slot=slot):
                for c in weight_copies(e):
                    c.wait()
                wgb[slot] = stage_g[...].astype(BF16)
                wub[slot] = stage_u[...].astype(BF16)
                wdb[slot] = stage_d[...].astype(BF16)

                @pl.when(pend_ref[e] < pend_ref[last])
                def _():
                    for c in weight_copies(owner(e + 1, pend_ref[e])):
                        c.start(priority=1)

            cur = e
            slots.append(slot)

        @pl.when(p >= 2)
        def _():
            for i in range(EXPERT_GROUP):
                y_copy((p - 2) * EXPERT_GROUP + i, half + i).wait()

        for i in range(EXPERT_GROUP):
            xb = _unpack_rows(_from_row_tiles(xbuf.at[half + i], MOE_ROWS))
            g = _dot(xb, wgb[slots[i]])
            u = _dot(xb, wub[slots[i]])
            hid = (g / (1.0 + jnp.exp(-g))) * u
            _to_row_tiles(ybuf.at[half + i], _pack_rows(_dot(hid.astype(BF16), wdb[slots[i]])))
        for i in range(EXPERT_GROUP):
            y_copy(p * EXPERT_GROUP + i, half + i).start()
        return cur, run

    lax.fori_loop(0, n_pairs, body, (jnp.int32(-1), jnp.int32(-1)))

    def drain(p):
        for i in range(EXPERT_GROUP):
            y_copy(p * EXPERT_GROUP + i, (p & 1) * EXPERT_GROUP + i).wait()

    @pl.when(n_pairs >= 2)
    def _():
        drain(n_pairs - 2)
    drain(n_pairs - 1)
    tail(lambda b: zero_copy(b).wait())


def _experts(pend, xs, w_gate, w_up, w_down):
    cap = xs.shape[0] // ROW_TILE
    nblk = cap // MOE_ROWS
    assert nblk % EXPERT_GROUP == 0
    block = (MOE_ROWS * ROW_TILE, LANES)
    anywhere = pl.BlockSpec(memory_space=pl.ANY)
    return pl.pallas_call(
        functools.partial(_expert_kernel, nblk=nblk),
        grid_spec=pltpu.PrefetchScalarGridSpec(
            num_scalar_prefetch=1,
            grid=(1,),
            in_specs=[anywhere, anywhere, anywhere, anywhere],
            out_specs=anywhere,
            scratch_shapes=[pltpu.VMEM((2 * EXPERT_GROUP,) + block, jnp.int32),
                            pltpu.VMEM((2 * EXPERT_GROUP,) + block, jnp.int32),
                            pltpu.VMEM(block, jnp.int32),
                            pltpu.VMEM((D_MODEL, MOE_D_FF), F32),
                            pltpu.VMEM((D_MODEL, MOE_D_FF), F32),
                            pltpu.VMEM((MOE_D_FF, D_MODEL), F32),
                            pltpu.VMEM((EXPERT_GROUP, D_MODEL, MOE_D_FF), BF16),
                            pltpu.VMEM((EXPERT_GROUP, D_MODEL, MOE_D_FF), BF16),
                            pltpu.VMEM((EXPERT_GROUP, MOE_D_FF, D_MODEL), BF16),
                            pltpu.SemaphoreType.DMA((2 * EXPERT_GROUP,)),
                            pltpu.SemaphoreType.DMA((2 * EXPERT_GROUP,)),
                            pltpu.SemaphoreType.DMA((3,)),
                            pltpu.SemaphoreType.DMA(())],
        ),
        out_shape=jax.ShapeDtypeStruct((cap * ROW_TILE, LANES), jnp.int32),
        compiler_params=_cparams(("arbitrary",)),
        name="experts",
    )(pend, xs, w_gate, w_up, w_down)


def _combine_kernel(dest_ref, ys_ref, info_ref, h_ref, fw_ref, o_ref, buf, sem, *, tc, T):
    i = pl.program_id(0)
    n = pl.num_programs(0)

    def issue(step, slot):
        base = step * tc

        def body(g, carry):
            for j in range(ROW_UNROLL):
                r = g * ROW_UNROLL + j
                for k in range(MOE_TOP_K):
                    _tile_copy(ys_ref, dest_ref[k * T + base + r], buf.at[slot, k], r,
                               sem.at[slot]).start(priority=k)
            return carry

        lax.fori_loop(0, tc // ROW_UNROLL, body, 0)

    @pl.when(i == 0)
    def _():
        issue(0, 0)

    slot = i % 2

    @pl.when(i + 1 < n)
    def _():
        issue(i + 1, 1 - slot)

    for k in range(MOE_TOP_K):
        pltpu.make_async_copy(ys_ref.at[pl.ds(0, tc * ROW_TILE)], buf.at[slot, k], sem.at[slot]).wait()

    info_t = jnp.concatenate([info_ref[...]] * (LANES // 8), axis=0).T
    w1 = info_t[:, INFO_W1:INFO_W1 + 1]
    w2 = info_t[:, INFO_W2:INFO_W2 + 1]
    y1 = _unpack_rows(_from_row_tiles(buf.at[slot, 0], tc)).astype(F32)
    y2 = _unpack_rows(_from_row_tiles(buf.at[slot, 1], tc)).astype(F32)
    h = h_ref[...] + (y1 * w1 + y2 * w2)
    o_ref[...] = _rms(h, fw_ref[...])


def _combine(dest, ys, info, h, final_w, tc=512):
    T = h.shape[0]
    return pl.pallas_call(
        functools.partial(_combine_kernel, tc=tc, T=T),
        grid_spec=pltpu.PrefetchScalarGridSpec(
            num_scalar_prefetch=1,
            grid=(T // tc,),
            in_specs=[pl.BlockSpec(memory_space=pl.ANY),
                      pl.BlockSpec((8, tc), lambda i, d: (0, i)),
                      pl.BlockSpec((tc, D_MODEL), lambda i, d: (i, 0)),
                      pl.BlockSpec((1, D_MODEL), lambda i, d: (0, 0))],
            out_specs=pl.BlockSpec((tc, D_MODEL), lambda i, d: (i, 0)),
            scratch_shapes=[pltpu.VMEM((2, MOE_TOP_K, tc * ROW_TILE, LANES), jnp.int32),
                            pltpu.SemaphoreType.DMA((2,))],
        ),
        out_shape=jax.ShapeDtypeStruct((T, D_MODEL), F32),
        compiler_params=_cparams(("arbitrary",)),
        name="combine",
    )(dest, ys, info, h, final_w[None, :])


def _plan_kernel(info_ref, cnt_ref, dest_ref, pend_ref):
    cnt = cnt_ref[...].astype(jnp.int32)
    nblk_e = ((cnt + (MOE_ROWS - 1)) >> MOE_ROWS_SHIFT).astype(F32)
    r = lax.broadcasted_iota(jnp.int32, (MOE_N_EXPERTS, MOE_N_EXPERTS), 0)
    c = lax.broadcasted_iota(jnp.int32, (MOE_N_EXPERTS, MOE_N_EXPERTS), 1)
    before = jnp.where(c < r, 1.0, 0.0).astype(BF16)
    first_blk = _dot(before, nblk_e.astype(BF16))
    pstart = first_blk[:, 0:1] * float(MOE_ROWS)
    pend_ref[...] = ((first_blk + nblk_e) * float(MOE_ROWS)).astype(jnp.int32)

    info = info_ref[...]
    erow = lax.broadcasted_iota(jnp.int32, (MOE_N_EXPERTS, info.shape[1]), 0)
    start_of = lambda e: jnp.sum(jnp.where(erow == e.astype(jnp.int32), pstart, 0.0), axis=0, keepdims=True)
    d1 = info[INFO_R1:INFO_R1 + 1] + start_of(info[INFO_E1:INFO_E1 + 1])
    d2 = info[INFO_R2:INFO_R2 + 1] + start_of(info[INFO_E2:INFO_E2 + 1])
    zero = jnp.zeros_like(d1)
    dest_ref[...] = jnp.concatenate([d1, d2] + [zero] * 6, axis=0).astype(jnp.int32)


def _plan(info, counts, tr=2048):
    T = info.shape[1]
    dest8, pend = pl.pallas_call(
        _plan_kernel,
        grid=(T // tr,),
        in_specs=[pl.BlockSpec((8, tr), lambda i: (0, i)),
                  pl.BlockSpec((MOE_N_EXPERTS, LANES), lambda i: (0, 0))],
        out_specs=[pl.BlockSpec((8, tr), lambda i: (0, i)),
                   pl.BlockSpec((MOE_N_EXPERTS, LANES), lambda i: (0, 0))],
        out_shape=[jax.ShapeDtypeStruct((8, T), jnp.int32),
                   jax.ShapeDtypeStruct((MOE_N_EXPERTS, LANES), jnp.int32)],
        compiler_params=_cparams(("arbitrary",)),
        name="plan",
    )(info, counts)
    return dest8[:MOE_TOP_K].reshape(-1), pend[:, 0]


def _moe_capacity(T):
    return (-(-(T * MOE_TOP_K) // MOE_ROWS) + MOE_N_EXPERTS) * MOE_ROWS


def _router_weights(router_group_w, router_group_b, router_expert_w, router_expert_b):
    we = jnp.transpose(router_expert_w, (0, 2, 1)).reshape(MOE_N_EXPERTS, D_MODEL)
    pad = LANES - MOE_N_EXPERTS - MOE_GROUPS
    wr = jnp.concatenate([we, router_group_w.T, jnp.zeros((pad, D_MODEL), F32)], axis=0)
    br = jnp.concatenate([router_expert_b.reshape(-1), router_group_b, jnp.zeros((pad,), F32)])[:, None]
    return wr, br


def kernel(x, norm1_w, w_in, gla_fwd_gate_w, gla_fwd_gate_b, gla_bwd_gate_w, gla_bwd_gate_b,
           gla_norm_w, w_out, norm2_w, router_group_w, router_group_b, router_expert_w,
           router_expert_b, expert_w_gate, expert_w_up, expert_w_down, final_norm_w):
    B, S, D = x.shape
    T = B * S
    assert norm1_w.shape[0] == 1, "single-layer trunk: the final norm is fused into the combine step"
    h = x.reshape(T, D)
    gla_slab, gate, loga, att_slab = _inproj(h, S, norm1_w[0], w_in[0], gla_fwd_gate_w[0], gla_fwd_gate_b[0],
                                       gla_bwd_gate_w[0], gla_bwd_gate_b[0])
    o_f, o_b = _gla(gla_slab, loga, B, S)
    att_out = _attention(att_slab.reshape(T, 3 * ATT_WIDTH), B, S)
    att_out = att_out.reshape(B, ATT_CLASSES, S // ATT_CLASSES, ATT_WIDTH)
    wr, br = _router_weights(router_group_w[0], router_group_b[0], router_expert_w[0], router_expert_b[0])
    h, u2, logits = _outproj(o_f, o_b, gate, att_out, h, gla_norm_w[0], w_out[0], norm2_w[0], wr, br)
    info, counts = _route(logits)
    dest, pend = _plan(info, counts)
    xs = _dispatch(dest, pend, u2, _moe_capacity(T))
    ys = _experts(pend, xs, expert_w_gate[0], expert_w_up[0], expert_w_down[0])
    out = _combine(dest, ys, info, h, final_norm_w)
    return out.reshape(B, S, D)
```
